```python
import math
import jax
import jax.numpy as jnp
from jax import lax
import numpy as np

D_MODEL = 1024
BATCH = 16
SEQ = 2048
DEPTH = 1

CHUNK = 64
N_META = 16
QBLK = 128

MLA_HEADS = 4
D_NOPE = 128
D_ROPE = 64
D_QK = D_NOPE + D_ROPE
D_V = 128
KV_RANK = 256
Q_RANK = 384
ROPE_THETA = 10000.0
MLA_WIDTH = MLA_HEADS * D_V

LRU_WIDTH = D_MODEL // 2
LRU_BLOCKS = 8
LRU_BLOCK = LRU_WIDTH // LRU_BLOCKS
CONV_W = 4
C_RGLRU = 8.0

MIX_WIDTH = MLA_WIDTH + LRU_WIDTH
IN_WIDTH = Q_RANK + KV_RANK + D_ROPE + LRU_WIDTH + LRU_WIDTH

D_FF = 2816
FFN_RESIDUAL = 0.5
EPS = 1e-6
NEG_INF = -1e30

kernel_name = "hymba_mla_rglru_macaron_block"


def _rmsnorm(x, g):
    xf = x.astype(jnp.float32)
    y = xf * lax.rsqrt(jnp.mean(xf * xf, axis=-1, keepdims=True) + EPS)
    return (y * g.astype(jnp.float32)).astype(x.dtype)


def _swiglu_half(h, g, w_gate, w_up, w_down):
    u = _rmsnorm(h, g)
    return FFN_RESIDUAL * ((jax.nn.silu(u @ w_gate) * (u @ w_up)) @ w_down)


def _rope(x, cos, sin):
    half = x.shape[-1] // 2
    x1, x2 = x[..., :half], x[..., half:]
    return jnp.concatenate([x1 * cos - x2 * sin, x2 * cos + x1 * sin], axis=-1)


def _mla(c_q, c_kv, k_r, q_latent_norm, w_uq, kv_latent_norm, w_uk, w_uv,
         q_head_norm, k_head_norm):
    B, L, _ = c_q.shape
    q = (_rmsnorm(c_q, q_latent_norm) @ w_uq).reshape(B, L, MLA_HEADS, D_QK)
    ckv = _rmsnorm(c_kv, kv_latent_norm)
    k_nope = (ckv @ w_uk).reshape(B, L, MLA_HEADS, D_NOPE)
    v = (ckv @ w_uv).reshape(B, L, MLA_HEADS, D_V)
    k_rope = jnp.broadcast_to(k_r[:, :, None, :], (B, L, MLA_HEADS, D_ROPE))
    k = jnp.concatenate([k_nope, k_rope], axis=-1)
    q = _rmsnorm(q, q_head_norm)
    k = _rmsnorm(k, k_head_norm)
    pos = jnp.arange(L, dtype=jnp.float32)
    inv_freq = ROPE_THETA ** (-jnp.arange(0, D_ROPE // 2, dtype=jnp.float32) / (D_ROPE // 2))
    ang = pos[:, None] * inv_freq[None, :]
    cos = jnp.cos(ang)[:, None, :].astype(q.dtype)
    sin = jnp.sin(ang)[:, None, :].astype(q.dtype)
    q = jnp.concatenate([q[..., :D_NOPE], _rope(q[..., D_NOPE:], cos, sin)], axis=-1)
    k = jnp.concatenate([k[..., :D_NOPE], _rope(k[..., D_NOPE:], cos, sin)], axis=-1)
    n_blk = -(-L // QBLK)
    L_pad = n_blk * QBLK
    padw = ((0, 0), (0, L_pad - L), (0, 0), (0, 0))
    q = jnp.pad(q, padw).transpose(0, 2, 1, 3)
    k = jnp.pad(k, padw).transpose(0, 2, 1, 3)
    v = jnp.pad(v, padw).transpose(0, 2, 1, 3)
    cid = (jnp.arange(L_pad, dtype=jnp.int32) + (CHUNK - N_META)) // CHUNK
    q_blocks = q.reshape(B, MLA_HEADS, n_blk, QBLK, D_QK).transpose(2, 0, 1, 3, 4)
    cid_blocks = cid.reshape(n_blk, QBLK)
    scale = 1.0 / math.sqrt(D_QK)

    def attend(args):
        qb, cq = args
        s = jnp.einsum('bhqd,bhkd->bhqk', qb, k,
                       preferred_element_type=jnp.float32) * scale
        mask = cid[None, :] <= cq[:, None]
        s = jnp.where(mask[None, None], s, NEG_INF)
        p = jax.nn.softmax(s, axis=-1).astype(v.dtype)
        return jnp.einsum('bhqk,bhkd->bhqd', p, v)

    o = lax.map(attend, (q_blocks, cid_blocks))
    o = o.transpose(1, 0, 3, 2, 4).reshape(B, L_pad, MLA_WIDTH)
    return o[:, :L]


def _rglru(u, gate, conv_w, conv_b, gate_a_w, gate_a_b, gate_x_w, gate_x_b, lru_lambda):
    B, L, W = u.shape
    xc = lax.conv_general_dilated(
        u, conv_w[:, None, :].astype(u.dtype), window_strides=(1,),
        padding=[(CONV_W - 1, 0)], dimension_numbers=('NWC', 'WIO', 'NWC'),
        feature_group_count=W) + conv_b
    xb = xc.reshape(B, L, LRU_BLOCKS, LRU_BLOCK)
    r = jax.nn.sigmoid(jnp.einsum('blni,nij->blnj', xb, gate_a_w).reshape(B, L, W) + gate_a_b)
    i = jax.nn.sigmoid(jnp.einsum('blni,nij->blnj', xb, gate_x_w).reshape(B, L, W) + gate_x_b)
    log_a = -C_RGLRU * r.astype(jnp.float32) * jax.nn.softplus(-lru_lambda.astype(jnp.float32))
    a = jnp.exp(log_a)
    mult = jnp.sqrt(-jnp.expm1(2.0 * log_a))
    first = (jnp.arange(L) == 0)[None, :, None]
    mult = jnp.where(first, 1.0, mult)
    b = mult * (i * xc).astype(jnp.float32)

    def combine(lhs, rhs):
        a1, b1 = lhs
        a2, b2 = rhs
        return a1 * a2, a2 * b1 + b2

    _, h = lax.associative_scan(combine, (a, b), axis=1)
    return h.astype(u.dtype) * jax.nn.gelu(gate)


def _fwd_setup_inputs(seed: int = 0) -> dict:
    key = jax.random.key(seed)
    ks = iter(jax.random.split(key, 40))

    def dense(shape, fan_in):
        return jax.random.normal(next(ks), shape, jnp.float32) * (fan_in ** -0.5)

    def gain(n):
        return 1.0 + 0.05 * jax.random.normal(next(ks), (DEPTH, n), jnp.float32)

    def bias(n):
        return 0.01 * jax.random.normal(next(ks), (DEPTH, n), jnp.float32)

    x = jax.random.normal(next(ks), (BATCH, SEQ, D_MODEL), jnp.float32)
    meta_tokens = jax.random.normal(next(ks), (N_META, D_MODEL), jnp.float32)
    a0 = 0.9 + 0.099 * jax.random.uniform(next(ks), (DEPTH, LRU_WIDTH), jnp.float32)
    s0 = a0 ** (1.0 / C_RGLRU)
    lru_lambda = jnp.log(s0) - jnp.log1p(-s0)
    return {
        "x": x,
        "meta_tokens": meta_tokens,
        "ffn1_norm": gain(D_MODEL),
        "ffn1_w_gate": dense((DEPTH, D_MODEL, D_FF), D_MODEL),
        "ffn1_w_up": dense((DEPTH, D_MODEL, D_FF), D_MODEL),
        "ffn1_w_down": dense((DEPTH, D_FF, D_MODEL), D_FF),
        "mix_norm": gain(D_MODEL),
        "w_in": dense((DEPTH, D_MODEL, IN_WIDTH), D_MODEL),
        "q_latent_norm": gain(Q_RANK),
        "w_uq": dense((DEPTH, Q_RANK, MLA_HEADS * D_QK), Q_RANK),
        "kv_latent_norm": gain(KV_RANK),
        "w_uk": dense((DEPTH, KV_RANK, MLA_HEADS * D_NOPE), KV_RANK),
        "w_uv": dense((DEPTH, KV_RANK, MLA_HEADS * D_V), KV_RANK),
        "q_head_norm": gain(D_QK),
        "k_head_norm": gain(D_QK),
        "conv_w": dense((DEPTH, CONV_W, LRU_WIDTH), CONV_W),
        "conv_b": bias(LRU_WIDTH),
        "gate_a_w": dense((DEPTH, LRU_BLOCKS, LRU_BLOCK, LRU_BLOCK), LRU_BLOCK),
        "gate_a_b": bias(LRU_WIDTH),
        "gate_x_w": dense((DEPTH, LRU_BLOCKS, LRU_BLOCK, LRU_BLOCK), LRU_BLOCK),
        "gate_x_b": bias(LRU_WIDTH),
        "lru_lambda": lru_lambda,
        "attn_out_norm": gain(MLA_WIDTH),
        "lru_out_norm": gain(LRU_WIDTH),
        "w_out": dense((DEPTH, MIX_WIDTH, D_MODEL), MIX_WIDTH),
        "ffn2_norm": gain(D_MODEL),
        "ffn2_w_gate": dense((DEPTH, D_MODEL, D_FF), D_MODEL),
        "ffn2_w_up": dense((DEPTH, D_MODEL, D_FF), D_MODEL),
        "ffn2_w_down": dense((DEPTH, D_FF, D_MODEL), D_FF),
        "final_norm": gain(D_MODEL),
    }


def _fwd_reference(x, meta_tokens, ffn1_norm, ffn1_w_gate, ffn1_w_up, ffn1_w_down,
              mix_norm, w_in, q_latent_norm, w_uq, kv_latent_norm, w_uk, w_uv,
              q_head_norm, k_head_norm, conv_w, conv_b, gate_a_w, gate_a_b,
              gate_x_w, gate_x_b, lru_lambda, attn_out_norm, lru_out_norm, w_out,
              ffn2_norm, ffn2_w_gate, ffn2_w_up, ffn2_w_down, final_norm):
    B = x.shape[0]
    meta = jnp.broadcast_to(meta_tokens.astype(x.dtype)[None], (B, N_META, D_MODEL))
    h = jnp.concatenate([meta, x], axis=1)
    o1 = Q_RANK
    o2 = o1 + KV_RANK
    o3 = o2 + D_ROPE
    o4 = o3 + LRU_WIDTH
    for l in range(DEPTH):
        h = h + _swiglu_half(h, ffn1_norm[l], ffn1_w_gate[l], ffn1_w_up[l], ffn1_w_down[l])
        z = _rmsnorm(h, mix_norm[l]) @ w_in[l]
        c_q, c_kv, k_r = z[..., :o1], z[..., o1:o2], z[..., o2:o3]
        u, g = z[..., o3:o4], z[..., o4:]
        y_mla = _mla(c_q, c_kv, k_r, q_latent_norm[l], w_uq[l], kv_latent_norm[l],
                     w_uk[l], w_uv[l], q_head_norm[l], k_head_norm[l])
        y_lru = _rglru(u, g, conv_w[l], conv_b[l], gate_a_w[l], gate_a_b[l],
                       gate_x_w[l], gate_x_b[l], lru_lambda[l])
        y = jnp.concatenate([_rmsnorm(y_mla, attn_out_norm[l]),
                             _rmsnorm(y_lru, lru_out_norm[l])], axis=-1)
        h = h + y @ w_out[l]
        h = h + _swiglu_half(h, ffn2_norm[l], ffn2_w_gate[l], ffn2_w_up[l], ffn2_w_down[l])
        h = _rmsnorm(h, final_norm[l])
    return h[:, N_META:]


import jax as _jax
import jax.numpy as _jnp

TWIN_FORMAT = 'train_step'
FWD_PARAMS = ['x', 'meta_tokens', 'ffn1_norm', 'ffn1_w_gate', 'ffn1_w_up', 'ffn1_w_down', 'mix_norm', 'w_in', 'q_latent_norm', 'w_uq', 'kv_latent_norm', 'w_uk', 'w_uv', 'q_head_norm', 'k_head_norm', 'conv_w', 'conv_b', 'gate_a_w', 'gate_a_b', 'gate_x_w', 'gate_x_b', 'lru_lambda', 'attn_out_norm', 'lru_out_norm', 'w_out', 'ffn2_norm', 'ffn2_w_gate', 'ffn2_w_up', 'ffn2_w_down', 'final_norm']
TWIN_WEIGHTS = ['meta_tokens', 'ffn1_norm', 'ffn1_w_gate', 'ffn1_w_up', 'ffn1_w_down', 'mix_norm', 'w_in', 'q_latent_norm', 'w_uq', 'kv_latent_norm', 'w_uk', 'w_uv', 'q_head_norm', 'k_head_norm', 'conv_w', 'conv_b', 'gate_a_w', 'gate_a_b', 'gate_x_w', 'gate_x_b', 'lru_lambda', 'attn_out_norm', 'lru_out_norm', 'w_out', 'ffn2_norm', 'ffn2_w_gate', 'ffn2_w_up', 'ffn2_w_down', 'final_norm']
TWIN_DIFF_INPUT = 'x'
TWIN_INPUTS = ['x', 'meta_tokens', 'ffn1_norm', 'ffn1_w_gate', 'ffn1_w_up', 'ffn1_w_down', 'mix_norm', 'w_in', 'q_latent_norm', 'w_uq', 'kv_latent_norm', 'w_uk', 'w_uv', 'q_head_norm', 'k_head_norm', 'conv_w', 'conv_b', 'gate_a_w', 'gate_a_b', 'gate_x_w', 'gate_x_b', 'lru_lambda', 'attn_out_norm', 'lru_out_norm', 'w_out', 'ffn2_norm', 'ffn2_w_gate', 'ffn2_w_up', 'ffn2_w_down', 'final_norm', 'loss_target', 'm_meta_tokens', 'm_ffn1_norm', 'm_ffn1_w_gate', 'm_ffn1_w_up', 'm_ffn1_w_down', 'm_mix_norm', 'm_w_in', 'm_q_latent_norm', 'm_w_uq', 'm_kv_latent_norm', 'm_w_uk', 'm_w_uv', 'm_q_head_norm', 'm_k_head_norm', 'm_conv_w', 'm_conv_b', 'm_gate_a_w', 'm_gate_a_b', 'm_gate_x_w', 'm_gate_x_b', 'm_lru_lambda', 'm_attn_out_norm', 'm_lru_out_norm', 'm_w_out', 'm_ffn2_norm', 'm_ffn2_w_gate', 'm_ffn2_w_up', 'm_ffn2_w_down', 'm_final_norm', 'v_meta_tokens', 'v_ffn1_norm', 'v_ffn1_w_gate', 'v_ffn1_w_up', 'v_ffn1_w_down', 'v_mix_norm', 'v_w_in', 'v_q_latent_norm', 'v_w_uq', 'v_kv_latent_norm', 'v_w_uk', 'v_w_uv', 'v_q_head_norm', 'v_k_head_norm', 'v_conv_w', 'v_conv_b', 'v_gate_a_w', 'v_gate_a_b', 'v_gate_x_w', 'v_gate_x_b', 'v_lru_lambda', 'v_attn_out_norm', 'v_lru_out_norm', 'v_w_out', 'v_ffn2_norm', 'v_ffn2_w_gate', 'v_ffn2_w_up', 'v_ffn2_w_down', 'v_final_norm']
TWIN_OUTPUTS = ['loss', 'grad_x', 'grad_meta_tokens', 'grad_ffn1_norm', 'grad_ffn1_w_gate', 'grad_ffn1_w_up', 'grad_ffn1_w_down', 'grad_mix_norm', 'grad_w_in', 'grad_q_latent_norm', 'grad_w_uq', 'grad_kv_latent_norm', 'grad_w_uk', 'grad_w_uv', 'grad_q_head_norm', 'grad_k_head_norm', 'grad_conv_w', 'grad_conv_b', 'grad_gate_a_w', 'grad_gate_a_b', 'grad_gate_x_w', 'grad_gate_x_b', 'grad_lru_lambda', 'grad_attn_out_norm', 'grad_lru_out_norm', 'grad_w_out', 'grad_ffn2_norm', 'grad_ffn2_w_gate', 'grad_ffn2_w_up', 'grad_ffn2_w_down', 'grad_final_norm', 'delta_meta_tokens', 'delta_ffn1_norm', 'delta_ffn1_w_gate', 'delta_ffn1_w_up', 'delta_ffn1_w_down', 'delta_mix_norm', 'delta_w_in', 'delta_q_latent_norm', 'delta_w_uq', 'delta_kv_latent_norm', 'delta_w_uk', 'delta_w_uv', 'delta_q_head_norm', 'delta_k_head_norm', 'delta_conv_w', 'delta_conv_b', 'delta_gate_a_w', 'delta_gate_a_b', 'delta_gate_x_w', 'delta_gate_x_b', 'delta_lru_lambda', 'delta_attn_out_norm', 'delta_lru_out_norm', 'delta_w_out', 'delta_ffn2_norm', 'delta_ffn2_w_gate', 'delta_ffn2_w_up', 'delta_ffn2_w_down', 'delta_final_norm', 'new_m_meta_tokens', 'new_m_ffn1_norm', 'new_m_ffn1_w_gate', 'new_m_ffn1_w_up', 'new_m_ffn1_w_down', 'new_m_mix_norm', 'new_m_w_in', 'new_m_q_latent_norm', 'new_m_w_uq', 'new_m_kv_latent_norm', 'new_m_w_uk', 'new_m_w_uv', 'new_m_q_head_norm', 'new_m_k_head_norm', 'new_m_conv_w', 'new_m_conv_b', 'new_m_gate_a_w', 'new_m_gate_a_b', 'new_m_gate_x_w', 'new_m_gate_x_b', 'new_m_lru_lambda', 'new_m_attn_out_norm', 'new_m_lru_out_norm', 'new_m_w_out', 'new_m_ffn2_norm', 'new_m_ffn2_w_gate', 'new_m_ffn2_w_up', 'new_m_ffn2_w_down', 'new_m_final_norm', 'new_v_meta_tokens', 'new_v_ffn1_norm', 'new_v_ffn1_w_gate', 'new_v_ffn1_w_up', 'new_v_ffn1_w_down', 'new_v_mix_norm', 'new_v_w_in', 'new_v_q_latent_norm', 'new_v_w_uq', 'new_v_kv_latent_norm', 'new_v_w_uk', 'new_v_w_uv', 'new_v_q_head_norm', 'new_v_k_head_norm', 'new_v_conv_w', 'new_v_conv_b', 'new_v_gate_a_w', 'new_v_gate_a_b', 'new_v_gate_x_w', 'new_v_gate_x_b', 'new_v_lru_lambda', 'new_v_attn_out_norm', 'new_v_lru_out_norm', 'new_v_w_out', 'new_v_ffn2_norm', 'new_v_ffn2_w_gate', 'new_v_ffn2_w_up', 'new_v_ffn2_w_down', 'new_v_final_norm']
TWIN_LEAF_KINDS = {'loss': 'loss', 'grad_x': 'grad_x', 'grad_meta_tokens': 'grad_w', 'grad_ffn1_norm': 'grad_w', 'grad_ffn1_w_gate': 'grad_w', 'grad_ffn1_w_up': 'grad_w', 'grad_ffn1_w_down': 'grad_w', 'grad_mix_norm': 'grad_w', 'grad_w_in': 'grad_w', 'grad_q_latent_norm': 'grad_w', 'grad_w_uq': 'grad_w', 'grad_kv_latent_norm': 'grad_w', 'grad_w_uk': 'grad_w', 'grad_w_uv': 'grad_w', 'grad_q_head_norm': 'grad_w', 'grad_k_head_norm': 'grad_w', 'grad_conv_w': 'grad_w', 'grad_conv_b': 'grad_w', 'grad_gate_a_w': 'grad_w', 'grad_gate_a_b': 'grad_w', 'grad_gate_x_w': 'grad_w', 'grad_gate_x_b': 'grad_w', 'grad_lru_lambda': 'grad_w', 'grad_attn_out_norm': 'grad_w', 'grad_lru_out_norm': 'grad_w', 'grad_w_out': 'grad_w', 'grad_ffn2_norm': 'grad_w', 'grad_ffn2_w_gate': 'grad_w', 'grad_ffn2_w_up': 'grad_w', 'grad_ffn2_w_down': 'grad_w', 'grad_final_norm': 'grad_w', 'delta_meta_tokens': 'delta_w', 'delta_ffn1_norm': 'delta_w', 'delta_ffn1_w_gate': 'delta_w', 'delta_ffn1_w_up': 'delta_w', 'delta_ffn1_w_down': 'delta_w', 'delta_mix_norm': 'delta_w', 'delta_w_in': 'delta_w', 'delta_q_latent_norm': 'delta_w', 'delta_w_uq': 'delta_w', 'delta_kv_latent_norm': 'delta_w', 'delta_w_uk': 'delta_w', 'delta_w_uv': 'delta_w', 'delta_q_head_norm': 'delta_w', 'delta_k_head_norm': 'delta_w', 'delta_conv_w': 'delta_w', 'delta_conv_b': 'delta_w', 'delta_gate_a_w': 'delta_w', 'delta_gate_a_b': 'delta_w', 'delta_gate_x_w': 'delta_w', 'delta_gate_x_b': 'delta_w', 'delta_lru_lambda': 'delta_w', 'delta_attn_out_norm': 'delta_w', 'delta_lru_out_norm': 'delta_w', 'delta_w_out': 'delta_w', 'delta_ffn2_norm': 'delta_w', 'delta_ffn2_w_gate': 'delta_w', 'delta_ffn2_w_up': 'delta_w', 'delta_ffn2_w_down': 'delta_w', 'delta_final_norm': 'delta_w', 'new_m_meta_tokens': 'new_m', 'new_m_ffn1_norm': 'new_m', 'new_m_ffn1_w_gate': 'new_m', 'new_m_ffn1_w_up': 'new_m', 'new_m_ffn1_w_down': 'new_m', 'new_m_mix_norm': 'new_m', 'new_m_w_in': 'new_m', 'new_m_q_latent_norm': 'new_m', 'new_m_w_uq': 'new_m', 'new_m_kv_latent_norm': 'new_m', 'new_m_w_uk': 'new_m', 'new_m_w_uv': 'new_m', 'new_m_q_head_norm': 'new_m', 'new_m_k_head_norm': 'new_m', 'new_m_conv_w': 'new_m', 'new_m_conv_b': 'new_m', 'new_m_gate_a_w': 'new_m', 'new_m_gate_a_b': 'new_m', 'new_m_gate_x_w': 'new_m', 'new_m_gate_x_b': 'new_m', 'new_m_lru_lambda': 'new_m', 'new_m_attn_out_norm': 'new_m', 'new_m_lru_out_norm': 'new_m', 'new_m_w_out': 'new_m', 'new_m_ffn2_norm': 'new_m', 'new_m_ffn2_w_gate': 'new_m', 'new_m_ffn2_w_up': 'new_m', 'new_m_ffn2_w_down': 'new_m', 'new_m_final_norm': 'new_m', 'new_v_meta_tokens': 'new_v', 'new_v_ffn1_norm': 'new_v', 'new_v_ffn1_w_gate': 'new_v', 'new_v_ffn1_w_up': 'new_v', 'new_v_ffn1_w_down': 'new_v', 'new_v_mix_norm': 'new_v', 'new_v_w_in': 'new_v', 'new_v_q_latent_norm': 'new_v', 'new_v_w_uq': 'new_v', 'new_v_kv_latent_norm': 'new_v', 'new_v_w_uk': 'new_v', 'new_v_w_uv': 'new_v', 'new_v_q_head_norm': 'new_v', 'new_v_k_head_norm': 'new_v', 'new_v_conv_w': 'new_v', 'new_v_conv_b': 'new_v', 'new_v_gate_a_w': 'new_v', 'new_v_gate_a_b': 'new_v', 'new_v_gate_x_w': 'new_v', 'new_v_gate_x_b': 'new_v', 'new_v_lru_lambda': 'new_v', 'new_v_attn_out_norm': 'new_v', 'new_v_lru_out_norm': 'new_v', 'new_v_w_out': 'new_v', 'new_v_ffn2_norm': 'new_v', 'new_v_ffn2_w_gate': 'new_v', 'new_v_ffn2_w_up': 'new_v', 'new_v_ffn2_w_down': 'new_v', 'new_v_final_norm': 'new_v'}


def _forward(args):
    return _fwd_reference(*[args[k] for k in FWD_PARAMS])


def _output_shape():
    out = _jax.eval_shape(lambda: _forward(_fwd_setup_inputs(0)))
    return out.shape, out.dtype

N_MICROBATCH = 1
ADAM_LR = 0.001
ADAM_B1 = 0.9
ADAM_B2 = 0.999
ADAM_EPS = 1e-08
ADAM_WD = 0.01
ADAM_STEP = 10
PER_EXAMPLE_BATCH_AXIS = {'x': 0, 'loss_target': 0}
SHARED_INPUTS = []
_WEIGHT_DTYPES = {'meta_tokens': _jnp.float32, 'ffn1_norm': _jnp.float32, 'ffn1_w_gate': _jnp.float32, 'ffn1_w_up': _jnp.float32, 'ffn1_w_down': _jnp.float32, 'mix_norm': _jnp.float32, 'w_in': _jnp.float32, 'q_latent_norm': _jnp.float32, 'w_uq': _jnp.float32, 'kv_latent_norm': _jnp.float32, 'w_uk': _jnp.float32, 'w_uv': _jnp.float32, 'q_head_norm': _jnp.float32, 'k_head_norm': _jnp.float32, 'conv_w': _jnp.float32, 'conv_b': _jnp.float32, 'gate_a_w': _jnp.float32, 'gate_a_b': _jnp.float32, 'gate_x_w': _jnp.float32, 'gate_x_b': _jnp.float32, 'lru_lambda': _jnp.float32, 'attn_out_norm': _jnp.float32, 'lru_out_norm': _jnp.float32, 'w_out': _jnp.float32, 'ffn2_norm': _jnp.float32, 'ffn2_w_gate': _jnp.float32, 'ffn2_w_up': _jnp.float32, 'ffn2_w_down': _jnp.float32, 'final_norm': _jnp.float32}
MOMENT_SCALE = {'meta_tokens': 1.207784e-02, 'ffn1_norm': 9.489078e-02, 'ffn1_w_gate': 4.131425e-02, 'ffn1_w_up': 4.006058e-02, 'ffn1_w_down': 6.656036e-02, 'mix_norm': 1.939744e-01, 'w_in': 1.486040e-01, 'q_latent_norm': 1.702019e-01, 'w_uq': 1.040431e-01, 'kv_latent_norm': 4.284307e-01, 'w_uk': 1.217553e-01, 'w_uv': 1.449395e-01, 'q_head_norm': 2.143692e-01, 'k_head_norm': 2.065133e-01, 'conv_w': 1.345519e-01, 'conv_b': 1.730510e+00, 'gate_a_w': 5.635938e-02, 'gate_a_b': 3.784237e-02, 'gate_x_w': 1.054103e-01, 'gate_x_b': 4.544042e-02, 'lru_lambda': 6.610506e-02, 'attn_out_norm': 1.528982e-01, 'lru_out_norm': 1.568766e-01, 'w_out': 1.443405e-01, 'ffn2_norm': 5.242473e-02, 'ffn2_w_gate': 2.246680e-02, 'ffn2_w_up': 2.205798e-02, 'ffn2_w_down': 3.649034e-02, 'final_norm': 3.211078e+01}


def _to_microbatches(a, axis):
    t = _jnp.moveaxis(a, axis, 0)
    t = t.reshape((N_MICROBATCH, t.shape[0] // N_MICROBATCH) + t.shape[1:])
    return _jnp.moveaxis(t, 1, axis + 1)


def setup_inputs(seed: int = 0) -> dict:
    inp = _fwd_setup_inputs(seed)
    key = _jax.random.fold_in(_jax.random.key(seed), 7919)
    shape, _ = _output_shape()
    out = dict(inp)
    out["loss_target"] = _jax.random.normal(_jax.random.fold_in(key, 0), shape, _jnp.float32)
    for i, name in enumerate(TWIN_WEIGHTS):
        w = inp[name].astype(_jnp.float32)
        if MOMENT_SCALE is None:
            s = _jnp.sqrt(_jnp.mean(_jnp.square(w)) + 1e-30)
        else:
            s = MOMENT_SCALE[name]
        km, kv = _jax.random.split(_jax.random.fold_in(key, i + 1))
        out[name] = w
        out["m_" + name] = s * _jax.random.normal(km, w.shape, _jnp.float32)
        out["v_" + name] = (s * s) * _jax.random.uniform(kv, w.shape, _jnp.float32, 0.5, 1.5)
    if N_MICROBATCH > 1:
        for name, axis in PER_EXAMPLE_BATCH_AXIS.items():
            out[name] = _to_microbatches(out[name], axis)
    return {'x': out['x'], 'meta_tokens': out['meta_tokens'], 'ffn1_norm': out['ffn1_norm'], 'ffn1_w_gate': out['ffn1_w_gate'], 'ffn1_w_up': out['ffn1_w_up'], 'ffn1_w_down': out['ffn1_w_down'], 'mix_norm': out['mix_norm'], 'w_in': out['w_in'], 'q_latent_norm': out['q_latent_norm'], 'w_uq': out['w_uq'], 'kv_latent_norm': out['kv_latent_norm'], 'w_uk': out['w_uk'], 'w_uv': out['w_uv'], 'q_head_norm': out['q_head_norm'], 'k_head_norm': out['k_head_norm'], 'conv_w': out['conv_w'], 'conv_b': out['conv_b'], 'gate_a_w': out['gate_a_w'], 'gate_a_b': out['gate_a_b'], 'gate_x_w': out['gate_x_w'], 'gate_x_b': out['gate_x_b'], 'lru_lambda': out['lru_lambda'], 'attn_out_norm': out['attn_out_norm'], 'lru_out_norm': out['lru_out_norm'], 'w_out': out['w_out'], 'ffn2_norm': out['ffn2_norm'], 'ffn2_w_gate': out['ffn2_w_gate'], 'ffn2_w_up': out['ffn2_w_up'], 'ffn2_w_down': out['ffn2_w_down'], 'final_norm': out['final_norm'], 'loss_target': out['loss_target'], 'm_meta_tokens': out['m_meta_tokens'], 'm_ffn1_norm': out['m_ffn1_norm'], 'm_ffn1_w_gate': out['m_ffn1_w_gate'], 'm_ffn1_w_up': out['m_ffn1_w_up'], 'm_ffn1_w_down': out['m_ffn1_w_down'], 'm_mix_norm': out['m_mix_norm'], 'm_w_in': out['m_w_in'], 'm_q_latent_norm': out['m_q_latent_norm'], 'm_w_uq': out['m_w_uq'], 'm_kv_latent_norm': out['m_kv_latent_norm'], 'm_w_uk': out['m_w_uk'], 'm_w_uv': out['m_w_uv'], 'm_q_head_norm': out['m_q_head_norm'], 'm_k_head_norm': out['m_k_head_norm'], 'm_conv_w': out['m_conv_w'], 'm_conv_b': out['m_conv_b'], 'm_gate_a_w': out['m_gate_a_w'], 'm_gate_a_b': out['m_gate_a_b'], 'm_gate_x_w': out['m_gate_x_w'], 'm_gate_x_b': out['m_gate_x_b'], 'm_lru_lambda': out['m_lru_lambda'], 'm_attn_out_norm': out['m_attn_out_norm'], 'm_lru_out_norm': out['m_lru_out_norm'], 'm_w_out': out['m_w_out'], 'm_ffn2_norm': out['m_ffn2_norm'], 'm_ffn2_w_gate': out['m_ffn2_w_gate'], 'm_ffn2_w_up': out['m_ffn2_w_up'], 'm_ffn2_w_down': out['m_ffn2_w_down'], 'm_final_norm': out['m_final_norm'], 'v_meta_tokens': out['v_meta_tokens'], 'v_ffn1_norm': out['v_ffn1_norm'], 'v_ffn1_w_gate': out['v_ffn1_w_gate'], 'v_ffn1_w_up': out['v_ffn1_w_up'], 'v_ffn1_w_down': out['v_ffn1_w_down'], 'v_mix_norm': out['v_mix_norm'], 'v_w_in': out['v_w_in'], 'v_q_latent_norm': out['v_q_latent_norm'], 'v_w_uq': out['v_w_uq'], 'v_kv_latent_norm': out['v_kv_latent_norm'], 'v_w_uk': out['v_w_uk'], 'v_w_uv': out['v_w_uv'], 'v_q_head_norm': out['v_q_head_norm'], 'v_k_head_norm': out['v_k_head_norm'], 'v_conv_w': out['v_conv_w'], 'v_conv_b': out['v_conv_b'], 'v_gate_a_w': out['v_gate_a_w'], 'v_gate_a_b': out['v_gate_a_b'], 'v_gate_x_w': out['v_gate_x_w'], 'v_gate_x_b': out['v_gate_x_b'], 'v_lru_lambda': out['v_lru_lambda'], 'v_attn_out_norm': out['v_attn_out_norm'], 'v_lru_out_norm': out['v_lru_out_norm'], 'v_w_out': out['v_w_out'], 'v_ffn2_norm': out['v_ffn2_norm'], 'v_ffn2_w_gate': out['v_ffn2_w_gate'], 'v_ffn2_w_up': out['v_ffn2_w_up'], 'v_ffn2_w_down': out['v_ffn2_w_down'], 'v_final_norm': out['v_final_norm']}


def _loss(weights, diff, rest, loss_target):
    with _jax.named_scope("forward"):
        args = {**rest, TWIN_DIFF_INPUT: diff, **{k: w.astype(_WEIGHT_DTYPES[k]) for k, w in weights.items()}}
        y = _forward(args)
    with _jax.named_scope("loss_head"):
        err = _jnp.square(y.astype(_jnp.float32) - loss_target)
        return 0.5 * _jnp.sum(_jnp.mean(err, axis=-1)) if err.ndim else 0.5 * err


def _adamw(w, g, m, v):
    m = ADAM_B1 * m + (1.0 - ADAM_B1) * g
    v = ADAM_B2 * v + (1.0 - ADAM_B2) * _jnp.square(g)
    m_hat = m / (1.0 - ADAM_B1 ** ADAM_STEP)
    v_hat = v / (1.0 - ADAM_B2 ** ADAM_STEP)
    delta = -ADAM_LR * (m_hat / (_jnp.sqrt(v_hat) + ADAM_EPS) + ADAM_WD * w)
    return delta, m, v


def reference(x, meta_tokens, ffn1_norm, ffn1_w_gate, ffn1_w_up, ffn1_w_down, mix_norm, w_in, q_latent_norm, w_uq, kv_latent_norm, w_uk, w_uv, q_head_norm, k_head_norm, conv_w, conv_b, gate_a_w, gate_a_b, gate_x_w, gate_x_b, lru_lambda, attn_out_norm, lru_out_norm, w_out, ffn2_norm, ffn2_w_gate, ffn2_w_up, ffn2_w_down, final_norm, loss_target, m_meta_tokens, m_ffn1_norm, m_ffn1_w_gate, m_ffn1_w_up, m_ffn1_w_down, m_mix_norm, m_w_in, m_q_latent_norm, m_w_uq, m_kv_latent_norm, m_w_uk, m_w_uv, m_q_head_norm, m_k_head_norm, m_conv_w, m_conv_b, m_gate_a_w, m_gate_a_b, m_gate_x_w, m_gate_x_b, m_lru_lambda, m_attn_out_norm, m_lru_out_norm, m_w_out, m_ffn2_norm, m_ffn2_w_gate, m_ffn2_w_up, m_ffn2_w_down, m_final_norm, v_meta_tokens, v_ffn1_norm, v_ffn1_w_gate, v_ffn1_w_up, v_ffn1_w_down, v_mix_norm, v_w_in, v_q_latent_norm, v_w_uq, v_kv_latent_norm, v_w_uk, v_w_uv, v_q_head_norm, v_k_head_norm, v_conv_w, v_conv_b, v_gate_a_w, v_gate_a_b, v_gate_x_w, v_gate_x_b, v_lru_lambda, v_attn_out_norm, v_lru_out_norm, v_w_out, v_ffn2_norm, v_ffn2_w_gate, v_ffn2_w_up, v_ffn2_w_down, v_final_norm):
    given = dict(x=x, meta_tokens=meta_tokens, ffn1_norm=ffn1_norm, ffn1_w_gate=ffn1_w_gate, ffn1_w_up=ffn1_w_up, ffn1_w_down=ffn1_w_down, mix_norm=mix_norm, w_in=w_in, q_latent_norm=q_latent_norm, w_uq=w_uq, kv_latent_norm=kv_latent_norm, w_uk=w_uk, w_uv=w_uv, q_head_norm=q_head_norm, k_head_norm=k_head_norm, conv_w=conv_w, conv_b=conv_b, gate_a_w=gate_a_w, gate_a_b=gate_a_b, gate_x_w=gate_x_w, gate_x_b=gate_x_b, lru_lambda=lru_lambda, attn_out_norm=attn_out_norm, lru_out_norm=lru_out_norm, w_out=w_out, ffn2_norm=ffn2_norm, ffn2_w_gate=ffn2_w_gate, ffn2_w_up=ffn2_w_up, ffn2_w_down=ffn2_w_down, final_norm=final_norm, loss_target=loss_target, m_meta_tokens=m_meta_tokens, m_ffn1_norm=m_ffn1_norm, m_ffn1_w_gate=m_ffn1_w_gate, m_ffn1_w_up=m_ffn1_w_up, m_ffn1_w_down=m_ffn1_w_down, m_mix_norm=m_mix_norm, m_w_in=m_w_in, m_q_latent_norm=m_q_latent_norm, m_w_uq=m_w_uq, m_kv_latent_norm=m_kv_latent_norm, m_w_uk=m_w_uk, m_w_uv=m_w_uv, m_q_head_norm=m_q_head_norm, m_k_head_norm=m_k_head_norm, m_conv_w=m_conv_w, m_conv_b=m_conv_b, m_gate_a_w=m_gate_a_w, m_gate_a_b=m_gate_a_b, m_gate_x_w=m_gate_x_w, m_gate_x_b=m_gate_x_b, m_lru_lambda=m_lru_lambda, m_attn_out_norm=m_attn_out_norm, m_lru_out_norm=m_lru_out_norm, m_w_out=m_w_out, m_ffn2_norm=m_ffn2_norm, m_ffn2_w_gate=m_ffn2_w_gate, m_ffn2_w_up=m_ffn2_w_up, m_ffn2_w_down=m_ffn2_w_down, m_final_norm=m_final_norm, v_meta_tokens=v_meta_tokens, v_ffn1_norm=v_ffn1_norm, v_ffn1_w_gate=v_ffn1_w_gate, v_ffn1_w_up=v_ffn1_w_up, v_ffn1_w_down=v_ffn1_w_down, v_mix_norm=v_mix_norm, v_w_in=v_w_in, v_q_latent_norm=v_q_latent_norm, v_w_uq=v_w_uq, v_kv_latent_norm=v_kv_latent_norm, v_w_uk=v_w_uk, v_w_uv=v_w_uv, v_q_head_norm=v_q_head_norm, v_k_head_norm=v_k_head_norm, v_conv_w=v_conv_w, v_conv_b=v_conv_b, v_gate_a_w=v_gate_a_w, v_gate_a_b=v_gate_a_b, v_gate_x_w=v_gate_x_w, v_gate_x_b=v_gate_x_b, v_lru_lambda=v_lru_lambda, v_attn_out_norm=v_attn_out_norm, v_lru_out_norm=v_lru_out_norm, v_w_out=v_w_out, v_ffn2_norm=v_ffn2_norm, v_ffn2_w_gate=v_ffn2_w_gate, v_ffn2_w_up=v_ffn2_w_up, v_ffn2_w_down=v_ffn2_w_down, v_final_norm=v_final_norm)
    weights = {n: given[n] for n in TWIN_WEIGHTS}
    shared = {n: given[n] for n in SHARED_INPUTS}
    per_example = {n: given[n] for n in ['x']}
    grad_fn = _jax.value_and_grad(_loss, argnums=(0, 1))

    def one_microbatch(ex, loss_target):
        ex = dict(ex)
        diff = ex.pop(TWIN_DIFF_INPUT)
        return grad_fn(weights, diff, {**shared, **ex}, loss_target)

    if N_MICROBATCH == 1:
        loss, (grad_w, grad_x) = one_microbatch(per_example, given["loss_target"])
    else:
        def body(carry, xs):
            loss_sum, grad_sum = carry
            l_k, (gw_k, gx_k) = one_microbatch(xs[0], xs[1])
            with _jax.named_scope("update"):
                return (loss_sum + l_k, _jax.tree.map(_jnp.add, grad_sum, gw_k)), gx_k

        init = (_jnp.zeros((), _jnp.float32), _jax.tree.map(_jnp.zeros_like, weights))
        (loss, grad_w), grad_x = _jax.lax.scan(body, init, (per_example, given["loss_target"]))
    with _jax.named_scope("update"):
        delta_w, new_m, new_v = {}, {}, {}
        for n in TWIN_WEIGHTS:
            delta_w[n], new_m[n], new_v[n] = _adamw(weights[n], grad_w[n], given["m_" + n], given["v_" + n])
    return (loss, grad_x, *[grad_w[n] for n in TWIN_WEIGHTS], *[delta_w[n] for n in TWIN_WEIGHTS],
            *[new_m[n] for n in TWIN_WEIGHTS], *[new_v[n] for n in TWIN_WEIGHTS])
```

```python
import math

import jax
import jax.numpy as jnp
from jax import lax
from jax.experimental import pallas as pl
from jax.experimental.pallas import tpu as pltpu

F32 = jnp.float32
BF16 = jnp.bfloat16

D_MODEL = 1024
CHUNK = 64
CHUNK_SHIFT = 6
N_META = 16
PAD = CHUNK - N_META
FIRST_FRAME = PAD + N_META
MLA_HEADS = 4
D_NOPE = 128
D_ROPE = 64
D_QK = D_NOPE + D_ROPE
D_V = 128
HEAD_SLAB = 256
KV_RANK = 256
Q_RANK = 384
ROPE_THETA = 10000.0
LRU_WIDTH = 512
LRU_BLOCKS = 8
LRU_BLOCK = 64
LRU_TILE = 128
CONV_W = 4
C_RGLRU = 8.0
D_FF = 2816
MLA_IN = 768
EPS = 1e-6
NEG_INF = -1e30
N_DEV = 8
LANES = 128
VMEM_LIMIT = 52 * 1024 * 1024

ADAM_LR = 0.001
ADAM_B1 = 0.9
ADAM_B2 = 0.999
ADAM_EPS = 1e-08
ADAM_WD = 0.01
ADAM_STEP = 10

VMEM_WHOLE = pl.BlockSpec(memory_space=pltpu.VMEM)
HBM_WHOLE = pl.BlockSpec(memory_space=pl.ANY)


def _params(sems):
    return pltpu.CompilerParams(dimension_semantics=sems, vmem_limit_bytes=VMEM_LIMIT)


def _tile(n, cap, mult=16):
    best = None
    for t in range(mult, min(n, cap) + 1, mult):
        if n % t == 0:
            best = t
    assert best is not None, (n, cap, mult)
    return best


def _row(tm, d):
    return pl.BlockSpec((tm, d), lambda i: (i, 0))


def _fixed(shape):
    return pl.BlockSpec(shape, lambda i: (0,) * len(shape))


def _nn(a, b):
    return jnp.dot(a, b, preferred_element_type=F32)


def _nt(a, b):
    return lax.dot_general(a, b, (((1,), (1,)), ((), ())), preferred_element_type=F32)


def _tn(a, b):
    return lax.dot_general(a, b, (((0,), (0,)), ((), ())), preferred_element_type=F32)


def _sig(x):
    return 1.0 / (1.0 + jnp.exp(-x))


def _rms_r(x, n=None):
    n = x.shape[-1] if n is None else n
    return lax.rsqrt(jnp.sum(x * x, axis=-1, keepdims=True) * (1.0 / n) + EPS)


def _rms_bwd(x, r, g, dy, n=None):
    n = x.shape[-1] if n is None else n
    xhat = x * r
    dxhat = dy * g
    dx = r * (dxhat - xhat * (jnp.sum(dxhat * xhat, axis=-1, keepdims=True) * (1.0 / n)))
    return dx, jnp.sum(dy * xhat, axis=0, keepdims=True)


def _accumulate(ref, val, first):
    @pl.when(first)
    def _():
        ref[...] = val

    @pl.when(jnp.logical_not(first))
    def _():
        ref[...] += val


_GELU_C = math.sqrt(2.0 / math.pi)


def _gelu_and_grad(x):
    inner = _GELU_C * (x + 0.044715 * x * x * x)
    t = jnp.tanh(inner)
    gelu = 0.5 * x * (1.0 + t)
    dgelu = 0.5 * (1.0 + t) + 0.5 * x * (1.0 - t * t) * _GELU_C * (1.0 + 3.0 * 0.044715 * x * x)
    return gelu, dgelu


def _log1p_small(t):
    return jnp.where(t < 1e-3, t * (1.0 - t * (0.5 - t * (1.0 / 3.0))), jnp.log(1.0 + t))


def _softplus(x):
    return jnp.maximum(x, 0.0) + _log1p_small(jnp.exp(-jnp.abs(x)))


def _neg_expm1(x):
    return jnp.where(x > -1e-2, -x * (1.0 + x * (0.5 + x * (1.0 / 6.0))), 1.0 - jnp.exp(x))


def _ff_chunks(f):
    return 2 if (f // 2) % LANES == 0 else 1


def ffn_fwd(h, g, wg, wu, wd, name):
    n, d = h.shape
    f = wg.shape[1]
    tm = _tile(n, 352)
    nc = _ff_chunks(f)
    fc = f // nc

    def body(h_ref, g_ref, wg_ref, wu_ref, wd_ref, ho_ref, u_ref, a_ref, b_ref):
        x = h_ref[...]
        u = (x * _rms_r(x) * g_ref[...]).astype(BF16)
        acc = jnp.zeros((tm, d), F32)
        for c in range(nc):
            cols = slice(c * fc, (c + 1) * fc)
            a = _nn(u, wg_ref[:, cols])
            b = _nn(u, wu_ref[:, cols])
            s = (a * _sig(a) * b).astype(BF16)
            acc = acc + _nn(s, wd_ref[cols, :])
            a_ref[:, cols] = a.astype(BF16)
            b_ref[:, cols] = b.astype(BF16)
        ho_ref[...] = x + 0.5 * acc
        u_ref[...] = u

    return pl.pallas_call(
        body, name=name, grid=(n // tm,),
        in_specs=[_row(tm, d), _fixed((1, d)), VMEM_WHOLE, VMEM_WHOLE, VMEM_WHOLE],
        out_specs=[_row(tm, d), _row(tm, d), _row(tm, f), _row(tm, f)],
        out_shape=[jax.ShapeDtypeStruct((n, d), F32), jax.ShapeDtypeStruct((n, d), BF16),
                   jax.ShapeDtypeStruct((n, f), BF16), jax.ShapeDtypeStruct((n, f), BF16)],
        compiler_params=_params(("parallel",)),
    )(h, g, wg, wu, wd)


def ffn_bwd_act(dh, h, g, a, b, wg, wu, wd, name):
    n, d = h.shape
    f = wg.shape[1]
    tm = _tile(n, 192)
    nc = _ff_chunks(f)
    fc = f // nc

    def body(dh_ref, h_ref, g_ref, a_ref, b_ref, wg_ref, wu_ref, wd_ref,
             dhi_ref, da_ref, db_ref, sh_ref, dg_ref):
        x = h_ref[...]
        dy = dh_ref[...]
        r = _rms_r(x)
        dhh = (0.5 * dy).astype(BF16)
        du = jnp.zeros((tm, d), F32)
        for c in range(nc):
            cols = slice(c * fc, (c + 1) * fc)
            ds = _nt(dhh, wd_ref[cols, :])
            av = a_ref[:, cols].astype(F32)
            bv = b_ref[:, cols].astype(F32)
            sg = _sig(av)
            sil = av * sg
            da = (ds * bv * (sg * (1.0 + av * (1.0 - sg)))).astype(BF16)
            db = (ds * sil).astype(BF16)
            da_ref[:, cols] = da
            db_ref[:, cols] = db
            sh_ref[:, cols] = (0.5 * sil * bv).astype(BF16)
            du = du + _nt(da, wg_ref[:, cols]) + _nt(db, wu_ref[:, cols])
        dx, dg = _rms_bwd(x, r, g_ref[...], du)
        dhi_ref[...] = dy + dx
        _accumulate(dg_ref, dg, pl.program_id(0) == 0)

    return pl.pallas_call(
        body, name=name, grid=(n // tm,),
        in_specs=[_row(tm, d), _row(tm, d), _fixed((1, d)), _row(tm, f), _row(tm, f),
                  VMEM_WHOLE, VMEM_WHOLE, VMEM_WHOLE],
        out_specs=[_row(tm, d), _row(tm, f), _row(tm, f), _row(tm, f), _fixed((1, d))],
        out_shape=[jax.ShapeDtypeStruct((n, d), F32), jax.ShapeDtypeStruct((n, f), BF16),
                   jax.ShapeDtypeStruct((n, f), BF16), jax.ShapeDtypeStruct((n, f), BF16),
                   jax.ShapeDtypeStruct((1, d), F32)],
        compiler_params=_params(("arbitrary",)),
    )(dh, h, g, a, b, wg, wu, wd)


def tn_matmul(x, y, name):
    n, k = x.shape
    m = y.shape[1]
    mc = m
    while k * mc > 1536 * 1024 and mc % (2 * LANES) == 0:
        mc //= 2
    tm = _tile(n, 704)

    def body(x_ref, y_ref, o_ref):
        part = _tn(x_ref[...].astype(BF16), y_ref[...].astype(BF16))
        _accumulate(o_ref, part, pl.program_id(1) == 0)

    return pl.pallas_call(
        body, name=name, grid=(m // mc, n // tm),
        in_specs=[pl.BlockSpec((tm, k), lambda j, i: (i, 0)), pl.BlockSpec((tm, mc), lambda j, i: (i, j))],
        out_specs=pl.BlockSpec((k, mc), lambda j, i: (0, j)),
        out_shape=jax.ShapeDtypeStruct((k, m), F32),
        compiler_params=_params(("parallel", "arbitrary")),
    )(x, y)


def inproj_fwd(h, g, wm, wl):
    n, d = h.shape
    tm = _tile(n, 352)

    def body(h_ref, g_ref, wm_ref, wl_ref, u_ref, zm_ref, zl_ref):
        x = h_ref[...]
        u = (x * _rms_r(x) * g_ref[...]).astype(BF16)
        u_ref[...] = u
        zm_ref[...] = _nn(u, wm_ref[...])
        zl_ref[...] = _nn(u, wl_ref[...])

    return pl.pallas_call(
        body, name="inproj_fwd", grid=(n // tm,),
        in_specs=[_row(tm, d), _fixed((1, d)), VMEM_WHOLE, VMEM_WHOLE],
        out_specs=[_row(tm, d), _row(tm, MLA_IN), _row(tm, 2 * LRU_WIDTH)],
        out_shape=[jax.ShapeDtypeStruct((n, d), BF16), jax.ShapeDtypeStruct((n, MLA_IN), F32),
                   jax.ShapeDtypeStruct((n, 2 * LRU_WIDTH), F32)],
        compiler_params=_params(("parallel",)),
    )(h, g, wm, wl)


def inproj_bwd(dzm, du, dgate, dh2, h, g, wm, wl):
    n, d = h.shape
    tm = _tile(n, 352)

    def body(dzm_ref, du_ref, dgt_ref, dh2_ref, h_ref, g_ref, wm_ref, wl_ref, dh_ref, dg_ref):
        x = h_ref[...]
        dun = (_nt(dzm_ref[...].astype(BF16), wm_ref[...])
               + _nt(du_ref[...].astype(BF16), wl_ref[:, :LRU_WIDTH])
               + _nt(dgt_ref[...].astype(BF16), wl_ref[:, LRU_WIDTH:]))
        dx, dg = _rms_bwd(x, _rms_r(x), g_ref[...], dun)
        dh_ref[...] = dh2_ref[...] + dx
        _accumulate(dg_ref, dg, pl.program_id(0) == 0)

    return pl.pallas_call(
        body, name="inproj_bwd", grid=(n // tm,),
        in_specs=[_row(tm, MLA_IN), _row(tm, LRU_WIDTH), _row(tm, LRU_WIDTH), _row(tm, d), _row(tm, d),
                  _fixed((1, d)), VMEM_WHOLE, VMEM_WHOLE],
        out_specs=[_row(tm, d), _fixed((1, d))],
        out_shape=[jax.ShapeDtypeStruct((n, d), F32), jax.ShapeDtypeStruct((1, d), F32)],
        compiler_params=_params(("arbitrary",)),
    )(dzm, du, dgate, dh2, h, g, wm, wl)


def _rope_tables(lp):
    pos = jnp.arange(lp, dtype=F32) - float(PAD)
    half = D_ROPE // 2
    inv_freq = ROPE_THETA ** (-jnp.arange(0, half, dtype=F32) / half)
    ang = pos[:, None] * inv_freq[None, :]
    cos, sin = jnp.cos(ang), jnp.sin(ang)
    one = jnp.ones((lp, D_NOPE), F32)
    z_nope = jnp.zeros((lp, D_NOPE), F32)
    z_half = jnp.zeros((lp, half), F32)
    z_tail = jnp.zeros((lp, HEAD_SLAB - D_QK), F32)
    cosr = jnp.concatenate([one, cos, cos, z_tail], axis=1)
    sin_up = jnp.concatenate([z_nope, z_half, sin, z_tail], axis=1)
    sin_dn = jnp.concatenate([z_nope, -sin, z_half, z_tail], axis=1)
    return cosr, sin_up, sin_dn


def _rope(x, cosr, sin_up, sin_dn):
    half = D_ROPE // 2
    return x * cosr + pltpu.roll(x, half, axis=1) * sin_up + pltpu.roll(x, HEAD_SLAB - half, axis=1) * sin_dn


def _rope_bwd(dy, cosr, sin_up, sin_dn):
    half = D_ROPE // 2
    return (dy * cosr + pltpu.roll(dy * sin_up, HEAD_SLAB - half, axis=1)
            + pltpu.roll(dy * sin_dn, half, axis=1))


def _k_rope_slab(zm_tile):
    tm = zm_tile.shape[0]
    krp = zm_tile[:, Q_RANK + KV_RANK:MLA_IN]
    return jnp.concatenate([jnp.zeros((tm, D_NOPE), F32), krp], axis=1)


def mla_prep_fwd(zm, gql, gkvl, wuq, wuk, wuv, gqh, gkh, tables, lp):
    n = zm.shape[0]
    tm = _tile(lp, 352)
    per_seq = lp // tm
    width = MLA_HEADS * HEAD_SLAB
    scale = 1.0 / math.sqrt(D_QK)

    def body(zm_ref, gql_ref, gkvl_ref, wuq_ref, wuk_ref, wuv_ref, gqh_ref, gkh_ref,
             cos_ref, up_ref, dn_ref, q_ref, k_ref, v_ref, qn_ref, cn_ref):
        z = zm_ref[...]
        cq = z[:, :Q_RANK]
        ckv = z[:, Q_RANK:Q_RANK + KV_RANK]
        qn = (cq * _rms_r(cq) * gql_ref[...]).astype(BF16)
        cn = (ckv * _rms_r(ckv) * gkvl_ref[...]).astype(BF16)
        qn_ref[...] = qn
        cn_ref[...] = cn
        q_raw = _nn(qn, wuq_ref[...])
        k_raw = _nn(cn, wuk_ref[...])
        v_ref[...] = _nn(cn, wuv_ref[...]).astype(BF16)
        kr_slab = _k_rope_slab(z)
        cosr, sin_up, sin_dn = cos_ref[...], up_ref[...], dn_ref[...]
        for hd in range(MLA_HEADS):
            cols = slice(hd * HEAD_SLAB, (hd + 1) * HEAD_SLAB)
            xq = q_raw[:, cols]
            yq = _rope(xq * _rms_r(xq, D_QK) * gqh_ref[...], cosr, sin_up, sin_dn)
            q_ref[:, cols] = (yq * scale).astype(BF16)
            xk = k_raw[:, cols] + kr_slab
            yk = _rope(xk * _rms_r(xk, D_QK) * gkh_ref[...], cosr, sin_up, sin_dn)
            k_ref[:, cols] = yk.astype(BF16)

    tab = pl.BlockSpec((tm, HEAD_SLAB), lambda i: (i % per_seq, 0))
    return pl.pallas_call(
        body, name="mla_prep_fwd", grid=(n // tm,),
        in_specs=[_row(tm, MLA_IN), _fixed((1, Q_RANK)), _fixed((1, KV_RANK)), VMEM_WHOLE, VMEM_WHOLE, VMEM_WHOLE,
                  _fixed((1, HEAD_SLAB)), _fixed((1, HEAD_SLAB)), tab, tab, tab],
        out_specs=[_row(tm, width), _row(tm, width), _row(tm, MLA_HEADS * D_V), _row(tm, Q_RANK), _row(tm, KV_RANK)],
        out_shape=[jax.ShapeDtypeStruct((n, width), BF16), jax.ShapeDtypeStruct((n, width), BF16),
                   jax.ShapeDtypeStruct((n, MLA_HEADS * D_V), BF16), jax.ShapeDtypeStruct((n, Q_RANK), BF16),
                   jax.ShapeDtypeStruct((n, KV_RANK), BF16)],
        compiler_params=_params(("parallel",)),
    )(zm, gql, gkvl, wuq, wuk, wuv, gqh, gkh, *tables)


def mla_prep_bwd(dq, dk, dv, zm, qn, cn, gql, gkvl, wuq, wuk, wuv, gqh, gkh, tables, lp):
    n = zm.shape[0]
    tm = _tile(lp, 352)
    per_seq = lp // tm
    width = MLA_HEADS * HEAD_SLAB
    scale = 1.0 / math.sqrt(D_QK)

    def body(dq_ref, dk_ref, dv_ref, zm_ref, qn_ref, cn_ref, gql_ref, gkvl_ref, wuq_ref, wuk_ref, wuv_ref,
             gqh_ref, gkh_ref, cos_ref, up_ref, dn_ref,
             dzm_ref, dqr_ref, dkr_ref, dgql_ref, dgkvl_ref, dgqh_ref, dgkh_ref):
        z = zm_ref[...]
        cq = z[:, :Q_RANK]
        ckv = z[:, Q_RANK:Q_RANK + KV_RANK]
        q_raw = _nn(qn_ref[...], wuq_ref[...])
        k_raw = _nn(cn_ref[...], wuk_ref[...])
        kr_slab = _k_rope_slab(z)
        cosr, sin_up, sin_dn = cos_ref[...], up_ref[...], dn_ref[...]
        dgq = jnp.zeros((1, HEAD_SLAB), F32)
        dgk = jnp.zeros((1, HEAD_SLAB), F32)
        dkrp = jnp.zeros((tm, HEAD_SLAB - D_NOPE), F32)
        for hd in range(MLA_HEADS):
            cols = slice(hd * HEAD_SLAB, (hd + 1) * HEAD_SLAB)
            xq = q_raw[:, cols]
            dxn = _rope_bwd(dq_ref[:, cols] * scale, cosr, sin_up, sin_dn)
            dxq, dg = _rms_bwd(xq, _rms_r(xq, D_QK), gqh_ref[...], dxn, D_QK)
            dgq = dgq + dg
            dqr_ref[:, cols] = dxq.astype(BF16)
            xk = k_raw[:, cols] + kr_slab
            dxn = _rope_bwd(dk_ref[:, cols], cosr, sin_up, sin_dn)
            dxk, dg = _rms_bwd(xk, _rms_r(xk, D_QK), gkh_ref[...], dxn, D_QK)
            dgk = dgk + dg
            dkr_ref[:, cols] = dxk.astype(BF16)
            dkrp = dkrp + dxk[:, D_NOPE:]
        dqn = _nt(dqr_ref[...], wuq_ref[...])
        dcn = _nt(dkr_ref[...], wuk_ref[...]) + _nt(dv_ref[...].astype(BF16), wuv_ref[...])
        dcq, dg1 = _rms_bwd(cq, _rms_r(cq), gql_ref[...], dqn)
        dckv, dg2 = _rms_bwd(ckv, _rms_r(ckv), gkvl_ref[...], dcn)
        dzm_ref[:, :Q_RANK] = dcq
        dzm_ref[:, Q_RANK:Q_RANK + KV_RANK] = dckv
        dzm_ref[:, Q_RANK + KV_RANK:] = dkrp
        first = pl.program_id(0) == 0
        _accumulate(dgql_ref, dg1, first)
        _accumulate(dgkvl_ref, dg2, first)
        _accumulate(dgqh_ref, dgq, first)
        _accumulate(dgkh_ref, dgk, first)

    tab = pl.BlockSpec((tm, HEAD_SLAB), lambda i: (i % per_seq, 0))
    return pl.pallas_call(
        body, name="mla_prep_bwd", grid=(n // tm,),
        in_specs=[_row(tm, width), _row(tm, width), _row(tm, MLA_HEADS * D_V), _row(tm, MLA_IN),
                  _row(tm, Q_RANK), _row(tm, KV_RANK), _fixed((1, Q_RANK)), _fixed((1, KV_RANK)),
                  VMEM_WHOLE, VMEM_WHOLE, VMEM_WHOLE, _fixed((1, HEAD_SLAB)), _fixed((1, HEAD_SLAB)), tab, tab, tab],
        out_specs=[_row(tm, MLA_IN), _row(tm, width), _row(tm, width), _fixed((1, Q_RANK)), _fixed((1, KV_RANK)),
                   _fixed((1, HEAD_SLAB)), _fixed((1, HEAD_SLAB))],
        out_shape=[jax.ShapeDtypeStruct((n, MLA_IN), F32), jax.ShapeDtypeStruct((n, width), BF16),
                   jax.ShapeDtypeStruct((n, width), BF16), jax.ShapeDtypeStruct((1, Q_RANK), F32),
                   jax.ShapeDtypeStruct((1, KV_RANK), F32), jax.ShapeDtypeStruct((1, HEAD_SLAB), F32),
                   jax.ShapeDtypeStruct((1, HEAD_SLAB), F32)],
        compiler_params=_params(("arbitrary",)),
    )(dq, dk, dv, zm, qn, cn, gql, gkvl, wuq, wuk, wuv, gqh, gkh, *tables)


def _attn_tile(lp):
    return _tile(lp, 192, CHUNK)


def _chunk_mask(i, j, t):
    qpos = i * t + lax.broadcasted_iota(jnp.int32, (t, t), 0)
    kpos = j * t + lax.broadcasted_iota(jnp.int32, (t, t), 1)
    same_or_earlier = jnp.right_shift(kpos, CHUNK_SHIFT) <= jnp.right_shift(qpos, CHUNK_SHIFT)
    return jnp.logical_and(same_or_earlier, kpos >= PAD)


def attn_fwd(q, k, v, nb, lp):
    n = q.shape[0]
    t = _attn_tile(lp)
    nq = lp // t

    def body(q_ref, k_ref, v_ref, o_ref, lse_ref):
        i = pl.program_id(2)
        qv = q_ref[...]

        def kv_step(j, carry):
            m, l, acc = carry
            off = pl.multiple_of(j * t, t)
            s = _nt(qv, k_ref[pl.ds(off, t), :])
            s = jnp.where(_chunk_mask(i, j, t), s, NEG_INF)
            m_new = jnp.maximum(m, jnp.max(s, axis=-1, keepdims=True))
            p = jnp.exp(s - m_new)
            alpha = jnp.exp(m - m_new)
            l = alpha * l + jnp.sum(p, axis=-1, keepdims=True)
            acc = alpha * acc + _nn(p.astype(BF16), v_ref[pl.ds(off, t), :])
            return m_new, l, acc

        init = (jnp.full((t, 1), NEG_INF, F32), jnp.zeros((t, 1), F32), jnp.zeros((t, D_V), F32))
        m, l, acc = lax.fori_loop(0, i + 1, kv_step, init)
        o_ref[...] = acc * (1.0 / l)
        lse_ref[0] = jnp.broadcast_to(m + jnp.log(l), (t, LANES))

    return pl.pallas_call(
        body, name="attn_fwd", grid=(nb, MLA_HEADS, nq),
        in_specs=[pl.BlockSpec((t, HEAD_SLAB), lambda b, h, i: (b * nq + i, h)),
                  pl.BlockSpec((lp, HEAD_SLAB), lambda b, h, i: (b, h)),
                  pl.BlockSpec((lp, D_V), lambda b, h, i: (b, h))],
        out_specs=[pl.BlockSpec((t, D_V), lambda b, h, i: (b * nq + i, h)),
                   pl.BlockSpec((1, t, LANES), lambda b, h, i: (h, b * nq + i, 0))],
        out_shape=[jax.ShapeDtypeStruct((n, MLA_HEADS * D_V), F32),
                   jax.ShapeDtypeStruct((MLA_HEADS, n, LANES), F32)],
        compiler_params=_params(("parallel", "parallel", "parallel")),
    )(q, k, v)


def attn_bwd(q, k, v, o, do, lse, nb, lp):
    n = q.shape[0]
    t = _attn_tile(lp)
    nq = lp // t

    def body(q_ref, k_ref, v_ref, o_ref, do_ref, lse_ref, dq_ref, dk_ref, dv_ref):
        dk_ref[...] = jnp.zeros_like(dk_ref)
        dv_ref[...] = jnp.zeros_like(dv_ref)

        def q_step(i, _):
            qoff = pl.multiple_of(i * t, t)
            qv = q_ref[pl.ds(qoff, t), :]
            dov = do_ref[pl.ds(qoff, t), :]
            delta = jnp.sum(o_ref[pl.ds(qoff, t), :] * dov, axis=-1, keepdims=True)
            lse_q = jnp.max(lse_ref[0, pl.ds(qoff, t), :], axis=-1, keepdims=True)
            do16 = dov.astype(BF16)

            def kv_step(j, dq_acc):
                koff = pl.multiple_of(j * t, t)
                kv = k_ref[pl.ds(koff, t), :]
                s = jnp.where(_chunk_mask(i, j, t), _nt(qv, kv), NEG_INF)
                p = jnp.exp(s - lse_q)
                dp = _nt(do16, v_ref[pl.ds(koff, t), :])
                ds16 = (p * (dp - delta)).astype(BF16)
                dv_ref[pl.ds(koff, t), :] += _tn(p.astype(BF16), do16)
                dk_ref[pl.ds(koff, t), :] += _tn(ds16, qv)
                return dq_acc + _nn(ds16, kv)

            dq_ref[pl.ds(qoff, t), :] = lax.fori_loop(0, i + 1, kv_step, jnp.zeros((t, HEAD_SLAB), F32))
            return 0

        lax.fori_loop(0, nq, q_step, 0)

    wide = pl.BlockSpec((lp, HEAD_SLAB), lambda b, h: (b, h))
    thin = pl.BlockSpec((lp, D_V), lambda b, h: (b, h))
    width = MLA_HEADS * HEAD_SLAB
    return pl.pallas_call(
        body, name="attn_bwd", grid=(nb, MLA_HEADS),
        in_specs=[wide, wide, thin, thin, thin, pl.BlockSpec((1, lp, LANES), lambda b, h: (h, b, 0))],
        out_specs=[wide, wide, thin],
        out_shape=[jax.ShapeDtypeStruct((n, width), F32), jax.ShapeDtypeStruct((n, width), F32),
                   jax.ShapeDtypeStruct((n, MLA_HEADS * D_V), F32)],
        compiler_params=_params(("parallel", "parallel")),
    )(q, k, v, o, do, lse)


def _seq_rows(nb, lp, width):
    rows = lax.broadcasted_iota(jnp.int32, (lp, width), 0)
    return jnp.concatenate([rows] * nb, axis=0) if nb > 1 else rows


def _lru_gates(u, w_ref, cb, wa, wx, ba, bx, lam):
    xc = (cb + w_ref[pl.ds(3, 1), :] * u + w_ref[pl.ds(2, 1), :] * pltpu.roll(u, 1, axis=0)
          + w_ref[pl.ds(1, 1), :] * pltpu.roll(u, 2, axis=0) + w_ref[pl.ds(0, 1), :] * pltpu.roll(u, 3, axis=0))
    xc16 = xc.astype(BF16)
    ra = _sig(_nn(xc16, wa) + ba)
    ia = _sig(_nn(xc16, wx) + bx)
    sp = _softplus(-lam)
    log_a = -C_RGLRU * ra * sp
    a = jnp.exp(log_a)
    mult = jnp.sqrt(_neg_expm1(2.0 * log_a))
    return xc, xc16, ra, ia, sp, a, mult


def _scan_block_rows(width):
    return lax.broadcasted_iota(jnp.int32, (8, width), 0)


def lru_fwd(zl, conv_w, conv_b, wa, wx, ba, bx, lam, nb, lp):
    n = zl.shape[0]
    w = LRU_TILE
    nt = LRU_WIDTH // w
    nblk = lp // 8

    def body(u_ref, gt_ref, cw_ref, cb_ref, wa_ref, wx_ref, ba_ref, bx_ref, lam_ref, y_ref, h_ref, a_s, b_s):
        u = u_ref[...]
        xc, _, _, ia, _, a, mult = _lru_gates(u, cw_ref, cb_ref[...], wa_ref[...], wx_ref[...],
                                              ba_ref[...], bx_ref[...], lam_ref[...])
        row = _seq_rows(nb, lp, w)
        mult = jnp.where(row == PAD, 1.0, mult)
        a_s[...] = a
        b_s[...] = jnp.where(row < PAD, 0.0, mult * (ia * xc))
        r8 = _scan_block_rows(w)

        def blk(i, carry):
            out = []
            for s_id in range(nb):
                off = pl.multiple_of(s_id * lp + i * 8, 8)
                av = a_s[pl.ds(off, 8), :]
                bv = b_s[pl.ds(off, 8), :]
                for sh in (1, 2, 4):
                    keep = r8 >= sh
                    bv = jnp.where(keep, av * pltpu.roll(bv, sh, axis=0) + bv, bv)
                    av = jnp.where(keep, av * pltpu.roll(av, sh, axis=0), av)
                hv = bv + av * carry[s_id]
                h_ref[pl.ds(off, 8), :] = hv
                out.append(jnp.sum(jnp.where(r8 == 7, hv, 0.0), axis=0, keepdims=True))
            return tuple(out)

        lax.fori_loop(0, nblk, blk, tuple(jnp.zeros((1, w), F32) for _ in range(nb)))
        gelu, _ = _gelu_and_grad(gt_ref[...])
        y_ref[...] = h_ref[...] * gelu

    col = lambda c: (0, c)
    return pl.pallas_call(
        body, name="lru_fwd", grid=(nt,),
        in_specs=[pl.BlockSpec((n, w), col), pl.BlockSpec((n, w), lambda c: (0, nt + c)),
                  pl.BlockSpec((CONV_W, w), col), pl.BlockSpec((1, w), col),
                  pl.BlockSpec((w, w), lambda c: (c, c)), pl.BlockSpec((w, w), lambda c: (c, c)),
                  pl.BlockSpec((1, w), col), pl.BlockSpec((1, w), col), pl.BlockSpec((1, w), col)],
        out_specs=[pl.BlockSpec((n, w), col), pl.BlockSpec((n, w), col)],
        out_shape=[jax.ShapeDtypeStruct((n, LRU_WIDTH), F32), jax.ShapeDtypeStruct((n, LRU_WIDTH), F32)],
        scratch_shapes=[pltpu.VMEM((n, w), F32), pltpu.VMEM((n, w), F32)],
        compiler_params=_params(("parallel",)),
    )(zl, zl, conv_w, conv_b, wa, wx, ba, bx, lam)


def lru_bwd(zl, hs, dy, conv_w, conv_b, wa, wx, ba, bx, lam, nb, lp):
    n = zl.shape[0]
    w = LRU_TILE
    nt = LRU_WIDTH // w
    nblk = lp // 8

    def body(u_ref, gt_ref, h_ref, dy_ref, cw_ref, cb_ref, wa_ref, wx_ref, ba_ref, bx_ref, lam_ref,
             du_ref, dgt_ref, dcw_ref, dcb_ref, dba_ref, dbx_ref, dlam_ref, dwa_ref, dwx_ref, c_s, d_s, g_s):
        u = u_ref[...]
        lam = lam_ref[...]
        xc, xc16, ra, ia, sp, a, mult = _lru_gates(u, cw_ref, cb_ref[...], wa_ref[...], wx_ref[...],
                                                   ba_ref[...], bx_ref[...], lam)
        row = lax.broadcasted_iota(jnp.int32, (lp, w), 0)
        hv = h_ref[...]
        dyv = dy_ref[...]
        gelu, dgelu = _gelu_and_grad(gt_ref[...])
        dgt_ref[...] = jnp.where(row >= PAD, dyv * hv * dgelu, 0.0)
        c_s[...] = pltpu.roll(a, lp - 1, axis=0)
        d_s[...] = dyv * gelu
        r8 = _scan_block_rows(w)

        def blk(ii, carry):
            off = pl.multiple_of((nblk - 1 - ii) * 8, 8)
            cv = c_s[pl.ds(off, 8), :]
            dv = d_s[pl.ds(off, 8), :]
            for sh in (1, 2, 4):
                keep = r8 < 8 - sh
                dv = jnp.where(keep, cv * pltpu.roll(dv, 8 - sh, axis=0) + dv, dv)
                cv = jnp.where(keep, cv * pltpu.roll(cv, 8 - sh, axis=0), cv)
            gv = dv + cv * carry
            g_s[pl.ds(off, 8), :] = gv
            return jnp.sum(jnp.where(r8 == 0, gv, 0.0), axis=0, keepdims=True)

        lax.fori_loop(0, nblk, blk, jnp.zeros((1, w), F32))
        gv = g_s[...]
        first_row = row == PAD
        db = jnp.where(row >= PAD, gv, 0.0)
        da = jnp.where(row > PAD, gv * pltpu.roll(hv, 1, axis=0), 0.0)
        mult_eff = jnp.where(first_row, 1.0, mult)
        dmult = jnp.where(first_row, 0.0, db * (ia * xc))
        dia = db * mult_eff * xc
        dxc = db * mult_eff * ia
        dla = da * a - dmult * (a * a) / mult
        dra = dla * (-C_RGLRU * sp)
        dsp = jnp.sum(dla * (-C_RGLRU * ra), axis=0, keepdims=True)
        dpa = dra * ra * (1.0 - ra)
        dpx = dia * ia * (1.0 - ia)
        dpa16 = dpa.astype(BF16)
        dpx16 = dpx.astype(BF16)
        dxc = dxc + _nt(dpa16, wa_ref[...]) + _nt(dpx16, wx_ref[...])
        du = cw_ref[pl.ds(CONV_W - 1, 1), :] * dxc
        dcw = [jnp.sum(dxc * u, axis=0, keepdims=True)]
        for tap in range(1, CONV_W):
            dcw.insert(0, jnp.sum(dxc * pltpu.roll(u, tap, axis=0), axis=0, keepdims=True))
            du = du + cw_ref[pl.ds(CONV_W - 1 - tap, 1), :] * pltpu.roll(dxc, lp - tap, axis=0)
        du_ref[...] = jnp.where(row >= PAD, du, 0.0)
        first = pl.program_id(1) == 0
        _accumulate(dlam_ref, -_sig(-lam) * dsp, first)
        _accumulate(dba_ref, jnp.sum(dpa, axis=0, keepdims=True), first)
        _accumulate(dbx_ref, jnp.sum(dpx, axis=0, keepdims=True), first)
        _accumulate(dcb_ref, jnp.sum(dxc, axis=0, keepdims=True), first)
        _accumulate(dcw_ref, jnp.concatenate(dcw, axis=0), first)
        _accumulate(dwa_ref, _tn(xc16, dpa16)[None], first)
        _accumulate(dwx_ref, _tn(xc16, dpx16)[None], first)

    col = lambda c, b: (0, c)
    vec = pl.BlockSpec((1, w), col)
    mat = pl.BlockSpec((w, w), lambda c, b: (c, c))
    big = pl.BlockSpec((lp, w), lambda c, b: (b, c))
    dmat = pl.BlockSpec((1, w, w), lambda c, b: (c, 0, 0))
    return pl.pallas_call(
        body, name="lru_bwd", grid=(nt, nb),
        in_specs=[big, pl.BlockSpec((lp, w), lambda c, b: (b, nt + c)), big, big,
                  pl.BlockSpec((CONV_W, w), col), vec, mat, mat, vec, vec, vec],
        out_specs=[big, big, pl.BlockSpec((CONV_W, w), col), vec, vec, vec, vec, dmat, dmat],
        out_shape=[jax.ShapeDtypeStruct((n, LRU_WIDTH), F32), jax.ShapeDtypeStruct((n, LRU_WIDTH), F32),
                   jax.ShapeDtypeStruct((CONV_W, LRU_WIDTH), F32), jax.ShapeDtypeStruct((1, LRU_WIDTH), F32),
                   jax.ShapeDtypeStruct((1, LRU_WIDTH), F32), jax.ShapeDtypeStruct((1, LRU_WIDTH), F32),
                   jax.ShapeDtypeStruct((1, LRU_WIDTH), F32), jax.ShapeDtypeStruct((nt, w, w), F32),
                   jax.ShapeDtypeStruct((nt, w, w), F32)],
        scratch_shapes=[pltpu.VMEM((lp, w), F32), pltpu.VMEM((lp, w), F32), pltpu.VMEM((lp, w), F32)],
        compiler_params=_params(("parallel", "arbitrary")),
    )(zl, zl, hs, dy, conv_w, conv_b, wa, wx, ba, bx, lam)


def outproj_fwd(h, ya, yl, gao, glo, wout):
    n, d = h.shape
    half = ya.shape[1]
    tm = _tile(n, 352)

    def body(h_ref, ya_ref, yl_ref, gao_ref, glo_ref, w_ref, ho_ref, yn_ref):
        xa = ya_ref[...]
        xl = yl_ref[...]
        na = (xa * _rms_r(xa) * gao_ref[...]).astype(BF16)
        nl = (xl * _rms_r(xl) * glo_ref[...]).astype(BF16)
        yn_ref[:, :half] = na
        yn_ref[:, half:] = nl
        ho_ref[...] = h_ref[...] + _nn(na, w_ref[:half, :]) + _nn(nl, w_ref[half:, :])

    return pl.pallas_call(
        body, name="outproj_fwd", grid=(n // tm,),
        in_specs=[_row(tm, d), _row(tm, half), _row(tm, half), _fixed((1, half)), _fixed((1, half)), VMEM_WHOLE],
        out_specs=[_row(tm, d), _row(tm, 2 * half)],
        out_shape=[jax.ShapeDtypeStruct((n, d), F32), jax.ShapeDtypeStruct((n, 2 * half), BF16)],
        compiler_params=_params(("parallel",)),
    )(h, ya, yl, gao, glo, wout)


def outproj_bwd(dh, ya, yl, gao, glo, wout):
    n, d = dh.shape
    half = ya.shape[1]
    tm = _tile(n, 352)

    def body(dh_ref, ya_ref, yl_ref, gao_ref, glo_ref, w_ref, dya_ref, dyl_ref, dgao_ref, dglo_ref):
        d16 = dh_ref[...].astype(BF16)
        xa = ya_ref[...]
        xl = yl_ref[...]
        dxa, dga = _rms_bwd(xa, _rms_r(xa), gao_ref[...], _nt(d16, w_ref[:half, :]))
        dxl, dgl = _rms_bwd(xl, _rms_r(xl), glo_ref[...], _nt(d16, w_ref[half:, :]))
        dya_ref[...] = dxa
        dyl_ref[...] = dxl
        first = pl.program_id(0) == 0
        _accumulate(dgao_ref, dga, first)
        _accumulate(dglo_ref, dgl, first)

    return pl.pallas_call(
        body, name="outproj_bwd", grid=(n // tm,),
        in_specs=[_row(tm, d), _row(tm, half), _row(tm, half), _fixed((1, half)), _fixed((1, half)), VMEM_WHOLE],
        out_specs=[_row(tm, half), _row(tm, half), _fixed((1, half)), _fixed((1, half))],
        out_shape=[jax.ShapeDtypeStruct((n, half), F32), jax.ShapeDtypeStruct((n, half), F32),
                   jax.ShapeDtypeStruct((1, half), F32), jax.ShapeDtypeStruct((1, half), F32)],
        compiler_params=_params(("arbitrary",)),
    )(dh, ya, yl, gao, glo, wout)


def final_loss(h, g, tgt, lp):
    n, d = h.shape
    tm = _tile(lp, 352)
    per_seq = lp // tm

    def body(h_ref, g_ref, t_ref, loss_ref, dh_ref, dg_ref):
        i = pl.program_id(0)
        x = h_ref[...]
        gv = g_ref[...]
        r = _rms_r(x)
        row = (i % per_seq) * tm + lax.broadcasted_iota(jnp.int32, (tm, d), 0)
        diff = jnp.where(row >= FIRST_FRAME, x * r * gv - t_ref[...], 0.0)
        part = 0.5 * jnp.sum(jnp.sum(diff * diff, axis=-1, keepdims=True) * (1.0 / d), axis=0, keepdims=True)
        dx, dg = _rms_bwd(x, r, gv, diff * (1.0 / d))
        dh_ref[...] = dx
        _accumulate(loss_ref, jnp.broadcast_to(part, (1, LANES)), i == 0)
        _accumulate(dg_ref, dg, i == 0)

    return pl.pallas_call(
        body, name="final_loss", grid=(n // tm,),
        in_specs=[_row(tm, d), _fixed((1, d)), _row(tm, d)],
        out_specs=[_fixed((1, LANES)), _row(tm, d), _fixed((1, d))],
        out_shape=[jax.ShapeDtypeStruct((1, LANES), F32), jax.ShapeDtypeStruct((n, d), F32),
                   jax.ShapeDtypeStruct((1, d), F32)],
        compiler_params=_params(("arbitrary",)),
    )(h, g, tgt)


def _mesh_position():
    return lax.axis_index("x"), lax.axis_index("y"), lax.axis_index("c")


def _flat_index(x, y, c):
    return 4 * x + 2 * y + c


def _peers(x, y, c):
    out = []
    for k in range(1, N_DEV):
        fx, fy, fc = (k >> 2) & 1, (k >> 1) & 1, k & 1
        out.append((1 - x if fx else x, 1 - y if fy else y, 1 - c if fc else c))
    return out


def all_gather_shards(wpack, spack):
    def body(w_ref, s_ref, gw_ref, gs_ref, send_w, recv_w, send_s, recv_s, local):
        x, y, c = _mesh_position()
        me = _flat_index(x, y, c)
        own = [pltpu.make_async_copy(w_ref, gw_ref.at[me], local.at[0]),
               pltpu.make_async_copy(s_ref, gs_ref.at[me], local.at[1])]
        for cp in own:
            cp.start()
        sent = []
        for k, peer in enumerate(_peers(x, y, c)):
            for src, dst, ss, rs in ((w_ref, gw_ref, send_w, recv_w), (s_ref, gs_ref, send_s, recv_s)):
                cp = pltpu.make_async_remote_copy(src_ref=src, dst_ref=dst.at[me], send_sem=ss.at[k], recv_sem=rs.at[k],
                                                  device_id=peer, device_id_type=pl.DeviceIdType.MESH)
                cp.start()
                sent.append(cp)
        for cp in sent:
            cp.wait_recv()
        for cp in sent:
            cp.wait_send()
        for cp in own:
            cp.wait()

    return pl.pallas_call(
        body, name="all_gather_shards",
        in_specs=[HBM_WHOLE, HBM_WHOLE], out_specs=[HBM_WHOLE, HBM_WHOLE],
        out_shape=[jax.ShapeDtypeStruct((N_DEV,) + wpack.shape, wpack.dtype),
                   jax.ShapeDtypeStruct((N_DEV,) + spack.shape, spack.dtype)],
        scratch_shapes=[pltpu.SemaphoreType.DMA((N_DEV - 1,)), pltpu.SemaphoreType.DMA((N_DEV - 1,)),
                        pltpu.SemaphoreType.DMA((N_DEV - 1,)), pltpu.SemaphoreType.DMA((N_DEV - 1,)),
                        pltpu.SemaphoreType.DMA((2,))],
    )(wpack, spack)


def exchange_grads(gpack, spack):
    def body(g_ref, s_ref, rg_ref, rs_ref, send_g, recv_g, send_s, recv_s, local):
        x, y, c = _mesh_position()
        me = _flat_index(x, y, c)
        own = [pltpu.make_async_copy(g_ref.at[me], rg_ref.at[me], local.at[0]),
               pltpu.make_async_copy(s_ref, rs_ref.at[me], local.at[1])]
        for cp in own:
            cp.start()
        sent = []
        for k, peer in enumerate(_peers(x, y, c)):
            cg = pltpu.make_async_remote_copy(src_ref=g_ref.at[_flat_index(*peer)], dst_ref=rg_ref.at[me],
                                              send_sem=send_g.at[k], recv_sem=recv_g.at[k],
                                              device_id=peer, device_id_type=pl.DeviceIdType.MESH)
            cs = pltpu.make_async_remote_copy(src_ref=s_ref, dst_ref=rs_ref.at[me],
                                              send_sem=send_s.at[k], recv_sem=recv_s.at[k],
                                              device_id=peer, device_id_type=pl.DeviceIdType.MESH)
            cg.start()
            cs.start()
            sent += [cg, cs]
        for cp in sent:
            cp.wait_recv()
        for cp in sent:
            cp.wait_send()
        for cp in own:
            cp.wait()

    return pl.pallas_call(
        body, name="exchange_grads",
        in_specs=[HBM_WHOLE, HBM_WHOLE], out_specs=[HBM_WHOLE, HBM_WHOLE],
        out_shape=[jax.ShapeDtypeStruct(gpack.shape, gpack.dtype),
                   jax.ShapeDtypeStruct((N_DEV,) + spack.shape, spack.dtype)],
        scratch_shapes=[pltpu.SemaphoreType.DMA((N_DEV - 1,)), pltpu.SemaphoreType.DMA((N_DEV - 1,)),
                        pltpu.SemaphoreType.DMA((N_DEV - 1,)), pltpu.SemaphoreType.DMA((N_DEV - 1,)),
                        pltpu.SemaphoreType.DMA((2,))],
    )(gpack, spack)


def sum_slots(r, name):
    _, rows, lanes = r.shape
    tp = _tile(rows, 1100, 8)

    def body(r_ref, o_ref):
        acc = r_ref[0]
        for s in range(1, N_DEV):
            acc = acc + r_ref[s]
        o_ref[...] = acc

    return pl.pallas_call(
        body, name=name, grid=(rows // tp,),
        in_specs=[pl.BlockSpec((N_DEV, tp, lanes), lambda i: (0, i, 0))],
        out_specs=pl.BlockSpec((tp, lanes), lambda i: (i, 0)),
        out_shape=jax.ShapeDtypeStruct((rows, lanes), F32),
        compiler_params=_params(("parallel",)),
    )(r)


def adamw(w, g, m, v, name):
    rows, cols = w.shape
    tr = _tile(rows, 512, 8) if rows % 8 == 0 else rows
    c1 = 1.0 / (1.0 - ADAM_B1 ** ADAM_STEP)
    c2 = 1.0 / (1.0 - ADAM_B2 ** ADAM_STEP)

    def body(w_ref, g_ref, m_ref, v_ref, d_ref, mo_ref, vo_ref):
        gv = g_ref[...]
        mn = ADAM_B1 * m_ref[...] + (1.0 - ADAM_B1) * gv
        vn = ADAM_B2 * v_ref[...] + (1.0 - ADAM_B2) * (gv * gv)
        mo_ref[...] = mn
        vo_ref[...] = vn
        d_ref[...] = -ADAM_LR * ((mn * c1) / (jnp.sqrt(vn * c2) + ADAM_EPS) + ADAM_WD * w_ref[...])

    spec = pl.BlockSpec((tr, cols), lambda i: (i, 0))
    shape = jax.ShapeDtypeStruct((rows, cols), F32)
    return pl.pallas_call(
        body, name=name, grid=(rows // tr,),
        in_specs=[spec] * 4, out_specs=[spec] * 3, out_shape=[shape] * 3,
        compiler_params=_params(("parallel",)),
    )(w, g, m, v)


BIG = [("ffn1_w_gate", D_MODEL, D_FF, 1), ("ffn1_w_up", D_MODEL, D_FF, 1), ("ffn1_w_down", D_FF, D_MODEL, 0),
       ("w_in", D_MODEL, 1728, 1), ("w_out", D_MODEL, D_MODEL, 0),
       ("ffn2_w_gate", D_MODEL, D_FF, 1), ("ffn2_w_up", D_MODEL, D_FF, 1), ("ffn2_w_down", D_FF, D_MODEL, 0)]
MID = [("w_uq", Q_RANK, MLA_HEADS * D_QK, 1), ("w_uk", KV_RANK, MLA_HEADS * D_NOPE, 1),
       ("w_uv", KV_RANK, MLA_HEADS * D_V, 1)]
FINE = [("meta_tokens", N_META, D_MODEL, 16), ("conv_w", CONV_W, LRU_WIDTH, 8)]
SMALL = [("ffn1_norm", 1024), ("mix_norm", 1024), ("q_latent_norm", 384), ("kv_latent_norm", 256),
         ("q_head_norm", 192), ("k_head_norm", 192), ("conv_b", 512), ("gate_a_w", 32768), ("gate_a_b", 512),
         ("gate_x_w", 32768), ("gate_x_b", 512), ("lru_lambda", 512), ("attn_out_norm", 512),
         ("lru_out_norm", 512), ("ffn2_norm", 1024), ("final_norm", 1024)]


def _shard_rows(rows, cols):
    return rows * cols // N_DEV // LANES


def _to_lanes(a, rows):
    flat = a.reshape(-1)
    return jnp.pad(flat, (0, rows * LANES - flat.shape[0])).reshape(rows, LANES)


def _split_shards(full, axis):
    rows, cols = full.shape
    if axis == 1:
        full = full.reshape(rows, N_DEV, cols // N_DEV).transpose(1, 0, 2)
    return full.reshape(N_DEV, rows * cols // N_DEV // LANES, LANES)


def _join_shards(g, rows, cols, axis):
    if axis == 1:
        return g.reshape(N_DEV, rows, cols // N_DEV).transpose(1, 0, 2).reshape(rows, cols)
    return g.reshape(rows, cols)


def _small_rows(count):
    return -(-count // (8 * LANES)) * 8


def _pack_small(parts):
    return jnp.concatenate([_to_lanes(parts[name].astype(F32), _small_rows(cnt)) for name, cnt in SMALL], axis=0)


def _unpack_small(pack, like):
    out, off = {}, 0
    for name, cnt in SMALL:
        r = _small_rows(cnt)
        out[name] = pack[off:off + r].reshape(-1)[:cnt].reshape(like[name].shape)
        off += r
    return out


def _pad_heads(w, per_head):
    k = w.shape[0]
    w = w.reshape(k, MLA_HEADS, per_head)
    w = jnp.concatenate([w, jnp.zeros((k, MLA_HEADS, HEAD_SLAB - per_head), w.dtype)], axis=2)
    return w.reshape(k, MLA_HEADS * HEAD_SLAB)


def _unpad_heads(w, per_head):
    k = w.shape[0]
    return w.reshape(k, MLA_HEADS, HEAD_SLAB)[:, :, :per_head].reshape(k, MLA_HEADS * per_head)


def _block_diag(w, tile):
    nb, n, _ = w.shape
    eye = jnp.eye(nb, dtype=w.dtype)
    return (eye[:, None, :, None] * w[:, :, None, :]).reshape(nb * n, nb * n)


def _diag_blocks(dw):
    nt, t, _ = dw.shape
    per = t // LRU_BLOCK
    out = [dw[i, j * LRU_BLOCK:(j + 1) * LRU_BLOCK, j * LRU_BLOCK:(j + 1) * LRU_BLOCK]
           for i in range(nt) for j in range(per)]
    return jnp.stack(out, axis=0)


def local_step(x, tgt, meta, wts):
    nb, seq, d = x.shape
    lp = PAD + N_META + seq
    n = nb * lp
    front = jnp.concatenate([jnp.zeros((PAD, d), F32), meta], axis=0)
    h0 = jnp.concatenate([jnp.broadcast_to(front[None], (nb, FIRST_FRAME, d)), x], axis=1).reshape(n, d)
    tgt_p = jnp.concatenate([jnp.zeros((nb, FIRST_FRAME, d), F32), tgt], axis=1).reshape(n, d)
    tables = _rope_tables(lp)

    w_in = wts["w_in"]
    wm = jnp.concatenate([w_in[:, :MLA_IN - D_ROPE], jnp.zeros((d, D_ROPE), BF16)], axis=1)
    wl = w_in[:, MLA_IN - D_ROPE:]
    wuq = _pad_heads(wts["w_uq"], D_QK)
    wuk = _pad_heads(wts["w_uk"], D_NOPE)
    wuv = wts["w_uv"]
    gqh = jnp.concatenate([wts["q_head_norm"], jnp.zeros((1, HEAD_SLAB - D_QK), F32)], axis=1)
    gkh = jnp.concatenate([wts["k_head_norm"], jnp.zeros((1, HEAD_SLAB - D_QK), F32)], axis=1)
    wa = _block_diag(wts["gate_a_w"][0], LRU_TILE).astype(BF16)
    wx = _block_diag(wts["gate_x_w"][0], LRU_TILE).astype(BF16)

    h1, u1, a1, b1 = ffn_fwd(h0, wts["ffn1_norm"], wts["ffn1_w_gate"], wts["ffn1_w_up"], wts["ffn1_w_down"],
                             "ffn1_fwd")
    u2, zm, zl = inproj_fwd(h1, wts["mix_norm"], wm, wl)
    q, k, v, qn, cn = mla_prep_fwd(zm, wts["q_latent_norm"], wts["kv_latent_norm"], wuq, wuk, wuv, gqh, gkh,
                                   tables, lp)
    y_mla, lse = attn_fwd(q, k, v, nb, lp)
    y_lru, hs = lru_fwd(zl, wts["conv_w"], wts["conv_b"], wa, wx, wts["gate_a_b"], wts["gate_x_b"],
                        wts["lru_lambda"], nb, lp)
    h2, yn = outproj_fwd(h1, y_mla, y_lru, wts["attn_out_norm"], wts["lru_out_norm"], wts["w_out"])
    h3, u3, a3, b3 = ffn_fwd(h2, wts["ffn2_norm"], wts["ffn2_w_gate"], wts["ffn2_w_up"], wts["ffn2_w_down"],
                             "ffn2_fwd")
    loss, dh3, g_final = final_loss(h3, wts["final_norm"], tgt_p, lp)

    grads = {"final_norm": g_final}
    dh2, da3, db3, sh3, grads["ffn2_norm"] = ffn_bwd_act(
        dh3, h2, wts["ffn2_norm"], a3, b3, wts["ffn2_w_gate"], wts["ffn2_w_up"], wts["ffn2_w_down"], "ffn2_bwd")
    grads["ffn2_w_gate"] = tn_matmul(u3, da3, "ffn2_dwg")
    grads["ffn2_w_up"] = tn_matmul(u3, db3, "ffn2_dwu")
    grads["ffn2_w_down"] = tn_matmul(sh3, dh3, "ffn2_dwd")

    dy_mla, dy_lru, grads["attn_out_norm"], grads["lru_out_norm"] = outproj_bwd(
        dh2, y_mla, y_lru, wts["attn_out_norm"], wts["lru_out_norm"], wts["w_out"])
    grads["w_out"] = tn_matmul(yn, dh2, "dw_out")

    du, dgate, grads["conv_w"], grads["conv_b"], grads["gate_a_b"], grads["gate_x_b"], grads["lru_lambda"], dwa, dwx = (
        lru_bwd(zl, hs, dy_lru, wts["conv_w"], wts["conv_b"], wa, wx, wts["gate_a_b"], wts["gate_x_b"],
                wts["lru_lambda"], nb, lp))
    grads["gate_a_w"] = _diag_blocks(dwa)[None]
    grads["gate_x_w"] = _diag_blocks(dwx)[None]

    dq, dk, dv = attn_bwd(q, k, v, y_mla, dy_mla, lse, nb, lp)
    dzm, dqr, dkr, grads["q_latent_norm"], grads["kv_latent_norm"], dgqh, dgkh = mla_prep_bwd(
        dq, dk, dv, zm, qn, cn, wts["q_latent_norm"], wts["kv_latent_norm"], wuq, wuk, wuv, gqh, gkh, tables, lp)
    grads["q_head_norm"] = dgqh[:, :D_QK]
    grads["k_head_norm"] = dgkh[:, :D_QK]
    grads["w_uq"] = _unpad_heads(tn_matmul(qn, dqr, "dw_uq"), D_QK)
    grads["w_uk"] = _unpad_heads(tn_matmul(cn, dkr, "dw_uk"), D_NOPE)
    grads["w_uv"] = tn_matmul(cn, dv, "dw_uv")

    dh1, grads["mix_norm"] = inproj_bwd(dzm, du, dgate, dh2, h1, wts["mix_norm"], wm, wl)
    dwm = tn_matmul(u2, dzm, "dw_in_mla")
    dwl_u = tn_matmul(u2, du, "dw_in_u")
    dwl_g = tn_matmul(u2, dgate, "dw_in_gate")
    grads["w_in"] = jnp.concatenate([dwm[:, :MLA_IN - D_ROPE], dwl_u, dwl_g], axis=1)

    dh0, da1, db1, sh1, grads["ffn1_norm"] = ffn_bwd_act(
        dh1, h0, wts["ffn1_norm"], a1, b1, wts["ffn1_w_gate"], wts["ffn1_w_up"], wts["ffn1_w_down"], "ffn1_bwd")
    grads["ffn1_w_gate"] = tn_matmul(u1, da1, "ffn1_dwg")
    grads["ffn1_w_up"] = tn_matmul(u1, db1, "ffn1_dwu")
    grads["ffn1_w_down"] = tn_matmul(sh1, dh1, "ffn1_dwd")

    dh0 = dh0.reshape(nb, lp, d)
    grads["meta_tokens"] = jnp.sum(dh0[:, PAD:FIRST_FRAME], axis=0)
    grad_x = dh0[:, FIRST_FRAME:]
    return loss[0, 0], grad_x, grads


_WEIGHT_NAMES = ['meta_tokens', 'ffn1_norm', 'ffn1_w_gate', 'ffn1_w_up', 'ffn1_w_down', 'mix_norm', 'w_in',
                 'q_latent_norm', 'w_uq', 'kv_latent_norm', 'w_uk', 'w_uv', 'q_head_norm', 'k_head_norm', 'conv_w',
                 'conv_b', 'gate_a_w', 'gate_a_b', 'gate_x_w', 'gate_x_b', 'lru_lambda', 'attn_out_norm',
                 'lru_out_norm', 'w_out', 'ffn2_norm', 'ffn2_w_gate', 'ffn2_w_up', 'ffn2_w_down', 'final_norm']


def _two_d(a):
    if a.ndim == 3:
        return a.reshape(a.shape[1], a.shape[2])
    if a.ndim == 4:
        return a.reshape(a.shape[1] * a.shape[2], a.shape[3])
    return a


def train_step(x, loss_target, w, m, v):
    shard = {name: _two_d(w[name]) for name in _WEIGHT_NAMES}

    wpack = jnp.concatenate([_to_lanes(shard[name].astype(BF16), _shard_rows(r, c)) for name, r, c, _ in BIG + MID],
                            axis=0)
    spack = jnp.concatenate([_to_lanes(shard[name], rows) for name, _, _, rows in FINE], axis=0)
    gw, gs = all_gather_shards(wpack, spack)

    full, off = {}, 0
    for name, r, c, axis in BIG + MID:
        rows = _shard_rows(r, c)
        full[name] = _join_shards(gw[:, off:off + rows], r, c, axis)
        off += rows
    off = 0
    for name, r, c, rows in FINE:
        part = gs[:, off:off + rows].reshape(N_DEV, -1)[:, :r * c // N_DEV]
        full[name] = part.reshape(N_DEV, r, c // N_DEV).transpose(1, 0, 2).reshape(r, c)
        off += rows
    for name, _ in SMALL:
        full[name] = w[name] if name in ("gate_a_w", "gate_x_w") else shard[name]

    loss, grad_x, grads = local_step(x, loss_target, full["meta_tokens"], full)

    parts = [_split_shards(grads[name], axis) for name, _, _, axis in BIG + MID]
    for name, r, c, rows in FINE:
        g = grads[name].reshape(r, N_DEV, c // N_DEV).transpose(1, 0, 2).reshape(N_DEV, -1)
        g = jnp.pad(g, ((0, 0), (0, rows * LANES - g.shape[1])))
        parts.append(g.reshape(N_DEV, rows, LANES))
    gpack = jnp.concatenate(parts, axis=1)
    small_pack = _pack_small({name: grads[name] for name, _ in SMALL})
    rg, rs = exchange_grads(gpack, small_pack)
    g_shards = sum_slots(rg, "sum_sharded_grads")
    g_small = sum_slots(rs, "sum_replicated_grads")

    grad_out, delta, new_m, new_v = {}, {}, {}, {}
    off = 0
    for name, r, c, _ in BIG:
        rows = _shard_rows(r, c)
        g = g_shards[off:off + rows].reshape(shard[name].shape)
        off += rows
        d_, m_, v_ = adamw(shard[name], g, _two_d(m[name]), _two_d(v[name]), "adamw_" + name)
        grad_out[name], delta[name], new_m[name], new_v[name] = g, d_, m_, v_
    tail_rows = g_shards.shape[0] - off
    tail_names = [(name, _shard_rows(r, c)) for name, r, c, _ in MID] + [(name, rows) for name, _, _, rows in FINE]

    def pack_rest(src):
        small = _pack_small({name: src[name] for name, _ in SMALL})
        tail = jnp.concatenate([_to_lanes(_two_d(src[name]), rows) for name, rows in tail_names], axis=0)
        return jnp.concatenate([small, tail], axis=0)

    g_rest = jnp.concatenate([g_small, g_shards[off:off + tail_rows]], axis=0)
    rest = [g_rest] + list(adamw(pack_rest(w), g_rest, pack_rest(m), pack_rest(v), "adamw_rest"))
    n_small = g_small.shape[0]
    for dst, pack in zip((grad_out, delta, new_m, new_v), rest):
        dst.update(_unpack_small(pack[:n_small], w))
        o = n_small
        for name, rows in tail_names:
            dst[name] = pack[o:o + rows].reshape(-1)[:shard[name].size].reshape(w[name].shape)
            o += rows

    def shaped(d):
        return [d[name].reshape(w[name].shape) for name in _WEIGHT_NAMES]

    loss = lax.psum(loss, ("x", "y", "c"))
    return (loss, grad_x, *shaped(grad_out), *shaped(delta), *shaped(new_m), *shaped(new_v))


def kernel(x, meta_tokens, ffn1_norm, ffn1_w_gate, ffn1_w_up, ffn1_w_down, mix_norm, w_in, q_latent_norm, w_uq, kv_latent_norm, w_uk, w_uv, q_head_norm, k_head_norm, conv_w, conv_b, gate_a_w, gate_a_b, gate_x_w, gate_x_b, lru_lambda, attn_out_norm, lru_out_norm, w_out, ffn2_norm, ffn2_w_gate, ffn2_w_up, ffn2_w_down, final_norm, loss_target, m_meta_tokens, m_ffn1_norm, m_ffn1_w_gate, m_ffn1_w_up, m_ffn1_w_down, m_mix_norm, m_w_in, m_q_latent_norm, m_w_uq, m_kv_latent_norm, m_w_uk, m_w_uv, m_q_head_norm, m_k_head_norm, m_conv_w, m_conv_b, m_gate_a_w, m_gate_a_b, m_gate_x_w, m_gate_x_b, m_lru_lambda, m_attn_out_norm, m_lru_out_norm, m_w_out, m_ffn2_norm, m_ffn2_w_gate, m_ffn2_w_up, m_ffn2_w_down, m_final_norm, v_meta_tokens, v_ffn1_norm, v_ffn1_w_gate, v_ffn1_w_up, v_ffn1_w_down, v_mix_norm, v_w_in, v_q_latent_norm, v_w_uq, v_kv_latent_norm, v_w_uk, v_w_uv, v_q_head_norm, v_k_head_norm, v_conv_w, v_conv_b, v_gate_a_w, v_gate_a_b, v_gate_x_w, v_gate_x_b, v_lru_lambda, v_attn_out_norm, v_lru_out_norm, v_w_out, v_ffn2_norm, v_ffn2_w_gate, v_ffn2_w_up, v_ffn2_w_down, v_final_norm):
    args = locals()
    w = {name: args[name] for name in _WEIGHT_NAMES}
    m = {name: args["m_" + name] for name in _WEIGHT_NAMES}
    v = {name: args["v_" + name] for name in _WEIGHT_NAMES}
    return train_step(x, loss_target, w, m, v)
```

```python
import math

import jax
import jax.numpy as jnp
from jax import lax
from jax.experimental import pallas as pl
from jax.experimental.pallas import tpu as pltpu

F32 = jnp.float32
BF16 = jnp.bfloat16

D_MODEL = 1024
CHUNK = 64
CHUNK_SHIFT = 6
N_META = 16
PAD = CHUNK - N_META
FIRST_FRAME = PAD + N_META
MLA_HEADS = 4
D_NOPE = 128
D_ROPE = 64
D_QK = D_NOPE + D_ROPE
D_V = 128
HEAD_SLAB = 256
KV_RANK = 256
Q_RANK = 384
ROPE_THETA = 10000.0
LRU_WIDTH = 512
LRU_BLOCKS = 8
LRU_BLOCK = 64
LRU_TILE = 128
CONV_W = 4
C_RGLRU = 8.0
D_FF = 2816
MLA_IN = 768
EPS = 1e-6
NEG_INF = -1e30
N_DEV = 8
LANES = 128
VMEM_LIMIT = 52 * 1024 * 1024

ADAM_LR = 0.001
ADAM_B1 = 0.9
ADAM_B2 = 0.999
ADAM_EPS = 1e-08
ADAM_WD = 0.01
ADAM_STEP = 10

VMEM_WHOLE = pl.BlockSpec(memory_space=pltpu.VMEM)
HBM_WHOLE = pl.BlockSpec(memory_space=pl.ANY)


def _params(sems):
    if sems is None:
        return pltpu.CompilerParams(vmem_limit_bytes=VMEM_LIMIT)
    return pltpu.CompilerParams(dimension_semantics=sems, vmem_limit_bytes=VMEM_LIMIT)


def _tile(n, cap, mult=16):
    best = None
    for t in range(mult, min(n, cap) + 1, mult):
        if n % t == 0:
            best = t
    assert best is not None, (n, cap, mult)
    return best


def _row(tm, d):
    return pl.BlockSpec((tm, d), lambda i: (i, 0))


def _fixed(shape):
    return pl.BlockSpec(shape, lambda i: (0,) * len(shape))


def _mesh_position():
    return lax.axis_index("x"), lax.axis_index("y"), lax.axis_index("c")


def _flat_index(x, y, c):
    return 4 * x + 2 * y + c


def _peers(x, y, c):
    out = []
    for k in range(1, N_DEV):
        fx, fy, fc = (k >> 2) & 1, (k >> 1) & 1, k & 1
        out.append((1 - x if fx else x, 1 - y if fy else y, 1 - c if fc else c))
    return out


def _comm_out_shapes(srcs, modes):
    return [jax.ShapeDtypeStruct((N_DEV,) + s.shape if md == "gather" else s.shape, s.dtype)
            for s, md in zip(srcs, modes)]


def _comm_scratch(n):
    per_peer = n * (N_DEV - 1)
    return [pltpu.SemaphoreType.DMA((per_peer,)), pltpu.SemaphoreType.DMA((per_peer,)), pltpu.SemaphoreType.DMA((n,))]


def _comm_copies(src_refs, dst_refs, modes, send, recv, local):
    x, y, c = _mesh_position()
    me = _flat_index(x, y, c)
    n = len(modes)
    own, sent = [], []
    for t, (src, dst, md) in enumerate(zip(src_refs, dst_refs, modes)):
        mine = src if md == "gather" else src.at[me]
        own.append(pltpu.make_async_copy(mine, dst.at[me], local.at[t]))
    for k, peer in enumerate(_peers(x, y, c)):
        for t, (src, dst, md) in enumerate(zip(src_refs, dst_refs, modes)):
            part = src if md == "gather" else src.at[_flat_index(*peer)]
            sent.append(pltpu.make_async_remote_copy(
                src_ref=part, dst_ref=dst.at[me], send_sem=send.at[k * n + t], recv_sem=recv.at[k * n + t],
                device_id=peer, device_id_type=pl.DeviceIdType.MESH))
    return own, sent


def _comm_wait(own, sent):
    for cp in sent:
        cp.wait_recv()
    for cp in sent:
        cp.wait_send()
    for cp in own:
        cp.wait()


def _hosted(body, n_in, n_out, modes, grid):
    t = len(modes)

    def wrapped(*refs):
        ins, csrc = refs[:n_in], refs[n_in:n_in + t]
        outs = refs[n_in + t:n_in + t + n_out]
        cdst = refs[n_in + t + n_out:n_in + 2 * t + n_out]
        scratch = refs[n_in + 2 * t + n_out:-3]
        own, sent = _comm_copies(csrc, cdst, modes, *refs[-3:])
        first = pl.program_id(0) == 0
        last = pl.program_id(0) == grid[0] - 1
        for axis in range(1, len(grid)):
            first = jnp.logical_and(first, pl.program_id(axis) == 0)
            last = jnp.logical_and(last, pl.program_id(axis) == grid[axis] - 1)

        @pl.when(first)
        def _():
            for cp in own + sent:
                cp.start()

        body(*ins, *outs, *scratch)

        @pl.when(last)
        def _():
            _comm_wait(own, sent)

    return wrapped


def _call(body, name, grid, in_specs, out_specs, out_shape, sems, args, scratch=(), comm=None):
    if comm is None:
        outs = pl.pallas_call(body, name=name, grid=grid, in_specs=in_specs, out_specs=out_specs, out_shape=out_shape,
                              scratch_shapes=list(scratch), compiler_params=_params(sems))(*args)
        return outs, []
    srcs, modes = comm
    n = len(modes)
    res = pl.pallas_call(
        _hosted(body, len(in_specs), len(out_specs), modes, grid), name=name, grid=grid,
        in_specs=list(in_specs) + [HBM_WHOLE] * n, out_specs=list(out_specs) + [HBM_WHOLE] * n,
        out_shape=list(out_shape) + _comm_out_shapes(srcs, modes),
        scratch_shapes=list(scratch) + _comm_scratch(n),
        compiler_params=_params(("arbitrary",) * len(grid)))(*args, *srcs)
    return res[:len(out_specs)], res[len(out_specs):]


def exchange(srcs, modes, name):
    n = len(modes)

    def body(*refs):
        own, sent = _comm_copies(refs[:n], refs[n:2 * n], modes, *refs[2 * n:])
        for cp in own + sent:
            cp.start()
        _comm_wait(own, sent)

    return pl.pallas_call(body, name=name, in_specs=[HBM_WHOLE] * n, out_specs=[HBM_WHOLE] * n,
                          out_shape=_comm_out_shapes(srcs, modes), scratch_shapes=_comm_scratch(n))(*srcs)


def _nn(a, b):
    return jnp.dot(a, b, preferred_element_type=F32)


def _nt(a, b):
    return lax.dot_general(a, b, (((1,), (1,)), ((), ())), preferred_element_type=F32)


def _tn(a, b):
    return lax.dot_general(a, b, (((0,), (0,)), ((), ())), preferred_element_type=F32)


def _sig(x):
    return 1.0 / (1.0 + jnp.exp(-x))


def _rms_r(x, n=None):
    n = x.shape[-1] if n is None else n
    return lax.rsqrt(jnp.sum(x * x, axis=-1, keepdims=True) * (1.0 / n) + EPS)


def _rms_bwd(x, r, g, dy, n=None):
    n = x.shape[-1] if n is None else n
    xhat = x * r
    dxhat = dy * g
    dx = r * (dxhat - xhat * (jnp.sum(dxhat * xhat, axis=-1, keepdims=True) * (1.0 / n)))
    return dx, jnp.sum(dy * xhat, axis=0, keepdims=True)


def _accumulate(ref, val, first):
    @pl.when(first)
    def _():
        ref[...] = val

    @pl.when(jnp.logical_not(first))
    def _():
        ref[...] += val


_GELU_C = math.sqrt(2.0 / math.pi)


def _gelu_and_grad(x):
    inner = _GELU_C * (x + 0.044715 * x * x * x)
    t = jnp.tanh(inner)
    gelu = 0.5 * x * (1.0 + t)
    dgelu = 0.5 * (1.0 + t) + 0.5 * x * (1.0 - t * t) * _GELU_C * (1.0 + 3.0 * 0.044715 * x * x)
    return gelu, dgelu


def _log1p_small(t):
    return jnp.where(t < 1e-3, t * (1.0 - t * (0.5 - t * (1.0 / 3.0))), jnp.log(1.0 + t))


def _softplus(x):
    return jnp.maximum(x, 0.0) + _log1p_small(jnp.exp(-jnp.abs(x)))


def _neg_expm1(x):
    return jnp.where(x > -1e-2, -x * (1.0 + x * (0.5 + x * (1.0 / 6.0))), 1.0 - jnp.exp(x))


def _ff_chunks(f):
    return 2 if (f // 2) % LANES == 0 else 1


def ffn_fwd(h, g, wg, wu, wd, name, comm=None):
    n, d = h.shape
    f = wg.shape[1]
    tm = _tile(n, 352)
    nc = _ff_chunks(f)
    fc = f // nc

    def body(h_ref, g_ref, wg_ref, wu_ref, wd_ref, ho_ref, u_ref, a_ref, b_ref):
        x = h_ref[...]
        u = (x * _rms_r(x) * g_ref[...]).astype(BF16)
        acc = jnp.zeros((tm, d), F32)
        for c in range(nc):
            cols = slice(c * fc, (c + 1) * fc)
            a = _nn(u, wg_ref[:, cols])
            b = _nn(u, wu_ref[:, cols])
            s = (a * _sig(a) * b).astype(BF16)
            acc = acc + _nn(s, wd_ref[cols, :])
            a_ref[:, cols] = a.astype(BF16)
            b_ref[:, cols] = b.astype(BF16)
        ho_ref[...] = x + 0.5 * acc
        u_ref[...] = u

    return _call(
        body, name, (n // tm,),
        [_row(tm, d), _fixed((1, d)), VMEM_WHOLE, VMEM_WHOLE, VMEM_WHOLE],
        [_row(tm, d), _row(tm, d), _row(tm, f), _row(tm, f)],
        [jax.ShapeDtypeStruct((n, d), F32), jax.ShapeDtypeStruct((n, d), BF16),
         jax.ShapeDtypeStruct((n, f), BF16), jax.ShapeDtypeStruct((n, f), BF16)],
        ("parallel",), (h, g, wg, wu, wd), comm=comm)


def ffn_bwd_act(dh, h, g, a, b, wg, wu, wd, name, comm=None):
    n, d = h.shape
    f = wg.shape[1]
    tm = _tile(n, 192)
    nc = _ff_chunks(f)
    fc = f // nc

    def body(dh_ref, h_ref, g_ref, a_ref, b_ref, wg_ref, wu_ref, wd_ref,
             dhi_ref, da_ref, db_ref, sh_ref, dg_ref):
        x = h_ref[...]
        dy = dh_ref[...]
        r = _rms_r(x)
        dhh = (0.5 * dy).astype(BF16)
        du = jnp.zeros((tm, d), F32)
        for c in range(nc):
            cols = slice(c * fc, (c + 1) * fc)
            ds = _nt(dhh, wd_ref[cols, :])
            av = a_ref[:, cols].astype(F32)
            bv = b_ref[:, cols].astype(F32)
            sg = _sig(av)
            sil = av * sg
            da = (ds * bv * (sg * (1.0 + av * (1.0 - sg)))).astype(BF16)
            db = (ds * sil).astype(BF16)
            da_ref[:, cols] = da
            db_ref[:, cols] = db
            sh_ref[:, cols] = (0.5 * sil * bv).astype(BF16)
            du = du + _nt(da, wg_ref[:, cols]) + _nt(db, wu_ref[:, cols])
        dx, dg = _rms_bwd(x, r, g_ref[...], du)
        dhi_ref[...] = dy + dx
        _accumulate(dg_ref, dg, pl.program_id(0) == 0)

    return _call(
        body, name, (n // tm,),
        [_row(tm, d), _row(tm, d), _fixed((1, d)), _row(tm, f), _row(tm, f), VMEM_WHOLE, VMEM_WHOLE, VMEM_WHOLE],
        [_row(tm, d), _row(tm, f), _row(tm, f), _row(tm, f), _fixed((1, d))],
        [jax.ShapeDtypeStruct((n, d), F32), jax.ShapeDtypeStruct((n, f), BF16),
         jax.ShapeDtypeStruct((n, f), BF16), jax.ShapeDtypeStruct((n, f), BF16),
         jax.ShapeDtypeStruct((1, d), F32)],
        ("arbitrary",), (dh, h, g, a, b, wg, wu, wd), comm=comm)


def tn_matmul(x, y, name, out="f32"):
    n, k = x.shape
    m = y.shape[1]
    mc = m
    while k * mc > 1536 * 1024 and mc % (2 * LANES) == 0:
        mc //= 2
    tm = _tile(n, 704)
    ns = m // N_DEV
    per = mc // ns if out == "cols" else 0

    def body(x_ref, y_ref, o_ref, *acc):
        i = pl.program_id(1)
        part = _tn(x_ref[...].astype(BF16), y_ref[...].astype(BF16))
        if out == "f32":
            _accumulate(o_ref, part, i == 0)
            return
        _accumulate(acc[0], part, i == 0)

        @pl.when(i == n // tm - 1)
        def _():
            if out == "bf16":
                o_ref[...] = acc[0][...].astype(BF16)
            else:
                for s in range(per):
                    o_ref[s] = acc[0][:, s * ns:(s + 1) * ns].astype(BF16)

    if out == "cols":
        assert mc % ns == 0
        out_spec = pl.BlockSpec((per, k, ns), lambda j, i: (j, 0, 0))
        out_shape = jax.ShapeDtypeStruct((N_DEV, k, ns), BF16)
    else:
        out_spec = pl.BlockSpec((k, mc), lambda j, i: (0, j))
        out_shape = jax.ShapeDtypeStruct((k, m), F32 if out == "f32" else BF16)
    return pl.pallas_call(
        body, name=name, grid=(m // mc, n // tm),
        in_specs=[pl.BlockSpec((tm, k), lambda j, i: (i, 0)), pl.BlockSpec((tm, mc), lambda j, i: (i, j))],
        out_specs=out_spec, out_shape=out_shape,
        scratch_shapes=[] if out == "f32" else [pltpu.VMEM((k, mc), F32)],
        compiler_params=_params(("parallel", "arbitrary")),
    )(x, y)


def inproj_fwd(h, g, wm, wl):
    n, d = h.shape
    tm = _tile(n, 352)

    def body(h_ref, g_ref, wm_ref, wl_ref, u_ref, zm_ref, zl_ref):
        x = h_ref[...]
        u = (x * _rms_r(x) * g_ref[...]).astype(BF16)
        u_ref[...] = u
        zm_ref[...] = _nn(u, wm_ref[...])
        zl_ref[...] = _nn(u, wl_ref[...])

    return pl.pallas_call(
        body, name="inproj_fwd", grid=(n // tm,),
        in_specs=[_row(tm, d), _fixed((1, d)), VMEM_WHOLE, VMEM_WHOLE],
        out_specs=[_row(tm, d), _row(tm, MLA_IN), _row(tm, 2 * LRU_WIDTH)],
        out_shape=[jax.ShapeDtypeStruct((n, d), BF16), jax.ShapeDtypeStruct((n, MLA_IN), F32),
                   jax.ShapeDtypeStruct((n, 2 * LRU_WIDTH), F32)],
        compiler_params=_params(("parallel",)),
    )(h, g, wm, wl)


def inproj_bwd(dzm, du, dgate, dh2, h, g, wm, wl):
    n, d = h.shape
    tm = _tile(n, 352)

    def body(dzm_ref, du_ref, dgt_ref, dh2_ref, h_ref, g_ref, wm_ref, wl_ref, dh_ref, dg_ref):
        x = h_ref[...]
        dun = (_nt(dzm_ref[...].astype(BF16), wm_ref[...])
               + _nt(du_ref[...].astype(BF16), wl_ref[:, :LRU_WIDTH])
               + _nt(dgt_ref[...].astype(BF16), wl_ref[:, LRU_WIDTH:]))
        dx, dg = _rms_bwd(x, _rms_r(x), g_ref[...], dun)
        dh_ref[...] = dh2_ref[...] + dx
        _accumulate(dg_ref, dg, pl.program_id(0) == 0)

    return pl.pallas_call(
        body, name="inproj_bwd", grid=(n // tm,),
        in_specs=[_row(tm, MLA_IN), _row(tm, LRU_WIDTH), _row(tm, LRU_WIDTH), _row(tm, d), _row(tm, d),
                  _fixed((1, d)), VMEM_WHOLE, VMEM_WHOLE],
        out_specs=[_row(tm, d), _fixed((1, d))],
        out_shape=[jax.ShapeDtypeStruct((n, d), F32), jax.ShapeDtypeStruct((1, d), F32)],
        compiler_params=_params(("arbitrary",)),
    )(dzm, du, dgate, dh2, h, g, wm, wl)


def _rope_tables(lp):
    pos = jnp.arange(lp, dtype=F32) - float(PAD)
    half = D_ROPE // 2
    inv_freq = ROPE_THETA ** (-jnp.arange(0, half, dtype=F32) / half)
    ang = pos[:, None] * inv_freq[None, :]
    cos, sin = jnp.cos(ang), jnp.sin(ang)
    one = jnp.ones((lp, D_NOPE), F32)
    z_nope = jnp.zeros((lp, D_NOPE), F32)
    z_half = jnp.zeros((lp, half), F32)
    z_tail = jnp.zeros((lp, HEAD_SLAB - D_QK), F32)
    cosr = jnp.concatenate([one, cos, cos, z_tail], axis=1)
    sin_up = jnp.concatenate([z_nope, z_half, sin, z_tail], axis=1)
    sin_dn = jnp.concatenate([z_nope, -sin, z_half, z_tail], axis=1)
    return cosr, sin_up, sin_dn


def _rope(x, cosr, sin_up, sin_dn):
    half = D_ROPE // 2
    return x * cosr + pltpu.roll(x, half, axis=1) * sin_up + pltpu.roll(x, HEAD_SLAB - half, axis=1) * sin_dn


def _rope_bwd(dy, cosr, sin_up, sin_dn):
    half = D_ROPE // 2
    return (dy * cosr + pltpu.roll(dy * sin_up, HEAD_SLAB - half, axis=1)
            + pltpu.roll(dy * sin_dn, half, axis=1))


def _k_rope_slab(zm_tile):
    tm = zm_tile.shape[0]
    krp = zm_tile[:, Q_RANK + KV_RANK:MLA_IN]
    return jnp.concatenate([jnp.zeros((tm, D_NOPE), F32), krp], axis=1)


def mla_prep_fwd(zm, gql, gkvl, wuq, wuk, wuv, gqh, gkh, tables, lp):
    n = zm.shape[0]
    tm = _tile(lp, 352)
    per_seq = lp // tm
    width = MLA_HEADS * HEAD_SLAB
    scale = 1.0 / math.sqrt(D_QK)

    def body(zm_ref, gql_ref, gkvl_ref, wuq_ref, wuk_ref, wuv_ref, gqh_ref, gkh_ref,
             cos_ref, up_ref, dn_ref, q_ref, k_ref, v_ref, qn_ref, cn_ref):
        z = zm_ref[...]
        cq = z[:, :Q_RANK]
        ckv = z[:, Q_RANK:Q_RANK + KV_RANK]
        qn = (cq * _rms_r(cq) * gql_ref[...]).astype(BF16)
        cn = (ckv * _rms_r(ckv) * gkvl_ref[...]).astype(BF16)
        qn_ref[...] = qn
        cn_ref[...] = cn
        q_raw = _nn(qn, wuq_ref[...])
        k_raw = _nn(cn, wuk_ref[...])
        v_ref[...] = _nn(cn, wuv_ref[...]).astype(BF16)
        kr_slab = _k_rope_slab(z)
        cosr, sin_up, sin_dn = cos_ref[...], up_ref[...], dn_ref[...]
        for hd in range(MLA_HEADS):
            cols = slice(hd * HEAD_SLAB, (hd + 1) * HEAD_SLAB)
            xq = q_raw[:, cols]
            yq = _rope(xq * _rms_r(xq, D_QK) * gqh_ref[...], cosr, sin_up, sin_dn)
            q_ref[:, cols] = (yq * scale).astype(BF16)
            xk = k_raw[:, cols] + kr_slab
            yk = _rope(xk * _rms_r(xk, D_QK) * gkh_ref[...], cosr, sin_up, sin_dn)
            k_ref[:, cols] = yk.astype(BF16)

    tab = pl.BlockSpec((tm, HEAD_SLAB), lambda i: (i % per_seq, 0))
    return pl.pallas_call(
        body, name="mla_prep_fwd", grid=(n // tm,),
        in_specs=[_row(tm, MLA_IN), _fixed((1, Q_RANK)), _fixed((1, KV_RANK)), VMEM_WHOLE, VMEM_WHOLE, VMEM_WHOLE,
                  _fixed((1, HEAD_SLAB)), _fixed((1, HEAD_SLAB)), tab, tab, tab],
        out_specs=[_row(tm, width), _row(tm, width), _row(tm, MLA_HEADS * D_V), _row(tm, Q_RANK), _row(tm, KV_RANK)],
        out_shape=[jax.ShapeDtypeStruct((n, width), BF16), jax.ShapeDtypeStruct((n, width), BF16),
                   jax.ShapeDtypeStruct((n, MLA_HEADS * D_V), BF16), jax.ShapeDtypeStruct((n, Q_RANK), BF16),
                   jax.ShapeDtypeStruct((n, KV_RANK), BF16)],
        compiler_params=_params(("parallel",)),
    )(zm, gql, gkvl, wuq, wuk, wuv, gqh, gkh, *tables)


def mla_prep_bwd(dq, dk, dv, zm, qn, cn, gql, gkvl, wuq, wuk, wuv, gqh, gkh, tables, lp):
    n = zm.shape[0]
    tm = _tile(lp, 352)
    per_seq = lp // tm
    width = MLA_HEADS * HEAD_SLAB
    scale = 1.0 / math.sqrt(D_QK)

    def body(dq_ref, dk_ref, dv_ref, zm_ref, qn_ref, cn_ref, gql_ref, gkvl_ref, wuq_ref, wuk_ref, wuv_ref,
             gqh_ref, gkh_ref, cos_ref, up_ref, dn_ref,
             dzm_ref, dqr_ref, dkr_ref, dgql_ref, dgkvl_ref, dgqh_ref, dgkh_ref):
        z = zm_ref[...]
        cq = z[:, :Q_RANK]
        ckv = z[:, Q_RANK:Q_RANK + KV_RANK]
        q_raw = _nn(qn_ref[...], wuq_ref[...])
        k_raw = _nn(cn_ref[...], wuk_ref[...])
        kr_slab = _k_rope_slab(z)
        cosr, sin_up, sin_dn = cos_ref[...], up_ref[...], dn_ref[...]
        dgq = jnp.zeros((1, HEAD_SLAB), F32)
        dgk = jnp.zeros((1, HEAD_SLAB), F32)
        dkrp = jnp.zeros((tm, HEAD_SLAB - D_NOPE), F32)
        for hd in range(MLA_HEADS):
            cols = slice(hd * HEAD_SLAB, (hd + 1) * HEAD_SLAB)
            xq = q_raw[:, cols]
            dxn = _rope_bwd(dq_ref[:, cols] * scale, cosr, sin_up, sin_dn)
            dxq, dg = _rms_bwd(xq, _rms_r(xq, D_QK), gqh_ref[...], dxn, D_QK)
            dgq = dgq + dg
            dqr_ref[:, cols] = dxq.astype(BF16)
            xk = k_raw[:, cols] + kr_slab
            dxn = _rope_bwd(dk_ref[:, cols], cosr, sin_up, sin_dn)
            dxk, dg = _rms_bwd(xk, _rms_r(xk, D_QK), gkh_ref[...], dxn, D_QK)
            dgk = dgk + dg
            dkr_ref[:, cols] = dxk.astype(BF16)
            dkrp = dkrp + dxk[:, D_NOPE:]
        dqn = _nt(dqr_ref[...], wuq_ref[...])
        dcn = _nt(dkr_ref[...], wuk_ref[...]) + _nt(dv_ref[...].astype(BF16), wuv_ref[...])
        dcq, dg1 = _rms_bwd(cq, _rms_r(cq), gql_ref[...], dqn)
        dckv, dg2 = _rms_bwd(ckv, _rms_r(ckv), gkvl_ref[...], dcn)
        dzm_ref[:, :Q_RANK] = dcq
        dzm_ref[:, Q_RANK:Q_RANK + KV_RANK] = dckv
        dzm_ref[:, Q_RANK + KV_RANK:] = dkrp
        first = pl.program_id(0) == 0
        _accumulate(dgql_ref, dg1, first)
        _accumulate(dgkvl_ref, dg2, first)
        _accumulate(dgqh_ref, dgq, first)
        _accumulate(dgkh_ref, dgk, first)

    tab = pl.BlockSpec((tm, HEAD_SLAB), lambda i: (i % per_seq, 0))
    return pl.pallas_call(
        body, name="mla_prep_bwd", grid=(n // tm,),
        in_specs=[_row(tm, width), _row(tm, width), _row(tm, MLA_HEADS * D_V), _row(tm, MLA_IN),
                  _row(tm, Q_RANK), _row(tm, KV_RANK), _fixed((1, Q_RANK)), _fixed((1, KV_RANK)),
                  VMEM_WHOLE, VMEM_WHOLE, VMEM_WHOLE, _fixed((1, HEAD_SLAB)), _fixed((1, HEAD_SLAB)), tab, tab, tab],
        out_specs=[_row(tm, MLA_IN), _row(tm, width), _row(tm, width), _fixed((1, Q_RANK)), _fixed((1, KV_RANK)),
                   _fixed((1, HEAD_SLAB)), _fixed((1, HEAD_SLAB))],
        out_shape=[jax.ShapeDtypeStruct((n, MLA_IN), F32), jax.ShapeDtypeStruct((n, width), BF16),
                   jax.ShapeDtypeStruct((n, width), BF16), jax.ShapeDtypeStruct((1, Q_RANK), F32),
                   jax.ShapeDtypeStruct((1, KV_RANK), F32), jax.ShapeDtypeStruct((1, HEAD_SLAB), F32),
                   jax.ShapeDtypeStruct((1, HEAD_SLAB), F32)],
        compiler_params=_params(("arbitrary",)),
    )(dq, dk, dv, zm, qn, cn, gql, gkvl, wuq, wuk, wuv, gqh, gkh, *tables)


def _attn_tile(lp):
    return _tile(lp, 192, CHUNK)


def _chunk_mask(i, j, t):
    qpos = i * t + lax.broadcasted_iota(jnp.int32, (t, t), 0)
    kpos = j * t + lax.broadcasted_iota(jnp.int32, (t, t), 1)
    same_or_earlier = jnp.right_shift(kpos, CHUNK_SHIFT) <= jnp.right_shift(qpos, CHUNK_SHIFT)
    return jnp.logical_and(same_or_earlier, kpos >= PAD)


def attn_fwd(q, k, v, nb, lp, comm=None):
    n = q.shape[0]
    t = _attn_tile(lp)
    nq = lp // t

    def body(q_ref, k_ref, v_ref, o_ref, lse_ref):
        i = pl.program_id(2)
        qv = q_ref[...]

        def kv_step(j, carry):
            m, l, acc = carry
            off = pl.multiple_of(j * t, t)
            s = _nt(qv, k_ref[pl.ds(off, t), :])
            s = jnp.where(_chunk_mask(i, j, t), s, NEG_INF)
            m_new = jnp.maximum(m, jnp.max(s, axis=-1, keepdims=True))
            p = jnp.exp(s - m_new)
            alpha = jnp.exp(m - m_new)
            l = alpha * l + jnp.sum(p, axis=-1, keepdims=True)
            acc = alpha * acc + _nn(p.astype(BF16), v_ref[pl.ds(off, t), :])
            return m_new, l, acc

        init = (jnp.full((t, 1), NEG_INF, F32), jnp.zeros((t, 1), F32), jnp.zeros((t, D_V), F32))
        m, l, acc = lax.fori_loop(0, i + 1, kv_step, init)
        o_ref[...] = acc * (1.0 / l)
        lse_ref[0] = jnp.broadcast_to(m + jnp.log(l), (t, LANES))

    return _call(
        body, "attn_fwd", (nb, MLA_HEADS, nq),
        [pl.BlockSpec((t, HEAD_SLAB), lambda b, h, i: (b * nq + i, h)),
         pl.BlockSpec((lp, HEAD_SLAB), lambda b, h, i: (b, h)),
         pl.BlockSpec((lp, D_V), lambda b, h, i: (b, h))],
        [pl.BlockSpec((t, D_V), lambda b, h, i: (b * nq + i, h)),
         pl.BlockSpec((1, t, LANES), lambda b, h, i: (h, b * nq + i, 0))],
        [jax.ShapeDtypeStruct((n, MLA_HEADS * D_V), F32), jax.ShapeDtypeStruct((MLA_HEADS, n, LANES), F32)],
        ("parallel", "parallel", "parallel"), (q, k, v), comm=comm)


def attn_bwd(q, k, v, o, do, lse, nb, lp, comm=None):
    n = q.shape[0]
    t = _attn_tile(lp)
    nq = lp // t

    def body(q_ref, k_ref, v_ref, o_ref, do_ref, lse_ref, dq_ref, dk_ref, dv_ref):
        dk_ref[...] = jnp.zeros_like(dk_ref)
        dv_ref[...] = jnp.zeros_like(dv_ref)

        def q_step(i, _):
            qoff = pl.multiple_of(i * t, t)
            qv = q_ref[pl.ds(qoff, t), :]
            dov = do_ref[pl.ds(qoff, t), :]
            delta = jnp.sum(o_ref[pl.ds(qoff, t), :] * dov, axis=-1, keepdims=True)
            lse_q = jnp.max(lse_ref[0, pl.ds(qoff, t), :], axis=-1, keepdims=True)
            do16 = dov.astype(BF16)

            def kv_step(j, dq_acc):
                koff = pl.multiple_of(j * t, t)
                kv = k_ref[pl.ds(koff, t), :]
                s = jnp.where(_chunk_mask(i, j, t), _nt(qv, kv), NEG_INF)
                p = jnp.exp(s - lse_q)
                dp = _nt(do16, v_ref[pl.ds(koff, t), :])
                ds16 = (p * (dp - delta)).astype(BF16)
                dv_ref[pl.ds(koff, t), :] += _tn(p.astype(BF16), do16)
                dk_ref[pl.ds(koff, t), :] += _tn(ds16, qv)
                return dq_acc + _nn(ds16, kv)

            dq_ref[pl.ds(qoff, t), :] = lax.fori_loop(0, i + 1, kv_step, jnp.zeros((t, HEAD_SLAB), F32))
            return 0

        lax.fori_loop(0, nq, q_step, 0)

    wide = pl.BlockSpec((lp, HEAD_SLAB), lambda b, h: (b, h))
    thin = pl.BlockSpec((lp, D_V), lambda b, h: (b, h))
    width = MLA_HEADS * HEAD_SLAB
    return _call(
        body, "attn_bwd", (nb, MLA_HEADS),
        [wide, wide, thin, thin, thin, pl.BlockSpec((1, lp, LANES), lambda b, h: (h, b, 0))],
        [wide, wide, thin],
        [jax.ShapeDtypeStruct((n, width), F32), jax.ShapeDtypeStruct((n, width), F32),
         jax.ShapeDtypeStruct((n, MLA_HEADS * D_V), F32)],
        ("parallel", "parallel"), (q, k, v, o, do, lse), comm=comm)


def _seq_rows(nb, lp, width):
    rows = lax.broadcasted_iota(jnp.int32, (lp, width), 0)
    return jnp.concatenate([rows] * nb, axis=0) if nb > 1 else rows


def _lru_gates(u, w_ref, cb, wa, wx, ba, bx, lam):
    xc = (cb + w_ref[pl.ds(3, 1), :] * u + w_ref[pl.ds(2, 1), :] * pltpu.roll(u, 1, axis=0)
          + w_ref[pl.ds(1, 1), :] * pltpu.roll(u, 2, axis=0) + w_ref[pl.ds(0, 1), :] * pltpu.roll(u, 3, axis=0))
    xc16 = xc.astype(BF16)
    ra = _sig(_nn(xc16, wa) + ba)
    ia = _sig(_nn(xc16, wx) + bx)
    sp = _softplus(-lam)
    log_a = -C_RGLRU * ra * sp
    a = jnp.exp(log_a)
    mult = jnp.sqrt(_neg_expm1(2.0 * log_a))
    return xc, xc16, ra, ia, sp, a, mult


def _scan_block_rows(width):
    return lax.broadcasted_iota(jnp.int32, (8, width), 0)


def lru_fwd(zl, conv_w, conv_b, wa, wx, ba, bx, lam, nb, lp):
    n = zl.shape[0]
    w = LRU_TILE
    nt = LRU_WIDTH // w
    nblk = lp // 8

    def body(u_ref, gt_ref, cw_ref, cb_ref, wa_ref, wx_ref, ba_ref, bx_ref, lam_ref, y_ref, h_ref, a_s, b_s):
        u = u_ref[...]
        xc, _, _, ia, _, a, mult = _lru_gates(u, cw_ref, cb_ref[...], wa_ref[...], wx_ref[...],
                                              ba_ref[...], bx_ref[...], lam_ref[...])
        row = _seq_rows(nb, lp, w)
        mult = jnp.where(row == PAD, 1.0, mult)
        a_s[...] = a
        b_s[...] = jnp.where(row < PAD, 0.0, mult * (ia * xc))
        r8 = _scan_block_rows(w)

        def blk(i, carry):
            out = []
            for s_id in range(nb):
                off = pl.multiple_of(s_id * lp + i * 8, 8)
                av = a_s[pl.ds(off, 8), :]
                bv = b_s[pl.ds(off, 8), :]
                for sh in (1, 2, 4):
                    keep = r8 >= sh
                    bv = jnp.where(keep, av * pltpu.roll(bv, sh, axis=0) + bv, bv)
                    av = jnp.where(keep, av * pltpu.roll(av, sh, axis=0), av)
                hv = bv + av * carry[s_id]
                h_ref[pl.ds(off, 8), :] = hv
                out.append(jnp.sum(jnp.where(r8 == 7, hv, 0.0), axis=0, keepdims=True))
            return tuple(out)

        lax.fori_loop(0, nblk, blk, tuple(jnp.zeros((1, w), F32) for _ in range(nb)))
        gelu, _ = _gelu_and_grad(gt_ref[...])
        y_ref[...] = h_ref[...] * gelu

    col = lambda c: (0, c)
    return pl.pallas_call(
        body, name="lru_fwd", grid=(nt,),
        in_specs=[pl.BlockSpec((n, w), col), pl.BlockSpec((n, w), lambda c: (0, nt + c)),
                  pl.BlockSpec((CONV_W, w), col), pl.BlockSpec((1, w), col),
                  pl.BlockSpec((w, w), lambda c: (c, c)), pl.BlockSpec((w, w), lambda c: (c, c)),
                  pl.BlockSpec((1, w), col), pl.BlockSpec((1, w), col), pl.BlockSpec((1, w), col)],
        out_specs=[pl.BlockSpec((n, w), col), pl.BlockSpec((n, w), col)],
        out_shape=[jax.ShapeDtypeStruct((n, LRU_WIDTH), F32), jax.ShapeDtypeStruct((n, LRU_WIDTH), F32)],
        scratch_shapes=[pltpu.VMEM((n, w), F32), pltpu.VMEM((n, w), F32)],
        compiler_params=_params(("parallel",)),
    )(zl, zl, conv_w, conv_b, wa, wx, ba, bx, lam)


def lru_bwd(zl, hs, dy, conv_w, conv_b, wa, wx, ba, bx, lam, nb, lp):
    n = zl.shape[0]
    w = LRU_TILE
    nt = LRU_WIDTH // w
    nblk = lp // 8

    def body(u_ref, gt_ref, h_ref, dy_ref, cw_ref, cb_ref, wa_ref, wx_ref, ba_ref, bx_ref, lam_ref,
             du_ref, dgt_ref, dcw_ref, dcb_ref, dba_ref, dbx_ref, dlam_ref, dwa_ref, dwx_ref,
             c_s, d_s, g_s, dwa_s, dwx_s):
        u = u_ref[...]
        lam = lam_ref[...]
        xc, xc16, ra, ia, sp, a, mult = _lru_gates(u, cw_ref, cb_ref[...], wa_ref[...], wx_ref[...],
                                                   ba_ref[...], bx_ref[...], lam)
        row = lax.broadcasted_iota(jnp.int32, (lp, w), 0)
        hv = h_ref[...]
        dyv = dy_ref[...]
        gelu, dgelu = _gelu_and_grad(gt_ref[...])
        dgt_ref[...] = jnp.where(row >= PAD, dyv * hv * dgelu, 0.0)
        c_s[...] = pltpu.roll(a, lp - 1, axis=0)
        d_s[...] = dyv * gelu
        r8 = _scan_block_rows(w)

        def blk(ii, carry):
            off = pl.multiple_of((nblk - 1 - ii) * 8, 8)
            cv = c_s[pl.ds(off, 8), :]
            dv = d_s[pl.ds(off, 8), :]
            for sh in (1, 2, 4):
                keep = r8 < 8 - sh
                dv = jnp.where(keep, cv * pltpu.roll(dv, 8 - sh, axis=0) + dv, dv)
                cv = jnp.where(keep, cv * pltpu.roll(cv, 8 - sh, axis=0), cv)
            gv = dv + cv * carry
            g_s[pl.ds(off, 8), :] = gv
            return jnp.sum(jnp.where(r8 == 0, gv, 0.0), axis=0, keepdims=True)

        lax.fori_loop(0, nblk, blk, jnp.zeros((1, w), F32))
        gv = g_s[...]
        first_row = row == PAD
        db = jnp.where(row >= PAD, gv, 0.0)
        da = jnp.where(row > PAD, gv * pltpu.roll(hv, 1, axis=0), 0.0)
        mult_eff = jnp.where(first_row, 1.0, mult)
        dmult = jnp.where(first_row, 0.0, db * (ia * xc))
        dia = db * mult_eff * xc
        dxc = db * mult_eff * ia
        dla = da * a - dmult * (a * a) / mult
        dra = dla * (-C_RGLRU * sp)
        dsp = jnp.sum(dla * (-C_RGLRU * ra), axis=0, keepdims=True)
        dpa = dra * ra * (1.0 - ra)
        dpx = dia * ia * (1.0 - ia)
        dpa16 = dpa.astype(BF16)
        dpx16 = dpx.astype(BF16)
        dxc = dxc + _nt(dpa16, wa_ref[...]) + _nt(dpx16, wx_ref[...])
        du = cw_ref[pl.ds(CONV_W - 1, 1), :] * dxc
        dcw = [jnp.sum(dxc * u, axis=0, keepdims=True)]
        for tap in range(1, CONV_W):
            dcw.insert(0, jnp.sum(dxc * pltpu.roll(u, tap, axis=0), axis=0, keepdims=True))
            du = du + cw_ref[pl.ds(CONV_W - 1 - tap, 1), :] * pltpu.roll(dxc, lp - tap, axis=0)
        du_ref[...] = jnp.where(row >= PAD, du, 0.0)
        first = pl.program_id(1) == 0
        _accumulate(dlam_ref, -_sig(-lam) * dsp, first)
        _accumulate(dba_ref, jnp.sum(dpa, axis=0, keepdims=True), first)
        _accumulate(dbx_ref, jnp.sum(dpx, axis=0, keepdims=True), first)
        _accumulate(dcb_ref, jnp.sum(dxc, axis=0, keepdims=True), first)
        _accumulate(dcw_ref, jnp.concatenate(dcw, axis=0), first)
        _accumulate(dwa_s, _tn(xc16, dpa16), first)
        _accumulate(dwx_s, _tn(xc16, dpx16), first)

        @pl.when(pl.program_id(1) == nb - 1)
        def _():
            for j in range(w // LRU_BLOCK):
                blk_rows = slice(j * LRU_BLOCK, (j + 1) * LRU_BLOCK)
                dwa_ref[0, blk_rows, :] = dwa_s[blk_rows, blk_rows]
                dwx_ref[0, blk_rows, :] = dwx_s[blk_rows, blk_rows]

    col = lambda c, b: (0, c)
    vec = pl.BlockSpec((1, w), col)
    mat = pl.BlockSpec((w, w), lambda c, b: (c, c))
    big = pl.BlockSpec((lp, w), lambda c, b: (b, c))
    dmat = pl.BlockSpec((1, w, LRU_BLOCK), lambda c, b: (c, 0, 0))
    return pl.pallas_call(
        body, name="lru_bwd", grid=(nt, nb),
        in_specs=[big, pl.BlockSpec((lp, w), lambda c, b: (b, nt + c)), big, big,
                  pl.BlockSpec((CONV_W, w), col), vec, mat, mat, vec, vec, vec],
        out_specs=[big, big, pl.BlockSpec((CONV_W, w), col), vec, vec, vec, vec, dmat, dmat],
        out_shape=[jax.ShapeDtypeStruct((n, LRU_WIDTH), F32), jax.ShapeDtypeStruct((n, LRU_WIDTH), F32),
                   jax.ShapeDtypeStruct((CONV_W, LRU_WIDTH), F32), jax.ShapeDtypeStruct((1, LRU_WIDTH), F32),
                   jax.ShapeDtypeStruct((1, LRU_WIDTH), F32), jax.ShapeDtypeStruct((1, LRU_WIDTH), F32),
                   jax.ShapeDtypeStruct((1, LRU_WIDTH), F32), jax.ShapeDtypeStruct((nt, w, LRU_BLOCK), F32),
                   jax.ShapeDtypeStruct((nt, w, LRU_BLOCK), F32)],
        scratch_shapes=[pltpu.VMEM((lp, w), F32), pltpu.VMEM((lp, w), F32), pltpu.VMEM((lp, w), F32),
                        pltpu.VMEM((w, w), F32), pltpu.VMEM((w, w), F32)],
        compiler_params=_params(("parallel", "arbitrary")),
    )(zl, zl, hs, dy, conv_w, conv_b, wa, wx, ba, bx, lam)


def outproj_fwd(h, ya, yl, gao, glo, wout):
    n, d = h.shape
    half = ya.shape[1]
    tm = _tile(n, 352)

    def body(h_ref, ya_ref, yl_ref, gao_ref, glo_ref, w_ref, ho_ref, yn_ref):
        xa = ya_ref[...]
        xl = yl_ref[...]
        na = (xa * _rms_r(xa) * gao_ref[...]).astype(BF16)
        nl = (xl * _rms_r(xl) * glo_ref[...]).astype(BF16)
        yn_ref[:, :half] = na
        yn_ref[:, half:] = nl
        ho_ref[...] = h_ref[...] + _nn(na, w_ref[:half, :]) + _nn(nl, w_ref[half:, :])

    return pl.pallas_call(
        body, name="outproj_fwd", grid=(n // tm,),
        in_specs=[_row(tm, d), _row(tm, half), _row(tm, half), _fixed((1, half)), _fixed((1, half)), VMEM_WHOLE],
        out_specs=[_row(tm, d), _row(tm, 2 * half)],
        out_shape=[jax.ShapeDtypeStruct((n, d), F32), jax.ShapeDtypeStruct((n, 2 * half), BF16)],
        compiler_params=_params(("parallel",)),
    )(h, ya, yl, gao, glo, wout)


def outproj_bwd(dh, ya, yl, gao, glo, wout):
    n, d = dh.shape
    half = ya.shape[1]
    tm = _tile(n, 352)

    def body(dh_ref, ya_ref, yl_ref, gao_ref, glo_ref, w_ref, dya_ref, dyl_ref, dgao_ref, dglo_ref):
        d16 = dh_ref[...].astype(BF16)
        xa = ya_ref[...]
        xl = yl_ref[...]
        dxa, dga = _rms_bwd(xa, _rms_r(xa), gao_ref[...], _nt(d16, w_ref[:half, :]))
        dxl, dgl = _rms_bwd(xl, _rms_r(xl), glo_ref[...], _nt(d16, w_ref[half:, :]))
        dya_ref[...] = dxa
        dyl_ref[...] = dxl
        first = pl.program_id(0) == 0
        _accumulate(dgao_ref, dga, first)
        _accumulate(dglo_ref, dgl, first)

    return pl.pallas_call(
        body, name="outproj_bwd", grid=(n // tm,),
        in_specs=[_row(tm, d), _row(tm, half), _row(tm, half), _fixed((1, half)), _fixed((1, half)), VMEM_WHOLE],
        out_specs=[_row(tm, half), _row(tm, half), _fixed((1, half)), _fixed((1, half))],
        out_shape=[jax.ShapeDtypeStruct((n, half), F32), jax.ShapeDtypeStruct((n, half), F32),
                   jax.ShapeDtypeStruct((1, half), F32), jax.ShapeDtypeStruct((1, half), F32)],
        compiler_params=_params(("arbitrary",)),
    )(dh, ya, yl, gao, glo, wout)


def final_loss(h, g, tgt, lp):
    n, d = h.shape
    tm = _tile(lp, 352)
    per_seq = lp // tm

    def body(h_ref, g_ref, t_ref, loss_ref, dh_ref, dg_ref):
        i = pl.program_id(0)
        x = h_ref[...]
        gv = g_ref[...]
        r = _rms_r(x)
        row = (i % per_seq) * tm + lax.broadcasted_iota(jnp.int32, (tm, d), 0)
        diff = jnp.where(row >= FIRST_FRAME, x * r * gv - t_ref[...], 0.0)
        part = 0.5 * jnp.sum(jnp.sum(diff * diff, axis=-1, keepdims=True) * (1.0 / d), axis=0, keepdims=True)
        dx, dg = _rms_bwd(x, r, gv, diff * (1.0 / d))
        dh_ref[...] = dx
        _accumulate(loss_ref, jnp.broadcast_to(part, (1, LANES)), i == 0)
        _accumulate(dg_ref, dg, i == 0)

    return pl.pallas_call(
        body, name="final_loss", grid=(n // tm,),
        in_specs=[_row(tm, d), _fixed((1, d)), _row(tm, d)],
        out_specs=[_fixed((1, LANES)), _row(tm, d), _fixed((1, d))],
        out_shape=[jax.ShapeDtypeStruct((1, LANES), F32), jax.ShapeDtypeStruct((n, d), F32),
                   jax.ShapeDtypeStruct((1, d), F32)],
        compiler_params=_params(("arbitrary",)),
    )(h, g, tgt)


def assemble_cols(g, name):
    _, k, ns = g.shape

    def body(g_ref, o_ref):
        for j in range(N_DEV):
            o_ref[:, j * ns:(j + 1) * ns] = g_ref[j]

    return pl.pallas_call(body, name=name, out_shape=jax.ShapeDtypeStruct((k, N_DEV * ns), g.dtype),
                          compiler_params=_params(None))(g)


def assemble_mid(g_in, g_uq, g_uk, g_uv):
    d = g_in.shape[1]
    n_in = g_in.shape[2]
    mla_cols = MLA_IN - D_ROPE
    width = MLA_HEADS * HEAD_SLAB

    def body(in_ref, uq_ref, uk_ref, uv_ref, wm_ref, wl_ref, wuq_ref, wuk_ref, wuv_ref):
        wm_ref[:, mla_cols:] = jnp.zeros((d, D_ROPE), BF16)
        for j in range(N_DEV):
            lo, hi = j * n_in, (j + 1) * n_in
            if hi <= mla_cols:
                wm_ref[:, lo:hi] = in_ref[j]
            elif lo >= mla_cols:
                wl_ref[:, lo - mla_cols:hi - mla_cols] = in_ref[j]
            else:
                cut = mla_cols - lo
                wm_ref[:, lo:mla_cols] = in_ref[j, :, :cut]
                wl_ref[:, :hi - mla_cols] = in_ref[j, :, cut:]
        wuq_ref[...] = jnp.zeros_like(wuq_ref)
        wuk_ref[...] = jnp.zeros_like(wuk_ref)
        for j in range(N_DEV):
            for src, dst, per_head in ((uq_ref, wuq_ref, D_QK), (uk_ref, wuk_ref, D_NOPE)):
                ns = src.shape[2]
                head, within = divmod(j * ns, per_head)
                dst[:, head * HEAD_SLAB + within:head * HEAD_SLAB + within + ns] = src[j]
            ns = uv_ref.shape[2]
            wuv_ref[:, j * ns:(j + 1) * ns] = uv_ref[j]

    return pl.pallas_call(
        body, name="assemble_mid",
        out_shape=[jax.ShapeDtypeStruct((d, MLA_IN), BF16), jax.ShapeDtypeStruct((d, 2 * LRU_WIDTH), BF16),
                   jax.ShapeDtypeStruct((Q_RANK, width), BF16), jax.ShapeDtypeStruct((KV_RANK, width), BF16),
                   jax.ShapeDtypeStruct((KV_RANK, MLA_HEADS * D_V), BF16)],
        compiler_params=_params(None))(g_in, g_uq, g_uk, g_uv)


def split_mid_grads(dwm, dwl_u, dwl_g, dwuq, dwuk, dwuv, dconv):
    d = dwm.shape[0]
    mla_cols = MLA_IN - D_ROPE
    n_in = (mla_cols + 2 * LRU_WIDTH) // N_DEV
    n_uq, n_uk, n_uv = MLA_HEADS * D_QK // N_DEV, MLA_HEADS * D_NOPE // N_DEV, MLA_HEADS * D_V // N_DEV
    n_conv = LRU_WIDTH // N_DEV

    def body(dwm_ref, dwu_ref, dwg_ref, dwuq_ref, dwuk_ref, dwuv_ref, dc_ref, in_ref, uq_ref, uk_ref, uv_ref, cv_ref):
        def w_in_cols(lo, hi):
            parts = []
            for ref, start, size in ((dwm_ref, 0, mla_cols), (dwu_ref, mla_cols, LRU_WIDTH),
                                     (dwg_ref, mla_cols + LRU_WIDTH, LRU_WIDTH)):
                a, b = max(lo, start), min(hi, start + size)
                if a < b:
                    parts.append(ref[:, a - start:b - start])
            return parts

        for j in range(N_DEV):
            off = 0
            for part in w_in_cols(j * n_in, (j + 1) * n_in):
                in_ref[j, :, off:off + part.shape[1]] = part.astype(BF16)
                off += part.shape[1]
            for src, dst, per_head, ns in ((dwuq_ref, uq_ref, D_QK, n_uq), (dwuk_ref, uk_ref, D_NOPE, n_uk)):
                head, within = divmod(j * ns, per_head)
                dst[j] = src[:, head * HEAD_SLAB + within:head * HEAD_SLAB + within + ns].astype(BF16)
            uv_ref[j] = dwuv_ref[:, j * n_uv:(j + 1) * n_uv].astype(BF16)
            cv_ref[j] = dc_ref[:, j * n_conv:(j + 1) * n_conv]

    return pl.pallas_call(
        body, name="split_mid_grads",
        out_shape=[jax.ShapeDtypeStruct((N_DEV, d, n_in), BF16), jax.ShapeDtypeStruct((N_DEV, Q_RANK, n_uq), BF16),
                   jax.ShapeDtypeStruct((N_DEV, KV_RANK, n_uk), BF16), jax.ShapeDtypeStruct((N_DEV, KV_RANK, n_uv), BF16),
                   jax.ShapeDtypeStruct((N_DEV, CONV_W, n_conv), F32)],
        compiler_params=_params(None))(dwm, dwl_u, dwl_g, dwuq, dwuk, dwuv, dconv)


def meta_grad(dh0, nb, lp):
    d = dh0.shape[1]
    ns = d // N_DEV
    per_seq = lp // N_META

    def body(x_ref, o_ref):
        x = x_ref[...]
        for j in range(N_DEV):
            _accumulate(o_ref.at[j], x[:, j * ns:(j + 1) * ns], pl.program_id(0) == 0)

    return pl.pallas_call(
        body, name="meta_grad", grid=(nb,),
        in_specs=[pl.BlockSpec((N_META, d), lambda b: (b * per_seq + PAD // N_META, 0))],
        out_specs=pl.BlockSpec((N_DEV, N_META, ns), lambda b: (0, 0, 0)),
        out_shape=jax.ShapeDtypeStruct((N_DEV, N_META, ns), F32),
        compiler_params=_params(("arbitrary",)))(dh0)


VECTORS = [("ffn1_norm", 1024), ("mix_norm", 1024), ("q_latent_norm", 384), ("kv_latent_norm", 256),
           ("q_head_norm", 192), ("k_head_norm", 192), ("conv_b", 512), ("gate_a_b", 512), ("gate_x_b", 512),
           ("lru_lambda", 512), ("attn_out_norm", 512), ("lru_out_norm", 512), ("ffn2_norm", 1024),
           ("final_norm", 1024)]
VEC_ROWS = 16
GATES = ["gate_a_w", "gate_x_w"]


def pack_vectors(grads):
    def body(*refs):
        o_ref = refs[-1]
        o_ref[...] = jnp.zeros_like(o_ref)
        for t, (ref, (_, cnt)) in enumerate(zip(refs[:-1], VECTORS)):
            o_ref[t:t + 1, :cnt] = ref[:, :cnt]

    return pl.pallas_call(body, name="pack_vectors", out_shape=jax.ShapeDtypeStruct((VEC_ROWS, D_MODEL), F32),
                          compiler_params=_params(None))(*[grads[name] for name, _ in VECTORS])


def _adamw_update(w, g, m, v):
    c1 = 1.0 / (1.0 - ADAM_B1 ** ADAM_STEP)
    c2 = 1.0 / (1.0 - ADAM_B2 ** ADAM_STEP)
    mn = ADAM_B1 * m + (1.0 - ADAM_B1) * g
    vn = ADAM_B2 * v + (1.0 - ADAM_B2) * (g * g)
    delta = -ADAM_LR * ((mn * c1) / (jnp.sqrt(vn * c2) + ADAM_EPS) + ADAM_WD * w)
    return delta, mn, vn


def _sum_slots(ref, index=()):
    acc = ref[(0,) + index].astype(F32)
    for s in range(1, N_DEV):
        acc = acc + ref[(s,) + index].astype(F32)
    return acc


def adamw_sharded(r, w, m, v, name):
    rows, cols = w.shape
    tr = _tile(rows, 256, 16) if rows % 16 == 0 else rows

    def body(r_ref, w_ref, m_ref, v_ref, g_ref, d_ref, mo_ref, vo_ref):
        g = _sum_slots(r_ref)
        g_ref[...] = g
        d_ref[...], mo_ref[...], vo_ref[...] = _adamw_update(w_ref[...], g, m_ref[...], v_ref[...])

    spec = pl.BlockSpec((tr, cols), lambda i: (i, 0))
    shape = jax.ShapeDtypeStruct((rows, cols), F32)
    return pl.pallas_call(
        body, name=name, grid=(rows // tr,),
        in_specs=[pl.BlockSpec((N_DEV, tr, cols), lambda i: (0, i, 0))] + [spec] * 3,
        out_specs=[spec] * 4, out_shape=[shape] * 4,
        compiler_params=_params(("parallel",)),
    )(r, w, m, v)


def adamw_small(r_vec, r_gates, w, m, v):
    nt = len(VECTORS) + len(GATES)

    def body(*refs):
        rv_ref = refs[0]
        rg_refs = refs[1:1 + len(GATES)]
        base = 1 + len(GATES)
        w_refs, m_refs, v_refs = (refs[base + i * nt:base + (i + 1) * nt] for i in range(3))
        outs = refs[base + 3 * nt:]
        g_o, d_o, m_o, v_o = (outs[i * nt:(i + 1) * nt] for i in range(4))
        for t in range(nt):
            if t < len(VECTORS):
                cnt = VECTORS[t][1]
                g = _sum_slots(rv_ref, (slice(t, t + 1), slice(0, cnt)))
            else:
                g = _sum_slots(rg_refs[t - len(VECTORS)])
            g_o[t][...] = g
            d_o[t][...], m_o[t][...], v_o[t][...] = _adamw_update(w_refs[t][...], g, m_refs[t][...], v_refs[t][...])

    shapes = [jax.ShapeDtypeStruct(a.shape, F32) for a in w]
    res = pl.pallas_call(body, name="adamw_small", out_shape=shapes * 4,
                         compiler_params=_params(None))(r_vec, *r_gates, *w, *m, *v)
    return [res[i * nt:(i + 1) * nt] for i in range(4)]


def _block_diag(w):
    nb, n, _ = w.shape
    eye = jnp.eye(nb, dtype=w.dtype)
    return (eye[:, None, :, None] * w[:, :, None, :]).reshape(nb * n, nb * n)


def _two_d(a):
    if a.ndim == 3:
        return a.reshape(a.shape[1], a.shape[2])
    if a.ndim == 4:
        return a.reshape(a.shape[1] * a.shape[2], a.shape[3])
    return a


_WEIGHT_NAMES = ['meta_tokens', 'ffn1_norm', 'ffn1_w_gate', 'ffn1_w_up', 'ffn1_w_down', 'mix_norm', 'w_in',
                 'q_latent_norm', 'w_uq', 'kv_latent_norm', 'w_uk', 'w_uv', 'q_head_norm', 'k_head_norm', 'conv_w',
                 'conv_b', 'gate_a_w', 'gate_a_b', 'gate_x_w', 'gate_x_b', 'lru_lambda', 'attn_out_norm',
                 'lru_out_norm', 'w_out', 'ffn2_norm', 'ffn2_w_gate', 'ffn2_w_up', 'ffn2_w_down', 'final_norm']


def train_step(x, tgt, w, m, v):
    nb, seq, d = x.shape
    lp = PAD + N_META + seq
    n = nb * lp
    sh = {name: _two_d(w[name]) for name in _WEIGHT_NAMES}
    m2 = {name: _two_d(m[name]) for name in _WEIGHT_NAMES}
    v2 = {name: _two_d(v[name]) for name in _WEIGHT_NAMES}

    def b16(name):
        return sh[name].astype(BF16)

    out = {}

    def update(name, landed):
        out[name] = adamw_sharded(landed, sh[name], m2[name], v2[name], "adamw_" + name)

    g_wg1, g_wu1, g_wd1, g_meta, g_conv = exchange(
        [b16("ffn1_w_gate"), b16("ffn1_w_up"), b16("ffn1_w_down"), sh["meta_tokens"], sh["conv_w"]],
        ["gather"] * 5, "gather_ffn1")
    wg1 = assemble_cols(g_wg1, "assemble_wg1")
    wu1 = assemble_cols(g_wu1, "assemble_wu1")
    wd1 = g_wd1.reshape(D_FF, d)
    meta = assemble_cols(g_meta, "assemble_meta")
    conv_w = assemble_cols(g_conv, "assemble_conv")

    front = jnp.concatenate([jnp.zeros((PAD, d), F32), meta], axis=0)
    h0 = jnp.concatenate([jnp.broadcast_to(front[None], (nb, FIRST_FRAME, d)), x], axis=1).reshape(n, d)
    tgt_p = jnp.concatenate([jnp.zeros((nb, FIRST_FRAME, d), F32), tgt], axis=1).reshape(n, d)
    tables = _rope_tables(lp)
    zero_tail = jnp.zeros((1, HEAD_SLAB - D_QK), F32)
    gqh = jnp.concatenate([sh["q_head_norm"], zero_tail], axis=1)
    gkh = jnp.concatenate([sh["k_head_norm"], zero_tail], axis=1)
    wa = _block_diag(w["gate_a_w"][0]).astype(BF16)
    wx = _block_diag(w["gate_x_w"][0]).astype(BF16)

    (h1, u1, a1, b1), (g_in, g_uq, g_uk, g_uv, g_out) = ffn_fwd(
        h0, sh["ffn1_norm"], wg1, wu1, wd1, "ffn1_fwd",
        comm=([b16("w_in"), b16("w_uq"), b16("w_uk"), b16("w_uv"), b16("w_out")], ["gather"] * 5))
    wm, wl, wuq, wuk, wuv = assemble_mid(g_in, g_uq, g_uk, g_uv)
    w_out = g_out.reshape(d, d)

    u2, zm, zl = inproj_fwd(h1, sh["mix_norm"], wm, wl)
    q, k, vv, qn, cn = mla_prep_fwd(zm, sh["q_latent_norm"], sh["kv_latent_norm"], wuq, wuk, wuv, gqh, gkh, tables, lp)
    (y_mla, lse), (g_wg2, g_wu2, g_wd2) = attn_fwd(
        q, k, vv, nb, lp, comm=([b16("ffn2_w_gate"), b16("ffn2_w_up"), b16("ffn2_w_down")], ["gather"] * 3))
    wg2 = assemble_cols(g_wg2, "assemble_wg2")
    wu2 = assemble_cols(g_wu2, "assemble_wu2")
    wd2 = g_wd2.reshape(D_FF, d)
    y_lru, hs = lru_fwd(zl, conv_w, sh["conv_b"], wa, wx, sh["gate_a_b"], sh["gate_x_b"], sh["lru_lambda"], nb, lp)
    h2, yn = outproj_fwd(h1, y_mla, y_lru, sh["attn_out_norm"], sh["lru_out_norm"], w_out)
    (h3, u3, a3, b3), _ = ffn_fwd(h2, sh["ffn2_norm"], wg2, wu2, wd2, "ffn2_fwd")
    loss, dh3, g_final = final_loss(h3, sh["final_norm"], tgt_p, lp)

    vec = {"final_norm": g_final}
    (dh2, da3, db3, sh3, vec["ffn2_norm"]), _ = ffn_bwd_act(dh3, h2, sh["ffn2_norm"], a3, b3, wg2, wu2, wd2, "ffn2_bwd")
    dwg2 = tn_matmul(u3, da3, "ffn2_dwg", "cols")
    dwu2 = tn_matmul(u3, db3, "ffn2_dwu", "cols")
    dwd2 = tn_matmul(sh3, dh3, "ffn2_dwd", "bf16").reshape(N_DEV, D_FF // N_DEV, d)

    dy_mla, dy_lru, vec["attn_out_norm"], vec["lru_out_norm"] = outproj_bwd(
        dh2, y_mla, y_lru, sh["attn_out_norm"], sh["lru_out_norm"], w_out)
    dw_out = tn_matmul(yn, dh2, "dw_out", "bf16").reshape(N_DEV, d // N_DEV, d)
    du, dgate, dconv, vec["conv_b"], vec["gate_a_b"], vec["gate_x_b"], vec["lru_lambda"], dga, dgx = lru_bwd(
        zl, hs, dy_lru, conv_w, sh["conv_b"], wa, wx, sh["gate_a_b"], sh["gate_x_b"], sh["lru_lambda"], nb, lp)

    (dq, dk, dv), (r_wg2, r_wu2, r_wd2) = attn_bwd(q, k, vv, y_mla, dy_mla, lse, nb, lp,
                                                   comm=([dwg2, dwu2, dwd2], ["scatter"] * 3))
    update("ffn2_w_gate", r_wg2)
    update("ffn2_w_up", r_wu2)
    update("ffn2_w_down", r_wd2)

    dzm, dqr, dkr, vec["q_latent_norm"], vec["kv_latent_norm"], vec["q_head_norm"], vec["k_head_norm"] = mla_prep_bwd(
        dq, dk, dv, zm, qn, cn, sh["q_latent_norm"], sh["kv_latent_norm"], wuq, wuk, wuv, gqh, gkh, tables, lp)
    dwuq = tn_matmul(qn, dqr, "dw_uq")
    dwuk = tn_matmul(cn, dkr, "dw_uk")
    dwuv = tn_matmul(cn, dv, "dw_uv")
    dh1, vec["mix_norm"] = inproj_bwd(dzm, du, dgate, dh2, h1, sh["mix_norm"], wm, wl)
    dwm = tn_matmul(u2, dzm, "dw_in_mla")
    dwl_u = tn_matmul(u2, du, "dw_in_u")
    dwl_g = tn_matmul(u2, dgate, "dw_in_gate")
    s_in, s_uq, s_uk, s_uv, s_conv = split_mid_grads(dwm, dwl_u, dwl_g, dwuq, dwuk, dwuv, dconv)

    (dh0, da1, db1, sh1, vec["ffn1_norm"]), landed = ffn_bwd_act(
        dh1, h0, sh["ffn1_norm"], a1, b1, wg1, wu1, wd1, "ffn1_bwd",
        comm=([s_in, s_uq, s_uk, s_uv, dw_out, s_conv], ["scatter"] * 6))
    for name, r in zip(("w_in", "w_uq", "w_uk", "w_uv", "w_out", "conv_w"), landed):
        update(name, r)

    dwg1 = tn_matmul(u1, da1, "ffn1_dwg", "cols")
    dwu1 = tn_matmul(u1, db1, "ffn1_dwu", "cols")
    dwd1 = tn_matmul(sh1, dh1, "ffn1_dwd", "bf16").reshape(N_DEV, D_FF // N_DEV, d)
    dmeta = meta_grad(dh0, nb, lp)
    gates = [dga.reshape(LRU_WIDTH, LRU_BLOCK), dgx.reshape(LRU_WIDTH, LRU_BLOCK)]
    r_wg1, r_wu1, r_wd1, r_meta, r_vec, r_ga, r_gx = exchange(
        [dwg1, dwu1, dwd1, dmeta, pack_vectors(vec)] + gates, ["scatter"] * 4 + ["gather"] * 3, "exchange_last")
    update("ffn1_w_gate", r_wg1)
    update("ffn1_w_up", r_wu1)
    update("ffn1_w_down", r_wd1)
    update("meta_tokens", r_meta)

    small = [name for name, _ in VECTORS] + GATES
    res = adamw_small(r_vec, [r_ga, r_gx], [sh[nm] for nm in small], [m2[nm] for nm in small], [v2[nm] for nm in small])
    for i, name in enumerate(small):
        out[name] = [res[j][i] for j in range(4)]

    grad_x = dh0.reshape(nb, lp, d)[:, FIRST_FRAME:]
    loss = lax.psum(loss[0, 0], ("x", "y", "c"))
    cols = [[out[name][j].reshape(w[name].shape) for name in _WEIGHT_NAMES] for j in range(4)]
    return (loss, grad_x, *cols[0], *cols[1], *cols[2], *cols[3])


def kernel(x, meta_tokens, ffn1_norm, ffn1_w_gate, ffn1_w_up, ffn1_w_down, mix_norm, w_in, q_latent_norm, w_uq, kv_latent_norm, w_uk, w_uv, q_head_norm, k_head_norm, conv_w, conv_b, gate_a_w, gate_a_b, gate_x_w, gate_x_b, lru_lambda, attn_out_norm, lru_out_norm, w_out, ffn2_norm, ffn2_w_gate, ffn2_w_up, ffn2_w_down, final_norm, loss_target, m_meta_tokens, m_ffn1_norm, m_ffn1_w_gate, m_ffn1_w_up, m_ffn1_w_down, m_mix_norm, m_w_in, m_q_latent_norm, m_w_uq, m_kv_latent_norm, m_w_uk, m_w_uv, m_q_head_norm, m_k_head_norm, m_conv_w, m_conv_b, m_gate_a_w, m_gate_a_b, m_gate_x_w, m_gate_x_b, m_lru_lambda, m_attn_out_norm, m_lru_out_norm, m_w_out, m_ffn2_norm, m_ffn2_w_gate, m_ffn2_w_up, m_ffn2_w_down, m_final_norm, v_meta_tokens, v_ffn1_norm, v_ffn1_w_gate, v_ffn1_w_up, v_ffn1_w_down, v_mix_norm, v_w_in, v_q_latent_norm, v_w_uq, v_kv_latent_norm, v_w_uk, v_w_uv, v_q_head_norm, v_k_head_norm, v_conv_w, v_conv_b, v_gate_a_w, v_gate_a_b, v_gate_x_w, v_gate_x_b, v_lru_lambda, v_attn_out_norm, v_lru_out_norm, v_w_out, v_ffn2_norm, v_ffn2_w_gate, v_ffn2_w_up, v_ffn2_w_down, v_final_norm):
    args = locals()
    w = {name: args[name] for name in _WEIGHT_NAMES}
    m = {name: args["m_" + name] for name in _WEIGHT_NAMES}
    v = {name: args["v_" + name] for name in _WEIGHT_NAMES}
    return train_step(x, loss_target, w, m, v)
```

```python
import math

import jax
import jax.numpy as jnp
from jax import lax
from jax.experimental import pallas as pl
from jax.experimental.pallas import tpu as pltpu

F32 = jnp.float32
BF16 = jnp.bfloat16

D_MODEL = 1024
CHUNK = 64
CHUNK_SHIFT = 6
N_META = 16
PAD = CHUNK - N_META
FIRST_FRAME = PAD + N_META
MLA_HEADS = 4
D_NOPE = 128
D_ROPE = 64
D_QK = D_NOPE + D_ROPE
D_V = 128
HEAD_SLAB = 256
KV_RANK = 256
Q_RANK = 384
ROPE_THETA = 10000.0
LRU_WIDTH = 512
LRU_BLOCKS = 8
LRU_BLOCK = 64
LRU_TILE = 128
CONV_W = 4
C_RGLRU = 8.0
D_FF = 2816
MLA_IN = 768
EPS = 1e-6
NEG_INF = -1e30
N_DEV = 8
LANES = 128
VMEM_LIMIT = 52 * 1024 * 1024

ADAM_LR = 0.001
ADAM_B1 = 0.9
ADAM_B2 = 0.999
ADAM_EPS = 1e-08
ADAM_WD = 0.01
ADAM_STEP = 10

VMEM_WHOLE = pl.BlockSpec(memory_space=pltpu.VMEM)
HBM_WHOLE = pl.BlockSpec(memory_space=pl.ANY)


def _params(sems):
    if sems is None:
        return pltpu.CompilerParams(vmem_limit_bytes=VMEM_LIMIT)
    return pltpu.CompilerParams(dimension_semantics=sems, vmem_limit_bytes=VMEM_LIMIT)


def _tile(n, cap, mult=16):
    best = None
    for t in range(mult, min(n, cap) + 1, mult):
        if n % t == 0:
            best = t
    assert best is not None, (n, cap, mult)
    return best


def _row(tm, d):
    return pl.BlockSpec((tm, d), lambda i: (i, 0))


def _fixed(shape):
    return pl.BlockSpec(shape, lambda i: (0,) * len(shape))


def _mesh_position():
    return lax.axis_index("x"), lax.axis_index("y"), lax.axis_index("c")


def _flat_index(x, y, c):
    return 4 * x + 2 * y + c


def _peers(x, y, c):
    out = []
    for k in range(1, N_DEV):
        fx, fy, fc = (k >> 2) & 1, (k >> 1) & 1, k & 1
        out.append((1 - x if fx else x, 1 - y if fy else y, 1 - c if fc else c))
    return out


def _comm_out_shapes(srcs, modes):
    return [jax.ShapeDtypeStruct((N_DEV,) + s.shape if md == "gather" else s.shape, s.dtype)
            for s, md in zip(srcs, modes)]


def _comm_scratch(n):
    per_peer = n * (N_DEV - 1)
    return [pltpu.SemaphoreType.DMA((per_peer,)), pltpu.SemaphoreType.DMA((per_peer,)), pltpu.SemaphoreType.DMA((n,))]


class _Copies:
    def __init__(self, own, first, relay):
        self.own, self.first, self.relay = own, first, relay

    def start(self):
        for cp in self.own + self.first:
            cp.start()

    def forward(self):
        for arrival, onward in self.relay:
            arrival.wait_recv()
            onward.start()

    def finish(self):
        arrivals = [a for a, _ in self.relay]
        onward = [f for _, f in self.relay]
        for cp in self.first + onward:
            if not any(cp is a for a in arrivals):
                cp.wait_recv()
        for cp in self.first + onward:
            cp.wait_send()
        for cp in self.own:
            cp.wait()


def _comm_copies(src_refs, dst_refs, modes, send, recv, local):
    x, y, c = _mesh_position()
    me = _flat_index(x, y, c)
    n = len(modes)
    sibling = (x, y, 1 - c)
    chips = [(1 - x, y), (x, 1 - y), (1 - x, 1 - y)]

    def remote(src, dst, k, t, to):
        return pltpu.make_async_remote_copy(src_ref=src, dst_ref=dst, send_sem=send.at[k * n + t],
                                            recv_sem=recv.at[k * n + t], device_id=to,
                                            device_id_type=pl.DeviceIdType.MESH)

    own, first, relay = [], [], []
    for t, (src, dst, md) in enumerate(zip(src_refs, dst_refs, modes)):
        if md == "scatter":
            own.append(pltpu.make_async_copy(src.at[me], dst.at[me], local.at[t]))
            for k, peer in enumerate(_peers(x, y, c)):
                first.append(remote(src.at[_flat_index(*peer)], dst.at[me], k, t, peer))
        else:
            own.append(pltpu.make_async_copy(src, dst.at[me], local.at[t]))
            first.append(remote(src, dst.at[me], 0, t, sibling))
            for j, chip in enumerate(chips):
                arrival = remote(src, dst.at[me], 1 + j, t, (*chip, c))
                landed = dst.at[_flat_index(*chip, c)]
                first.append(arrival)
                relay.append((arrival, remote(landed, landed, 4 + j, t, sibling)))
    return _Copies(own, first, relay)


def _hosted(body, n_in, n_out, modes, grid):
    t = len(modes)
    total = math.prod(grid)

    def wrapped(*refs):
        ins, csrc = refs[:n_in], refs[n_in:n_in + t]
        outs = refs[n_in + t:n_in + t + n_out]
        cdst = refs[n_in + t + n_out:n_in + 2 * t + n_out]
        scratch = refs[n_in + 2 * t + n_out:-3]
        copies = _comm_copies(csrc, cdst, modes, *refs[-3:])
        step = pl.program_id(0)
        for axis in range(1, len(grid)):
            step = step * grid[axis] + pl.program_id(axis)

        @pl.when(step == 0)
        def _():
            copies.start()

        body(*ins, *outs, *scratch)

        @pl.when(step == (total * 3) // 5)
        def _():
            copies.forward()

        @pl.when(step == total - 1)
        def _():
            copies.finish()

    return wrapped


def _call(body, name, grid, in_specs, out_specs, out_shape, sems, args, scratch=(), comm=None):
    if comm is None:
        outs = pl.pallas_call(body, name=name, grid=grid, in_specs=in_specs, out_specs=out_specs, out_shape=out_shape,
                              scratch_shapes=list(scratch), compiler_params=_params(sems))(*args)
        return outs, []
    srcs, modes = comm
    n = len(modes)
    res = pl.pallas_call(
        _hosted(body, len(in_specs), len(out_specs), modes, grid), name=name, grid=grid,
        in_specs=list(in_specs) + [HBM_WHOLE] * n, out_specs=list(out_specs) + [HBM_WHOLE] * n,
        out_shape=list(out_shape) + _comm_out_shapes(srcs, modes),
        scratch_shapes=list(scratch) + _comm_scratch(n),
        compiler_params=_params(("arbitrary",) * len(grid)))(*args, *srcs)
    return res[:len(out_specs)], res[len(out_specs):]


def exchange(srcs, modes, name):
    n = len(modes)

    def body(*refs):
        copies = _comm_copies(refs[:n], refs[n:2 * n], modes, *refs[2 * n:])
        copies.start()
        copies.forward()
        copies.finish()

    return pl.pallas_call(body, name=name, in_specs=[HBM_WHOLE] * n, out_specs=[HBM_WHOLE] * n,
                          out_shape=_comm_out_shapes(srcs, modes), scratch_shapes=_comm_scratch(n))(*srcs)


def _nn(a, b):
    return jnp.dot(a, b, preferred_element_type=F32)


def _nt(a, b):
    return lax.dot_general(a, b, (((1,), (1,)), ((), ())), preferred_element_type=F32)


def _tn(a, b):
    return lax.dot_general(a, b, (((0,), (0,)), ((), ())), preferred_element_type=F32)


def _sig(x):
    return 1.0 / (1.0 + jnp.exp(-x))


def _rms_r(x, n=None):
    n = x.shape[-1] if n is None else n
    return lax.rsqrt(jnp.sum(x * x, axis=-1, keepdims=True) * (1.0 / n) + EPS)


def _rms_bwd(x, r, g, dy, n=None):
    n = x.shape[-1] if n is None else n
    xhat = x * r
    dxhat = dy * g
    dx = r * (dxhat - xhat * (jnp.sum(dxhat * xhat, axis=-1, keepdims=True) * (1.0 / n)))
    return dx, jnp.sum(dy * xhat, axis=0, keepdims=True)


def _accumulate(ref, val, first):
    @pl.when(first)
    def _():
        ref[...] = val

    @pl.when(jnp.logical_not(first))
    def _():
        ref[...] += val


_GELU_C = math.sqrt(2.0 / math.pi)


def _gelu_and_grad(x):
    inner = _GELU_C * (x + 0.044715 * x * x * x)
    t = jnp.tanh(inner)
    gelu = 0.5 * x * (1.0 + t)
    dgelu = 0.5 * (1.0 + t) + 0.5 * x * (1.0 - t * t) * _GELU_C * (1.0 + 3.0 * 0.044715 * x * x)
    return gelu, dgelu


def _log1p_small(t):
    return jnp.where(t < 1e-3, t * (1.0 - t * (0.5 - t * (1.0 / 3.0))), jnp.log(1.0 + t))


def _softplus(x):
    return jnp.maximum(x, 0.0) + _log1p_small(jnp.exp(-jnp.abs(x)))


def _neg_expm1(x):
    return jnp.where(x > -1e-2, -x * (1.0 + x * (0.5 + x * (1.0 / 6.0))), 1.0 - jnp.exp(x))


def _ff_chunks(f):
    return 2 if (f // 2) % LANES == 0 else 1


def ffn_fwd(h, g, wg, wu, wd, name, comm=None):
    n, d = h.shape
    f = wg.shape[1]
    tm = _tile(n, 352)
    nc = _ff_chunks(f)
    fc = f // nc

    def body(h_ref, g_ref, wg_ref, wu_ref, wd_ref, ho_ref, u_ref, a_ref, b_ref):
        x = h_ref[...]
        u = (x * _rms_r(x) * g_ref[...]).astype(BF16)
        acc = jnp.zeros((tm, d), F32)
        for c in range(nc):
            cols = slice(c * fc, (c + 1) * fc)
            a = _nn(u, wg_ref[:, cols])
            b = _nn(u, wu_ref[:, cols])
            s = (a * _sig(a) * b).astype(BF16)
            acc = acc + _nn(s, wd_ref[cols, :])
            a_ref[:, cols] = a.astype(BF16)
            b_ref[:, cols] = b.astype(BF16)
        ho_ref[...] = x + 0.5 * acc
        u_ref[...] = u

    return _call(
        body, name, (n // tm,),
        [_row(tm, d), _fixed((1, d)), VMEM_WHOLE, VMEM_WHOLE, VMEM_WHOLE],
        [_row(tm, d), _row(tm, d), _row(tm, f), _row(tm, f)],
        [jax.ShapeDtypeStruct((n, d), F32), jax.ShapeDtypeStruct((n, d), BF16),
         jax.ShapeDtypeStruct((n, f), BF16), jax.ShapeDtypeStruct((n, f), BF16)],
        ("parallel",), (h, g, wg, wu, wd), comm=comm)


def ffn_bwd_act(dh, h, g, a, b, wg, wu, wd, name, comm=None):
    n, d = h.shape
    f = wg.shape[1]
    tm = _tile(n, 192)
    nc = _ff_chunks(f)
    fc = f // nc

    def body(dh_ref, h_ref, g_ref, a_ref, b_ref, wg_ref, wu_ref, wd_ref,
             dhi_ref, da_ref, db_ref, sh_ref, dg_ref):
        x = h_ref[...]
        dy = dh_ref[...]
        r = _rms_r(x)
        dhh = (0.5 * dy).astype(BF16)
        du = jnp.zeros((tm, d), F32)
        for c in range(nc):
            cols = slice(c * fc, (c + 1) * fc)
            ds = _nt(dhh, wd_ref[cols, :])
            av = a_ref[:, cols].astype(F32)
            bv = b_ref[:, cols].astype(F32)
            sg = _sig(av)
            sil = av * sg
            da = (ds * bv * (sg * (1.0 + av * (1.0 - sg)))).astype(BF16)
            db = (ds * sil).astype(BF16)
            da_ref[:, cols] = da
            db_ref[:, cols] = db
            sh_ref[:, cols] = (0.5 * sil * bv).astype(BF16)
            du = du + _nt(da, wg_ref[:, cols]) + _nt(db, wu_ref[:, cols])
        dx, dg = _rms_bwd(x, r, g_ref[...], du)
        dhi_ref[...] = dy + dx
        _accumulate(dg_ref, dg, pl.program_id(0) == 0)

    return _call(
        body, name, (n // tm,),
        [_row(tm, d), _row(tm, d), _fixed((1, d)), _row(tm, f), _row(tm, f), VMEM_WHOLE, VMEM_WHOLE, VMEM_WHOLE],
        [_row(tm, d), _row(tm, f), _row(tm, f), _row(tm, f), _fixed((1, d))],
        [jax.ShapeDtypeStruct((n, d), F32), jax.ShapeDtypeStruct((n, f), BF16),
         jax.ShapeDtypeStruct((n, f), BF16), jax.ShapeDtypeStruct((n, f), BF16),
         jax.ShapeDtypeStruct((1, d), F32)],
        ("arbitrary",), (dh, h, g, a, b, wg, wu, wd), comm=comm)


def tn_matmul(x, y, name, out="f32", comm=None):
    n, k = x.shape
    m = y.shape[1]
    mc = m
    while k * mc > 1536 * 1024 and mc % (2 * LANES) == 0:
        mc //= 2
    tm = _tile(n, 704)
    ns = m // N_DEV
    per = mc // ns if out == "cols" else 0

    def body(x_ref, y_ref, o_ref, *acc):
        i = pl.program_id(1)
        part = _tn(x_ref[...].astype(BF16), y_ref[...].astype(BF16))
        if out == "f32":
            _accumulate(o_ref, part, i == 0)
            return
        _accumulate(acc[0], part, i == 0)

        @pl.when(i == n // tm - 1)
        def _():
            if out == "bf16":
                o_ref[...] = acc[0][...].astype(BF16)
            else:
                for s in range(per):
                    o_ref[s] = acc[0][:, s * ns:(s + 1) * ns].astype(BF16)

    if out == "cols":
        assert mc % ns == 0
        out_spec = pl.BlockSpec((per, k, ns), lambda j, i: (j, 0, 0))
        out_shape = jax.ShapeDtypeStruct((N_DEV, k, ns), BF16)
    else:
        out_spec = pl.BlockSpec((k, mc), lambda j, i: (0, j))
        out_shape = jax.ShapeDtypeStruct((k, m), F32 if out == "f32" else BF16)
    (res,), landed = _call(
        body, name, (m // mc, n // tm),
        [pl.BlockSpec((tm, k), lambda j, i: (i, 0)), pl.BlockSpec((tm, mc), lambda j, i: (i, j))],
        [out_spec], [out_shape], ("parallel", "arbitrary"), (x, y),
        scratch=[] if out == "f32" else [pltpu.VMEM((k, mc), F32)], comm=comm)
    return (res, landed) if comm is not None else res


def inproj_fwd(h, g, wm, wl):
    n, d = h.shape
    tm = _tile(n, 352)

    def body(h_ref, g_ref, wm_ref, wl_ref, u_ref, zm_ref, zl_ref):
        x = h_ref[...]
        u = (x * _rms_r(x) * g_ref[...]).astype(BF16)
        u_ref[...] = u
        zm_ref[...] = _nn(u, wm_ref[...])
        zl_ref[...] = _nn(u, wl_ref[...])

    return pl.pallas_call(
        body, name="inproj_fwd", grid=(n // tm,),
        in_specs=[_row(tm, d), _fixed((1, d)), VMEM_WHOLE, VMEM_WHOLE],
        out_specs=[_row(tm, d), _row(tm, MLA_IN), _row(tm, 2 * LRU_WIDTH)],
        out_shape=[jax.ShapeDtypeStruct((n, d), BF16), jax.ShapeDtypeStruct((n, MLA_IN), F32),
                   jax.ShapeDtypeStruct((n, 2 * LRU_WIDTH), F32)],
        compiler_params=_params(("parallel",)),
    )(h, g, wm, wl)


def inproj_bwd(dzm, du, dgate, dh2, h, g, wm, wl):
    n, d = h.shape
    tm = _tile(n, 352)

    def body(dzm_ref, du_ref, dgt_ref, dh2_ref, h_ref, g_ref, wm_ref, wl_ref, dh_ref, dg_ref):
        x = h_ref[...]
        dun = (_nt(dzm_ref[...].astype(BF16), wm_ref[...])
               + _nt(du_ref[...].astype(BF16), wl_ref[:, :LRU_WIDTH])
               + _nt(dgt_ref[...].astype(BF16), wl_ref[:, LRU_WIDTH:]))
        dx, dg = _rms_bwd(x, _rms_r(x), g_ref[...], dun)
        dh_ref[...] = dh2_ref[...] + dx
        _accumulate(dg_ref, dg, pl.program_id(0) == 0)

    return pl.pallas_call(
        body, name="inproj_bwd", grid=(n // tm,),
        in_specs=[_row(tm, MLA_IN), _row(tm, LRU_WIDTH), _row(tm, LRU_WIDTH), _row(tm, d), _row(tm, d),
                  _fixed((1, d)), VMEM_WHOLE, VMEM_WHOLE],
        out_specs=[_row(tm, d), _fixed((1, d))],
        out_shape=[jax.ShapeDtypeStruct((n, d), F32), jax.ShapeDtypeStruct((1, d), F32)],
        compiler_params=_params(("arbitrary",)),
    )(dzm, du, dgate, dh2, h, g, wm, wl)


def _rope_tables(lp):
    pos = jnp.arange(lp, dtype=F32) - float(PAD)
    half = D_ROPE // 2
    inv_freq = ROPE_THETA ** (-jnp.arange(0, half, dtype=F32) / half)
    ang = pos[:, None] * inv_freq[None, :]
    cos, sin = jnp.cos(ang), jnp.sin(ang)
    one = jnp.ones((lp, D_NOPE), F32)
    z_nope = jnp.zeros((lp, D_NOPE), F32)
    z_half = jnp.zeros((lp, half), F32)
    z_tail = jnp.zeros((lp, HEAD_SLAB - D_QK), F32)
    cosr = jnp.concatenate([one, cos, cos, z_tail], axis=1)
    sin_up = jnp.concatenate([z_nope, z_half, sin, z_tail], axis=1)
    sin_dn = jnp.concatenate([z_nope, -sin, z_half, z_tail], axis=1)
    return cosr, sin_up, sin_dn


def _rope(x, cosr, sin_up, sin_dn):
    half = D_ROPE // 2
    return x * cosr + pltpu.roll(x, half, axis=1) * sin_up + pltpu.roll(x, HEAD_SLAB - half, axis=1) * sin_dn


def _rope_bwd(dy, cosr, sin_up, sin_dn):
    half = D_ROPE // 2
    return (dy * cosr + pltpu.roll(dy * sin_up, HEAD_SLAB - half, axis=1)
            + pltpu.roll(dy * sin_dn, half, axis=1))


def _k_rope_slab(zm_tile):
    tm = zm_tile.shape[0]
    krp = zm_tile[:, Q_RANK + KV_RANK:MLA_IN]
    return jnp.concatenate([jnp.zeros((tm, D_NOPE), F32), krp], axis=1)


def mla_prep_fwd(zm, gql, gkvl, wuq, wuk, wuv, gqh, gkh, tables, lp):
    n = zm.shape[0]
    tm = _tile(lp, 352)
    per_seq = lp // tm
    width = MLA_HEADS * HEAD_SLAB
    scale = 1.0 / math.sqrt(D_QK)

    def body(zm_ref, gql_ref, gkvl_ref, wuq_ref, wuk_ref, wuv_ref, gqh_ref, gkh_ref,
             cos_ref, up_ref, dn_ref, q_ref, k_ref, v_ref, qn_ref, cn_ref):
        z = zm_ref[...]
        cq = z[:, :Q_RANK]
        ckv = z[:, Q_RANK:Q_RANK + KV_RANK]
        qn = (cq * _rms_r(cq) * gql_ref[...]).astype(BF16)
        cn = (ckv * _rms_r(ckv) * gkvl_ref[...]).astype(BF16)
        qn_ref[...] = qn
        cn_ref[...] = cn
        q_raw = _nn(qn, wuq_ref[...])
        k_raw = _nn(cn, wuk_ref[...])
        v_ref[...] = _nn(cn, wuv_ref[...]).astype(BF16)
        kr_slab = _k_rope_slab(z)
        cosr, sin_up, sin_dn = cos_ref[...], up_ref[...], dn_ref[...]
        for hd in range(MLA_HEADS):
            cols = slice(hd * HEAD_SLAB, (hd + 1) * HEAD_SLAB)
            xq = q_raw[:, cols]
            yq = _rope(xq * _rms_r(xq, D_QK) * gqh_ref[...], cosr, sin_up, sin_dn)
            q_ref[:, cols] = (yq * scale).astype(BF16)
            xk = k_raw[:, cols] + kr_slab
            yk = _rope(xk * _rms_r(xk, D_QK) * gkh_ref[...], cosr, sin_up, sin_dn)
            k_ref[:, cols] = yk.astype(BF16)

    tab = pl.BlockSpec((tm, HEAD_SLAB), lambda i: (i % per_seq, 0))
    return pl.pallas_call(
        body, name="mla_prep_fwd", grid=(n // tm,),
        in_specs=[_row(tm, MLA_IN), _fixed((1, Q_RANK)), _fixed((1, KV_RANK)), VMEM_WHOLE, VMEM_WHOLE, VMEM_WHOLE,
                  _fixed((1, HEAD_SLAB)), _fixed((1, HEAD_SLAB)), tab, tab, tab],
        out_specs=[_row(tm, width), _row(tm, width), _row(tm, MLA_HEADS * D_V), _row(tm, Q_RANK), _row(tm, KV_RANK)],
        out_shape=[jax.ShapeDtypeStruct((n, width), BF16), jax.ShapeDtypeStruct((n, width), BF16),
                   jax.ShapeDtypeStruct((n, MLA_HEADS * D_V), BF16), jax.ShapeDtypeStruct((n, Q_RANK), BF16),
                   jax.ShapeDtypeStruct((n, KV_RANK), BF16)],
        compiler_params=_params(("parallel",)),
    )(zm, gql, gkvl, wuq, wuk, wuv, gqh, gkh, *tables)


def mla_prep_bwd(dq, dk, dv, zm, qn, cn, gql, gkvl, wuq, wuk, wuv, gqh, gkh, tables, lp):
    n = zm.shape[0]
    tm = _tile(lp, 352)
    per_seq = lp // tm
    width = MLA_HEADS * HEAD_SLAB
    scale = 1.0 / math.sqrt(D_QK)

    def body(dq_ref, dk_ref, dv_ref, zm_ref, qn_ref, cn_ref, gql_ref, gkvl_ref, wuq_ref, wuk_ref, wuv_ref,
             gqh_ref, gkh_ref, cos_ref, up_ref, dn_ref,
             dzm_ref, dqr_ref, dkr_ref, dgql_ref, dgkvl_ref, dgqh_ref, dgkh_ref):
        z = zm_ref[...]
        cq = z[:, :Q_RANK]
        ckv = z[:, Q_RANK:Q_RANK + KV_RANK]
        q_raw = _nn(qn_ref[...], wuq_ref[...])
        k_raw = _nn(cn_ref[...], wuk_ref[...])
        kr_slab = _k_rope_slab(z)
        cosr, sin_up, sin_dn = cos_ref[...], up_ref[...], dn_ref[...]
        dgq = jnp.zeros((1, HEAD_SLAB), F32)
        dgk = jnp.zeros((1, HEAD_SLAB), F32)
        dkrp = jnp.zeros((tm, HEAD_SLAB - D_NOPE), F32)
        for hd in range(MLA_HEADS):
            cols = slice(hd * HEAD_SLAB, (hd + 1) * HEAD_SLAB)
            xq = q_raw[:, cols]
            dxn = _rope_bwd(dq_ref[:, cols] * scale, cosr, sin_up, sin_dn)
            dxq, dg = _rms_bwd(xq, _rms_r(xq, D_QK), gqh_ref[...], dxn, D_QK)
            dgq = dgq + dg
            dqr_ref[:, cols] = dxq.astype(BF16)
            xk = k_raw[:, cols] + kr_slab
            dxn = _rope_bwd(dk_ref[:, cols], cosr, sin_up, sin_dn)
            dxk, dg = _rms_bwd(xk, _rms_r(xk, D_QK), gkh_ref[...], dxn, D_QK)
            dgk = dgk + dg
            dkr_ref[:, cols] = dxk.astype(BF16)
            dkrp = dkrp + dxk[:, D_NOPE:]
        dqn = _nt(dqr_ref[...], wuq_ref[...])
        dcn = _nt(dkr_ref[...], wuk_ref[...]) + _nt(dv_ref[...].astype(BF16), wuv_ref[...])
        dcq, dg1 = _rms_bwd(cq, _rms_r(cq), gql_ref[...], dqn)
        dckv, dg2 = _rms_bwd(ckv, _rms_r(ckv), gkvl_ref[...], dcn)
        dzm_ref[:, :Q_RANK] = dcq
        dzm_ref[:, Q_RANK:Q_RANK + KV_RANK] = dckv
        dzm_ref[:, Q_RANK + KV_RANK:] = dkrp
        first = pl.program_id(0) == 0
        _accumulate(dgql_ref, dg1, first)
        _accumulate(dgkvl_ref, dg2, first)
        _accumulate(dgqh_ref, dgq, first)
        _accumulate(dgkh_ref, dgk, first)

    tab = pl.BlockSpec((tm, HEAD_SLAB), lambda i: (i % per_seq, 0))
    return pl.pallas_call(
        body, name="mla_prep_bwd", grid=(n // tm,),
        in_specs=[_row(tm, width), _row(tm, width), _row(tm, MLA_HEADS * D_V), _row(tm, MLA_IN),
                  _row(tm, Q_RANK), _row(tm, KV_RANK), _fixed((1, Q_RANK)), _fixed((1, KV_RANK)),
                  VMEM_WHOLE, VMEM_WHOLE, VMEM_WHOLE, _fixed((1, HEAD_SLAB)), _fixed((1, HEAD_SLAB)), tab, tab, tab],
        out_specs=[_row(tm, MLA_IN), _row(tm, width), _row(tm, width), _fixed((1, Q_RANK)), _fixed((1, KV_RANK)),
                   _fixed((1, HEAD_SLAB)), _fixed((1, HEAD_SLAB))],
        out_shape=[jax.ShapeDtypeStruct((n, MLA_IN), F32), jax.ShapeDtypeStruct((n, width), BF16),
                   jax.ShapeDtypeStruct((n, width), BF16), jax.ShapeDtypeStruct((1, Q_RANK), F32),
                   jax.ShapeDtypeStruct((1, KV_RANK), F32), jax.ShapeDtypeStruct((1, HEAD_SLAB), F32),
                   jax.ShapeDtypeStruct((1, HEAD_SLAB), F32)],
        compiler_params=_params(("arbitrary",)),
    )(dq, dk, dv, zm, qn, cn, gql, gkvl, wuq, wuk, wuv, gqh, gkh, *tables)


def _attn_tile(lp):
    return _tile(lp, 704, CHUNK)


def _chunk_mask(i, j, t):
    qpos = i * t + lax.broadcasted_iota(jnp.int32, (t, t), 0)
    kpos = j * t + lax.broadcasted_iota(jnp.int32, (t, t), 1)
    same_or_earlier = jnp.right_shift(kpos, CHUNK_SHIFT) <= jnp.right_shift(qpos, CHUNK_SHIFT)
    return jnp.logical_and(same_or_earlier, kpos >= PAD)


def attn_fwd(q, k, v, nb, lp, comm=None):
    n = q.shape[0]
    t = _attn_tile(lp)
    nq = lp // t

    def body(q_ref, k_ref, v_ref, o_ref, lse_ref):
        i = pl.program_id(2)
        qv = q_ref[...]

        def kv_step(j, carry):
            m, l, acc = carry
            off = pl.multiple_of(j * t, t)
            s = _nt(qv, k_ref[pl.ds(off, t), :])
            s = jnp.where(_chunk_mask(i, j, t), s, NEG_INF)
            m_new = jnp.maximum(m, jnp.max(s, axis=-1, keepdims=True))
            p = jnp.exp(s - m_new)
            alpha = jnp.exp(m - m_new)
            l = alpha * l + jnp.sum(p, axis=-1, keepdims=True)
            acc = alpha * acc + _nn(p.astype(BF16), v_ref[pl.ds(off, t), :])
            return m_new, l, acc

        init = (jnp.full((t, 1), NEG_INF, F32), jnp.zeros((t, 1), F32), jnp.zeros((t, D_V), F32))
        m, l, acc = lax.fori_loop(0, i + 1, kv_step, init)
        o_ref[...] = acc * (1.0 / l)
        lse_ref[0] = jnp.broadcast_to(m + jnp.log(l), (t, LANES))

    return _call(
        body, "attn_fwd", (nb, MLA_HEADS, nq),
        [pl.BlockSpec((t, HEAD_SLAB), lambda b, h, i: (b * nq + i, h)),
         pl.BlockSpec((lp, HEAD_SLAB), lambda b, h, i: (b, h)),
         pl.BlockSpec((lp, D_V), lambda b, h, i: (b, h))],
        [pl.BlockSpec((t, D_V), lambda b, h, i: (b * nq + i, h)),
         pl.BlockSpec((1, t, LANES), lambda b, h, i: (h, b * nq + i, 0))],
        [jax.ShapeDtypeStruct((n, MLA_HEADS * D_V), F32), jax.ShapeDtypeStruct((MLA_HEADS, n, LANES), F32)],
        ("parallel", "parallel", "parallel"), (q, k, v), comm=comm)


def attn_bwd(q, k, v, o, do, lse, nb, lp, comm=None):
    n = q.shape[0]
    t = _attn_tile(lp)
    nq = lp // t

    def body(q_ref, k_ref, v_ref, o_ref, do_ref, lse_ref, dq_ref, dk_ref, dv_ref):
        dk_ref[...] = jnp.zeros_like(dk_ref)
        dv_ref[...] = jnp.zeros_like(dv_ref)

        def q_step(i, _):
            qoff = pl.multiple_of(i * t, t)
            qv = q_ref[pl.ds(qoff, t), :]
            dov = do_ref[pl.ds(qoff, t), :]
            delta = jnp.sum(o_ref[pl.ds(qoff, t), :] * dov, axis=-1, keepdims=True)
            lse_q = jnp.max(lse_ref[0, pl.ds(qoff, t), :], axis=-1, keepdims=True)
            do16 = dov.astype(BF16)

            def kv_step(j, dq_acc):
                koff = pl.multiple_of(j * t, t)
                kv = k_ref[pl.ds(koff, t), :]
                s = jnp.where(_chunk_mask(i, j, t), _nt(qv, kv), NEG_INF)
                p = jnp.exp(s - lse_q)
                dp = _nt(do16, v_ref[pl.ds(koff, t), :])
                ds16 = (p * (dp - delta)).astype(BF16)
                dv_ref[pl.ds(koff, t), :] += _tn(p.astype(BF16), do16)
                dk_ref[pl.ds(koff, t), :] += _tn(ds16, qv)
                return dq_acc + _nn(ds16, kv)

            dq_ref[pl.ds(qoff, t), :] = lax.fori_loop(0, i + 1, kv_step, jnp.zeros((t, HEAD_SLAB), F32))
            return 0

        lax.fori_loop(0, nq, q_step, 0)

    wide = pl.BlockSpec((lp, HEAD_SLAB), lambda b, h: (b, h))
    thin = pl.BlockSpec((lp, D_V), lambda b, h: (b, h))
    width = MLA_HEADS * HEAD_SLAB
    return _call(
        body, "attn_bwd", (nb, MLA_HEADS),
        [wide, wide, thin, thin, thin, pl.BlockSpec((1, lp, LANES), lambda b, h: (h, b, 0))],
        [wide, wide, thin],
        [jax.ShapeDtypeStruct((n, width), F32), jax.ShapeDtypeStruct((n, width), F32),
         jax.ShapeDtypeStruct((n, MLA_HEADS * D_V), F32)],
        ("parallel", "parallel"), (q, k, v, o, do, lse), comm=comm)


def _seq_rows(nb, lp, width):
    rows = lax.broadcasted_iota(jnp.int32, (lp, width), 0)
    return jnp.concatenate([rows] * nb, axis=0) if nb > 1 else rows


def _lru_gates(u, w_ref, cb, wa, wx, ba, bx, lam):
    xc = (cb + w_ref[pl.ds(3, 1), :] * u + w_ref[pl.ds(2, 1), :] * pltpu.roll(u, 1, axis=0)
          + w_ref[pl.ds(1, 1), :] * pltpu.roll(u, 2, axis=0) + w_ref[pl.ds(0, 1), :] * pltpu.roll(u, 3, axis=0))
    xc16 = xc.astype(BF16)
    ra = _sig(_nn(xc16, wa) + ba)
    ia = _sig(_nn(xc16, wx) + bx)
    sp = _softplus(-lam)
    log_a = -C_RGLRU * ra * sp
    a = jnp.exp(log_a)
    mult = jnp.sqrt(_neg_expm1(2.0 * log_a))
    return xc, xc16, ra, ia, sp, a, mult


def _scan_block_rows(width):
    return lax.broadcasted_iota(jnp.int32, (8, width), 0)


def lru_fwd(zl, conv_w, conv_b, wa, wx, ba, bx, lam, nb, lp):
    n = zl.shape[0]
    w = LRU_TILE
    nt = LRU_WIDTH // w
    nblk = lp // 8

    def body(u_ref, gt_ref, cw_ref, cb_ref, wa_ref, wx_ref, ba_ref, bx_ref, lam_ref, y_ref, h_ref, a_s, b_s):
        u = u_ref[...]
        xc, _, _, ia, _, a, mult = _lru_gates(u, cw_ref, cb_ref[...], wa_ref[...], wx_ref[...],
                                              ba_ref[...], bx_ref[...], lam_ref[...])
        row = _seq_rows(nb, lp, w)
        mult = jnp.where(row == PAD, 1.0, mult)
        a_s[...] = a
        b_s[...] = jnp.where(row < PAD, 0.0, mult * (ia * xc))
        r8 = _scan_block_rows(w)

        def blk(i, carry):
            out = []
            for s_id in range(nb):
                off = pl.multiple_of(s_id * lp + i * 8, 8)
                av = a_s[pl.ds(off, 8), :]
                bv = b_s[pl.ds(off, 8), :]
                for sh in (1, 2, 4):
                    keep = r8 >= sh
                    bv = jnp.where(keep, av * pltpu.roll(bv, sh, axis=0) + bv, bv)
                    av = jnp.where(keep, av * pltpu.roll(av, sh, axis=0), av)
                hv = bv + av * carry[s_id]
                h_ref[pl.ds(off, 8), :] = hv
                out.append(jnp.sum(jnp.where(r8 == 7, hv, 0.0), axis=0, keepdims=True))
            return tuple(out)

        lax.fori_loop(0, nblk, blk, tuple(jnp.zeros((1, w), F32) for _ in range(nb)))
        gelu, _ = _gelu_and_grad(gt_ref[...])
        y_ref[...] = h_ref[...] * gelu

    col = lambda c: (0, c)
    return pl.pallas_call(
        body, name="lru_fwd", grid=(nt,),
        in_specs=[pl.BlockSpec((n, w), col), pl.BlockSpec((n, w), lambda c: (0, nt + c)),
                  pl.BlockSpec((CONV_W, w), col), pl.BlockSpec((1, w), col),
                  pl.BlockSpec((w, w), lambda c: (c, c)), pl.BlockSpec((w, w), lambda c: (c, c)),
                  pl.BlockSpec((1, w), col), pl.BlockSpec((1, w), col), pl.BlockSpec((1, w), col)],
        out_specs=[pl.BlockSpec((n, w), col), pl.BlockSpec((n, w), col)],
        out_shape=[jax.ShapeDtypeStruct((n, LRU_WIDTH), F32), jax.ShapeDtypeStruct((n, LRU_WIDTH), F32)],
        scratch_shapes=[pltpu.VMEM((n, w), F32), pltpu.VMEM((n, w), F32)],
        compiler_params=_params(("parallel",)),
    )(zl, zl, conv_w, conv_b, wa, wx, ba, bx, lam)


def lru_bwd(zl, hs, dy, conv_w, conv_b, wa, wx, ba, bx, lam, nb, lp):
    n = zl.shape[0]
    w = LRU_TILE
    nt = LRU_WIDTH // w
    nblk = lp // 8

    def body(u_ref, gt_ref, h_ref, dy_ref, cw_ref, cb_ref, wa_ref, wx_ref, ba_ref, bx_ref, lam_ref,
             du_ref, dgt_ref, dcw_ref, dcb_ref, dba_ref, dbx_ref, dlam_ref, dwa_ref, dwx_ref,
             c_s, d_s, g_s, dwa_s, dwx_s):
        u = u_ref[...]
        lam = lam_ref[...]
        xc, xc16, ra, ia, sp, a, mult = _lru_gates(u, cw_ref, cb_ref[...], wa_ref[...], wx_ref[...],
                                                   ba_ref[...], bx_ref[...], lam)
        row = lax.broadcasted_iota(jnp.int32, (lp, w), 0)
        hv = h_ref[...]
        dyv = dy_ref[...]
        gelu, dgelu = _gelu_and_grad(gt_ref[...])
        dgt_ref[...] = jnp.where(row >= PAD, dyv * hv * dgelu, 0.0)
        c_s[...] = pltpu.roll(a, lp - 1, axis=0)
        d_s[...] = dyv * gelu
        r8 = _scan_block_rows(w)

        def blk(ii, carry):
            off = pl.multiple_of((nblk - 1 - ii) * 8, 8)
            cv = c_s[pl.ds(off, 8), :]
            dv = d_s[pl.ds(off, 8), :]
            for sh in (1, 2, 4):
                keep = r8 < 8 - sh
                dv = jnp.where(keep, cv * pltpu.roll(dv, 8 - sh, axis=0) + dv, dv)
                cv = jnp.where(keep, cv * pltpu.roll(cv, 8 - sh, axis=0), cv)
            gv = dv + cv * carry
            g_s[pl.ds(off, 8), :] = gv
            return jnp.sum(jnp.where(r8 == 0, gv, 0.0), axis=0, keepdims=True)

        lax.fori_loop(0, nblk, blk, jnp.zeros((1, w), F32))
        gv = g_s[...]
        first_row = row == PAD
        db = jnp.where(row >= PAD, gv, 0.0)
        da = jnp.where(row > PAD, gv * pltpu.roll(hv, 1, axis=0), 0.0)
        mult_eff = jnp.where(first_row, 1.0, mult)
        dmult = jnp.where(first_row, 0.0, db * (ia * xc))
        dia = db * mult_eff * xc
        dxc = db * mult_eff * ia
        dla = da * a - dmult * (a * a) / mult
        dra = dla * (-C_RGLRU * sp)
        dsp = jnp.sum(dla * (-C_RGLRU * ra), axis=0, keepdims=True)
        dpa = dra * ra * (1.0 - ra)
        dpx = dia * ia * (1.0 - ia)
        dpa16 = dpa.astype(BF16)
        dpx16 = dpx.astype(BF16)
        dxc = dxc + _nt(dpa16, wa_ref[...]) + _nt(dpx16, wx_ref[...])
        du = cw_ref[pl.ds(CONV_W - 1, 1), :] * dxc
        dcw = [jnp.sum(dxc * u, axis=0, keepdims=True)]
        for tap in range(1, CONV_W):
            dcw.insert(0, jnp.sum(dxc * pltpu.roll(u, tap, axis=0), axis=0, keepdims=True))
            du = du + cw_ref[pl.ds(CONV_W - 1 - tap, 1), :] * pltpu.roll(dxc, lp - tap, axis=0)
        du_ref[...] = jnp.where(row >= PAD, du, 0.0)
        first = pl.program_id(1) == 0
        _accumulate(dlam_ref, -_sig(-lam) * dsp, first)
        _accumulate(dba_ref, jnp.sum(dpa, axis=0, keepdims=True), first)
        _accumulate(dbx_ref, jnp.sum(dpx, axis=0, keepdims=True), first)
        _accumulate(dcb_ref, jnp.sum(dxc, axis=0, keepdims=True), first)
        _accumulate(dcw_ref, jnp.concatenate(dcw, axis=0), first)
        _accumulate(dwa_s, _tn(xc16, dpa16), first)
        _accumulate(dwx_s, _tn(xc16, dpx16), first)

        @pl.when(pl.program_id(1) == nb - 1)
        def _():
            for j in range(w // LRU_BLOCK):
                blk_rows = slice(j * LRU_BLOCK, (j + 1) * LRU_BLOCK)
                dwa_ref[0, blk_rows, :] = dwa_s[blk_rows, blk_rows]
                dwx_ref[0, blk_rows, :] = dwx_s[blk_rows, blk_rows]

    col = lambda c, b: (0, c)
    vec = pl.BlockSpec((1, w), col)
    mat = pl.BlockSpec((w, w), lambda c, b: (c, c))
    big = pl.BlockSpec((lp, w), lambda c, b: (b, c))
    dmat = pl.BlockSpec((1, w, LRU_BLOCK), lambda c, b: (c, 0, 0))
    return pl.pallas_call(
        body, name="lru_bwd", grid=(nt, nb),
        in_specs=[big, pl.BlockSpec((lp, w), lambda c, b: (b, nt + c)), big, big,
                  pl.BlockSpec((CONV_W, w), col), vec, mat, mat, vec, vec, vec],
        out_specs=[big, big, pl.BlockSpec((CONV_W, w), col), vec, vec, vec, vec, dmat, dmat],
        out_shape=[jax.ShapeDtypeStruct((n, LRU_WIDTH), F32), jax.ShapeDtypeStruct((n, LRU_WIDTH), F32),
                   jax.ShapeDtypeStruct((CONV_W, LRU_WIDTH), F32), jax.ShapeDtypeStruct((1, LRU_WIDTH), F32),
                   jax.ShapeDtypeStruct((1, LRU_WIDTH), F32), jax.ShapeDtypeStruct((1, LRU_WIDTH), F32),
                   jax.ShapeDtypeStruct((1, LRU_WIDTH), F32), jax.ShapeDtypeStruct((nt, w, LRU_BLOCK), F32),
                   jax.ShapeDtypeStruct((nt, w, LRU_BLOCK), F32)],
        scratch_shapes=[pltpu.VMEM((lp, w), F32), pltpu.VMEM((lp, w), F32), pltpu.VMEM((lp, w), F32),
                        pltpu.VMEM((w, w), F32), pltpu.VMEM((w, w), F32)],
        compiler_params=_params(("parallel", "arbitrary")),
    )(zl, zl, hs, dy, conv_w, conv_b, wa, wx, ba, bx, lam)


def outproj_fwd(h, ya, yl, gao, glo, wout):
    n, d = h.shape
    half = ya.shape[1]
    tm = _tile(n, 352)

    def body(h_ref, ya_ref, yl_ref, gao_ref, glo_ref, w_ref, ho_ref, yn_ref):
        xa = ya_ref[...]
        xl = yl_ref[...]
        na = (xa * _rms_r(xa) * gao_ref[...]).astype(BF16)
        nl = (xl * _rms_r(xl) * glo_ref[...]).astype(BF16)
        yn_ref[:, :half] = na
        yn_ref[:, half:] = nl
        ho_ref[...] = h_ref[...] + _nn(na, w_ref[:half, :]) + _nn(nl, w_ref[half:, :])

    return pl.pallas_call(
        body, name="outproj_fwd", grid=(n // tm,),
        in_specs=[_row(tm, d), _row(tm, half), _row(tm, half), _fixed((1, half)), _fixed((1, half)), VMEM_WHOLE],
        out_specs=[_row(tm, d), _row(tm, 2 * half)],
        out_shape=[jax.ShapeDtypeStruct((n, d), F32), jax.ShapeDtypeStruct((n, 2 * half), BF16)],
        compiler_params=_params(("parallel",)),
    )(h, ya, yl, gao, glo, wout)


def outproj_bwd(dh, ya, yl, gao, glo, wout):
    n, d = dh.shape
    half = ya.shape[1]
    tm = _tile(n, 352)

    def body(dh_ref, ya_ref, yl_ref, gao_ref, glo_ref, w_ref, dya_ref, dyl_ref, dgao_ref, dglo_ref):
        d16 = dh_ref[...].astype(BF16)
        xa = ya_ref[...]
        xl = yl_ref[...]
        dxa, dga = _rms_bwd(xa, _rms_r(xa), gao_ref[...], _nt(d16, w_ref[:half, :]))
        dxl, dgl = _rms_bwd(xl, _rms_r(xl), glo_ref[...], _nt(d16, w_ref[half:, :]))
        dya_ref[...] = dxa
        dyl_ref[...] = dxl
        first = pl.program_id(0) == 0
        _accumulate(dgao_ref, dga, first)
        _accumulate(dglo_ref, dgl, first)

    return pl.pallas_call(
        body, name="outproj_bwd", grid=(n // tm,),
        in_specs=[_row(tm, d), _row(tm, half), _row(tm, half), _fixed((1, half)), _fixed((1, half)), VMEM_WHOLE],
        out_specs=[_row(tm, half), _row(tm, half), _fixed((1, half)), _fixed((1, half))],
        out_shape=[jax.ShapeDtypeStruct((n, half), F32), jax.ShapeDtypeStruct((n, half), F32),
                   jax.ShapeDtypeStruct((1, half), F32), jax.ShapeDtypeStruct((1, half), F32)],
        compiler_params=_params(("arbitrary",)),
    )(dh, ya, yl, gao, glo, wout)


def final_loss(h, g, tgt, lp):
    n, d = h.shape
    tm = _tile(lp, 352)
    per_seq = lp // tm

    def body(h_ref, g_ref, t_ref, loss_ref, dh_ref, dg_ref):
        i = pl.program_id(0)
        x = h_ref[...]
        gv = g_ref[...]
        r = _rms_r(x)
        row = (i % per_seq) * tm + lax.broadcasted_iota(jnp.int32, (tm, d), 0)
        diff = jnp.where(row >= FIRST_FRAME, x * r * gv - t_ref[...], 0.0)
        part = 0.5 * jnp.sum(jnp.sum(diff * diff, axis=-1, keepdims=True) * (1.0 / d), axis=0, keepdims=True)
        dx, dg = _rms_bwd(x, r, gv, diff * (1.0 / d))
        dh_ref[...] = dx
        _accumulate(loss_ref, jnp.broadcast_to(part, (1, LANES)), i == 0)
        _accumulate(dg_ref, dg, i == 0)

    return pl.pallas_call(
        body, name="final_loss", grid=(n // tm,),
        in_specs=[_row(tm, d), _fixed((1, d)), _row(tm, d)],
        out_specs=[_fixed((1, LANES)), _row(tm, d), _fixed((1, d))],
        out_shape=[jax.ShapeDtypeStruct((1, LANES), F32), jax.ShapeDtypeStruct((n, d), F32),
                   jax.ShapeDtypeStruct((1, d), F32)],
        compiler_params=_params(("arbitrary",)),
    )(h, g, tgt)


def assemble_cols(g, name):
    _, k, ns = g.shape

    def body(g_ref, o_ref):
        for j in range(N_DEV):
            o_ref[:, j * ns:(j + 1) * ns] = g_ref[j]

    return pl.pallas_call(body, name=name, out_shape=jax.ShapeDtypeStruct((k, N_DEV * ns), g.dtype),
                          compiler_params=_params(None))(g)


def assemble_mid(g_in, g_uq, g_uk, g_uv):
    d = g_in.shape[1]
    n_in = g_in.shape[2]
    mla_cols = MLA_IN - D_ROPE
    width = MLA_HEADS * HEAD_SLAB

    def body(in_ref, uq_ref, uk_ref, uv_ref, wm_ref, wl_ref, wuq_ref, wuk_ref, wuv_ref):
        wm_ref[:, mla_cols:] = jnp.zeros((d, D_ROPE), BF16)
        for j in range(N_DEV):
            lo, hi = j * n_in, (j + 1) * n_in
            if hi <= mla_cols:
                wm_ref[:, lo:hi] = in_ref[j]
            elif lo >= mla_cols:
                wl_ref[:, lo - mla_cols:hi - mla_cols] = in_ref[j]
            else:
                cut = mla_cols - lo
                wm_ref[:, lo:mla_cols] = in_ref[j, :, :cut]
                wl_ref[:, :hi - mla_cols] = in_ref[j, :, cut:]
        wuq_ref[...] = jnp.zeros_like(wuq_ref)
        wuk_ref[...] = jnp.zeros_like(wuk_ref)
        for j in range(N_DEV):
            for src, dst, per_head in ((uq_ref, wuq_ref, D_QK), (uk_ref, wuk_ref, D_NOPE)):
                ns = src.shape[2]
                head, within = divmod(j * ns, per_head)
                dst[:, head * HEAD_SLAB + within:head * HEAD_SLAB + within + ns] = src[j]
            ns = uv_ref.shape[2]
            wuv_ref[:, j * ns:(j + 1) * ns] = uv_ref[j]

    return pl.pallas_call(
        body, name="assemble_mid",
        out_shape=[jax.ShapeDtypeStruct((d, MLA_IN), BF16), jax.ShapeDtypeStruct((d, 2 * LRU_WIDTH), BF16),
                   jax.ShapeDtypeStruct((Q_RANK, width), BF16), jax.ShapeDtypeStruct((KV_RANK, width), BF16),
                   jax.ShapeDtypeStruct((KV_RANK, MLA_HEADS * D_V), BF16)],
        compiler_params=_params(None))(g_in, g_uq, g_uk, g_uv)


def split_mid_grads(dwm, dwl_u, dwl_g, dwuq, dwuk, dwuv, dconv):
    d = dwm.shape[0]
    mla_cols = MLA_IN - D_ROPE
    n_in = (mla_cols + 2 * LRU_WIDTH) // N_DEV
    n_uq, n_uk, n_uv = MLA_HEADS * D_QK // N_DEV, MLA_HEADS * D_NOPE // N_DEV, MLA_HEADS * D_V // N_DEV
    n_conv = LRU_WIDTH // N_DEV

    def body(dwm_ref, dwu_ref, dwg_ref, dwuq_ref, dwuk_ref, dwuv_ref, dc_ref, in_ref, uq_ref, uk_ref, uv_ref, cv_ref):
        def w_in_cols(lo, hi):
            parts = []
            for ref, start, size in ((dwm_ref, 0, mla_cols), (dwu_ref, mla_cols, LRU_WIDTH),
                                     (dwg_ref, mla_cols + LRU_WIDTH, LRU_WIDTH)):
                a, b = max(lo, start), min(hi, start + size)
                if a < b:
                    parts.append(ref[:, a - start:b - start])
            return parts

        for j in range(N_DEV):
            off = 0
            for part in w_in_cols(j * n_in, (j + 1) * n_in):
                in_ref[j, :, off:off + part.shape[1]] = part.astype(BF16)
                off += part.shape[1]
            for src, dst, per_head, ns in ((dwuq_ref, uq_ref, D_QK, n_uq), (dwuk_ref, uk_ref, D_NOPE, n_uk)):
                head, within = divmod(j * ns, per_head)
                dst[j] = src[:, head * HEAD_SLAB + within:head * HEAD_SLAB + within + ns].astype(BF16)
            uv_ref[j] = dwuv_ref[:, j * n_uv:(j + 1) * n_uv].astype(BF16)
            cv_ref[j] = dc_ref[:, j * n_conv:(j + 1) * n_conv]

    return pl.pallas_call(
        body, name="split_mid_grads",
        out_shape=[jax.ShapeDtypeStruct((N_DEV, d, n_in), BF16), jax.ShapeDtypeStruct((N_DEV, Q_RANK, n_uq), BF16),
                   jax.ShapeDtypeStruct((N_DEV, KV_RANK, n_uk), BF16), jax.ShapeDtypeStruct((N_DEV, KV_RANK, n_uv), BF16),
                   jax.ShapeDtypeStruct((N_DEV, CONV_W, n_conv), F32)],
        compiler_params=_params(None))(dwm, dwl_u, dwl_g, dwuq, dwuk, dwuv, dconv)


def meta_grad(dh0, nb, lp):
    d = dh0.shape[1]
    ns = d // N_DEV
    per_seq = lp // N_META

    def body(x_ref, o_ref):
        x = x_ref[...]
        for j in range(N_DEV):
            _accumulate(o_ref.at[j], x[:, j * ns:(j + 1) * ns], pl.program_id(0) == 0)

    return pl.pallas_call(
        body, name="meta_grad", grid=(nb,),
        in_specs=[pl.BlockSpec((N_META, d), lambda b: (b * per_seq + PAD // N_META, 0))],
        out_specs=pl.BlockSpec((N_DEV, N_META, ns), lambda b: (0, 0, 0)),
        out_shape=jax.ShapeDtypeStruct((N_DEV, N_META, ns), F32),
        compiler_params=_params(("arbitrary",)))(dh0)


VECTORS = [("ffn1_norm", 1024), ("mix_norm", 1024), ("q_latent_norm", 384), ("kv_latent_norm", 256),
           ("q_head_norm", 192), ("k_head_norm", 192), ("conv_b", 512), ("gate_a_b", 512), ("gate_x_b", 512),
           ("lru_lambda", 512), ("attn_out_norm", 512), ("lru_out_norm", 512), ("ffn2_norm", 1024),
           ("final_norm", 1024)]
VEC_ROWS = 16
GATES = ["gate_a_w", "gate_x_w"]


def pack_vectors(grads):
    def body(*refs):
        o_ref = refs[-1]
        o_ref[...] = jnp.zeros_like(o_ref)
        for t, (ref, (_, cnt)) in enumerate(zip(refs[:-1], VECTORS)):
            o_ref[t:t + 1, :cnt] = ref[:, :cnt]

    return pl.pallas_call(body, name="pack_vectors", out_shape=jax.ShapeDtypeStruct((VEC_ROWS, D_MODEL), F32),
                          compiler_params=_params(None))(*[grads[name] for name, _ in VECTORS])


def _adamw_update(w, g, m, v):
    c1 = 1.0 / (1.0 - ADAM_B1 ** ADAM_STEP)
    c2 = 1.0 / (1.0 - ADAM_B2 ** ADAM_STEP)
    mn = ADAM_B1 * m + (1.0 - ADAM_B1) * g
    vn = ADAM_B2 * v + (1.0 - ADAM_B2) * (g * g)
    delta = -ADAM_LR * ((mn * c1) / (jnp.sqrt(vn * c2) + ADAM_EPS) + ADAM_WD * w)
    return delta, mn, vn


def _sum_slots(ref, index=()):
    acc = ref[(0,) + index].astype(F32)
    for s in range(1, N_DEV):
        acc = acc + ref[(s,) + index].astype(F32)
    return acc


def adamw_sharded(r, w, m, v, name):
    rows, cols = w.shape
    tr = _tile(rows, 256, 16) if rows % 16 == 0 else rows

    def body(r_ref, w_ref, m_ref, v_ref, g_ref, d_ref, mo_ref, vo_ref):
        g = _sum_slots(r_ref)
        g_ref[...] = g
        d_ref[...], mo_ref[...], vo_ref[...] = _adamw_update(w_ref[...], g, m_ref[...], v_ref[...])

    spec = pl.BlockSpec((tr, cols), lambda i: (i, 0))
    shape = jax.ShapeDtypeStruct((rows, cols), F32)
    return pl.pallas_call(
        body, name=name, grid=(rows // tr,),
        in_specs=[pl.BlockSpec((N_DEV, tr, cols), lambda i: (0, i, 0))] + [spec] * 3,
        out_specs=[spec] * 4, out_shape=[shape] * 4,
        compiler_params=_params(("parallel",)),
    )(r, w, m, v)


def adamw_small(r_vec, r_gates, w, m, v):
    nt = len(VECTORS) + len(GATES)

    def body(*refs):
        rv_ref = refs[0]
        rg_refs = refs[1:1 + len(GATES)]
        base = 1 + len(GATES)
        w_refs, m_refs, v_refs = (refs[base + i * nt:base + (i + 1) * nt] for i in range(3))
        outs = refs[base + 3 * nt:]
        g_o, d_o, m_o, v_o = (outs[i * nt:(i + 1) * nt] for i in range(4))
        for t in range(nt):
            if t < len(VECTORS):
                cnt = VECTORS[t][1]
                g = _sum_slots(rv_ref, (slice(t, t + 1), slice(0, cnt)))
            else:
                g = _sum_slots(rg_refs[t - len(VECTORS)])
            g_o[t][...] = g
            d_o[t][...], m_o[t][...], v_o[t][...] = _adamw_update(w_refs[t][...], g, m_refs[t][...], v_refs[t][...])

    shapes = [jax.ShapeDtypeStruct(a.shape, F32) for a in w]
    res = pl.pallas_call(body, name="adamw_small", out_shape=shapes * 4,
                         compiler_params=_params(None))(r_vec, *r_gates, *w, *m, *v)
    return [res[i * nt:(i + 1) * nt] for i in range(4)]


def _block_diag(w):
    nb, n, _ = w.shape
    eye = jnp.eye(nb, dtype=w.dtype)
    return (eye[:, None, :, None] * w[:, :, None, :]).reshape(nb * n, nb * n)


def _two_d(a):
    if a.ndim == 3:
        return a.reshape(a.shape[1], a.shape[2])
    if a.ndim == 4:
        return a.reshape(a.shape[1] * a.shape[2], a.shape[3])
    return a


_WEIGHT_NAMES = ['meta_tokens', 'ffn1_norm', 'ffn1_w_gate', 'ffn1_w_up', 'ffn1_w_down', 'mix_norm', 'w_in',
                 'q_latent_norm', 'w_uq', 'kv_latent_norm', 'w_uk', 'w_uv', 'q_head_norm', 'k_head_norm', 'conv_w',
                 'conv_b', 'gate_a_w', 'gate_a_b', 'gate_x_w', 'gate_x_b', 'lru_lambda', 'attn_out_norm',
                 'lru_out_norm', 'w_out', 'ffn2_norm', 'ffn2_w_gate', 'ffn2_w_up', 'ffn2_w_down', 'final_norm']


def train_step(x, tgt, w, m, v):
    nb, seq, d = x.shape
    lp = PAD + N_META + seq
    n = nb * lp
    sh = {name: _two_d(w[name]) for name in _WEIGHT_NAMES}
    m2 = {name: _two_d(m[name]) for name in _WEIGHT_NAMES}
    v2 = {name: _two_d(v[name]) for name in _WEIGHT_NAMES}

    def b16(name):
        return sh[name].astype(BF16)

    out = {}

    def update(name, landed):
        out[name] = adamw_sharded(landed, sh[name], m2[name], v2[name], "adamw_" + name)

    g_wg1, g_wu1, g_wd1, g_meta, g_conv = exchange(
        [b16("ffn1_w_gate"), b16("ffn1_w_up"), b16("ffn1_w_down"), sh["meta_tokens"], sh["conv_w"]],
        ["gather"] * 5, "gather_ffn1")
    wg1 = assemble_cols(g_wg1, "assemble_wg1")
    wu1 = assemble_cols(g_wu1, "assemble_wu1")
    wd1 = g_wd1.reshape(D_FF, d)
    meta = assemble_cols(g_meta, "assemble_meta")
    conv_w = assemble_cols(g_conv, "assemble_conv")

    front = jnp.concatenate([jnp.zeros((PAD, d), F32), meta], axis=0)
    h0 = jnp.concatenate([jnp.broadcast_to(front[None], (nb, FIRST_FRAME, d)), x], axis=1).reshape(n, d)
    tgt_p = jnp.concatenate([jnp.zeros((nb, FIRST_FRAME, d), F32), tgt], axis=1).reshape(n, d)
    tables = _rope_tables(lp)
    zero_tail = jnp.zeros((1, HEAD_SLAB - D_QK), F32)
    gqh = jnp.concatenate([sh["q_head_norm"], zero_tail], axis=1)
    gkh = jnp.concatenate([sh["k_head_norm"], zero_tail], axis=1)
    wa = _block_diag(w["gate_a_w"][0]).astype(BF16)
    wx = _block_diag(w["gate_x_w"][0]).astype(BF16)

    (h1, u1, a1, b1), (g_in, g_uq, g_uk, g_uv, g_out, g_wg2) = ffn_fwd(
        h0, sh["ffn1_norm"], wg1, wu1, wd1, "ffn1_fwd",
        comm=([b16("w_in"), b16("w_uq"), b16("w_uk"), b16("w_uv"), b16("w_out"), b16("ffn2_w_gate")], ["gather"] * 6))
    wm, wl, wuq, wuk, wuv = assemble_mid(g_in, g_uq, g_uk, g_uv)
    w_out = g_out.reshape(d, d)

    u2, zm, zl = inproj_fwd(h1, sh["mix_norm"], wm, wl)
    q, k, vv, qn, cn = mla_prep_fwd(zm, sh["q_latent_norm"], sh["kv_latent_norm"], wuq, wuk, wuv, gqh, gkh, tables, lp)
    (y_mla, lse), (g_wu2, g_wd2) = attn_fwd(
        q, k, vv, nb, lp, comm=([b16("ffn2_w_up"), b16("ffn2_w_down")], ["gather"] * 2))
    wg2 = assemble_cols(g_wg2, "assemble_wg2")
    wu2 = assemble_cols(g_wu2, "assemble_wu2")
    wd2 = g_wd2.reshape(D_FF, d)
    y_lru, hs = lru_fwd(zl, conv_w, sh["conv_b"], wa, wx, sh["gate_a_b"], sh["gate_x_b"], sh["lru_lambda"], nb, lp)
    h2, yn = outproj_fwd(h1, y_mla, y_lru, sh["attn_out_norm"], sh["lru_out_norm"], w_out)
    (h3, u3, a3, b3), _ = ffn_fwd(h2, sh["ffn2_norm"], wg2, wu2, wd2, "ffn2_fwd")
    loss, dh3, g_final = final_loss(h3, sh["final_norm"], tgt_p, lp)

    vec = {"final_norm": g_final}
    (dh2, da3, db3, sh3, vec["ffn2_norm"]), _ = ffn_bwd_act(dh3, h2, sh["ffn2_norm"], a3, b3, wg2, wu2, wd2, "ffn2_bwd")
    dwg2 = tn_matmul(u3, da3, "ffn2_dwg", "cols")
    dwu2, (r_wg2,) = tn_matmul(u3, db3, "ffn2_dwu", "cols", comm=([dwg2], ["scatter"]))
    dwd2, (r_wu2,) = tn_matmul(sh3, dh3, "ffn2_dwd", "bf16", comm=([dwu2], ["scatter"]))
    dwd2 = dwd2.reshape(N_DEV, D_FF // N_DEV, d)
    update("ffn2_w_gate", r_wg2)
    update("ffn2_w_up", r_wu2)

    dy_mla, dy_lru, vec["attn_out_norm"], vec["lru_out_norm"] = outproj_bwd(
        dh2, y_mla, y_lru, sh["attn_out_norm"], sh["lru_out_norm"], w_out)
    dw_out = tn_matmul(yn, dh2, "dw_out", "bf16").reshape(N_DEV, d // N_DEV, d)
    du, dgate, dconv, vec["conv_b"], vec["gate_a_b"], vec["gate_x_b"], vec["lru_lambda"], dga, dgx = lru_bwd(
        zl, hs, dy_lru, conv_w, sh["conv_b"], wa, wx, sh["gate_a_b"], sh["gate_x_b"], sh["lru_lambda"], nb, lp)

    (dq, dk, dv), (r_wd2,) = attn_bwd(q, k, vv, y_mla, dy_mla, lse, nb, lp, comm=([dwd2], ["scatter"]))
    update("ffn2_w_down", r_wd2)

    dzm, dqr, dkr, vec["q_latent_norm"], vec["kv_latent_norm"], vec["q_head_norm"], vec["k_head_norm"] = mla_prep_bwd(
        dq, dk, dv, zm, qn, cn, sh["q_latent_norm"], sh["kv_latent_norm"], wuq, wuk, wuv, gqh, gkh, tables, lp)
    dwuq = tn_matmul(qn, dqr, "dw_uq")
    dwuk = tn_matmul(cn, dkr, "dw_uk")
    dwuv = tn_matmul(cn, dv, "dw_uv")
    dh1, vec["mix_norm"] = inproj_bwd(dzm, du, dgate, dh2, h1, sh["mix_norm"], wm, wl)
    dwm = tn_matmul(u2, dzm, "dw_in_mla")
    dwl_u = tn_matmul(u2, du, "dw_in_u")
    dwl_g = tn_matmul(u2, dgate, "dw_in_gate")
    s_in, s_uq, s_uk, s_uv, s_conv = split_mid_grads(dwm, dwl_u, dwl_g, dwuq, dwuk, dwuv, dconv)

    (dh0, da1, db1, sh1, vec["ffn1_norm"]), landed = ffn_bwd_act(
        dh1, h0, sh["ffn1_norm"], a1, b1, wg1, wu1, wd1, "ffn1_bwd",
        comm=([s_in, s_uq, s_uk, s_uv, dw_out, s_conv], ["scatter"] * 6))
    for name, r in zip(("w_in", "w_uq", "w_uk", "w_uv", "w_out", "conv_w"), landed):
        update(name, r)

    dwg1 = tn_matmul(u1, da1, "ffn1_dwg", "cols")
    dwu1, (r_wg1,) = tn_matmul(u1, db1, "ffn1_dwu", "cols", comm=([dwg1], ["scatter"]))
    dwd1, (r_wu1,) = tn_matmul(sh1, dh1, "ffn1_dwd", "bf16", comm=([dwu1], ["scatter"]))
    dwd1 = dwd1.reshape(N_DEV, D_FF // N_DEV, d)
    dmeta = meta_grad(dh0, nb, lp)
    gates = [dga.reshape(LRU_WIDTH, LRU_BLOCK), dgx.reshape(LRU_WIDTH, LRU_BLOCK)]
    r_wd1, r_meta, r_vec, r_ga, r_gx = exchange(
        [dwd1, dmeta, pack_vectors(vec)] + gates, ["scatter"] * 2 + ["gather"] * 3, "exchange_last")
    update("ffn1_w_gate", r_wg1)
    update("ffn1_w_up", r_wu1)
    update("ffn1_w_down", r_wd1)
    update("meta_tokens", r_meta)

    small = [name for name, _ in VECTORS] + GATES
    res = adamw_small(r_vec, [r_ga, r_gx], [sh[nm] for nm in small], [m2[nm] for nm in small], [v2[nm] for nm in small])
    for i, name in enumerate(small):
        out[name] = [res[j][i] for j in range(4)]

    grad_x = dh0.reshape(nb, lp, d)[:, FIRST_FRAME:]
    loss = lax.psum(loss[0, 0], ("x", "y", "c"))
    cols = [[out[name][j].reshape(w[name].shape) for name in _WEIGHT_NAMES] for j in range(4)]
    return (loss, grad_x, *cols[0], *cols[1], *cols[2], *cols[3])


def kernel(x, meta_tokens, ffn1_norm, ffn1_w_gate, ffn1_w_up, ffn1_w_down, mix_norm, w_in, q_latent_norm, w_uq, kv_latent_norm, w_uk, w_uv, q_head_norm, k_head_norm, conv_w, conv_b, gate_a_w, gate_a_b, gate_x_w, gate_x_b, lru_lambda, attn_out_norm, lru_out_norm, w_out, ffn2_norm, ffn2_w_gate, ffn2_w_up, ffn2_w_down, final_norm, loss_target, m_meta_tokens, m_ffn1_norm, m_ffn1_w_gate, m_ffn1_w_up, m_ffn1_w_down, m_mix_norm, m_w_in, m_q_latent_norm, m_w_uq, m_kv_latent_norm, m_w_uk, m_w_uv, m_q_head_norm, m_k_head_norm, m_conv_w, m_conv_b, m_gate_a_w, m_gate_a_b, m_gate_x_w, m_gate_x_b, m_lru_lambda, m_attn_out_norm, m_lru_out_norm, m_w_out, m_ffn2_norm, m_ffn2_w_gate, m_ffn2_w_up, m_ffn2_w_down, m_final_norm, v_meta_tokens, v_ffn1_norm, v_ffn1_w_gate, v_ffn1_w_up, v_ffn1_w_down, v_mix_norm, v_w_in, v_q_latent_norm, v_w_uq, v_kv_latent_norm, v_w_uk, v_w_uv, v_q_head_norm, v_k_head_norm, v_conv_w, v_conv_b, v_gate_a_w, v_gate_a_b, v_gate_x_w, v_gate_x_b, v_lru_lambda, v_attn_out_norm, v_lru_out_norm, v_w_out, v_ffn2_norm, v_ffn2_w_gate, v_ffn2_w_up, v_ffn2_w_down, v_final_norm):
    args = locals()
    w = {name: args[name] for name in _WEIGHT_NAMES}
    m = {name: args["m_" + name] for name in _WEIGHT_NAMES}
    v = {name: args["v_" + name] for name in _WEIGHT_NAMES}
    return train_step(x, loss_target, w, m, v)
```

```python
import math

import jax
import jax.numpy as jnp
from jax import lax
from jax.experimental import pallas as pl
from jax.experimental.pallas import tpu as pltpu

F32 = jnp.float32
BF16 = jnp.bfloat16

D_MODEL = 1024
CHUNK = 64
CHUNK_SHIFT = 6
N_META = 16
PAD = CHUNK - N_META
FIRST_FRAME = PAD + N_META
MLA_HEADS = 4
D_NOPE = 128
D_ROPE = 64
D_QK = D_NOPE + D_ROPE
D_V = 128
HEAD_SLAB = 256
KV_RANK = 256
Q_RANK = 384
ROPE_THETA = 10000.0
LRU_WIDTH = 512
LRU_BLOCKS = 8
LRU_BLOCK = 64
LRU_TILE = 128
CONV_W = 4
C_RGLRU = 8.0
D_FF = 2816
MLA_IN = 768
EPS = 1e-6
NEG_INF = -1e30
N_DEV = 8
LANES = 128
VMEM_LIMIT = 52 * 1024 * 1024

ADAM_LR = 0.001
ADAM_B1 = 0.9
ADAM_B2 = 0.999
ADAM_EPS = 1e-08
ADAM_WD = 0.01
ADAM_STEP = 10

VMEM_WHOLE = pl.BlockSpec(memory_space=pltpu.VMEM)
HBM_WHOLE = pl.BlockSpec(memory_space=pl.ANY)


def _params(sems):
    if sems is None:
        return pltpu.CompilerParams(vmem_limit_bytes=VMEM_LIMIT)
    return pltpu.CompilerParams(dimension_semantics=sems, vmem_limit_bytes=VMEM_LIMIT)


def _tile(n, cap, mult=16):
    best = None
    for t in range(mult, min(n, cap) + 1, mult):
        if n % t == 0:
            best = t
    assert best is not None, (n, cap, mult)
    return best


def _row(tm, d):
    return pl.BlockSpec((tm, d), lambda i: (i, 0))


def _fixed(shape):
    return pl.BlockSpec(shape, lambda i: (0,) * len(shape))


def _mesh_position():
    return lax.axis_index("x"), lax.axis_index("y"), lax.axis_index("c")


def _flat_index(x, y, c):
    return 4 * x + 2 * y + c


def _peers(x, y, c):
    out = []
    for k in range(1, N_DEV):
        fx, fy, fc = (k >> 2) & 1, (k >> 1) & 1, k & 1
        out.append((1 - x if fx else x, 1 - y if fy else y, 1 - c if fc else c))
    return out


def _comm_out_shapes(srcs, modes):
    return [jax.ShapeDtypeStruct((N_DEV,) + s.shape if md == "gather" else s.shape, s.dtype)
            for s, md in zip(srcs, modes)]


def _comm_scratch(n):
    per_peer = n * (N_DEV - 1)
    return [pltpu.SemaphoreType.DMA((per_peer,)), pltpu.SemaphoreType.DMA((per_peer,)), pltpu.SemaphoreType.DMA((n,))]


class _Copies:
    def __init__(self, own, first, relay):
        self.own, self.first, self.relay = own, first, relay

    def start(self):
        for cp in self.own + self.first:
            cp.start()

    def forward(self):
        for arrival, onward in self.relay:
            arrival.wait_recv()
            onward.start()

    def finish(self):
        arrivals = [a for a, _ in self.relay]
        onward = [f for _, f in self.relay]
        for cp in self.first + onward:
            if not any(cp is a for a in arrivals):
                cp.wait_recv()
        for cp in self.first + onward:
            cp.wait_send()
        for cp in self.own:
            cp.wait()


def _comm_copies(src_refs, dst_refs, modes, send, recv, local):
    x, y, c = _mesh_position()
    me = _flat_index(x, y, c)
    n = len(modes)
    sibling = (x, y, 1 - c)
    chips = [(1 - x, y), (x, 1 - y), (1 - x, 1 - y)]

    def remote(src, dst, k, t, to):
        return pltpu.make_async_remote_copy(src_ref=src, dst_ref=dst, send_sem=send.at[k * n + t],
                                            recv_sem=recv.at[k * n + t], device_id=to,
                                            device_id_type=pl.DeviceIdType.MESH)

    own, first, relay = [], [], []
    for t, (src, dst, md) in enumerate(zip(src_refs, dst_refs, modes)):
        if md == "scatter":
            own.append(pltpu.make_async_copy(src.at[me], dst.at[me], local.at[t]))
            for k, peer in enumerate(_peers(x, y, c)):
                first.append(remote(src.at[_flat_index(*peer)], dst.at[me], k, t, peer))
        else:
            own.append(pltpu.make_async_copy(src, dst.at[me], local.at[t]))
            first.append(remote(src, dst.at[me], 0, t, sibling))
            for j, chip in enumerate(chips):
                arrival = remote(src, dst.at[me], 1 + j, t, (*chip, c))
                landed = dst.at[_flat_index(*chip, c)]
                first.append(arrival)
                relay.append((arrival, remote(landed, landed, 4 + j, t, sibling)))
    return _Copies(own, first, relay)


def _hosted(body, n_in, n_out, modes, grid):
    t = len(modes)
    total = math.prod(grid)

    def wrapped(*refs):
        ins, csrc = refs[:n_in], refs[n_in:n_in + t]
        outs = refs[n_in + t:n_in + t + n_out]
        cdst = refs[n_in + t + n_out:n_in + 2 * t + n_out]
        scratch = refs[n_in + 2 * t + n_out:-3]
        copies = _comm_copies(csrc, cdst, modes, *refs[-3:])
        step = pl.program_id(0)
        for axis in range(1, len(grid)):
            step = step * grid[axis] + pl.program_id(axis)

        @pl.when(step == 0)
        def _():
            copies.start()

        body(*ins, *outs, *scratch)

        @pl.when(step == (total * 3) // 5)
        def _():
            copies.forward()

        @pl.when(step == total - 1)
        def _():
            copies.finish()

    return wrapped


def _call(body, name, grid, in_specs, out_specs, out_shape, sems, args, scratch=(), comm=None):
    if comm is None:
        outs = pl.pallas_call(body, name=name, grid=grid, in_specs=in_specs, out_specs=out_specs, out_shape=out_shape,
                              scratch_shapes=list(scratch), compiler_params=_params(sems))(*args)
        return outs, []
    srcs, modes = comm
    n = len(modes)
    res = pl.pallas_call(
        _hosted(body, len(in_specs), len(out_specs), modes, grid), name=name, grid=grid,
        in_specs=list(in_specs) + [HBM_WHOLE] * n, out_specs=list(out_specs) + [HBM_WHOLE] * n,
        out_shape=list(out_shape) + _comm_out_shapes(srcs, modes),
        scratch_shapes=list(scratch) + _comm_scratch(n),
        compiler_params=_params(("arbitrary",) * len(grid)))(*args, *srcs)
    return res[:len(out_specs)], res[len(out_specs):]


def exchange(srcs, modes, name):
    n = len(modes)

    def body(*refs):
        copies = _comm_copies(refs[:n], refs[n:2 * n], modes, *refs[2 * n:])
        copies.start()
        copies.forward()
        copies.finish()

    return pl.pallas_call(body, name=name, in_specs=[HBM_WHOLE] * n, out_specs=[HBM_WHOLE] * n,
                          out_shape=_comm_out_shapes(srcs, modes), scratch_shapes=_comm_scratch(n))(*srcs)


def _nn(a, b):
    return jnp.dot(a, b, preferred_element_type=F32)


def _nt(a, b):
    return lax.dot_general(a, b, (((1,), (1,)), ((), ())), preferred_element_type=F32)


def _tn(a, b):
    return lax.dot_general(a, b, (((0,), (0,)), ((), ())), preferred_element_type=F32)


def _sig(x):
    return 1.0 / (1.0 + jnp.exp(-x))


def _rms_r(x, n=None):
    n = x.shape[-1] if n is None else n
    return lax.rsqrt(jnp.sum(x * x, axis=-1, keepdims=True) * (1.0 / n) + EPS)


def _rms_bwd(x, r, g, dy, n=None):
    n = x.shape[-1] if n is None else n
    xhat = x * r
    dxhat = dy * g
    dx = r * (dxhat - xhat * (jnp.sum(dxhat * xhat, axis=-1, keepdims=True) * (1.0 / n)))
    return dx, jnp.sum(dy * xhat, axis=0, keepdims=True)


def _accumulate(ref, val, first):
    @pl.when(first)
    def _():
        ref[...] = val

    @pl.when(jnp.logical_not(first))
    def _():
        ref[...] += val


_GELU_C = math.sqrt(2.0 / math.pi)


def _gelu_and_grad(x):
    inner = _GELU_C * (x + 0.044715 * x * x * x)
    t = jnp.tanh(inner)
    gelu = 0.5 * x * (1.0 + t)
    dgelu = 0.5 * (1.0 + t) + 0.5 * x * (1.0 - t * t) * _GELU_C * (1.0 + 3.0 * 0.044715 * x * x)
    return gelu, dgelu


def _log1p_small(t):
    return jnp.where(t < 1e-3, t * (1.0 - t * (0.5 - t * (1.0 / 3.0))), jnp.log(1.0 + t))


def _softplus(x):
    return jnp.maximum(x, 0.0) + _log1p_small(jnp.exp(-jnp.abs(x)))


def _neg_expm1(x):
    return jnp.where(x > -1e-2, -x * (1.0 + x * (0.5 + x * (1.0 / 6.0))), 1.0 - jnp.exp(x))


def _ff_chunks(f):
    return 2 if (f // 2) % LANES == 0 else 1


def ffn_fwd(h, g, wg, wu, wd, name, comm=None):
    n, d = h.shape
    f = wg.shape[0]
    tm = _tile(n, 352)
    nc = _ff_chunks(f)
    fc = f // nc

    def body(h_ref, g_ref, wg_ref, wu_ref, wd_ref, ho_ref, u_ref, a_ref, b_ref):
        x = h_ref[...]
        u = (x * _rms_r(x) * g_ref[...]).astype(BF16)
        acc = jnp.zeros((tm, d), F32)
        for c in range(nc):
            cols = slice(c * fc, (c + 1) * fc)
            a = _nt(u, wg_ref[cols, :])
            b = _nt(u, wu_ref[cols, :])
            s = (a * _sig(a) * b).astype(BF16)
            acc = acc + _nn(s, wd_ref[cols, :])
            a_ref[:, cols] = a.astype(BF16)
            b_ref[:, cols] = b.astype(BF16)
        ho_ref[...] = x + 0.5 * acc
        u_ref[...] = u

    return _call(
        body, name, (n // tm,),
        [_row(tm, d), _fixed((1, d)), VMEM_WHOLE, VMEM_WHOLE, VMEM_WHOLE],
        [_row(tm, d), _row(tm, d), _row(tm, f), _row(tm, f)],
        [jax.ShapeDtypeStruct((n, d), F32), jax.ShapeDtypeStruct((n, d), BF16),
         jax.ShapeDtypeStruct((n, f), BF16), jax.ShapeDtypeStruct((n, f), BF16)],
        ("parallel",), (h, g, wg, wu, wd), comm=comm)


def ffn_bwd_act(dh, h, g, a, b, wg, wu, wd, name, comm=None):
    n, d = h.shape
    f = wg.shape[0]
    tm = _tile(n, 192)
    nc = _ff_chunks(f)
    fc = f // nc

    def body(dh_ref, h_ref, g_ref, a_ref, b_ref, wg_ref, wu_ref, wd_ref,
             dhi_ref, da_ref, db_ref, sh_ref, dg_ref):
        x = h_ref[...]
        dy = dh_ref[...]
        r = _rms_r(x)
        dhh = (0.5 * dy).astype(BF16)
        du = jnp.zeros((tm, d), F32)
        for c in range(nc):
            cols = slice(c * fc, (c + 1) * fc)
            ds = _nt(dhh, wd_ref[cols, :])
            av = a_ref[:, cols].astype(F32)
            bv = b_ref[:, cols].astype(F32)
            sg = _sig(av)
            sil = av * sg
            da = (ds * bv * (sg * (1.0 + av * (1.0 - sg)))).astype(BF16)
            db = (ds * sil).astype(BF16)
            da_ref[:, cols] = da
            db_ref[:, cols] = db
            sh_ref[:, cols] = (0.5 * sil * bv).astype(BF16)
            du = du + _nn(da, wg_ref[cols, :]) + _nn(db, wu_ref[cols, :])
        dx, dg = _rms_bwd(x, r, g_ref[...], du)
        dhi_ref[...] = dy + dx
        _accumulate(dg_ref, dg, pl.program_id(0) == 0)

    return _call(
        body, name, (n // tm,),
        [_row(tm, d), _row(tm, d), _fixed((1, d)), _row(tm, f), _row(tm, f), VMEM_WHOLE, VMEM_WHOLE, VMEM_WHOLE],
        [_row(tm, d), _row(tm, f), _row(tm, f), _row(tm, f), _fixed((1, d))],
        [jax.ShapeDtypeStruct((n, d), F32), jax.ShapeDtypeStruct((n, f), BF16),
         jax.ShapeDtypeStruct((n, f), BF16), jax.ShapeDtypeStruct((n, f), BF16),
         jax.ShapeDtypeStruct((1, d), F32)],
        ("arbitrary",), (dh, h, g, a, b, wg, wu, wd), comm=comm)


def tn_matmul(x, y, name, out="f32", comm=None):
    n, k = x.shape
    m = y.shape[1]
    mc = m
    while k * mc > 1536 * 1024 and mc % (2 * LANES) == 0:
        mc //= 2
    tm = _tile(n, 704)

    def body(x_ref, y_ref, o_ref, *acc):
        i = pl.program_id(1)
        part = _tn(x_ref[...].astype(BF16), y_ref[...].astype(BF16))
        if out == "f32":
            _accumulate(o_ref, part, i == 0)
            return
        _accumulate(acc[0], part, i == 0)

        @pl.when(i == n // tm - 1)
        def _():
            o_ref[...] = acc[0][...].astype(BF16)

    out_spec = pl.BlockSpec((k, mc), lambda j, i: (0, j))
    out_shape = jax.ShapeDtypeStruct((k, m), F32 if out == "f32" else BF16)
    (res,), landed = _call(
        body, name, (m // mc, n // tm),
        [pl.BlockSpec((tm, k), lambda j, i: (i, 0)), pl.BlockSpec((tm, mc), lambda j, i: (i, j))],
        [out_spec], [out_shape], ("parallel", "arbitrary"), (x, y),
        scratch=[] if out == "f32" else [pltpu.VMEM((k, mc), F32)], comm=comm)
    return (res, landed) if comm is not None else res


def inproj_fwd(h, g, wm, wl):
    n, d = h.shape
    tm = _tile(n, 352)

    def body(h_ref, g_ref, wm_ref, wl_ref, u_ref, zm_ref, zl_ref):
        x = h_ref[...]
        u = (x * _rms_r(x) * g_ref[...]).astype(BF16)
        u_ref[...] = u
        zm_ref[...] = _nt(u, wm_ref[...])
        zl_ref[...] = _nt(u, wl_ref[...])

    return pl.pallas_call(
        body, name="inproj_fwd", grid=(n // tm,),
        in_specs=[_row(tm, d), _fixed((1, d)), VMEM_WHOLE, VMEM_WHOLE],
        out_specs=[_row(tm, d), _row(tm, MLA_IN), _row(tm, 2 * LRU_WIDTH)],
        out_shape=[jax.ShapeDtypeStruct((n, d), BF16), jax.ShapeDtypeStruct((n, MLA_IN), F32),
                   jax.ShapeDtypeStruct((n, 2 * LRU_WIDTH), F32)],
        compiler_params=_params(("parallel",)),
    )(h, g, wm, wl)


def inproj_bwd(dzm, du, dgate, dh2, h, g, wm, wl):
    n, d = h.shape
    tm = _tile(n, 352)

    def body(dzm_ref, du_ref, dgt_ref, dh2_ref, h_ref, g_ref, wm_ref, wl_ref, dh_ref, dg_ref):
        x = h_ref[...]
        dun = (_nn(dzm_ref[...].astype(BF16), wm_ref[...])
               + _nn(du_ref[...].astype(BF16), wl_ref[:LRU_WIDTH, :])
               + _nn(dgt_ref[...].astype(BF16), wl_ref[LRU_WIDTH:, :]))
        dx, dg = _rms_bwd(x, _rms_r(x), g_ref[...], dun)
        dh_ref[...] = dh2_ref[...] + dx
        _accumulate(dg_ref, dg, pl.program_id(0) == 0)

    return pl.pallas_call(
        body, name="inproj_bwd", grid=(n // tm,),
        in_specs=[_row(tm, MLA_IN), _row(tm, LRU_WIDTH), _row(tm, LRU_WIDTH), _row(tm, d), _row(tm, d),
                  _fixed((1, d)), VMEM_WHOLE, VMEM_WHOLE],
        out_specs=[_row(tm, d), _fixed((1, d))],
        out_shape=[jax.ShapeDtypeStruct((n, d), F32), jax.ShapeDtypeStruct((1, d), F32)],
        compiler_params=_params(("arbitrary",)),
    )(dzm, du, dgate, dh2, h, g, wm, wl)


def _rope_tables(lp):
    pos = jnp.arange(lp, dtype=F32) - float(PAD)
    half = D_ROPE // 2
    inv_freq = ROPE_THETA ** (-jnp.arange(0, half, dtype=F32) / half)
    ang = pos[:, None] * inv_freq[None, :]
    cos, sin = jnp.cos(ang), jnp.sin(ang)
    one = jnp.ones((lp, D_NOPE), F32)
    z_nope = jnp.zeros((lp, D_NOPE), F32)
    z_half = jnp.zeros((lp, half), F32)
    z_tail = jnp.zeros((lp, HEAD_SLAB - D_QK), F32)
    cosr = jnp.concatenate([one, cos, cos, z_tail], axis=1)
    sin_up = jnp.concatenate([z_nope, z_half, sin, z_tail], axis=1)
    sin_dn = jnp.concatenate([z_nope, -sin, z_half, z_tail], axis=1)
    return cosr, sin_up, sin_dn


def _rope(x, cosr, sin_up, sin_dn):
    half = D_ROPE // 2
    return x * cosr + pltpu.roll(x, half, axis=1) * sin_up + pltpu.roll(x, HEAD_SLAB - half, axis=1) * sin_dn


def _rope_bwd(dy, cosr, sin_up, sin_dn):
    half = D_ROPE // 2
    return (dy * cosr + pltpu.roll(dy * sin_up, HEAD_SLAB - half, axis=1)
            + pltpu.roll(dy * sin_dn, half, axis=1))


def _k_rope_slab(zm_tile):
    tm = zm_tile.shape[0]
    krp = zm_tile[:, Q_RANK + KV_RANK:MLA_IN]
    return jnp.concatenate([jnp.zeros((tm, D_NOPE), F32), krp], axis=1)


def mla_prep_fwd(zm, gql, gkvl, wuq, wuk, wuv, gqh, gkh, tables, lp):
    n = zm.shape[0]
    tm = _tile(lp, 352)
    per_seq = lp // tm
    width = MLA_HEADS * HEAD_SLAB
    scale = 1.0 / math.sqrt(D_QK)

    def body(zm_ref, gql_ref, gkvl_ref, wuq_ref, wuk_ref, wuv_ref, gqh_ref, gkh_ref,
             cos_ref, up_ref, dn_ref, q_ref, k_ref, v_ref, qn_ref, cn_ref):
        z = zm_ref[...]
        cq = z[:, :Q_RANK]
        ckv = z[:, Q_RANK:Q_RANK + KV_RANK]
        qn = (cq * _rms_r(cq) * gql_ref[...]).astype(BF16)
        cn = (ckv * _rms_r(ckv) * gkvl_ref[...]).astype(BF16)
        qn_ref[...] = qn
        cn_ref[...] = cn
        q_raw = _nt(qn, wuq_ref[...])
        k_raw = _nt(cn, wuk_ref[...])
        v_ref[...] = _nt(cn, wuv_ref[...]).astype(BF16)
        kr_slab = _k_rope_slab(z)
        cosr, sin_up, sin_dn = cos_ref[...], up_ref[...], dn_ref[...]
        for hd in range(MLA_HEADS):
            cols = slice(hd * HEAD_SLAB, (hd + 1) * HEAD_SLAB)
            xq = q_raw[:, cols]
            yq = _rope(xq * _rms_r(xq, D_QK) * gqh_ref[...], cosr, sin_up, sin_dn)
            q_ref[:, cols] = (yq * scale).astype(BF16)
            xk = k_raw[:, cols] + kr_slab
            yk = _rope(xk * _rms_r(xk, D_QK) * gkh_ref[...], cosr, sin_up, sin_dn)
            k_ref[:, cols] = yk.astype(BF16)

    tab = pl.BlockSpec((tm, HEAD_SLAB), lambda i: (i % per_seq, 0))
    return pl.pallas_call(
        body, name="mla_prep_fwd", grid=(n // tm,),
        in_specs=[_row(tm, MLA_IN), _fixed((1, Q_RANK)), _fixed((1, KV_RANK)), VMEM_WHOLE, VMEM_WHOLE, VMEM_WHOLE,
                  _fixed((1, HEAD_SLAB)), _fixed((1, HEAD_SLAB)), tab, tab, tab],
        out_specs=[_row(tm, width), _row(tm, width), _row(tm, MLA_HEADS * D_V), _row(tm, Q_RANK), _row(tm, KV_RANK)],
        out_shape=[jax.ShapeDtypeStruct((n, width), BF16), jax.ShapeDtypeStruct((n, width), BF16),
                   jax.ShapeDtypeStruct((n, MLA_HEADS * D_V), BF16), jax.ShapeDtypeStruct((n, Q_RANK), BF16),
                   jax.ShapeDtypeStruct((n, KV_RANK), BF16)],
        compiler_params=_params(("parallel",)),
    )(zm, gql, gkvl, wuq, wuk, wuv, gqh, gkh, *tables)


def mla_prep_bwd(dq, dk, dv, zm, qn, cn, gql, gkvl, wuq, wuk, wuv, gqh, gkh, tables, lp, comm=None):
    n = zm.shape[0]
    tm = _tile(lp, 352)
    per_seq = lp // tm
    width = MLA_HEADS * HEAD_SLAB
    scale = 1.0 / math.sqrt(D_QK)

    def body(dq_ref, dk_ref, dv_ref, zm_ref, qn_ref, cn_ref, gql_ref, gkvl_ref, wuq_ref, wuk_ref, wuv_ref,
             gqh_ref, gkh_ref, cos_ref, up_ref, dn_ref,
             dzm_ref, dqr_ref, dkr_ref, dgql_ref, dgkvl_ref, dgqh_ref, dgkh_ref):
        z = zm_ref[...]
        cq = z[:, :Q_RANK]
        ckv = z[:, Q_RANK:Q_RANK + KV_RANK]
        q_raw = _nt(qn_ref[...], wuq_ref[...])
        k_raw = _nt(cn_ref[...], wuk_ref[...])
        kr_slab = _k_rope_slab(z)
        cosr, sin_up, sin_dn = cos_ref[...], up_ref[...], dn_ref[...]
        dgq = jnp.zeros((1, HEAD_SLAB), F32)
        dgk = jnp.zeros((1, HEAD_SLAB), F32)
        dkrp = jnp.zeros((tm, HEAD_SLAB - D_NOPE), F32)
        for hd in range(MLA_HEADS):
            cols = slice(hd * HEAD_SLAB, (hd + 1) * HEAD_SLAB)
            xq = q_raw[:, cols]
            dxn = _rope_bwd(dq_ref[:, cols] * scale, cosr, sin_up, sin_dn)
            dxq, dg = _rms_bwd(xq, _rms_r(xq, D_QK), gqh_ref[...], dxn, D_QK)
            dgq = dgq + dg
            dqr_ref[:, cols] = dxq.astype(BF16)
            xk = k_raw[:, cols] + kr_slab
            dxn = _rope_bwd(dk_ref[:, cols], cosr, sin_up, sin_dn)
            dxk, dg = _rms_bwd(xk, _rms_r(xk, D_QK), gkh_ref[...], dxn, D_QK)
            dgk = dgk + dg
            dkr_ref[:, cols] = dxk.astype(BF16)
            dkrp = dkrp + dxk[:, D_NOPE:]
        dqn = _nn(dqr_ref[...], wuq_ref[...])
        dcn = _nn(dkr_ref[...], wuk_ref[...]) + _nn(dv_ref[...].astype(BF16), wuv_ref[...])
        dcq, dg1 = _rms_bwd(cq, _rms_r(cq), gql_ref[...], dqn)
        dckv, dg2 = _rms_bwd(ckv, _rms_r(ckv), gkvl_ref[...], dcn)
        dzm_ref[:, :Q_RANK] = dcq
        dzm_ref[:, Q_RANK:Q_RANK + KV_RANK] = dckv
        dzm_ref[:, Q_RANK + KV_RANK:] = dkrp
        first = pl.program_id(0) == 0
        _accumulate(dgql_ref, dg1, first)
        _accumulate(dgkvl_ref, dg2, first)
        _accumulate(dgqh_ref, dgq, first)
        _accumulate(dgkh_ref, dgk, first)

    tab = pl.BlockSpec((tm, HEAD_SLAB), lambda i: (i % per_seq, 0))
    return _call(
        body, "mla_prep_bwd", (n // tm,),
        [_row(tm, width), _row(tm, width), _row(tm, MLA_HEADS * D_V), _row(tm, MLA_IN),
         _row(tm, Q_RANK), _row(tm, KV_RANK), _fixed((1, Q_RANK)), _fixed((1, KV_RANK)),
         VMEM_WHOLE, VMEM_WHOLE, VMEM_WHOLE, _fixed((1, HEAD_SLAB)), _fixed((1, HEAD_SLAB)), tab, tab, tab],
        [_row(tm, MLA_IN), _row(tm, width), _row(tm, width), _fixed((1, Q_RANK)), _fixed((1, KV_RANK)),
         _fixed((1, HEAD_SLAB)), _fixed((1, HEAD_SLAB))],
        [jax.ShapeDtypeStruct((n, MLA_IN), F32), jax.ShapeDtypeStruct((n, width), BF16),
         jax.ShapeDtypeStruct((n, width), BF16), jax.ShapeDtypeStruct((1, Q_RANK), F32),
         jax.ShapeDtypeStruct((1, KV_RANK), F32), jax.ShapeDtypeStruct((1, HEAD_SLAB), F32),
         jax.ShapeDtypeStruct((1, HEAD_SLAB), F32)],
        ("arbitrary",), (dq, dk, dv, zm, qn, cn, gql, gkvl, wuq, wuk, wuv, gqh, gkh, *tables), comm=comm)


def _attn_tile(lp):
    return _tile(lp, 704, CHUNK)


def _chunk_mask(i, j, t):
    qpos = i * t + lax.broadcasted_iota(jnp.int32, (t, t), 0)
    kpos = j * t + lax.broadcasted_iota(jnp.int32, (t, t), 1)
    same_or_earlier = jnp.right_shift(kpos, CHUNK_SHIFT) <= jnp.right_shift(qpos, CHUNK_SHIFT)
    return jnp.logical_and(same_or_earlier, kpos >= PAD)


def attn_fwd(q, k, v, nb, lp, comm=None):
    n = q.shape[0]
    t = _attn_tile(lp)
    nq = lp // t

    def body(q_ref, k_ref, v_ref, o_ref, lse_ref):
        i = pl.program_id(2)
        qv = q_ref[...]

        def kv_step(j, carry):
            m, l, acc = carry
            off = pl.multiple_of(j * t, t)
            s = _nt(qv, k_ref[pl.ds(off, t), :])
            s = jnp.where(_chunk_mask(i, j, t), s, NEG_INF)
            m_new = jnp.maximum(m, jnp.max(s, axis=-1, keepdims=True))
            p = jnp.exp(s - m_new)
            alpha = jnp.exp(m - m_new)
            l = alpha * l + jnp.sum(p, axis=-1, keepdims=True)
            acc = alpha * acc + _nn(p.astype(BF16), v_ref[pl.ds(off, t), :])
            return m_new, l, acc

        init = (jnp.full((t, 1), NEG_INF, F32), jnp.zeros((t, 1), F32), jnp.zeros((t, D_V), F32))
        m, l, acc = lax.fori_loop(0, i + 1, kv_step, init)
        o_ref[...] = acc * (1.0 / l)
        lse_ref[0] = jnp.broadcast_to(m + jnp.log(l), (t, LANES))

    return _call(
        body, "attn_fwd", (nb, MLA_HEADS, nq),
        [pl.BlockSpec((t, HEAD_SLAB), lambda b, h, i: (b * nq + i, h)),
         pl.BlockSpec((lp, HEAD_SLAB), lambda b, h, i: (b, h)),
         pl.BlockSpec((lp, D_V), lambda b, h, i: (b, h))],
        [pl.BlockSpec((t, D_V), lambda b, h, i: (b * nq + i, h)),
         pl.BlockSpec((1, t, LANES), lambda b, h, i: (h, b * nq + i, 0))],
        [jax.ShapeDtypeStruct((n, MLA_HEADS * D_V), F32), jax.ShapeDtypeStruct((MLA_HEADS, n, LANES), F32)],
        ("parallel", "parallel", "parallel"), (q, k, v), comm=comm)


def attn_bwd(q, k, v, o, do, lse, nb, lp, comm=None):
    n = q.shape[0]
    t = _attn_tile(lp)
    nq = lp // t

    def body(q_ref, k_ref, v_ref, o_ref, do_ref, lse_ref, dq_ref, dk_ref, dv_ref):
        dk_ref[...] = jnp.zeros_like(dk_ref)
        dv_ref[...] = jnp.zeros_like(dv_ref)

        def q_step(i, _):
            qoff = pl.multiple_of(i * t, t)
            qv = q_ref[pl.ds(qoff, t), :]
            dov = do_ref[pl.ds(qoff, t), :]
            delta = jnp.sum(o_ref[pl.ds(qoff, t), :] * dov, axis=-1, keepdims=True)
            lse_q = jnp.max(lse_ref[0, pl.ds(qoff, t), :], axis=-1, keepdims=True)
            do16 = dov.astype(BF16)

            def kv_step(j, dq_acc):
                koff = pl.multiple_of(j * t, t)
                kv = k_ref[pl.ds(koff, t), :]
                s = jnp.where(_chunk_mask(i, j, t), _nt(qv, kv), NEG_INF)
                p = jnp.exp(s - lse_q)
                dp = _nt(do16, v_ref[pl.ds(koff, t), :])
                ds16 = (p * (dp - delta)).astype(BF16)
                dv_ref[pl.ds(koff, t), :] += _tn(p.astype(BF16), do16)
                dk_ref[pl.ds(koff, t), :] += _tn(ds16, qv)
                return dq_acc + _nn(ds16, kv)

            dq_ref[pl.ds(qoff, t), :] = lax.fori_loop(0, i + 1, kv_step, jnp.zeros((t, HEAD_SLAB), F32))
            return 0

        lax.fori_loop(0, nq, q_step, 0)

    wide = pl.BlockSpec((lp, HEAD_SLAB), lambda b, h: (b, h))
    thin = pl.BlockSpec((lp, D_V), lambda b, h: (b, h))
    width = MLA_HEADS * HEAD_SLAB
    return _call(
        body, "attn_bwd", (nb, MLA_HEADS),
        [wide, wide, thin, thin, thin, pl.BlockSpec((1, lp, LANES), lambda b, h: (h, b, 0))],
        [wide, wide, thin],
        [jax.ShapeDtypeStruct((n, width), F32), jax.ShapeDtypeStruct((n, width), F32),
         jax.ShapeDtypeStruct((n, MLA_HEADS * D_V), F32)],
        ("parallel", "parallel"), (q, k, v, o, do, lse), comm=comm)


def _seq_rows(nb, lp, width):
    rows = lax.broadcasted_iota(jnp.int32, (lp, width), 0)
    return jnp.concatenate([rows] * nb, axis=0) if nb > 1 else rows


def _lru_gates(u, w_ref, cb, wa, wx, ba, bx, lam):
    xc = (cb + w_ref[pl.ds(3, 1), :] * u + w_ref[pl.ds(2, 1), :] * pltpu.roll(u, 1, axis=0)
          + w_ref[pl.ds(1, 1), :] * pltpu.roll(u, 2, axis=0) + w_ref[pl.ds(0, 1), :] * pltpu.roll(u, 3, axis=0))
    xc16 = xc.astype(BF16)
    ra = _sig(_nn(xc16, wa) + ba)
    ia = _sig(_nn(xc16, wx) + bx)
    sp = _softplus(-lam)
    log_a = -C_RGLRU * ra * sp
    a = jnp.exp(log_a)
    mult = jnp.sqrt(_neg_expm1(2.0 * log_a))
    return xc, xc16, ra, ia, sp, a, mult


def _scan_block_rows(width):
    return lax.broadcasted_iota(jnp.int32, (8, width), 0)


def lru_fwd(zl, conv_w, conv_b, wa, wx, ba, bx, lam, nb, lp):
    n = zl.shape[0]
    w = LRU_TILE
    nt = LRU_WIDTH // w
    nblk = lp // 8

    def body(u_ref, gt_ref, cw_ref, cb_ref, wa_ref, wx_ref, ba_ref, bx_ref, lam_ref, y_ref, h_ref, a_s, b_s):
        u = u_ref[...]
        xc, _, _, ia, _, a, mult = _lru_gates(u, cw_ref, cb_ref[...], wa_ref[...], wx_ref[...],
                                              ba_ref[...], bx_ref[...], lam_ref[...])
        row = _seq_rows(nb, lp, w)
        mult = jnp.where(row == PAD, 1.0, mult)
        a_s[...] = a
        b_s[...] = jnp.where(row < PAD, 0.0, mult * (ia * xc))
        r8 = _scan_block_rows(w)

        def blk(i, carry):
            out = []
            for s_id in range(nb):
                off = pl.multiple_of(s_id * lp + i * 8, 8)
                av = a_s[pl.ds(off, 8), :]
                bv = b_s[pl.ds(off, 8), :]
                for sh in (1, 2, 4):
                    keep = r8 >= sh
                    bv = jnp.where(keep, av * pltpu.roll(bv, sh, axis=0) + bv, bv)
                    av = jnp.where(keep, av * pltpu.roll(av, sh, axis=0), av)
                hv = bv + av * carry[s_id]
                h_ref[pl.ds(off, 8), :] = hv
                out.append(jnp.sum(jnp.where(r8 == 7, hv, 0.0), axis=0, keepdims=True))
            return tuple(out)

        lax.fori_loop(0, nblk, blk, tuple(jnp.zeros((1, w), F32) for _ in range(nb)))
        gelu, _ = _gelu_and_grad(gt_ref[...])
        y_ref[...] = h_ref[...] * gelu

    col = lambda c: (0, c)
    return pl.pallas_call(
        body, name="lru_fwd", grid=(nt,),
        in_specs=[pl.BlockSpec((n, w), col), pl.BlockSpec((n, w), lambda c: (0, nt + c)),
                  pl.BlockSpec((CONV_W, w), col), pl.BlockSpec((1, w), col),
                  pl.BlockSpec((w, w), lambda c: (c, c)), pl.BlockSpec((w, w), lambda c: (c, c)),
                  pl.BlockSpec((1, w), col), pl.BlockSpec((1, w), col), pl.BlockSpec((1, w), col)],
        out_specs=[pl.BlockSpec((n, w), col), pl.BlockSpec((n, w), col)],
        out_shape=[jax.ShapeDtypeStruct((n, LRU_WIDTH), F32), jax.ShapeDtypeStruct((n, LRU_WIDTH), F32)],
        scratch_shapes=[pltpu.VMEM((n, w), F32), pltpu.VMEM((n, w), F32)],
        compiler_params=_params(("parallel",)),
    )(zl, zl, conv_w, conv_b, wa, wx, ba, bx, lam)


def lru_bwd(zl, hs, dy, conv_w, conv_b, wa, wx, ba, bx, lam, nb, lp, comm=None):
    n = zl.shape[0]
    w = LRU_TILE
    nt = LRU_WIDTH // w
    nblk = lp // 8

    def body(u_ref, gt_ref, h_ref, dy_ref, cw_ref, cb_ref, wa_ref, wx_ref, ba_ref, bx_ref, lam_ref,
             du_ref, dgt_ref, dcw_ref, dcb_ref, dba_ref, dbx_ref, dlam_ref, dwa_ref, dwx_ref,
             c_s, d_s, g_s, dwa_s, dwx_s):
        u = u_ref[...]
        lam = lam_ref[...]
        xc, xc16, ra, ia, sp, a, mult = _lru_gates(u, cw_ref, cb_ref[...], wa_ref[...], wx_ref[...],
                                                   ba_ref[...], bx_ref[...], lam)
        row = lax.broadcasted_iota(jnp.int32, (lp, w), 0)
        hv = h_ref[...]
        dyv = dy_ref[...]
        gelu, dgelu = _gelu_and_grad(gt_ref[...])
        dgt_ref[...] = jnp.where(row >= PAD, dyv * hv * dgelu, 0.0)
        c_s[...] = pltpu.roll(a, lp - 1, axis=0)
        d_s[...] = dyv * gelu
        r8 = _scan_block_rows(w)

        def blk(ii, carry):
            off = pl.multiple_of((nblk - 1 - ii) * 8, 8)
            cv = c_s[pl.ds(off, 8), :]
            dv = d_s[pl.ds(off, 8), :]
            for sh in (1, 2, 4):
                keep = r8 < 8 - sh
                dv = jnp.where(keep, cv * pltpu.roll(dv, 8 - sh, axis=0) + dv, dv)
                cv = jnp.where(keep, cv * pltpu.roll(cv, 8 - sh, axis=0), cv)
            gv = dv + cv * carry
            g_s[pl.ds(off, 8), :] = gv
            return jnp.sum(jnp.where(r8 == 0, gv, 0.0), axis=0, keepdims=True)

        lax.fori_loop(0, nblk, blk, jnp.zeros((1, w), F32))
        gv = g_s[...]
        first_row = row == PAD
        db = jnp.where(row >= PAD, gv, 0.0)
        da = jnp.where(row > PAD, gv * pltpu.roll(hv, 1, axis=0), 0.0)
        mult_eff = jnp.where(first_row, 1.0, mult)
        dmult = jnp.where(first_row, 0.0, db * (ia * xc))
        dia = db * mult_eff * xc
        dxc = db * mult_eff * ia
        dla = da * a - dmult * (a * a) / mult
        dra = dla * (-C_RGLRU * sp)
        dsp = jnp.sum(dla * (-C_RGLRU * ra), axis=0, keepdims=True)
        dpa = dra * ra * (1.0 - ra)
        dpx = dia * ia * (1.0 - ia)
        dpa16 = dpa.astype(BF16)
        dpx16 = dpx.astype(BF16)
        dxc = dxc + _nt(dpa16, wa_ref[...]) + _nt(dpx16, wx_ref[...])
        du = cw_ref[pl.ds(CONV_W - 1, 1), :] * dxc
        dcw = [jnp.sum(dxc * u, axis=0, keepdims=True)]
        for tap in range(1, CONV_W):
            dcw.insert(0, jnp.sum(dxc * pltpu.roll(u, tap, axis=0), axis=0, keepdims=True))
            du = du + cw_ref[pl.ds(CONV_W - 1 - tap, 1), :] * pltpu.roll(dxc, lp - tap, axis=0)
        du_ref[...] = jnp.where(row >= PAD, du, 0.0)
        first = pl.program_id(1) == 0
        _accumulate(dlam_ref, -_sig(-lam) * dsp, first)
        _accumulate(dba_ref, jnp.sum(dpa, axis=0, keepdims=True), first)
        _accumulate(dbx_ref, jnp.sum(dpx, axis=0, keepdims=True), first)
        _accumulate(dcb_ref, jnp.sum(dxc, axis=0, keepdims=True), first)
        _accumulate(dcw_ref, jnp.concatenate(dcw, axis=0), first)
        _accumulate(dwa_s, _tn(xc16, dpa16), first)
        _accumulate(dwx_s, _tn(xc16, dpx16), first)

        @pl.when(pl.program_id(1) == nb - 1)
        def _():
            for j in range(w // LRU_BLOCK):
                blk_rows = slice(j * LRU_BLOCK, (j + 1) * LRU_BLOCK)
                dwa_ref[0, blk_rows, :] = dwa_s[blk_rows, blk_rows]
                dwx_ref[0, blk_rows, :] = dwx_s[blk_rows, blk_rows]

    col = lambda c, b: (0, c)
    vec = pl.BlockSpec((1, w), col)
    mat = pl.BlockSpec((w, w), lambda c, b: (c, c))
    big = pl.BlockSpec((lp, w), lambda c, b: (b, c))
    dmat = pl.BlockSpec((1, w, LRU_BLOCK), lambda c, b: (c, 0, 0))
    return _call(
        body, "lru_bwd", (nt, nb),
        [big, pl.BlockSpec((lp, w), lambda c, b: (b, nt + c)), big, big,
         pl.BlockSpec((CONV_W, w), col), vec, mat, mat, vec, vec, vec],
        [big, big, pl.BlockSpec((CONV_W, w), col), vec, vec, vec, vec, dmat, dmat],
        [jax.ShapeDtypeStruct((n, LRU_WIDTH), F32), jax.ShapeDtypeStruct((n, LRU_WIDTH), F32),
         jax.ShapeDtypeStruct((CONV_W, LRU_WIDTH), F32), jax.ShapeDtypeStruct((1, LRU_WIDTH), F32),
         jax.ShapeDtypeStruct((1, LRU_WIDTH), F32), jax.ShapeDtypeStruct((1, LRU_WIDTH), F32),
         jax.ShapeDtypeStruct((1, LRU_WIDTH), F32), jax.ShapeDtypeStruct((nt, w, LRU_BLOCK), F32),
         jax.ShapeDtypeStruct((nt, w, LRU_BLOCK), F32)],
        ("parallel", "arbitrary"), (zl, zl, hs, dy, conv_w, conv_b, wa, wx, ba, bx, lam),
        scratch=[pltpu.VMEM((lp, w), F32), pltpu.VMEM((lp, w), F32), pltpu.VMEM((lp, w), F32),
                 pltpu.VMEM((w, w), F32), pltpu.VMEM((w, w), F32)], comm=comm)


def outproj_fwd(h, ya, yl, gao, glo, wout):
    n, d = h.shape
    half = ya.shape[1]
    tm = _tile(n, 352)

    def body(h_ref, ya_ref, yl_ref, gao_ref, glo_ref, w_ref, ho_ref, yn_ref):
        xa = ya_ref[...]
        xl = yl_ref[...]
        na = (xa * _rms_r(xa) * gao_ref[...]).astype(BF16)
        nl = (xl * _rms_r(xl) * glo_ref[...]).astype(BF16)
        yn_ref[:, :half] = na
        yn_ref[:, half:] = nl
        ho_ref[...] = h_ref[...] + _nn(na, w_ref[:half, :]) + _nn(nl, w_ref[half:, :])

    return pl.pallas_call(
        body, name="outproj_fwd", grid=(n // tm,),
        in_specs=[_row(tm, d), _row(tm, half), _row(tm, half), _fixed((1, half)), _fixed((1, half)), VMEM_WHOLE],
        out_specs=[_row(tm, d), _row(tm, 2 * half)],
        out_shape=[jax.ShapeDtypeStruct((n, d), F32), jax.ShapeDtypeStruct((n, 2 * half), BF16)],
        compiler_params=_params(("parallel",)),
    )(h, ya, yl, gao, glo, wout)


def outproj_bwd(dh, ya, yl, gao, glo, wout):
    n, d = dh.shape
    half = ya.shape[1]
    tm = _tile(n, 352)

    def body(dh_ref, ya_ref, yl_ref, gao_ref, glo_ref, w_ref, dya_ref, dyl_ref, dgao_ref, dglo_ref):
        d16 = dh_ref[...].astype(BF16)
        xa = ya_ref[...]
        xl = yl_ref[...]
        dxa, dga = _rms_bwd(xa, _rms_r(xa), gao_ref[...], _nt(d16, w_ref[:half, :]))
        dxl, dgl = _rms_bwd(xl, _rms_r(xl), glo_ref[...], _nt(d16, w_ref[half:, :]))
        dya_ref[...] = dxa
        dyl_ref[...] = dxl
        first = pl.program_id(0) == 0
        _accumulate(dgao_ref, dga, first)
        _accumulate(dglo_ref, dgl, first)

    return pl.pallas_call(
        body, name="outproj_bwd", grid=(n // tm,),
        in_specs=[_row(tm, d), _row(tm, half), _row(tm, half), _fixed((1, half)), _fixed((1, half)), VMEM_WHOLE],
        out_specs=[_row(tm, half), _row(tm, half), _fixed((1, half)), _fixed((1, half))],
        out_shape=[jax.ShapeDtypeStruct((n, half), F32), jax.ShapeDtypeStruct((n, half), F32),
                   jax.ShapeDtypeStruct((1, half), F32), jax.ShapeDtypeStruct((1, half), F32)],
        compiler_params=_params(("arbitrary",)),
    )(dh, ya, yl, gao, glo, wout)


def final_loss(h, g, tgt, lp):
    n, d = h.shape
    tm = _tile(lp, 352)
    per_seq = lp // tm

    def body(h_ref, g_ref, t_ref, loss_ref, dh_ref, dg_ref):
        i = pl.program_id(0)
        x = h_ref[...]
        gv = g_ref[...]
        r = _rms_r(x)
        row = (i % per_seq) * tm + lax.broadcasted_iota(jnp.int32, (tm, d), 0)
        diff = jnp.where(row >= FIRST_FRAME, x * r * gv - t_ref[...], 0.0)
        part = 0.5 * jnp.sum(jnp.sum(diff * diff, axis=-1, keepdims=True) * (1.0 / d), axis=0, keepdims=True)
        dx, dg = _rms_bwd(x, r, gv, diff * (1.0 / d))
        dh_ref[...] = dx
        _accumulate(loss_ref, jnp.broadcast_to(part, (1, LANES)), i == 0)
        _accumulate(dg_ref, dg, i == 0)

    return pl.pallas_call(
        body, name="final_loss", grid=(n // tm,),
        in_specs=[_row(tm, d), _fixed((1, d)), _row(tm, d)],
        out_specs=[_fixed((1, LANES)), _row(tm, d), _fixed((1, d))],
        out_shape=[jax.ShapeDtypeStruct((1, LANES), F32), jax.ShapeDtypeStruct((n, d), F32),
                   jax.ShapeDtypeStruct((1, d), F32)],
        compiler_params=_params(("arbitrary",)),
    )(h, g, tgt)


def assemble_cols(g, name):
    _, k, ns = g.shape

    def body(g_ref, o_ref):
        for j in range(N_DEV):
            o_ref[:, j * ns:(j + 1) * ns] = g_ref[j]

    return pl.pallas_call(body, name=name, out_shape=jax.ShapeDtypeStruct((k, N_DEV * ns), g.dtype),
                          compiler_params=_params(None))(g)


def split_cols(x, name):
    k, cols = x.shape
    ns = cols // N_DEV

    def body(x_ref, o_ref):
        for j in range(N_DEV):
            o_ref[j] = x_ref[:, j * ns:(j + 1) * ns]

    return pl.pallas_call(body, name=name, out_shape=jax.ShapeDtypeStruct((N_DEV, k, ns), x.dtype),
                          compiler_params=_params(None))(x)


def _slab_rows(w, per_head):
    k = w.shape[1]
    w = w.reshape(MLA_HEADS, per_head, k)
    return jnp.pad(w, ((0, 0), (0, HEAD_SLAB - per_head), (0, 0))).reshape(MLA_HEADS * HEAD_SLAB, k)


def _unslab_rows(w, per_head):
    k = w.shape[1]
    return w.reshape(MLA_HEADS, HEAD_SLAB, k)[:, :per_head].reshape(MLA_HEADS * per_head, k)


def meta_grad(dh0, nb, lp):
    d = dh0.shape[1]
    ns = d // N_DEV
    per_seq = lp // N_META

    def body(x_ref, o_ref):
        x = x_ref[...]
        for j in range(N_DEV):
            _accumulate(o_ref.at[j], x[:, j * ns:(j + 1) * ns], pl.program_id(0) == 0)

    return pl.pallas_call(
        body, name="meta_grad", grid=(nb,),
        in_specs=[pl.BlockSpec((N_META, d), lambda b: (b * per_seq + PAD // N_META, 0))],
        out_specs=pl.BlockSpec((N_DEV, N_META, ns), lambda b: (0, 0, 0)),
        out_shape=jax.ShapeDtypeStruct((N_DEV, N_META, ns), F32),
        compiler_params=_params(("arbitrary",)))(dh0)


VECTORS = [("ffn1_norm", 1024), ("mix_norm", 1024), ("q_latent_norm", 384), ("kv_latent_norm", 256),
           ("q_head_norm", 192), ("k_head_norm", 192), ("conv_b", 512), ("gate_a_b", 512), ("gate_x_b", 512),
           ("lru_lambda", 512), ("attn_out_norm", 512), ("lru_out_norm", 512), ("ffn2_norm", 1024),
           ("final_norm", 1024)]
VEC_ROWS = 16
GATES = ["gate_a_w", "gate_x_w"]


def pack_vectors(grads):
    def body(*refs):
        o_ref = refs[-1]
        o_ref[...] = jnp.zeros_like(o_ref)
        for t, (ref, (_, cnt)) in enumerate(zip(refs[:-1], VECTORS)):
            o_ref[t:t + 1, :cnt] = ref[:, :cnt]

    return pl.pallas_call(body, name="pack_vectors", out_shape=jax.ShapeDtypeStruct((VEC_ROWS, D_MODEL), F32),
                          compiler_params=_params(None))(*[grads[name] for name, _ in VECTORS])


def _adamw_update(w, g, m, v):
    c1 = 1.0 / (1.0 - ADAM_B1 ** ADAM_STEP)
    c2 = 1.0 / (1.0 - ADAM_B2 ** ADAM_STEP)
    mn = ADAM_B1 * m + (1.0 - ADAM_B1) * g
    vn = ADAM_B2 * v + (1.0 - ADAM_B2) * (g * g)
    delta = -ADAM_LR * ((mn * c1) / (jnp.sqrt(vn * c2) + ADAM_EPS) + ADAM_WD * w)
    return delta, mn, vn


def _sum_slots(ref, index=()):
    acc = ref[(0,) + index].astype(F32)
    for s in range(1, N_DEV):
        acc = acc + ref[(s,) + index].astype(F32)
    return acc


def adamw_sharded(r, w, m, v, name):
    rows, cols = w.shape
    tr = _tile(rows, 256, 16) if rows % 16 == 0 else rows

    def body(r_ref, w_ref, m_ref, v_ref, g_ref, d_ref, mo_ref, vo_ref):
        g = _sum_slots(r_ref)
        g_ref[...] = g
        d_ref[...], mo_ref[...], vo_ref[...] = _adamw_update(w_ref[...], g, m_ref[...], v_ref[...])

    spec = pl.BlockSpec((tr, cols), lambda i: (i, 0))
    shape = jax.ShapeDtypeStruct((rows, cols), F32)
    return pl.pallas_call(
        body, name=name, grid=(rows // tr,),
        in_specs=[pl.BlockSpec((N_DEV, tr, cols), lambda i: (0, i, 0))] + [spec] * 3,
        out_specs=[spec] * 4, out_shape=[shape] * 4,
        compiler_params=_params(("parallel",)),
    )(r, w, m, v)


def adamw_small(r_vec, r_gates, w, m, v):
    nt = len(VECTORS) + len(GATES)

    def body(*refs):
        rv_ref = refs[0]
        rg_refs = refs[1:1 + len(GATES)]
        base = 1 + len(GATES)
        w_refs, m_refs, v_refs = (refs[base + i * nt:base + (i + 1) * nt] for i in range(3))
        outs = refs[base + 3 * nt:]
        g_o, d_o, m_o, v_o = (outs[i * nt:(i + 1) * nt] for i in range(4))
        for t in range(nt):
            if t < len(VECTORS):
                cnt = VECTORS[t][1]
                g = _sum_slots(rv_ref, (slice(t, t + 1), slice(0, cnt)))
            else:
                g = _sum_slots(rg_refs[t - len(VECTORS)])
            g_o[t][...] = g
            d_o[t][...], m_o[t][...], v_o[t][...] = _adamw_update(w_refs[t][...], g, m_refs[t][...], v_refs[t][...])

    shapes = [jax.ShapeDtypeStruct(a.shape, F32) for a in w]
    res = pl.pallas_call(body, name="adamw_small", out_shape=shapes * 4,
                         compiler_params=_params(None))(r_vec, *r_gates, *w, *m, *v)
    return [res[i * nt:(i + 1) * nt] for i in range(4)]


def _block_diag(w):
    nb, n, _ = w.shape
    eye = jnp.eye(nb, dtype=w.dtype)
    return (eye[:, None, :, None] * w[:, :, None, :]).reshape(nb * n, nb * n)


def _two_d(a):
    if a.ndim == 3:
        return a.reshape(a.shape[1], a.shape[2])
    if a.ndim == 4:
        return a.reshape(a.shape[1] * a.shape[2], a.shape[3])
    return a


_WEIGHT_NAMES = ['meta_tokens', 'ffn1_norm', 'ffn1_w_gate', 'ffn1_w_up', 'ffn1_w_down', 'mix_norm', 'w_in',
                 'q_latent_norm', 'w_uq', 'kv_latent_norm', 'w_uk', 'w_uv', 'q_head_norm', 'k_head_norm', 'conv_w',
                 'conv_b', 'gate_a_w', 'gate_a_b', 'gate_x_w', 'gate_x_b', 'lru_lambda', 'attn_out_norm',
                 'lru_out_norm', 'w_out', 'ffn2_norm', 'ffn2_w_gate', 'ffn2_w_up', 'ffn2_w_down', 'final_norm']


COLUMN_SHARDED = ("ffn1_w_gate", "ffn1_w_up", "ffn2_w_gate", "ffn2_w_up", "w_in", "w_uq", "w_uk", "w_uv")


def train_step(x, tgt, w, m, v):
    nb, seq, d = x.shape
    lp = PAD + N_META + seq
    n = nb * lp
    def local(a, name):
        a = _two_d(a)
        return a.T if name in COLUMN_SHARDED else a

    sh = {name: local(w[name], name) for name in _WEIGHT_NAMES}
    m2 = {name: local(m[name], name) for name in _WEIGHT_NAMES}
    v2 = {name: local(v[name], name) for name in _WEIGHT_NAMES}

    def b16(name):
        return sh[name].astype(BF16)

    out = {}

    def update(name, landed):
        out[name] = adamw_sharded(landed, sh[name], m2[name], v2[name], "adamw_" + name)

    g_wg1, g_wu1, g_wd1, g_meta, g_conv = exchange(
        [b16("ffn1_w_gate"), b16("ffn1_w_up"), b16("ffn1_w_down"), sh["meta_tokens"], sh["conv_w"]],
        ["gather"] * 5, "gather_ffn1")
    wg1, wu1, wd1 = (g.reshape(D_FF, d) for g in (g_wg1, g_wu1, g_wd1))
    meta = assemble_cols(g_meta, "assemble_meta")
    conv_w = assemble_cols(g_conv, "assemble_conv")

    front = jnp.concatenate([jnp.zeros((PAD, d), F32), meta], axis=0)
    h0 = jnp.concatenate([jnp.broadcast_to(front[None], (nb, FIRST_FRAME, d)), x], axis=1).reshape(n, d)
    tgt_p = jnp.concatenate([jnp.zeros((nb, FIRST_FRAME, d), F32), tgt], axis=1).reshape(n, d)
    tables = _rope_tables(lp)
    zero_tail = jnp.zeros((1, HEAD_SLAB - D_QK), F32)
    gqh = jnp.concatenate([sh["q_head_norm"], zero_tail], axis=1)
    gkh = jnp.concatenate([sh["k_head_norm"], zero_tail], axis=1)
    wa = _block_diag(w["gate_a_w"][0]).astype(BF16)
    wx = _block_diag(w["gate_x_w"][0]).astype(BF16)

    (h1, u1, a1, b1), (g_in, g_uq, g_uk, g_uv, g_out, g_wg2) = ffn_fwd(
        h0, sh["ffn1_norm"], wg1, wu1, wd1, "ffn1_fwd",
        comm=([b16("w_in"), b16("w_uq"), b16("w_uk"), b16("w_uv"), b16("w_out"), b16("ffn2_w_gate")], ["gather"] * 6))
    mla_rows = MLA_IN - D_ROPE
    w_in = g_in.reshape(mla_rows + 2 * LRU_WIDTH, d)
    wm = jnp.concatenate([w_in[:mla_rows], jnp.zeros((D_ROPE, d), BF16)], axis=0)
    wl = w_in[mla_rows:]
    wuq = _slab_rows(g_uq.reshape(MLA_HEADS * D_QK, Q_RANK), D_QK)
    wuk = _slab_rows(g_uk.reshape(MLA_HEADS * D_NOPE, KV_RANK), D_NOPE)
    wuv = g_uv.reshape(MLA_HEADS * D_V, KV_RANK)
    w_out = g_out.reshape(d, d)

    u2, zm, zl = inproj_fwd(h1, sh["mix_norm"], wm, wl)
    q, k, vv, qn, cn = mla_prep_fwd(zm, sh["q_latent_norm"], sh["kv_latent_norm"], wuq, wuk, wuv, gqh, gkh, tables, lp)
    (y_mla, lse), (g_wu2, g_wd2) = attn_fwd(
        q, k, vv, nb, lp, comm=([b16("ffn2_w_up"), b16("ffn2_w_down")], ["gather"] * 2))
    wg2, wu2, wd2 = (g.reshape(D_FF, d) for g in (g_wg2, g_wu2, g_wd2))
    y_lru, hs = lru_fwd(zl, conv_w, sh["conv_b"], wa, wx, sh["gate_a_b"], sh["gate_x_b"], sh["lru_lambda"], nb, lp)
    h2, yn = outproj_fwd(h1, y_mla, y_lru, sh["attn_out_norm"], sh["lru_out_norm"], w_out)
    (h3, u3, a3, b3), _ = ffn_fwd(h2, sh["ffn2_norm"], wg2, wu2, wd2, "ffn2_fwd")
    loss, dh3, g_final = final_loss(h3, sh["final_norm"], tgt_p, lp)

    vec = {"final_norm": g_final}
    (dh2, da3, db3, sh3, vec["ffn2_norm"]), _ = ffn_bwd_act(dh3, h2, sh["ffn2_norm"], a3, b3, wg2, wu2, wd2, "ffn2_bwd")
    ff_shards = (N_DEV, D_FF // N_DEV, d)
    dwg2 = tn_matmul(da3, u3, "ffn2_dwg", "bf16").reshape(ff_shards)
    dwu2 = tn_matmul(db3, u3, "ffn2_dwu", "bf16").reshape(ff_shards)
    dwd2 = tn_matmul(sh3, dh3, "ffn2_dwd", "bf16").reshape(ff_shards)

    dy_mla, dy_lru, vec["attn_out_norm"], vec["lru_out_norm"] = outproj_bwd(
        dh2, y_mla, y_lru, sh["attn_out_norm"], sh["lru_out_norm"], w_out)
    dw_out = tn_matmul(yn, dh2, "dw_out", "bf16").reshape(N_DEV, d // N_DEV, d)
    (du, dgate, dconv, vec["conv_b"], vec["gate_a_b"], vec["gate_x_b"], vec["lru_lambda"], dga, dgx), (r_wg2,) = lru_bwd(
        zl, hs, dy_lru, conv_w, sh["conv_b"], wa, wx, sh["gate_a_b"], sh["gate_x_b"], sh["lru_lambda"], nb, lp,
        comm=([dwg2], ["scatter"]))
    update("ffn2_w_gate", r_wg2)

    (dq, dk, dv), (r_wu2,) = attn_bwd(q, k, vv, y_mla, dy_mla, lse, nb, lp, comm=([dwu2], ["scatter"]))
    update("ffn2_w_up", r_wu2)

    (dzm, dqr, dkr, vec["q_latent_norm"], vec["kv_latent_norm"], vec["q_head_norm"], vec["k_head_norm"]), (r_wd2,) = (
        mla_prep_bwd(dq, dk, dv, zm, qn, cn, sh["q_latent_norm"], sh["kv_latent_norm"], wuq, wuk, wuv, gqh, gkh,
                     tables, lp, comm=([dwd2], ["scatter"])))
    update("ffn2_w_down", r_wd2)
    dwuq = _unslab_rows(tn_matmul(dqr, qn, "dw_uq"), D_QK).reshape(N_DEV, -1, Q_RANK)
    dwuk = _unslab_rows(tn_matmul(dkr, cn, "dw_uk"), D_NOPE).reshape(N_DEV, -1, KV_RANK)
    dwuv = tn_matmul(dv, cn, "dw_uv").reshape(N_DEV, -1, KV_RANK)
    dh1, vec["mix_norm"] = inproj_bwd(dzm, du, dgate, dh2, h1, sh["mix_norm"], wm, wl)
    dw_in = jnp.concatenate([tn_matmul(dzm, u2, "dw_in_mla")[:mla_rows], tn_matmul(du, u2, "dw_in_u"),
                             tn_matmul(dgate, u2, "dw_in_gate")], axis=0).reshape(N_DEV, -1, d)

    (dh0, da1, db1, sh1, vec["ffn1_norm"]), landed = ffn_bwd_act(
        dh1, h0, sh["ffn1_norm"], a1, b1, wg1, wu1, wd1, "ffn1_bwd",
        comm=([dw_in, dwuq, dwuk, dwuv, dw_out, split_cols(dconv, "split_conv")], ["scatter"] * 6))
    for name, r in zip(("w_in", "w_uq", "w_uk", "w_uv", "w_out", "conv_w"), landed):
        update(name, r)

    dwg1 = tn_matmul(da1, u1, "ffn1_dwg", "bf16").reshape(ff_shards)
    dwu1, (r_wg1,) = tn_matmul(db1, u1, "ffn1_dwu", "bf16", comm=([dwg1], ["scatter"]))
    dwd1, (r_wu1,) = tn_matmul(sh1, dh1, "ffn1_dwd", "bf16", comm=([dwu1.reshape(ff_shards)], ["scatter"]))
    dwd1 = dwd1.reshape(ff_shards)
    dmeta = meta_grad(dh0, nb, lp)
    gates = [dga.reshape(LRU_WIDTH, LRU_BLOCK), dgx.reshape(LRU_WIDTH, LRU_BLOCK)]
    r_wd1, r_meta, r_vec, r_ga, r_gx = exchange(
        [dwd1, dmeta, pack_vectors(vec)] + gates, ["scatter"] * 2 + ["gather"] * 3, "exchange_last")
    update("ffn1_w_gate", r_wg1)
    update("ffn1_w_up", r_wu1)
    update("ffn1_w_down", r_wd1)
    update("meta_tokens", r_meta)

    small = [name for name, _ in VECTORS] + GATES
    res = adamw_small(r_vec, [r_ga, r_gx], [sh[nm] for nm in small], [m2[nm] for nm in small], [v2[nm] for nm in small])
    for i, name in enumerate(small):
        out[name] = [res[j][i] for j in range(4)]

    grad_x = dh0.reshape(nb, lp, d)[:, FIRST_FRAME:]
    loss = lax.psum(loss[0, 0], ("x", "y", "c"))
    def as_given(a, name):
        return (a.T if name in COLUMN_SHARDED else a).reshape(w[name].shape)

    cols = [[as_given(out[name][j], name) for name in _WEIGHT_NAMES] for j in range(4)]
    return (loss, grad_x, *cols[0], *cols[1], *cols[2], *cols[3])


def kernel(x, meta_tokens, ffn1_norm, ffn1_w_gate, ffn1_w_up, ffn1_w_down, mix_norm, w_in, q_latent_norm, w_uq, kv_latent_norm, w_uk, w_uv, q_head_norm, k_head_norm, conv_w, conv_b, gate_a_w, gate_a_b, gate_x_w, gate_x_b, lru_lambda, attn_out_norm, lru_out_norm, w_out, ffn2_norm, ffn2_w_gate, ffn2_w_up, ffn2_w_down, final_norm, loss_target, m_meta_tokens, m_ffn1_norm, m_ffn1_w_gate, m_ffn1_w_up, m_ffn1_w_down, m_mix_norm, m_w_in, m_q_latent_norm, m_w_uq, m_kv_latent_norm, m_w_uk, m_w_uv, m_q_head_norm, m_k_head_norm, m_conv_w, m_conv_b, m_gate_a_w, m_gate_a_b, m_gate_x_w, m_gate_x_b, m_lru_lambda, m_attn_out_norm, m_lru_out_norm, m_w_out, m_ffn2_norm, m_ffn2_w_gate, m_ffn2_w_up, m_ffn2_w_down, m_final_norm, v_meta_tokens, v_ffn1_norm, v_ffn1_w_gate, v_ffn1_w_up, v_ffn1_w_down, v_mix_norm, v_w_in, v_q_latent_norm, v_w_uq, v_kv_latent_norm, v_w_uk, v_w_uv, v_q_head_norm, v_k_head_norm, v_conv_w, v_conv_b, v_gate_a_w, v_gate_a_b, v_gate_x_w, v_gate_x_b, v_lru_lambda, v_attn_out_norm, v_lru_out_norm, v_w_out, v_ffn2_norm, v_ffn2_w_gate, v_ffn2_w_up, v_ffn2_w_down, v_final_norm):
    args = locals()
    w = {name: args[name] for name in _WEIGHT_NAMES}
    m = {name: args["m_" + name] for name in _WEIGHT_NAMES}
    v = {name: args["v_" + name] for name in _WEIGHT_NAMES}
    return train_step(x, loss_target, w, m, v)
```

```python
import math

import jax
import jax.numpy as jnp
from jax import lax
from jax.experimental import pallas as pl
from jax.experimental.pallas import tpu as pltpu

F32 = jnp.float32
BF16 = jnp.bfloat16

D_MODEL = 1024
CHUNK = 64
CHUNK_SHIFT = 6
N_META = 16
PAD = CHUNK - N_META
FIRST_FRAME = PAD + N_META
MLA_HEADS = 4
D_NOPE = 128
D_ROPE = 64
D_QK = D_NOPE + D_ROPE
D_V = 128
HEAD_SLAB = 256
KV_RANK = 256
Q_RANK = 384
ROPE_THETA = 10000.0
LRU_WIDTH = 512
LRU_BLOCKS = 8
LRU_BLOCK = 64
LRU_TILE = 128
CONV_W = 4
C_RGLRU = 8.0
D_FF = 2816
MLA_IN = 768
EPS = 1e-6
NEG_INF = -1e30
N_DEV = 8
LANES = 128
VMEM_LIMIT = 52 * 1024 * 1024
TN_ROWS = 4224
TN_X_BYTES = 12 * 1024 * 1024
TN_Y_BYTES = 9 * 1024 * 1024 // 2

ADAM_LR = 0.001
ADAM_B1 = 0.9
ADAM_B2 = 0.999
ADAM_EPS = 1e-08
ADAM_WD = 0.01
ADAM_STEP = 10

VMEM_WHOLE = pl.BlockSpec(memory_space=pltpu.VMEM)
HBM_WHOLE = pl.BlockSpec(memory_space=pl.ANY)


def _params(sems):
    if sems is None:
        return pltpu.CompilerParams(vmem_limit_bytes=VMEM_LIMIT)
    return pltpu.CompilerParams(dimension_semantics=sems, vmem_limit_bytes=VMEM_LIMIT)


def _tile(n, cap, mult=16):
    best = None
    for t in range(mult, min(n, cap) + 1, mult):
        if n % t == 0:
            best = t
    assert best is not None, (n, cap, mult)
    return best


def _row(tm, d):
    return pl.BlockSpec((tm, d), lambda i: (i, 0))


def _fixed(shape):
    return pl.BlockSpec(shape, lambda i: (0,) * len(shape))


def _mesh_position():
    return lax.axis_index("x"), lax.axis_index("y"), lax.axis_index("c")


def _flat_index(x, y, c):
    return 4 * x + 2 * y + c


def _peers(x, y, c):
    out = []
    for k in range(1, N_DEV):
        fx, fy, fc = (k >> 2) & 1, (k >> 1) & 1, k & 1
        out.append((1 - x if fx else x, 1 - y if fy else y, 1 - c if fc else c))
    return out


def _comm_out_shapes(srcs, modes):
    return [jax.ShapeDtypeStruct((N_DEV,) + s.shape if md == "gather" else s.shape, s.dtype)
            for s, md in zip(srcs, modes)]


def _comm_scratch(n):
    per_peer = n * (N_DEV - 1)
    return [pltpu.SemaphoreType.DMA((per_peer,)), pltpu.SemaphoreType.DMA((per_peer,)), pltpu.SemaphoreType.DMA((n,))]


class _Copies:
    def __init__(self, own, first, relay):
        self.own, self.first, self.relay = own, first, relay

    def start(self):
        for cp in self.own + self.first:
            cp.start()

    def forward(self):
        for arrival, onward in self.relay:
            arrival.wait_recv()
            onward.start()

    def finish(self):
        arrivals = [a for a, _ in self.relay]
        onward = [f for _, f in self.relay]
        for cp in self.first + onward:
            if not any(cp is a for a in arrivals):
                cp.wait_recv()
        for cp in self.first + onward:
            cp.wait_send()
        for cp in self.own:
            cp.wait()


def _comm_copies(src_refs, dst_refs, modes, send, recv, local):
    x, y, c = _mesh_position()
    me = _flat_index(x, y, c)
    n = len(modes)
    sibling = (x, y, 1 - c)
    chips = [(1 - x, y), (x, 1 - y), (1 - x, 1 - y)]

    def remote(src, dst, k, t, to):
        return pltpu.make_async_remote_copy(src_ref=src, dst_ref=dst, send_sem=send.at[k * n + t],
                                            recv_sem=recv.at[k * n + t], device_id=to,
                                            device_id_type=pl.DeviceIdType.MESH)

    own, first, relay = [], [], []
    for t, (src, dst, md) in enumerate(zip(src_refs, dst_refs, modes)):
        if md == "scatter":
            own.append(pltpu.make_async_copy(src.at[me], dst.at[me], local.at[t]))
            for k, peer in enumerate(_peers(x, y, c)):
                first.append(remote(src.at[_flat_index(*peer)], dst.at[me], k, t, peer))
        else:
            own.append(pltpu.make_async_copy(src, dst.at[me], local.at[t]))
            first.append(remote(src, dst.at[me], 0, t, sibling))
            for j, chip in enumerate(chips):
                arrival = remote(src, dst.at[me], 1 + j, t, (*chip, c))
                landed = dst.at[_flat_index(*chip, c)]
                first.append(arrival)
                relay.append((arrival, remote(landed, landed, 4 + j, t, sibling)))
    return _Copies(own, first, relay)


def _hosted(body, n_in, n_out, modes, grid):
    t = len(modes)
    total = math.prod(grid)

    def wrapped(*refs):
        ins, csrc = refs[:n_in], refs[n_in:n_in + t]
        outs = refs[n_in + t:n_in + t + n_out]
        cdst = refs[n_in + t + n_out:n_in + 2 * t + n_out]
        scratch = refs[n_in + 2 * t + n_out:-3]
        copies = _comm_copies(csrc, cdst, modes, *refs[-3:])
        step = pl.program_id(0)
        for axis in range(1, len(grid)):
            step = step * grid[axis] + pl.program_id(axis)

        @pl.when(step == 0)
        def _():
            copies.start()

        body(*ins, *outs, *scratch)

        @pl.when(step == (total * 3) // 5)
        def _():
            copies.forward()

        @pl.when(step == total - 1)
        def _():
            copies.finish()

    return wrapped


def _call(body, name, grid, in_specs, out_specs, out_shape, sems, args, scratch=(), comm=None):
    if comm is None:
        outs = pl.pallas_call(body, name=name, grid=grid, in_specs=in_specs, out_specs=out_specs, out_shape=out_shape,
                              scratch_shapes=list(scratch), compiler_params=_params(sems))(*args)
        return outs, []
    srcs, modes = comm
    n = len(modes)
    res = pl.pallas_call(
        _hosted(body, len(in_specs), len(out_specs), modes, grid), name=name, grid=grid,
        in_specs=list(in_specs) + [HBM_WHOLE] * n, out_specs=list(out_specs) + [HBM_WHOLE] * n,
        out_shape=list(out_shape) + _comm_out_shapes(srcs, modes),
        scratch_shapes=list(scratch) + _comm_scratch(n),
        compiler_params=_params(("arbitrary",) * len(grid)))(*args, *srcs)
    return res[:len(out_specs)], res[len(out_specs):]


def exchange(srcs, modes, name):
    n = len(modes)

    def body(*refs):
        copies = _comm_copies(refs[:n], refs[n:2 * n], modes, *refs[2 * n:])
        copies.start()
        copies.forward()
        copies.finish()

    return pl.pallas_call(body, name=name, in_specs=[HBM_WHOLE] * n, out_specs=[HBM_WHOLE] * n,
                          out_shape=_comm_out_shapes(srcs, modes), scratch_shapes=_comm_scratch(n))(*srcs)


def _nn(a, b):
    return jnp.dot(a, b, preferred_element_type=F32)


def _nt(a, b):
    return lax.dot_general(a, b, (((1,), (1,)), ((), ())), preferred_element_type=F32)


def _tn(a, b):
    return lax.dot_general(a, b, (((0,), (0,)), ((), ())), preferred_element_type=F32)


def _sig(x):
    return 1.0 / (1.0 + jnp.exp(-x))


def _rms_r(x, n=None):
    n = x.shape[-1] if n is None else n
    return lax.rsqrt(jnp.sum(x * x, axis=-1, keepdims=True) * (1.0 / n) + EPS)


def _rms_bwd(x, r, g, dy, n=None):
    n = x.shape[-1] if n is None else n
    xhat = x * r
    dxhat = dy * g
    dx = r * (dxhat - xhat * (jnp.sum(dxhat * xhat, axis=-1, keepdims=True) * (1.0 / n)))
    return dx, jnp.sum(dy * xhat, axis=0, keepdims=True)


def _accumulate(ref, val, first):
    @pl.when(first)
    def _():
        ref[...] = val

    @pl.when(jnp.logical_not(first))
    def _():
        ref[...] += val


_GELU_C = math.sqrt(2.0 / math.pi)


def _gelu_and_grad(x):
    inner = _GELU_C * (x + 0.044715 * x * x * x)
    t = jnp.tanh(inner)
    gelu = 0.5 * x * (1.0 + t)
    dgelu = 0.5 * (1.0 + t) + 0.5 * x * (1.0 - t * t) * _GELU_C * (1.0 + 3.0 * 0.044715 * x * x)
    return gelu, dgelu


def _log1p_small(t):
    return jnp.where(t < 1e-3, t * (1.0 - t * (0.5 - t * (1.0 / 3.0))), jnp.log(1.0 + t))


def _softplus(x):
    return jnp.maximum(x, 0.0) + _log1p_small(jnp.exp(-jnp.abs(x)))


def _sig_tanh(x):
    return 0.5 + 0.5 * jnp.tanh(0.5 * x)


def _ff_chunks(f):
    return 2 if (f // 2) % LANES == 0 else 1


def ffn_fwd(h, g, wg, wu, wd, name, comm=None):
    n, d = h.shape
    f = wg.shape[0]
    tm = _tile(n, 528)
    fc = 2 * LANES if f % (2 * LANES) == 0 else f
    nc = f // fc

    def body(h_ref, g_ref, wg_ref, wu_ref, wd_ref, ho_ref, u_ref, a_ref, b_ref):
        x = h_ref[...]
        u = (x * _rms_r(x) * g_ref[...]).astype(BF16)
        acc = jnp.zeros((tm, d), F32)
        for c in range(nc):
            cols = slice(c * fc, (c + 1) * fc)
            a = _nt(u, wg_ref[cols, :])
            b = _nt(u, wu_ref[cols, :])
            s = (a * _sig(a) * b).astype(BF16)
            acc = acc + _nn(s, wd_ref[cols, :])
            a_ref[:, cols] = a.astype(BF16)
            b_ref[:, cols] = b.astype(BF16)
        ho_ref[...] = x + 0.5 * acc
        u_ref[...] = u

    return _call(
        body, name, (n // tm,),
        [_row(tm, d), _fixed((1, d)), VMEM_WHOLE, VMEM_WHOLE, VMEM_WHOLE],
        [_row(tm, d), _row(tm, d), _row(tm, f), _row(tm, f)],
        [jax.ShapeDtypeStruct((n, d), F32), jax.ShapeDtypeStruct((n, d), BF16),
         jax.ShapeDtypeStruct((n, f), BF16), jax.ShapeDtypeStruct((n, f), BF16)],
        ("parallel",), (h, g, wg, wu, wd), comm=comm)


def ffn_bwd_act(dh, h, g, a, b, wg, wu, wd, name, comm=None):
    n, d = h.shape
    f = wg.shape[0]
    tm = _tile(n, 192)
    nc = _ff_chunks(f)
    fc = f // nc

    def body(dh_ref, h_ref, g_ref, a_ref, b_ref, wg_ref, wu_ref, wd_ref,
             dhi_ref, da_ref, db_ref, sh_ref, dg_ref):
        x = h_ref[...]
        dy = dh_ref[...]
        r = _rms_r(x)
        dhh = (0.5 * dy).astype(BF16)
        du = jnp.zeros((tm, d), F32)
        for c in range(nc):
            cols = slice(c * fc, (c + 1) * fc)
            ds = _nt(dhh, wd_ref[cols, :])
            av = a_ref[:, cols].astype(F32)
            bv = b_ref[:, cols].astype(F32)
            sg = _sig(av)
            sil = av * sg
            da = (ds * bv * (sg * (1.0 + av * (1.0 - sg)))).astype(BF16)
            db = (ds * sil).astype(BF16)
            da_ref[:, cols] = da
            db_ref[:, cols] = db
            sh_ref[:, cols] = (0.5 * sil * bv).astype(BF16)
            du = du + _nn(da, wg_ref[cols, :]) + _nn(db, wu_ref[cols, :])
        dx, dg = _rms_bwd(x, r, g_ref[...], du)
        dhi_ref[...] = dy + dx
        _accumulate(dg_ref, dg, pl.program_id(0) == 0)

    return _call(
        body, name, (n // tm,),
        [_row(tm, d), _row(tm, d), _fixed((1, d)), _row(tm, f), _row(tm, f), VMEM_WHOLE, VMEM_WHOLE, VMEM_WHOLE],
        [_row(tm, d), _row(tm, f), _row(tm, f), _row(tm, f), _fixed((1, d))],
        [jax.ShapeDtypeStruct((n, d), F32), jax.ShapeDtypeStruct((n, f), BF16),
         jax.ShapeDtypeStruct((n, f), BF16), jax.ShapeDtypeStruct((n, f), BF16),
         jax.ShapeDtypeStruct((1, d), F32)],
        ("arbitrary",), (dh, h, g, a, b, wg, wu, wd), comm=comm)


def tn_matmul(x, y, name, out="f32", comm=None):
    n, k = x.shape
    m = y.shape[1]
    tm = _tile(n, TN_ROWS)
    kc, mc = k, (512 if m % 512 == 0 else m)
    while tm * kc * x.dtype.itemsize > TN_X_BYTES and kc % (2 * LANES) == 0:
        kc //= 2
    while tm * mc * y.dtype.itemsize > TN_Y_BYTES and mc % (2 * LANES) == 0:
        mc //= 2
    steps = n // tm

    def body(x_ref, y_ref, o_ref, *acc):
        i = pl.program_id(2)
        part = _tn(x_ref[...].astype(BF16), y_ref[...].astype(BF16))
        if steps == 1:
            o_ref[...] = part.astype(o_ref.dtype)
        elif out == "f32":
            _accumulate(o_ref, part, i == 0)
        else:
            _accumulate(acc[0], part, i == 0)

            @pl.when(i == steps - 1)
            def _():
                o_ref[...] = acc[0][...].astype(BF16)

    out_shape = jax.ShapeDtypeStruct((k, m), F32 if out == "f32" else BF16)
    (res,), landed = _call(
        body, name, (k // kc, m // mc, steps),
        [pl.BlockSpec((tm, kc), lambda a, b, i: (i, a)), pl.BlockSpec((tm, mc), lambda a, b, i: (i, b))],
        [pl.BlockSpec((kc, mc), lambda a, b, i: (a, b))], [out_shape], ("parallel", "parallel", "arbitrary"), (x, y),
        scratch=[pltpu.VMEM((kc, mc), F32)] if (out == "bf16" and steps > 1) else [], comm=comm)
    return (res, landed) if comm is not None else res


def inproj_fwd(h, g, wm, wl):
    n, d = h.shape
    tm = _tile(n, 352)

    def body(h_ref, g_ref, wm_ref, wl_ref, u_ref, zm_ref, zl_ref):
        x = h_ref[...]
        u = (x * _rms_r(x) * g_ref[...]).astype(BF16)
        u_ref[...] = u
        zm_ref[...] = _nt(u, wm_ref[...])
        zl_ref[...] = _nt(u, wl_ref[...])

    return pl.pallas_call(
        body, name="inproj_fwd", grid=(n // tm,),
        in_specs=[_row(tm, d), _fixed((1, d)), VMEM_WHOLE, VMEM_WHOLE],
        out_specs=[_row(tm, d), _row(tm, MLA_IN), _row(tm, 2 * LRU_WIDTH)],
        out_shape=[jax.ShapeDtypeStruct((n, d), BF16), jax.ShapeDtypeStruct((n, MLA_IN), F32),
                   jax.ShapeDtypeStruct((n, 2 * LRU_WIDTH), F32)],
        compiler_params=_params(("parallel",)),
    )(h, g, wm, wl)


def inproj_bwd(dzm, du, dgate, dh2, h, g, wm, wl):
    n, d = h.shape
    tm = _tile(n, 352)

    def body(dzm_ref, du_ref, dgt_ref, dh2_ref, h_ref, g_ref, wm_ref, wl_ref, dh_ref, dg_ref):
        x = h_ref[...]
        dun = (_nn(dzm_ref[...].astype(BF16), wm_ref[...])
               + _nn(du_ref[...].astype(BF16), wl_ref[:LRU_WIDTH, :])
               + _nn(dgt_ref[...].astype(BF16), wl_ref[LRU_WIDTH:, :]))
        dx, dg = _rms_bwd(x, _rms_r(x), g_ref[...], dun)
        dh_ref[...] = dh2_ref[...] + dx
        _accumulate(dg_ref, dg, pl.program_id(0) == 0)

    return pl.pallas_call(
        body, name="inproj_bwd", grid=(n // tm,),
        in_specs=[_row(tm, MLA_IN), _row(tm, LRU_WIDTH), _row(tm, LRU_WIDTH), _row(tm, d), _row(tm, d),
                  _fixed((1, d)), VMEM_WHOLE, VMEM_WHOLE],
        out_specs=[_row(tm, d), _fixed((1, d))],
        out_shape=[jax.ShapeDtypeStruct((n, d), F32), jax.ShapeDtypeStruct((1, d), F32)],
        compiler_params=_params(("arbitrary",)),
    )(dzm, du, dgate, dh2, h, g, wm, wl)


def _rope_tables(lp):
    pos = jnp.arange(lp, dtype=F32) - float(PAD)
    half = D_ROPE // 2
    inv_freq = ROPE_THETA ** (-jnp.arange(0, half, dtype=F32) / half)
    ang = pos[:, None] * inv_freq[None, :]
    cos, sin = jnp.cos(ang), jnp.sin(ang)
    one = jnp.ones((lp, D_NOPE), F32)
    z_nope = jnp.zeros((lp, D_NOPE), F32)
    z_half = jnp.zeros((lp, half), F32)
    z_tail = jnp.zeros((lp, HEAD_SLAB - D_QK), F32)
    cosr = jnp.concatenate([one, cos, cos, z_tail], axis=1)
    sin_up = jnp.concatenate([z_nope, z_half, sin, z_tail], axis=1)
    sin_dn = jnp.concatenate([z_nope, -sin, z_half, z_tail], axis=1)
    return cosr, sin_up, sin_dn


def _rope(x, cosr, sin_up, sin_dn):
    half = D_ROPE // 2
    return x * cosr + pltpu.roll(x, half, axis=1) * sin_up + pltpu.roll(x, HEAD_SLAB - half, axis=1) * sin_dn


def _rope_bwd(dy, cosr, sin_up, sin_dn):
    half = D_ROPE // 2
    return (dy * cosr + pltpu.roll(dy * sin_up, HEAD_SLAB - half, axis=1)
            + pltpu.roll(dy * sin_dn, half, axis=1))


def _k_rope_slab(zm_tile):
    tm = zm_tile.shape[0]
    krp = zm_tile[:, Q_RANK + KV_RANK:MLA_IN]
    return jnp.concatenate([jnp.zeros((tm, D_NOPE), F32), krp], axis=1)


def mla_prep_fwd(zm, gql, gkvl, wuq, wuk, wuv, gqh, gkh, tables, lp):
    n = zm.shape[0]
    tm = _tile(lp, 352)
    per_seq = lp // tm
    width = MLA_HEADS * HEAD_SLAB
    scale = 1.0 / math.sqrt(D_QK)

    def body(zm_ref, gql_ref, gkvl_ref, wuq_ref, wuk_ref, wuv_ref, gqh_ref, gkh_ref,
             cos_ref, up_ref, dn_ref, q_ref, k_ref, v_ref, qn_ref, cn_ref):
        z = zm_ref[...]
        cq = z[:, :Q_RANK]
        ckv = z[:, Q_RANK:Q_RANK + KV_RANK]
        qn = (cq * _rms_r(cq) * gql_ref[...]).astype(BF16)
        cn = (ckv * _rms_r(ckv) * gkvl_ref[...]).astype(BF16)
        qn_ref[...] = qn
        cn_ref[...] = cn
        q_raw = _nt(qn, wuq_ref[...])
        k_raw = _nt(cn, wuk_ref[...])
        v_ref[...] = _nt(cn, wuv_ref[...]).astype(BF16)
        kr_slab = _k_rope_slab(z)
        cosr, sin_up, sin_dn = cos_ref[...], up_ref[...], dn_ref[...]
        for hd in range(MLA_HEADS):
            cols = slice(hd * HEAD_SLAB, (hd + 1) * HEAD_SLAB)
            xq = q_raw[:, cols]
            yq = _rope(xq * _rms_r(xq, D_QK) * gqh_ref[...], cosr, sin_up, sin_dn)
            q_ref[:, cols] = (yq * scale).astype(BF16)
            xk = k_raw[:, cols] + kr_slab
            yk = _rope(xk * _rms_r(xk, D_QK) * gkh_ref[...], cosr, sin_up, sin_dn)
            k_ref[:, cols] = yk.astype(BF16)

    tab = pl.BlockSpec((tm, HEAD_SLAB), lambda i: (i % per_seq, 0))
    return pl.pallas_call(
        body, name="mla_prep_fwd", grid=(n // tm,),
        in_specs=[_row(tm, MLA_IN), _fixed((1, Q_RANK)), _fixed((1, KV_RANK)), VMEM_WHOLE, VMEM_WHOLE, VMEM_WHOLE,
                  _fixed((1, HEAD_SLAB)), _fixed((1, HEAD_SLAB)), tab, tab, tab],
        out_specs=[_row(tm, width), _row(tm, width), _row(tm, MLA_HEADS * D_V), _row(tm, Q_RANK), _row(tm, KV_RANK)],
        out_shape=[jax.ShapeDtypeStruct((n, width), BF16), jax.ShapeDtypeStruct((n, width), BF16),
                   jax.ShapeDtypeStruct((n, MLA_HEADS * D_V), BF16), jax.ShapeDtypeStruct((n, Q_RANK), BF16),
                   jax.ShapeDtypeStruct((n, KV_RANK), BF16)],
        compiler_params=_params(("parallel",)),
    )(zm, gql, gkvl, wuq, wuk, wuv, gqh, gkh, *tables)


def mla_prep_bwd(dq, dk, dv, zm, qn, cn, gql, gkvl, wuq, wuk, wuv, gqh, gkh, tables, lp, comm=None):
    n = zm.shape[0]
    tm = _tile(lp, 352)
    per_seq = lp // tm
    width = MLA_HEADS * HEAD_SLAB
    scale = 1.0 / math.sqrt(D_QK)

    def body(dq_ref, dk_ref, dv_ref, zm_ref, qn_ref, cn_ref, gql_ref, gkvl_ref, wuq_ref, wuk_ref, wuv_ref,
             gqh_ref, gkh_ref, cos_ref, up_ref, dn_ref,
             dzm_ref, dqr_ref, dkr_ref, dgql_ref, dgkvl_ref, dgqh_ref, dgkh_ref):
        z = zm_ref[...]
        cq = z[:, :Q_RANK]
        ckv = z[:, Q_RANK:Q_RANK + KV_RANK]
        q_raw = _nt(qn_ref[...], wuq_ref[...])
        k_raw = _nt(cn_ref[...], wuk_ref[...])
        kr_slab = _k_rope_slab(z)
        cosr, sin_up, sin_dn = cos_ref[...], up_ref[...], dn_ref[...]
        dgq = jnp.zeros((1, HEAD_SLAB), F32)
        dgk = jnp.zeros((1, HEAD_SLAB), F32)
        dkrp = jnp.zeros((tm, HEAD_SLAB - D_NOPE), F32)
        for hd in range(MLA_HEADS):
            cols = slice(hd * HEAD_SLAB, (hd + 1) * HEAD_SLAB)
            xq = q_raw[:, cols]
            dxn = _rope_bwd(dq_ref[:, cols] * scale, cosr, sin_up, sin_dn)
            dxq, dg = _rms_bwd(xq, _rms_r(xq, D_QK), gqh_ref[...], dxn, D_QK)
            dgq = dgq + dg
            dqr_ref[:, cols] = dxq.astype(BF16)
            xk = k_raw[:, cols] + kr_slab
            dxn = _rope_bwd(dk_ref[:, cols], cosr, sin_up, sin_dn)
            dxk, dg = _rms_bwd(xk, _rms_r(xk, D_QK), gkh_ref[...], dxn, D_QK)
            dgk = dgk + dg
            dkr_ref[:, cols] = dxk.astype(BF16)
            dkrp = dkrp + dxk[:, D_NOPE:]
        dqn = _nn(dqr_ref[...], wuq_ref[...])
        dcn = _nn(dkr_ref[...], wuk_ref[...]) + _nn(dv_ref[...].astype(BF16), wuv_ref[...])
        dcq, dg1 = _rms_bwd(cq, _rms_r(cq), gql_ref[...], dqn)
        dckv, dg2 = _rms_bwd(ckv, _rms_r(ckv), gkvl_ref[...], dcn)
        dzm_ref[:, :Q_RANK] = dcq
        dzm_ref[:, Q_RANK:Q_RANK + KV_RANK] = dckv
        dzm_ref[:, Q_RANK + KV_RANK:] = dkrp
        first = pl.program_id(0) == 0
        _accumulate(dgql_ref, dg1, first)
        _accumulate(dgkvl_ref, dg2, first)
        _accumulate(dgqh_ref, dgq, first)
        _accumulate(dgkh_ref, dgk, first)

    tab = pl.BlockSpec((tm, HEAD_SLAB), lambda i: (i % per_seq, 0))
    return _call(
        body, "mla_prep_bwd", (n // tm,),
        [_row(tm, width), _row(tm, width), _row(tm, MLA_HEADS * D_V), _row(tm, MLA_IN),
         _row(tm, Q_RANK), _row(tm, KV_RANK), _fixed((1, Q_RANK)), _fixed((1, KV_RANK)),
         VMEM_WHOLE, VMEM_WHOLE, VMEM_WHOLE, _fixed((1, HEAD_SLAB)), _fixed((1, HEAD_SLAB)), tab, tab, tab],
        [_row(tm, MLA_IN), _row(tm, width), _row(tm, width), _fixed((1, Q_RANK)), _fixed((1, KV_RANK)),
         _fixed((1, HEAD_SLAB)), _fixed((1, HEAD_SLAB))],
        [jax.ShapeDtypeStruct((n, MLA_IN), F32), jax.ShapeDtypeStruct((n, width), BF16),
         jax.ShapeDtypeStruct((n, width), BF16), jax.ShapeDtypeStruct((1, Q_RANK), F32),
         jax.ShapeDtypeStruct((1, KV_RANK), F32), jax.ShapeDtypeStruct((1, HEAD_SLAB), F32),
         jax.ShapeDtypeStruct((1, HEAD_SLAB), F32)],
        ("arbitrary",), (dq, dk, dv, zm, qn, cn, gql, gkvl, wuq, wuk, wuv, gqh, gkh, *tables), comm=comm)


def _attn_tile(lp):
    return _tile(lp, 704, CHUNK)


def _chunk_mask(i, j, t):
    qpos = i * t + lax.broadcasted_iota(jnp.int32, (t, t), 0)
    kpos = j * t + lax.broadcasted_iota(jnp.int32, (t, t), 1)
    same_or_earlier = jnp.right_shift(kpos, CHUNK_SHIFT) <= jnp.right_shift(qpos, CHUNK_SHIFT)
    return jnp.logical_and(same_or_earlier, kpos >= PAD)


def _masked_scores(s, i, j, t, diagonal):
    if diagonal:
        return jnp.where(_chunk_mask(i, j, t), s, NEG_INF)
    kpos = j * t + lax.broadcasted_iota(jnp.int32, (1, t), 1)
    return s + jnp.where(kpos < PAD, NEG_INF, 0.0)


def attn_fwd(q, k, v, nb, lp, comm=None):
    n = q.shape[0]
    t = _attn_tile(lp)
    nq = lp // t

    def body(q_ref, k_ref, v_ref, o_ref, lse_ref):
        i = pl.program_id(2)
        qv = q_ref[...]

        def kv_step(j, carry, diagonal=False):
            m, l, acc = carry
            off = pl.multiple_of(j * t, t)
            s = _masked_scores(_nt(qv, k_ref[pl.ds(off, t), :]), i, j, t, diagonal)
            m_new = jnp.maximum(m, jnp.max(s, axis=-1, keepdims=True))
            p = jnp.exp(s - m_new)
            alpha = jnp.exp(m - m_new)
            l = alpha * l + jnp.sum(p, axis=-1, keepdims=True)
            acc = alpha * acc + _nn(p.astype(BF16), v_ref[pl.ds(off, t), :])
            return m_new, l, acc

        init = (jnp.full((t, 1), NEG_INF, F32), jnp.zeros((t, 1), F32), jnp.zeros((t, D_V), F32))
        m, l, acc = kv_step(i, lax.fori_loop(0, i, kv_step, init), diagonal=True)
        o_ref[...] = acc * (1.0 / l)
        lse_ref[0] = jnp.broadcast_to(m + jnp.log(l), (t, LANES))

    return _call(
        body, "attn_fwd", (nb, MLA_HEADS, nq),
        [pl.BlockSpec((t, HEAD_SLAB), lambda b, h, i: (b * nq + i, h)),
         pl.BlockSpec((lp, HEAD_SLAB), lambda b, h, i: (b, h)),
         pl.BlockSpec((lp, D_V), lambda b, h, i: (b, h))],
        [pl.BlockSpec((t, D_V), lambda b, h, i: (b * nq + i, h)),
         pl.BlockSpec((1, t, LANES), lambda b, h, i: (h, b * nq + i, 0))],
        [jax.ShapeDtypeStruct((n, MLA_HEADS * D_V), F32), jax.ShapeDtypeStruct((MLA_HEADS, n, LANES), F32)],
        ("parallel", "parallel", "parallel"), (q, k, v), comm=comm)


def attn_bwd(q, k, v, o, do, lse, nb, lp, comm=None):
    n = q.shape[0]
    t = _attn_tile(lp)
    nq = lp // t

    def body(q_ref, k_ref, v_ref, o_ref, do_ref, lse_ref, dq_ref, dk_ref, dv_ref):
        dk_ref[...] = jnp.zeros_like(dk_ref)
        dv_ref[...] = jnp.zeros_like(dv_ref)

        def q_step(i, _):
            qoff = pl.multiple_of(i * t, t)
            qv = q_ref[pl.ds(qoff, t), :]
            dov = do_ref[pl.ds(qoff, t), :]
            delta = jnp.sum(o_ref[pl.ds(qoff, t), :] * dov, axis=-1, keepdims=True)
            lse_q = jnp.max(lse_ref[0, pl.ds(qoff, t), :], axis=-1, keepdims=True)
            do16 = dov.astype(BF16)

            def kv_step(j, dq_acc, diagonal=False):
                koff = pl.multiple_of(j * t, t)
                kv = k_ref[pl.ds(koff, t), :]
                s = _masked_scores(_nt(qv, kv), i, j, t, diagonal)
                p = jnp.exp(s - lse_q)
                dp = _nt(do16, v_ref[pl.ds(koff, t), :])
                ds16 = (p * (dp - delta)).astype(BF16)
                dv_ref[pl.ds(koff, t), :] += _tn(p.astype(BF16), do16)
                dk_ref[pl.ds(koff, t), :] += _tn(ds16, qv)
                return dq_acc + _nn(ds16, kv)

            earlier = lax.fori_loop(0, i, kv_step, jnp.zeros((t, HEAD_SLAB), F32))
            dq_ref[pl.ds(qoff, t), :] = kv_step(i, earlier, diagonal=True)
            return 0

        lax.fori_loop(0, nq, q_step, 0)

    wide = pl.BlockSpec((lp, HEAD_SLAB), lambda b, h: (b, h))
    thin = pl.BlockSpec((lp, D_V), lambda b, h: (b, h))
    width = MLA_HEADS * HEAD_SLAB
    return _call(
        body, "attn_bwd", (nb, MLA_HEADS),
        [wide, wide, thin, thin, thin, pl.BlockSpec((1, lp, LANES), lambda b, h: (h, b, 0))],
        [wide, wide, thin],
        [jax.ShapeDtypeStruct((n, width), F32), jax.ShapeDtypeStruct((n, width), F32),
         jax.ShapeDtypeStruct((n, MLA_HEADS * D_V), F32)],
        ("parallel", "parallel"), (q, k, v, o, do, lse), comm=comm)


def _seq_rows(nb, lp, width):
    rows = lax.broadcasted_iota(jnp.int32, (lp, width), 0)
    return jnp.concatenate([rows] * nb, axis=0) if nb > 1 else rows


def _lru_gates(u, w_ref, cb, wa, wx, ba, bx, lam):
    xc = (cb + w_ref[pl.ds(3, 1), :] * u + w_ref[pl.ds(2, 1), :] * pltpu.roll(u, 1, axis=0)
          + w_ref[pl.ds(1, 1), :] * pltpu.roll(u, 2, axis=0) + w_ref[pl.ds(0, 1), :] * pltpu.roll(u, 3, axis=0))
    xc16 = xc.astype(BF16)
    ra = _sig_tanh(_nn(xc16, wa) + ba)
    ia = _sig_tanh(_nn(xc16, wx) + bx)
    sp = _softplus(-lam)
    log_a = -C_RGLRU * ra * sp
    a = jnp.exp(log_a)
    x2 = 2.0 * log_a
    mult = jnp.sqrt(jnp.where(x2 > -1e-2, -x2 * (1.0 + x2 * (0.5 + x2 * (1.0 / 6.0))), 1.0 - a * a))
    return xc, xc16, ra, ia, sp, a, mult


def _scan_block_rows(width):
    return lax.broadcasted_iota(jnp.int32, (8, width), 0)


def lru_fwd(zl, conv_w, conv_b, wa, wx, ba, bx, lam, nb, lp):
    n = zl.shape[0]
    w = LRU_TILE
    nt = LRU_WIDTH // w
    nblk = lp // 8

    def body(u_ref, gt_ref, cw_ref, cb_ref, wa_ref, wx_ref, ba_ref, bx_ref, lam_ref, y_ref, h_ref, a_s, b_s):
        u = u_ref[...]
        xc, _, _, ia, _, a, mult = _lru_gates(u, cw_ref, cb_ref[...], wa_ref[...], wx_ref[...],
                                              ba_ref[...], bx_ref[...], lam_ref[...])
        row = _seq_rows(nb, lp, w)
        mult = jnp.where(row == PAD, 1.0, mult)
        a_s[...] = a
        b_s[...] = jnp.where(row < PAD, 0.0, mult * (ia * xc))
        r8 = _scan_block_rows(w)

        def blk(i, carry):
            out = []
            for s_id in range(nb):
                off = pl.multiple_of(s_id * lp + i * 8, 8)
                av = a_s[pl.ds(off, 8), :]
                bv = b_s[pl.ds(off, 8), :]
                for sh in (1, 2, 4):
                    keep = r8 >= sh
                    bv = jnp.where(keep, av * pltpu.roll(bv, sh, axis=0) + bv, bv)
                    av = jnp.where(keep, av * pltpu.roll(av, sh, axis=0), av)
                hv = bv + av * carry[s_id]
                h_ref[pl.ds(off, 8), :] = hv
                out.append(jnp.sum(jnp.where(r8 == 7, hv, 0.0), axis=0, keepdims=True))
            return tuple(out)

        lax.fori_loop(0, nblk, blk, tuple(jnp.zeros((1, w), F32) for _ in range(nb)))
        gelu, _ = _gelu_and_grad(gt_ref[...])
        y_ref[...] = h_ref[...] * gelu

    col = lambda c: (0, c)
    return pl.pallas_call(
        body, name="lru_fwd", grid=(nt,),
        in_specs=[pl.BlockSpec((n, w), col), pl.BlockSpec((n, w), lambda c: (0, nt + c)),
                  pl.BlockSpec((CONV_W, w), col), pl.BlockSpec((1, w), col),
                  pl.BlockSpec((w, w), lambda c: (c, c)), pl.BlockSpec((w, w), lambda c: (c, c)),
                  pl.BlockSpec((1, w), col), pl.BlockSpec((1, w), col), pl.BlockSpec((1, w), col)],
        out_specs=[pl.BlockSpec((n, w), col), pl.BlockSpec((n, w), col)],
        out_shape=[jax.ShapeDtypeStruct((n, LRU_WIDTH), F32), jax.ShapeDtypeStruct((n, LRU_WIDTH), F32)],
        scratch_shapes=[pltpu.VMEM((n, w), F32), pltpu.VMEM((n, w), F32)],
        compiler_params=_params(("parallel",)),
    )(zl, zl, conv_w, conv_b, wa, wx, ba, bx, lam)


def lru_bwd(zl, hs, dy, conv_w, conv_b, wa, wx, ba, bx, lam, nb, lp, comm=None):
    n = zl.shape[0]
    w = LRU_TILE
    nt = LRU_WIDTH // w
    nblk = lp // 8

    def body(u_ref, gt_ref, h_ref, dy_ref, cw_ref, cb_ref, wa_ref, wx_ref, ba_ref, bx_ref, lam_ref,
             du_ref, dgt_ref, dcw_ref, dcb_ref, dba_ref, dbx_ref, dlam_ref, dwa_ref, dwx_ref,
             c_s, d_s, g_s, dwa_s, dwx_s):
        u = u_ref[...]
        lam = lam_ref[...]
        xc, xc16, ra, ia, sp, a, mult = _lru_gates(u, cw_ref, cb_ref[...], wa_ref[...], wx_ref[...],
                                                   ba_ref[...], bx_ref[...], lam)
        row = lax.broadcasted_iota(jnp.int32, (lp, w), 0)
        hv = h_ref[...]
        dyv = dy_ref[...]
        gelu, dgelu = _gelu_and_grad(gt_ref[...])
        dgt_ref[...] = jnp.where(row >= PAD, dyv * hv * dgelu, 0.0)
        c_s[...] = pltpu.roll(a, lp - 1, axis=0)
        d_s[...] = dyv * gelu
        r8 = _scan_block_rows(w)

        def blk(ii, carry):
            off = pl.multiple_of((nblk - 1 - ii) * 8, 8)
            cv = c_s[pl.ds(off, 8), :]
            dv = d_s[pl.ds(off, 8), :]
            for sh in (1, 2, 4):
                keep = r8 < 8 - sh
                dv = jnp.where(keep, cv * pltpu.roll(dv, 8 - sh, axis=0) + dv, dv)
                cv = jnp.where(keep, cv * pltpu.roll(cv, 8 - sh, axis=0), cv)
            gv = dv + cv * carry
            g_s[pl.ds(off, 8), :] = gv
            return jnp.sum(jnp.where(r8 == 0, gv, 0.0), axis=0, keepdims=True)

        lax.fori_loop(0, nblk, blk, jnp.zeros((1, w), F32))
        gv = g_s[...]
        first_row = row == PAD
        db = jnp.where(row >= PAD, gv, 0.0)
        da = jnp.where(row > PAD, gv * pltpu.roll(hv, 1, axis=0), 0.0)
        mult_eff = jnp.where(first_row, 1.0, mult)
        dmult = jnp.where(first_row, 0.0, db * (ia * xc))
        dia = db * mult_eff * xc
        dxc = db * mult_eff * ia
        dla = da * a - dmult * (a * a) / mult
        dra = dla * (-C_RGLRU * sp)
        dsp = jnp.sum(dla * (-C_RGLRU * ra), axis=0, keepdims=True)
        dpa = dra * ra * (1.0 - ra)
        dpx = dia * ia * (1.0 - ia)
        dpa16 = dpa.astype(BF16)
        dpx16 = dpx.astype(BF16)
        dxc = dxc + _nt(dpa16, wa_ref[...]) + _nt(dpx16, wx_ref[...])
        du = cw_ref[pl.ds(CONV_W - 1, 1), :] * dxc
        dcw = [jnp.sum(dxc * u, axis=0, keepdims=True)]
        for tap in range(1, CONV_W):
            dcw.insert(0, jnp.sum(dxc * pltpu.roll(u, tap, axis=0), axis=0, keepdims=True))
            du = du + cw_ref[pl.ds(CONV_W - 1 - tap, 1), :] * pltpu.roll(dxc, lp - tap, axis=0)
        du_ref[...] = jnp.where(row >= PAD, du, 0.0)
        first = pl.program_id(1) == 0
        _accumulate(dlam_ref, -_sig(-lam) * dsp, first)
        _accumulate(dba_ref, jnp.sum(dpa, axis=0, keepdims=True), first)
        _accumulate(dbx_ref, jnp.sum(dpx, axis=0, keepdims=True), first)
        _accumulate(dcb_ref, jnp.sum(dxc, axis=0, keepdims=True), first)
        _accumulate(dcw_ref, jnp.concatenate(dcw, axis=0), first)
        _accumulate(dwa_s, _tn(xc16, dpa16), first)
        _accumulate(dwx_s, _tn(xc16, dpx16), first)

        @pl.when(pl.program_id(1) == nb - 1)
        def _():
            for j in range(w // LRU_BLOCK):
                blk_rows = slice(j * LRU_BLOCK, (j + 1) * LRU_BLOCK)
                dwa_ref[0, blk_rows, :] = dwa_s[blk_rows, blk_rows]
                dwx_ref[0, blk_rows, :] = dwx_s[blk_rows, blk_rows]

    col = lambda c, b: (0, c)
    vec = pl.BlockSpec((1, w), col)
    mat = pl.BlockSpec((w, w), lambda c, b: (c, c))
    big = pl.BlockSpec((lp, w), lambda c, b: (b, c))
    dmat = pl.BlockSpec((1, w, LRU_BLOCK), lambda c, b: (c, 0, 0))
    return _call(
        body, "lru_bwd", (nt, nb),
        [big, pl.BlockSpec((lp, w), lambda c, b: (b, nt + c)), big, big,
         pl.BlockSpec((CONV_W, w), col), vec, mat, mat, vec, vec, vec],
        [big, big, pl.BlockSpec((CONV_W, w), col), vec, vec, vec, vec, dmat, dmat],
        [jax.ShapeDtypeStruct((n, LRU_WIDTH), F32), jax.ShapeDtypeStruct((n, LRU_WIDTH), F32),
         jax.ShapeDtypeStruct((CONV_W, LRU_WIDTH), F32), jax.ShapeDtypeStruct((1, LRU_WIDTH), F32),
         jax.ShapeDtypeStruct((1, LRU_WIDTH), F32), jax.ShapeDtypeStruct((1, LRU_WIDTH), F32),
         jax.ShapeDtypeStruct((1, LRU_WIDTH), F32), jax.ShapeDtypeStruct((nt, w, LRU_BLOCK), F32),
         jax.ShapeDtypeStruct((nt, w, LRU_BLOCK), F32)],
        ("parallel", "arbitrary"), (zl, zl, hs, dy, conv_w, conv_b, wa, wx, ba, bx, lam),
        scratch=[pltpu.VMEM((lp, w), F32), pltpu.VMEM((lp, w), F32), pltpu.VMEM((lp, w), F32),
                 pltpu.VMEM((w, w), F32), pltpu.VMEM((w, w), F32)], comm=comm)


def outproj_fwd(h, ya, yl, gao, glo, wout):
    n, d = h.shape
    half = ya.shape[1]
    tm = _tile(n, 352)

    def body(h_ref, ya_ref, yl_ref, gao_ref, glo_ref, w_ref, ho_ref, yn_ref):
        xa = ya_ref[...]
        xl = yl_ref[...]
        na = (xa * _rms_r(xa) * gao_ref[...]).astype(BF16)
        nl = (xl * _rms_r(xl) * glo_ref[...]).astype(BF16)
        yn_ref[:, :half] = na
        yn_ref[:, half:] = nl
        ho_ref[...] = h_ref[...] + _nn(na, w_ref[:half, :]) + _nn(nl, w_ref[half:, :])

    return pl.pallas_call(
        body, name="outproj_fwd", grid=(n // tm,),
        in_specs=[_row(tm, d), _row(tm, half), _row(tm, half), _fixed((1, half)), _fixed((1, half)), VMEM_WHOLE],
        out_specs=[_row(tm, d), _row(tm, 2 * half)],
        out_shape=[jax.ShapeDtypeStruct((n, d), F32), jax.ShapeDtypeStruct((n, 2 * half), BF16)],
        compiler_params=_params(("parallel",)),
    )(h, ya, yl, gao, glo, wout)


def outproj_bwd(dh, ya, yl, gao, glo, wout):
    n, d = dh.shape
    half = ya.shape[1]
    tm = _tile(n, 352)

    def body(dh_ref, ya_ref, yl_ref, gao_ref, glo_ref, w_ref, dya_ref, dyl_ref, dgao_ref, dglo_ref):
        d16 = dh_ref[...].astype(BF16)
        xa = ya_ref[...]
        xl = yl_ref[...]
        dxa, dga = _rms_bwd(xa, _rms_r(xa), gao_ref[...], _nt(d16, w_ref[:half, :]))
        dxl, dgl = _rms_bwd(xl, _rms_r(xl), glo_ref[...], _nt(d16, w_ref[half:, :]))
        dya_ref[...] = dxa
        dyl_ref[...] = dxl
        first = pl.program_id(0) == 0
        _accumulate(dgao_ref, dga, first)
        _accumulate(dglo_ref, dgl, first)

    return pl.pallas_call(
        body, name="outproj_bwd", grid=(n // tm,),
        in_specs=[_row(tm, d), _row(tm, half), _row(tm, half), _fixed((1, half)), _fixed((1, half)), VMEM_WHOLE],
        out_specs=[_row(tm, half), _row(tm, half), _fixed((1, half)), _fixed((1, half))],
        out_shape=[jax.ShapeDtypeStruct((n, half), F32), jax.ShapeDtypeStruct((n, half), F32),
                   jax.ShapeDtypeStruct((1, half), F32), jax.ShapeDtypeStruct((1, half), F32)],
        compiler_params=_params(("arbitrary",)),
    )(dh, ya, yl, gao, glo, wout)


def final_loss(h, g, tgt, lp):
    n, d = h.shape
    tm = _tile(lp, 352)
    per_seq = lp // tm

    def body(h_ref, g_ref, t_ref, loss_ref, dh_ref, dg_ref):
        i = pl.program_id(0)
        x = h_ref[...]
        gv = g_ref[...]
        r = _rms_r(x)
        row = (i % per_seq) * tm + lax.broadcasted_iota(jnp.int32, (tm, d), 0)
        diff = jnp.where(row >= FIRST_FRAME, x * r * gv - t_ref[...], 0.0)
        part = 0.5 * jnp.sum(jnp.sum(diff * diff, axis=-1, keepdims=True) * (1.0 / d), axis=0, keepdims=True)
        dx, dg = _rms_bwd(x, r, gv, diff * (1.0 / d))
        dh_ref[...] = dx
        _accumulate(loss_ref, jnp.broadcast_to(part, (1, LANES)), i == 0)
        _accumulate(dg_ref, dg, i == 0)

    return pl.pallas_call(
        body, name="final_loss", grid=(n // tm,),
        in_specs=[_row(tm, d), _fixed((1, d)), _row(tm, d)],
        out_specs=[_fixed((1, LANES)), _row(tm, d), _fixed((1, d))],
        out_shape=[jax.ShapeDtypeStruct((1, LANES), F32), jax.ShapeDtypeStruct((n, d), F32),
                   jax.ShapeDtypeStruct((1, d), F32)],
        compiler_params=_params(("arbitrary",)),
    )(h, g, tgt)


def assemble_cols(g, name):
    _, k, ns = g.shape

    def body(g_ref, o_ref):
        for j in range(N_DEV):
            o_ref[:, j * ns:(j + 1) * ns] = g_ref[j]

    return pl.pallas_call(body, name=name, out_shape=jax.ShapeDtypeStruct((k, N_DEV * ns), g.dtype),
                          compiler_params=_params(None))(g)


def split_cols(x, name):
    k, cols = x.shape
    ns = cols // N_DEV

    def body(x_ref, o_ref):
        for j in range(N_DEV):
            o_ref[j] = x_ref[:, j * ns:(j + 1) * ns]

    return pl.pallas_call(body, name=name, out_shape=jax.ShapeDtypeStruct((N_DEV, k, ns), x.dtype),
                          compiler_params=_params(None))(x)


def _slab_rows(w, per_head):
    k = w.shape[1]
    w = w.reshape(MLA_HEADS, per_head, k)
    return jnp.pad(w, ((0, 0), (0, HEAD_SLAB - per_head), (0, 0))).reshape(MLA_HEADS * HEAD_SLAB, k)


def _unslab_rows(w, per_head):
    k = w.shape[1]
    return w.reshape(MLA_HEADS, HEAD_SLAB, k)[:, :per_head].reshape(MLA_HEADS * per_head, k)


def meta_grad(dh0, nb, lp):
    d = dh0.shape[1]
    ns = d // N_DEV
    per_seq = lp // N_META

    def body(x_ref, o_ref):
        x = x_ref[...]
        for j in range(N_DEV):
            _accumulate(o_ref.at[j], x[:, j * ns:(j + 1) * ns], pl.program_id(0) == 0)

    return pl.pallas_call(
        body, name="meta_grad", grid=(nb,),
        in_specs=[pl.BlockSpec((N_META, d), lambda b: (b * per_seq + PAD // N_META, 0))],
        out_specs=pl.BlockSpec((N_DEV, N_META, ns), lambda b: (0, 0, 0)),
        out_shape=jax.ShapeDtypeStruct((N_DEV, N_META, ns), F32),
        compiler_params=_params(("arbitrary",)))(dh0)


VECTORS = [("ffn1_norm", 1024), ("mix_norm", 1024), ("q_latent_norm", 384), ("kv_latent_norm", 256),
           ("q_head_norm", 192), ("k_head_norm", 192), ("conv_b", 512), ("gate_a_b", 512), ("gate_x_b", 512),
           ("lru_lambda", 512), ("attn_out_norm", 512), ("lru_out_norm", 512), ("ffn2_norm", 1024),
           ("final_norm", 1024)]
VEC_ROWS = 16
GATES = ["gate_a_w", "gate_x_w"]


def pack_vectors(grads):
    def body(*refs):
        o_ref = refs[-1]
        o_ref[...] = jnp.zeros_like(o_ref)
        for t, (ref, (_, cnt)) in enumerate(zip(refs[:-1], VECTORS)):
            o_ref[t:t + 1, :cnt] = ref[:, :cnt]

    return pl.pallas_call(body, name="pack_vectors", out_shape=jax.ShapeDtypeStruct((VEC_ROWS, D_MODEL), F32),
                          compiler_params=_params(None))(*[grads[name] for name, _ in VECTORS])


def _adamw_update(w, g, m, v):
    c1 = 1.0 / (1.0 - ADAM_B1 ** ADAM_STEP)
    c2 = 1.0 / (1.0 - ADAM_B2 ** ADAM_STEP)
    mn = ADAM_B1 * m + (1.0 - ADAM_B1) * g
    vn = ADAM_B2 * v + (1.0 - ADAM_B2) * (g * g)
    delta = -ADAM_LR * ((mn * c1) / (jnp.sqrt(vn * c2) + ADAM_EPS) + ADAM_WD * w)
    return delta, mn, vn


def _sum_slots(ref, index=()):
    acc = ref[(0,) + index].astype(F32)
    for s in range(1, N_DEV):
        acc = acc + ref[(s,) + index].astype(F32)
    return acc


def adamw_sharded(r, w, m, v, name):
    rows, cols = w.shape
    tr = _tile(rows, 256, 16) if rows % 16 == 0 else rows

    def body(r_ref, w_ref, m_ref, v_ref, g_ref, d_ref, mo_ref, vo_ref):
        g = _sum_slots(r_ref)
        g_ref[...] = g
        d_ref[...], mo_ref[...], vo_ref[...] = _adamw_update(w_ref[...], g, m_ref[...], v_ref[...])

    spec = pl.BlockSpec((tr, cols), lambda i: (i, 0))
    shape = jax.ShapeDtypeStruct((rows, cols), F32)
    return pl.pallas_call(
        body, name=name, grid=(rows // tr,),
        in_specs=[pl.BlockSpec((N_DEV, tr, cols), lambda i: (0, i, 0))] + [spec] * 3,
        out_specs=[spec] * 4, out_shape=[shape] * 4,
        compiler_params=_params(("parallel",)),
    )(r, w, m, v)


def adamw_small(r_vec, r_gates, w, m, v):
    nt = len(VECTORS) + len(GATES)

    def body(*refs):
        rv_ref = refs[0]
        rg_refs = refs[1:1 + len(GATES)]
        base = 1 + len(GATES)
        w_refs, m_refs, v_refs = (refs[base + i * nt:base + (i + 1) * nt] for i in range(3))
        outs = refs[base + 3 * nt:]
        g_o, d_o, m_o, v_o = (outs[i * nt:(i + 1) * nt] for i in range(4))
        for t in range(nt):
            if t < len(VECTORS):
                cnt = VECTORS[t][1]
                g = _sum_slots(rv_ref, (slice(t, t + 1), slice(0, cnt)))
            else:
                g = _sum_slots(rg_refs[t - len(VECTORS)])
            g_o[t][...] = g
            d_o[t][...], m_o[t][...], v_o[t][...] = _adamw_update(w_refs[t][...], g, m_refs[t][...], v_refs[t][...])

    shapes = [jax.ShapeDtypeStruct(a.shape, F32) for a in w]
    res = pl.pallas_call(body, name="adamw_small", out_shape=shapes * 4,
                         compiler_params=_params(None))(r_vec, *r_gates, *w, *m, *v)
    return [res[i * nt:(i + 1) * nt] for i in range(4)]


def _block_diag(w):
    nb, n, _ = w.shape
    eye = jnp.eye(nb, dtype=w.dtype)
    return (eye[:, None, :, None] * w[:, :, None, :]).reshape(nb * n, nb * n)


def _two_d(a):
    if a.ndim == 3:
        return a.reshape(a.shape[1], a.shape[2])
    if a.ndim == 4:
        return a.reshape(a.shape[1] * a.shape[2], a.shape[3])
    return a


_WEIGHT_NAMES = ['meta_tokens', 'ffn1_norm', 'ffn1_w_gate', 'ffn1_w_up', 'ffn1_w_down', 'mix_norm', 'w_in',
                 'q_latent_norm', 'w_uq', 'kv_latent_norm', 'w_uk', 'w_uv', 'q_head_norm', 'k_head_norm', 'conv_w',
                 'conv_b', 'gate_a_w', 'gate_a_b', 'gate_x_w', 'gate_x_b', 'lru_lambda', 'attn_out_norm',
                 'lru_out_norm', 'w_out', 'ffn2_norm', 'ffn2_w_gate', 'ffn2_w_up', 'ffn2_w_down', 'final_norm']


COLUMN_SHARDED = ("ffn1_w_gate", "ffn1_w_up", "ffn2_w_gate", "ffn2_w_up", "w_in", "w_uq", "w_uk", "w_uv")


def train_step(x, tgt, w, m, v):
    nb, seq, d = x.shape
    lp = PAD + N_META + seq
    n = nb * lp
    def local(a, name):
        a = _two_d(a)
        return a.T if name in COLUMN_SHARDED else a

    sh = {name: local(w[name], name) for name in _WEIGHT_NAMES}
    m2 = {name: local(m[name], name) for name in _WEIGHT_NAMES}
    v2 = {name: local(v[name], name) for name in _WEIGHT_NAMES}

    def b16(name):
        return sh[name].astype(BF16)

    out = {}

    def update(name, landed):
        out[name] = adamw_sharded(landed, sh[name], m2[name], v2[name], "adamw_" + name)

    g_wg1, g_wu1, g_wd1, g_meta, g_conv = exchange(
        [b16("ffn1_w_gate"), b16("ffn1_w_up"), b16("ffn1_w_down"), sh["meta_tokens"], sh["conv_w"]],
        ["gather"] * 5, "gather_ffn1")
    wg1, wu1, wd1 = (g.reshape(D_FF, d) for g in (g_wg1, g_wu1, g_wd1))
    meta = assemble_cols(g_meta, "assemble_meta")
    conv_w = assemble_cols(g_conv, "assemble_conv")

    front = jnp.concatenate([jnp.zeros((PAD, d), F32), meta], axis=0)
    h0 = jnp.concatenate([jnp.broadcast_to(front[None], (nb, FIRST_FRAME, d)), x], axis=1).reshape(n, d)
    tgt_p = jnp.concatenate([jnp.zeros((nb, FIRST_FRAME, d), F32), tgt], axis=1).reshape(n, d)
    tables = _rope_tables(lp)
    zero_tail = jnp.zeros((1, HEAD_SLAB - D_QK), F32)
    gqh = jnp.concatenate([sh["q_head_norm"], zero_tail], axis=1)
    gkh = jnp.concatenate([sh["k_head_norm"], zero_tail], axis=1)
    wa = _block_diag(w["gate_a_w"][0]).astype(BF16)
    wx = _block_diag(w["gate_x_w"][0]).astype(BF16)

    (h1, u1, a1, b1), (g_in, g_uq, g_uk, g_uv, g_out, g_wg2) = ffn_fwd(
        h0, sh["ffn1_norm"], wg1, wu1, wd1, "ffn1_fwd",
        comm=([b16("w_in"), b16("w_uq"), b16("w_uk"), b16("w_uv"), b16("w_out"), b16("ffn2_w_gate")], ["gather"] * 6))
    mla_rows = MLA_IN - D_ROPE
    w_in = g_in.reshape(mla_rows + 2 * LRU_WIDTH, d)
    wm = jnp.concatenate([w_in[:mla_rows], jnp.zeros((D_ROPE, d), BF16)], axis=0)
    wl = w_in[mla_rows:]
    wuq = _slab_rows(g_uq.reshape(MLA_HEADS * D_QK, Q_RANK), D_QK)
    wuk = _slab_rows(g_uk.reshape(MLA_HEADS * D_NOPE, KV_RANK), D_NOPE)
    wuv = g_uv.reshape(MLA_HEADS * D_V, KV_RANK)
    w_out = g_out.reshape(d, d)

    u2, zm, zl = inproj_fwd(h1, sh["mix_norm"], wm, wl)
    q, k, vv, qn, cn = mla_prep_fwd(zm, sh["q_latent_norm"], sh["kv_latent_norm"], wuq, wuk, wuv, gqh, gkh, tables, lp)
    (y_mla, lse), (g_wu2, g_wd2) = attn_fwd(
        q, k, vv, nb, lp, comm=([b16("ffn2_w_up"), b16("ffn2_w_down")], ["gather"] * 2))
    wg2, wu2, wd2 = (g.reshape(D_FF, d) for g in (g_wg2, g_wu2, g_wd2))
    y_lru, hs = lru_fwd(zl, conv_w, sh["conv_b"], wa, wx, sh["gate_a_b"], sh["gate_x_b"], sh["lru_lambda"], nb, lp)
    h2, yn = outproj_fwd(h1, y_mla, y_lru, sh["attn_out_norm"], sh["lru_out_norm"], w_out)
    (h3, u3, a3, b3), _ = ffn_fwd(h2, sh["ffn2_norm"], wg2, wu2, wd2, "ffn2_fwd")
    loss, dh3, g_final = final_loss(h3, sh["final_norm"], tgt_p, lp)

    vec = {"final_norm": g_final}
    (dh2, da3, db3, sh3, vec["ffn2_norm"]), _ = ffn_bwd_act(dh3, h2, sh["ffn2_norm"], a3, b3, wg2, wu2, wd2, "ffn2_bwd")
    ff_shards = (N_DEV, D_FF // N_DEV, d)
    dwg2 = tn_matmul(da3, u3, "ffn2_dwg", "bf16").reshape(ff_shards)
    dwu2 = tn_matmul(db3, u3, "ffn2_dwu", "bf16").reshape(ff_shards)
    dwd2 = tn_matmul(sh3, dh3, "ffn2_dwd", "bf16").reshape(ff_shards)

    dy_mla, dy_lru, vec["attn_out_norm"], vec["lru_out_norm"] = outproj_bwd(
        dh2, y_mla, y_lru, sh["attn_out_norm"], sh["lru_out_norm"], w_out)
    dw_out = tn_matmul(yn, dh2, "dw_out", "bf16").reshape(N_DEV, d // N_DEV, d)
    (du, dgate, dconv, vec["conv_b"], vec["gate_a_b"], vec["gate_x_b"], vec["lru_lambda"], dga, dgx), (r_wg2,) = lru_bwd(
        zl, hs, dy_lru, conv_w, sh["conv_b"], wa, wx, sh["gate_a_b"], sh["gate_x_b"], sh["lru_lambda"], nb, lp,
        comm=([dwg2], ["scatter"]))
    update("ffn2_w_gate", r_wg2)

    (dq, dk, dv), (r_wu2,) = attn_bwd(q, k, vv, y_mla, dy_mla, lse, nb, lp, comm=([dwu2], ["scatter"]))
    update("ffn2_w_up", r_wu2)

    (dzm, dqr, dkr, vec["q_latent_norm"], vec["kv_latent_norm"], vec["q_head_norm"], vec["k_head_norm"]), (r_wd2,) = (
        mla_prep_bwd(dq, dk, dv, zm, qn, cn, sh["q_latent_norm"], sh["kv_latent_norm"], wuq, wuk, wuv, gqh, gkh,
                     tables, lp, comm=([dwd2], ["scatter"])))
    update("ffn2_w_down", r_wd2)
    dwuq = _unslab_rows(tn_matmul(dqr, qn, "dw_uq"), D_QK).reshape(N_DEV, -1, Q_RANK)
    dwuk = _unslab_rows(tn_matmul(dkr, cn, "dw_uk"), D_NOPE).reshape(N_DEV, -1, KV_RANK)
    dwuv = tn_matmul(dv, cn, "dw_uv").reshape(N_DEV, -1, KV_RANK)
    dh1, vec["mix_norm"] = inproj_bwd(dzm, du, dgate, dh2, h1, sh["mix_norm"], wm, wl)
    dw_in = jnp.concatenate([tn_matmul(dzm, u2, "dw_in_mla")[:mla_rows], tn_matmul(du, u2, "dw_in_u"),
                             tn_matmul(dgate, u2, "dw_in_gate")], axis=0).reshape(N_DEV, -1, d)

    (dh0, da1, db1, sh1, vec["ffn1_norm"]), landed = ffn_bwd_act(
        dh1, h0, sh["ffn1_norm"], a1, b1, wg1, wu1, wd1, "ffn1_bwd",
        comm=([dw_in, dwuq, dwuk, dwuv, dw_out, split_cols(dconv, "split_conv")], ["scatter"] * 6))
    for name, r in zip(("w_in", "w_uq", "w_uk", "w_uv", "w_out", "conv_w"), landed):
        update(name, r)

    dwg1 = tn_matmul(da1, u1, "ffn1_dwg", "bf16").reshape(ff_shards)
    dwu1, (r_wg1,) = tn_matmul(db1, u1, "ffn1_dwu", "bf16", comm=([dwg1], ["scatter"]))
    dwd1, (r_wu1,) = tn_matmul(sh1, dh1, "ffn1_dwd", "bf16", comm=([dwu1.reshape(ff_shards)], ["scatter"]))
    dwd1 = dwd1.reshape(ff_shards)
    dmeta = meta_grad(dh0, nb, lp)
    gates = [dga.reshape(LRU_WIDTH, LRU_BLOCK), dgx.reshape(LRU_WIDTH, LRU_BLOCK)]
    r_wd1, r_meta, r_vec, r_ga, r_gx = exchange(
        [dwd1, dmeta, pack_vectors(vec)] + gates, ["scatter"] * 2 + ["gather"] * 3, "exchange_last")
    update("ffn1_w_gate", r_wg1)
    update("ffn1_w_up", r_wu1)
    update("ffn1_w_down", r_wd1)
    update("meta_tokens", r_meta)

    small = [name for name, _ in VECTORS] + GATES
    res = adamw_small(r_vec, [r_ga, r_gx], [sh[nm] for nm in small], [m2[nm] for nm in small], [v2[nm] for nm in small])
    for i, name in enumerate(small):
        out[name] = [res[j][i] for j in range(4)]

    grad_x = dh0.reshape(nb, lp, d)[:, FIRST_FRAME:]
    loss = lax.psum(loss[0, 0], ("x", "y", "c"))
    def as_given(a, name):
        return (a.T if name in COLUMN_SHARDED else a).reshape(w[name].shape)

    cols = [[as_given(out[name][j], name) for name in _WEIGHT_NAMES] for j in range(4)]
    return (loss, grad_x, *cols[0], *cols[1], *cols[2], *cols[3])


def kernel(x, meta_tokens, ffn1_norm, ffn1_w_gate, ffn1_w_up, ffn1_w_down, mix_norm, w_in, q_latent_norm, w_uq, kv_latent_norm, w_uk, w_uv, q_head_norm, k_head_norm, conv_w, conv_b, gate_a_w, gate_a_b, gate_x_w, gate_x_b, lru_lambda, attn_out_norm, lru_out_norm, w_out, ffn2_norm, ffn2_w_gate, ffn2_w_up, ffn2_w_down, final_norm, loss_target, m_meta_tokens, m_ffn1_norm, m_ffn1_w_gate, m_ffn1_w_up, m_ffn1_w_down, m_mix_norm, m_w_in, m_q_latent_norm, m_w_uq, m_kv_latent_norm, m_w_uk, m_w_uv, m_q_head_norm, m_k_head_norm, m_conv_w, m_conv_b, m_gate_a_w, m_gate_a_b, m_gate_x_w, m_gate_x_b, m_lru_lambda, m_attn_out_norm, m_lru_out_norm, m_w_out, m_ffn2_norm, m_ffn2_w_gate, m_ffn2_w_up, m_ffn2_w_down, m_final_norm, v_meta_tokens, v_ffn1_norm, v_ffn1_w_gate, v_ffn1_w_up, v_ffn1_w_down, v_mix_norm, v_w_in, v_q_latent_norm, v_w_uq, v_kv_latent_norm, v_w_uk, v_w_uv, v_q_head_norm, v_k_head_norm, v_conv_w, v_conv_b, v_gate_a_w, v_gate_a_b, v_gate_x_w, v_gate_x_b, v_lru_lambda, v_attn_out_norm, v_lru_out_norm, v_w_out, v_ffn2_norm, v_ffn2_w_gate, v_ffn2_w_up, v_ffn2_w_down, v_final_norm):
    args = locals()
    w = {name: args[name] for name in _WEIGHT_NAMES}
    m = {name: args["m_" + name] for name in _WEIGHT_NAMES}
    v = {name: args["v_" + name] for name in _WEIGHT_NAMES}
    return train_step(x, loss_target, w, m, v)
```

```python
import math

import jax
import jax.numpy as jnp
from jax import lax
from jax.experimental import pallas as pl
from jax.experimental.pallas import tpu as pltpu

F32 = jnp.float32
BF16 = jnp.bfloat16

D_MODEL = 1024
CHUNK = 64
CHUNK_SHIFT = 6
N_META = 16
PAD = CHUNK - N_META
FIRST_FRAME = PAD + N_META
MLA_HEADS = 4
D_NOPE = 128
D_ROPE = 64
D_QK = D_NOPE + D_ROPE
D_V = 128
HEAD_SLAB = 256
KV_RANK = 256
Q_RANK = 384
ROPE_THETA = 10000.0
LRU_WIDTH = 512
LRU_BLOCKS = 8
LRU_BLOCK = 64
LRU_TILE = 128
CONV_W = 4
C_RGLRU = 8.0
D_FF = 2816
MLA_IN = 768
EPS = 1e-6
NEG_INF = -1e30
N_DEV = 8
LANES = 128
VMEM_LIMIT = 52 * 1024 * 1024
TN_ROWS = 4224
TN_X_BYTES = 12 * 1024 * 1024
TN_Y_BYTES = 9 * 1024 * 1024 // 2

ADAM_LR = 0.001
ADAM_B1 = 0.9
ADAM_B2 = 0.999
ADAM_EPS = 1e-08
ADAM_WD = 0.01
ADAM_STEP = 10

VMEM_WHOLE = pl.BlockSpec(memory_space=pltpu.VMEM)
HBM_WHOLE = pl.BlockSpec(memory_space=pl.ANY)


def _params(sems):
    if sems is None:
        return pltpu.CompilerParams(vmem_limit_bytes=VMEM_LIMIT)
    return pltpu.CompilerParams(dimension_semantics=sems, vmem_limit_bytes=VMEM_LIMIT)


def _tile(n, cap, mult=16):
    best = None
    for t in range(mult, min(n, cap) + 1, mult):
        if n % t == 0:
            best = t
    assert best is not None, (n, cap, mult)
    return best


def _row(tm, d):
    return pl.BlockSpec((tm, d), lambda i: (i, 0))


def _fixed(shape):
    return pl.BlockSpec(shape, lambda i: (0,) * len(shape))


def _mesh_position():
    return lax.axis_index("x"), lax.axis_index("y"), lax.axis_index("c")


def _flat_index(x, y, c):
    return 4 * x + 2 * y + c


def _peers(x, y, c):
    out = []
    for k in range(1, N_DEV):
        fx, fy, fc = (k >> 2) & 1, (k >> 1) & 1, k & 1
        out.append((1 - x if fx else x, 1 - y if fy else y, 1 - c if fc else c))
    return out


def _comm_out_shapes(srcs, modes):
    return [jax.ShapeDtypeStruct((N_DEV,) + s.shape if md == "gather" else s.shape, s.dtype)
            for s, md in zip(srcs, modes)]


def _comm_scratch(n):
    per_peer = n * (N_DEV - 1)
    return [pltpu.SemaphoreType.DMA((per_peer,)), pltpu.SemaphoreType.DMA((per_peer,)), pltpu.SemaphoreType.DMA((n,))]


class _Copies:
    def __init__(self, own, first, relay):
        self.own, self.first, self.relay = own, first, relay

    def start(self):
        for cp in self.own + self.first:
            cp.start()

    def forward(self):
        for arrival, onward in self.relay:
            arrival.wait_recv()
            onward.start()

    def finish(self):
        arrivals = [a for a, _ in self.relay]
        onward = [f for _, f in self.relay]
        for cp in self.first + onward:
            if not any(cp is a for a in arrivals):
                cp.wait_recv()
        for cp in self.first + onward:
            cp.wait_send()
        for cp in self.own:
            cp.wait()


def _comm_copies(src_refs, dst_refs, modes, send, recv, local):
    x, y, c = _mesh_position()
    me = _flat_index(x, y, c)
    n = len(modes)
    sibling = (x, y, 1 - c)
    chips = [(1 - x, y), (x, 1 - y), (1 - x, 1 - y)]

    def remote(src, dst, k, t, to):
        return pltpu.make_async_remote_copy(src_ref=src, dst_ref=dst, send_sem=send.at[k * n + t],
                                            recv_sem=recv.at[k * n + t], device_id=to,
                                            device_id_type=pl.DeviceIdType.MESH)

    own, first, relay = [], [], []
    for t, (src, dst, md) in enumerate(zip(src_refs, dst_refs, modes)):
        if md == "scatter":
            own.append(pltpu.make_async_copy(src.at[me], dst.at[me], local.at[t]))
            for k, peer in enumerate(_peers(x, y, c)):
                first.append(remote(src.at[_flat_index(*peer)], dst.at[me], k, t, peer))
        else:
            own.append(pltpu.make_async_copy(src, dst.at[me], local.at[t]))
            first.append(remote(src, dst.at[me], 0, t, sibling))
            for j, chip in enumerate(chips):
                arrival = remote(src, dst.at[me], 1 + j, t, (*chip, c))
                landed = dst.at[_flat_index(*chip, c)]
                first.append(arrival)
                relay.append((arrival, remote(landed, landed, 4 + j, t, sibling)))
    return _Copies(own, first, relay)


def _hosted(body, n_in, n_out, modes, grid):
    t = len(modes)
    total = math.prod(grid)

    def wrapped(*refs):
        ins, csrc = refs[:n_in], refs[n_in:n_in + t]
        outs = refs[n_in + t:n_in + t + n_out]
        cdst = refs[n_in + t + n_out:n_in + 2 * t + n_out]
        scratch = refs[n_in + 2 * t + n_out:-3]
        copies = _comm_copies(csrc, cdst, modes, *refs[-3:])
        step = pl.program_id(0)
        for axis in range(1, len(grid)):
            step = step * grid[axis] + pl.program_id(axis)

        @pl.when(step == 0)
        def _():
            copies.start()

        body(*ins, *outs, *scratch)

        @pl.when(step == (total * 3) // 5)
        def _():
            copies.forward()

        @pl.when(step == total - 1)
        def _():
            copies.finish()

    return wrapped


def _call(body, name, grid, in_specs, out_specs, out_shape, sems, args, scratch=(), comm=None):
    if comm is None:
        outs = pl.pallas_call(body, name=name, grid=grid, in_specs=in_specs, out_specs=out_specs, out_shape=out_shape,
                              scratch_shapes=list(scratch), compiler_params=_params(sems))(*args)
        return outs, []
    srcs, modes = comm
    n = len(modes)
    res = pl.pallas_call(
        _hosted(body, len(in_specs), len(out_specs), modes, grid), name=name, grid=grid,
        in_specs=list(in_specs) + [HBM_WHOLE] * n, out_specs=list(out_specs) + [HBM_WHOLE] * n,
        out_shape=list(out_shape) + _comm_out_shapes(srcs, modes),
        scratch_shapes=list(scratch) + _comm_scratch(n),
        compiler_params=_params(("arbitrary",) * len(grid)))(*args, *srcs)
    return res[:len(out_specs)], res[len(out_specs):]


def exchange(srcs, modes, name):
    n = len(modes)

    def body(*refs):
        copies = _comm_copies(refs[:n], refs[n:2 * n], modes, *refs[2 * n:])
        copies.start()
        copies.forward()
        copies.finish()

    return pl.pallas_call(body, name=name, in_specs=[HBM_WHOLE] * n, out_specs=[HBM_WHOLE] * n,
                          out_shape=_comm_out_shapes(srcs, modes), scratch_shapes=_comm_scratch(n))(*srcs)


def _nn(a, b):
    return jnp.dot(a, b, preferred_element_type=F32)


def _nt(a, b):
    return lax.dot_general(a, b, (((1,), (1,)), ((), ())), preferred_element_type=F32)


def _tn(a, b):
    return lax.dot_general(a, b, (((0,), (0,)), ((), ())), preferred_element_type=F32)


def _sig(x):
    return 1.0 / (1.0 + jnp.exp(-x))


def _rms_r(x, n=None):
    n = x.shape[-1] if n is None else n
    return lax.rsqrt(jnp.sum(x * x, axis=-1, keepdims=True) * (1.0 / n) + EPS)


def _rms_bwd(x, r, g, dy, n=None):
    n = x.shape[-1] if n is None else n
    xhat = x * r
    dxhat = dy * g
    dx = r * (dxhat - xhat * (jnp.sum(dxhat * xhat, axis=-1, keepdims=True) * (1.0 / n)))
    return dx, jnp.sum(dy * xhat, axis=0, keepdims=True)


def _accumulate(ref, val, first):
    @pl.when(first)
    def _():
        ref[...] = val

    @pl.when(jnp.logical_not(first))
    def _():
        ref[...] += val


_GELU_C = math.sqrt(2.0 / math.pi)


def _gelu_and_grad(x):
    inner = _GELU_C * (x + 0.044715 * x * x * x)
    t = jnp.tanh(inner)
    gelu = 0.5 * x * (1.0 + t)
    dgelu = 0.5 * (1.0 + t) + 0.5 * x * (1.0 - t * t) * _GELU_C * (1.0 + 3.0 * 0.044715 * x * x)
    return gelu, dgelu


def _log1p_small(t):
    return jnp.where(t < 1e-3, t * (1.0 - t * (0.5 - t * (1.0 / 3.0))), jnp.log(1.0 + t))


def _softplus(x):
    return jnp.maximum(x, 0.0) + _log1p_small(jnp.exp(-jnp.abs(x)))


def _sig_tanh(x):
    return 0.5 + 0.5 * jnp.tanh(0.5 * x)


def _ff_chunks(f):
    return 2 if (f // 2) % LANES == 0 else 1


def _swiglu_half(x, g_ref, wg_ref, wu_ref, wd_ref, a_ref, b_ref, fc):
    f = wg_ref.shape[0]
    u = (x * _rms_r(x) * g_ref[...]).astype(BF16)
    acc = jnp.zeros(x.shape, F32)
    for c in range(f // fc):
        cols = slice(c * fc, (c + 1) * fc)
        a = _nt(u, wg_ref[cols, :])
        b = _nt(u, wu_ref[cols, :])
        s = (a * _sig(a) * b).astype(BF16)
        acc = acc + _nn(s, wd_ref[cols, :])
        a_ref[:, cols] = a.astype(BF16)
        b_ref[:, cols] = b.astype(BF16)
    return x + 0.5 * acc, u


def ffn_fwd(h, g, wg, wu, wd, name, comm=None):
    n, d = h.shape
    f = wg.shape[0]
    tm = _tile(n, 528)
    fc = 2 * LANES if f % (2 * LANES) == 0 else f

    def body(h_ref, g_ref, wg_ref, wu_ref, wd_ref, ho_ref, u_ref, a_ref, b_ref):
        ho_ref[...], u_ref[...] = _swiglu_half(h_ref[...], g_ref, wg_ref, wu_ref, wd_ref, a_ref, b_ref, fc)

    return _call(
        body, name, (n // tm,),
        [_row(tm, d), _fixed((1, d)), VMEM_WHOLE, VMEM_WHOLE, VMEM_WHOLE],
        [_row(tm, d), _row(tm, d), _row(tm, f), _row(tm, f)],
        [jax.ShapeDtypeStruct((n, d), F32), jax.ShapeDtypeStruct((n, d), BF16),
         jax.ShapeDtypeStruct((n, f), BF16), jax.ShapeDtypeStruct((n, f), BF16)],
        ("parallel",), (h, g, wg, wu, wd), comm=comm)


def ffn_fwd_loss(h, g, wg, wu, wd, g_final, tgt, lp, name):
    n, d = h.shape
    f = wg.shape[0]
    tm = _tile(lp, 352)
    per_seq = lp // tm
    fc = 2 * LANES if f % (2 * LANES) == 0 else f

    def body(h_ref, g_ref, wg_ref, wu_ref, wd_ref, gf_ref, t_ref, dh_ref, u_ref, a_ref, b_ref, loss_ref, dgf_ref):
        i = pl.program_id(0)
        y, u_ref[...] = _swiglu_half(h_ref[...], g_ref, wg_ref, wu_ref, wd_ref, a_ref, b_ref, fc)
        dh_ref[...], part, dg = _loss_and_grad(y, gf_ref[...], t_ref[...], (i % per_seq) * tm)
        _accumulate(loss_ref, jnp.broadcast_to(part, (1, LANES)), i == 0)
        _accumulate(dgf_ref, dg, i == 0)

    outs, _ = _call(
        body, name, (n // tm,),
        [_row(tm, d), _fixed((1, d)), VMEM_WHOLE, VMEM_WHOLE, VMEM_WHOLE, _fixed((1, d)), _row(tm, d)],
        [_row(tm, d), _row(tm, d), _row(tm, f), _row(tm, f), _fixed((1, LANES)), _fixed((1, d))],
        [jax.ShapeDtypeStruct((n, d), F32), jax.ShapeDtypeStruct((n, d), BF16),
         jax.ShapeDtypeStruct((n, f), BF16), jax.ShapeDtypeStruct((n, f), BF16),
         jax.ShapeDtypeStruct((1, LANES), F32), jax.ShapeDtypeStruct((1, d), F32)],
        ("arbitrary",), (h, g, wg, wu, wd, g_final, tgt))
    return outs


def ffn_bwd_act(dh, h, g, a, b, wg, wu, wd, name, comm=None):
    n, d = h.shape
    f = wg.shape[0]
    tm = _tile(n, 192)
    nc = _ff_chunks(f)
    fc = f // nc

    def body(dh_ref, h_ref, g_ref, a_ref, b_ref, wg_ref, wu_ref, wd_ref,
             dhi_ref, da_ref, db_ref, sh_ref, dg_ref):
        x = h_ref[...]
        dy = dh_ref[...]
        r = _rms_r(x)
        dhh = (0.5 * dy).astype(BF16)
        du = jnp.zeros((tm, d), F32)
        for c in range(nc):
            cols = slice(c * fc, (c + 1) * fc)
            ds = _nt(dhh, wd_ref[cols, :])
            av = a_ref[:, cols].astype(F32)
            bv = b_ref[:, cols].astype(F32)
            sg = _sig(av)
            sil = av * sg
            da = (ds * bv * (sg * (1.0 + av * (1.0 - sg)))).astype(BF16)
            db = (ds * sil).astype(BF16)
            da_ref[:, cols] = da
            db_ref[:, cols] = db
            sh_ref[:, cols] = (0.5 * sil * bv).astype(BF16)
            du = du + _nn(da, wg_ref[cols, :]) + _nn(db, wu_ref[cols, :])
        dx, dg = _rms_bwd(x, r, g_ref[...], du)
        dhi_ref[...] = dy + dx
        _accumulate(dg_ref, dg, pl.program_id(0) == 0)

    return _call(
        body, name, (n // tm,),
        [_row(tm, d), _row(tm, d), _fixed((1, d)), _row(tm, f), _row(tm, f), VMEM_WHOLE, VMEM_WHOLE, VMEM_WHOLE],
        [_row(tm, d), _row(tm, f), _row(tm, f), _row(tm, f), _fixed((1, d))],
        [jax.ShapeDtypeStruct((n, d), F32), jax.ShapeDtypeStruct((n, f), BF16),
         jax.ShapeDtypeStruct((n, f), BF16), jax.ShapeDtypeStruct((n, f), BF16),
         jax.ShapeDtypeStruct((1, d), F32)],
        ("arbitrary",), (dh, h, g, a, b, wg, wu, wd), comm=comm)


def tn_matmul(x, y, name, out="f32", comm=None):
    n, k = x.shape
    m = y.shape[1]
    tm = _tile(n, TN_ROWS)
    kc, mc = k, (512 if m % 512 == 0 else m)
    while tm * kc * x.dtype.itemsize > TN_X_BYTES and kc % (2 * LANES) == 0:
        kc //= 2
    while tm * mc * y.dtype.itemsize > TN_Y_BYTES and mc % (2 * LANES) == 0:
        mc //= 2
    steps = n // tm

    def body(x_ref, y_ref, o_ref, *acc):
        i = pl.program_id(2)
        part = _tn(x_ref[...].astype(BF16), y_ref[...].astype(BF16))
        if steps == 1:
            o_ref[...] = part.astype(o_ref.dtype)
        elif out == "f32":
            _accumulate(o_ref, part, i == 0)
        else:
            _accumulate(acc[0], part, i == 0)

            @pl.when(i == steps - 1)
            def _():
                o_ref[...] = acc[0][...].astype(BF16)

    out_shape = jax.ShapeDtypeStruct((k, m), F32 if out == "f32" else BF16)
    (res,), landed = _call(
        body, name, (k // kc, m // mc, steps),
        [pl.BlockSpec((tm, kc), lambda a, b, i: (i, a)), pl.BlockSpec((tm, mc), lambda a, b, i: (i, b))],
        [pl.BlockSpec((kc, mc), lambda a, b, i: (a, b))], [out_shape], ("parallel", "parallel", "arbitrary"), (x, y),
        scratch=[pltpu.VMEM((kc, mc), F32)] if (out == "bf16" and steps > 1) else [], comm=comm)
    return (res, landed) if comm is not None else res


def inproj_fwd(h, g, wm, wl):
    n, d = h.shape
    tm = _tile(n, 352)

    def body(h_ref, g_ref, wm_ref, wl_ref, u_ref, zm_ref, zl_ref):
        x = h_ref[...]
        u = (x * _rms_r(x) * g_ref[...]).astype(BF16)
        u_ref[...] = u
        zm_ref[...] = _nt(u, wm_ref[...])
        zl_ref[...] = _nt(u, wl_ref[...])

    return pl.pallas_call(
        body, name="inproj_fwd", grid=(n // tm,),
        in_specs=[_row(tm, d), _fixed((1, d)), VMEM_WHOLE, VMEM_WHOLE],
        out_specs=[_row(tm, d), _row(tm, MLA_IN), _row(tm, 2 * LRU_WIDTH)],
        out_shape=[jax.ShapeDtypeStruct((n, d), BF16), jax.ShapeDtypeStruct((n, MLA_IN), F32),
                   jax.ShapeDtypeStruct((n, 2 * LRU_WIDTH), F32)],
        compiler_params=_params(("parallel",)),
    )(h, g, wm, wl)


def inproj_bwd(dzm, du, dgate, dh2, h, g, wm, wl):
    n, d = h.shape
    tm = _tile(n, 352)

    def body(dzm_ref, du_ref, dgt_ref, dh2_ref, h_ref, g_ref, wm_ref, wl_ref, dh_ref, dg_ref):
        x = h_ref[...]
        dun = (_nn(dzm_ref[...].astype(BF16), wm_ref[...])
               + _nn(du_ref[...].astype(BF16), wl_ref[:LRU_WIDTH, :])
               + _nn(dgt_ref[...].astype(BF16), wl_ref[LRU_WIDTH:, :]))
        dx, dg = _rms_bwd(x, _rms_r(x), g_ref[...], dun)
        dh_ref[...] = dh2_ref[...] + dx
        _accumulate(dg_ref, dg, pl.program_id(0) == 0)

    return pl.pallas_call(
        body, name="inproj_bwd", grid=(n // tm,),
        in_specs=[_row(tm, MLA_IN), _row(tm, LRU_WIDTH), _row(tm, LRU_WIDTH), _row(tm, d), _row(tm, d),
                  _fixed((1, d)), VMEM_WHOLE, VMEM_WHOLE],
        out_specs=[_row(tm, d), _fixed((1, d))],
        out_shape=[jax.ShapeDtypeStruct((n, d), F32), jax.ShapeDtypeStruct((1, d), F32)],
        compiler_params=_params(("arbitrary",)),
    )(dzm, du, dgate, dh2, h, g, wm, wl)


def _rope_tables(lp):
    pos = jnp.arange(lp, dtype=F32) - float(PAD)
    half = D_ROPE // 2
    inv_freq = ROPE_THETA ** (-jnp.arange(0, half, dtype=F32) / half)
    ang = pos[:, None] * inv_freq[None, :]
    cos, sin = jnp.cos(ang), jnp.sin(ang)
    one = jnp.ones((lp, D_NOPE), F32)
    z_nope = jnp.zeros((lp, D_NOPE), F32)
    z_half = jnp.zeros((lp, half), F32)
    z_tail = jnp.zeros((lp, HEAD_SLAB - D_QK), F32)
    cosr = jnp.concatenate([one, cos, cos, z_tail], axis=1)
    sin_up = jnp.concatenate([z_nope, z_half, sin, z_tail], axis=1)
    sin_dn = jnp.concatenate([z_nope, -sin, z_half, z_tail], axis=1)
    return cosr, sin_up, sin_dn


def _rope(x, cosr, sin_up, sin_dn):
    half = D_ROPE // 2
    return x * cosr + pltpu.roll(x, half, axis=1) * sin_up + pltpu.roll(x, HEAD_SLAB - half, axis=1) * sin_dn


def _rope_bwd(dy, cosr, sin_up, sin_dn):
    half = D_ROPE // 2
    return (dy * cosr + pltpu.roll(dy * sin_up, HEAD_SLAB - half, axis=1)
            + pltpu.roll(dy * sin_dn, half, axis=1))


def _k_rope_slab(zm_tile):
    tm = zm_tile.shape[0]
    krp = zm_tile[:, Q_RANK + KV_RANK:MLA_IN]
    return jnp.concatenate([jnp.zeros((tm, D_NOPE), F32), krp], axis=1)


def mla_prep_fwd(zm, gql, gkvl, wuq, wuk, wuv, gqh, gkh, tables, lp):
    n = zm.shape[0]
    tm = _tile(lp, 352)
    per_seq = lp // tm
    width = MLA_HEADS * HEAD_SLAB
    scale = 1.0 / math.sqrt(D_QK)

    def body(zm_ref, gql_ref, gkvl_ref, wuq_ref, wuk_ref, wuv_ref, gqh_ref, gkh_ref,
             cos_ref, up_ref, dn_ref, q_ref, k_ref, v_ref, qn_ref, cn_ref):
        z = zm_ref[...]
        cq = z[:, :Q_RANK]
        ckv = z[:, Q_RANK:Q_RANK + KV_RANK]
        qn = (cq * _rms_r(cq) * gql_ref[...]).astype(BF16)
        cn = (ckv * _rms_r(ckv) * gkvl_ref[...]).astype(BF16)
        qn_ref[...] = qn
        cn_ref[...] = cn
        q_raw = _nt(qn, wuq_ref[...])
        k_raw = _nt(cn, wuk_ref[...])
        v_ref[...] = _nt(cn, wuv_ref[...]).astype(BF16)
        kr_slab = _k_rope_slab(z)
        cosr, sin_up, sin_dn = cos_ref[...], up_ref[...], dn_ref[...]
        for hd in range(MLA_HEADS):
            cols = slice(hd * HEAD_SLAB, (hd + 1) * HEAD_SLAB)
            xq = q_raw[:, cols]
            yq = _rope(xq * _rms_r(xq, D_QK) * gqh_ref[...], cosr, sin_up, sin_dn)
            q_ref[:, cols] = (yq * scale).astype(BF16)
            xk = k_raw[:, cols] + kr_slab
            yk = _rope(xk * _rms_r(xk, D_QK) * gkh_ref[...], cosr, sin_up, sin_dn)
            k_ref[:, cols] = yk.astype(BF16)

    tab = pl.BlockSpec((tm, HEAD_SLAB), lambda i: (i % per_seq, 0))
    return pl.pallas_call(
        body, name="mla_prep_fwd", grid=(n // tm,),
        in_specs=[_row(tm, MLA_IN), _fixed((1, Q_RANK)), _fixed((1, KV_RANK)), VMEM_WHOLE, VMEM_WHOLE, VMEM_WHOLE,
                  _fixed((1, HEAD_SLAB)), _fixed((1, HEAD_SLAB)), tab, tab, tab],
        out_specs=[_row(tm, width), _row(tm, width), _row(tm, MLA_HEADS * D_V), _row(tm, Q_RANK), _row(tm, KV_RANK)],
        out_shape=[jax.ShapeDtypeStruct((n, width), BF16), jax.ShapeDtypeStruct((n, width), BF16),
                   jax.ShapeDtypeStruct((n, MLA_HEADS * D_V), BF16), jax.ShapeDtypeStruct((n, Q_RANK), BF16),
                   jax.ShapeDtypeStruct((n, KV_RANK), BF16)],
        compiler_params=_params(("parallel",)),
    )(zm, gql, gkvl, wuq, wuk, wuv, gqh, gkh, *tables)


def mla_prep_bwd(dq, dk, dv, zm, qn, cn, gql, gkvl, wuq, wuk, wuv, gqh, gkh, tables, lp, comm=None):
    n = zm.shape[0]
    tm = _tile(lp, 704)
    per_seq = lp // tm
    width = MLA_HEADS * HEAD_SLAB
    scale = 1.0 / math.sqrt(D_QK)

    def body(dq_ref, dk_ref, dv_ref, zm_ref, qn_ref, cn_ref, gql_ref, gkvl_ref, wuq_ref, wuk_ref, wuv_ref,
             gqh_ref, gkh_ref, cos_ref, up_ref, dn_ref,
             dzm_ref, dqr_ref, dkr_ref, dgql_ref, dgkvl_ref, dgqh_ref, dgkh_ref):
        z = zm_ref[...]
        cq = z[:, :Q_RANK]
        ckv = z[:, Q_RANK:Q_RANK + KV_RANK]
        q_raw = _nt(qn_ref[...], wuq_ref[...])
        k_raw = _nt(cn_ref[...], wuk_ref[...])
        kr_slab = _k_rope_slab(z)
        cosr, sin_up, sin_dn = cos_ref[...], up_ref[...], dn_ref[...]
        dgq = jnp.zeros((1, HEAD_SLAB), F32)
        dgk = jnp.zeros((1, HEAD_SLAB), F32)
        dkrp = jnp.zeros((tm, HEAD_SLAB - D_NOPE), F32)
        for hd in range(MLA_HEADS):
            cols = slice(hd * HEAD_SLAB, (hd + 1) * HEAD_SLAB)
            xq = q_raw[:, cols]
            dxn = _rope_bwd(dq_ref[:, cols] * scale, cosr, sin_up, sin_dn)
            dxq, dg = _rms_bwd(xq, _rms_r(xq, D_QK), gqh_ref[...], dxn, D_QK)
            dgq = dgq + dg
            dqr_ref[:, cols] = dxq.astype(BF16)
            xk = k_raw[:, cols] + kr_slab
            dxn = _rope_bwd(dk_ref[:, cols], cosr, sin_up, sin_dn)
            dxk, dg = _rms_bwd(xk, _rms_r(xk, D_QK), gkh_ref[...], dxn, D_QK)
            dgk = dgk + dg
            dkr_ref[:, cols] = dxk.astype(BF16)
            dkrp = dkrp + dxk[:, D_NOPE:]
        dqn = _nn(dqr_ref[...], wuq_ref[...])
        dcn = _nn(dkr_ref[...], wuk_ref[...]) + _nn(dv_ref[...].astype(BF16), wuv_ref[...])
        dcq, dg1 = _rms_bwd(cq, _rms_r(cq), gql_ref[...], dqn)
        dckv, dg2 = _rms_bwd(ckv, _rms_r(ckv), gkvl_ref[...], dcn)
        dzm_ref[:, :Q_RANK] = dcq
        dzm_ref[:, Q_RANK:Q_RANK + KV_RANK] = dckv
        dzm_ref[:, Q_RANK + KV_RANK:] = dkrp
        first = pl.program_id(0) == 0
        _accumulate(dgql_ref, dg1, first)
        _accumulate(dgkvl_ref, dg2, first)
        _accumulate(dgqh_ref, dgq, first)
        _accumulate(dgkh_ref, dgk, first)

    tab = pl.BlockSpec((tm, HEAD_SLAB), lambda i: (i % per_seq, 0))
    return _call(
        body, "mla_prep_bwd", (n // tm,),
        [_row(tm, width), _row(tm, width), _row(tm, MLA_HEADS * D_V), _row(tm, MLA_IN),
         _row(tm, Q_RANK), _row(tm, KV_RANK), _fixed((1, Q_RANK)), _fixed((1, KV_RANK)),
         VMEM_WHOLE, VMEM_WHOLE, VMEM_WHOLE, _fixed((1, HEAD_SLAB)), _fixed((1, HEAD_SLAB)), tab, tab, tab],
        [_row(tm, MLA_IN), _row(tm, width), _row(tm, width), _fixed((1, Q_RANK)), _fixed((1, KV_RANK)),
         _fixed((1, HEAD_SLAB)), _fixed((1, HEAD_SLAB))],
        [jax.ShapeDtypeStruct((n, MLA_IN), F32), jax.ShapeDtypeStruct((n, width), BF16),
         jax.ShapeDtypeStruct((n, width), BF16), jax.ShapeDtypeStruct((1, Q_RANK), F32),
         jax.ShapeDtypeStruct((1, KV_RANK), F32), jax.ShapeDtypeStruct((1, HEAD_SLAB), F32),
         jax.ShapeDtypeStruct((1, HEAD_SLAB), F32)],
        ("arbitrary",), (dq, dk, dv, zm, qn, cn, gql, gkvl, wuq, wuk, wuv, gqh, gkh, *tables), comm=comm)


def _attn_tile(lp):
    return _tile(lp, 704, CHUNK)


def _chunk_mask(i, j, t):
    qpos = i * t + lax.broadcasted_iota(jnp.int32, (t, t), 0)
    kpos = j * t + lax.broadcasted_iota(jnp.int32, (t, t), 1)
    same_or_earlier = jnp.right_shift(kpos, CHUNK_SHIFT) <= jnp.right_shift(qpos, CHUNK_SHIFT)
    return jnp.logical_and(same_or_earlier, kpos >= PAD)


def _masked_scores(s, i, j, t, diagonal):
    if diagonal:
        return jnp.where(_chunk_mask(i, j, t), s, NEG_INF)
    kpos = j * t + lax.broadcasted_iota(jnp.int32, (1, t), 1)
    return s + jnp.where(kpos < PAD, NEG_INF, 0.0)


def attn_fwd(q, k, v, nb, lp, comm=None):
    n = q.shape[0]
    t = _attn_tile(lp)
    nq = lp // t

    def body(q_ref, k_ref, v_ref, o_ref, lse_ref):
        i = pl.program_id(2)
        qv = q_ref[...]

        def kv_step(j, carry, diagonal=False):
            m, l, acc = carry
            off = pl.multiple_of(j * t, t)
            s = _masked_scores(_nt(qv, k_ref[pl.ds(off, t), :]), i, j, t, diagonal)
            m_new = jnp.maximum(m, jnp.max(s, axis=-1, keepdims=True))
            p = jnp.exp(s - m_new)
            alpha = jnp.exp(m - m_new)
            l = alpha * l + jnp.sum(p, axis=-1, keepdims=True)
            acc = alpha * acc + _nn(p.astype(BF16), v_ref[pl.ds(off, t), :])
            return m_new, l, acc

        init = (jnp.full((t, 1), NEG_INF, F32), jnp.zeros((t, 1), F32), jnp.zeros((t, D_V), F32))
        m, l, acc = kv_step(i, lax.fori_loop(0, i, kv_step, init), diagonal=True)
        o_ref[...] = acc * (1.0 / l)
        lse_ref[0] = jnp.broadcast_to(m + jnp.log(l), (t, LANES))

    return _call(
        body, "attn_fwd", (nb, MLA_HEADS, nq),
        [pl.BlockSpec((t, HEAD_SLAB), lambda b, h, i: (b * nq + i, h)),
         pl.BlockSpec((lp, HEAD_SLAB), lambda b, h, i: (b, h)),
         pl.BlockSpec((lp, D_V), lambda b, h, i: (b, h))],
        [pl.BlockSpec((t, D_V), lambda b, h, i: (b * nq + i, h)),
         pl.BlockSpec((1, t, LANES), lambda b, h, i: (h, b * nq + i, 0))],
        [jax.ShapeDtypeStruct((n, MLA_HEADS * D_V), F32), jax.ShapeDtypeStruct((MLA_HEADS, n, LANES), F32)],
        ("parallel", "parallel", "parallel"), (q, k, v), comm=comm)


def attn_bwd(q, k, v, o, do, lse, nb, lp, comm=None):
    n = q.shape[0]
    t = _attn_tile(lp)
    nq = lp // t

    def body(q_ref, k_ref, v_ref, o_ref, do_ref, lse_ref, dq_ref, dk_ref, dv_ref):
        dk_ref[...] = jnp.zeros_like(dk_ref)
        dv_ref[...] = jnp.zeros_like(dv_ref)

        def q_step(i, _):
            qoff = pl.multiple_of(i * t, t)
            qv = q_ref[pl.ds(qoff, t), :]
            dov = do_ref[pl.ds(qoff, t), :]
            delta = jnp.sum(o_ref[pl.ds(qoff, t), :] * dov, axis=-1, keepdims=True)
            lse_q = jnp.max(lse_ref[0, pl.ds(qoff, t), :], axis=-1, keepdims=True)
            do16 = dov.astype(BF16)

            def kv_step(j, dq_acc, diagonal=False):
                koff = pl.multiple_of(j * t, t)
                kv = k_ref[pl.ds(koff, t), :]
                s = _masked_scores(_nt(qv, kv), i, j, t, diagonal)
                p = jnp.exp(s - lse_q)
                dp = _nt(do16, v_ref[pl.ds(koff, t), :])
                ds16 = (p * (dp - delta)).astype(BF16)
                dv_ref[pl.ds(koff, t), :] += _tn(p.astype(BF16), do16)
                dk_ref[pl.ds(koff, t), :] += _tn(ds16, qv)
                return dq_acc + _nn(ds16, kv)

            earlier = lax.fori_loop(0, i, kv_step, jnp.zeros((t, HEAD_SLAB), F32))
            dq_ref[pl.ds(qoff, t), :] = kv_step(i, earlier, diagonal=True)
            return 0

        lax.fori_loop(0, nq, q_step, 0)

    wide = pl.BlockSpec((lp, HEAD_SLAB), lambda b, h: (b, h))
    thin = pl.BlockSpec((lp, D_V), lambda b, h: (b, h))
    width = MLA_HEADS * HEAD_SLAB
    return _call(
        body, "attn_bwd", (nb, MLA_HEADS),
        [wide, wide, thin, thin, thin, pl.BlockSpec((1, lp, LANES), lambda b, h: (h, b, 0))],
        [wide, wide, thin],
        [jax.ShapeDtypeStruct((n, width), F32), jax.ShapeDtypeStruct((n, width), F32),
         jax.ShapeDtypeStruct((n, MLA_HEADS * D_V), F32)],
        ("parallel", "parallel"), (q, k, v, o, do, lse), comm=comm)


def _seq_rows(nb, lp, width):
    rows = lax.broadcasted_iota(jnp.int32, (lp, width), 0)
    return jnp.concatenate([rows] * nb, axis=0) if nb > 1 else rows


def _lru_gates(u, w_ref, cb, wa, wx, ba, bx, lam):
    xc = (cb + w_ref[pl.ds(3, 1), :] * u + w_ref[pl.ds(2, 1), :] * pltpu.roll(u, 1, axis=0)
          + w_ref[pl.ds(1, 1), :] * pltpu.roll(u, 2, axis=0) + w_ref[pl.ds(0, 1), :] * pltpu.roll(u, 3, axis=0))
    xc16 = xc.astype(BF16)
    ra = _sig_tanh(_nn(xc16, wa) + ba)
    ia = _sig_tanh(_nn(xc16, wx) + bx)
    sp = _softplus(-lam)
    log_a = -C_RGLRU * ra * sp
    a = jnp.exp(log_a)
    x2 = 2.0 * log_a
    mult = jnp.sqrt(jnp.where(x2 > -1e-2, -x2 * (1.0 + x2 * (0.5 + x2 * (1.0 / 6.0))), 1.0 - a * a))
    return xc, xc16, ra, ia, sp, a, mult


def _scan_block_rows(width):
    return lax.broadcasted_iota(jnp.int32, (8, width), 0)


def lru_fwd(zl, conv_w, conv_b, wa, wx, ba, bx, lam, nb, lp):
    n = zl.shape[0]
    w = LRU_TILE
    nt = LRU_WIDTH // w
    nblk = lp // 8

    def body(u_ref, gt_ref, cw_ref, cb_ref, wa_ref, wx_ref, ba_ref, bx_ref, lam_ref, y_ref, h_ref, a_s, b_s):
        u = u_ref[...]
        xc, _, _, ia, _, a, mult = _lru_gates(u, cw_ref, cb_ref[...], wa_ref[...], wx_ref[...],
                                              ba_ref[...], bx_ref[...], lam_ref[...])
        row = _seq_rows(nb, lp, w)
        mult = jnp.where(row == PAD, 1.0, mult)
        a_s[...] = a
        b_s[...] = jnp.where(row < PAD, 0.0, mult * (ia * xc))
        r8 = _scan_block_rows(w)

        def blk(i, carry):
            out = []
            for s_id in range(nb):
                off = pl.multiple_of(s_id * lp + i * 8, 8)
                av = a_s[pl.ds(off, 8), :]
                bv = b_s[pl.ds(off, 8), :]
                for sh in (1, 2, 4):
                    keep = r8 >= sh
                    bv = jnp.where(keep, av * pltpu.roll(bv, sh, axis=0) + bv, bv)
                    av = jnp.where(keep, av * pltpu.roll(av, sh, axis=0), av)
                hv = bv + av * carry[s_id]
                h_ref[pl.ds(off, 8), :] = hv
                out.append(jnp.sum(jnp.where(r8 == 7, hv, 0.0), axis=0, keepdims=True))
            return tuple(out)

        lax.fori_loop(0, nblk, blk, tuple(jnp.zeros((1, w), F32) for _ in range(nb)))
        gelu, _ = _gelu_and_grad(gt_ref[...])
        y_ref[...] = h_ref[...] * gelu

    col = lambda c: (0, c)
    return pl.pallas_call(
        body, name="lru_fwd", grid=(nt,),
        in_specs=[pl.BlockSpec((n, w), col), pl.BlockSpec((n, w), lambda c: (0, nt + c)),
                  pl.BlockSpec((CONV_W, w), col), pl.BlockSpec((1, w), col),
                  pl.BlockSpec((w, w), lambda c: (c, c)), pl.BlockSpec((w, w), lambda c: (c, c)),
                  pl.BlockSpec((1, w), col), pl.BlockSpec((1, w), col), pl.BlockSpec((1, w), col)],
        out_specs=[pl.BlockSpec((n, w), col), pl.BlockSpec((n, w), col)],
        out_shape=[jax.ShapeDtypeStruct((n, LRU_WIDTH), F32), jax.ShapeDtypeStruct((n, LRU_WIDTH), F32)],
        scratch_shapes=[pltpu.VMEM((n, w), F32), pltpu.VMEM((n, w), F32)],
        compiler_params=_params(("parallel",)),
    )(zl, zl, conv_w, conv_b, wa, wx, ba, bx, lam)


def lru_bwd(zl, hs, dy, conv_w, conv_b, wa, wx, ba, bx, lam, nb, lp, comm=None):
    n = zl.shape[0]
    w = LRU_TILE
    nt = LRU_WIDTH // w
    nblk = lp // 8

    def body(u_ref, gt_ref, h_ref, dy_ref, cw_ref, cb_ref, wa_ref, wx_ref, ba_ref, bx_ref, lam_ref,
             du_ref, dgt_ref, dcw_ref, dcb_ref, dba_ref, dbx_ref, dlam_ref, dwa_ref, dwx_ref,
             c_s, d_s, g_s, dwa_s, dwx_s):
        u = u_ref[...]
        lam = lam_ref[...]
        xc, xc16, ra, ia, sp, a, mult = _lru_gates(u, cw_ref, cb_ref[...], wa_ref[...], wx_ref[...],
                                                   ba_ref[...], bx_ref[...], lam)
        row = lax.broadcasted_iota(jnp.int32, (lp, w), 0)
        hv = h_ref[...]
        dyv = dy_ref[...]
        gelu, dgelu = _gelu_and_grad(gt_ref[...])
        dgt_ref[...] = jnp.where(row >= PAD, dyv * hv * dgelu, 0.0)
        c_s[...] = pltpu.roll(a, lp - 1, axis=0)
        d_s[...] = dyv * gelu
        r8 = _scan_block_rows(w)

        def blk(ii, carry):
            off = pl.multiple_of((nblk - 1 - ii) * 8, 8)
            cv = c_s[pl.ds(off, 8), :]
            dv = d_s[pl.ds(off, 8), :]
            for sh in (1, 2, 4):
                keep = r8 < 8 - sh
                dv = jnp.where(keep, cv * pltpu.roll(dv, 8 - sh, axis=0) + dv, dv)
                cv = jnp.where(keep, cv * pltpu.roll(cv, 8 - sh, axis=0), cv)
            gv = dv + cv * carry
            g_s[pl.ds(off, 8), :] = gv
            return jnp.sum(jnp.where(r8 == 0, gv, 0.0), axis=0, keepdims=True)

        lax.fori_loop(0, nblk, blk, jnp.zeros((1, w), F32))
        gv = g_s[...]
        first_row = row == PAD
        db = jnp.where(row >= PAD, gv, 0.0)
        da = jnp.where(row > PAD, gv * pltpu.roll(hv, 1, axis=0), 0.0)
        mult_eff = jnp.where(first_row, 1.0, mult)
        dmult = jnp.where(first_row, 0.0, db * (ia * xc))
        dia = db * mult_eff * xc
        dxc = db * mult_eff * ia
        dla = da * a - dmult * (a * a) / mult
        dra = dla * (-C_RGLRU * sp)
        dsp = jnp.sum(dla * (-C_RGLRU * ra), axis=0, keepdims=True)
        dpa = dra * ra * (1.0 - ra)
        dpx = dia * ia * (1.0 - ia)
        dpa16 = dpa.astype(BF16)
        dpx16 = dpx.astype(BF16)
        dxc = dxc + _nt(dpa16, wa_ref[...]) + _nt(dpx16, wx_ref[...])
        du = cw_ref[pl.ds(CONV_W - 1, 1), :] * dxc
        dcw = [jnp.sum(dxc * u, axis=0, keepdims=True)]
        for tap in range(1, CONV_W):
            dcw.insert(0, jnp.sum(dxc * pltpu.roll(u, tap, axis=0), axis=0, keepdims=True))
            du = du + cw_ref[pl.ds(CONV_W - 1 - tap, 1), :] * pltpu.roll(dxc, lp - tap, axis=0)
        du_ref[...] = jnp.where(row >= PAD, du, 0.0)
        first = pl.program_id(1) == 0
        _accumulate(dlam_ref, -_sig(-lam) * dsp, first)
        _accumulate(dba_ref, jnp.sum(dpa, axis=0, keepdims=True), first)
        _accumulate(dbx_ref, jnp.sum(dpx, axis=0, keepdims=True), first)
        _accumulate(dcb_ref, jnp.sum(dxc, axis=0, keepdims=True), first)
        _accumulate(dcw_ref, jnp.concatenate(dcw, axis=0), first)
        _accumulate(dwa_s, _tn(xc16, dpa16), first)
        _accumulate(dwx_s, _tn(xc16, dpx16), first)

        @pl.when(pl.program_id(1) == nb - 1)
        def _():
            for j in range(w // LRU_BLOCK):
                blk_rows = slice(j * LRU_BLOCK, (j + 1) * LRU_BLOCK)
                dwa_ref[0, blk_rows, :] = dwa_s[blk_rows, blk_rows]
                dwx_ref[0, blk_rows, :] = dwx_s[blk_rows, blk_rows]

    col = lambda c, b: (0, c)
    vec = pl.BlockSpec((1, w), col)
    mat = pl.BlockSpec((w, w), lambda c, b: (c, c))
    big = pl.BlockSpec((lp, w), lambda c, b: (b, c))
    dmat = pl.BlockSpec((1, w, LRU_BLOCK), lambda c, b: (c, 0, 0))
    return _call(
        body, "lru_bwd", (nt, nb),
        [big, pl.BlockSpec((lp, w), lambda c, b: (b, nt + c)), big, big,
         pl.BlockSpec((CONV_W, w), col), vec, mat, mat, vec, vec, vec],
        [big, big, pl.BlockSpec((CONV_W, w), col), vec, vec, vec, vec, dmat, dmat],
        [jax.ShapeDtypeStruct((n, LRU_WIDTH), F32), jax.ShapeDtypeStruct((n, LRU_WIDTH), F32),
         jax.ShapeDtypeStruct((CONV_W, LRU_WIDTH), F32), jax.ShapeDtypeStruct((1, LRU_WIDTH), F32),
         jax.ShapeDtypeStruct((1, LRU_WIDTH), F32), jax.ShapeDtypeStruct((1, LRU_WIDTH), F32),
         jax.ShapeDtypeStruct((1, LRU_WIDTH), F32), jax.ShapeDtypeStruct((nt, w, LRU_BLOCK), F32),
         jax.ShapeDtypeStruct((nt, w, LRU_BLOCK), F32)],
        ("parallel", "arbitrary"), (zl, zl, hs, dy, conv_w, conv_b, wa, wx, ba, bx, lam),
        scratch=[pltpu.VMEM((lp, w), F32), pltpu.VMEM((lp, w), F32), pltpu.VMEM((lp, w), F32),
                 pltpu.VMEM((w, w), F32), pltpu.VMEM((w, w), F32)], comm=comm)


def outproj_fwd(h, ya, yl, gao, glo, wout):
    n, d = h.shape
    half = ya.shape[1]
    tm = _tile(n, 704)

    def body(h_ref, ya_ref, yl_ref, gao_ref, glo_ref, w_ref, ho_ref, yn_ref):
        xa = ya_ref[...]
        xl = yl_ref[...]
        na = (xa * _rms_r(xa) * gao_ref[...]).astype(BF16)
        nl = (xl * _rms_r(xl) * glo_ref[...]).astype(BF16)
        yn_ref[:, :half] = na
        yn_ref[:, half:] = nl
        ho_ref[...] = h_ref[...] + _nn(na, w_ref[:half, :]) + _nn(nl, w_ref[half:, :])

    return pl.pallas_call(
        body, name="outproj_fwd", grid=(n // tm,),
        in_specs=[_row(tm, d), _row(tm, half), _row(tm, half), _fixed((1, half)), _fixed((1, half)), VMEM_WHOLE],
        out_specs=[_row(tm, d), _row(tm, 2 * half)],
        out_shape=[jax.ShapeDtypeStruct((n, d), F32), jax.ShapeDtypeStruct((n, 2 * half), BF16)],
        compiler_params=_params(("parallel",)),
    )(h, ya, yl, gao, glo, wout)


def outproj_bwd(dh, ya, yl, gao, glo, wout):
    n, d = dh.shape
    half = ya.shape[1]
    tm = _tile(n, 704)

    def body(dh_ref, ya_ref, yl_ref, gao_ref, glo_ref, w_ref, dya_ref, dyl_ref, dgao_ref, dglo_ref):
        d16 = dh_ref[...].astype(BF16)
        xa = ya_ref[...]
        xl = yl_ref[...]
        dxa, dga = _rms_bwd(xa, _rms_r(xa), gao_ref[...], _nt(d16, w_ref[:half, :]))
        dxl, dgl = _rms_bwd(xl, _rms_r(xl), glo_ref[...], _nt(d16, w_ref[half:, :]))
        dya_ref[...] = dxa
        dyl_ref[...] = dxl
        first = pl.program_id(0) == 0
        _accumulate(dgao_ref, dga, first)
        _accumulate(dglo_ref, dgl, first)

    return pl.pallas_call(
        body, name="outproj_bwd", grid=(n // tm,),
        in_specs=[_row(tm, d), _row(tm, half), _row(tm, half), _fixed((1, half)), _fixed((1, half)), VMEM_WHOLE],
        out_specs=[_row(tm, half), _row(tm, half), _fixed((1, half)), _fixed((1, half))],
        out_shape=[jax.ShapeDtypeStruct((n, half), F32), jax.ShapeDtypeStruct((n, half), F32),
                   jax.ShapeDtypeStruct((1, half), F32), jax.ShapeDtypeStruct((1, half), F32)],
        compiler_params=_params(("arbitrary",)),
    )(dh, ya, yl, gao, glo, wout)


def _loss_and_grad(x, gv, tgt, first_row):
    tm, d = x.shape
    r = _rms_r(x)
    row = first_row + lax.broadcasted_iota(jnp.int32, (tm, d), 0)
    diff = jnp.where(row >= FIRST_FRAME, x * r * gv - tgt, 0.0)
    part = 0.5 * jnp.sum(jnp.sum(diff * diff, axis=-1, keepdims=True) * (1.0 / d), axis=0, keepdims=True)
    dx, dg = _rms_bwd(x, r, gv, diff * (1.0 / d))
    return dx, part, dg


def assemble_cols(g, name):
    _, k, ns = g.shape

    def body(g_ref, o_ref):
        for j in range(N_DEV):
            o_ref[:, j * ns:(j + 1) * ns] = g_ref[j]

    return pl.pallas_call(body, name=name, out_shape=jax.ShapeDtypeStruct((k, N_DEV * ns), g.dtype),
                          compiler_params=_params(None))(g)


def split_cols(x, name):
    k, cols = x.shape
    ns = cols // N_DEV

    def body(x_ref, o_ref):
        for j in range(N_DEV):
            o_ref[j] = x_ref[:, j * ns:(j + 1) * ns]

    return pl.pallas_call(body, name=name, out_shape=jax.ShapeDtypeStruct((N_DEV, k, ns), x.dtype),
                          compiler_params=_params(None))(x)


def _slab_rows(w, per_head):
    k = w.shape[1]
    w = w.reshape(MLA_HEADS, per_head, k)
    return jnp.pad(w, ((0, 0), (0, HEAD_SLAB - per_head), (0, 0))).reshape(MLA_HEADS * HEAD_SLAB, k)


def _unslab_rows(w, per_head):
    k = w.shape[1]
    return w.reshape(MLA_HEADS, HEAD_SLAB, k)[:, :per_head].reshape(MLA_HEADS * per_head, k)


def meta_grad(dh0, nb, lp):
    d = dh0.shape[1]
    ns = d // N_DEV
    per_seq = lp // N_META

    def body(x_ref, o_ref):
        x = x_ref[...]
        for j in range(N_DEV):
            _accumulate(o_ref.at[j], x[:, j * ns:(j + 1) * ns], pl.program_id(0) == 0)

    return pl.pallas_call(
        body, name="meta_grad", grid=(nb,),
        in_specs=[pl.BlockSpec((N_META, d), lambda b: (b * per_seq + PAD // N_META, 0))],
        out_specs=pl.BlockSpec((N_DEV, N_META, ns), lambda b: (0, 0, 0)),
        out_shape=jax.ShapeDtypeStruct((N_DEV, N_META, ns), F32),
        compiler_params=_params(("arbitrary",)))(dh0)


VECTORS = [("ffn1_norm", 1024), ("mix_norm", 1024), ("q_latent_norm", 384), ("kv_latent_norm", 256),
           ("q_head_norm", 192), ("k_head_norm", 192), ("conv_b", 512), ("gate_a_b", 512), ("gate_x_b", 512),
           ("lru_lambda", 512), ("attn_out_norm", 512), ("lru_out_norm", 512), ("ffn2_norm", 1024),
           ("final_norm", 1024)]
VEC_ROWS = 16
LOSS_ROW = len(VECTORS)
GATES = ["gate_a_w", "gate_x_w"]


def pack_vectors(grads, loss):
    def body(*refs):
        o_ref = refs[-1]
        o_ref[...] = jnp.zeros_like(o_ref)
        for t, (ref, (_, cnt)) in enumerate(zip(refs[:-2], VECTORS)):
            o_ref[t:t + 1, :cnt] = ref[:, :cnt]
        o_ref[LOSS_ROW:LOSS_ROW + 1, :LANES] = refs[-2][...]

    return pl.pallas_call(body, name="pack_vectors", out_shape=jax.ShapeDtypeStruct((VEC_ROWS, D_MODEL), F32),
                          compiler_params=_params(None))(*[grads[name] for name, _ in VECTORS], loss)


def _adamw_update(w, g, m, v):
    c1 = 1.0 / (1.0 - ADAM_B1 ** ADAM_STEP)
    c2 = 1.0 / (1.0 - ADAM_B2 ** ADAM_STEP)
    mn = ADAM_B1 * m + (1.0 - ADAM_B1) * g
    vn = ADAM_B2 * v + (1.0 - ADAM_B2) * (g * g)
    delta = -ADAM_LR * ((mn * c1) / (jnp.sqrt(vn * c2) + ADAM_EPS) + ADAM_WD * w)
    return delta, mn, vn


def _sum_slots(ref, index=()):
    acc = ref[(0,) + index].astype(F32)
    for s in range(1, N_DEV):
        acc = acc + ref[(s,) + index].astype(F32)
    return acc


def adamw_sharded(r, w, m, v, name):
    rows, cols = w.shape
    tr = _tile(rows, 256, 16) if rows % 16 == 0 else rows

    def body(r_ref, w_ref, m_ref, v_ref, g_ref, d_ref, mo_ref, vo_ref):
        g = _sum_slots(r_ref)
        g_ref[...] = g
        d_ref[...], mo_ref[...], vo_ref[...] = _adamw_update(w_ref[...], g, m_ref[...], v_ref[...])

    spec = pl.BlockSpec((tr, cols), lambda i: (i, 0))
    shape = jax.ShapeDtypeStruct((rows, cols), F32)
    return pl.pallas_call(
        body, name=name, grid=(rows // tr,),
        in_specs=[pl.BlockSpec((N_DEV, tr, cols), lambda i: (0, i, 0))] + [spec] * 3,
        out_specs=[spec] * 4, out_shape=[shape] * 4,
        compiler_params=_params(("parallel",)),
    )(r, w, m, v)


def adamw_small(r_vec, r_gates, w, m, v):
    nt = len(VECTORS) + len(GATES)

    def body(*refs):
        rv_ref = refs[0]
        rg_refs = refs[1:1 + len(GATES)]
        base = 1 + len(GATES)
        w_refs, m_refs, v_refs = (refs[base + i * nt:base + (i + 1) * nt] for i in range(3))
        outs = refs[base + 3 * nt:]
        g_o, d_o, m_o, v_o = (outs[i * nt:(i + 1) * nt] for i in range(4))
        outs[4 * nt][...] = _sum_slots(rv_ref, (slice(LOSS_ROW, LOSS_ROW + 1), slice(0, LANES)))
        for t in range(nt):
            if t < len(VECTORS):
                cnt = VECTORS[t][1]
                g = _sum_slots(rv_ref, (slice(t, t + 1), slice(0, cnt)))
            else:
                g = _sum_slots(rg_refs[t - len(VECTORS)])
            g_o[t][...] = g
            d_o[t][...], m_o[t][...], v_o[t][...] = _adamw_update(w_refs[t][...], g, m_refs[t][...], v_refs[t][...])

    shapes = [jax.ShapeDtypeStruct(a.shape, F32) for a in w]
    res = pl.pallas_call(body, name="adamw_small", out_shape=shapes * 4 + [jax.ShapeDtypeStruct((1, LANES), F32)],
                         compiler_params=_params(None))(r_vec, *r_gates, *w, *m, *v)
    return [res[i * nt:(i + 1) * nt] for i in range(4)], res[4 * nt]


def _block_diag(w):
    nb, n, _ = w.shape
    eye = jnp.eye(nb, dtype=w.dtype)
    return (eye[:, None, :, None] * w[:, :, None, :]).reshape(nb * n, nb * n)


def _two_d(a):
    if a.ndim == 3:
        return a.reshape(a.shape[1], a.shape[2])
    if a.ndim == 4:
        return a.reshape(a.shape[1] * a.shape[2], a.shape[3])
    return a


_WEIGHT_NAMES = ['meta_tokens', 'ffn1_norm', 'ffn1_w_gate', 'ffn1_w_up', 'ffn1_w_down', 'mix_norm', 'w_in',
                 'q_latent_norm', 'w_uq', 'kv_latent_norm', 'w_uk', 'w_uv', 'q_head_norm', 'k_head_norm', 'conv_w',
                 'conv_b', 'gate_a_w', 'gate_a_b', 'gate_x_w', 'gate_x_b', 'lru_lambda', 'attn_out_norm',
                 'lru_out_norm', 'w_out', 'ffn2_norm', 'ffn2_w_gate', 'ffn2_w_up', 'ffn2_w_down', 'final_norm']


COLUMN_SHARDED = ("ffn1_w_gate", "ffn1_w_up", "ffn2_w_gate", "ffn2_w_up", "w_in", "w_uq", "w_uk", "w_uv")


def train_step(x, tgt, w, m, v):
    nb, seq, d = x.shape
    lp = PAD + N_META + seq
    n = nb * lp
    def local(a, name):
        a = _two_d(a)
        return a.T if name in COLUMN_SHARDED else a

    sh = {name: local(w[name], name) for name in _WEIGHT_NAMES}
    m2 = {name: local(m[name], name) for name in _WEIGHT_NAMES}
    v2 = {name: local(v[name], name) for name in _WEIGHT_NAMES}

    def b16(name):
        return sh[name].astype(BF16)

    out = {}

    def update(name, landed):
        out[name] = adamw_sharded(landed, sh[name], m2[name], v2[name], "adamw_" + name)

    g_wg1, g_wu1, g_wd1, g_meta, g_conv = exchange(
        [b16("ffn1_w_gate"), b16("ffn1_w_up"), b16("ffn1_w_down"), sh["meta_tokens"], sh["conv_w"]],
        ["gather"] * 5, "gather_ffn1")
    wg1, wu1, wd1 = (g.reshape(D_FF, d) for g in (g_wg1, g_wu1, g_wd1))
    meta = assemble_cols(g_meta, "assemble_meta")
    conv_w = assemble_cols(g_conv, "assemble_conv")

    front = jnp.concatenate([jnp.zeros((PAD, d), F32), meta], axis=0)
    h0 = jnp.concatenate([jnp.broadcast_to(front[None], (nb, FIRST_FRAME, d)), x], axis=1).reshape(n, d)
    tgt_p = jnp.concatenate([jnp.zeros((nb, FIRST_FRAME, d), F32), tgt], axis=1).reshape(n, d)
    tables = _rope_tables(lp)
    zero_tail = jnp.zeros((1, HEAD_SLAB - D_QK), F32)
    gqh = jnp.concatenate([sh["q_head_norm"], zero_tail], axis=1)
    gkh = jnp.concatenate([sh["k_head_norm"], zero_tail], axis=1)
    wa = _block_diag(w["gate_a_w"][0]).astype(BF16)
    wx = _block_diag(w["gate_x_w"][0]).astype(BF16)

    (h1, u1, a1, b1), (g_in, g_uq, g_uk, g_uv, g_out, g_wg2) = ffn_fwd(
        h0, sh["ffn1_norm"], wg1, wu1, wd1, "ffn1_fwd",
        comm=([b16("w_in"), b16("w_uq"), b16("w_uk"), b16("w_uv"), b16("w_out"), b16("ffn2_w_gate")], ["gather"] * 6))
    mla_rows = MLA_IN - D_ROPE
    w_in = g_in.reshape(mla_rows + 2 * LRU_WIDTH, d)
    wm = jnp.concatenate([w_in[:mla_rows], jnp.zeros((D_ROPE, d), BF16)], axis=0)
    wl = w_in[mla_rows:]
    wuq = _slab_rows(g_uq.reshape(MLA_HEADS * D_QK, Q_RANK), D_QK)
    wuk = _slab_rows(g_uk.reshape(MLA_HEADS * D_NOPE, KV_RANK), D_NOPE)
    wuv = g_uv.reshape(MLA_HEADS * D_V, KV_RANK)
    w_out = g_out.reshape(d, d)

    u2, zm, zl = inproj_fwd(h1, sh["mix_norm"], wm, wl)
    q, k, vv, qn, cn = mla_prep_fwd(zm, sh["q_latent_norm"], sh["kv_latent_norm"], wuq, wuk, wuv, gqh, gkh, tables, lp)
    (y_mla, lse), (g_wu2, g_wd2) = attn_fwd(
        q, k, vv, nb, lp, comm=([b16("ffn2_w_up"), b16("ffn2_w_down")], ["gather"] * 2))
    wg2, wu2, wd2 = (g.reshape(D_FF, d) for g in (g_wg2, g_wu2, g_wd2))
    y_lru, hs = lru_fwd(zl, conv_w, sh["conv_b"], wa, wx, sh["gate_a_b"], sh["gate_x_b"], sh["lru_lambda"], nb, lp)
    h2, yn = outproj_fwd(h1, y_mla, y_lru, sh["attn_out_norm"], sh["lru_out_norm"], w_out)
    dh3, u3, a3, b3, loss, g_final = ffn_fwd_loss(h2, sh["ffn2_norm"], wg2, wu2, wd2, sh["final_norm"], tgt_p, lp,
                                                  "ffn2_fwd_loss")

    vec = {"final_norm": g_final}
    (dh2, da3, db3, sh3, vec["ffn2_norm"]), _ = ffn_bwd_act(dh3, h2, sh["ffn2_norm"], a3, b3, wg2, wu2, wd2, "ffn2_bwd")
    ff_shards = (N_DEV, D_FF // N_DEV, d)
    dwg2 = tn_matmul(da3, u3, "ffn2_dwg", "bf16").reshape(ff_shards)
    dwu2 = tn_matmul(db3, u3, "ffn2_dwu", "bf16").reshape(ff_shards)
    dwd2 = tn_matmul(sh3, dh3, "ffn2_dwd", "bf16").reshape(ff_shards)

    dy_mla, dy_lru, vec["attn_out_norm"], vec["lru_out_norm"] = outproj_bwd(
        dh2, y_mla, y_lru, sh["attn_out_norm"], sh["lru_out_norm"], w_out)
    dw_out = tn_matmul(yn, dh2, "dw_out", "bf16").reshape(N_DEV, d // N_DEV, d)
    (du, dgate, dconv, vec["conv_b"], vec["gate_a_b"], vec["gate_x_b"], vec["lru_lambda"], dga, dgx), (r_wg2,) = lru_bwd(
        zl, hs, dy_lru, conv_w, sh["conv_b"], wa, wx, sh["gate_a_b"], sh["gate_x_b"], sh["lru_lambda"], nb, lp,
        comm=([dwg2], ["scatter"]))
    update("ffn2_w_gate", r_wg2)

    (dq, dk, dv), (r_wu2,) = attn_bwd(q, k, vv, y_mla, dy_mla, lse, nb, lp, comm=([dwu2], ["scatter"]))
    update("ffn2_w_up", r_wu2)

    (dzm, dqr, dkr, vec["q_latent_norm"], vec["kv_latent_norm"], vec["q_head_norm"], vec["k_head_norm"]), (r_wd2,) = (
        mla_prep_bwd(dq, dk, dv, zm, qn, cn, sh["q_latent_norm"], sh["kv_latent_norm"], wuq, wuk, wuv, gqh, gkh,
                     tables, lp, comm=([dwd2], ["scatter"])))
    update("ffn2_w_down", r_wd2)
    dwuq = _unslab_rows(tn_matmul(dqr, qn, "dw_uq"), D_QK).reshape(N_DEV, -1, Q_RANK)
    dwuk = _unslab_rows(tn_matmul(dkr, cn, "dw_uk"), D_NOPE).reshape(N_DEV, -1, KV_RANK)
    dwuv = tn_matmul(dv, cn, "dw_uv").reshape(N_DEV, -1, KV_RANK)
    dh1, vec["mix_norm"] = inproj_bwd(dzm, du, dgate, dh2, h1, sh["mix_norm"], wm, wl)
    dw_in = jnp.concatenate([tn_matmul(dzm, u2, "dw_in_mla")[:mla_rows], tn_matmul(du, u2, "dw_in_u"),
                             tn_matmul(dgate, u2, "dw_in_gate")], axis=0).reshape(N_DEV, -1, d)

    (dh0, da1, db1, sh1, vec["ffn1_norm"]), landed = ffn_bwd_act(
        dh1, h0, sh["ffn1_norm"], a1, b1, wg1, wu1, wd1, "ffn1_bwd",
        comm=([dw_in, dwuq, dwuk, dwuv, dw_out, split_cols(dconv, "split_conv")], ["scatter"] * 6))
    for name, r in zip(("w_in", "w_uq", "w_uk", "w_uv", "w_out", "conv_w"), landed):
        update(name, r)

    dwg1 = tn_matmul(da1, u1, "ffn1_dwg", "bf16").reshape(ff_shards)
    dwu1, (r_wg1,) = tn_matmul(db1, u1, "ffn1_dwu", "bf16", comm=([dwg1], ["scatter"]))
    dwd1, (r_wu1,) = tn_matmul(sh1, dh1, "ffn1_dwd", "bf16", comm=([dwu1.reshape(ff_shards)], ["scatter"]))
    dwd1 = dwd1.reshape(ff_shards)
    dmeta = meta_grad(dh0, nb, lp)
    gates = [dga.reshape(LRU_WIDTH, LRU_BLOCK), dgx.reshape(LRU_WIDTH, LRU_BLOCK)]
    r_wd1, r_meta, r_vec, r_ga, r_gx = exchange(
        [dwd1, dmeta, pack_vectors(vec, loss)] + gates, ["scatter"] * 2 + ["gather"] * 3, "exchange_last")
    update("ffn1_w_gate", r_wg1)
    update("ffn1_w_up", r_wu1)
    update("ffn1_w_down", r_wd1)
    update("meta_tokens", r_meta)

    small = [name for name, _ in VECTORS] + GATES
    res, total_loss = adamw_small(r_vec, [r_ga, r_gx], [sh[nm] for nm in small], [m2[nm] for nm in small],
                                  [v2[nm] for nm in small])
    for i, name in enumerate(small):
        out[name] = [res[j][i] for j in range(4)]

    grad_x = dh0.reshape(nb, lp, d)[:, FIRST_FRAME:]
    loss = total_loss[0, 0]

    def as_given(a, name):
        return (a.T if name in COLUMN_SHARDED else a).reshape(w[name].shape)

    cols = [[as_given(out[name][j], name) for name in _WEIGHT_NAMES] for j in range(4)]
    return (loss, grad_x, *cols[0], *cols[1], *cols[2], *cols[3])


def kernel(x, meta_tokens, ffn1_norm, ffn1_w_gate, ffn1_w_up, ffn1_w_down, mix_norm, w_in, q_latent_norm, w_uq, kv_latent_norm, w_uk, w_uv, q_head_norm, k_head_norm, conv_w, conv_b, gate_a_w, gate_a_b, gate_x_w, gate_x_b, lru_lambda, attn_out_norm, lru_out_norm, w_out, ffn2_norm, ffn2_w_gate, ffn2_w_up, ffn2_w_down, final_norm, loss_target, m_meta_tokens, m_ffn1_norm, m_ffn1_w_gate, m_ffn1_w_up, m_ffn1_w_down, m_mix_norm, m_w_in, m_q_latent_norm, m_w_uq, m_kv_latent_norm, m_w_uk, m_w_uv, m_q_head_norm, m_k_head_norm, m_conv_w, m_conv_b, m_gate_a_w, m_gate_a_b, m_gate_x_w, m_gate_x_b, m_lru_lambda, m_attn_out_norm, m_lru_out_norm, m_w_out, m_ffn2_norm, m_ffn2_w_gate, m_ffn2_w_up, m_ffn2_w_down, m_final_norm, v_meta_tokens, v_ffn1_norm, v_ffn1_w_gate, v_ffn1_w_up, v_ffn1_w_down, v_mix_norm, v_w_in, v_q_latent_norm, v_w_uq, v_kv_latent_norm, v_w_uk, v_w_uv, v_q_head_norm, v_k_head_norm, v_conv_w, v_conv_b, v_gate_a_w, v_gate_a_b, v_gate_x_w, v_gate_x_b, v_lru_lambda, v_attn_out_norm, v_lru_out_norm, v_w_out, v_ffn2_norm, v_ffn2_w_gate, v_ffn2_w_up, v_ffn2_w_down, v_final_norm):
    args = locals()
    w = {name: args[name] for name in _WEIGHT_NAMES}
    m = {name: args["m_" + name] for name in _WEIGHT_NAMES}
    v = {name: args["v_" + name] for name in _WEIGHT_NAMES}
    return train_step(x, loss_target, w, m, v)
```

```python
import math

import jax
import jax.numpy as jnp
from jax import lax
from jax.experimental import pallas as pl
from jax.experimental.pallas import tpu as pltpu

F32 = jnp.float32
BF16 = jnp.bfloat16

D_MODEL = 1024
CHUNK = 64
CHUNK_SHIFT = 6
N_META = 16
PAD = CHUNK - N_META
FIRST_FRAME = PAD + N_META
MLA_HEADS = 4
D_NOPE = 128
D_ROPE = 64
D_QK = D_NOPE + D_ROPE
D_V = 128
HEAD_SLAB = 256
KV_RANK = 256
Q_RANK = 384
ROPE_THETA = 10000.0
LRU_WIDTH = 512
LRU_BLOCKS = 8
LRU_BLOCK = 64
LRU_TILE = 128
CONV_W = 4
C_RGLRU = 8.0
D_FF = 2816
MLA_IN = 768
EPS = 1e-6
NEG_INF = -1e30
N_DEV = 8
LANES = 128
VMEM_LIMIT = 52 * 1024 * 1024
TN_ROWS = 4224
TN_X_BYTES = 12 * 1024 * 1024
TN_Y_BYTES = 9 * 1024 * 1024 // 2

ADAM_LR = 0.001
ADAM_B1 = 0.9
ADAM_B2 = 0.999
ADAM_EPS = 1e-08
ADAM_WD = 0.01
ADAM_STEP = 10

VMEM_WHOLE = pl.BlockSpec(memory_space=pltpu.VMEM)
HBM_WHOLE = pl.BlockSpec(memory_space=pl.ANY)


def _params(sems):
    if sems is None:
        return pltpu.CompilerParams(vmem_limit_bytes=VMEM_LIMIT)
    return pltpu.CompilerParams(dimension_semantics=sems, vmem_limit_bytes=VMEM_LIMIT)


def _tile(n, cap, mult=16):
    best = None
    for t in range(mult, min(n, cap) + 1, mult):
        if n % t == 0:
            best = t
    assert best is not None, (n, cap, mult)
    return best


def _row(tm, d):
    return pl.BlockSpec((tm, d), lambda i: (i, 0))


def _fixed(shape):
    return pl.BlockSpec(shape, lambda i: (0,) * len(shape))


def _mesh_position():
    return lax.axis_index("x"), lax.axis_index("y"), lax.axis_index("c")


def _flat_index(x, y, c):
    return 4 * x + 2 * y + c


def _peers(x, y, c):
    out = []
    for k in range(1, N_DEV):
        fx, fy, fc = (k >> 2) & 1, (k >> 1) & 1, k & 1
        out.append((1 - x if fx else x, 1 - y if fy else y, 1 - c if fc else c))
    return out


def _comm_out_shapes(srcs, modes):
    return [jax.ShapeDtypeStruct((N_DEV,) + s.shape if md == "gather" else s.shape, s.dtype)
            for s, md in zip(srcs, modes)]


def _comm_scratch(n):
    per_peer = n * (N_DEV - 1)
    return [pltpu.SemaphoreType.DMA((per_peer,)), pltpu.SemaphoreType.DMA((per_peer,)), pltpu.SemaphoreType.DMA((n,))]


class _Copies:
    def __init__(self, own, first, relay):
        self.own, self.first, self.relay = own, first, relay

    def start(self):
        for cp in self.own + self.first:
            cp.start()

    def forward(self):
        for arrival, onward in self.relay:
            arrival.wait_recv()
            onward.start()

    def finish(self):
        arrivals = [a for a, _ in self.relay]
        onward = [f for _, f in self.relay]
        for cp in self.first + onward:
            if not any(cp is a for a in arrivals):
                cp.wait_recv()
        for cp in self.first + onward:
            cp.wait_send()
        for cp in self.own:
            cp.wait()


def _comm_copies(src_refs, dst_refs, modes, send, recv, local):
    x, y, c = _mesh_position()
    me = _flat_index(x, y, c)
    n = len(modes)
    sibling = (x, y, 1 - c)
    chips = [(1 - x, y), (x, 1 - y), (1 - x, 1 - y)]

    def remote(src, dst, k, t, to):
        return pltpu.make_async_remote_copy(src_ref=src, dst_ref=dst, send_sem=send.at[k * n + t],
                                            recv_sem=recv.at[k * n + t], device_id=to,
                                            device_id_type=pl.DeviceIdType.MESH)

    own, first, relay = [], [], []
    for t, (src, dst, md) in enumerate(zip(src_refs, dst_refs, modes)):
        if md == "scatter":
            own.append(pltpu.make_async_copy(src.at[me], dst.at[me], local.at[t]))
            for k, peer in enumerate(_peers(x, y, c)):
                first.append(remote(src.at[_flat_index(*peer)], dst.at[me], k, t, peer))
        else:
            own.append(pltpu.make_async_copy(src, dst.at[me], local.at[t]))
            first.append(remote(src, dst.at[me], 0, t, sibling))
            for j, chip in enumerate(chips):
                arrival = remote(src, dst.at[me], 1 + j, t, (*chip, c))
                landed = dst.at[_flat_index(*chip, c)]
                first.append(arrival)
                relay.append((arrival, remote(landed, landed, 4 + j, t, sibling)))
    return _Copies(own, first, relay)


def _hosted(body, n_in, n_out, modes, grid):
    t = len(modes)
    total = math.prod(grid)

    def wrapped(*refs):
        ins, csrc = refs[:n_in], refs[n_in:n_in + t]
        outs = refs[n_in + t:n_in + t + n_out]
        cdst = refs[n_in + t + n_out:n_in + 2 * t + n_out]
        scratch = refs[n_in + 2 * t + n_out:-3]
        copies = _comm_copies(csrc, cdst, modes, *refs[-3:])
        step = pl.program_id(0)
        for axis in range(1, len(grid)):
            step = step * grid[axis] + pl.program_id(axis)

        @pl.when(step == 0)
        def _():
            copies.start()

        body(*ins, *outs, *scratch)

        @pl.when(step == (total * 3) // 5)
        def _():
            copies.forward()

        @pl.when(step == total - 1)
        def _():
            copies.finish()

    return wrapped


def _call(body, name, grid, in_specs, out_specs, out_shape, sems, args, scratch=(), comm=None):
    if comm is None:
        outs = pl.pallas_call(body, name=name, grid=grid, in_specs=in_specs, out_specs=out_specs, out_shape=out_shape,
                              scratch_shapes=list(scratch), compiler_params=_params(sems))(*args)
        return outs, []
    srcs, modes = comm
    n = len(modes)
    res = pl.pallas_call(
        _hosted(body, len(in_specs), len(out_specs), modes, grid), name=name, grid=grid,
        in_specs=list(in_specs) + [HBM_WHOLE] * n, out_specs=list(out_specs) + [HBM_WHOLE] * n,
        out_shape=list(out_shape) + _comm_out_shapes(srcs, modes),
        scratch_shapes=list(scratch) + _comm_scratch(n),
        compiler_params=_params(("arbitrary",) * len(grid)))(*args, *srcs)
    return res[:len(out_specs)], res[len(out_specs):]


def exchange(srcs, modes, name):
    n = len(modes)

    def body(*refs):
        copies = _comm_copies(refs[:n], refs[n:2 * n], modes, *refs[2 * n:])
        copies.start()
        copies.forward()
        copies.finish()

    return pl.pallas_call(body, name=name, in_specs=[HBM_WHOLE] * n, out_specs=[HBM_WHOLE] * n,
                          out_shape=_comm_out_shapes(srcs, modes), scratch_shapes=_comm_scratch(n))(*srcs)


def _nn(a, b):
    return jnp.dot(a, b, preferred_element_type=F32)


def _nt(a, b):
    return lax.dot_general(a, b, (((1,), (1,)), ((), ())), preferred_element_type=F32)


def _tn(a, b):
    return lax.dot_general(a, b, (((0,), (0,)), ((), ())), preferred_element_type=F32)


def _sig(x):
    return 1.0 / (1.0 + jnp.exp(-x))


def _rms_r(x, n=None):
    n = x.shape[-1] if n is None else n
    return lax.rsqrt(jnp.sum(x * x, axis=-1, keepdims=True) * (1.0 / n) + EPS)


def _rms_bwd(x, r, g, dy, n=None):
    n = x.shape[-1] if n is None else n
    xhat = x * r
    dxhat = dy * g
    dx = r * (dxhat - xhat * (jnp.sum(dxhat * xhat, axis=-1, keepdims=True) * (1.0 / n)))
    return dx, jnp.sum(dy * xhat, axis=0, keepdims=True)


def _accumulate(ref, val, first):
    @pl.when(first)
    def _():
        ref[...] = val

    @pl.when(jnp.logical_not(first))
    def _():
        ref[...] += val


_GELU_C = math.sqrt(2.0 / math.pi)


def _gelu_and_grad(x):
    inner = _GELU_C * (x + 0.044715 * x * x * x)
    t = jnp.tanh(inner)
    gelu = 0.5 * x * (1.0 + t)
    dgelu = 0.5 * (1.0 + t) + 0.5 * x * (1.0 - t * t) * _GELU_C * (1.0 + 3.0 * 0.044715 * x * x)
    return gelu, dgelu


def _log1p_small(t):
    return jnp.where(t < 1e-3, t * (1.0 - t * (0.5 - t * (1.0 / 3.0))), jnp.log(1.0 + t))


def _softplus(x):
    return jnp.maximum(x, 0.0) + _log1p_small(jnp.exp(-jnp.abs(x)))


def _sig_tanh(x):
    return 0.5 + 0.5 * jnp.tanh(0.5 * x)


def _ff_chunks(f):
    return 2 if (f // 2) % LANES == 0 else 1


def _swiglu_half(x, g_ref, wg_ref, wu_ref, wd_ref, a_ref, b_ref, fc):
    f = wg_ref.shape[0]
    u = (x * _rms_r(x) * g_ref[...]).astype(BF16)
    acc = jnp.zeros(x.shape, F32)
    for c in range(f // fc):
        cols = slice(c * fc, (c + 1) * fc)
        a = _nt(u, wg_ref[cols, :])
        b = _nt(u, wu_ref[cols, :])
        s = (a * _sig(a) * b).astype(BF16)
        acc = acc + _nn(s, wd_ref[cols, :])
        a_ref[:, cols] = a.astype(BF16)
        b_ref[:, cols] = b.astype(BF16)
    return x + 0.5 * acc, u


def ffn_up(h, g, wg, wu, name, comm=None):
    n, d = h.shape
    f = wg.shape[0]
    tm = _tile(n, 528)
    fc = 2 * LANES if f % (2 * LANES) == 0 else f

    def body(h_ref, g_ref, wg_ref, wu_ref, u_ref, a_ref, b_ref, s_ref):
        x = h_ref[...]
        u = (x * _rms_r(x) * g_ref[...]).astype(BF16)
        u_ref[...] = u
        for c in range(f // fc):
            cols = slice(c * fc, (c + 1) * fc)
            a = _nt(u, wg_ref[cols, :])
            b = _nt(u, wu_ref[cols, :])
            a_ref[:, cols] = a.astype(BF16)
            b_ref[:, cols] = b.astype(BF16)
            s_ref[:, cols] = (a * _sig(a) * b).astype(BF16)

    wide = jax.ShapeDtypeStruct((n, f), BF16)
    return _call(
        body, name, (n // tm,),
        [_row(tm, d), _fixed((1, d)), VMEM_WHOLE, VMEM_WHOLE],
        [_row(tm, d), _row(tm, f), _row(tm, f), _row(tm, f)],
        [jax.ShapeDtypeStruct((n, d), BF16), wide, wide, wide],
        ("parallel",), (h, g, wg, wu), comm=comm)


def ffn_down(h, s, wd, name):
    n, d = h.shape
    f = wd.shape[0]
    tm = _tile(n, 528)

    def body(h_ref, s_ref, wd_ref, ho_ref):
        ho_ref[...] = h_ref[...] + 0.5 * _nn(s_ref[...], wd_ref[...])

    return pl.pallas_call(
        body, name=name, grid=(n // tm,),
        in_specs=[_row(tm, d), _row(tm, f), VMEM_WHOLE], out_specs=_row(tm, d),
        out_shape=jax.ShapeDtypeStruct((n, d), F32), compiler_params=_params(("parallel",)))(h, s, wd)


def ffn_fwd_loss(h, g, wg, wu, wd, g_final, tgt, lp, name):
    n, d = h.shape
    f = wg.shape[0]
    tm = _tile(lp, 528)
    per_seq = lp // tm
    fc = 2 * LANES if f % (2 * LANES) == 0 else f

    def body(h_ref, g_ref, wg_ref, wu_ref, wd_ref, gf_ref, t_ref, dh_ref, u_ref, a_ref, b_ref, loss_ref, dgf_ref):
        i = pl.program_id(0)
        y, u_ref[...] = _swiglu_half(h_ref[...], g_ref, wg_ref, wu_ref, wd_ref, a_ref, b_ref, fc)
        dh_ref[...], part, dg = _loss_and_grad(y, gf_ref[...], t_ref[...], (i % per_seq) * tm)
        _accumulate(loss_ref, jnp.broadcast_to(part, (1, LANES)), i == 0)
        _accumulate(dgf_ref, dg, i == 0)

    outs, _ = _call(
        body, name, (n // tm,),
        [_row(tm, d), _fixed((1, d)), VMEM_WHOLE, VMEM_WHOLE, VMEM_WHOLE, _fixed((1, d)), _row(tm, d)],
        [_row(tm, d), _row(tm, d), _row(tm, f), _row(tm, f), _fixed((1, LANES)), _fixed((1, d))],
        [jax.ShapeDtypeStruct((n, d), F32), jax.ShapeDtypeStruct((n, d), BF16),
         jax.ShapeDtypeStruct((n, f), BF16), jax.ShapeDtypeStruct((n, f), BF16),
         jax.ShapeDtypeStruct((1, LANES), F32), jax.ShapeDtypeStruct((1, d), F32)],
        ("arbitrary",), (h, g, wg, wu, wd, g_final, tgt))
    return outs


def ffn_bwd_act(dh, h, g, a, b, wg, wu, wd, name, comm=None):
    n, d = h.shape
    f = wg.shape[0]
    tm = _tile(n, 192)
    nc = _ff_chunks(f)
    fc = f // nc

    def body(dh_ref, h_ref, g_ref, a_ref, b_ref, wg_ref, wu_ref, wd_ref,
             dhi_ref, da_ref, db_ref, sh_ref, dg_ref):
        x = h_ref[...]
        dy = dh_ref[...]
        r = _rms_r(x)
        dhh = (0.5 * dy).astype(BF16)
        du = jnp.zeros((tm, d), F32)
        for c in range(nc):
            cols = slice(c * fc, (c + 1) * fc)
            ds = _nt(dhh, wd_ref[cols, :])
            av = a_ref[:, cols].astype(F32)
            bv = b_ref[:, cols].astype(F32)
            sg = _sig(av)
            sil = av * sg
            da = (ds * bv * (sg * (1.0 + av * (1.0 - sg)))).astype(BF16)
            db = (ds * sil).astype(BF16)
            da_ref[:, cols] = da
            db_ref[:, cols] = db
            sh_ref[:, cols] = (0.5 * sil * bv).astype(BF16)
            du = du + _nn(da, wg_ref[cols, :]) + _nn(db, wu_ref[cols, :])
        dx, dg = _rms_bwd(x, r, g_ref[...], du)
        dhi_ref[...] = dy + dx
        _accumulate(dg_ref, dg, pl.program_id(0) == 0)

    return _call(
        body, name, (n // tm,),
        [_row(tm, d), _row(tm, d), _fixed((1, d)), _row(tm, f), _row(tm, f), VMEM_WHOLE, VMEM_WHOLE, VMEM_WHOLE],
        [_row(tm, d), _row(tm, f), _row(tm, f), _row(tm, f), _fixed((1, d))],
        [jax.ShapeDtypeStruct((n, d), F32), jax.ShapeDtypeStruct((n, f), BF16),
         jax.ShapeDtypeStruct((n, f), BF16), jax.ShapeDtypeStruct((n, f), BF16),
         jax.ShapeDtypeStruct((1, d), F32)],
        ("arbitrary",), (dh, h, g, a, b, wg, wu, wd), comm=comm)


def tn_matmul(x, y, name, out="f32", comm=None):
    n, k = x.shape
    m = y.shape[1]
    tm = _tile(n, TN_ROWS)
    kc, mc = k, (512 if m % 512 == 0 else m)
    while tm * kc * x.dtype.itemsize > TN_X_BYTES and kc % (2 * LANES) == 0:
        kc //= 2
    while tm * mc * y.dtype.itemsize > TN_Y_BYTES and mc % (2 * LANES) == 0:
        mc //= 2
    steps = n // tm

    def body(x_ref, y_ref, o_ref, *acc):
        i = pl.program_id(2)
        part = _tn(x_ref[...].astype(BF16), y_ref[...].astype(BF16))
        if steps == 1:
            o_ref[...] = part.astype(o_ref.dtype)
        elif out == "f32":
            _accumulate(o_ref, part, i == 0)
        else:
            _accumulate(acc[0], part, i == 0)

            @pl.when(i == steps - 1)
            def _():
                o_ref[...] = acc[0][...].astype(BF16)

    out_shape = jax.ShapeDtypeStruct((k, m), F32 if out == "f32" else BF16)
    (res,), landed = _call(
        body, name, (k // kc, m // mc, steps),
        [pl.BlockSpec((tm, kc), lambda a, b, i: (i, a)), pl.BlockSpec((tm, mc), lambda a, b, i: (i, b))],
        [pl.BlockSpec((kc, mc), lambda a, b, i: (a, b))], [out_shape], ("parallel", "parallel", "arbitrary"), (x, y),
        scratch=[pltpu.VMEM((kc, mc), F32)] if (out == "bf16" and steps > 1) else [], comm=comm)
    return (res, landed) if comm is not None else res


def inproj_fwd(h, g, wm, wl):
    n, d = h.shape
    tm = _tile(n, 352)

    def body(h_ref, g_ref, wm_ref, wl_ref, u_ref, zm_ref, zl_ref):
        x = h_ref[...]
        u = (x * _rms_r(x) * g_ref[...]).astype(BF16)
        u_ref[...] = u
        zm_ref[...] = _nt(u, wm_ref[...])
        zl_ref[...] = _nt(u, wl_ref[...])

    return pl.pallas_call(
        body, name="inproj_fwd", grid=(n // tm,),
        in_specs=[_row(tm, d), _fixed((1, d)), VMEM_WHOLE, VMEM_WHOLE],
        out_specs=[_row(tm, d), _row(tm, MLA_IN), _row(tm, 2 * LRU_WIDTH)],
        out_shape=[jax.ShapeDtypeStruct((n, d), BF16), jax.ShapeDtypeStruct((n, MLA_IN), F32),
                   jax.ShapeDtypeStruct((n, 2 * LRU_WIDTH), F32)],
        compiler_params=_params(("parallel",)),
    )(h, g, wm, wl)


def inproj_bwd(dzm, du, dgate, dh2, h, g, wm, wl):
    n, d = h.shape
    tm = _tile(n, 352)

    def body(dzm_ref, du_ref, dgt_ref, dh2_ref, h_ref, g_ref, wm_ref, wl_ref, dh_ref, dg_ref):
        x = h_ref[...]
        dun = (_nn(dzm_ref[...].astype(BF16), wm_ref[...])
               + _nn(du_ref[...].astype(BF16), wl_ref[:LRU_WIDTH, :])
               + _nn(dgt_ref[...].astype(BF16), wl_ref[LRU_WIDTH:, :]))
        dx, dg = _rms_bwd(x, _rms_r(x), g_ref[...], dun)
        dh_ref[...] = dh2_ref[...] + dx
        _accumulate(dg_ref, dg, pl.program_id(0) == 0)

    return pl.pallas_call(
        body, name="inproj_bwd", grid=(n // tm,),
        in_specs=[_row(tm, MLA_IN), _row(tm, LRU_WIDTH), _row(tm, LRU_WIDTH), _row(tm, d), _row(tm, d),
                  _fixed((1, d)), VMEM_WHOLE, VMEM_WHOLE],
        out_specs=[_row(tm, d), _fixed((1, d))],
        out_shape=[jax.ShapeDtypeStruct((n, d), F32), jax.ShapeDtypeStruct((1, d), F32)],
        compiler_params=_params(("arbitrary",)),
    )(dzm, du, dgate, dh2, h, g, wm, wl)


def _rope_tables(lp):
    pos = jnp.arange(lp, dtype=F32) - float(PAD)
    half = D_ROPE // 2
    inv_freq = ROPE_THETA ** (-jnp.arange(0, half, dtype=F32) / half)
    ang = pos[:, None] * inv_freq[None, :]
    cos, sin = jnp.cos(ang), jnp.sin(ang)
    one = jnp.ones((lp, D_NOPE), F32)
    z_nope = jnp.zeros((lp, D_NOPE), F32)
    z_half = jnp.zeros((lp, half), F32)
    z_tail = jnp.zeros((lp, HEAD_SLAB - D_QK), F32)
    cosr = jnp.concatenate([one, cos, cos, z_tail], axis=1)
    sin_up = jnp.concatenate([z_nope, z_half, sin, z_tail], axis=1)
    sin_dn = jnp.concatenate([z_nope, -sin, z_half, z_tail], axis=1)
    return cosr, sin_up, sin_dn


def _rope(x, cosr, sin_up, sin_dn):
    half = D_ROPE // 2
    return x * cosr + pltpu.roll(x, half, axis=1) * sin_up + pltpu.roll(x, HEAD_SLAB - half, axis=1) * sin_dn


def _rope_bwd(dy, cosr, sin_up, sin_dn):
    half = D_ROPE // 2
    return (dy * cosr + pltpu.roll(dy * sin_up, HEAD_SLAB - half, axis=1)
            + pltpu.roll(dy * sin_dn, half, axis=1))


def _k_rope_slab(zm_tile):
    tm = zm_tile.shape[0]
    krp = zm_tile[:, Q_RANK + KV_RANK:MLA_IN]
    return jnp.concatenate([jnp.zeros((tm, D_NOPE), F32), krp], axis=1)


def mla_prep_fwd(zm, gql, gkvl, wuq, wuk, wuv, gqh, gkh, tables, lp):
    n = zm.shape[0]
    tm = _tile(lp, 352)
    per_seq = lp // tm
    width = MLA_HEADS * HEAD_SLAB
    scale = 1.0 / math.sqrt(D_QK)

    def body(zm_ref, gql_ref, gkvl_ref, wuq_ref, wuk_ref, wuv_ref, gqh_ref, gkh_ref,
             cos_ref, up_ref, dn_ref, q_ref, k_ref, v_ref, qn_ref, cn_ref):
        z = zm_ref[...]
        cq = z[:, :Q_RANK]
        ckv = z[:, Q_RANK:Q_RANK + KV_RANK]
        qn = (cq * _rms_r(cq) * gql_ref[...]).astype(BF16)
        cn = (ckv * _rms_r(ckv) * gkvl_ref[...]).astype(BF16)
        qn_ref[...] = qn
        cn_ref[...] = cn
        q_raw = _nt(qn, wuq_ref[...])
        k_raw = _nt(cn, wuk_ref[...])
        v_ref[...] = _nt(cn, wuv_ref[...]).astype(BF16)
        kr_slab = _k_rope_slab(z)
        cosr, sin_up, sin_dn = cos_ref[...], up_ref[...], dn_ref[...]
        for hd in range(MLA_HEADS):
            cols = slice(hd * HEAD_SLAB, (hd + 1) * HEAD_SLAB)
            xq = q_raw[:, cols]
            yq = _rope(xq * _rms_r(xq, D_QK) * gqh_ref[...], cosr, sin_up, sin_dn)
            q_ref[:, cols] = (yq * scale).astype(BF16)
            xk = k_raw[:, cols] + kr_slab
            yk = _rope(xk * _rms_r(xk, D_QK) * gkh_ref[...], cosr, sin_up, sin_dn)
            k_ref[:, cols] = yk.astype(BF16)

    tab = pl.BlockSpec((tm, HEAD_SLAB), lambda i: (i % per_seq, 0))
    return pl.pallas_call(
        body, name="mla_prep_fwd", grid=(n // tm,),
        in_specs=[_row(tm, MLA_IN), _fixed((1, Q_RANK)), _fixed((1, KV_RANK)), VMEM_WHOLE, VMEM_WHOLE, VMEM_WHOLE,
                  _fixed((1, HEAD_SLAB)), _fixed((1, HEAD_SLAB)), tab, tab, tab],
        out_specs=[_row(tm, width), _row(tm, width), _row(tm, MLA_HEADS * D_V), _row(tm, Q_RANK), _row(tm, KV_RANK)],
        out_shape=[jax.ShapeDtypeStruct((n, width), BF16), jax.ShapeDtypeStruct((n, width), BF16),
                   jax.ShapeDtypeStruct((n, MLA_HEADS * D_V), BF16), jax.ShapeDtypeStruct((n, Q_RANK), BF16),
                   jax.ShapeDtypeStruct((n, KV_RANK), BF16)],
        compiler_params=_params(("parallel",)),
    )(zm, gql, gkvl, wuq, wuk, wuv, gqh, gkh, *tables)


def mla_prep_bwd(dq, dk, dv, zm, qn, cn, gql, gkvl, wuq, wuk, wuv, gqh, gkh, tables, lp, comm=None):
    n = zm.shape[0]
    tm = _tile(lp, 704)
    per_seq = lp // tm
    width = MLA_HEADS * HEAD_SLAB
    scale = 1.0 / math.sqrt(D_QK)

    def body(dq_ref, dk_ref, dv_ref, zm_ref, qn_ref, cn_ref, gql_ref, gkvl_ref, wuq_ref, wuk_ref, wuv_ref,
             gqh_ref, gkh_ref, cos_ref, up_ref, dn_ref,
             dzm_ref, dqr_ref, dkr_ref, dgql_ref, dgkvl_ref, dgqh_ref, dgkh_ref):
        z = zm_ref[...]
        cq = z[:, :Q_RANK]
        ckv = z[:, Q_RANK:Q_RANK + KV_RANK]
        q_raw = _nt(qn_ref[...], wuq_ref[...])
        k_raw = _nt(cn_ref[...], wuk_ref[...])
        kr_slab = _k_rope_slab(z)
        cosr, sin_up, sin_dn = cos_ref[...], up_ref[...], dn_ref[...]
        dgq = jnp.zeros((1, HEAD_SLAB), F32)
        dgk = jnp.zeros((1, HEAD_SLAB), F32)
        dkrp = jnp.zeros((tm, HEAD_SLAB - D_NOPE), F32)
        for hd in range(MLA_HEADS):
            cols = slice(hd * HEAD_SLAB, (hd + 1) * HEAD_SLAB)
            xq = q_raw[:, cols]
            dxn = _rope_bwd(dq_ref[:, cols] * scale, cosr, sin_up, sin_dn)
            dxq, dg = _rms_bwd(xq, _rms_r(xq, D_QK), gqh_ref[...], dxn, D_QK)
            dgq = dgq + dg
            dqr_ref[:, cols] = dxq.astype(BF16)
            xk = k_raw[:, cols] + kr_slab
            dxn = _rope_bwd(dk_ref[:, cols], cosr, sin_up, sin_dn)
            dxk, dg = _rms_bwd(xk, _rms_r(xk, D_QK), gkh_ref[...], dxn, D_QK)
            dgk = dgk + dg
            dkr_ref[:, cols] = dxk.astype(BF16)
            dkrp = dkrp + dxk[:, D_NOPE:]
        dqn = _nn(dqr_ref[...], wuq_ref[...])
        dcn = _nn(dkr_ref[...], wuk_ref[...]) + _nn(dv_ref[...].astype(BF16), wuv_ref[...])
        dcq, dg1 = _rms_bwd(cq, _rms_r(cq), gql_ref[...], dqn)
        dckv, dg2 = _rms_bwd(ckv, _rms_r(ckv), gkvl_ref[...], dcn)
        dzm_ref[:, :Q_RANK] = dcq
        dzm_ref[:, Q_RANK:Q_RANK + KV_RANK] = dckv
        dzm_ref[:, Q_RANK + KV_RANK:] = dkrp
        first = pl.program_id(0) == 0
        _accumulate(dgql_ref, dg1, first)
        _accumulate(dgkvl_ref, dg2, first)
        _accumulate(dgqh_ref, dgq, first)
        _accumulate(dgkh_ref, dgk, first)

    tab = pl.BlockSpec((tm, HEAD_SLAB), lambda i: (i % per_seq, 0))
    return _call(
        body, "mla_prep_bwd", (n // tm,),
        [_row(tm, width), _row(tm, width), _row(tm, MLA_HEADS * D_V), _row(tm, MLA_IN),
         _row(tm, Q_RANK), _row(tm, KV_RANK), _fixed((1, Q_RANK)), _fixed((1, KV_RANK)),
         VMEM_WHOLE, VMEM_WHOLE, VMEM_WHOLE, _fixed((1, HEAD_SLAB)), _fixed((1, HEAD_SLAB)), tab, tab, tab],
        [_row(tm, MLA_IN), _row(tm, width), _row(tm, width), _fixed((1, Q_RANK)), _fixed((1, KV_RANK)),
         _fixed((1, HEAD_SLAB)), _fixed((1, HEAD_SLAB))],
        [jax.ShapeDtypeStruct((n, MLA_IN), F32), jax.ShapeDtypeStruct((n, width), BF16),
         jax.ShapeDtypeStruct((n, width), BF16), jax.ShapeDtypeStruct((1, Q_RANK), F32),
         jax.ShapeDtypeStruct((1, KV_RANK), F32), jax.ShapeDtypeStruct((1, HEAD_SLAB), F32),
         jax.ShapeDtypeStruct((1, HEAD_SLAB), F32)],
        ("arbitrary",), (dq, dk, dv, zm, qn, cn, gql, gkvl, wuq, wuk, wuv, gqh, gkh, *tables), comm=comm)


def _attn_tile(lp):
    return _tile(lp, 704, CHUNK)


def _chunk_mask(i, j, t):
    qpos = i * t + lax.broadcasted_iota(jnp.int32, (t, t), 0)
    kpos = j * t + lax.broadcasted_iota(jnp.int32, (t, t), 1)
    same_or_earlier = jnp.right_shift(kpos, CHUNK_SHIFT) <= jnp.right_shift(qpos, CHUNK_SHIFT)
    return jnp.logical_and(same_or_earlier, kpos >= PAD)


def _masked_scores(s, i, j, t, diagonal):
    if diagonal:
        return jnp.where(_chunk_mask(i, j, t), s, NEG_INF)
    kpos = j * t + lax.broadcasted_iota(jnp.int32, (1, t), 1)
    return s + jnp.where(kpos < PAD, NEG_INF, 0.0)


def attn_fwd(q, k, v, nb, lp, comm=None):
    n = q.shape[0]
    t = _attn_tile(lp)
    nq = lp // t

    def body(q_ref, k_ref, v_ref, o_ref, lse_ref):
        i = pl.program_id(2)
        qv = q_ref[...]

        def kv_step(j, carry, diagonal=False):
            m, l, acc = carry
            off = pl.multiple_of(j * t, t)
            s = _masked_scores(_nt(qv, k_ref[pl.ds(off, t), :]), i, j, t, diagonal)
            m_new = jnp.maximum(m, jnp.max(s, axis=-1, keepdims=True))
            p = jnp.exp(s - m_new)
            alpha = jnp.exp(m - m_new)
            l = alpha * l + jnp.sum(p, axis=-1, keepdims=True)
            acc = alpha * acc + _nn(p.astype(BF16), v_ref[pl.ds(off, t), :])
            return m_new, l, acc

        init = (jnp.full((t, 1), NEG_INF, F32), jnp.zeros((t, 1), F32), jnp.zeros((t, D_V), F32))
        m, l, acc = kv_step(i, lax.fori_loop(0, i, kv_step, init), diagonal=True)
        o_ref[...] = acc * (1.0 / l)
        lse_ref[0] = jnp.broadcast_to(m + jnp.log(l), (t, LANES))

    return _call(
        body, "attn_fwd", (nb, MLA_HEADS, nq),
        [pl.BlockSpec((t, HEAD_SLAB), lambda b, h, i: (b * nq + i, h)),
         pl.BlockSpec((lp, HEAD_SLAB), lambda b, h, i: (b, h)),
         pl.BlockSpec((lp, D_V), lambda b, h, i: (b, h))],
        [pl.BlockSpec((t, D_V), lambda b, h, i: (b * nq + i, h)),
         pl.BlockSpec((1, t, LANES), lambda b, h, i: (h, b * nq + i, 0))],
        [jax.ShapeDtypeStruct((n, MLA_HEADS * D_V), F32), jax.ShapeDtypeStruct((MLA_HEADS, n, LANES), F32)],
        ("parallel", "parallel", "parallel"), (q, k, v), comm=comm)


def attn_bwd(q, k, v, o, do, lse, nb, lp, comm=None):
    n = q.shape[0]
    t = _attn_tile(lp)
    nq = lp // t

    def body(q_ref, k_ref, v_ref, o_ref, do_ref, lse_ref, dq_ref, dk_ref, dv_ref):
        dk_ref[...] = jnp.zeros_like(dk_ref)
        dv_ref[...] = jnp.zeros_like(dv_ref)

        def q_step(i, _):
            qoff = pl.multiple_of(i * t, t)
            qv = q_ref[pl.ds(qoff, t), :]
            dov = do_ref[pl.ds(qoff, t), :]
            delta = jnp.sum(o_ref[pl.ds(qoff, t), :] * dov, axis=-1, keepdims=True)
            lse_q = jnp.max(lse_ref[0, pl.ds(qoff, t), :], axis=-1, keepdims=True)
            do16 = dov.astype(BF16)

            def kv_step(j, dq_acc, diagonal=False):
                koff = pl.multiple_of(j * t, t)
                kv = k_ref[pl.ds(koff, t), :]
                s = _masked_scores(_nt(qv, kv), i, j, t, diagonal)
                p = jnp.exp(s - lse_q)
                dp = _nt(do16, v_ref[pl.ds(koff, t), :])
                ds16 = (p * (dp - delta)).astype(BF16)
                dv_ref[pl.ds(koff, t), :] += _tn(p.astype(BF16), do16)
                dk_ref[pl.ds(koff, t), :] += _tn(ds16, qv)
                return dq_acc + _nn(ds16, kv)

            earlier = lax.fori_loop(0, i, kv_step, jnp.zeros((t, HEAD_SLAB), F32))
            dq_ref[pl.ds(qoff, t), :] = kv_step(i, earlier, diagonal=True)
            return 0

        lax.fori_loop(0, nq, q_step, 0)

    wide = pl.BlockSpec((lp, HEAD_SLAB), lambda b, h: (b, h))
    thin = pl.BlockSpec((lp, D_V), lambda b, h: (b, h))
    width = MLA_HEADS * HEAD_SLAB
    return _call(
        body, "attn_bwd", (nb, MLA_HEADS),
        [wide, wide, thin, thin, thin, pl.BlockSpec((1, lp, LANES), lambda b, h: (h, b, 0))],
        [wide, wide, thin],
        [jax.ShapeDtypeStruct((n, width), F32), jax.ShapeDtypeStruct((n, width), F32),
         jax.ShapeDtypeStruct((n, MLA_HEADS * D_V), F32)],
        ("parallel", "parallel"), (q, k, v, o, do, lse), comm=comm)


def _seq_rows(nb, lp, width):
    rows = lax.broadcasted_iota(jnp.int32, (lp, width), 0)
    return jnp.concatenate([rows] * nb, axis=0) if nb > 1 else rows


def _lru_gates(u, w_ref, cb, wa, wx, ba, bx, lam):
    xc = (cb + w_ref[pl.ds(3, 1), :] * u + w_ref[pl.ds(2, 1), :] * pltpu.roll(u, 1, axis=0)
          + w_ref[pl.ds(1, 1), :] * pltpu.roll(u, 2, axis=0) + w_ref[pl.ds(0, 1), :] * pltpu.roll(u, 3, axis=0))
    xc16 = xc.astype(BF16)
    ra = _sig_tanh(_nn(xc16, wa) + ba)
    ia = _sig_tanh(_nn(xc16, wx) + bx)
    sp = _softplus(-lam)
    log_a = -C_RGLRU * ra * sp
    a = jnp.exp(log_a)
    x2 = 2.0 * log_a
    mult = jnp.sqrt(jnp.where(x2 > -1e-2, -x2 * (1.0 + x2 * (0.5 + x2 * (1.0 / 6.0))), 1.0 - a * a))
    return xc, xc16, ra, ia, sp, a, mult


def _scan_block_rows(width):
    return lax.broadcasted_iota(jnp.int32, (8, width), 0)


def lru_fwd(zl, conv_w, conv_b, wa, wx, ba, bx, lam, nb, lp, comm=None):
    n = zl.shape[0]
    w = LRU_TILE
    nt = LRU_WIDTH // w
    nblk = lp // 8

    def body(u_ref, gt_ref, cw_ref, cb_ref, wa_ref, wx_ref, ba_ref, bx_ref, lam_ref, y_ref, h_ref, a_s, b_s):
        u = u_ref[...]
        xc, _, _, ia, _, a, mult = _lru_gates(u, cw_ref, cb_ref[...], wa_ref[...], wx_ref[...],
                                              ba_ref[...], bx_ref[...], lam_ref[...])
        row = _seq_rows(nb, lp, w)
        mult = jnp.where(row == PAD, 1.0, mult)
        a_s[...] = a
        b_s[...] = jnp.where(row < PAD, 0.0, mult * (ia * xc))
        r8 = _scan_block_rows(w)

        def blk(i, carry):
            out = []
            for s_id in range(nb):
                off = pl.multiple_of(s_id * lp + i * 8, 8)
                av = a_s[pl.ds(off, 8), :]
                bv = b_s[pl.ds(off, 8), :]
                for sh in (1, 2, 4):
                    keep = r8 >= sh
                    bv = jnp.where(keep, av * pltpu.roll(bv, sh, axis=0) + bv, bv)
                    av = jnp.where(keep, av * pltpu.roll(av, sh, axis=0), av)
                hv = bv + av * carry[s_id]
                h_ref[pl.ds(off, 8), :] = hv
                out.append(jnp.sum(jnp.where(r8 == 7, hv, 0.0), axis=0, keepdims=True))
            return tuple(out)

        lax.fori_loop(0, nblk, blk, tuple(jnp.zeros((1, w), F32) for _ in range(nb)))
        gelu, _ = _gelu_and_grad(gt_ref[...])
        y_ref[...] = h_ref[...] * gelu

    col = lambda c: (0, c)
    return _call(
        body, "lru_fwd", (nt,),
        [pl.BlockSpec((n, w), col), pl.BlockSpec((n, w), lambda c: (0, nt + c)),
         pl.BlockSpec((CONV_W, w), col), pl.BlockSpec((1, w), col),
         pl.BlockSpec((w, w), lambda c: (c, c)), pl.BlockSpec((w, w), lambda c: (c, c)),
         pl.BlockSpec((1, w), col), pl.BlockSpec((1, w), col), pl.BlockSpec((1, w), col)],
        [pl.BlockSpec((n, w), col), pl.BlockSpec((n, w), col)],
        [jax.ShapeDtypeStruct((n, LRU_WIDTH), F32), jax.ShapeDtypeStruct((n, LRU_WIDTH), F32)],
        ("parallel",), (zl, zl, conv_w, conv_b, wa, wx, ba, bx, lam),
        scratch=[pltpu.VMEM((n, w), F32), pltpu.VMEM((n, w), F32)], comm=comm)


def lru_bwd(zl, hs, dy, conv_w, conv_b, wa, wx, ba, bx, lam, nb, lp, comm=None):
    n = zl.shape[0]
    w = LRU_TILE
    nt = LRU_WIDTH // w
    nblk = lp // 8

    def body(u_ref, gt_ref, h_ref, dy_ref, cw_ref, cb_ref, wa_ref, wx_ref, ba_ref, bx_ref, lam_ref,
             du_ref, dgt_ref, dcw_ref, dcb_ref, dba_ref, dbx_ref, dlam_ref, dwa_ref, dwx_ref,
             c_s, d_s, g_s, dwa_s, dwx_s):
        u = u_ref[...]
        lam = lam_ref[...]
        xc, xc16, ra, ia, sp, a, mult = _lru_gates(u, cw_ref, cb_ref[...], wa_ref[...], wx_ref[...],
                                                   ba_ref[...], bx_ref[...], lam)
        row = lax.broadcasted_iota(jnp.int32, (lp, w), 0)
        hv = h_ref[...]
        dyv = dy_ref[...]
        gelu, dgelu = _gelu_and_grad(gt_ref[...])
        dgt_ref[...] = jnp.where(row >= PAD, dyv * hv * dgelu, 0.0)
        c_s[...] = pltpu.roll(a, lp - 1, axis=0)
        d_s[...] = dyv * gelu
        r8 = _scan_block_rows(w)

        def blk(ii, carry):
            off = pl.multiple_of((nblk - 1 - ii) * 8, 8)
            cv = c_s[pl.ds(off, 8), :]
            dv = d_s[pl.ds(off, 8), :]
            for sh in (1, 2, 4):
                keep = r8 < 8 - sh
                dv = jnp.where(keep, cv * pltpu.roll(dv, 8 - sh, axis=0) + dv, dv)
                cv = jnp.where(keep, cv * pltpu.roll(cv, 8 - sh, axis=0), cv)
            gv = dv + cv * carry
            g_s[pl.ds(off, 8), :] = gv
            return jnp.sum(jnp.where(r8 == 0, gv, 0.0), axis=0, keepdims=True)

        lax.fori_loop(0, nblk, blk, jnp.zeros((1, w), F32))
        gv = g_s[...]
        first_row = row == PAD
        db = jnp.where(row >= PAD, gv, 0.0)
        da = jnp.where(row > PAD, gv * pltpu.roll(hv, 1, axis=0), 0.0)
        mult_eff = jnp.where(first_row, 1.0, mult)
        dmult = jnp.where(first_row, 0.0, db * (ia * xc))
        dia = db * mult_eff * xc
        dxc = db * mult_eff * ia
        dla = da * a - dmult * (a * a) / mult
        dra = dla * (-C_RGLRU * sp)
        dsp = jnp.sum(dla * (-C_RGLRU * ra), axis=0, keepdims=True)
        dpa = dra * ra * (1.0 - ra)
        dpx = dia * ia * (1.0 - ia)
        dpa16 = dpa.astype(BF16)
        dpx16 = dpx.astype(BF16)
        dxc = dxc + _nt(dpa16, wa_ref[...]) + _nt(dpx16, wx_ref[...])
        du = cw_ref[pl.ds(CONV_W - 1, 1), :] * dxc
        dcw = [jnp.sum(dxc * u, axis=0, keepdims=True)]
        for tap in range(1, CONV_W):
            dcw.insert(0, jnp.sum(dxc * pltpu.roll(u, tap, axis=0), axis=0, keepdims=True))
            du = du + cw_ref[pl.ds(CONV_W - 1 - tap, 1), :] * pltpu.roll(dxc, lp - tap, axis=0)
        du_ref[...] = jnp.where(row >= PAD, du, 0.0)
        first = pl.program_id(1) == 0
        _accumulate(dlam_ref, -_sig(-lam) * dsp, first)
        _accumulate(dba_ref, jnp.sum(dpa, axis=0, keepdims=True), first)
        _accumulate(dbx_ref, jnp.sum(dpx, axis=0, keepdims=True), first)
        _accumulate(dcb_ref, jnp.sum(dxc, axis=0, keepdims=True), first)
        _accumulate(dcw_ref, jnp.concatenate(dcw, axis=0), first)
        _accumulate(dwa_s, _tn(xc16, dpa16), first)
        _accumulate(dwx_s, _tn(xc16, dpx16), first)

        @pl.when(pl.program_id(1) == nb - 1)
        def _():
            for j in range(w // LRU_BLOCK):
                blk_rows = slice(j * LRU_BLOCK, (j + 1) * LRU_BLOCK)
                dwa_ref[0, blk_rows, :] = dwa_s[blk_rows, blk_rows]
                dwx_ref[0, blk_rows, :] = dwx_s[blk_rows, blk_rows]

    col = lambda c, b: (0, c)
    vec = pl.BlockSpec((1, w), col)
    mat = pl.BlockSpec((w, w), lambda c, b: (c, c))
    big = pl.BlockSpec((lp, w), lambda c, b: (b, c))
    dmat = pl.BlockSpec((1, w, LRU_BLOCK), lambda c, b: (c, 0, 0))
    return _call(
        body, "lru_bwd", (nt, nb),
        [big, pl.BlockSpec((lp, w), lambda c, b: (b, nt + c)), big, big,
         pl.BlockSpec((CONV_W, w), col), vec, mat, mat, vec, vec, vec],
        [big, big, pl.BlockSpec((CONV_W, w), col), vec, vec, vec, vec, dmat, dmat],
        [jax.ShapeDtypeStruct((n, LRU_WIDTH), F32), jax.ShapeDtypeStruct((n, LRU_WIDTH), F32),
         jax.ShapeDtypeStruct((CONV_W, LRU_WIDTH), F32), jax.ShapeDtypeStruct((1, LRU_WIDTH), F32),
         jax.ShapeDtypeStruct((1, LRU_WIDTH), F32), jax.ShapeDtypeStruct((1, LRU_WIDTH), F32),
         jax.ShapeDtypeStruct((1, LRU_WIDTH), F32), jax.ShapeDtypeStruct((nt, w, LRU_BLOCK), F32),
         jax.ShapeDtypeStruct((nt, w, LRU_BLOCK), F32)],
        ("parallel", "arbitrary"), (zl, zl, hs, dy, conv_w, conv_b, wa, wx, ba, bx, lam),
        scratch=[pltpu.VMEM((lp, w), F32), pltpu.VMEM((lp, w), F32), pltpu.VMEM((lp, w), F32),
                 pltpu.VMEM((w, w), F32), pltpu.VMEM((w, w), F32)], comm=comm)


def outproj_fwd(h, ya, yl, gao, glo, wout):
    n, d = h.shape
    half = ya.shape[1]
    tm = _tile(n, 704)

    def body(h_ref, ya_ref, yl_ref, gao_ref, glo_ref, w_ref, ho_ref, yn_ref):
        xa = ya_ref[...]
        xl = yl_ref[...]
        na = (xa * _rms_r(xa) * gao_ref[...]).astype(BF16)
        nl = (xl * _rms_r(xl) * glo_ref[...]).astype(BF16)
        yn_ref[:, :half] = na
        yn_ref[:, half:] = nl
        ho_ref[...] = h_ref[...] + _nn(na, w_ref[:half, :]) + _nn(nl, w_ref[half:, :])

    return pl.pallas_call(
        body, name="outproj_fwd", grid=(n // tm,),
        in_specs=[_row(tm, d), _row(tm, half), _row(tm, half), _fixed((1, half)), _fixed((1, half)), VMEM_WHOLE],
        out_specs=[_row(tm, d), _row(tm, 2 * half)],
        out_shape=[jax.ShapeDtypeStruct((n, d), F32), jax.ShapeDtypeStruct((n, 2 * half), BF16)],
        compiler_params=_params(("parallel",)),
    )(h, ya, yl, gao, glo, wout)


def outproj_bwd(dh, ya, yl, gao, glo, wout):
    n, d = dh.shape
    half = ya.shape[1]
    tm = _tile(n, 704)

    def body(dh_ref, ya_ref, yl_ref, gao_ref, glo_ref, w_ref, dya_ref, dyl_ref, dgao_ref, dglo_ref):
        d16 = dh_ref[...].astype(BF16)
        xa = ya_ref[...]
        xl = yl_ref[...]
        dxa, dga = _rms_bwd(xa, _rms_r(xa), gao_ref[...], _nt(d16, w_ref[:half, :]))
        dxl, dgl = _rms_bwd(xl, _rms_r(xl), glo_ref[...], _nt(d16, w_ref[half:, :]))
        dya_ref[...] = dxa
        dyl_ref[...] = dxl
        first = pl.program_id(0) == 0
        _accumulate(dgao_ref, dga, first)
        _accumulate(dglo_ref, dgl, first)

    return pl.pallas_call(
        body, name="outproj_bwd", grid=(n // tm,),
        in_specs=[_row(tm, d), _row(tm, half), _row(tm, half), _fixed((1, half)), _fixed((1, half)), VMEM_WHOLE],
        out_specs=[_row(tm, half), _row(tm, half), _fixed((1, half)), _fixed((1, half))],
        out_shape=[jax.ShapeDtypeStruct((n, half), F32), jax.ShapeDtypeStruct((n, half), F32),
                   jax.ShapeDtypeStruct((1, half), F32), jax.ShapeDtypeStruct((1, half), F32)],
        compiler_params=_params(("arbitrary",)),
    )(dh, ya, yl, gao, glo, wout)


def _loss_and_grad(x, gv, tgt, first_row):
    tm, d = x.shape
    r = _rms_r(x)
    row = first_row + lax.broadcasted_iota(jnp.int32, (tm, d), 0)
    diff = jnp.where(row >= FIRST_FRAME, x * r * gv - tgt, 0.0)
    part = 0.5 * jnp.sum(jnp.sum(diff * diff, axis=-1, keepdims=True) * (1.0 / d), axis=0, keepdims=True)
    dx, dg = _rms_bwd(x, r, gv, diff * (1.0 / d))
    return dx, part, dg


def assemble_cols(g, name):
    _, k, ns = g.shape

    def body(g_ref, o_ref):
        for j in range(N_DEV):
            o_ref[:, j * ns:(j + 1) * ns] = g_ref[j]

    return pl.pallas_call(body, name=name, out_shape=jax.ShapeDtypeStruct((k, N_DEV * ns), g.dtype),
                          compiler_params=_params(None))(g)


def split_cols(x, name):
    k, cols = x.shape
    ns = cols // N_DEV

    def body(x_ref, o_ref):
        for j in range(N_DEV):
            o_ref[j] = x_ref[:, j * ns:(j + 1) * ns]

    return pl.pallas_call(body, name=name, out_shape=jax.ShapeDtypeStruct((N_DEV, k, ns), x.dtype),
                          compiler_params=_params(None))(x)


def _slab_rows(w, per_head):
    k = w.shape[1]
    w = w.reshape(MLA_HEADS, per_head, k)
    return jnp.pad(w, ((0, 0), (0, HEAD_SLAB - per_head), (0, 0))).reshape(MLA_HEADS * HEAD_SLAB, k)


def _unslab_rows(w, per_head):
    k = w.shape[1]
    return w.reshape(MLA_HEADS, HEAD_SLAB, k)[:, :per_head].reshape(MLA_HEADS * per_head, k)


def meta_grad(dh0, nb, lp):
    d = dh0.shape[1]
    ns = d // N_DEV
    per_seq = lp // N_META

    def body(x_ref, o_ref):
        x = x_ref[...]
        for j in range(N_DEV):
            _accumulate(o_ref.at[j], x[:, j * ns:(j + 1) * ns], pl.program_id(0) == 0)

    return pl.pallas_call(
        body, name="meta_grad", grid=(nb,),
        in_specs=[pl.BlockSpec((N_META, d), lambda b: (b * per_seq + PAD // N_META, 0))],
        out_specs=pl.BlockSpec((N_DEV, N_META, ns), lambda b: (0, 0, 0)),
        out_shape=jax.ShapeDtypeStruct((N_DEV, N_META, ns), F32),
        compiler_params=_params(("arbitrary",)))(dh0)


VECTORS = [("ffn1_norm", 1024), ("mix_norm", 1024), ("q_latent_norm", 384), ("kv_latent_norm", 256),
           ("q_head_norm", 192), ("k_head_norm", 192), ("conv_b", 512), ("gate_a_b", 512), ("gate_x_b", 512),
           ("lru_lambda", 512), ("attn_out_norm", 512), ("lru_out_norm", 512), ("ffn2_norm", 1024),
           ("final_norm", 1024)]
VEC_ROWS = 16
LOSS_ROW = len(VECTORS)
GATES = ["gate_a_w", "gate_x_w"]


def pack_vectors(grads, loss):
    def body(*refs):
        o_ref = refs[-1]
        o_ref[...] = jnp.zeros_like(o_ref)
        for t, (ref, (_, cnt)) in enumerate(zip(refs[:-2], VECTORS)):
            o_ref[t:t + 1, :cnt] = ref[:, :cnt]
        o_ref[LOSS_ROW:LOSS_ROW + 1, :LANES] = refs[-2][...]

    return pl.pallas_call(body, name="pack_vectors", out_shape=jax.ShapeDtypeStruct((VEC_ROWS, D_MODEL), F32),
                          compiler_params=_params(None))(*[grads[name] for name, _ in VECTORS], loss)


def _adamw_update(w, g, m, v):
    c1 = 1.0 / (1.0 - ADAM_B1 ** ADAM_STEP)
    c2 = 1.0 / (1.0 - ADAM_B2 ** ADAM_STEP)
    mn = ADAM_B1 * m + (1.0 - ADAM_B1) * g
    vn = ADAM_B2 * v + (1.0 - ADAM_B2) * (g * g)
    delta = -ADAM_LR * ((mn * c1) / (jnp.sqrt(vn * c2) + ADAM_EPS) + ADAM_WD * w)
    return delta, mn, vn


def _sum_slots(ref, index=()):
    acc = ref[(0,) + index].astype(F32)
    for s in range(1, N_DEV):
        acc = acc + ref[(s,) + index].astype(F32)
    return acc


def adamw_sharded(r, w, m, v, name):
    rows, cols = w.shape
    tr = _tile(rows, 256, 16) if rows % 16 == 0 else rows

    def body(r_ref, w_ref, m_ref, v_ref, g_ref, d_ref, mo_ref, vo_ref):
        g = _sum_slots(r_ref)
        g_ref[...] = g
        d_ref[...], mo_ref[...], vo_ref[...] = _adamw_update(w_ref[...], g, m_ref[...], v_ref[...])

    spec = pl.BlockSpec((tr, cols), lambda i: (i, 0))
    shape = jax.ShapeDtypeStruct((rows, cols), F32)
    return pl.pallas_call(
        body, name=name, grid=(rows // tr,),
        in_specs=[pl.BlockSpec((N_DEV, tr, cols), lambda i: (0, i, 0))] + [spec] * 3,
        out_specs=[spec] * 4, out_shape=[shape] * 4,
        compiler_params=_params(("parallel",)),
    )(r, w, m, v)


def adamw_small(r_vec, r_gates, w, m, v):
    nt = len(VECTORS) + len(GATES)

    def body(*refs):
        rv_ref = refs[0]
        rg_refs = refs[1:1 + len(GATES)]
        base = 1 + len(GATES)
        w_refs, m_refs, v_refs = (refs[base + i * nt:base + (i + 1) * nt] for i in range(3))
        outs = refs[base + 3 * nt:]
        g_o, d_o, m_o, v_o = (outs[i * nt:(i + 1) * nt] for i in range(4))
        outs[4 * nt][...] = _sum_slots(rv_ref, (slice(LOSS_ROW, LOSS_ROW + 1), slice(0, LANES)))
        for t in range(nt):
            if t < len(VECTORS):
                cnt = VECTORS[t][1]
                g = _sum_slots(rv_ref, (slice(t, t + 1), slice(0, cnt)))
            else:
                g = _sum_slots(rg_refs[t - len(VECTORS)])
            g_o[t][...] = g
            d_o[t][...], m_o[t][...], v_o[t][...] = _adamw_update(w_refs[t][...], g, m_refs[t][...], v_refs[t][...])

    shapes = [jax.ShapeDtypeStruct(a.shape, F32) for a in w]
    res = pl.pallas_call(body, name="adamw_small", out_shape=shapes * 4 + [jax.ShapeDtypeStruct((1, LANES), F32)],
                         compiler_params=_params(None))(r_vec, *r_gates, *w, *m, *v)
    return [res[i * nt:(i + 1) * nt] for i in range(4)], res[4 * nt]


def _block_diag(w):
    nb, n, _ = w.shape
    eye = jnp.eye(nb, dtype=w.dtype)
    return (eye[:, None, :, None] * w[:, :, None, :]).reshape(nb * n, nb * n)


def _two_d(a):
    if a.ndim == 3:
        return a.reshape(a.shape[1], a.shape[2])
    if a.ndim == 4:
        return a.reshape(a.shape[1] * a.shape[2], a.shape[3])
    return a


_WEIGHT_NAMES = ['meta_tokens', 'ffn1_norm', 'ffn1_w_gate', 'ffn1_w_up', 'ffn1_w_down', 'mix_norm', 'w_in',
                 'q_latent_norm', 'w_uq', 'kv_latent_norm', 'w_uk', 'w_uv', 'q_head_norm', 'k_head_norm', 'conv_w',
                 'conv_b', 'gate_a_w', 'gate_a_b', 'gate_x_w', 'gate_x_b', 'lru_lambda', 'attn_out_norm',
                 'lru_out_norm', 'w_out', 'ffn2_norm', 'ffn2_w_gate', 'ffn2_w_up', 'ffn2_w_down', 'final_norm']


COLUMN_SHARDED = ("ffn1_w_gate", "ffn1_w_up", "ffn2_w_gate", "ffn2_w_up", "w_in", "w_uq", "w_uk", "w_uv")


def train_step(x, tgt, w, m, v):
    nb, seq, d = x.shape
    lp = PAD + N_META + seq
    n = nb * lp
    def local(a, name):
        a = _two_d(a)
        return a.T if name in COLUMN_SHARDED else a

    sh = {name: local(w[name], name) for name in _WEIGHT_NAMES}
    m2 = {name: local(m[name], name) for name in _WEIGHT_NAMES}
    v2 = {name: local(v[name], name) for name in _WEIGHT_NAMES}

    def b16(name):
        return sh[name].astype(BF16)

    out = {}

    def update(name, landed):
        out[name] = adamw_sharded(landed, sh[name], m2[name], v2[name], "adamw_" + name)

    g_wg1, g_wu1, g_meta, g_conv = exchange(
        [b16("ffn1_w_gate"), b16("ffn1_w_up"), sh["meta_tokens"], sh["conv_w"]], ["gather"] * 4, "gather_ffn1")
    wg1, wu1 = g_wg1.reshape(D_FF, d), g_wu1.reshape(D_FF, d)
    meta = assemble_cols(g_meta, "assemble_meta")
    conv_w = assemble_cols(g_conv, "assemble_conv")

    front = jnp.concatenate([jnp.zeros((PAD, d), F32), meta], axis=0)
    h0 = jnp.concatenate([jnp.broadcast_to(front[None], (nb, FIRST_FRAME, d)), x], axis=1).reshape(n, d)
    tgt_p = jnp.concatenate([jnp.zeros((nb, FIRST_FRAME, d), F32), tgt], axis=1).reshape(n, d)
    tables = _rope_tables(lp)
    zero_tail = jnp.zeros((1, HEAD_SLAB - D_QK), F32)
    gqh = jnp.concatenate([sh["q_head_norm"], zero_tail], axis=1)
    gkh = jnp.concatenate([sh["k_head_norm"], zero_tail], axis=1)
    wa = _block_diag(w["gate_a_w"][0]).astype(BF16)
    wx = _block_diag(w["gate_x_w"][0]).astype(BF16)

    (u1, a1, b1, s1), (g_wd1, g_in, g_uq, g_uk, g_uv, g_out) = ffn_up(
        h0, sh["ffn1_norm"], wg1, wu1, "ffn1_up",
        comm=([b16("ffn1_w_down"), b16("w_in"), b16("w_uq"), b16("w_uk"), b16("w_uv"), b16("w_out")], ["gather"] * 6))
    wd1 = g_wd1.reshape(D_FF, d)
    h1 = ffn_down(h0, s1, wd1, "ffn1_down")
    mla_rows = MLA_IN - D_ROPE
    w_in = g_in.reshape(mla_rows + 2 * LRU_WIDTH, d)
    wm = jnp.concatenate([w_in[:mla_rows], jnp.zeros((D_ROPE, d), BF16)], axis=0)
    wl = w_in[mla_rows:]
    wuq = _slab_rows(g_uq.reshape(MLA_HEADS * D_QK, Q_RANK), D_QK)
    wuk = _slab_rows(g_uk.reshape(MLA_HEADS * D_NOPE, KV_RANK), D_NOPE)
    wuv = g_uv.reshape(MLA_HEADS * D_V, KV_RANK)
    w_out = g_out.reshape(d, d)

    u2, zm, zl = inproj_fwd(h1, sh["mix_norm"], wm, wl)
    q, k, vv, qn, cn = mla_prep_fwd(zm, sh["q_latent_norm"], sh["kv_latent_norm"], wuq, wuk, wuv, gqh, gkh, tables, lp)
    (y_mla, lse), (g_wu2, g_wd2) = attn_fwd(
        q, k, vv, nb, lp, comm=([b16("ffn2_w_up"), b16("ffn2_w_down")], ["gather"] * 2))
    (y_lru, hs), (g_wg2,) = lru_fwd(zl, conv_w, sh["conv_b"], wa, wx, sh["gate_a_b"], sh["gate_x_b"], sh["lru_lambda"],
                                    nb, lp, comm=([b16("ffn2_w_gate")], ["gather"]))
    wg2, wu2, wd2 = (g.reshape(D_FF, d) for g in (g_wg2, g_wu2, g_wd2))
    h2, yn = outproj_fwd(h1, y_mla, y_lru, sh["attn_out_norm"], sh["lru_out_norm"], w_out)
    dh3, u3, a3, b3, loss, g_final = ffn_fwd_loss(h2, sh["ffn2_norm"], wg2, wu2, wd2, sh["final_norm"], tgt_p, lp,
                                                  "ffn2_fwd_loss")

    vec = {"final_norm": g_final}
    (dh2, da3, db3, sh3, vec["ffn2_norm"]), _ = ffn_bwd_act(dh3, h2, sh["ffn2_norm"], a3, b3, wg2, wu2, wd2, "ffn2_bwd")
    ff_shards = (N_DEV, D_FF // N_DEV, d)
    dwg2 = tn_matmul(da3, u3, "ffn2_dwg", "bf16").reshape(ff_shards)
    dwu2 = tn_matmul(db3, u3, "ffn2_dwu", "bf16").reshape(ff_shards)
    dwd2 = tn_matmul(sh3, dh3, "ffn2_dwd", "bf16").reshape(ff_shards)

    dy_mla, dy_lru, vec["attn_out_norm"], vec["lru_out_norm"] = outproj_bwd(
        dh2, y_mla, y_lru, sh["attn_out_norm"], sh["lru_out_norm"], w_out)
    dw_out = tn_matmul(yn, dh2, "dw_out", "bf16").reshape(N_DEV, d // N_DEV, d)
    (du, dgate, dconv, vec["conv_b"], vec["gate_a_b"], vec["gate_x_b"], vec["lru_lambda"], dga, dgx), (r_wg2,) = lru_bwd(
        zl, hs, dy_lru, conv_w, sh["conv_b"], wa, wx, sh["gate_a_b"], sh["gate_x_b"], sh["lru_lambda"], nb, lp,
        comm=([dwg2], ["scatter"]))
    update("ffn2_w_gate", r_wg2)

    (dq, dk, dv), (r_wu2,) = attn_bwd(q, k, vv, y_mla, dy_mla, lse, nb, lp, comm=([dwu2], ["scatter"]))
    update("ffn2_w_up", r_wu2)

    (dzm, dqr, dkr, vec["q_latent_norm"], vec["kv_latent_norm"], vec["q_head_norm"], vec["k_head_norm"]), (r_wd2,) = (
        mla_prep_bwd(dq, dk, dv, zm, qn, cn, sh["q_latent_norm"], sh["kv_latent_norm"], wuq, wuk, wuv, gqh, gkh,
                     tables, lp, comm=([dwd2], ["scatter"])))
    update("ffn2_w_down", r_wd2)
    dwuq = _unslab_rows(tn_matmul(dqr, qn, "dw_uq"), D_QK).reshape(N_DEV, -1, Q_RANK)
    dwuk = _unslab_rows(tn_matmul(dkr, cn, "dw_uk"), D_NOPE).reshape(N_DEV, -1, KV_RANK)
    dwuv = tn_matmul(dv, cn, "dw_uv").reshape(N_DEV, -1, KV_RANK)
    dh1, vec["mix_norm"] = inproj_bwd(dzm, du, dgate, dh2, h1, sh["mix_norm"], wm, wl)
    dw_in = jnp.concatenate([tn_matmul(dzm, u2, "dw_in_mla")[:mla_rows], tn_matmul(du, u2, "dw_in_u"),
                             tn_matmul(dgate, u2, "dw_in_gate")], axis=0).reshape(N_DEV, -1, d)

    (dh0, da1, db1, sh1, vec["ffn1_norm"]), landed = ffn_bwd_act(
        dh1, h0, sh["ffn1_norm"], a1, b1, wg1, wu1, wd1, "ffn1_bwd",
        comm=([dw_in, dwuq, dwuk, dwuv, dw_out, split_cols(dconv, "split_conv")], ["scatter"] * 6))
    for name, r in zip(("w_in", "w_uq", "w_uk", "w_uv", "w_out", "conv_w"), landed):
        update(name, r)

    dwg1 = tn_matmul(da1, u1, "ffn1_dwg", "bf16").reshape(ff_shards)
    dwu1, (r_wg1,) = tn_matmul(db1, u1, "ffn1_dwu", "bf16", comm=([dwg1], ["scatter"]))
    dwd1, (r_wu1,) = tn_matmul(sh1, dh1, "ffn1_dwd", "bf16", comm=([dwu1.reshape(ff_shards)], ["scatter"]))
    dwd1 = dwd1.reshape(ff_shards)
    dmeta = meta_grad(dh0, nb, lp)
    gates = [dga.reshape(LRU_WIDTH, LRU_BLOCK), dgx.reshape(LRU_WIDTH, LRU_BLOCK)]
    r_wd1, r_meta, r_vec, r_ga, r_gx = exchange(
        [dwd1, dmeta, pack_vectors(vec, loss)] + gates, ["scatter"] * 2 + ["gather"] * 3, "exchange_last")
    update("ffn1_w_gate", r_wg1)
    update("ffn1_w_up", r_wu1)
    update("ffn1_w_down", r_wd1)
    update("meta_tokens", r_meta)

    small = [name for name, _ in VECTORS] + GATES
    res, total_loss = adamw_small(r_vec, [r_ga, r_gx], [sh[nm] for nm in small], [m2[nm] for nm in small],
                                  [v2[nm] for nm in small])
    for i, name in enumerate(small):
        out[name] = [res[j][i] for j in range(4)]

    grad_x = dh0.reshape(nb, lp, d)[:, FIRST_FRAME:]
    loss = total_loss[0, 0]

    def as_given(a, name):
        return (a.T if name in COLUMN_SHARDED else a).reshape(w[name].shape)

    cols = [[as_given(out[name][j], name) for name in _WEIGHT_NAMES] for j in range(4)]
    return (loss, grad_x, *cols[0], *cols[1], *cols[2], *cols[3])


def kernel(x, meta_tokens, ffn1_norm, ffn1_w_gate, ffn1_w_up, ffn1_w_down, mix_norm, w_in, q_latent_norm, w_uq, kv_latent_norm, w_uk, w_uv, q_head_norm, k_head_norm, conv_w, conv_b, gate_a_w, gate_a_b, gate_x_w, gate_x_b, lru_lambda, attn_out_norm, lru_out_norm, w_out, ffn2_norm, ffn2_w_gate, ffn2_w_up, ffn2_w_down, final_norm, loss_target, m_meta_tokens, m_ffn1_norm, m_ffn1_w_gate, m_ffn1_w_up, m_ffn1_w_down, m_mix_norm, m_w_in, m_q_latent_norm, m_w_uq, m_kv_latent_norm, m_w_uk, m_w_uv, m_q_head_norm, m_k_head_norm, m_conv_w, m_conv_b, m_gate_a_w, m_gate_a_b, m_gate_x_w, m_gate_x_b, m_lru_lambda, m_attn_out_norm, m_lru_out_norm, m_w_out, m_ffn2_norm, m_ffn2_w_gate, m_ffn2_w_up, m_ffn2_w_down, m_final_norm, v_meta_tokens, v_ffn1_norm, v_ffn1_w_gate, v_ffn1_w_up, v_ffn1_w_down, v_mix_norm, v_w_in, v_q_latent_norm, v_w_uq, v_kv_latent_norm, v_w_uk, v_w_uv, v_q_head_norm, v_k_head_norm, v_conv_w, v_conv_b, v_gate_a_w, v_gate_a_b, v_gate_x_w, v_gate_x_b, v_lru_lambda, v_attn_out_norm, v_lru_out_norm, v_w_out, v_ffn2_norm, v_ffn2_w_gate, v_ffn2_w_up, v_ffn2_w_down, v_final_norm):
    args = locals()
    w = {name: args[name] for name in _WEIGHT_NAMES}
    m = {name: args["m_" + name] for name in _WEIGHT_NAMES}
    v = {name: args["v_" + name] for name in _WEIGHT_NAMES}
    return train_step(x, loss_target, w, m, v)
```

```python
import math

import jax
import jax.numpy as jnp
from jax import lax
from jax.experimental import pallas as pl
from jax.experimental.pallas import tpu as pltpu

F32 = jnp.float32
BF16 = jnp.bfloat16

D_MODEL = 1024
CHUNK = 64
CHUNK_SHIFT = 6
N_META = 16
PAD = CHUNK - N_META
FIRST_FRAME = PAD + N_META
MLA_HEADS = 4
D_NOPE = 128
D_ROPE = 64
D_QK = D_NOPE + D_ROPE
D_V = 128
HEAD_SLAB = 256
KV_RANK = 256
Q_RANK = 384
ROPE_THETA = 10000.0
LRU_WIDTH = 512
LRU_BLOCKS = 8
LRU_BLOCK = 64
LRU_TILE = 128
CONV_W = 4
C_RGLRU = 8.0
D_FF = 2816
MLA_IN = 768
EPS = 1e-6
NEG_INF = -1e30
N_DEV = 8
LANES = 128
VMEM_LIMIT = 52 * 1024 * 1024
TN_ROWS = 4224
TN_X_BYTES = 12 * 1024 * 1024
TN_Y_BYTES = 9 * 1024 * 1024 // 2

ADAM_LR = 0.001
ADAM_B1 = 0.9
ADAM_B2 = 0.999
ADAM_EPS = 1e-08
ADAM_WD = 0.01
ADAM_STEP = 10

VMEM_WHOLE = pl.BlockSpec(memory_space=pltpu.VMEM)
HBM_WHOLE = pl.BlockSpec(memory_space=pl.ANY)


def _params(sems):
    if sems is None:
        return pltpu.CompilerParams(vmem_limit_bytes=VMEM_LIMIT)
    return pltpu.CompilerParams(dimension_semantics=sems, vmem_limit_bytes=VMEM_LIMIT)


def _tile(n, cap, mult=16):
    best = None
    for t in range(mult, min(n, cap) + 1, mult):
        if n % t == 0:
            best = t
    assert best is not None, (n, cap, mult)
    return best


def _row(tm, d):
    return pl.BlockSpec((tm, d), lambda i: (i, 0))


def _fixed(shape):
    return pl.BlockSpec(shape, lambda i: (0,) * len(shape))


def _mesh_position():
    return lax.axis_index("x"), lax.axis_index("y"), lax.axis_index("c")


def _flat_index(x, y, c):
    return 4 * x + 2 * y + c


def _peers(x, y, c):
    out = []
    for k in range(1, N_DEV):
        fx, fy, fc = (k >> 2) & 1, (k >> 1) & 1, k & 1
        out.append((1 - x if fx else x, 1 - y if fy else y, 1 - c if fc else c))
    return out


def _comm_out_shapes(srcs, modes):
    return [jax.ShapeDtypeStruct((N_DEV,) + s.shape if md == "gather" else s.shape, s.dtype)
            for s, md in zip(srcs, modes)]


def _comm_scratch(n):
    per_peer = n * (N_DEV - 1)
    return [pltpu.SemaphoreType.DMA((per_peer,)), pltpu.SemaphoreType.DMA((per_peer,)), pltpu.SemaphoreType.DMA((n,))]


class _Copies:
    def __init__(self, own, first, relay):
        self.own, self.first, self.relay = own, first, relay

    def start(self):
        for cp in self.own + self.first:
            cp.start()

    def forward(self):
        for arrival, onward in self.relay:
            arrival.wait_recv()
            onward.start()

    def finish(self):
        arrivals = [a for a, _ in self.relay]
        onward = [f for _, f in self.relay]
        for cp in self.first + onward:
            if not any(cp is a for a in arrivals):
                cp.wait_recv()
        for cp in self.first + onward:
            cp.wait_send()
        for cp in self.own:
            cp.wait()


def _comm_copies(src_refs, dst_refs, modes, send, recv, local):
    x, y, c = _mesh_position()
    me = _flat_index(x, y, c)
    n = len(modes)
    sibling = (x, y, 1 - c)
    chips = [(1 - x, y), (x, 1 - y), (1 - x, 1 - y)]

    def remote(src, dst, k, t, to):
        return pltpu.make_async_remote_copy(src_ref=src, dst_ref=dst, send_sem=send.at[k * n + t],
                                            recv_sem=recv.at[k * n + t], device_id=to,
                                            device_id_type=pl.DeviceIdType.MESH)

    own, first, relay = [], [], []
    for t, (src, dst, md) in enumerate(zip(src_refs, dst_refs, modes)):
        if md == "scatter":
            own.append(pltpu.make_async_copy(src.at[me], dst.at[me], local.at[t]))
            for k, peer in enumerate(_peers(x, y, c)):
                first.append(remote(src.at[_flat_index(*peer)], dst.at[me], k, t, peer))
        else:
            own.append(pltpu.make_async_copy(src, dst.at[me], local.at[t]))
            first.append(remote(src, dst.at[me], 0, t, sibling))
            for j, chip in enumerate(chips):
                arrival = remote(src, dst.at[me], 1 + j, t, (*chip, c))
                landed = dst.at[_flat_index(*chip, c)]
                first.append(arrival)
                relay.append((arrival, remote(landed, landed, 4 + j, t, sibling)))
    return _Copies(own, first, relay)


def _hosted(body, n_in, n_out, modes, grid):
    t = len(modes)
    total = math.prod(grid)

    def wrapped(*refs):
        ins, csrc = refs[:n_in], refs[n_in:n_in + t]
        outs = refs[n_in + t:n_in + t + n_out]
        cdst = refs[n_in + t + n_out:n_in + 2 * t + n_out]
        scratch = refs[n_in + 2 * t + n_out:-3]
        copies = _comm_copies(csrc, cdst, modes, *refs[-3:])
        step = pl.program_id(0)
        for axis in range(1, len(grid)):
            step = step * grid[axis] + pl.program_id(axis)

        @pl.when(step == 0)
        def _():
            copies.start()

        body(*ins, *outs, *scratch)

        @pl.when(step == (total * 3) // 5)
        def _():
            copies.forward()

        @pl.when(step == total - 1)
        def _():
            copies.finish()

    return wrapped


def _call(body, name, grid, in_specs, out_specs, out_shape, sems, args, scratch=(), comm=None):
    if comm is None:
        outs = pl.pallas_call(body, name=name, grid=grid, in_specs=in_specs, out_specs=out_specs, out_shape=out_shape,
                              scratch_shapes=list(scratch), compiler_params=_params(sems))(*args)
        return outs, []
    srcs, modes = comm
    n = len(modes)
    res = pl.pallas_call(
        _hosted(body, len(in_specs), len(out_specs), modes, grid), name=name, grid=grid,
        in_specs=list(in_specs) + [HBM_WHOLE] * n, out_specs=list(out_specs) + [HBM_WHOLE] * n,
        out_shape=list(out_shape) + _comm_out_shapes(srcs, modes),
        scratch_shapes=list(scratch) + _comm_scratch(n),
        compiler_params=_params(("arbitrary",) * len(grid)))(*args, *srcs)
    return res[:len(out_specs)], res[len(out_specs):]


def exchange(srcs, modes, name):
    n = len(modes)

    def body(*refs):
        copies = _comm_copies(refs[:n], refs[n:2 * n], modes, *refs[2 * n:])
        copies.start()
        copies.forward()
        copies.finish()

    return pl.pallas_call(body, name=name, in_specs=[HBM_WHOLE] * n, out_specs=[HBM_WHOLE] * n,
                          out_shape=_comm_out_shapes(srcs, modes), scratch_shapes=_comm_scratch(n))(*srcs)


def _nn(a, b):
    return jnp.dot(a, b, preferred_element_type=F32)


def _nt(a, b):
    return lax.dot_general(a, b, (((1,), (1,)), ((), ())), preferred_element_type=F32)


def _tn(a, b):
    return lax.dot_general(a, b, (((0,), (0,)), ((), ())), preferred_element_type=F32)


def _sig(x):
    return 1.0 / (1.0 + jnp.exp(-x))


def _rms_r(x, n=None):
    n = x.shape[-1] if n is None else n
    return lax.rsqrt(jnp.sum(x * x, axis=-1, keepdims=True) * (1.0 / n) + EPS)


def _rms_bwd(x, r, g, dy, n=None):
    n = x.shape[-1] if n is None else n
    xhat = x * r
    dxhat = dy * g
    dx = r * (dxhat - xhat * (jnp.sum(dxhat * xhat, axis=-1, keepdims=True) * (1.0 / n)))
    return dx, jnp.sum(dy * xhat, axis=0, keepdims=True)


def _accumulate(ref, val, first):
    @pl.when(first)
    def _():
        ref[...] = val

    @pl.when(jnp.logical_not(first))
    def _():
        ref[...] += val


_GELU_C = math.sqrt(2.0 / math.pi)


def _gelu_and_grad(x):
    inner = _GELU_C * (x + 0.044715 * x * x * x)
    t = jnp.tanh(inner)
    gelu = 0.5 * x * (1.0 + t)
    dgelu = 0.5 * (1.0 + t) + 0.5 * x * (1.0 - t * t) * _GELU_C * (1.0 + 3.0 * 0.044715 * x * x)
    return gelu, dgelu


def _log1p_small(t):
    return jnp.where(t < 1e-3, t * (1.0 - t * (0.5 - t * (1.0 / 3.0))), jnp.log(1.0 + t))


def _softplus(x):
    return jnp.maximum(x, 0.0) + _log1p_small(jnp.exp(-jnp.abs(x)))


def _sig_tanh(x):
    return 0.5 + 0.5 * jnp.tanh(0.5 * x)


def _ff_chunks(f):
    return 2 if (f // 2) % LANES == 0 else 1


def _swiglu_half(x, g_ref, wg_ref, wu_ref, wd_ref, a_ref, b_ref, fc):
    f = wg_ref.shape[0]
    u = (x * _rms_r(x) * g_ref[...]).astype(BF16)
    acc = jnp.zeros(x.shape, F32)
    for c in range(f // fc):
        cols = slice(c * fc, (c + 1) * fc)
        a = _nt(u, wg_ref[cols, :])
        b = _nt(u, wu_ref[cols, :])
        s = (a * _sig(a) * b).astype(BF16)
        acc = acc + _nn(s, wd_ref[cols, :])
        a_ref[:, cols] = a.astype(BF16)
        b_ref[:, cols] = b.astype(BF16)
    return x + 0.5 * acc, u


def ffn_up(h, g, wg, wu, name, comm=None):
    n, d = h.shape
    f = wg.shape[0]
    tm = _tile(n, 528)
    fc = 2 * LANES if f % (2 * LANES) == 0 else f

    def body(h_ref, g_ref, wg_ref, wu_ref, u_ref, a_ref, b_ref, s_ref):
        x = h_ref[...]
        u = (x * _rms_r(x) * g_ref[...]).astype(BF16)
        u_ref[...] = u
        for c in range(f // fc):
            cols = slice(c * fc, (c + 1) * fc)
            a = _nt(u, wg_ref[cols, :])
            b = _nt(u, wu_ref[cols, :])
            a_ref[:, cols] = a.astype(BF16)
            b_ref[:, cols] = b.astype(BF16)
            s_ref[:, cols] = (0.5 * (a * _sig(a) * b)).astype(BF16)

    wide = jax.ShapeDtypeStruct((n, f), BF16)
    return _call(
        body, name, (n // tm,),
        [_row(tm, d), _fixed((1, d)), VMEM_WHOLE, VMEM_WHOLE],
        [_row(tm, d), _row(tm, f), _row(tm, f), _row(tm, f)],
        [jax.ShapeDtypeStruct((n, d), BF16), wide, wide, wide],
        ("parallel",), (h, g, wg, wu), comm=comm)


def ffn_down(h, s, wd, name, comm=None):
    n, d = h.shape
    f = wd.shape[0]
    tm = _tile(n, 528)

    def body(h_ref, s_ref, wd_ref, ho_ref):
        ho_ref[...] = h_ref[...] + _nn(s_ref[...], wd_ref[...])

    (ho,), landed = _call(body, name, (n // tm,), [_row(tm, d), _row(tm, f), VMEM_WHOLE], [_row(tm, d)],
                          [jax.ShapeDtypeStruct((n, d), F32)], ("parallel",), (h, s, wd), comm=comm)
    return ho, landed


def ffn_fwd_loss(h, g, wg, wu, wd, g_final, tgt, lp, name):
    n, d = h.shape
    f = wg.shape[0]
    tm = _tile(lp, 528)
    per_seq = lp // tm
    fc = 2 * LANES if f % (2 * LANES) == 0 else f

    def body(h_ref, g_ref, wg_ref, wu_ref, wd_ref, gf_ref, t_ref, dh_ref, u_ref, a_ref, b_ref, loss_ref, dgf_ref):
        i = pl.program_id(0)
        y, u_ref[...] = _swiglu_half(h_ref[...], g_ref, wg_ref, wu_ref, wd_ref, a_ref, b_ref, fc)
        dh_ref[...], part, dg = _loss_and_grad(y, gf_ref[...], t_ref[...], (i % per_seq) * tm)
        _accumulate(loss_ref, jnp.broadcast_to(part, (1, LANES)), i == 0)
        _accumulate(dgf_ref, dg, i == 0)

    outs, _ = _call(
        body, name, (n // tm,),
        [_row(tm, d), _fixed((1, d)), VMEM_WHOLE, VMEM_WHOLE, VMEM_WHOLE, _fixed((1, d)), _row(tm, d)],
        [_row(tm, d), _row(tm, d), _row(tm, f), _row(tm, f), _fixed((1, LANES)), _fixed((1, d))],
        [jax.ShapeDtypeStruct((n, d), F32), jax.ShapeDtypeStruct((n, d), BF16),
         jax.ShapeDtypeStruct((n, f), BF16), jax.ShapeDtypeStruct((n, f), BF16),
         jax.ShapeDtypeStruct((1, LANES), F32), jax.ShapeDtypeStruct((1, d), F32)],
        ("arbitrary",), (h, g, wg, wu, wd, g_final, tgt))
    return outs


def ffn_bwd_act(dh, h, g, a, b, wg, wu, wd, name, comm=None, emit_sh=True):
    n, d = h.shape
    f = wg.shape[0]
    tm = _tile(n, 192)
    nc = _ff_chunks(f)
    fc = f // nc

    def body(dh_ref, h_ref, g_ref, a_ref, b_ref, wg_ref, wu_ref, wd_ref, dhi_ref, da_ref, db_ref, *rest):
        dg_ref = rest[-1]
        x = h_ref[...]
        dy = dh_ref[...]
        r = _rms_r(x)
        dhh = (0.5 * dy).astype(BF16)
        du = jnp.zeros((tm, d), F32)
        for c in range(nc):
            cols = slice(c * fc, (c + 1) * fc)
            ds = _nt(dhh, wd_ref[cols, :])
            av = a_ref[:, cols].astype(F32)
            bv = b_ref[:, cols].astype(F32)
            sg = _sig(av)
            sil = av * sg
            da = (ds * bv * (sg * (1.0 + av * (1.0 - sg)))).astype(BF16)
            db = (ds * sil).astype(BF16)
            da_ref[:, cols] = da
            db_ref[:, cols] = db
            if emit_sh:
                rest[0][:, cols] = (0.5 * sil * bv).astype(BF16)
            du = du + _nn(da, wg_ref[cols, :]) + _nn(db, wu_ref[cols, :])
        dx, dg = _rms_bwd(x, r, g_ref[...], du)
        dhi_ref[...] = dy + dx
        _accumulate(dg_ref, dg, pl.program_id(0) == 0)

    wide = [jax.ShapeDtypeStruct((n, f), BF16)] * (3 if emit_sh else 2)
    return _call(
        body, name, (n // tm,),
        [_row(tm, d), _row(tm, d), _fixed((1, d)), _row(tm, f), _row(tm, f), VMEM_WHOLE, VMEM_WHOLE, VMEM_WHOLE],
        [_row(tm, d)] + [_row(tm, f)] * len(wide) + [_fixed((1, d))],
        [jax.ShapeDtypeStruct((n, d), F32)] + wide + [jax.ShapeDtypeStruct((1, d), F32)],
        ("arbitrary",), (dh, h, g, a, b, wg, wu, wd), comm=comm)


def tn_matmul(x, y, name, out="f32", comm=None):
    n, k = x.shape
    m = y.shape[1]
    tm = _tile(n, TN_ROWS)
    kc, mc = k, (512 if m % 512 == 0 else m)
    while tm * kc * x.dtype.itemsize > TN_X_BYTES and kc % (2 * LANES) == 0:
        kc //= 2
    while tm * mc * y.dtype.itemsize > TN_Y_BYTES and mc % (2 * LANES) == 0:
        mc //= 2
    steps = n // tm

    def body(x_ref, y_ref, o_ref, *acc):
        i = pl.program_id(2)
        part = _tn(x_ref[...].astype(BF16), y_ref[...].astype(BF16))
        if steps == 1:
            o_ref[...] = part.astype(o_ref.dtype)
        elif out == "f32":
            _accumulate(o_ref, part, i == 0)
        else:
            _accumulate(acc[0], part, i == 0)

            @pl.when(i == steps - 1)
            def _():
                o_ref[...] = acc[0][...].astype(BF16)

    out_shape = jax.ShapeDtypeStruct((k, m), F32 if out == "f32" else BF16)
    (res,), landed = _call(
        body, name, (k // kc, m // mc, steps),
        [pl.BlockSpec((tm, kc), lambda a, b, i: (i, a)), pl.BlockSpec((tm, mc), lambda a, b, i: (i, b))],
        [pl.BlockSpec((kc, mc), lambda a, b, i: (a, b))], [out_shape], ("parallel", "parallel", "arbitrary"), (x, y),
        scratch=[pltpu.VMEM((kc, mc), F32)] if (out == "bf16" and steps > 1) else [], comm=comm)
    return (res, landed) if comm is not None else res


def inproj_fwd(h, g, wm, wl):
    n, d = h.shape
    tm = _tile(n, 352)

    def body(h_ref, g_ref, wm_ref, wl_ref, u_ref, zm_ref, zl_ref):
        x = h_ref[...]
        u = (x * _rms_r(x) * g_ref[...]).astype(BF16)
        u_ref[...] = u
        zm_ref[...] = _nt(u, wm_ref[...])
        zl_ref[...] = _nt(u, wl_ref[...])

    return pl.pallas_call(
        body, name="inproj_fwd", grid=(n // tm,),
        in_specs=[_row(tm, d), _fixed((1, d)), VMEM_WHOLE, VMEM_WHOLE],
        out_specs=[_row(tm, d), _row(tm, MLA_IN), _row(tm, 2 * LRU_WIDTH)],
        out_shape=[jax.ShapeDtypeStruct((n, d), BF16), jax.ShapeDtypeStruct((n, MLA_IN), F32),
                   jax.ShapeDtypeStruct((n, 2 * LRU_WIDTH), F32)],
        compiler_params=_params(("parallel",)),
    )(h, g, wm, wl)


def inproj_bwd(dzm, du, dgate, dh2, h, g, wm, wl):
    n, d = h.shape
    tm = _tile(n, 352)

    def body(dzm_ref, du_ref, dgt_ref, dh2_ref, h_ref, g_ref, wm_ref, wl_ref, dh_ref, dg_ref):
        x = h_ref[...]
        dun = (_nn(dzm_ref[...].astype(BF16), wm_ref[...])
               + _nn(du_ref[...].astype(BF16), wl_ref[:LRU_WIDTH, :])
               + _nn(dgt_ref[...].astype(BF16), wl_ref[LRU_WIDTH:, :]))
        dx, dg = _rms_bwd(x, _rms_r(x), g_ref[...], dun)
        dh_ref[...] = dh2_ref[...] + dx
        _accumulate(dg_ref, dg, pl.program_id(0) == 0)

    return pl.pallas_call(
        body, name="inproj_bwd", grid=(n // tm,),
        in_specs=[_row(tm, MLA_IN), _row(tm, LRU_WIDTH), _row(tm, LRU_WIDTH), _row(tm, d), _row(tm, d),
                  _fixed((1, d)), VMEM_WHOLE, VMEM_WHOLE],
        out_specs=[_row(tm, d), _fixed((1, d))],
        out_shape=[jax.ShapeDtypeStruct((n, d), F32), jax.ShapeDtypeStruct((1, d), F32)],
        compiler_params=_params(("arbitrary",)),
    )(dzm, du, dgate, dh2, h, g, wm, wl)


def _rope_tables(lp):
    pos = jnp.arange(lp, dtype=F32) - float(PAD)
    half = D_ROPE // 2
    inv_freq = ROPE_THETA ** (-jnp.arange(0, half, dtype=F32) / half)
    ang = pos[:, None] * inv_freq[None, :]
    cos, sin = jnp.cos(ang), jnp.sin(ang)
    one = jnp.ones((lp, D_NOPE), F32)
    z_nope = jnp.zeros((lp, D_NOPE), F32)
    z_half = jnp.zeros((lp, half), F32)
    z_tail = jnp.zeros((lp, HEAD_SLAB - D_QK), F32)
    cosr = jnp.concatenate([one, cos, cos, z_tail], axis=1)
    sin_up = jnp.concatenate([z_nope, z_half, sin, z_tail], axis=1)
    sin_dn = jnp.concatenate([z_nope, -sin, z_half, z_tail], axis=1)
    return cosr, sin_up, sin_dn


def _rope(x, cosr, sin_up, sin_dn):
    half = D_ROPE // 2
    return x * cosr + pltpu.roll(x, half, axis=1) * sin_up + pltpu.roll(x, HEAD_SLAB - half, axis=1) * sin_dn


def _rope_bwd(dy, cosr, sin_up, sin_dn):
    half = D_ROPE // 2
    return (dy * cosr + pltpu.roll(dy * sin_up, HEAD_SLAB - half, axis=1)
            + pltpu.roll(dy * sin_dn, half, axis=1))


def _k_rope_slab(zm_tile):
    tm = zm_tile.shape[0]
    krp = zm_tile[:, Q_RANK + KV_RANK:MLA_IN]
    return jnp.concatenate([jnp.zeros((tm, D_NOPE), F32), krp], axis=1)


def mla_prep_fwd(zm, gql, gkvl, wuq, wuk, wuv, gqh, gkh, tables, lp):
    n = zm.shape[0]
    tm = _tile(lp, 352)
    per_seq = lp // tm
    width = MLA_HEADS * HEAD_SLAB
    scale = 1.0 / math.sqrt(D_QK)

    def body(zm_ref, gql_ref, gkvl_ref, wuq_ref, wuk_ref, wuv_ref, gqh_ref, gkh_ref,
             cos_ref, up_ref, dn_ref, q_ref, k_ref, v_ref, qn_ref, cn_ref):
        z = zm_ref[...]
        cq = z[:, :Q_RANK]
        ckv = z[:, Q_RANK:Q_RANK + KV_RANK]
        qn = (cq * _rms_r(cq) * gql_ref[...]).astype(BF16)
        cn = (ckv * _rms_r(ckv) * gkvl_ref[...]).astype(BF16)
        qn_ref[...] = qn
        cn_ref[...] = cn
        q_raw = _nt(qn, wuq_ref[...])
        k_raw = _nt(cn, wuk_ref[...])
        v_ref[...] = _nt(cn, wuv_ref[...]).astype(BF16)
        kr_slab = _k_rope_slab(z)
        cosr, sin_up, sin_dn = cos_ref[...], up_ref[...], dn_ref[...]
        for hd in range(MLA_HEADS):
            cols = slice(hd * HEAD_SLAB, (hd + 1) * HEAD_SLAB)
            xq = q_raw[:, cols]
            yq = _rope(xq * _rms_r(xq, D_QK) * gqh_ref[...], cosr, sin_up, sin_dn)
            q_ref[:, cols] = (yq * scale).astype(BF16)
            xk = k_raw[:, cols] + kr_slab
            yk = _rope(xk * _rms_r(xk, D_QK) * gkh_ref[...], cosr, sin_up, sin_dn)
            k_ref[:, cols] = yk.astype(BF16)

    tab = pl.BlockSpec((tm, HEAD_SLAB), lambda i: (i % per_seq, 0))
    return pl.pallas_call(
        body, name="mla_prep_fwd", grid=(n // tm,),
        in_specs=[_row(tm, MLA_IN), _fixed((1, Q_RANK)), _fixed((1, KV_RANK)), VMEM_WHOLE, VMEM_WHOLE, VMEM_WHOLE,
                  _fixed((1, HEAD_SLAB)), _fixed((1, HEAD_SLAB)), tab, tab, tab],
        out_specs=[_row(tm, width), _row(tm, width), _row(tm, MLA_HEADS * D_V), _row(tm, Q_RANK), _row(tm, KV_RANK)],
        out_shape=[jax.ShapeDtypeStruct((n, width), BF16), jax.ShapeDtypeStruct((n, width), BF16),
                   jax.ShapeDtypeStruct((n, MLA_HEADS * D_V), BF16), jax.ShapeDtypeStruct((n, Q_RANK), BF16),
                   jax.ShapeDtypeStruct((n, KV_RANK), BF16)],
        compiler_params=_params(("parallel",)),
    )(zm, gql, gkvl, wuq, wuk, wuv, gqh, gkh, *tables)


def mla_prep_bwd(dq, dk, dv, zm, qn, cn, gql, gkvl, wuq, wuk, wuv, gqh, gkh, tables, lp, comm=None):
    n = zm.shape[0]
    tm = _tile(lp, 704)
    per_seq = lp // tm
    width = MLA_HEADS * HEAD_SLAB
    scale = 1.0 / math.sqrt(D_QK)

    def body(dq_ref, dk_ref, dv_ref, zm_ref, qn_ref, cn_ref, gql_ref, gkvl_ref, wuq_ref, wuk_ref, wuv_ref,
             gqh_ref, gkh_ref, cos_ref, up_ref, dn_ref,
             dzm_ref, dqr_ref, dkr_ref, dgql_ref, dgkvl_ref, dgqh_ref, dgkh_ref):
        z = zm_ref[...]
        cq = z[:, :Q_RANK]
        ckv = z[:, Q_RANK:Q_RANK + KV_RANK]
        q_raw = _nt(qn_ref[...], wuq_ref[...])
        k_raw = _nt(cn_ref[...], wuk_ref[...])
        kr_slab = _k_rope_slab(z)
        cosr, sin_up, sin_dn = cos_ref[...], up_ref[...], dn_ref[...]
        dgq = jnp.zeros((1, HEAD_SLAB), F32)
        dgk = jnp.zeros((1, HEAD_SLAB), F32)
        dkrp = jnp.zeros((tm, HEAD_SLAB - D_NOPE), F32)
        for hd in range(MLA_HEADS):
            cols = slice(hd * HEAD_SLAB, (hd + 1) * HEAD_SLAB)
            xq = q_raw[:, cols]
            dxn = _rope_bwd(dq_ref[:, cols] * scale, cosr, sin_up, sin_dn)
            dxq, dg = _rms_bwd(xq, _rms_r(xq, D_QK), gqh_ref[...], dxn, D_QK)
            dgq = dgq + dg
            dqr_ref[:, cols] = dxq.astype(BF16)
            xk = k_raw[:, cols] + kr_slab
            dxn = _rope_bwd(dk_ref[:, cols], cosr, sin_up, sin_dn)
            dxk, dg = _rms_bwd(xk, _rms_r(xk, D_QK), gkh_ref[...], dxn, D_QK)
            dgk = dgk + dg
            dkr_ref[:, cols] = dxk.astype(BF16)
            dkrp = dkrp + dxk[:, D_NOPE:]
        dqn = _nn(dqr_ref[...], wuq_ref[...])
        dcn = _nn(dkr_ref[...], wuk_ref[...]) + _nn(dv_ref[...].astype(BF16), wuv_ref[...])
        dcq, dg1 = _rms_bwd(cq, _rms_r(cq), gql_ref[...], dqn)
        dckv, dg2 = _rms_bwd(ckv, _rms_r(ckv), gkvl_ref[...], dcn)
        dzm_ref[:, :Q_RANK] = dcq
        dzm_ref[:, Q_RANK:Q_RANK + KV_RANK] = dckv
        dzm_ref[:, Q_RANK + KV_RANK:] = dkrp
        first = pl.program_id(0) == 0
        _accumulate(dgql_ref, dg1, first)
        _accumulate(dgkvl_ref, dg2, first)
        _accumulate(dgqh_ref, dgq, first)
        _accumulate(dgkh_ref, dgk, first)

    tab = pl.BlockSpec((tm, HEAD_SLAB), lambda i: (i % per_seq, 0))
    return _call(
        body, "mla_prep_bwd", (n // tm,),
        [_row(tm, width), _row(tm, width), _row(tm, MLA_HEADS * D_V), _row(tm, MLA_IN),
         _row(tm, Q_RANK), _row(tm, KV_RANK), _fixed((1, Q_RANK)), _fixed((1, KV_RANK)),
         VMEM_WHOLE, VMEM_WHOLE, VMEM_WHOLE, _fixed((1, HEAD_SLAB)), _fixed((1, HEAD_SLAB)), tab, tab, tab],
        [_row(tm, MLA_IN), _row(tm, width), _row(tm, width), _fixed((1, Q_RANK)), _fixed((1, KV_RANK)),
         _fixed((1, HEAD_SLAB)), _fixed((1, HEAD_SLAB))],
        [jax.ShapeDtypeStruct((n, MLA_IN), F32), jax.ShapeDtypeStruct((n, width), BF16),
         jax.ShapeDtypeStruct((n, width), BF16), jax.ShapeDtypeStruct((1, Q_RANK), F32),
         jax.ShapeDtypeStruct((1, KV_RANK), F32), jax.ShapeDtypeStruct((1, HEAD_SLAB), F32),
         jax.ShapeDtypeStruct((1, HEAD_SLAB), F32)],
        ("arbitrary",), (dq, dk, dv, zm, qn, cn, gql, gkvl, wuq, wuk, wuv, gqh, gkh, *tables), comm=comm)


def _attn_tile(lp):
    return _tile(lp, 704, CHUNK)


def _chunk_mask(i, j, t):
    qpos = i * t + lax.broadcasted_iota(jnp.int32, (t, t), 0)
    kpos = j * t + lax.broadcasted_iota(jnp.int32, (t, t), 1)
    same_or_earlier = jnp.right_shift(kpos, CHUNK_SHIFT) <= jnp.right_shift(qpos, CHUNK_SHIFT)
    return jnp.logical_and(same_or_earlier, kpos >= PAD)


def _masked_scores(s, i, j, t, diagonal):
    if diagonal:
        return jnp.where(_chunk_mask(i, j, t), s, NEG_INF)
    kpos = j * t + lax.broadcasted_iota(jnp.int32, (1, t), 1)
    return s + jnp.where(kpos < PAD, NEG_INF, 0.0)


def attn_fwd(q, k, v, nb, lp, comm=None):
    n = q.shape[0]
    t = _attn_tile(lp)
    nq = lp // t

    def body(q_ref, k_ref, v_ref, o_ref, lse_ref):
        i = pl.program_id(2)
        qv = q_ref[...]

        def kv_step(j, carry, diagonal=False):
            m, l, acc = carry
            off = pl.multiple_of(j * t, t)
            s = _masked_scores(_nt(qv, k_ref[pl.ds(off, t), :]), i, j, t, diagonal)
            m_new = jnp.maximum(m, jnp.max(s, axis=-1, keepdims=True))
            p = jnp.exp(s - m_new)
            alpha = jnp.exp(m - m_new)
            l = alpha * l + jnp.sum(p, axis=-1, keepdims=True)
            acc = alpha * acc + _nn(p.astype(BF16), v_ref[pl.ds(off, t), :])
            return m_new, l, acc

        init = (jnp.full((t, 1), NEG_INF, F32), jnp.zeros((t, 1), F32), jnp.zeros((t, D_V), F32))
        m, l, acc = kv_step(i, lax.fori_loop(0, i, kv_step, init), diagonal=True)
        o_ref[...] = acc * (1.0 / l)
        lse_ref[0] = jnp.broadcast_to(m + jnp.log(l), (t, LANES))

    return _call(
        body, "attn_fwd", (nb, MLA_HEADS, nq),
        [pl.BlockSpec((t, HEAD_SLAB), lambda b, h, i: (b * nq + i, h)),
         pl.BlockSpec((lp, HEAD_SLAB), lambda b, h, i: (b, h)),
         pl.BlockSpec((lp, D_V), lambda b, h, i: (b, h))],
        [pl.BlockSpec((t, D_V), lambda b, h, i: (b * nq + i, h)),
         pl.BlockSpec((1, t, LANES), lambda b, h, i: (h, b * nq + i, 0))],
        [jax.ShapeDtypeStruct((n, MLA_HEADS * D_V), F32), jax.ShapeDtypeStruct((MLA_HEADS, n, LANES), F32)],
        ("parallel", "parallel", "parallel"), (q, k, v), comm=comm)


def attn_bwd(q, k, v, o, do, lse, nb, lp, comm=None):
    n = q.shape[0]
    t = _attn_tile(lp)
    nq = lp // t

    def body(q_ref, k_ref, v_ref, o_ref, do_ref, lse_ref, dq_ref, dk_ref, dv_ref):
        dk_ref[...] = jnp.zeros_like(dk_ref)
        dv_ref[...] = jnp.zeros_like(dv_ref)

        def q_step(i, _):
            qoff = pl.multiple_of(i * t, t)
            qv = q_ref[pl.ds(qoff, t), :]
            dov = do_ref[pl.ds(qoff, t), :]
            delta = jnp.sum(o_ref[pl.ds(qoff, t), :] * dov, axis=-1, keepdims=True)
            lse_q = jnp.max(lse_ref[0, pl.ds(qoff, t), :], axis=-1, keepdims=True)
            do16 = dov.astype(BF16)

            def kv_step(j, dq_acc, diagonal=False):
                koff = pl.multiple_of(j * t, t)
                kv = k_ref[pl.ds(koff, t), :]
                s = _masked_scores(_nt(qv, kv), i, j, t, diagonal)
                p = jnp.exp(s - lse_q)
                dp = _nt(do16, v_ref[pl.ds(koff, t), :])
                ds16 = (p * (dp - delta)).astype(BF16)
                dv_ref[pl.ds(koff, t), :] += _tn(p.astype(BF16), do16)
                dk_ref[pl.ds(koff, t), :] += _tn(ds16, qv)
                return dq_acc + _nn(ds16, kv)

            earlier = lax.fori_loop(0, i, kv_step, jnp.zeros((t, HEAD_SLAB), F32))
            dq_ref[pl.ds(qoff, t), :] = kv_step(i, earlier, diagonal=True)
            return 0

        lax.fori_loop(0, nq, q_step, 0)

    wide = pl.BlockSpec((lp, HEAD_SLAB), lambda b, h: (b, h))
    thin = pl.BlockSpec((lp, D_V), lambda b, h: (b, h))
    width = MLA_HEADS * HEAD_SLAB
    return _call(
        body, "attn_bwd", (nb, MLA_HEADS),
        [wide, wide, thin, thin, thin, pl.BlockSpec((1, lp, LANES), lambda b, h: (h, b, 0))],
        [wide, wide, thin],
        [jax.ShapeDtypeStruct((n, width), F32), jax.ShapeDtypeStruct((n, width), F32),
         jax.ShapeDtypeStruct((n, MLA_HEADS * D_V), F32)],
        ("parallel", "parallel"), (q, k, v, o, do, lse), comm=comm)


def _seq_rows(nb, lp, width):
    rows = lax.broadcasted_iota(jnp.int32, (lp, width), 0)
    return jnp.concatenate([rows] * nb, axis=0) if nb > 1 else rows


def _lru_gates(u, w_ref, cb, wa, wx, ba, bx, lam):
    xc = (cb + w_ref[pl.ds(3, 1), :] * u + w_ref[pl.ds(2, 1), :] * pltpu.roll(u, 1, axis=0)
          + w_ref[pl.ds(1, 1), :] * pltpu.roll(u, 2, axis=0) + w_ref[pl.ds(0, 1), :] * pltpu.roll(u, 3, axis=0))
    xc16 = xc.astype(BF16)
    ra = _sig_tanh(_nn(xc16, wa) + ba)
    ia = _sig_tanh(_nn(xc16, wx) + bx)
    sp = _softplus(-lam)
    log_a = -C_RGLRU * ra * sp
    a = jnp.exp(log_a)
    x2 = 2.0 * log_a
    mult = jnp.sqrt(jnp.where(x2 > -1e-2, -x2 * (1.0 + x2 * (0.5 + x2 * (1.0 / 6.0))), 1.0 - a * a))
    return xc, xc16, ra, ia, sp, a, mult


def _scan_block_rows(width):
    return lax.broadcasted_iota(jnp.int32, (8, width), 0)


def lru_fwd(zl, conv_w, conv_b, wa, wx, ba, bx, lam, nb, lp, comm=None):
    n = zl.shape[0]
    w = LRU_TILE
    nt = LRU_WIDTH // w
    nblk = lp // 8

    def body(u_ref, gt_ref, cw_ref, cb_ref, wa_ref, wx_ref, ba_ref, bx_ref, lam_ref, y_ref, h_ref, a_s, b_s):
        u = u_ref[...]
        xc, _, _, ia, _, a, mult = _lru_gates(u, cw_ref, cb_ref[...], wa_ref[...], wx_ref[...],
                                              ba_ref[...], bx_ref[...], lam_ref[...])
        row = _seq_rows(nb, lp, w)
        mult = jnp.where(row == PAD, 1.0, mult)
        a_s[...] = a
        b_s[...] = jnp.where(row < PAD, 0.0, mult * (ia * xc))
        r8 = _scan_block_rows(w)

        def blk(i, carry):
            out = []
            for s_id in range(nb):
                off = pl.multiple_of(s_id * lp + i * 8, 8)
                av = a_s[pl.ds(off, 8), :]
                bv = b_s[pl.ds(off, 8), :]
                for sh in (1, 2, 4):
                    keep = r8 >= sh
                    bv = jnp.where(keep, av * pltpu.roll(bv, sh, axis=0) + bv, bv)
                    av = jnp.where(keep, av * pltpu.roll(av, sh, axis=0), av)
                hv = bv + av * carry[s_id]
                h_ref[pl.ds(off, 8), :] = hv
                out.append(jnp.sum(jnp.where(r8 == 7, hv, 0.0), axis=0, keepdims=True))
            return tuple(out)

        lax.fori_loop(0, nblk, blk, tuple(jnp.zeros((1, w), F32) for _ in range(nb)))
        gelu, _ = _gelu_and_grad(gt_ref[...])
        y_ref[...] = h_ref[...] * gelu

    col = lambda c: (0, c)
    return _call(
        body, "lru_fwd", (nt,),
        [pl.BlockSpec((n, w), col), pl.BlockSpec((n, w), lambda c: (0, nt + c)),
         pl.BlockSpec((CONV_W, w), col), pl.BlockSpec((1, w), col),
         pl.BlockSpec((w, w), lambda c: (c, c)), pl.BlockSpec((w, w), lambda c: (c, c)),
         pl.BlockSpec((1, w), col), pl.BlockSpec((1, w), col), pl.BlockSpec((1, w), col)],
        [pl.BlockSpec((n, w), col), pl.BlockSpec((n, w), col)],
        [jax.ShapeDtypeStruct((n, LRU_WIDTH), F32), jax.ShapeDtypeStruct((n, LRU_WIDTH), F32)],
        ("parallel",), (zl, zl, conv_w, conv_b, wa, wx, ba, bx, lam),
        scratch=[pltpu.VMEM((n, w), F32), pltpu.VMEM((n, w), F32)], comm=comm)


def lru_bwd(zl, hs, dy, conv_w, conv_b, wa, wx, ba, bx, lam, nb, lp, comm=None):
    n = zl.shape[0]
    w = LRU_TILE
    nt = LRU_WIDTH // w
    nblk = lp // 8

    def body(u_ref, gt_ref, h_ref, dy_ref, cw_ref, cb_ref, wa_ref, wx_ref, ba_ref, bx_ref, lam_ref,
             du_ref, dgt_ref, dcw_ref, dcb_ref, dba_ref, dbx_ref, dlam_ref, dwa_ref, dwx_ref,
             c_s, d_s, g_s, dwa_s, dwx_s):
        u = u_ref[...]
        lam = lam_ref[...]
        xc, xc16, ra, ia, sp, a, mult = _lru_gates(u, cw_ref, cb_ref[...], wa_ref[...], wx_ref[...],
                                                   ba_ref[...], bx_ref[...], lam)
        row = lax.broadcasted_iota(jnp.int32, (lp, w), 0)
        hv = h_ref[...]
        dyv = dy_ref[...]
        gelu, dgelu = _gelu_and_grad(gt_ref[...])
        dgt_ref[...] = jnp.where(row >= PAD, dyv * hv * dgelu, 0.0)
        c_s[...] = pltpu.roll(a, lp - 1, axis=0)
        d_s[...] = dyv * gelu
        r8 = _scan_block_rows(w)

        def blk(ii, carry):
            off = pl.multiple_of((nblk - 1 - ii) * 8, 8)
            cv = c_s[pl.ds(off, 8), :]
            dv = d_s[pl.ds(off, 8), :]
            for sh in (1, 2, 4):
                keep = r8 < 8 - sh
                dv = jnp.where(keep, cv * pltpu.roll(dv, 8 - sh, axis=0) + dv, dv)
                cv = jnp.where(keep, cv * pltpu.roll(cv, 8 - sh, axis=0), cv)
            gv = dv + cv * carry
            g_s[pl.ds(off, 8), :] = gv
            return jnp.sum(jnp.where(r8 == 0, gv, 0.0), axis=0, keepdims=True)

        lax.fori_loop(0, nblk, blk, jnp.zeros((1, w), F32))
        gv = g_s[...]
        first_row = row == PAD
        db = jnp.where(row >= PAD, gv, 0.0)
        da = jnp.where(row > PAD, gv * pltpu.roll(hv, 1, axis=0), 0.0)
        mult_eff = jnp.where(first_row, 1.0, mult)
        dmult = jnp.where(first_row, 0.0, db * (ia * xc))
        dia = db * mult_eff * xc
        dxc = db * mult_eff * ia
        dla = da * a - dmult * (a * a) / mult
        dra = dla * (-C_RGLRU * sp)
        dsp = jnp.sum(dla * (-C_RGLRU * ra), axis=0, keepdims=True)
        dpa = dra * ra * (1.0 - ra)
        dpx = dia * ia * (1.0 - ia)
        dpa16 = dpa.astype(BF16)
        dpx16 = dpx.astype(BF16)
        dxc = dxc + _nt(dpa16, wa_ref[...]) + _nt(dpx16, wx_ref[...])
        du = cw_ref[pl.ds(CONV_W - 1, 1), :] * dxc
        dcw = [jnp.sum(dxc * u, axis=0, keepdims=True)]
        for tap in range(1, CONV_W):
            dcw.insert(0, jnp.sum(dxc * pltpu.roll(u, tap, axis=0), axis=0, keepdims=True))
            du = du + cw_ref[pl.ds(CONV_W - 1 - tap, 1), :] * pltpu.roll(dxc, lp - tap, axis=0)
        du_ref[...] = jnp.where(row >= PAD, du, 0.0)
        first = pl.program_id(1) == 0
        _accumulate(dlam_ref, -_sig(-lam) * dsp, first)
        _accumulate(dba_ref, jnp.sum(dpa, axis=0, keepdims=True), first)
        _accumulate(dbx_ref, jnp.sum(dpx, axis=0, keepdims=True), first)
        _accumulate(dcb_ref, jnp.sum(dxc, axis=0, keepdims=True), first)
        _accumulate(dcw_ref, jnp.concatenate(dcw, axis=0), first)
        _accumulate(dwa_s, _tn(xc16, dpa16), first)
        _accumulate(dwx_s, _tn(xc16, dpx16), first)

        @pl.when(pl.program_id(1) == nb - 1)
        def _():
            for j in range(w // LRU_BLOCK):
                blk_rows = slice(j * LRU_BLOCK, (j + 1) * LRU_BLOCK)
                dwa_ref[0, blk_rows, :] = dwa_s[blk_rows, blk_rows]
                dwx_ref[0, blk_rows, :] = dwx_s[blk_rows, blk_rows]

    col = lambda c, b: (0, c)
    vec = pl.BlockSpec((1, w), col)
    mat = pl.BlockSpec((w, w), lambda c, b: (c, c))
    big = pl.BlockSpec((lp, w), lambda c, b: (b, c))
    dmat = pl.BlockSpec((1, w, LRU_BLOCK), lambda c, b: (c, 0, 0))
    return _call(
        body, "lru_bwd", (nt, nb),
        [big, pl.BlockSpec((lp, w), lambda c, b: (b, nt + c)), big, big,
         pl.BlockSpec((CONV_W, w), col), vec, mat, mat, vec, vec, vec],
        [big, big, pl.BlockSpec((CONV_W, w), col), vec, vec, vec, vec, dmat, dmat],
        [jax.ShapeDtypeStruct((n, LRU_WIDTH), F32), jax.ShapeDtypeStruct((n, LRU_WIDTH), F32),
         jax.ShapeDtypeStruct((CONV_W, LRU_WIDTH), F32), jax.ShapeDtypeStruct((1, LRU_WIDTH), F32),
         jax.ShapeDtypeStruct((1, LRU_WIDTH), F32), jax.ShapeDtypeStruct((1, LRU_WIDTH), F32),
         jax.ShapeDtypeStruct((1, LRU_WIDTH), F32), jax.ShapeDtypeStruct((nt, w, LRU_BLOCK), F32),
         jax.ShapeDtypeStruct((nt, w, LRU_BLOCK), F32)],
        ("parallel", "arbitrary"), (zl, zl, hs, dy, conv_w, conv_b, wa, wx, ba, bx, lam),
        scratch=[pltpu.VMEM((lp, w), F32), pltpu.VMEM((lp, w), F32), pltpu.VMEM((lp, w), F32),
                 pltpu.VMEM((w, w), F32), pltpu.VMEM((w, w), F32)], comm=comm)


def outproj_fwd(h, ya, yl, gao, glo, wout):
    n, d = h.shape
    half = ya.shape[1]
    tm = _tile(n, 704)

    def body(h_ref, ya_ref, yl_ref, gao_ref, glo_ref, w_ref, ho_ref, yn_ref):
        xa = ya_ref[...]
        xl = yl_ref[...]
        na = (xa * _rms_r(xa) * gao_ref[...]).astype(BF16)
        nl = (xl * _rms_r(xl) * glo_ref[...]).astype(BF16)
        yn_ref[:, :half] = na
        yn_ref[:, half:] = nl
        ho_ref[...] = h_ref[...] + _nn(na, w_ref[:half, :]) + _nn(nl, w_ref[half:, :])

    return pl.pallas_call(
        body, name="outproj_fwd", grid=(n // tm,),
        in_specs=[_row(tm, d), _row(tm, half), _row(tm, half), _fixed((1, half)), _fixed((1, half)), VMEM_WHOLE],
        out_specs=[_row(tm, d), _row(tm, 2 * half)],
        out_shape=[jax.ShapeDtypeStruct((n, d), F32), jax.ShapeDtypeStruct((n, 2 * half), BF16)],
        compiler_params=_params(("parallel",)),
    )(h, ya, yl, gao, glo, wout)


def outproj_bwd(dh, ya, yl, gao, glo, wout):
    n, d = dh.shape
    half = ya.shape[1]
    tm = _tile(n, 704)

    def body(dh_ref, ya_ref, yl_ref, gao_ref, glo_ref, w_ref, dya_ref, dyl_ref, dgao_ref, dglo_ref):
        d16 = dh_ref[...].astype(BF16)
        xa = ya_ref[...]
        xl = yl_ref[...]
        dxa, dga = _rms_bwd(xa, _rms_r(xa), gao_ref[...], _nt(d16, w_ref[:half, :]))
        dxl, dgl = _rms_bwd(xl, _rms_r(xl), glo_ref[...], _nt(d16, w_ref[half:, :]))
        dya_ref[...] = dxa
        dyl_ref[...] = dxl
        first = pl.program_id(0) == 0
        _accumulate(dgao_ref, dga, first)
        _accumulate(dglo_ref, dgl, first)

    return pl.pallas_call(
        body, name="outproj_bwd", grid=(n // tm,),
        in_specs=[_row(tm, d), _row(tm, half), _row(tm, half), _fixed((1, half)), _fixed((1, half)), VMEM_WHOLE],
        out_specs=[_row(tm, half), _row(tm, half), _fixed((1, half)), _fixed((1, half))],
        out_shape=[jax.ShapeDtypeStruct((n, half), F32), jax.ShapeDtypeStruct((n, half), F32),
                   jax.ShapeDtypeStruct((1, half), F32), jax.ShapeDtypeStruct((1, half), F32)],
        compiler_params=_params(("arbitrary",)),
    )(dh, ya, yl, gao, glo, wout)


def _loss_and_grad(x, gv, tgt, first_row):
    tm, d = x.shape
    r = _rms_r(x)
    row = first_row + lax.broadcasted_iota(jnp.int32, (tm, d), 0)
    diff = jnp.where(row >= FIRST_FRAME, x * r * gv - tgt, 0.0)
    part = 0.5 * jnp.sum(jnp.sum(diff * diff, axis=-1, keepdims=True) * (1.0 / d), axis=0, keepdims=True)
    dx, dg = _rms_bwd(x, r, gv, diff * (1.0 / d))
    return dx, part, dg


def assemble_cols(g, name):
    _, k, ns = g.shape

    def body(g_ref, o_ref):
        for j in range(N_DEV):
            o_ref[:, j * ns:(j + 1) * ns] = g_ref[j]

    return pl.pallas_call(body, name=name, out_shape=jax.ShapeDtypeStruct((k, N_DEV * ns), g.dtype),
                          compiler_params=_params(None))(g)


def split_cols(x, name):
    k, cols = x.shape
    ns = cols // N_DEV

    def body(x_ref, o_ref):
        for j in range(N_DEV):
            o_ref[j] = x_ref[:, j * ns:(j + 1) * ns]

    return pl.pallas_call(body, name=name, out_shape=jax.ShapeDtypeStruct((N_DEV, k, ns), x.dtype),
                          compiler_params=_params(None))(x)


def _slab_rows(w, per_head):
    k = w.shape[1]
    w = w.reshape(MLA_HEADS, per_head, k)
    return jnp.pad(w, ((0, 0), (0, HEAD_SLAB - per_head), (0, 0))).reshape(MLA_HEADS * HEAD_SLAB, k)


def _unslab_rows(w, per_head):
    k = w.shape[1]
    return w.reshape(MLA_HEADS, HEAD_SLAB, k)[:, :per_head].reshape(MLA_HEADS * per_head, k)


def meta_grad(dh0, nb, lp):
    d = dh0.shape[1]
    ns = d // N_DEV
    per_seq = lp // N_META

    def body(x_ref, o_ref):
        x = x_ref[...]
        for j in range(N_DEV):
            _accumulate(o_ref.at[j], x[:, j * ns:(j + 1) * ns], pl.program_id(0) == 0)

    return pl.pallas_call(
        body, name="meta_grad", grid=(nb,),
        in_specs=[pl.BlockSpec((N_META, d), lambda b: (b * per_seq + PAD // N_META, 0))],
        out_specs=pl.BlockSpec((N_DEV, N_META, ns), lambda b: (0, 0, 0)),
        out_shape=jax.ShapeDtypeStruct((N_DEV, N_META, ns), F32),
        compiler_params=_params(("arbitrary",)))(dh0)


VECTORS = [("ffn1_norm", 1024), ("mix_norm", 1024), ("q_latent_norm", 384), ("kv_latent_norm", 256),
           ("q_head_norm", 192), ("k_head_norm", 192), ("conv_b", 512), ("gate_a_b", 512), ("gate_x_b", 512),
           ("lru_lambda", 512), ("attn_out_norm", 512), ("lru_out_norm", 512), ("ffn2_norm", 1024),
           ("final_norm", 1024)]
VEC_ROWS = 16
LOSS_ROW = len(VECTORS)
GATES = ["gate_a_w", "gate_x_w"]


def pack_vectors(grads, loss):
    def body(*refs):
        o_ref = refs[-1]
        o_ref[...] = jnp.zeros_like(o_ref)
        for t, (ref, (_, cnt)) in enumerate(zip(refs[:-2], VECTORS)):
            o_ref[t:t + 1, :cnt] = ref[:, :cnt]
        o_ref[LOSS_ROW:LOSS_ROW + 1, :LANES] = refs[-2][...]

    return pl.pallas_call(body, name="pack_vectors", out_shape=jax.ShapeDtypeStruct((VEC_ROWS, D_MODEL), F32),
                          compiler_params=_params(None))(*[grads[name] for name, _ in VECTORS], loss)


def _adamw_update(w, g, m, v):
    c1 = 1.0 / (1.0 - ADAM_B1 ** ADAM_STEP)
    c2 = 1.0 / (1.0 - ADAM_B2 ** ADAM_STEP)
    mn = ADAM_B1 * m + (1.0 - ADAM_B1) * g
    vn = ADAM_B2 * v + (1.0 - ADAM_B2) * (g * g)
    delta = -ADAM_LR * ((mn * c1) / (jnp.sqrt(vn * c2) + ADAM_EPS) + ADAM_WD * w)
    return delta, mn, vn


def _sum_slots(ref, index=()):
    acc = ref[(0,) + index].astype(F32)
    for s in range(1, N_DEV):
        acc = acc + ref[(s,) + index].astype(F32)
    return acc


def adamw_sharded(r, w, m, v, name):
    rows, cols = w.shape
    tr = _tile(rows, 256, 16) if rows % 16 == 0 else rows

    def body(r_ref, w_ref, m_ref, v_ref, g_ref, d_ref, mo_ref, vo_ref):
        g = _sum_slots(r_ref)
        g_ref[...] = g
        d_ref[...], mo_ref[...], vo_ref[...] = _adamw_update(w_ref[...], g, m_ref[...], v_ref[...])

    spec = pl.BlockSpec((tr, cols), lambda i: (i, 0))
    shape = jax.ShapeDtypeStruct((rows, cols), F32)
    return pl.pallas_call(
        body, name=name, grid=(rows // tr,),
        in_specs=[pl.BlockSpec((N_DEV, tr, cols), lambda i: (0, i, 0))] + [spec] * 3,
        out_specs=[spec] * 4, out_shape=[shape] * 4,
        compiler_params=_params(("parallel",)),
    )(r, w, m, v)


def adamw_small(r_vec, r_gates, w, m, v):
    nt = len(VECTORS) + len(GATES)

    def body(*refs):
        rv_ref = refs[0]
        rg_refs = refs[1:1 + len(GATES)]
        base = 1 + len(GATES)
        w_refs, m_refs, v_refs = (refs[base + i * nt:base + (i + 1) * nt] for i in range(3))
        outs = refs[base + 3 * nt:]
        g_o, d_o, m_o, v_o = (outs[i * nt:(i + 1) * nt] for i in range(4))
        outs[4 * nt][...] = _sum_slots(rv_ref, (slice(LOSS_ROW, LOSS_ROW + 1), slice(0, LANES)))
        for t in range(nt):
            if t < len(VECTORS):
                cnt = VECTORS[t][1]
                g = _sum_slots(rv_ref, (slice(t, t + 1), slice(0, cnt)))
            else:
                g = _sum_slots(rg_refs[t - len(VECTORS)])
            g_o[t][...] = g
            d_o[t][...], m_o[t][...], v_o[t][...] = _adamw_update(w_refs[t][...], g, m_refs[t][...], v_refs[t][...])

    shapes = [jax.ShapeDtypeStruct(a.shape, F32) for a in w]
    res = pl.pallas_call(body, name="adamw_small", out_shape=shapes * 4 + [jax.ShapeDtypeStruct((1, LANES), F32)],
                         compiler_params=_params(None))(r_vec, *r_gates, *w, *m, *v)
    return [res[i * nt:(i + 1) * nt] for i in range(4)], res[4 * nt]


def _block_diag(w):
    nb, n, _ = w.shape
    eye = jnp.eye(nb, dtype=w.dtype)
    return (eye[:, None, :, None] * w[:, :, None, :]).reshape(nb * n, nb * n)


def _two_d(a):
    if a.ndim == 3:
        return a.reshape(a.shape[1], a.shape[2])
    if a.ndim == 4:
        return a.reshape(a.shape[1] * a.shape[2], a.shape[3])
    return a


_WEIGHT_NAMES = ['meta_tokens', 'ffn1_norm', 'ffn1_w_gate', 'ffn1_w_up', 'ffn1_w_down', 'mix_norm', 'w_in',
                 'q_latent_norm', 'w_uq', 'kv_latent_norm', 'w_uk', 'w_uv', 'q_head_norm', 'k_head_norm', 'conv_w',
                 'conv_b', 'gate_a_w', 'gate_a_b', 'gate_x_w', 'gate_x_b', 'lru_lambda', 'attn_out_norm',
                 'lru_out_norm', 'w_out', 'ffn2_norm', 'ffn2_w_gate', 'ffn2_w_up', 'ffn2_w_down', 'final_norm']


COLUMN_SHARDED = ("ffn1_w_gate", "ffn1_w_up", "ffn2_w_gate", "ffn2_w_up", "w_in", "w_uq", "w_uk", "w_uv")


def train_step(x, tgt, w, m, v):
    nb, seq, d = x.shape
    lp = PAD + N_META + seq
    n = nb * lp
    def local(a, name):
        a = _two_d(a)
        return a.T if name in COLUMN_SHARDED else a

    sh = {name: local(w[name], name) for name in _WEIGHT_NAMES}
    m2 = {name: local(m[name], name) for name in _WEIGHT_NAMES}
    v2 = {name: local(v[name], name) for name in _WEIGHT_NAMES}

    def b16(name):
        return sh[name].astype(BF16)

    out = {}

    def update(name, landed):
        out[name] = adamw_sharded(landed, sh[name], m2[name], v2[name], "adamw_" + name)

    g_wg1, g_wu1, g_meta, g_conv = exchange(
        [b16("ffn1_w_gate"), b16("ffn1_w_up"), sh["meta_tokens"], sh["conv_w"]], ["gather"] * 4, "gather_ffn1")
    wg1, wu1 = g_wg1.reshape(D_FF, d), g_wu1.reshape(D_FF, d)
    meta = assemble_cols(g_meta, "assemble_meta")
    conv_w = assemble_cols(g_conv, "assemble_conv")

    front = jnp.concatenate([jnp.zeros((PAD, d), F32), meta], axis=0)
    h0 = jnp.concatenate([jnp.broadcast_to(front[None], (nb, FIRST_FRAME, d)), x], axis=1).reshape(n, d)
    tgt_p = jnp.concatenate([jnp.zeros((nb, FIRST_FRAME, d), F32), tgt], axis=1).reshape(n, d)
    tables = _rope_tables(lp)
    zero_tail = jnp.zeros((1, HEAD_SLAB - D_QK), F32)
    gqh = jnp.concatenate([sh["q_head_norm"], zero_tail], axis=1)
    gkh = jnp.concatenate([sh["k_head_norm"], zero_tail], axis=1)
    wa = _block_diag(w["gate_a_w"][0]).astype(BF16)
    wx = _block_diag(w["gate_x_w"][0]).astype(BF16)

    (u1, a1, b1, s1), (g_wd1, g_in) = ffn_up(h0, sh["ffn1_norm"], wg1, wu1, "ffn1_up",
                                             comm=([b16("ffn1_w_down"), b16("w_in")], ["gather"] * 2))
    wd1 = g_wd1.reshape(D_FF, d)
    h1, (g_uq, g_uk, g_uv, g_out) = ffn_down(
        h0, s1, wd1, "ffn1_down", comm=([b16("w_uq"), b16("w_uk"), b16("w_uv"), b16("w_out")], ["gather"] * 4))
    mla_rows = MLA_IN - D_ROPE
    w_in = g_in.reshape(mla_rows + 2 * LRU_WIDTH, d)
    wm = jnp.concatenate([w_in[:mla_rows], jnp.zeros((D_ROPE, d), BF16)], axis=0)
    wl = w_in[mla_rows:]
    wuq = _slab_rows(g_uq.reshape(MLA_HEADS * D_QK, Q_RANK), D_QK)
    wuk = _slab_rows(g_uk.reshape(MLA_HEADS * D_NOPE, KV_RANK), D_NOPE)
    wuv = g_uv.reshape(MLA_HEADS * D_V, KV_RANK)
    w_out = g_out.reshape(d, d)

    u2, zm, zl = inproj_fwd(h1, sh["mix_norm"], wm, wl)
    q, k, vv, qn, cn = mla_prep_fwd(zm, sh["q_latent_norm"], sh["kv_latent_norm"], wuq, wuk, wuv, gqh, gkh, tables, lp)
    (y_mla, lse), (g_wu2, g_wd2) = attn_fwd(
        q, k, vv, nb, lp, comm=([b16("ffn2_w_up"), b16("ffn2_w_down")], ["gather"] * 2))
    (y_lru, hs), (g_wg2,) = lru_fwd(zl, conv_w, sh["conv_b"], wa, wx, sh["gate_a_b"], sh["gate_x_b"], sh["lru_lambda"],
                                    nb, lp, comm=([b16("ffn2_w_gate")], ["gather"]))
    wg2, wu2, wd2 = (g.reshape(D_FF, d) for g in (g_wg2, g_wu2, g_wd2))
    h2, yn = outproj_fwd(h1, y_mla, y_lru, sh["attn_out_norm"], sh["lru_out_norm"], w_out)
    dh3, u3, a3, b3, loss, g_final = ffn_fwd_loss(h2, sh["ffn2_norm"], wg2, wu2, wd2, sh["final_norm"], tgt_p, lp,
                                                  "ffn2_fwd_loss")

    vec = {"final_norm": g_final}
    (dh2, da3, db3, sh3, vec["ffn2_norm"]), _ = ffn_bwd_act(dh3, h2, sh["ffn2_norm"], a3, b3, wg2, wu2, wd2, "ffn2_bwd")
    ff_shards = (N_DEV, D_FF // N_DEV, d)
    dwg2 = tn_matmul(da3, u3, "ffn2_dwg", "bf16").reshape(ff_shards)
    dwu2 = tn_matmul(db3, u3, "ffn2_dwu", "bf16").reshape(ff_shards)
    dwd2 = tn_matmul(sh3, dh3, "ffn2_dwd", "bf16").reshape(ff_shards)

    dy_mla, dy_lru, vec["attn_out_norm"], vec["lru_out_norm"] = outproj_bwd(
        dh2, y_mla, y_lru, sh["attn_out_norm"], sh["lru_out_norm"], w_out)
    dw_out = tn_matmul(yn, dh2, "dw_out", "bf16").reshape(N_DEV, d // N_DEV, d)
    (du, dgate, dconv, vec["conv_b"], vec["gate_a_b"], vec["gate_x_b"], vec["lru_lambda"], dga, dgx), (r_wg2,) = lru_bwd(
        zl, hs, dy_lru, conv_w, sh["conv_b"], wa, wx, sh["gate_a_b"], sh["gate_x_b"], sh["lru_lambda"], nb, lp,
        comm=([dwg2], ["scatter"]))
    update("ffn2_w_gate", r_wg2)

    (dq, dk, dv), (r_wu2,) = attn_bwd(q, k, vv, y_mla, dy_mla, lse, nb, lp, comm=([dwu2], ["scatter"]))
    update("ffn2_w_up", r_wu2)

    (dzm, dqr, dkr, vec["q_latent_norm"], vec["kv_latent_norm"], vec["q_head_norm"], vec["k_head_norm"]), (r_wd2,) = (
        mla_prep_bwd(dq, dk, dv, zm, qn, cn, sh["q_latent_norm"], sh["kv_latent_norm"], wuq, wuk, wuv, gqh, gkh,
                     tables, lp, comm=([dwd2], ["scatter"])))
    update("ffn2_w_down", r_wd2)
    dwuq = _unslab_rows(tn_matmul(dqr, qn, "dw_uq"), D_QK).reshape(N_DEV, -1, Q_RANK)
    dwuk = _unslab_rows(tn_matmul(dkr, cn, "dw_uk"), D_NOPE).reshape(N_DEV, -1, KV_RANK)
    dwuv = tn_matmul(dv, cn, "dw_uv").reshape(N_DEV, -1, KV_RANK)
    dh1, vec["mix_norm"] = inproj_bwd(dzm, du, dgate, dh2, h1, sh["mix_norm"], wm, wl)
    dw_in = jnp.concatenate([tn_matmul(dzm, u2, "dw_in_mla")[:mla_rows], tn_matmul(du, u2, "dw_in_u"),
                             tn_matmul(dgate, u2, "dw_in_gate")], axis=0).reshape(N_DEV, -1, d)

    (dh0, da1, db1, vec["ffn1_norm"]), landed = ffn_bwd_act(
        dh1, h0, sh["ffn1_norm"], a1, b1, wg1, wu1, wd1, "ffn1_bwd", emit_sh=False,
        comm=([dw_in, dwuq, dwuk, dwuv, dw_out, split_cols(dconv, "split_conv")], ["scatter"] * 6))
    for name, r in zip(("w_in", "w_uq", "w_uk", "w_uv", "w_out", "conv_w"), landed):
        update(name, r)

    dwg1 = tn_matmul(da1, u1, "ffn1_dwg", "bf16").reshape(ff_shards)
    dwu1, (r_wg1,) = tn_matmul(db1, u1, "ffn1_dwu", "bf16", comm=([dwg1], ["scatter"]))
    dwd1, (r_wu1,) = tn_matmul(s1, dh1, "ffn1_dwd", "bf16", comm=([dwu1.reshape(ff_shards)], ["scatter"]))
    dwd1 = dwd1.reshape(ff_shards)
    dmeta = meta_grad(dh0, nb, lp)
    gates = [dga.reshape(LRU_WIDTH, LRU_BLOCK), dgx.reshape(LRU_WIDTH, LRU_BLOCK)]
    r_wd1, r_meta, r_vec, r_ga, r_gx = exchange(
        [dwd1, dmeta, pack_vectors(vec, loss)] + gates, ["scatter"] * 2 + ["gather"] * 3, "exchange_last")
    update("ffn1_w_gate", r_wg1)
    update("ffn1_w_up", r_wu1)
    update("ffn1_w_down", r_wd1)
    update("meta_tokens", r_meta)

    small = [name for name, _ in VECTORS] + GATES
    res, total_loss = adamw_small(r_vec, [r_ga, r_gx], [sh[nm] for nm in small], [m2[nm] for nm in small],
                                  [v2[nm] for nm in small])
    for i, name in enumerate(small):
        out[name] = [res[j][i] for j in range(4)]

    grad_x = dh0.reshape(nb, lp, d)[:, FIRST_FRAME:]
    loss = total_loss[0, 0]

    def as_given(a, name):
        return (a.T if name in COLUMN_SHARDED else a).reshape(w[name].shape)

    cols = [[as_given(out[name][j], name) for name in _WEIGHT_NAMES] for j in range(4)]
    return (loss, grad_x, *cols[0], *cols[1], *cols[2], *cols[3])


def kernel(x, meta_tokens, ffn1_norm, ffn1_w_gate, ffn1_w_up, ffn1_w_down, mix_norm, w_in, q_latent_norm, w_uq, kv_latent_norm, w_uk, w_uv, q_head_norm, k_head_norm, conv_w, conv_b, gate_a_w, gate_a_b, gate_x_w, gate_x_b, lru_lambda, attn_out_norm, lru_out_norm, w_out, ffn2_norm, ffn2_w_gate, ffn2_w_up, ffn2_w_down, final_norm, loss_target, m_meta_tokens, m_ffn1_norm, m_ffn1_w_gate, m_ffn1_w_up, m_ffn1_w_down, m_mix_norm, m_w_in, m_q_latent_norm, m_w_uq, m_kv_latent_norm, m_w_uk, m_w_uv, m_q_head_norm, m_k_head_norm, m_conv_w, m_conv_b, m_gate_a_w, m_gate_a_b, m_gate_x_w, m_gate_x_b, m_lru_lambda, m_attn_out_norm, m_lru_out_norm, m_w_out, m_ffn2_norm, m_ffn2_w_gate, m_ffn2_w_up, m_ffn2_w_down, m_final_norm, v_meta_tokens, v_ffn1_norm, v_ffn1_w_gate, v_ffn1_w_up, v_ffn1_w_down, v_mix_norm, v_w_in, v_q_latent_norm, v_w_uq, v_kv_latent_norm, v_w_uk, v_w_uv, v_q_head_norm, v_k_head_norm, v_conv_w, v_conv_b, v_gate_a_w, v_gate_a_b, v_gate_x_w, v_gate_x_b, v_lru_lambda, v_attn_out_norm, v_lru_out_norm, v_w_out, v_ffn2_norm, v_ffn2_w_gate, v_ffn2_w_up, v_ffn2_w_down, v_final_norm):
    args = locals()
    w = {name: args[name] for name in _WEIGHT_NAMES}
    m = {name: args["m_" + name] for name in _WEIGHT_NAMES}
    v = {name: args["v_" + name] for name in _WEIGHT_NAMES}
    return train_step(x, loss_target, w, m, v)
```

```python
import math

import jax
import jax.numpy as jnp
from jax import lax
from jax.experimental import pallas as pl
from jax.experimental.pallas import tpu as pltpu

F32 = jnp.float32
BF16 = jnp.bfloat16

D_MODEL = 1024
CHUNK = 64
CHUNK_SHIFT = 6
N_META = 16
PAD = CHUNK - N_META
FIRST_FRAME = PAD + N_META
MLA_HEADS = 4
D_NOPE = 128
D_ROPE = 64
D_QK = D_NOPE + D_ROPE
D_V = 128
HEAD_SLAB = 256
KV_RANK = 256
Q_RANK = 384
ROPE_THETA = 10000.0
LRU_WIDTH = 512
LRU_BLOCKS = 8
LRU_BLOCK = 64
LRU_TILE = 128
CONV_W = 4
C_RGLRU = 8.0
D_FF = 2816
MLA_IN = 768
EPS = 1e-6
NEG_INF = -1e30
N_DEV = 8
LANES = 128
VMEM_LIMIT = 52 * 1024 * 1024
ATTN_HEADS_PER_STEP = 2
TN_ROWS = 4224
TN_X_BYTES = 12 * 1024 * 1024
TN_Y_BYTES = 9 * 1024 * 1024 // 2

ADAM_LR = 0.001
ADAM_B1 = 0.9
ADAM_B2 = 0.999
ADAM_EPS = 1e-08
ADAM_WD = 0.01
ADAM_STEP = 10

VMEM_WHOLE = pl.BlockSpec(memory_space=pltpu.VMEM)
HBM_WHOLE = pl.BlockSpec(memory_space=pl.ANY)


def _params(sems):
    if sems is None:
        return pltpu.CompilerParams(vmem_limit_bytes=VMEM_LIMIT)
    return pltpu.CompilerParams(dimension_semantics=sems, vmem_limit_bytes=VMEM_LIMIT)


def _tile(n, cap, mult=16):
    best = None
    for t in range(mult, min(n, cap) + 1, mult):
        if n % t == 0:
            best = t
    assert best is not None, (n, cap, mult)
    return best


def _row(tm, d):
    return pl.BlockSpec((tm, d), lambda i: (i, 0))


def _fixed(shape):
    return pl.BlockSpec(shape, lambda i: (0,) * len(shape))


def _mesh_position():
    return lax.axis_index("x"), lax.axis_index("y"), lax.axis_index("c")


def _flat_index(x, y, c):
    return 4 * x + 2 * y + c


def _peers(x, y, c):
    out = []
    for k in range(1, N_DEV):
        fx, fy, fc = (k >> 2) & 1, (k >> 1) & 1, k & 1
        out.append((1 - x if fx else x, 1 - y if fy else y, 1 - c if fc else c))
    return out


def _comm_out_shapes(srcs, modes):
    return [jax.ShapeDtypeStruct((N_DEV,) + s.shape if md == "gather" else s.shape, s.dtype)
            for s, md in zip(srcs, modes)]


def _comm_scratch(n):
    per_peer = n * (N_DEV - 1)
    return [pltpu.SemaphoreType.DMA((per_peer,)), pltpu.SemaphoreType.DMA((per_peer,)), pltpu.SemaphoreType.DMA((n,))]


class _Copies:
    def __init__(self, own, first, relay):
        self.own, self.first, self.relay = own, first, relay

    def start(self):
        for cp in self.own + self.first:
            cp.start()

    def forward(self):
        for arrival, onward in self.relay:
            arrival.wait_recv()
            onward.start()

    def finish(self):
        arrivals = [a for a, _ in self.relay]
        onward = [f for _, f in self.relay]
        for cp in self.first + onward:
            if not any(cp is a for a in arrivals):
                cp.wait_recv()
        for cp in self.first + onward:
            cp.wait_send()
        for cp in self.own:
            cp.wait()


def _comm_copies(src_refs, dst_refs, modes, send, recv, local):
    x, y, c = _mesh_position()
    me = _flat_index(x, y, c)
    n = len(modes)
    sibling = (x, y, 1 - c)
    chips = [(1 - x, y), (x, 1 - y), (1 - x, 1 - y)]

    def remote(src, dst, k, t, to):
        return pltpu.make_async_remote_copy(src_ref=src, dst_ref=dst, send_sem=send.at[k * n + t],
                                            recv_sem=recv.at[k * n + t], device_id=to,
                                            device_id_type=pl.DeviceIdType.MESH)

    own, first, relay = [], [], []
    for t, (src, dst, md) in enumerate(zip(src_refs, dst_refs, modes)):
        if md == "scatter":
            own.append(pltpu.make_async_copy(src.at[me], dst.at[me], local.at[t]))
            for k, peer in enumerate(_peers(x, y, c)):
                first.append(remote(src.at[_flat_index(*peer)], dst.at[me], k, t, peer))
        else:
            own.append(pltpu.make_async_copy(src, dst.at[me], local.at[t]))
            first.append(remote(src, dst.at[me], 0, t, sibling))
            for j, chip in enumerate(chips):
                arrival = remote(src, dst.at[me], 1 + j, t, (*chip, c))
                landed = dst.at[_flat_index(*chip, c)]
                first.append(arrival)
                relay.append((arrival, remote(landed, landed, 4 + j, t, sibling)))
    return _Copies(own, first, relay)


def _hosted(body, n_in, n_out, modes, grid):
    t = len(modes)
    total = math.prod(grid)

    def wrapped(*refs):
        ins, csrc = refs[:n_in], refs[n_in:n_in + t]
        outs = refs[n_in + t:n_in + t + n_out]
        cdst = refs[n_in + t + n_out:n_in + 2 * t + n_out]
        scratch = refs[n_in + 2 * t + n_out:-3]
        copies = _comm_copies(csrc, cdst, modes, *refs[-3:])
        step = pl.program_id(0)
        for axis in range(1, len(grid)):
            step = step * grid[axis] + pl.program_id(axis)

        @pl.when(step == 0)
        def _():
            copies.start()

        body(*ins, *outs, *scratch)

        @pl.when(step == (total * 3) // 5)
        def _():
            copies.forward()

        @pl.when(step == total - 1)
        def _():
            copies.finish()

    return wrapped


def _call(body, name, grid, in_specs, out_specs, out_shape, sems, args, scratch=(), comm=None):
    if comm is None:
        outs = pl.pallas_call(body, name=name, grid=grid, in_specs=in_specs, out_specs=out_specs, out_shape=out_shape,
                              scratch_shapes=list(scratch), compiler_params=_params(sems))(*args)
        return outs, []
    srcs, modes = comm
    n = len(modes)
    res = pl.pallas_call(
        _hosted(body, len(in_specs), len(out_specs), modes, grid), name=name, grid=grid,
        in_specs=list(in_specs) + [HBM_WHOLE] * n, out_specs=list(out_specs) + [HBM_WHOLE] * n,
        out_shape=list(out_shape) + _comm_out_shapes(srcs, modes),
        scratch_shapes=list(scratch) + _comm_scratch(n),
        compiler_params=_params(("arbitrary",) * len(grid)))(*args, *srcs)
    return res[:len(out_specs)], res[len(out_specs):]


def exchange(srcs, modes, name):
    n = len(modes)

    def body(*refs):
        copies = _comm_copies(refs[:n], refs[n:2 * n], modes, *refs[2 * n:])
        copies.start()
        copies.forward()
        copies.finish()

    return pl.pallas_call(body, name=name, in_specs=[HBM_WHOLE] * n, out_specs=[HBM_WHOLE] * n,
                          out_shape=_comm_out_shapes(srcs, modes), scratch_shapes=_comm_scratch(n))(*srcs)


def _nn(a, b):
    return jnp.dot(a, b, preferred_element_type=F32)


def _nt(a, b):
    return lax.dot_general(a, b, (((1,), (1,)), ((), ())), preferred_element_type=F32)


def _tn(a, b):
    return lax.dot_general(a, b, (((0,), (0,)), ((), ())), preferred_element_type=F32)


def _sig(x):
    return 1.0 / (1.0 + jnp.exp(-x))


def _rms_r(x, n=None):
    n = x.shape[-1] if n is None else n
    return lax.rsqrt(jnp.sum(x * x, axis=-1, keepdims=True) * (1.0 / n) + EPS)


def _rms_bwd(x, r, g, dy, n=None):
    n = x.shape[-1] if n is None else n
    xhat = x * r
    dxhat = dy * g
    dx = r * (dxhat - xhat * (jnp.sum(dxhat * xhat, axis=-1, keepdims=True) * (1.0 / n)))
    return dx, jnp.sum(dy * xhat, axis=0, keepdims=True)


def _accumulate(ref, val, first):
    @pl.when(first)
    def _():
        ref[...] = val

    @pl.when(jnp.logical_not(first))
    def _():
        ref[...] += val


_GELU_C = math.sqrt(2.0 / math.pi)


def _gelu_and_grad(x):
    inner = _GELU_C * (x + 0.044715 * x * x * x)
    t = jnp.tanh(inner)
    gelu = 0.5 * x * (1.0 + t)
    dgelu = 0.5 * (1.0 + t) + 0.5 * x * (1.0 - t * t) * _GELU_C * (1.0 + 3.0 * 0.044715 * x * x)
    return gelu, dgelu


def _log1p_small(t):
    return jnp.where(t < 1e-3, t * (1.0 - t * (0.5 - t * (1.0 / 3.0))), jnp.log(1.0 + t))


def _softplus(x):
    return jnp.maximum(x, 0.0) + _log1p_small(jnp.exp(-jnp.abs(x)))


def _sig_tanh(x):
    return 0.5 + 0.5 * jnp.tanh(0.5 * x)


def _ff_chunks(f):
    return 2 if (f // 2) % LANES == 0 else 1


def _swiglu_half(x, g_ref, wg_ref, wu_ref, wd_ref, a_ref, b_ref, fc):
    f = wg_ref.shape[0]
    u = (x * _rms_r(x) * g_ref[...]).astype(BF16)
    acc = jnp.zeros(x.shape, F32)
    for c in range(f // fc):
        cols = slice(c * fc, (c + 1) * fc)
        a = _nt(u, wg_ref[cols, :])
        b = _nt(u, wu_ref[cols, :])
        s = (a * _sig(a) * b).astype(BF16)
        acc = acc + _nn(s, wd_ref[cols, :])
        a_ref[:, cols] = a.astype(BF16)
        b_ref[:, cols] = b.astype(BF16)
    return x + 0.5 * acc, u


def ffn_up(h, g, wg, wu, name, comm=None):
    n, d = h.shape
    f = wg.shape[0]
    tm = _tile(n, 528)
    fc = 2 * LANES if f % (2 * LANES) == 0 else f

    def body(h_ref, g_ref, wg_ref, wu_ref, u_ref, a_ref, b_ref, s_ref):
        x = h_ref[...]
        u = (x * _rms_r(x) * g_ref[...]).astype(BF16)
        u_ref[...] = u
        for c in range(f // fc):
            cols = slice(c * fc, (c + 1) * fc)
            a = _nt(u, wg_ref[cols, :])
            b = _nt(u, wu_ref[cols, :])
            a_ref[:, cols] = a.astype(BF16)
            b_ref[:, cols] = b.astype(BF16)
            s_ref[:, cols] = (0.5 * (a * _sig(a) * b)).astype(BF16)

    wide = jax.ShapeDtypeStruct((n, f), BF16)
    return _call(
        body, name, (n // tm,),
        [_row(tm, d), _fixed((1, d)), VMEM_WHOLE, VMEM_WHOLE],
        [_row(tm, d), _row(tm, f), _row(tm, f), _row(tm, f)],
        [jax.ShapeDtypeStruct((n, d), BF16), wide, wide, wide],
        ("parallel",), (h, g, wg, wu), comm=comm)


def ffn_down(h, s, wd, name, comm=None):
    n, d = h.shape
    f = wd.shape[0]
    tm = _tile(n, 528)

    def body(h_ref, s_ref, wd_ref, ho_ref):
        ho_ref[...] = h_ref[...] + _nn(s_ref[...], wd_ref[...])

    (ho,), landed = _call(body, name, (n // tm,), [_row(tm, d), _row(tm, f), VMEM_WHOLE], [_row(tm, d)],
                          [jax.ShapeDtypeStruct((n, d), F32)], ("parallel",), (h, s, wd), comm=comm)
    return ho, landed


def ffn_fwd_loss(h, g, wg, wu, wd, g_final, tgt, lp, name):
    n, d = h.shape
    f = wg.shape[0]
    tm = _tile(lp, 528)
    per_seq = lp // tm
    fc = 2 * LANES if f % (2 * LANES) == 0 else f

    def body(h_ref, g_ref, wg_ref, wu_ref, wd_ref, gf_ref, t_ref, dh_ref, u_ref, a_ref, b_ref, loss_ref, dgf_ref):
        i = pl.program_id(0)
        y, u_ref[...] = _swiglu_half(h_ref[...], g_ref, wg_ref, wu_ref, wd_ref, a_ref, b_ref, fc)
        dh_ref[...], part, dg = _loss_and_grad(y, gf_ref[...], t_ref[...], (i % per_seq) * tm)
        _accumulate(loss_ref, jnp.broadcast_to(part, (1, LANES)), i == 0)
        _accumulate(dgf_ref, dg, i == 0)

    outs, _ = _call(
        body, name, (n // tm,),
        [_row(tm, d), _fixed((1, d)), VMEM_WHOLE, VMEM_WHOLE, VMEM_WHOLE, _fixed((1, d)), _row(tm, d)],
        [_row(tm, d), _row(tm, d), _row(tm, f), _row(tm, f), _fixed((1, LANES)), _fixed((1, d))],
        [jax.ShapeDtypeStruct((n, d), F32), jax.ShapeDtypeStruct((n, d), BF16),
         jax.ShapeDtypeStruct((n, f), BF16), jax.ShapeDtypeStruct((n, f), BF16),
         jax.ShapeDtypeStruct((1, LANES), F32), jax.ShapeDtypeStruct((1, d), F32)],
        ("arbitrary",), (h, g, wg, wu, wd, g_final, tgt))
    return outs


def ffn_bwd_act(dh, h, g, a, b, wg, wu, wd, name, comm=None, emit_sh=True):
    n, d = h.shape
    f = wg.shape[0]
    tm = _tile(n, 352 if emit_sh else 384)
    nc = _ff_chunks(f)
    fc = f // nc

    def body(dh_ref, h_ref, g_ref, a_ref, b_ref, wg_ref, wu_ref, wd_ref, dhi_ref, da_ref, db_ref, *rest):
        dg_ref = rest[-1]
        x = h_ref[...]
        dy = dh_ref[...]
        r = _rms_r(x)
        dhh = (0.5 * dy).astype(BF16)
        du = jnp.zeros((tm, d), F32)
        for c in range(nc):
            cols = slice(c * fc, (c + 1) * fc)
            ds = _nt(dhh, wd_ref[cols, :])
            av = a_ref[:, cols].astype(F32)
            bv = b_ref[:, cols].astype(F32)
            sg = _sig(av)
            sil = av * sg
            da = (ds * bv * (sg * (1.0 + av * (1.0 - sg)))).astype(BF16)
            db = (ds * sil).astype(BF16)
            da_ref[:, cols] = da
            db_ref[:, cols] = db
            if emit_sh:
                rest[0][:, cols] = (0.5 * sil * bv).astype(BF16)
            du = du + _nn(da, wg_ref[cols, :]) + _nn(db, wu_ref[cols, :])
        dx, dg = _rms_bwd(x, r, g_ref[...], du)
        dhi_ref[...] = dy + dx
        _accumulate(dg_ref, dg, pl.program_id(0) == 0)

    wide = [jax.ShapeDtypeStruct((n, f), BF16)] * (3 if emit_sh else 2)
    return _call(
        body, name, (n // tm,),
        [_row(tm, d), _row(tm, d), _fixed((1, d)), _row(tm, f), _row(tm, f), VMEM_WHOLE, VMEM_WHOLE, VMEM_WHOLE],
        [_row(tm, d)] + [_row(tm, f)] * len(wide) + [_fixed((1, d))],
        [jax.ShapeDtypeStruct((n, d), F32)] + wide + [jax.ShapeDtypeStruct((1, d), F32)],
        ("arbitrary",), (dh, h, g, a, b, wg, wu, wd), comm=comm)


def tn_matmul(x, y, name, out="f32", comm=None):
    n, k = x.shape
    m = y.shape[1]
    tm = _tile(n, TN_ROWS)
    kc, mc = k, (512 if m % 512 == 0 else m)
    while tm * kc * x.dtype.itemsize > TN_X_BYTES and kc % (2 * LANES) == 0:
        kc //= 2
    while tm * mc * y.dtype.itemsize > TN_Y_BYTES and mc % (2 * LANES) == 0:
        mc //= 2
    steps = n // tm

    def body(x_ref, y_ref, o_ref, *acc):
        i = pl.program_id(2)
        part = _tn(x_ref[...].astype(BF16), y_ref[...].astype(BF16))
        if steps == 1:
            o_ref[...] = part.astype(o_ref.dtype)
        elif out == "f32":
            _accumulate(o_ref, part, i == 0)
        else:
            _accumulate(acc[0], part, i == 0)

            @pl.when(i == steps - 1)
            def _():
                o_ref[...] = acc[0][...].astype(BF16)

    out_shape = jax.ShapeDtypeStruct((k, m), F32 if out == "f32" else BF16)
    (res,), landed = _call(
        body, name, (k // kc, m // mc, steps),
        [pl.BlockSpec((tm, kc), lambda a, b, i: (i, a)), pl.BlockSpec((tm, mc), lambda a, b, i: (i, b))],
        [pl.BlockSpec((kc, mc), lambda a, b, i: (a, b))], [out_shape], ("parallel", "parallel", "arbitrary"), (x, y),
        scratch=[pltpu.VMEM((kc, mc), F32)] if (out == "bf16" and steps > 1) else [], comm=comm)
    return (res, landed) if comm is not None else res


def inproj_fwd(h, g, wm, wl):
    n, d = h.shape
    tm = _tile(n, 352)

    def body(h_ref, g_ref, wm_ref, wl_ref, u_ref, zm_ref, zl_ref):
        x = h_ref[...]
        u = (x * _rms_r(x) * g_ref[...]).astype(BF16)
        u_ref[...] = u
        zm_ref[...] = _nt(u, wm_ref[...])
        zl_ref[...] = _nt(u, wl_ref[...])

    return pl.pallas_call(
        body, name="inproj_fwd", grid=(n // tm,),
        in_specs=[_row(tm, d), _fixed((1, d)), VMEM_WHOLE, VMEM_WHOLE],
        out_specs=[_row(tm, d), _row(tm, MLA_IN), _row(tm, 2 * LRU_WIDTH)],
        out_shape=[jax.ShapeDtypeStruct((n, d), BF16), jax.ShapeDtypeStruct((n, MLA_IN), F32),
                   jax.ShapeDtypeStruct((n, 2 * LRU_WIDTH), F32)],
        compiler_params=_params(("parallel",)),
    )(h, g, wm, wl)


def inproj_bwd(dzm, du, dgate, dh2, h, g, wm, wl):
    n, d = h.shape
    tm = _tile(n, 352)

    def body(dzm_ref, du_ref, dgt_ref, dh2_ref, h_ref, g_ref, wm_ref, wl_ref, dh_ref, dg_ref):
        x = h_ref[...]
        dun = (_nn(dzm_ref[...].astype(BF16), wm_ref[...])
               + _nn(du_ref[...].astype(BF16), wl_ref[:LRU_WIDTH, :])
               + _nn(dgt_ref[...].astype(BF16), wl_ref[LRU_WIDTH:, :]))
        dx, dg = _rms_bwd(x, _rms_r(x), g_ref[...], dun)
        dh_ref[...] = dh2_ref[...] + dx
        _accumulate(dg_ref, dg, pl.program_id(0) == 0)

    return pl.pallas_call(
        body, name="inproj_bwd", grid=(n // tm,),
        in_specs=[_row(tm, MLA_IN), _row(tm, LRU_WIDTH), _row(tm, LRU_WIDTH), _row(tm, d), _row(tm, d),
                  _fixed((1, d)), VMEM_WHOLE, VMEM_WHOLE],
        out_specs=[_row(tm, d), _fixed((1, d))],
        out_shape=[jax.ShapeDtypeStruct((n, d), F32), jax.ShapeDtypeStruct((1, d), F32)],
        compiler_params=_params(("arbitrary",)),
    )(dzm, du, dgate, dh2, h, g, wm, wl)


def _rope_tables(lp):
    pos = jnp.arange(lp, dtype=F32) - float(PAD)
    half = D_ROPE // 2
    inv_freq = ROPE_THETA ** (-jnp.arange(0, half, dtype=F32) / half)
    ang = pos[:, None] * inv_freq[None, :]
    cos, sin = jnp.cos(ang), jnp.sin(ang)
    one = jnp.ones((lp, D_NOPE), F32)
    z_nope = jnp.zeros((lp, D_NOPE), F32)
    z_half = jnp.zeros((lp, half), F32)
    z_tail = jnp.zeros((lp, HEAD_SLAB - D_QK), F32)
    cosr = jnp.concatenate([one, cos, cos, z_tail], axis=1)
    sin_up = jnp.concatenate([z_nope, z_half, sin, z_tail], axis=1)
    sin_dn = jnp.concatenate([z_nope, -sin, z_half, z_tail], axis=1)
    return cosr, sin_up, sin_dn


def _rope(x, cosr, sin_up, sin_dn):
    half = D_ROPE // 2
    return x * cosr + pltpu.roll(x, half, axis=1) * sin_up + pltpu.roll(x, HEAD_SLAB - half, axis=1) * sin_dn


def _rope_bwd(dy, cosr, sin_up, sin_dn):
    half = D_ROPE // 2
    return (dy * cosr + pltpu.roll(dy * sin_up, HEAD_SLAB - half, axis=1)
            + pltpu.roll(dy * sin_dn, half, axis=1))


def _k_rope_slab(zm_tile):
    tm = zm_tile.shape[0]
    krp = zm_tile[:, Q_RANK + KV_RANK:MLA_IN]
    return jnp.concatenate([jnp.zeros((tm, D_NOPE), F32), krp], axis=1)


def mla_prep_fwd(zm, gql, gkvl, wuq, wuk, wuv, gqh, gkh, tables, lp):
    n = zm.shape[0]
    tm = _tile(lp, 352)
    per_seq = lp // tm
    width = MLA_HEADS * HEAD_SLAB
    scale = 1.0 / math.sqrt(D_QK)

    def body(zm_ref, gql_ref, gkvl_ref, wuq_ref, wuk_ref, wuv_ref, gqh_ref, gkh_ref,
             cos_ref, up_ref, dn_ref, q_ref, k_ref, v_ref, qn_ref, cn_ref):
        z = zm_ref[...]
        cq = z[:, :Q_RANK]
        ckv = z[:, Q_RANK:Q_RANK + KV_RANK]
        qn = (cq * _rms_r(cq) * gql_ref[...]).astype(BF16)
        cn = (ckv * _rms_r(ckv) * gkvl_ref[...]).astype(BF16)
        qn_ref[...] = qn
        cn_ref[...] = cn
        q_raw = _nt(qn, wuq_ref[...])
        k_raw = _nt(cn, wuk_ref[...])
        v_ref[...] = _nt(cn, wuv_ref[...]).astype(BF16)
        kr_slab = _k_rope_slab(z)
        cosr, sin_up, sin_dn = cos_ref[...], up_ref[...], dn_ref[...]
        for hd in range(MLA_HEADS):
            cols = slice(hd * HEAD_SLAB, (hd + 1) * HEAD_SLAB)
            xq = q_raw[:, cols]
            yq = _rope(xq * _rms_r(xq, D_QK) * gqh_ref[...], cosr, sin_up, sin_dn)
            q_ref[:, cols] = (yq * scale).astype(BF16)
            xk = k_raw[:, cols] + kr_slab
            yk = _rope(xk * _rms_r(xk, D_QK) * gkh_ref[...], cosr, sin_up, sin_dn)
            k_ref[:, cols] = yk.astype(BF16)

    tab = pl.BlockSpec((tm, HEAD_SLAB), lambda i: (i % per_seq, 0))
    return pl.pallas_call(
        body, name="mla_prep_fwd", grid=(n // tm,),
        in_specs=[_row(tm, MLA_IN), _fixed((1, Q_RANK)), _fixed((1, KV_RANK)), VMEM_WHOLE, VMEM_WHOLE, VMEM_WHOLE,
                  _fixed((1, HEAD_SLAB)), _fixed((1, HEAD_SLAB)), tab, tab, tab],
        out_specs=[_row(tm, width), _row(tm, width), _row(tm, MLA_HEADS * D_V), _row(tm, Q_RANK), _row(tm, KV_RANK)],
        out_shape=[jax.ShapeDtypeStruct((n, width), BF16), jax.ShapeDtypeStruct((n, width), BF16),
                   jax.ShapeDtypeStruct((n, MLA_HEADS * D_V), BF16), jax.ShapeDtypeStruct((n, Q_RANK), BF16),
                   jax.ShapeDtypeStruct((n, KV_RANK), BF16)],
        compiler_params=_params(("parallel",)),
    )(zm, gql, gkvl, wuq, wuk, wuv, gqh, gkh, *tables)


def mla_prep_bwd(dq, dk, dv, zm, qn, cn, gql, gkvl, wuq, wuk, wuv, gqh, gkh, tables, lp, comm=None):
    n = zm.shape[0]
    tm = _tile(lp, 704)
    per_seq = lp // tm
    width = MLA_HEADS * HEAD_SLAB
    scale = 1.0 / math.sqrt(D_QK)

    def body(dq_ref, dk_ref, dv_ref, zm_ref, qn_ref, cn_ref, gql_ref, gkvl_ref, wuq_ref, wuk_ref, wuv_ref,
             gqh_ref, gkh_ref, cos_ref, up_ref, dn_ref,
             dzm_ref, dqr_ref, dkr_ref, dgql_ref, dgkvl_ref, dgqh_ref, dgkh_ref):
        z = zm_ref[...]
        cq = z[:, :Q_RANK]
        ckv = z[:, Q_RANK:Q_RANK + KV_RANK]
        q_raw = _nt(qn_ref[...], wuq_ref[...])
        k_raw = _nt(cn_ref[...], wuk_ref[...])
        kr_slab = _k_rope_slab(z)
        cosr, sin_up, sin_dn = cos_ref[...], up_ref[...], dn_ref[...]
        dgq = jnp.zeros((1, HEAD_SLAB), F32)
        dgk = jnp.zeros((1, HEAD_SLAB), F32)
        dkrp = jnp.zeros((tm, HEAD_SLAB - D_NOPE), F32)
        for hd in range(MLA_HEADS):
            cols = slice(hd * HEAD_SLAB, (hd + 1) * HEAD_SLAB)
            xq = q_raw[:, cols]
            dxn = _rope_bwd(dq_ref[:, cols] * scale, cosr, sin_up, sin_dn)
            dxq, dg = _rms_bwd(xq, _rms_r(xq, D_QK), gqh_ref[...], dxn, D_QK)
            dgq = dgq + dg
            dqr_ref[:, cols] = dxq.astype(BF16)
            xk = k_raw[:, cols] + kr_slab
            dxn = _rope_bwd(dk_ref[:, cols], cosr, sin_up, sin_dn)
            dxk, dg = _rms_bwd(xk, _rms_r(xk, D_QK), gkh_ref[...], dxn, D_QK)
            dgk = dgk + dg
            dkr_ref[:, cols] = dxk.astype(BF16)
            dkrp = dkrp + dxk[:, D_NOPE:]
        dqn = _nn(dqr_ref[...], wuq_ref[...])
        dcn = _nn(dkr_ref[...], wuk_ref[...]) + _nn(dv_ref[...].astype(BF16), wuv_ref[...])
        dcq, dg1 = _rms_bwd(cq, _rms_r(cq), gql_ref[...], dqn)
        dckv, dg2 = _rms_bwd(ckv, _rms_r(ckv), gkvl_ref[...], dcn)
        dzm_ref[:, :Q_RANK] = dcq
        dzm_ref[:, Q_RANK:Q_RANK + KV_RANK] = dckv
        dzm_ref[:, Q_RANK + KV_RANK:] = dkrp
        first = pl.program_id(0) == 0
        _accumulate(dgql_ref, dg1, first)
        _accumulate(dgkvl_ref, dg2, first)
        _accumulate(dgqh_ref, dgq, first)
        _accumulate(dgkh_ref, dgk, first)

    tab = pl.BlockSpec((tm, HEAD_SLAB), lambda i: (i % per_seq, 0))
    return _call(
        body, "mla_prep_bwd", (n // tm,),
        [_row(tm, width), _row(tm, width), _row(tm, MLA_HEADS * D_V), _row(tm, MLA_IN),
         _row(tm, Q_RANK), _row(tm, KV_RANK), _fixed((1, Q_RANK)), _fixed((1, KV_RANK)),
         VMEM_WHOLE, VMEM_WHOLE, VMEM_WHOLE, _fixed((1, HEAD_SLAB)), _fixed((1, HEAD_SLAB)), tab, tab, tab],
        [_row(tm, MLA_IN), _row(tm, width), _row(tm, width), _fixed((1, Q_RANK)), _fixed((1, KV_RANK)),
         _fixed((1, HEAD_SLAB)), _fixed((1, HEAD_SLAB))],
        [jax.ShapeDtypeStruct((n, MLA_IN), F32), jax.ShapeDtypeStruct((n, width), BF16),
         jax.ShapeDtypeStruct((n, width), BF16), jax.ShapeDtypeStruct((1, Q_RANK), F32),
         jax.ShapeDtypeStruct((1, KV_RANK), F32), jax.ShapeDtypeStruct((1, HEAD_SLAB), F32),
         jax.ShapeDtypeStruct((1, HEAD_SLAB), F32)],
        ("arbitrary",), (dq, dk, dv, zm, qn, cn, gql, gkvl, wuq, wuk, wuv, gqh, gkh, *tables), comm=comm)


def _attn_tile(lp):
    return _tile(lp, 704, CHUNK)


def _chunk_mask(i, j, t):
    qpos = i * t + lax.broadcasted_iota(jnp.int32, (t, t), 0)
    kpos = j * t + lax.broadcasted_iota(jnp.int32, (t, t), 1)
    same_or_earlier = jnp.right_shift(kpos, CHUNK_SHIFT) <= jnp.right_shift(qpos, CHUNK_SHIFT)
    return jnp.logical_and(same_or_earlier, kpos >= PAD)


def _masked_scores(s, i, j, t, diagonal):
    if diagonal:
        return jnp.where(_chunk_mask(i, j, t), s, NEG_INF)
    kpos = j * t + lax.broadcasted_iota(jnp.int32, (1, t), 1)
    return s + jnp.where(kpos < PAD, NEG_INF, 0.0)


def attn_fwd(q, k, v, nb, lp, comm=None):
    n = q.shape[0]
    t = _attn_tile(lp)
    nq = lp // t

    hp = ATTN_HEADS_PER_STEP

    def body(q_ref, k_ref, v_ref, o_ref, lse_ref):
        i = pl.program_id(2)
        qs = [q_ref[:, hh * HEAD_SLAB:(hh + 1) * HEAD_SLAB] for hh in range(hp)]

        def kv_step(j, carry, diagonal=False):
            off = pl.multiple_of(j * t, t)
            out = []
            for hh in range(hp):
                m, l, acc = carry[hh]
                kv = k_ref[pl.ds(off, t), hh * HEAD_SLAB:(hh + 1) * HEAD_SLAB]
                s = _masked_scores(_nt(qs[hh], kv), i, j, t, diagonal)
                m_new = jnp.maximum(m, jnp.max(s, axis=-1, keepdims=True))
                p = jnp.exp(s - m_new)
                alpha = jnp.exp(m - m_new)
                l = alpha * l + jnp.sum(p, axis=-1, keepdims=True)
                acc = alpha * acc + _nn(p.astype(BF16), v_ref[pl.ds(off, t), hh * D_V:(hh + 1) * D_V])
                out.append((m_new, l, acc))
            return tuple(out)

        init = tuple((jnp.full((t, 1), NEG_INF, F32), jnp.zeros((t, 1), F32), jnp.zeros((t, D_V), F32))
                     for _ in range(hp))
        done = kv_step(i, lax.fori_loop(0, i, kv_step, init), diagonal=True)
        for hh, (m, l, acc) in enumerate(done):
            o_ref[:, hh * D_V:(hh + 1) * D_V] = acc * (1.0 / l)
            lse_ref[hh] = jnp.broadcast_to(m + jnp.log(l), (t, LANES))

    return _call(
        body, "attn_fwd", (nb, MLA_HEADS // hp, nq),
        [pl.BlockSpec((t, hp * HEAD_SLAB), lambda b, h, i: (b * nq + i, h)),
         pl.BlockSpec((lp, hp * HEAD_SLAB), lambda b, h, i: (b, h)),
         pl.BlockSpec((lp, hp * D_V), lambda b, h, i: (b, h))],
        [pl.BlockSpec((t, hp * D_V), lambda b, h, i: (b * nq + i, h)),
         pl.BlockSpec((hp, t, LANES), lambda b, h, i: (h, b * nq + i, 0))],
        [jax.ShapeDtypeStruct((n, MLA_HEADS * D_V), F32), jax.ShapeDtypeStruct((MLA_HEADS, n, LANES), F32)],
        ("parallel", "parallel", "parallel"), (q, k, v), comm=comm)


def attn_bwd(q, k, v, o, do, lse, nb, lp, comm=None):
    n = q.shape[0]
    t = _attn_tile(lp)
    nq = lp // t

    def body(q_ref, k_ref, v_ref, o_ref, do_ref, lse_ref, dq_ref, dk_ref, dv_ref):
        dk_ref[...] = jnp.zeros_like(dk_ref)
        dv_ref[...] = jnp.zeros_like(dv_ref)

        def q_step(i, _):
            qoff = pl.multiple_of(i * t, t)
            qv = q_ref[pl.ds(qoff, t), :]
            dov = do_ref[pl.ds(qoff, t), :]
            delta = jnp.sum(o_ref[pl.ds(qoff, t), :] * dov, axis=-1, keepdims=True)
            lse_q = jnp.max(lse_ref[0, pl.ds(qoff, t), :], axis=-1, keepdims=True)
            do16 = dov.astype(BF16)

            def kv_step(j, dq_acc, diagonal=False):
                koff = pl.multiple_of(j * t, t)
                kv = k_ref[pl.ds(koff, t), :]
                s = _masked_scores(_nt(qv, kv), i, j, t, diagonal)
                p = jnp.exp(s - lse_q)
                dp = _nt(do16, v_ref[pl.ds(koff, t), :])
                ds16 = (p * (dp - delta)).astype(BF16)
                dv_ref[pl.ds(koff, t), :] += _tn(p.astype(BF16), do16)
                dk_ref[pl.ds(koff, t), :] += _tn(ds16, qv)
                return dq_acc + _nn(ds16, kv)

            earlier = lax.fori_loop(0, i, kv_step, jnp.zeros((t, HEAD_SLAB), F32))
            dq_ref[pl.ds(qoff, t), :] = kv_step(i, earlier, diagonal=True)
            return 0

        lax.fori_loop(0, nq, q_step, 0)

    wide = pl.BlockSpec((lp, HEAD_SLAB), lambda b, h: (b, h))
    thin = pl.BlockSpec((lp, D_V), lambda b, h: (b, h))
    width = MLA_HEADS * HEAD_SLAB
    return _call(
        body, "attn_bwd", (nb, MLA_HEADS),
        [wide, wide, thin, thin, thin, pl.BlockSpec((1, lp, LANES), lambda b, h: (h, b, 0))],
        [wide, wide, thin],
        [jax.ShapeDtypeStruct((n, width), F32), jax.ShapeDtypeStruct((n, width), F32),
         jax.ShapeDtypeStruct((n, MLA_HEADS * D_V), F32)],
        ("parallel", "parallel"), (q, k, v, o, do, lse), comm=comm)


def _seq_rows(nb, lp, width):
    rows = lax.broadcasted_iota(jnp.int32, (lp, width), 0)
    return jnp.concatenate([rows] * nb, axis=0) if nb > 1 else rows


def _lru_gates(u, w_ref, cb, wa, wx, ba, bx, lam):
    xc = (cb + w_ref[pl.ds(3, 1), :] * u + w_ref[pl.ds(2, 1), :] * pltpu.roll(u, 1, axis=0)
          + w_ref[pl.ds(1, 1), :] * pltpu.roll(u, 2, axis=0) + w_ref[pl.ds(0, 1), :] * pltpu.roll(u, 3, axis=0))
    xc16 = xc.astype(BF16)
    ra = _sig_tanh(_nn(xc16, wa) + ba)
    ia = _sig_tanh(_nn(xc16, wx) + bx)
    sp = _softplus(-lam)
    log_a = -C_RGLRU * ra * sp
    a = jnp.exp(log_a)
    x2 = 2.0 * log_a
    mult = jnp.sqrt(jnp.where(x2 > -1e-2, -x2 * (1.0 + x2 * (0.5 + x2 * (1.0 / 6.0))), 1.0 - a * a))
    return xc, xc16, ra, ia, sp, a, mult


def _scan_block_rows(width):
    return lax.broadcasted_iota(jnp.int32, (8, width), 0)


def lru_fwd(zl, conv_w, conv_b, wa, wx, ba, bx, lam, nb, lp, comm=None):
    n = zl.shape[0]
    w = LRU_TILE
    nt = LRU_WIDTH // w
    nblk = lp // 8

    def body(u_ref, gt_ref, cw_ref, cb_ref, wa_ref, wx_ref, ba_ref, bx_ref, lam_ref, y_ref, h_ref, a_s, b_s):
        u = u_ref[...]
        xc, _, _, ia, _, a, mult = _lru_gates(u, cw_ref, cb_ref[...], wa_ref[...], wx_ref[...],
                                              ba_ref[...], bx_ref[...], lam_ref[...])
        row = _seq_rows(nb, lp, w)
        mult = jnp.where(row == PAD, 1.0, mult)
        a_s[...] = a
        b_s[...] = jnp.where(row < PAD, 0.0, mult * (ia * xc))
        r8 = _scan_block_rows(w)

        def blk(i, carry):
            out = []
            for s_id in range(nb):
                off = pl.multiple_of(s_id * lp + i * 8, 8)
                av = a_s[pl.ds(off, 8), :]
                bv = b_s[pl.ds(off, 8), :]
                for sh in (1, 2, 4):
                    keep = r8 >= sh
                    bv = jnp.where(keep, av * pltpu.roll(bv, sh, axis=0) + bv, bv)
                    av = jnp.where(keep, av * pltpu.roll(av, sh, axis=0), av)
                hv = bv + av * carry[s_id]
                h_ref[pl.ds(off, 8), :] = hv
                out.append(jnp.sum(jnp.where(r8 == 7, hv, 0.0), axis=0, keepdims=True))
            return tuple(out)

        lax.fori_loop(0, nblk, blk, tuple(jnp.zeros((1, w), F32) for _ in range(nb)))
        gelu, _ = _gelu_and_grad(gt_ref[...])
        y_ref[...] = h_ref[...] * gelu

    col = lambda c: (0, c)
    return _call(
        body, "lru_fwd", (nt,),
        [pl.BlockSpec((n, w), col), pl.BlockSpec((n, w), lambda c: (0, nt + c)),
         pl.BlockSpec((CONV_W, w), col), pl.BlockSpec((1, w), col),
         pl.BlockSpec((w, w), lambda c: (c, c)), pl.BlockSpec((w, w), lambda c: (c, c)),
         pl.BlockSpec((1, w), col), pl.BlockSpec((1, w), col), pl.BlockSpec((1, w), col)],
        [pl.BlockSpec((n, w), col), pl.BlockSpec((n, w), col)],
        [jax.ShapeDtypeStruct((n, LRU_WIDTH), F32), jax.ShapeDtypeStruct((n, LRU_WIDTH), F32)],
        ("parallel",), (zl, zl, conv_w, conv_b, wa, wx, ba, bx, lam),
        scratch=[pltpu.VMEM((n, w), F32), pltpu.VMEM((n, w), F32)], comm=comm)


def lru_bwd(zl, hs, dy, conv_w, conv_b, wa, wx, ba, bx, lam, nb, lp, comm=None):
    n = zl.shape[0]
    w = LRU_TILE
    nt = LRU_WIDTH // w
    nblk = lp // 8

    def body(u_ref, gt_ref, h_ref, dy_ref, cw_ref, cb_ref, wa_ref, wx_ref, ba_ref, bx_ref, lam_ref,
             du_ref, dgt_ref, dcw_ref, dcb_ref, dba_ref, dbx_ref, dlam_ref, dwa_ref, dwx_ref,
             c_s, d_s, g_s, dwa_s, dwx_s):
        u = u_ref[...]
        lam = lam_ref[...]
        xc, xc16, ra, ia, sp, a, mult = _lru_gates(u, cw_ref, cb_ref[...], wa_ref[...], wx_ref[...],
                                                   ba_ref[...], bx_ref[...], lam)
        row = lax.broadcasted_iota(jnp.int32, (lp, w), 0)
        hv = h_ref[...]
        dyv = dy_ref[...]
        gelu, dgelu = _gelu_and_grad(gt_ref[...])
        dgt_ref[...] = jnp.where(row >= PAD, dyv * hv * dgelu, 0.0)
        c_s[...] = pltpu.roll(a, lp - 1, axis=0)
        d_s[...] = dyv * gelu
        r8 = _scan_block_rows(w)

        def blk(ii, carry):
            off = pl.multiple_of((nblk - 1 - ii) * 8, 8)
            cv = c_s[pl.ds(off, 8), :]
            dv = d_s[pl.ds(off, 8), :]
            for sh in (1, 2, 4):
                keep = r8 < 8 - sh
                dv = jnp.where(keep, cv * pltpu.roll(dv, 8 - sh, axis=0) + dv, dv)
                cv = jnp.where(keep, cv * pltpu.roll(cv, 8 - sh, axis=0), cv)
            gv = dv + cv * carry
            g_s[pl.ds(off, 8), :] = gv
            return jnp.sum(jnp.where(r8 == 0, gv, 0.0), axis=0, keepdims=True)

        lax.fori_loop(0, nblk, blk, jnp.zeros((1, w), F32))
        gv = g_s[...]
        first_row = row == PAD
        db = jnp.where(row >= PAD, gv, 0.0)
        da = jnp.where(row > PAD, gv * pltpu.roll(hv, 1, axis=0), 0.0)
        mult_eff = jnp.where(first_row, 1.0, mult)
        dmult = jnp.where(first_row, 0.0, db * (ia * xc))
        dia = db * mult_eff * xc
        dxc = db * mult_eff * ia
        dla = da * a - dmult * (a * a) / mult
        dra = dla * (-C_RGLRU * sp)
        dsp = jnp.sum(dla * (-C_RGLRU * ra), axis=0, keepdims=True)
        dpa = dra * ra * (1.0 - ra)
        dpx = dia * ia * (1.0 - ia)
        dpa16 = dpa.astype(BF16)
        dpx16 = dpx.astype(BF16)
        dxc = dxc + _nt(dpa16, wa_ref[...]) + _nt(dpx16, wx_ref[...])
        du = cw_ref[pl.ds(CONV_W - 1, 1), :] * dxc
        dcw = [jnp.sum(dxc * u, axis=0, keepdims=True)]
        for tap in range(1, CONV_W):
            dcw.insert(0, jnp.sum(dxc * pltpu.roll(u, tap, axis=0), axis=0, keepdims=True))
            du = du + cw_ref[pl.ds(CONV_W - 1 - tap, 1), :] * pltpu.roll(dxc, lp - tap, axis=0)
        du_ref[...] = jnp.where(row >= PAD, du, 0.0)
        first = pl.program_id(1) == 0
        _accumulate(dlam_ref, -_sig(-lam) * dsp, first)
        _accumulate(dba_ref, jnp.sum(dpa, axis=0, keepdims=True), first)
        _accumulate(dbx_ref, jnp.sum(dpx, axis=0, keepdims=True), first)
        _accumulate(dcb_ref, jnp.sum(dxc, axis=0, keepdims=True), first)
        _accumulate(dcw_ref, jnp.concatenate(dcw, axis=0), first)
        _accumulate(dwa_s, _tn(xc16, dpa16), first)
        _accumulate(dwx_s, _tn(xc16, dpx16), first)

        @pl.when(pl.program_id(1) == nb - 1)
        def _():
            for j in range(w // LRU_BLOCK):
                blk_rows = slice(j * LRU_BLOCK, (j + 1) * LRU_BLOCK)
                dwa_ref[0, blk_rows, :] = dwa_s[blk_rows, blk_rows]
                dwx_ref[0, blk_rows, :] = dwx_s[blk_rows, blk_rows]

    col = lambda c, b: (0, c)
    vec = pl.BlockSpec((1, w), col)
    mat = pl.BlockSpec((w, w), lambda c, b: (c, c))
    big = pl.BlockSpec((lp, w), lambda c, b: (b, c))
    dmat = pl.BlockSpec((1, w, LRU_BLOCK), lambda c, b: (c, 0, 0))
    return _call(
        body, "lru_bwd", (nt, nb),
        [big, pl.BlockSpec((lp, w), lambda c, b: (b, nt + c)), big, big,
         pl.BlockSpec((CONV_W, w), col), vec, mat, mat, vec, vec, vec],
        [big, big, pl.BlockSpec((CONV_W, w), col), vec, vec, vec, vec, dmat, dmat],
        [jax.ShapeDtypeStruct((n, LRU_WIDTH), F32), jax.ShapeDtypeStruct((n, LRU_WIDTH), F32),
         jax.ShapeDtypeStruct((CONV_W, LRU_WIDTH), F32), jax.ShapeDtypeStruct((1, LRU_WIDTH), F32),
         jax.ShapeDtypeStruct((1, LRU_WIDTH), F32), jax.ShapeDtypeStruct((1, LRU_WIDTH), F32),
         jax.ShapeDtypeStruct((1, LRU_WIDTH), F32), jax.ShapeDtypeStruct((nt, w, LRU_BLOCK), F32),
         jax.ShapeDtypeStruct((nt, w, LRU_BLOCK), F32)],
        ("parallel", "arbitrary"), (zl, zl, hs, dy, conv_w, conv_b, wa, wx, ba, bx, lam),
        scratch=[pltpu.VMEM((lp, w), F32), pltpu.VMEM((lp, w), F32), pltpu.VMEM((lp, w), F32),
                 pltpu.VMEM((w, w), F32), pltpu.VMEM((w, w), F32)], comm=comm)


def outproj_fwd(h, ya, yl, gao, glo, wout):
    n, d = h.shape
    half = ya.shape[1]
    tm = _tile(n, 704)

    def body(h_ref, ya_ref, yl_ref, gao_ref, glo_ref, w_ref, ho_ref, yn_ref):
        xa = ya_ref[...]
        xl = yl_ref[...]
        na = (xa * _rms_r(xa) * gao_ref[...]).astype(BF16)
        nl = (xl * _rms_r(xl) * glo_ref[...]).astype(BF16)
        yn_ref[:, :half] = na
        yn_ref[:, half:] = nl
        ho_ref[...] = h_ref[...] + _nn(na, w_ref[:half, :]) + _nn(nl, w_ref[half:, :])

    return pl.pallas_call(
        body, name="outproj_fwd", grid=(n // tm,),
        in_specs=[_row(tm, d), _row(tm, half), _row(tm, half), _fixed((1, half)), _fixed((1, half)), VMEM_WHOLE],
        out_specs=[_row(tm, d), _row(tm, 2 * half)],
        out_shape=[jax.ShapeDtypeStruct((n, d), F32), jax.ShapeDtypeStruct((n, 2 * half), BF16)],
        compiler_params=_params(("parallel",)),
    )(h, ya, yl, gao, glo, wout)


def outproj_bwd(dh, ya, yl, gao, glo, wout):
    n, d = dh.shape
    half = ya.shape[1]
    tm = _tile(n, 704)

    def body(dh_ref, ya_ref, yl_ref, gao_ref, glo_ref, w_ref, dya_ref, dyl_ref, dgao_ref, dglo_ref):
        d16 = dh_ref[...].astype(BF16)
        xa = ya_ref[...]
        xl = yl_ref[...]
        dxa, dga = _rms_bwd(xa, _rms_r(xa), gao_ref[...], _nt(d16, w_ref[:half, :]))
        dxl, dgl = _rms_bwd(xl, _rms_r(xl), glo_ref[...], _nt(d16, w_ref[half:, :]))
        dya_ref[...] = dxa
        dyl_ref[...] = dxl
        first = pl.program_id(0) == 0
        _accumulate(dgao_ref, dga, first)
        _accumulate(dglo_ref, dgl, first)

    return pl.pallas_call(
        body, name="outproj_bwd", grid=(n // tm,),
        in_specs=[_row(tm, d), _row(tm, half), _row(tm, half), _fixed((1, half)), _fixed((1, half)), VMEM_WHOLE],
        out_specs=[_row(tm, half), _row(tm, half), _fixed((1, half)), _fixed((1, half))],
        out_shape=[jax.ShapeDtypeStruct((n, half), F32), jax.ShapeDtypeStruct((n, half), F32),
                   jax.ShapeDtypeStruct((1, half), F32), jax.ShapeDtypeStruct((1, half), F32)],
        compiler_params=_params(("arbitrary",)),
    )(dh, ya, yl, gao, glo, wout)


def _loss_and_grad(x, gv, tgt, first_row):
    tm, d = x.shape
    r = _rms_r(x)
    row = first_row + lax.broadcasted_iota(jnp.int32, (tm, d), 0)
    diff = jnp.where(row >= FIRST_FRAME, x * r * gv - tgt, 0.0)
    part = 0.5 * jnp.sum(jnp.sum(diff * diff, axis=-1, keepdims=True) * (1.0 / d), axis=0, keepdims=True)
    dx, dg = _rms_bwd(x, r, gv, diff * (1.0 / d))
    return dx, part, dg


def assemble_cols(g, name):
    _, k, ns = g.shape

    def body(g_ref, o_ref):
        for j in range(N_DEV):
            o_ref[:, j * ns:(j + 1) * ns] = g_ref[j]

    return pl.pallas_call(body, name=name, out_shape=jax.ShapeDtypeStruct((k, N_DEV * ns), g.dtype),
                          compiler_params=_params(None))(g)


def split_cols(x, name):
    k, cols = x.shape
    ns = cols // N_DEV

    def body(x_ref, o_ref):
        for j in range(N_DEV):
            o_ref[j] = x_ref[:, j * ns:(j + 1) * ns]

    return pl.pallas_call(body, name=name, out_shape=jax.ShapeDtypeStruct((N_DEV, k, ns), x.dtype),
                          compiler_params=_params(None))(x)


def _slab_rows(w, per_head):
    k = w.shape[1]
    w = w.reshape(MLA_HEADS, per_head, k)
    return jnp.pad(w, ((0, 0), (0, HEAD_SLAB - per_head), (0, 0))).reshape(MLA_HEADS * HEAD_SLAB, k)


def _unslab_rows(w, per_head):
    k = w.shape[1]
    return w.reshape(MLA_HEADS, HEAD_SLAB, k)[:, :per_head].reshape(MLA_HEADS * per_head, k)


def meta_grad(dh0, nb, lp):
    d = dh0.shape[1]
    ns = d // N_DEV
    per_seq = lp // N_META

    def body(x_ref, o_ref):
        x = x_ref[...]
        for j in range(N_DEV):
            _accumulate(o_ref.at[j], x[:, j * ns:(j + 1) * ns], pl.program_id(0) == 0)

    return pl.pallas_call(
        body, name="meta_grad", grid=(nb,),
        in_specs=[pl.BlockSpec((N_META, d), lambda b: (b * per_seq + PAD // N_META, 0))],
        out_specs=pl.BlockSpec((N_DEV, N_META, ns), lambda b: (0, 0, 0)),
        out_shape=jax.ShapeDtypeStruct((N_DEV, N_META, ns), F32),
        compiler_params=_params(("arbitrary",)))(dh0)


VECTORS = [("ffn1_norm", 1024), ("mix_norm", 1024), ("q_latent_norm", 384), ("kv_latent_norm", 256),
           ("q_head_norm", 192), ("k_head_norm", 192), ("conv_b", 512), ("gate_a_b", 512), ("gate_x_b", 512),
           ("lru_lambda", 512), ("attn_out_norm", 512), ("lru_out_norm", 512), ("ffn2_norm", 1024),
           ("final_norm", 1024)]
VEC_ROWS = 16
LOSS_ROW = len(VECTORS)
GATES = ["gate_a_w", "gate_x_w"]


def pack_vectors(grads, loss):
    def body(*refs):
        o_ref = refs[-1]
        o_ref[...] = jnp.zeros_like(o_ref)
        for t, (ref, (_, cnt)) in enumerate(zip(refs[:-2], VECTORS)):
            o_ref[t:t + 1, :cnt] = ref[:, :cnt]
        o_ref[LOSS_ROW:LOSS_ROW + 1, :LANES] = refs[-2][...]

    return pl.pallas_call(body, name="pack_vectors", out_shape=jax.ShapeDtypeStruct((VEC_ROWS, D_MODEL), F32),
                          compiler_params=_params(None))(*[grads[name] for name, _ in VECTORS], loss)


def _adamw_update(w, g, m, v):
    c1 = 1.0 / (1.0 - ADAM_B1 ** ADAM_STEP)
    c2 = 1.0 / (1.0 - ADAM_B2 ** ADAM_STEP)
    mn = ADAM_B1 * m + (1.0 - ADAM_B1) * g
    vn = ADAM_B2 * v + (1.0 - ADAM_B2) * (g * g)
    delta = -ADAM_LR * ((mn * c1) / (jnp.sqrt(vn * c2) + ADAM_EPS) + ADAM_WD * w)
    return delta, mn, vn


def _sum_slots(ref, index=()):
    acc = ref[(0,) + index].astype(F32)
    for s in range(1, N_DEV):
        acc = acc + ref[(s,) + index].astype(F32)
    return acc


def adamw_sharded(r, w, m, v, name):
    rows, cols = w.shape
    tr = _tile(rows, 256, 16) if rows % 16 == 0 else rows

    def body(r_ref, w_ref, m_ref, v_ref, g_ref, d_ref, mo_ref, vo_ref):
        g = _sum_slots(r_ref)
        g_ref[...] = g
        d_ref[...], mo_ref[...], vo_ref[...] = _adamw_update(w_ref[...], g, m_ref[...], v_ref[...])

    spec = pl.BlockSpec((tr, cols), lambda i: (i, 0))
    shape = jax.ShapeDtypeStruct((rows, cols), F32)
    return pl.pallas_call(
        body, name=name, grid=(rows // tr,),
        in_specs=[pl.BlockSpec((N_DEV, tr, cols), lambda i: (0, i, 0))] + [spec] * 3,
        out_specs=[spec] * 4, out_shape=[shape] * 4,
        compiler_params=_params(("parallel",)),
    )(r, w, m, v)


def adamw_small(r_vec, r_gates, w, m, v):
    nt = len(VECTORS) + len(GATES)

    def body(*refs):
        rv_ref = refs[0]
        rg_refs = refs[1:1 + len(GATES)]
        base = 1 + len(GATES)
        w_refs, m_refs, v_refs = (refs[base + i * nt:base + (i + 1) * nt] for i in range(3))
        outs = refs[base + 3 * nt:]
        g_o, d_o, m_o, v_o = (outs[i * nt:(i + 1) * nt] for i in range(4))
        outs[4 * nt][...] = _sum_slots(rv_ref, (slice(LOSS_ROW, LOSS_ROW + 1), slice(0, LANES)))
        for t in range(nt):
            if t < len(VECTORS):
                cnt = VECTORS[t][1]
                g = _sum_slots(rv_ref, (slice(t, t + 1), slice(0, cnt)))
            else:
                g = _sum_slots(rg_refs[t - len(VECTORS)])
            g_o[t][...] = g
            d_o[t][...], m_o[t][...], v_o[t][...] = _adamw_update(w_refs[t][...], g, m_refs[t][...], v_refs[t][...])

    shapes = [jax.ShapeDtypeStruct(a.shape, F32) for a in w]
    res = pl.pallas_call(body, name="adamw_small", out_shape=shapes * 4 + [jax.ShapeDtypeStruct((1, LANES), F32)],
                         compiler_params=_params(None))(r_vec, *r_gates, *w, *m, *v)
    return [res[i * nt:(i + 1) * nt] for i in range(4)], res[4 * nt]


def _block_diag(w):
    nb, n, _ = w.shape
    eye = jnp.eye(nb, dtype=w.dtype)
    return (eye[:, None, :, None] * w[:, :, None, :]).reshape(nb * n, nb * n)


def _two_d(a):
    if a.ndim == 3:
        return a.reshape(a.shape[1], a.shape[2])
    if a.ndim == 4:
        return a.reshape(a.shape[1] * a.shape[2], a.shape[3])
    return a


_WEIGHT_NAMES = ['meta_tokens', 'ffn1_norm', 'ffn1_w_gate', 'ffn1_w_up', 'ffn1_w_down', 'mix_norm', 'w_in',
                 'q_latent_norm', 'w_uq', 'kv_latent_norm', 'w_uk', 'w_uv', 'q_head_norm', 'k_head_norm', 'conv_w',
                 'conv_b', 'gate_a_w', 'gate_a_b', 'gate_x_w', 'gate_x_b', 'lru_lambda', 'attn_out_norm',
                 'lru_out_norm', 'w_out', 'ffn2_norm', 'ffn2_w_gate', 'ffn2_w_up', 'ffn2_w_down', 'final_norm']


COLUMN_SHARDED = ("ffn1_w_gate", "ffn1_w_up", "ffn2_w_gate", "ffn2_w_up", "w_in", "w_uq", "w_uk", "w_uv")


def train_step(x, tgt, w, m, v):
    nb, seq, d = x.shape
    lp = PAD + N_META + seq
    n = nb * lp
    def local(a, name):
        a = _two_d(a)
        return a.T if name in COLUMN_SHARDED else a

    sh = {name: local(w[name], name) for name in _WEIGHT_NAMES}
    m2 = {name: local(m[name], name) for name in _WEIGHT_NAMES}
    v2 = {name: local(v[name], name) for name in _WEIGHT_NAMES}

    def b16(name):
        return sh[name].astype(BF16)

    out = {}

    def update(name, landed):
        out[name] = adamw_sharded(landed, sh[name], m2[name], v2[name], "adamw_" + name)

    g_wg1, g_wu1, g_meta, g_conv = exchange(
        [b16("ffn1_w_gate"), b16("ffn1_w_up"), sh["meta_tokens"], sh["conv_w"]], ["gather"] * 4, "gather_ffn1")
    wg1, wu1 = g_wg1.reshape(D_FF, d), g_wu1.reshape(D_FF, d)
    meta = assemble_cols(g_meta, "assemble_meta")
    conv_w = assemble_cols(g_conv, "assemble_conv")

    front = jnp.concatenate([jnp.zeros((PAD, d), F32), meta], axis=0)
    h0 = jnp.concatenate([jnp.broadcast_to(front[None], (nb, FIRST_FRAME, d)), x], axis=1).reshape(n, d)
    tgt_p = jnp.concatenate([jnp.zeros((nb, FIRST_FRAME, d), F32), tgt], axis=1).reshape(n, d)
    tables = _rope_tables(lp)
    zero_tail = jnp.zeros((1, HEAD_SLAB - D_QK), F32)
    gqh = jnp.concatenate([sh["q_head_norm"], zero_tail], axis=1)
    gkh = jnp.concatenate([sh["k_head_norm"], zero_tail], axis=1)
    wa = _block_diag(w["gate_a_w"][0]).astype(BF16)
    wx = _block_diag(w["gate_x_w"][0]).astype(BF16)

    (u1, a1, b1, s1), (g_wd1, g_in) = ffn_up(h0, sh["ffn1_norm"], wg1, wu1, "ffn1_up",
                                             comm=([b16("ffn1_w_down"), b16("w_in")], ["gather"] * 2))
    wd1 = g_wd1.reshape(D_FF, d)
    h1, (g_uq, g_uk, g_uv, g_out) = ffn_down(
        h0, s1, wd1, "ffn1_down", comm=([b16("w_uq"), b16("w_uk"), b16("w_uv"), b16("w_out")], ["gather"] * 4))
    mla_rows = MLA_IN - D_ROPE
    w_in = g_in.reshape(mla_rows + 2 * LRU_WIDTH, d)
    wm = jnp.concatenate([w_in[:mla_rows], jnp.zeros((D_ROPE, d), BF16)], axis=0)
    wl = w_in[mla_rows:]
    wuq = _slab_rows(g_uq.reshape(MLA_HEADS * D_QK, Q_RANK), D_QK)
    wuk = _slab_rows(g_uk.reshape(MLA_HEADS * D_NOPE, KV_RANK), D_NOPE)
    wuv = g_uv.reshape(MLA_HEADS * D_V, KV_RANK)
    w_out = g_out.reshape(d, d)

    u2, zm, zl = inproj_fwd(h1, sh["mix_norm"], wm, wl)
    q, k, vv, qn, cn = mla_prep_fwd(zm, sh["q_latent_norm"], sh["kv_latent_norm"], wuq, wuk, wuv, gqh, gkh, tables, lp)
    (y_mla, lse), (g_wu2, g_wd2) = attn_fwd(
        q, k, vv, nb, lp, comm=([b16("ffn2_w_up"), b16("ffn2_w_down")], ["gather"] * 2))
    (y_lru, hs), (g_wg2,) = lru_fwd(zl, conv_w, sh["conv_b"], wa, wx, sh["gate_a_b"], sh["gate_x_b"], sh["lru_lambda"],
                                    nb, lp, comm=([b16("ffn2_w_gate")], ["gather"]))
    wg2, wu2, wd2 = (g.reshape(D_FF, d) for g in (g_wg2, g_wu2, g_wd2))
    h2, yn = outproj_fwd(h1, y_mla, y_lru, sh["attn_out_norm"], sh["lru_out_norm"], w_out)
    dh3, u3, a3, b3, loss, g_final = ffn_fwd_loss(h2, sh["ffn2_norm"], wg2, wu2, wd2, sh["final_norm"], tgt_p, lp,
                                                  "ffn2_fwd_loss")

    vec = {"final_norm": g_final}
    (dh2, da3, db3, sh3, vec["ffn2_norm"]), _ = ffn_bwd_act(dh3, h2, sh["ffn2_norm"], a3, b3, wg2, wu2, wd2, "ffn2_bwd")
    ff_shards = (N_DEV, D_FF // N_DEV, d)
    dwg2 = tn_matmul(da3, u3, "ffn2_dwg", "bf16").reshape(ff_shards)
    dwu2 = tn_matmul(db3, u3, "ffn2_dwu", "bf16").reshape(ff_shards)
    dwd2 = tn_matmul(sh3, dh3, "ffn2_dwd", "bf16").reshape(ff_shards)

    dy_mla, dy_lru, vec["attn_out_norm"], vec["lru_out_norm"] = outproj_bwd(
        dh2, y_mla, y_lru, sh["attn_out_norm"], sh["lru_out_norm"], w_out)
    dw_out = tn_matmul(yn, dh2, "dw_out", "bf16").reshape(N_DEV, d // N_DEV, d)
    (du, dgate, dconv, vec["conv_b"], vec["gate_a_b"], vec["gate_x_b"], vec["lru_lambda"], dga, dgx), (r_wg2,) = lru_bwd(
        zl, hs, dy_lru, conv_w, sh["conv_b"], wa, wx, sh["gate_a_b"], sh["gate_x_b"], sh["lru_lambda"], nb, lp,
        comm=([dwg2], ["scatter"]))
    update("ffn2_w_gate", r_wg2)

    (dq, dk, dv), (r_wu2,) = attn_bwd(q, k, vv, y_mla, dy_mla, lse, nb, lp, comm=([dwu2], ["scatter"]))
    update("ffn2_w_up", r_wu2)

    (dzm, dqr, dkr, vec["q_latent_norm"], vec["kv_latent_norm"], vec["q_head_norm"], vec["k_head_norm"]), (r_wd2,) = (
        mla_prep_bwd(dq, dk, dv, zm, qn, cn, sh["q_latent_norm"], sh["kv_latent_norm"], wuq, wuk, wuv, gqh, gkh,
                     tables, lp, comm=([dwd2], ["scatter"])))
    update("ffn2_w_down", r_wd2)
    dwuq = _unslab_rows(tn_matmul(dqr, qn, "dw_uq"), D_QK).reshape(N_DEV, -1, Q_RANK)
    dwuk = _unslab_rows(tn_matmul(dkr, cn, "dw_uk"), D_NOPE).reshape(N_DEV, -1, KV_RANK)
    dwuv = tn_matmul(dv, cn, "dw_uv").reshape(N_DEV, -1, KV_RANK)
    dh1, vec["mix_norm"] = inproj_bwd(dzm, du, dgate, dh2, h1, sh["mix_norm"], wm, wl)
    dw_in = jnp.concatenate([tn_matmul(dzm, u2, "dw_in_mla")[:mla_rows], tn_matmul(du, u2, "dw_in_u"),
                             tn_matmul(dgate, u2, "dw_in_gate")], axis=0).reshape(N_DEV, -1, d)

    (dh0, da1, db1, vec["ffn1_norm"]), landed = ffn_bwd_act(
        dh1, h0, sh["ffn1_norm"], a1, b1, wg1, wu1, wd1, "ffn1_bwd", emit_sh=False,
        comm=([dw_in, dwuq, dwuk, dwuv, dw_out, split_cols(dconv, "split_conv")], ["scatter"] * 6))
    for name, r in zip(("w_in", "w_uq", "w_uk", "w_uv", "w_out", "conv_w"), landed):
        update(name, r)

    dwg1 = tn_matmul(da1, u1, "ffn1_dwg", "bf16").reshape(ff_shards)
    dwu1, (r_wg1,) = tn_matmul(db1, u1, "ffn1_dwu", "bf16", comm=([dwg1], ["scatter"]))
    dwd1, (r_wu1,) = tn_matmul(s1, dh1, "ffn1_dwd", "bf16", comm=([dwu1.reshape(ff_shards)], ["scatter"]))
    dwd1 = dwd1.reshape(ff_shards)
    dmeta = meta_grad(dh0, nb, lp)
    gates = [dga.reshape(LRU_WIDTH, LRU_BLOCK), dgx.reshape(LRU_WIDTH, LRU_BLOCK)]
    r_wd1, r_meta, r_vec, r_ga, r_gx = exchange(
        [dwd1, dmeta, pack_vectors(vec, loss)] + gates, ["scatter"] * 2 + ["gather"] * 3, "exchange_last")
    update("ffn1_w_gate", r_wg1)
    update("ffn1_w_up", r_wu1)
    update("ffn1_w_down", r_wd1)
    update("meta_tokens", r_meta)

    small = [name for name, _ in VECTORS] + GATES
    res, total_loss = adamw_small(r_vec, [r_ga, r_gx], [sh[nm] for nm in small], [m2[nm] for nm in small],
                                  [v2[nm] for nm in small])
    for i, name in enumerate(small):
        out[name] = [res[j][i] for j in range(4)]

    grad_x = dh0.reshape(nb, lp, d)[:, FIRST_FRAME:]
    loss = total_loss[0, 0]

    def as_given(a, name):
        return (a.T if name in COLUMN_SHARDED else a).reshape(w[name].shape)

    cols = [[as_given(out[name][j], name) for name in _WEIGHT_NAMES] for j in range(4)]
    return (loss, grad_x, *cols[0], *cols[1], *cols[2], *cols[3])


def kernel(x, meta_tokens, ffn1_norm, ffn1_w_gate, ffn1_w_up, ffn1_w_down, mix_norm, w_in, q_latent_norm, w_uq, kv_latent_norm, w_uk, w_uv, q_head_norm, k_head_norm, conv_w, conv_b, gate_a_w, gate_a_b, gate_x_w, gate_x_b, lru_lambda, attn_out_norm, lru_out_norm, w_out, ffn2_norm, ffn2_w_gate, ffn2_w_up, ffn2_w_down, final_norm, loss_target, m_meta_tokens, m_ffn1_norm, m_ffn1_w_gate, m_ffn1_w_up, m_ffn1_w_down, m_mix_norm, m_w_in, m_q_latent_norm, m_w_uq, m_kv_latent_norm, m_w_uk, m_w_uv, m_q_head_norm, m_k_head_norm, m_conv_w, m_conv_b, m_gate_a_w, m_gate_a_b, m_gate_x_w, m_gate_x_b, m_lru_lambda, m_attn_out_norm, m_lru_out_norm, m_w_out, m_ffn2_norm, m_ffn2_w_gate, m_ffn2_w_up, m_ffn2_w_down, m_final_norm, v_meta_tokens, v_ffn1_norm, v_ffn1_w_gate, v_ffn1_w_up, v_ffn1_w_down, v_mix_norm, v_w_in, v_q_latent_norm, v_w_uq, v_kv_latent_norm, v_w_uk, v_w_uv, v_q_head_norm, v_k_head_norm, v_conv_w, v_conv_b, v_gate_a_w, v_gate_a_b, v_gate_x_w, v_gate_x_b, v_lru_lambda, v_attn_out_norm, v_lru_out_norm, v_w_out, v_ffn2_norm, v_ffn2_w_gate, v_ffn2_w_up, v_ffn2_w_down, v_final_norm):
    args = locals()
    w = {name: args[name] for name in _WEIGHT_NAMES}
    m = {name: args["m_" + name] for name in _WEIGHT_NAMES}
    v = {name: args["v_" + name] for name in _WEIGHT_NAMES}
    return train_step(x, loss_target, w, m, v)
```

```python
import math

import jax
import jax.numpy as jnp
from jax import lax
from jax.experimental import pallas as pl
from jax.experimental.pallas import tpu as pltpu

F32 = jnp.float32
BF16 = jnp.bfloat16

D_MODEL = 1024
CHUNK = 64
CHUNK_SHIFT = 6
N_META = 16
PAD = CHUNK - N_META
FIRST_FRAME = PAD + N_META
MLA_HEADS = 4
D_NOPE = 128
D_ROPE = 64
D_QK = D_NOPE + D_ROPE
D_V = 128
HEAD_SLAB = 256
KV_RANK = 256
Q_RANK = 384
ROPE_THETA = 10000.0
LRU_WIDTH = 512
LRU_BLOCKS = 8
LRU_BLOCK = 64
LRU_TILE = 128
CONV_W = 4
C_RGLRU = 8.0
D_FF = 2816
MLA_IN = 768
EPS = 1e-6
NEG_INF = -1e30
N_DEV = 8
LANES = 128
VMEM_LIMIT = 52 * 1024 * 1024
ATTN_HEADS_PER_STEP = 2
TN_ROWS = 4224
TN_X_BYTES = 12 * 1024 * 1024
TN_Y_BYTES = 9 * 1024 * 1024 // 2

ADAM_LR = 0.001
ADAM_B1 = 0.9
ADAM_B2 = 0.999
ADAM_EPS = 1e-08
ADAM_WD = 0.01
ADAM_STEP = 10

VMEM_WHOLE = pl.BlockSpec(memory_space=pltpu.VMEM)
HBM_WHOLE = pl.BlockSpec(memory_space=pl.ANY)


def _params(sems):
    if sems is None:
        return pltpu.CompilerParams(vmem_limit_bytes=VMEM_LIMIT)
    return pltpu.CompilerParams(dimension_semantics=sems, vmem_limit_bytes=VMEM_LIMIT)


def _tile(n, cap, mult=16):
    best = None
    for t in range(mult, min(n, cap) + 1, mult):
        if n % t == 0:
            best = t
    assert best is not None, (n, cap, mult)
    return best


def _row(tm, d):
    return pl.BlockSpec((tm, d), lambda i: (i, 0))


def _fixed(shape):
    return pl.BlockSpec(shape, lambda i: (0,) * len(shape))


def _mesh_position():
    return lax.axis_index("x"), lax.axis_index("y"), lax.axis_index("c")


def _flat_index(x, y, c):
    return 4 * x + 2 * y + c


def _peers(x, y, c):
    out = []
    for k in range(1, N_DEV):
        fx, fy, fc = (k >> 2) & 1, (k >> 1) & 1, k & 1
        out.append((1 - x if fx else x, 1 - y if fy else y, 1 - c if fc else c))
    return out


def _comm_out_shapes(srcs, modes):
    return [jax.ShapeDtypeStruct((N_DEV,) + s.shape if md == "gather" else s.shape, s.dtype)
            for s, md in zip(srcs, modes)]


def _comm_scratch(n):
    per_peer = n * (N_DEV - 1)
    return [pltpu.SemaphoreType.DMA((per_peer,)), pltpu.SemaphoreType.DMA((per_peer,)), pltpu.SemaphoreType.DMA((n,))]


class _Copies:
    def __init__(self, own, first, relay):
        self.own, self.first, self.relay = own, first, relay

    def start(self):
        for cp in self.own + self.first:
            cp.start()

    def forward(self):
        for arrival, onward in self.relay:
            arrival.wait_recv()
            onward.start()

    def finish(self):
        arrivals = [a for a, _ in self.relay]
        onward = [f for _, f in self.relay]
        for cp in self.first + onward:
            if not any(cp is a for a in arrivals):
                cp.wait_recv()
        for cp in self.first + onward:
            cp.wait_send()
        for cp in self.own:
            cp.wait()


def _comm_copies(src_refs, dst_refs, modes, send, recv, local):
    x, y, c = _mesh_position()
    me = _flat_index(x, y, c)
    n = len(modes)
    sibling = (x, y, 1 - c)
    chips = [(1 - x, y), (x, 1 - y), (1 - x, 1 - y)]

    def remote(src, dst, k, t, to):
        return pltpu.make_async_remote_copy(src_ref=src, dst_ref=dst, send_sem=send.at[k * n + t],
                                            recv_sem=recv.at[k * n + t], device_id=to,
                                            device_id_type=pl.DeviceIdType.MESH)

    own, first, relay = [], [], []
    for t, (src, dst, md) in enumerate(zip(src_refs, dst_refs, modes)):
        if md == "scatter":
            own.append(pltpu.make_async_copy(src.at[me], dst.at[me], local.at[t]))
            for k, peer in enumerate(_peers(x, y, c)):
                first.append(remote(src.at[_flat_index(*peer)], dst.at[me], k, t, peer))
        else:
            own.append(pltpu.make_async_copy(src, dst.at[me], local.at[t]))
            first.append(remote(src, dst.at[me], 0, t, sibling))
            for j, chip in enumerate(chips):
                arrival = remote(src, dst.at[me], 1 + j, t, (*chip, c))
                landed = dst.at[_flat_index(*chip, c)]
                first.append(arrival)
                relay.append((arrival, remote(landed, landed, 4 + j, t, sibling)))
    return _Copies(own, first, relay)


def _hosted(body, n_in, n_out, modes, grid):
    t = len(modes)
    total = math.prod(grid)

    def wrapped(*refs):
        ins, csrc = refs[:n_in], refs[n_in:n_in + t]
        outs = refs[n_in + t:n_in + t + n_out]
        cdst = refs[n_in + t + n_out:n_in + 2 * t + n_out]
        scratch = refs[n_in + 2 * t + n_out:-3]
        copies = _comm_copies(csrc, cdst, modes, *refs[-3:])
        step = pl.program_id(0)
        for axis in range(1, len(grid)):
            step = step * grid[axis] + pl.program_id(axis)

        @pl.when(step == 0)
        def _():
            copies.start()

        body(*ins, *outs, *scratch)

        @pl.when(step == (total * 3) // 5)
        def _():
            copies.forward()

        @pl.when(step == total - 1)
        def _():
            copies.finish()

    return wrapped


def _call(body, name, grid, in_specs, out_specs, out_shape, sems, args, scratch=(), comm=None):
    if comm is None:
        outs = pl.pallas_call(body, name=name, grid=grid, in_specs=in_specs, out_specs=out_specs, out_shape=out_shape,
                              scratch_shapes=list(scratch), compiler_params=_params(sems))(*args)
        return outs, []
    srcs, modes = comm
    n = len(modes)
    res = pl.pallas_call(
        _hosted(body, len(in_specs), len(out_specs), modes, grid), name=name, grid=grid,
        in_specs=list(in_specs) + [HBM_WHOLE] * n, out_specs=list(out_specs) + [HBM_WHOLE] * n,
        out_shape=list(out_shape) + _comm_out_shapes(srcs, modes),
        scratch_shapes=list(scratch) + _comm_scratch(n),
        compiler_params=_params(("arbitrary",) * len(grid)))(*args, *srcs)
    return res[:len(out_specs)], res[len(out_specs):]


def exchange(srcs, modes, name):
    n = len(modes)

    def body(*refs):
        copies = _comm_copies(refs[:n], refs[n:2 * n], modes, *refs[2 * n:])
        copies.start()
        copies.forward()
        copies.finish()

    return pl.pallas_call(body, name=name, in_specs=[HBM_WHOLE] * n, out_specs=[HBM_WHOLE] * n,
                          out_shape=_comm_out_shapes(srcs, modes), scratch_shapes=_comm_scratch(n))(*srcs)


def _nn(a, b):
    return jnp.dot(a, b, preferred_element_type=F32)


def _nt(a, b):
    return lax.dot_general(a, b, (((1,), (1,)), ((), ())), preferred_element_type=F32)


def _tn(a, b):
    return lax.dot_general(a, b, (((0,), (0,)), ((), ())), preferred_element_type=F32)


def _sig(x):
    return 1.0 / (1.0 + jnp.exp(-x))


def _rms_r(x, n=None):
    n = x.shape[-1] if n is None else n
    return lax.rsqrt(jnp.sum(x * x, axis=-1, keepdims=True) * (1.0 / n) + EPS)


def _rms_bwd(x, r, g, dy, n=None):
    n = x.shape[-1] if n is None else n
    xhat = x * r
    dxhat = dy * g
    dx = r * (dxhat - xhat * (jnp.sum(dxhat * xhat, axis=-1, keepdims=True) * (1.0 / n)))
    return dx, jnp.sum(dy * xhat, axis=0, keepdims=True)


def _accumulate(ref, val, first):
    @pl.when(first)
    def _():
        ref[...] = val

    @pl.when(jnp.logical_not(first))
    def _():
        ref[...] += val


_GELU_C = math.sqrt(2.0 / math.pi)


def _gelu_and_grad(x):
    inner = _GELU_C * (x + 0.044715 * x * x * x)
    t = jnp.tanh(inner)
    gelu = 0.5 * x * (1.0 + t)
    dgelu = 0.5 * (1.0 + t) + 0.5 * x * (1.0 - t * t) * _GELU_C * (1.0 + 3.0 * 0.044715 * x * x)
    return gelu, dgelu


def _log1p_small(t):
    return jnp.where(t < 1e-3, t * (1.0 - t * (0.5 - t * (1.0 / 3.0))), jnp.log(1.0 + t))


def _softplus(x):
    return jnp.maximum(x, 0.0) + _log1p_small(jnp.exp(-jnp.abs(x)))


def _sig_tanh(x):
    return 0.5 + 0.5 * jnp.tanh(0.5 * x)


def _ff_chunks(f):
    return 2 if (f // 2) % LANES == 0 else 1


def _swiglu_half(x, g_ref, wg_ref, wu_ref, wd_ref, a_ref, b_ref, fc):
    f = wg_ref.shape[0]
    u = (x * _rms_r(x) * g_ref[...]).astype(BF16)
    acc = jnp.zeros(x.shape, F32)
    for c in range(f // fc):
        cols = slice(c * fc, (c + 1) * fc)
        a = _nt(u, wg_ref[cols, :])
        b = _nt(u, wu_ref[cols, :])
        s = (a * _sig(a) * b).astype(BF16)
        acc = acc + _nn(s, wd_ref[cols, :])
        a_ref[:, cols] = a.astype(BF16)
        b_ref[:, cols] = b.astype(BF16)
    return x + 0.5 * acc, u


def ffn_up(h, g, wg, wu, name, comm=None):
    n, d = h.shape
    f = wg.shape[0]
    tm = _tile(n, 528)
    fc = 2 * LANES if f % (2 * LANES) == 0 else f

    def body(h_ref, g_ref, wg_ref, wu_ref, u_ref, a_ref, b_ref, s_ref):
        x = h_ref[...]
        u = (x * _rms_r(x) * g_ref[...]).astype(BF16)
        u_ref[...] = u
        for c in range(f // fc):
            cols = slice(c * fc, (c + 1) * fc)
            a = _nt(u, wg_ref[cols, :])
            b = _nt(u, wu_ref[cols, :])
            a_ref[:, cols] = a.astype(BF16)
            b_ref[:, cols] = b.astype(BF16)
            s_ref[:, cols] = (0.5 * (a * _sig(a) * b)).astype(BF16)

    wide = jax.ShapeDtypeStruct((n, f), BF16)
    return _call(
        body, name, (n // tm,),
        [_row(tm, d), _fixed((1, d)), VMEM_WHOLE, VMEM_WHOLE],
        [_row(tm, d), _row(tm, f), _row(tm, f), _row(tm, f)],
        [jax.ShapeDtypeStruct((n, d), BF16), wide, wide, wide],
        ("parallel",), (h, g, wg, wu), comm=comm)


def ffn_down(h, s, wd, name, comm=None):
    n, d = h.shape
    f = wd.shape[0]
    tm = _tile(n, 528)

    def body(h_ref, s_ref, wd_ref, ho_ref):
        ho_ref[...] = h_ref[...] + _nn(s_ref[...], wd_ref[...])

    (ho,), landed = _call(body, name, (n // tm,), [_row(tm, d), _row(tm, f), VMEM_WHOLE], [_row(tm, d)],
                          [jax.ShapeDtypeStruct((n, d), F32)], ("parallel",), (h, s, wd), comm=comm)
    return ho, landed


def ffn_fwd_loss(h, g, wg, wu, wd, g_final, tgt, lp, name):
    n, d = h.shape
    f = wg.shape[0]
    tm = _tile(lp, 528)
    per_seq = lp // tm
    fc = 2 * LANES if f % (2 * LANES) == 0 else f

    def body(h_ref, g_ref, wg_ref, wu_ref, wd_ref, gf_ref, t_ref, dh_ref, u_ref, a_ref, b_ref, loss_ref, dgf_ref):
        i = pl.program_id(0)
        y, u_ref[...] = _swiglu_half(h_ref[...], g_ref, wg_ref, wu_ref, wd_ref, a_ref, b_ref, fc)
        dh_ref[...], part, dg = _loss_and_grad(y, gf_ref[...], t_ref[...], (i % per_seq) * tm)
        _accumulate(loss_ref, jnp.broadcast_to(part, (1, LANES)), i == 0)
        _accumulate(dgf_ref, dg, i == 0)

    outs, _ = _call(
        body, name, (n // tm,),
        [_row(tm, d), _fixed((1, d)), VMEM_WHOLE, VMEM_WHOLE, VMEM_WHOLE, _fixed((1, d)), _row(tm, d)],
        [_row(tm, d), _row(tm, d), _row(tm, f), _row(tm, f), _fixed((1, LANES)), _fixed((1, d))],
        [jax.ShapeDtypeStruct((n, d), F32), jax.ShapeDtypeStruct((n, d), BF16),
         jax.ShapeDtypeStruct((n, f), BF16), jax.ShapeDtypeStruct((n, f), BF16),
         jax.ShapeDtypeStruct((1, LANES), F32), jax.ShapeDtypeStruct((1, d), F32)],
        ("arbitrary",), (h, g, wg, wu, wd, g_final, tgt))
    return outs


def ffn_bwd_act(dh, h, g, a, b, wg, wu, wd, name, comm=None, emit_sh=True):
    n, d = h.shape
    f = wg.shape[0]
    tm = _tile(n, 352 if emit_sh else 384)
    nc = _ff_chunks(f)
    fc = f // nc

    def body(dh_ref, h_ref, g_ref, a_ref, b_ref, wg_ref, wu_ref, wd_ref, dhi_ref, da_ref, db_ref, *rest):
        dg_ref = rest[-1]
        x = h_ref[...]
        dy = dh_ref[...]
        r = _rms_r(x)
        dhh = (0.5 * dy).astype(BF16)
        du = jnp.zeros((tm, d), F32)
        for c in range(nc):
            cols = slice(c * fc, (c + 1) * fc)
            ds = _nt(dhh, wd_ref[cols, :])
            av = a_ref[:, cols].astype(F32)
            bv = b_ref[:, cols].astype(F32)
            sg = _sig(av)
            sil = av * sg
            da = (ds * bv * (sg * (1.0 + av * (1.0 - sg)))).astype(BF16)
            db = (ds * sil).astype(BF16)
            da_ref[:, cols] = da
            db_ref[:, cols] = db
            if emit_sh:
                rest[0][:, cols] = (0.5 * sil * bv).astype(BF16)
            du = du + _nn(da, wg_ref[cols, :]) + _nn(db, wu_ref[cols, :])
        dx, dg = _rms_bwd(x, r, g_ref[...], du)
        dhi_ref[...] = dy + dx
        _accumulate(dg_ref, dg, pl.program_id(0) == 0)

    wide = [jax.ShapeDtypeStruct((n, f), BF16)] * (3 if emit_sh else 2)
    return _call(
        body, name, (n // tm,),
        [_row(tm, d), _row(tm, d), _fixed((1, d)), _row(tm, f), _row(tm, f), VMEM_WHOLE, VMEM_WHOLE, VMEM_WHOLE],
        [_row(tm, d)] + [_row(tm, f)] * len(wide) + [_fixed((1, d))],
        [jax.ShapeDtypeStruct((n, d), F32)] + wide + [jax.ShapeDtypeStruct((1, d), F32)],
        ("arbitrary",), (dh, h, g, a, b, wg, wu, wd), comm=comm)


def tn_matmul(x, y, name, out="f32", comm=None):
    n, k = x.shape
    m = y.shape[1]
    tm = _tile(n, TN_ROWS)
    kc, mc = k, (512 if m % 512 == 0 else m)
    while tm * kc * x.dtype.itemsize > TN_X_BYTES and kc % (2 * LANES) == 0:
        kc //= 2
    while tm * mc * y.dtype.itemsize > TN_Y_BYTES and mc % (2 * LANES) == 0:
        mc //= 2
    steps = n // tm

    def body(x_ref, y_ref, o_ref, *acc):
        i = pl.program_id(2)
        part = _tn(x_ref[...].astype(BF16), y_ref[...].astype(BF16))
        if steps == 1:
            o_ref[...] = part.astype(o_ref.dtype)
        elif out == "f32":
            _accumulate(o_ref, part, i == 0)
        else:
            _accumulate(acc[0], part, i == 0)

            @pl.when(i == steps - 1)
            def _():
                o_ref[...] = acc[0][...].astype(BF16)

    out_shape = jax.ShapeDtypeStruct((k, m), F32 if out == "f32" else BF16)
    (res,), landed = _call(
        body, name, (k // kc, m // mc, steps),
        [pl.BlockSpec((tm, kc), lambda a, b, i: (i, a)), pl.BlockSpec((tm, mc), lambda a, b, i: (i, b))],
        [pl.BlockSpec((kc, mc), lambda a, b, i: (a, b))], [out_shape], ("parallel", "parallel", "arbitrary"), (x, y),
        scratch=[pltpu.VMEM((kc, mc), F32)] if (out == "bf16" and steps > 1) else [], comm=comm)
    return (res, landed) if comm is not None else res


def inproj_fwd(h, g, wm, wl):
    n, d = h.shape
    tm = _tile(n, 352)

    def body(h_ref, g_ref, wm_ref, wl_ref, u_ref, zm_ref, zl_ref):
        x = h_ref[...]
        u = (x * _rms_r(x) * g_ref[...]).astype(BF16)
        u_ref[...] = u
        zm_ref[...] = _nt(u, wm_ref[...])
        zl_ref[...] = _nt(u, wl_ref[...])

    return pl.pallas_call(
        body, name="inproj_fwd", grid=(n // tm,),
        in_specs=[_row(tm, d), _fixed((1, d)), VMEM_WHOLE, VMEM_WHOLE],
        out_specs=[_row(tm, d), _row(tm, MLA_IN), _row(tm, 2 * LRU_WIDTH)],
        out_shape=[jax.ShapeDtypeStruct((n, d), BF16), jax.ShapeDtypeStruct((n, MLA_IN), F32),
                   jax.ShapeDtypeStruct((n, 2 * LRU_WIDTH), F32)],
        compiler_params=_params(("parallel",)),
    )(h, g, wm, wl)


def inproj_bwd(dzm, du, dgate, dh2, h, g, wm, wl):
    n, d = h.shape
    tm = _tile(n, 352)

    def body(dzm_ref, du_ref, dgt_ref, dh2_ref, h_ref, g_ref, wm_ref, wl_ref, dh_ref, dg_ref):
        x = h_ref[...]
        dun = (_nn(dzm_ref[...].astype(BF16), wm_ref[...])
               + _nn(du_ref[...].astype(BF16), wl_ref[:LRU_WIDTH, :])
               + _nn(dgt_ref[...].astype(BF16), wl_ref[LRU_WIDTH:, :]))
        dx, dg = _rms_bwd(x, _rms_r(x), g_ref[...], dun)
        dh_ref[...] = dh2_ref[...] + dx
        _accumulate(dg_ref, dg, pl.program_id(0) == 0)

    return pl.pallas_call(
        body, name="inproj_bwd", grid=(n // tm,),
        in_specs=[_row(tm, MLA_IN), _row(tm, LRU_WIDTH), _row(tm, LRU_WIDTH), _row(tm, d), _row(tm, d),
                  _fixed((1, d)), VMEM_WHOLE, VMEM_WHOLE],
        out_specs=[_row(tm, d), _fixed((1, d))],
        out_shape=[jax.ShapeDtypeStruct((n, d), F32), jax.ShapeDtypeStruct((1, d), F32)],
        compiler_params=_params(("arbitrary",)),
    )(dzm, du, dgate, dh2, h, g, wm, wl)


def _rope_tables(lp):
    pos = jnp.arange(lp, dtype=F32) - float(PAD)
    half = D_ROPE // 2
    inv_freq = ROPE_THETA ** (-jnp.arange(0, half, dtype=F32) / half)
    ang = pos[:, None] * inv_freq[None, :]
    cos, sin = jnp.cos(ang), jnp.sin(ang)
    one = jnp.ones((lp, D_NOPE), F32)
    z_nope = jnp.zeros((lp, D_NOPE), F32)
    z_half = jnp.zeros((lp, half), F32)
    z_tail = jnp.zeros((lp, HEAD_SLAB - D_QK), F32)
    cosr = jnp.concatenate([one, cos, cos, z_tail], axis=1)
    sin_up = jnp.concatenate([z_nope, z_half, sin, z_tail], axis=1)
    sin_dn = jnp.concatenate([z_nope, -sin, z_half, z_tail], axis=1)
    return cosr, sin_up, sin_dn


def _rope(x, cosr, sin_up, sin_dn):
    half = D_ROPE // 2
    return x * cosr + pltpu.roll(x, half, axis=1) * sin_up + pltpu.roll(x, HEAD_SLAB - half, axis=1) * sin_dn


def _rope_bwd(dy, cosr, sin_up, sin_dn):
    half = D_ROPE // 2
    return (dy * cosr + pltpu.roll(dy * sin_up, HEAD_SLAB - half, axis=1)
            + pltpu.roll(dy * sin_dn, half, axis=1))


def _k_rope_slab(zm_tile):
    tm = zm_tile.shape[0]
    krp = zm_tile[:, Q_RANK + KV_RANK:MLA_IN]
    return jnp.concatenate([jnp.zeros((tm, D_NOPE), F32), krp], axis=1)


def mla_prep_fwd(zm, gql, gkvl, wuq, wuk, wuv, gqh, gkh, tables, lp):
    n = zm.shape[0]
    tm = _tile(lp, 352)
    per_seq = lp // tm
    width = MLA_HEADS * HEAD_SLAB
    scale = 1.0 / math.sqrt(D_QK)

    def body(zm_ref, gql_ref, gkvl_ref, wuq_ref, wuk_ref, wuv_ref, gqh_ref, gkh_ref,
             cos_ref, up_ref, dn_ref, q_ref, k_ref, v_ref, qn_ref, cn_ref):
        z = zm_ref[...]
        cq = z[:, :Q_RANK]
        ckv = z[:, Q_RANK:Q_RANK + KV_RANK]
        qn = (cq * _rms_r(cq) * gql_ref[...]).astype(BF16)
        cn = (ckv * _rms_r(ckv) * gkvl_ref[...]).astype(BF16)
        qn_ref[...] = qn
        cn_ref[...] = cn
        q_raw = _nt(qn, wuq_ref[...])
        k_raw = _nt(cn, wuk_ref[...])
        v_ref[...] = _nt(cn, wuv_ref[...]).astype(BF16)
        kr_slab = _k_rope_slab(z)
        cosr, sin_up, sin_dn = cos_ref[...], up_ref[...], dn_ref[...]
        for hd in range(MLA_HEADS):
            cols = slice(hd * HEAD_SLAB, (hd + 1) * HEAD_SLAB)
            xq = q_raw[:, cols]
            yq = _rope(xq * _rms_r(xq, D_QK) * gqh_ref[...], cosr, sin_up, sin_dn)
            q_ref[:, cols] = (yq * scale).astype(BF16)
            xk = k_raw[:, cols] + kr_slab
            yk = _rope(xk * _rms_r(xk, D_QK) * gkh_ref[...], cosr, sin_up, sin_dn)
            k_ref[:, cols] = yk.astype(BF16)

    tab = pl.BlockSpec((tm, HEAD_SLAB), lambda i: (i % per_seq, 0))
    return pl.pallas_call(
        body, name="mla_prep_fwd", grid=(n // tm,),
        in_specs=[_row(tm, MLA_IN), _fixed((1, Q_RANK)), _fixed((1, KV_RANK)), VMEM_WHOLE, VMEM_WHOLE, VMEM_WHOLE,
                  _fixed((1, HEAD_SLAB)), _fixed((1, HEAD_SLAB)), tab, tab, tab],
        out_specs=[_row(tm, width), _row(tm, width), _row(tm, MLA_HEADS * D_V), _row(tm, Q_RANK), _row(tm, KV_RANK)],
        out_shape=[jax.ShapeDtypeStruct((n, width), BF16), jax.ShapeDtypeStruct((n, width), BF16),
                   jax.ShapeDtypeStruct((n, MLA_HEADS * D_V), BF16), jax.ShapeDtypeStruct((n, Q_RANK), BF16),
                   jax.ShapeDtypeStruct((n, KV_RANK), BF16)],
        compiler_params=_params(("parallel",)),
    )(zm, gql, gkvl, wuq, wuk, wuv, gqh, gkh, *tables)


def mla_prep_bwd(dq, dk, dv, zm, qn, cn, gql, gkvl, wuq, wuk, wuv, gqh, gkh, tables, lp, comm=None):
    n = zm.shape[0]
    tm = _tile(lp, 704)
    per_seq = lp // tm
    width = MLA_HEADS * HEAD_SLAB
    scale = 1.0 / math.sqrt(D_QK)

    def body(dq_ref, dk_ref, dv_ref, zm_ref, qn_ref, cn_ref, gql_ref, gkvl_ref, wuq_ref, wuk_ref, wuv_ref,
             gqh_ref, gkh_ref, cos_ref, up_ref, dn_ref,
             dzm_ref, dqr_ref, dkr_ref, dgql_ref, dgkvl_ref, dgqh_ref, dgkh_ref):
        z = zm_ref[...]
        cq = z[:, :Q_RANK]
        ckv = z[:, Q_RANK:Q_RANK + KV_RANK]
        q_raw = _nt(qn_ref[...], wuq_ref[...])
        k_raw = _nt(cn_ref[...], wuk_ref[...])
        kr_slab = _k_rope_slab(z)
        cosr, sin_up, sin_dn = cos_ref[...], up_ref[...], dn_ref[...]
        dgq = jnp.zeros((1, HEAD_SLAB), F32)
        dgk = jnp.zeros((1, HEAD_SLAB), F32)
        dkrp = jnp.zeros((tm, HEAD_SLAB - D_NOPE), F32)
        for hd in range(MLA_HEADS):
            cols = slice(hd * HEAD_SLAB, (hd + 1) * HEAD_SLAB)
            xq = q_raw[:, cols]
            dxn = _rope_bwd(dq_ref[:, cols] * scale, cosr, sin_up, sin_dn)
            dxq, dg = _rms_bwd(xq, _rms_r(xq, D_QK), gqh_ref[...], dxn, D_QK)
            dgq = dgq + dg
            dqr_ref[:, cols] = dxq.astype(BF16)
            xk = k_raw[:, cols] + kr_slab
            dxn = _rope_bwd(dk_ref[:, cols], cosr, sin_up, sin_dn)
            dxk, dg = _rms_bwd(xk, _rms_r(xk, D_QK), gkh_ref[...], dxn, D_QK)
            dgk = dgk + dg
            dkr_ref[:, cols] = dxk.astype(BF16)
            dkrp = dkrp + dxk[:, D_NOPE:]
        dqn = _nn(dqr_ref[...], wuq_ref[...])
        dcn = _nn(dkr_ref[...], wuk_ref[...]) + _nn(dv_ref[...].astype(BF16), wuv_ref[...])
        dcq, dg1 = _rms_bwd(cq, _rms_r(cq), gql_ref[...], dqn)
        dckv, dg2 = _rms_bwd(ckv, _rms_r(ckv), gkvl_ref[...], dcn)
        dzm_ref[:, :Q_RANK] = dcq
        dzm_ref[:, Q_RANK:Q_RANK + KV_RANK] = dckv
        dzm_ref[:, Q_RANK + KV_RANK:] = dkrp
        first = pl.program_id(0) == 0
        _accumulate(dgql_ref, dg1, first)
        _accumulate(dgkvl_ref, dg2, first)
        _accumulate(dgqh_ref, dgq, first)
        _accumulate(dgkh_ref, dgk, first)

    tab = pl.BlockSpec((tm, HEAD_SLAB), lambda i: (i % per_seq, 0))
    return _call(
        body, "mla_prep_bwd", (n // tm,),
        [_row(tm, width), _row(tm, width), _row(tm, MLA_HEADS * D_V), _row(tm, MLA_IN),
         _row(tm, Q_RANK), _row(tm, KV_RANK), _fixed((1, Q_RANK)), _fixed((1, KV_RANK)),
         VMEM_WHOLE, VMEM_WHOLE, VMEM_WHOLE, _fixed((1, HEAD_SLAB)), _fixed((1, HEAD_SLAB)), tab, tab, tab],
        [_row(tm, MLA_IN), _row(tm, width), _row(tm, width), _fixed((1, Q_RANK)), _fixed((1, KV_RANK)),
         _fixed((1, HEAD_SLAB)), _fixed((1, HEAD_SLAB))],
        [jax.ShapeDtypeStruct((n, MLA_IN), F32), jax.ShapeDtypeStruct((n, width), BF16),
         jax.ShapeDtypeStruct((n, width), BF16), jax.ShapeDtypeStruct((1, Q_RANK), F32),
         jax.ShapeDtypeStruct((1, KV_RANK), F32), jax.ShapeDtypeStruct((1, HEAD_SLAB), F32),
         jax.ShapeDtypeStruct((1, HEAD_SLAB), F32)],
        ("arbitrary",), (dq, dk, dv, zm, qn, cn, gql, gkvl, wuq, wuk, wuv, gqh, gkh, *tables), comm=comm)


def _attn_tile(lp):
    return _tile(lp, 704, CHUNK)


def _chunk_mask(i, j, t):
    qpos = i * t + lax.broadcasted_iota(jnp.int32, (t, t), 0)
    kpos = j * t + lax.broadcasted_iota(jnp.int32, (t, t), 1)
    same_or_earlier = jnp.right_shift(kpos, CHUNK_SHIFT) <= jnp.right_shift(qpos, CHUNK_SHIFT)
    return jnp.logical_and(same_or_earlier, kpos >= PAD)


def _masked_scores(s, i, j, t, diagonal):
    if diagonal:
        return jnp.where(_chunk_mask(i, j, t), s, NEG_INF)
    kpos = j * t + lax.broadcasted_iota(jnp.int32, (1, t), 1)
    return s + jnp.where(kpos < PAD, NEG_INF, 0.0)


def attn_fwd(q, k, v, nb, lp, comm=None):
    n = q.shape[0]
    t = _attn_tile(lp)
    nq = lp // t

    hp = ATTN_HEADS_PER_STEP

    def body(q_ref, k_ref, v_ref, o_ref, lse_ref):
        i = pl.program_id(2)
        qs = [q_ref[:, hh * HEAD_SLAB:(hh + 1) * HEAD_SLAB] for hh in range(hp)]

        def kv_step(j, carry, diagonal=False):
            off = pl.multiple_of(j * t, t)
            out = []
            for hh in range(hp):
                m, l, acc = carry[hh]
                kv = k_ref[pl.ds(off, t), hh * HEAD_SLAB:(hh + 1) * HEAD_SLAB]
                s = _masked_scores(_nt(qs[hh], kv), i, j, t, diagonal)
                m_new = jnp.maximum(m, jnp.max(s, axis=-1, keepdims=True))
                p = jnp.exp(s - m_new)
                alpha = jnp.exp(m - m_new)
                l = alpha * l + jnp.sum(p, axis=-1, keepdims=True)
                acc = alpha * acc + _nn(p.astype(BF16), v_ref[pl.ds(off, t), hh * D_V:(hh + 1) * D_V])
                out.append((m_new, l, acc))
            return tuple(out)

        init = tuple((jnp.full((t, 1), NEG_INF, F32), jnp.zeros((t, 1), F32), jnp.zeros((t, D_V), F32))
                     for _ in range(hp))
        done = kv_step(i, lax.fori_loop(0, i, kv_step, init), diagonal=True)
        for hh, (m, l, acc) in enumerate(done):
            o_ref[:, hh * D_V:(hh + 1) * D_V] = acc * (1.0 / l)
            lse_ref[hh] = jnp.broadcast_to(m + jnp.log(l), (t, LANES))

    return _call(
        body, "attn_fwd", (nb, MLA_HEADS // hp, nq),
        [pl.BlockSpec((t, hp * HEAD_SLAB), lambda b, h, i: (b * nq + i, h)),
         pl.BlockSpec((lp, hp * HEAD_SLAB), lambda b, h, i: (b, h)),
         pl.BlockSpec((lp, hp * D_V), lambda b, h, i: (b, h))],
        [pl.BlockSpec((t, hp * D_V), lambda b, h, i: (b * nq + i, h)),
         pl.BlockSpec((hp, t, LANES), lambda b, h, i: (h, b * nq + i, 0))],
        [jax.ShapeDtypeStruct((n, MLA_HEADS * D_V), F32), jax.ShapeDtypeStruct((MLA_HEADS, n, LANES), F32)],
        ("parallel", "parallel", "parallel"), (q, k, v), comm=comm)


def attn_bwd(q, k, v, o, do, lse, nb, lp, comm=None):
    n = q.shape[0]
    t = _attn_tile(lp)
    nq = lp // t

    def body(q_ref, k_ref, v_ref, o_ref, do_ref, lse_ref, dq_ref, dk_ref, dv_ref):
        dk_ref[...] = jnp.zeros_like(dk_ref)
        dv_ref[...] = jnp.zeros_like(dv_ref)

        def q_step(i, _):
            qoff = pl.multiple_of(i * t, t)
            qv = q_ref[pl.ds(qoff, t), :]
            dov = do_ref[pl.ds(qoff, t), :]
            delta = jnp.sum(o_ref[pl.ds(qoff, t), :] * dov, axis=-1, keepdims=True)
            lse_q = jnp.max(lse_ref[0, pl.ds(qoff, t), :], axis=-1, keepdims=True)
            do16 = dov.astype(BF16)

            def kv_step(j, dq_acc, diagonal=False):
                koff = pl.multiple_of(j * t, t)
                kv = k_ref[pl.ds(koff, t), :]
                s = _masked_scores(_nt(qv, kv), i, j, t, diagonal)
                p = jnp.exp(s - lse_q)
                dp = _nt(do16, v_ref[pl.ds(koff, t), :])
                ds16 = (p * (dp - delta)).astype(BF16)
                dv_ref[pl.ds(koff, t), :] += _tn(p.astype(BF16), do16)
                dk_ref[pl.ds(koff, t), :] += _tn(ds16, qv)
                return dq_acc + _nn(ds16, kv)

            earlier = lax.fori_loop(0, i, kv_step, jnp.zeros((t, HEAD_SLAB), F32))
            dq_ref[pl.ds(qoff, t), :] = kv_step(i, earlier, diagonal=True)
            return 0

        lax.fori_loop(0, nq, q_step, 0)

    wide = pl.BlockSpec((lp, HEAD_SLAB), lambda b, h: (b, h))
    thin = pl.BlockSpec((lp, D_V), lambda b, h: (b, h))
    width = MLA_HEADS * HEAD_SLAB
    return _call(
        body, "attn_bwd", (nb, MLA_HEADS),
        [wide, wide, thin, thin, thin, pl.BlockSpec((1, lp, LANES), lambda b, h: (h, b, 0))],
        [wide, wide, thin],
        [jax.ShapeDtypeStruct((n, width), F32), jax.ShapeDtypeStruct((n, width), F32),
         jax.ShapeDtypeStruct((n, MLA_HEADS * D_V), F32)],
        ("parallel", "parallel"), (q, k, v, o, do, lse), comm=comm)


def _seq_rows(nb, lp, width):
    rows = lax.broadcasted_iota(jnp.int32, (lp, width), 0)
    return jnp.concatenate([rows] * nb, axis=0) if nb > 1 else rows


def _lru_gates(u, w_ref, cb, wa, wx, ba, bx, lam):
    xc = (cb + w_ref[pl.ds(3, 1), :] * u + w_ref[pl.ds(2, 1), :] * pltpu.roll(u, 1, axis=0)
          + w_ref[pl.ds(1, 1), :] * pltpu.roll(u, 2, axis=0) + w_ref[pl.ds(0, 1), :] * pltpu.roll(u, 3, axis=0))
    xc16 = xc.astype(BF16)
    ra = _sig_tanh(_nn(xc16, wa) + ba)
    ia = _sig_tanh(_nn(xc16, wx) + bx)
    sp = _softplus(-lam)
    log_a = -C_RGLRU * ra * sp
    a = jnp.exp(log_a)
    x2 = 2.0 * log_a
    mult = jnp.sqrt(jnp.where(x2 > -1e-2, -x2 * (1.0 + x2 * (0.5 + x2 * (1.0 / 6.0))), 1.0 - a * a))
    return xc, xc16, ra, ia, sp, a, mult


def _scan_block_rows(width):
    return lax.broadcasted_iota(jnp.int32, (8, width), 0)


def lru_fwd(zl, conv_w, conv_b, wa, wx, ba, bx, lam, nb, lp, comm=None):
    n = zl.shape[0]
    w = LRU_TILE
    nt = LRU_WIDTH // w
    nblk = lp // 8

    def body(u_ref, gt_ref, cw_ref, cb_ref, wa_ref, wx_ref, ba_ref, bx_ref, lam_ref, y_ref, h_ref, a_s, b_s):
        u = u_ref[...]
        xc, _, _, ia, _, a, mult = _lru_gates(u, cw_ref, cb_ref[...], wa_ref[...], wx_ref[...],
                                              ba_ref[...], bx_ref[...], lam_ref[...])
        row = _seq_rows(nb, lp, w)
        mult = jnp.where(row == PAD, 1.0, mult)
        a_s[...] = a
        b_s[...] = jnp.where(row < PAD, 0.0, mult * (ia * xc))
        r8 = _scan_block_rows(w)

        def blk(i, carry):
            out = []
            for s_id in range(nb):
                off = pl.multiple_of(s_id * lp + i * 8, 8)
                av = a_s[pl.ds(off, 8), :]
                bv = b_s[pl.ds(off, 8), :]
                for sh in (1, 2, 4):
                    keep = r8 >= sh
                    bv = jnp.where(keep, av * pltpu.roll(bv, sh, axis=0) + bv, bv)
                    av = jnp.where(keep, av * pltpu.roll(av, sh, axis=0), av)
                hv = bv + av * carry[s_id]
                h_ref[pl.ds(off, 8), :] = hv
                out.append(jnp.sum(jnp.where(r8 == 7, hv, 0.0), axis=0, keepdims=True))
            return tuple(out)

        lax.fori_loop(0, nblk, blk, tuple(jnp.zeros((1, w), F32) for _ in range(nb)))
        gelu, _ = _gelu_and_grad(gt_ref[...])
        y_ref[...] = h_ref[...] * gelu

    col = lambda c: (0, c)
    return _call(
        body, "lru_fwd", (nt,),
        [pl.BlockSpec((n, w), col), pl.BlockSpec((n, w), lambda c: (0, nt + c)),
         pl.BlockSpec((CONV_W, w), col), pl.BlockSpec((1, w), col),
         pl.BlockSpec((w, w), lambda c: (c, c)), pl.BlockSpec((w, w), lambda c: (c, c)),
         pl.BlockSpec((1, w), col), pl.BlockSpec((1, w), col), pl.BlockSpec((1, w), col)],
        [pl.BlockSpec((n, w), col), pl.BlockSpec((n, w), col)],
        [jax.ShapeDtypeStruct((n, LRU_WIDTH), F32), jax.ShapeDtypeStruct((n, LRU_WIDTH), F32)],
        ("parallel",), (zl, zl, conv_w, conv_b, wa, wx, ba, bx, lam),
        scratch=[pltpu.VMEM((n, w), F32), pltpu.VMEM((n, w), F32)], comm=comm)


def lru_bwd(zl, hs, dy, conv_w, conv_b, wa, wx, ba, bx, lam, nb, lp, comm=None):
    n = zl.shape[0]
    w = LRU_TILE
    nt = LRU_WIDTH // w
    nblk = lp // 8

    def body(u_ref, gt_ref, h_ref, dy_ref, cw_ref, cb_ref, wa_ref, wx_ref, ba_ref, bx_ref, lam_ref,
             du_ref, dgt_ref, dcw_ref, dcb_ref, dba_ref, dbx_ref, dlam_ref, dwa_ref, dwx_ref,
             c_s, d_s, g_s, dwa_s, dwx_s):
        u = u_ref[...]
        lam = lam_ref[...]
        xc, xc16, ra, ia, sp, a, mult = _lru_gates(u, cw_ref, cb_ref[...], wa_ref[...], wx_ref[...],
                                                   ba_ref[...], bx_ref[...], lam)
        row = lax.broadcasted_iota(jnp.int32, (lp, w), 0)
        hv = h_ref[...]
        dyv = dy_ref[...]
        gelu, dgelu = _gelu_and_grad(gt_ref[...])
        dgt_ref[...] = jnp.where(row >= PAD, dyv * hv * dgelu, 0.0)
        c_s[...] = pltpu.roll(a, lp - 1, axis=0)
        d_s[...] = dyv * gelu
        r8 = _scan_block_rows(w)

        def blk(ii, carry):
            off = pl.multiple_of((nblk - 1 - ii) * 8, 8)
            cv = c_s[pl.ds(off, 8), :]
            dv = d_s[pl.ds(off, 8), :]
            for sh in (1, 2, 4):
                keep = r8 < 8 - sh
                dv = jnp.where(keep, cv * pltpu.roll(dv, 8 - sh, axis=0) + dv, dv)
                cv = jnp.where(keep, cv * pltpu.roll(cv, 8 - sh, axis=0), cv)
            gv = dv + cv * carry
            g_s[pl.ds(off, 8), :] = gv
            return jnp.sum(jnp.where(r8 == 0, gv, 0.0), axis=0, keepdims=True)

        lax.fori_loop(0, nblk, blk, jnp.zeros((1, w), F32))
        gv = g_s[...]
        first_row = row == PAD
        db = jnp.where(row >= PAD, gv, 0.0)
        da = jnp.where(row > PAD, gv * pltpu.roll(hv, 1, axis=0), 0.0)
        mult_eff = jnp.where(first_row, 1.0, mult)
        dmult = jnp.where(first_row, 0.0, db * (ia * xc))
        dia = db * mult_eff * xc
        dxc = db * mult_eff * ia
        dla = da * a - dmult * (a * a) / mult
        dra = dla * (-C_RGLRU * sp)
        dsp = jnp.sum(dla * (-C_RGLRU * ra), axis=0, keepdims=True)
        dpa = dra * ra * (1.0 - ra)
        dpx = dia * ia * (1.0 - ia)
        dpa16 = dpa.astype(BF16)
        dpx16 = dpx.astype(BF16)
        dxc = dxc + _nt(dpa16, wa_ref[...]) + _nt(dpx16, wx_ref[...])
        du = cw_ref[pl.ds(CONV_W - 1, 1), :] * dxc
        dcw = [jnp.sum(dxc * u, axis=0, keepdims=True)]
        for tap in range(1, CONV_W):
            dcw.insert(0, jnp.sum(dxc * pltpu.roll(u, tap, axis=0), axis=0, keepdims=True))
            du = du + cw_ref[pl.ds(CONV_W - 1 - tap, 1), :] * pltpu.roll(dxc, lp - tap, axis=0)
        du_ref[...] = jnp.where(row >= PAD, du, 0.0)
        first = pl.program_id(1) == 0
        _accumulate(dlam_ref, -_sig(-lam) * dsp, first)
        _accumulate(dba_ref, jnp.sum(dpa, axis=0, keepdims=True), first)
        _accumulate(dbx_ref, jnp.sum(dpx, axis=0, keepdims=True), first)
        _accumulate(dcb_ref, jnp.sum(dxc, axis=0, keepdims=True), first)
        _accumulate(dcw_ref, jnp.concatenate(dcw, axis=0), first)
        _accumulate(dwa_s, _tn(xc16, dpa16), first)
        _accumulate(dwx_s, _tn(xc16, dpx16), first)

        @pl.when(pl.program_id(1) == nb - 1)
        def _():
            for j in range(w // LRU_BLOCK):
                blk_rows = slice(j * LRU_BLOCK, (j + 1) * LRU_BLOCK)
                dwa_ref[0, blk_rows, :] = dwa_s[blk_rows, blk_rows]
                dwx_ref[0, blk_rows, :] = dwx_s[blk_rows, blk_rows]

    col = lambda c, b: (0, c)
    vec = pl.BlockSpec((1, w), col)
    mat = pl.BlockSpec((w, w), lambda c, b: (c, c))
    big = pl.BlockSpec((lp, w), lambda c, b: (b, c))
    dmat = pl.BlockSpec((1, w, LRU_BLOCK), lambda c, b: (c, 0, 0))
    return _call(
        body, "lru_bwd", (nt, nb),
        [big, pl.BlockSpec((lp, w), lambda c, b: (b, nt + c)), big, big,
         pl.BlockSpec((CONV_W, w), col), vec, mat, mat, vec, vec, vec],
        [big, big, pl.BlockSpec((CONV_W, w), col), vec, vec, vec, vec, dmat, dmat],
        [jax.ShapeDtypeStruct((n, LRU_WIDTH), F32), jax.ShapeDtypeStruct((n, LRU_WIDTH), F32),
         jax.ShapeDtypeStruct((CONV_W, LRU_WIDTH), F32), jax.ShapeDtypeStruct((1, LRU_WIDTH), F32),
         jax.ShapeDtypeStruct((1, LRU_WIDTH), F32), jax.ShapeDtypeStruct((1, LRU_WIDTH), F32),
         jax.ShapeDtypeStruct((1, LRU_WIDTH), F32), jax.ShapeDtypeStruct((nt, w, LRU_BLOCK), F32),
         jax.ShapeDtypeStruct((nt, w, LRU_BLOCK), F32)],
        ("parallel", "arbitrary"), (zl, zl, hs, dy, conv_w, conv_b, wa, wx, ba, bx, lam),
        scratch=[pltpu.VMEM((lp, w), F32), pltpu.VMEM((lp, w), F32), pltpu.VMEM((lp, w), F32),
                 pltpu.VMEM((w, w), F32), pltpu.VMEM((w, w), F32)], comm=comm)


def outproj_fwd(h, ya, yl, gao, glo, wout):
    n, d = h.shape
    half = ya.shape[1]
    tm = _tile(n, 704)

    def body(h_ref, ya_ref, yl_ref, gao_ref, glo_ref, w_ref, ho_ref, yn_ref):
        xa = ya_ref[...]
        xl = yl_ref[...]
        na = (xa * _rms_r(xa) * gao_ref[...]).astype(BF16)
        nl = (xl * _rms_r(xl) * glo_ref[...]).astype(BF16)
        yn_ref[:, :half] = na
        yn_ref[:, half:] = nl
        ho_ref[...] = h_ref[...] + _nn(na, w_ref[:half, :]) + _nn(nl, w_ref[half:, :])

    return pl.pallas_call(
        body, name="outproj_fwd", grid=(n // tm,),
        in_specs=[_row(tm, d), _row(tm, half), _row(tm, half), _fixed((1, half)), _fixed((1, half)), VMEM_WHOLE],
        out_specs=[_row(tm, d), _row(tm, 2 * half)],
        out_shape=[jax.ShapeDtypeStruct((n, d), F32), jax.ShapeDtypeStruct((n, 2 * half), BF16)],
        compiler_params=_params(("parallel",)),
    )(h, ya, yl, gao, glo, wout)


def outproj_bwd(dh, ya, yl, gao, glo, wout):
    n, d = dh.shape
    half = ya.shape[1]
    tm = _tile(n, 704)

    def body(dh_ref, ya_ref, yl_ref, gao_ref, glo_ref, w_ref, dya_ref, dyl_ref, dgao_ref, dglo_ref):
        d16 = dh_ref[...].astype(BF16)
        xa = ya_ref[...]
        xl = yl_ref[...]
        dxa, dga = _rms_bwd(xa, _rms_r(xa), gao_ref[...], _nt(d16, w_ref[:half, :]))
        dxl, dgl = _rms_bwd(xl, _rms_r(xl), glo_ref[...], _nt(d16, w_ref[half:, :]))
        dya_ref[...] = dxa
        dyl_ref[...] = dxl
        first = pl.program_id(0) == 0
        _accumulate(dgao_ref, dga, first)
        _accumulate(dglo_ref, dgl, first)

    return pl.pallas_call(
        body, name="outproj_bwd", grid=(n // tm,),
        in_specs=[_row(tm, d), _row(tm, half), _row(tm, half), _fixed((1, half)), _fixed((1, half)), VMEM_WHOLE],
        out_specs=[_row(tm, half), _row(tm, half), _fixed((1, half)), _fixed((1, half))],
        out_shape=[jax.ShapeDtypeStruct((n, half), F32), jax.ShapeDtypeStruct((n, half), F32),
                   jax.ShapeDtypeStruct((1, half), F32), jax.ShapeDtypeStruct((1, half), F32)],
        compiler_params=_params(("arbitrary",)),
    )(dh, ya, yl, gao, glo, wout)


def _loss_and_grad(x, gv, tgt, first_row):
    tm, d = x.shape
    r = _rms_r(x)
    row = first_row + lax.broadcasted_iota(jnp.int32, (tm, d), 0)
    diff = jnp.where(row >= FIRST_FRAME, x * r * gv - tgt, 0.0)
    part = 0.5 * jnp.sum(jnp.sum(diff * diff, axis=-1, keepdims=True) * (1.0 / d), axis=0, keepdims=True)
    dx, dg = _rms_bwd(x, r, gv, diff * (1.0 / d))
    return dx, part, dg


def assemble_cols(g, name):
    _, k, ns = g.shape

    def body(g_ref, o_ref):
        for j in range(N_DEV):
            o_ref[:, j * ns:(j + 1) * ns] = g_ref[j]

    return pl.pallas_call(body, name=name, out_shape=jax.ShapeDtypeStruct((k, N_DEV * ns), g.dtype),
                          compiler_params=_params(None))(g)


def split_cols(x, name):
    k, cols = x.shape
    ns = cols // N_DEV

    def body(x_ref, o_ref):
        for j in range(N_DEV):
            o_ref[j] = x_ref[:, j * ns:(j + 1) * ns]

    return pl.pallas_call(body, name=name, out_shape=jax.ShapeDtypeStruct((N_DEV, k, ns), x.dtype),
                          compiler_params=_params(None))(x)


def _slab_rows(w, per_head):
    k = w.shape[1]
    w = w.reshape(MLA_HEADS, per_head, k)
    return jnp.pad(w, ((0, 0), (0, HEAD_SLAB - per_head), (0, 0))).reshape(MLA_HEADS * HEAD_SLAB, k)


def _unslab_rows(w, per_head):
    k = w.shape[1]
    return w.reshape(MLA_HEADS, HEAD_SLAB, k)[:, :per_head].reshape(MLA_HEADS * per_head, k)


def meta_grad(dh0, nb, lp):
    d = dh0.shape[1]
    ns = d // N_DEV
    per_seq = lp // N_META

    def body(x_ref, o_ref):
        x = x_ref[...]
        for j in range(N_DEV):
            _accumulate(o_ref.at[j], x[:, j * ns:(j + 1) * ns], pl.program_id(0) == 0)

    return pl.pallas_call(
        body, name="meta_grad", grid=(nb,),
        in_specs=[pl.BlockSpec((N_META, d), lambda b: (b * per_seq + PAD // N_META, 0))],
        out_specs=pl.BlockSpec((N_DEV, N_META, ns), lambda b: (0, 0, 0)),
        out_shape=jax.ShapeDtypeStruct((N_DEV, N_META, ns), F32),
        compiler_params=_params(("arbitrary",)))(dh0)


VECTORS = [("ffn1_norm", 1024), ("mix_norm", 1024), ("q_latent_norm", 384), ("kv_latent_norm", 256),
           ("q_head_norm", 192), ("k_head_norm", 192), ("conv_b", 512), ("gate_a_b", 512), ("gate_x_b", 512),
           ("lru_lambda", 512), ("attn_out_norm", 512), ("lru_out_norm", 512), ("ffn2_norm", 1024),
           ("final_norm", 1024)]
VEC_ROWS = 16
LOSS_ROW = len(VECTORS)
GATES = ["gate_a_w", "gate_x_w"]


def pack_vectors(grads, loss):
    def body(*refs):
        o_ref = refs[-1]
        o_ref[...] = jnp.zeros_like(o_ref)
        for t, (ref, (_, cnt)) in enumerate(zip(refs[:-2], VECTORS)):
            o_ref[t:t + 1, :cnt] = ref[:, :cnt]
        o_ref[LOSS_ROW:LOSS_ROW + 1, :LANES] = refs[-2][...]

    return pl.pallas_call(body, name="pack_vectors", out_shape=jax.ShapeDtypeStruct((VEC_ROWS, D_MODEL), F32),
                          compiler_params=_params(None))(*[grads[name] for name, _ in VECTORS], loss)


def _adamw_update(w, g, m, v):
    c1 = 1.0 / (1.0 - ADAM_B1 ** ADAM_STEP)
    c2 = 1.0 / (1.0 - ADAM_B2 ** ADAM_STEP)
    mn = ADAM_B1 * m + (1.0 - ADAM_B1) * g
    vn = ADAM_B2 * v + (1.0 - ADAM_B2) * (g * g)
    delta = -ADAM_LR * ((mn * c1) / (jnp.sqrt(vn * c2) + ADAM_EPS) + ADAM_WD * w)
    return delta, mn, vn


def _sum_slots(ref, index=()):
    acc = ref[(0,) + index].astype(F32)
    for s in range(1, N_DEV):
        acc = acc + ref[(s,) + index].astype(F32)
    return acc


def adamw_sharded(r, w, m, v, name):
    rows, cols = w.shape
    tr = _tile(rows, 256, 16) if rows % 16 == 0 else rows

    def body(r_ref, w_ref, m_ref, v_ref, g_ref, d_ref, mo_ref, vo_ref):
        g = _sum_slots(r_ref)
        g_ref[...] = g
        d_ref[...], mo_ref[...], vo_ref[...] = _adamw_update(w_ref[...], g, m_ref[...], v_ref[...])

    spec = pl.BlockSpec((tr, cols), lambda i: (i, 0))
    shape = jax.ShapeDtypeStruct((rows, cols), F32)
    return pl.pallas_call(
        body, name=name, grid=(rows // tr,),
        in_specs=[pl.BlockSpec((N_DEV, tr, cols), lambda i: (0, i, 0))] + [spec] * 3,
        out_specs=[spec] * 4, out_shape=[shape] * 4,
        compiler_params=_params(("parallel",)),
    )(r, w, m, v)


def adamw_small(r_vec, r_gates, w, m, v):
    nt = len(VECTORS) + len(GATES)

    def body(*refs):
        rv_ref = refs[0]
        rg_refs = refs[1:1 + len(GATES)]
        base = 1 + len(GATES)
        w_refs, m_refs, v_refs = (refs[base + i * nt:base + (i + 1) * nt] for i in range(3))
        outs = refs[base + 3 * nt:]
        g_o, d_o, m_o, v_o = (outs[i * nt:(i + 1) * nt] for i in range(4))
        outs[4 * nt][...] = _sum_slots(rv_ref, (slice(LOSS_ROW, LOSS_ROW + 1), slice(0, LANES)))
        for t in range(nt):
            if t < len(VECTORS):
                cnt = VECTORS[t][1]
                g = _sum_slots(rv_ref, (slice(t, t + 1), slice(0, cnt)))
            else:
                g = _sum_slots(rg_refs[t - len(VECTORS)])
            g_o[t][...] = g
            d_o[t][...], m_o[t][...], v_o[t][...] = _adamw_update(w_refs[t][...], g, m_refs[t][...], v_refs[t][...])

    shapes = [jax.ShapeDtypeStruct(a.shape, F32) for a in w]
    res = pl.pallas_call(body, name="adamw_small", out_shape=shapes * 4 + [jax.ShapeDtypeStruct((1, LANES), F32)],
                         compiler_params=_params(None))(r_vec, *r_gates, *w, *m, *v)
    return [res[i * nt:(i + 1) * nt] for i in range(4)], res[4 * nt]


def _block_diag(w):
    nb, n, _ = w.shape
    eye = jnp.eye(nb, dtype=w.dtype)
    return (eye[:, None, :, None] * w[:, :, None, :]).reshape(nb * n, nb * n)


def _two_d(a):
    if a.ndim == 3:
        return a.reshape(a.shape[1], a.shape[2])
    if a.ndim == 4:
        return a.reshape(a.shape[1] * a.shape[2], a.shape[3])
    return a


_WEIGHT_NAMES = ['meta_tokens', 'ffn1_norm', 'ffn1_w_gate', 'ffn1_w_up', 'ffn1_w_down', 'mix_norm', 'w_in',
                 'q_latent_norm', 'w_uq', 'kv_latent_norm', 'w_uk', 'w_uv', 'q_head_norm', 'k_head_norm', 'conv_w',
                 'conv_b', 'gate_a_w', 'gate_a_b', 'gate_x_w', 'gate_x_b', 'lru_lambda', 'attn_out_norm',
                 'lru_out_norm', 'w_out', 'ffn2_norm', 'ffn2_w_gate', 'ffn2_w_up', 'ffn2_w_down', 'final_norm']


COLUMN_SHARDED = ("ffn1_w_gate", "ffn1_w_up", "ffn2_w_gate", "ffn2_w_up", "w_in", "w_uq", "w_uk", "w_uv")


def train_step(x, tgt, w, m, v):
    nb, seq, d = x.shape
    lp = PAD + N_META + seq
    n = nb * lp
    def local(a, name):
        a = _two_d(a)
        return a.T if name in COLUMN_SHARDED else a

    sh = {name: local(w[name], name) for name in _WEIGHT_NAMES}
    m2 = {name: local(m[name], name) for name in _WEIGHT_NAMES}
    v2 = {name: local(v[name], name) for name in _WEIGHT_NAMES}

    def b16(name):
        return sh[name].astype(BF16)

    out = {}

    def update(name, landed):
        out[name] = adamw_sharded(landed, sh[name], m2[name], v2[name], "adamw_" + name)

    g_wg1, g_wu1, g_meta, g_conv = exchange(
        [b16("ffn1_w_gate"), b16("ffn1_w_up"), sh["meta_tokens"], sh["conv_w"]], ["gather"] * 4, "gather_ffn1")
    wg1, wu1 = g_wg1.reshape(D_FF, d), g_wu1.reshape(D_FF, d)
    meta = assemble_cols(g_meta, "assemble_meta")
    conv_w = assemble_cols(g_conv, "assemble_conv")

    front = jnp.concatenate([jnp.zeros((PAD, d), F32), meta], axis=0)
    h0 = jnp.concatenate([jnp.broadcast_to(front[None], (nb, FIRST_FRAME, d)), x], axis=1).reshape(n, d)
    tgt_p = jnp.concatenate([jnp.zeros((nb, FIRST_FRAME, d), F32), tgt], axis=1).reshape(n, d)
    tables = _rope_tables(lp)
    zero_tail = jnp.zeros((1, HEAD_SLAB - D_QK), F32)
    gqh = jnp.concatenate([sh["q_head_norm"], zero_tail], axis=1)
    gkh = jnp.concatenate([sh["k_head_norm"], zero_tail], axis=1)
    wa = _block_diag(w["gate_a_w"][0]).astype(BF16)
    wx = _block_diag(w["gate_x_w"][0]).astype(BF16)

    (u1, a1, b1, s1), (g_wd1, g_in) = ffn_up(h0, sh["ffn1_norm"], wg1, wu1, "ffn1_up",
                                             comm=([b16("ffn1_w_down"), b16("w_in")], ["gather"] * 2))
    wd1 = g_wd1.reshape(D_FF, d)
    h1, (g_uq, g_uk, g_uv, g_out) = ffn_down(
        h0, s1, wd1, "ffn1_down", comm=([b16("w_uq"), b16("w_uk"), b16("w_uv"), b16("w_out")], ["gather"] * 4))
    mla_rows = MLA_IN - D_ROPE
    w_in = g_in.reshape(mla_rows + 2 * LRU_WIDTH, d)
    wm = jnp.concatenate([w_in[:mla_rows], jnp.zeros((D_ROPE, d), BF16)], axis=0)
    wl = w_in[mla_rows:]
    wuq = _slab_rows(g_uq.reshape(MLA_HEADS * D_QK, Q_RANK), D_QK)
    wuk = _slab_rows(g_uk.reshape(MLA_HEADS * D_NOPE, KV_RANK), D_NOPE)
    wuv = g_uv.reshape(MLA_HEADS * D_V, KV_RANK)
    w_out = g_out.reshape(d, d)

    u2, zm, zl = inproj_fwd(h1, sh["mix_norm"], wm, wl)
    q, k, vv, qn, cn = mla_prep_fwd(zm, sh["q_latent_norm"], sh["kv_latent_norm"], wuq, wuk, wuv, gqh, gkh, tables, lp)
    (y_mla, lse), (g_wu2, g_wd2) = attn_fwd(
        q, k, vv, nb, lp, comm=([b16("ffn2_w_up"), b16("ffn2_w_down")], ["gather"] * 2))
    (y_lru, hs), (g_wg2,) = lru_fwd(zl, conv_w, sh["conv_b"], wa, wx, sh["gate_a_b"], sh["gate_x_b"], sh["lru_lambda"],
                                    nb, lp, comm=([b16("ffn2_w_gate")], ["gather"]))
    wg2, wu2, wd2 = (g.reshape(D_FF, d) for g in (g_wg2, g_wu2, g_wd2))
    h2, yn = outproj_fwd(h1, y_mla, y_lru, sh["attn_out_norm"], sh["lru_out_norm"], w_out)
    dh3, u3, a3, b3, loss, g_final = ffn_fwd_loss(h2, sh["ffn2_norm"], wg2, wu2, wd2, sh["final_norm"], tgt_p, lp,
                                                  "ffn2_fwd_loss")

    vec = {"final_norm": g_final}
    (dh2, da3, db3, sh3, vec["ffn2_norm"]), _ = ffn_bwd_act(dh3, h2, sh["ffn2_norm"], a3, b3, wg2, wu2, wd2, "ffn2_bwd")
    ff_shards = (N_DEV, D_FF // N_DEV, d)
    dwg2 = tn_matmul(da3, u3, "ffn2_dwg", "bf16").reshape(ff_shards)
    dwu2 = tn_matmul(db3, u3, "ffn2_dwu", "bf16").reshape(ff_shards)
    dwd2 = tn_matmul(sh3, dh3, "ffn2_dwd", "bf16").reshape(ff_shards)

    dy_mla, dy_lru, vec["attn_out_norm"], vec["lru_out_norm"] = outproj_bwd(
        dh2, y_mla, y_lru, sh["attn_out_norm"], sh["lru_out_norm"], w_out)
    dw_out = tn_matmul(yn, dh2, "dw_out", "bf16").reshape(N_DEV, d // N_DEV, d)
    (du, dgate, dconv, vec["conv_b"], vec["gate_a_b"], vec["gate_x_b"], vec["lru_lambda"], dga, dgx), (r_wg2,) = lru_bwd(
        zl, hs, dy_lru, conv_w, sh["conv_b"], wa, wx, sh["gate_a_b"], sh["gate_x_b"], sh["lru_lambda"], nb, lp,
        comm=([dwg2], ["scatter"]))
    update("ffn2_w_gate", r_wg2)

    (dq, dk, dv), (r_wu2,) = attn_bwd(q, k, vv, y_mla, dy_mla, lse, nb, lp, comm=([dwu2], ["scatter"]))
    update("ffn2_w_up", r_wu2)

    (dzm, dqr, dkr, vec["q_latent_norm"], vec["kv_latent_norm"], vec["q_head_norm"], vec["k_head_norm"]), (r_wd2,) = (
        mla_prep_bwd(dq, dk, dv, zm, qn, cn, sh["q_latent_norm"], sh["kv_latent_norm"], wuq, wuk, wuv, gqh, gkh,
                     tables, lp, comm=([dwd2], ["scatter"])))
    update("ffn2_w_down", r_wd2)
    dwuq = _unslab_rows(tn_matmul(dqr, qn, "dw_uq"), D_QK).reshape(N_DEV, -1, Q_RANK)
    dwuk = _unslab_rows(tn_matmul(dkr, cn, "dw_uk"), D_NOPE).reshape(N_DEV, -1, KV_RANK)
    dwuv = tn_matmul(dv, cn, "dw_uv").reshape(N_DEV, -1, KV_RANK)
    dh1, vec["mix_norm"] = inproj_bwd(dzm, du, dgate, dh2, h1, sh["mix_norm"], wm, wl)
    dw_in = jnp.concatenate([tn_matmul(dzm, u2, "dw_in_mla", "bf16")[:mla_rows], tn_matmul(du, u2, "dw_in_u", "bf16"),
                             tn_matmul(dgate, u2, "dw_in_gate", "bf16")], axis=0).reshape(N_DEV, -1, d)

    (dh0, da1, db1, vec["ffn1_norm"]), landed = ffn_bwd_act(
        dh1, h0, sh["ffn1_norm"], a1, b1, wg1, wu1, wd1, "ffn1_bwd", emit_sh=False,
        comm=([dw_in, dwuq, dwuk, dwuv, dw_out, split_cols(dconv, "split_conv")], ["scatter"] * 6))
    for name, r in zip(("w_in", "w_uq", "w_uk", "w_uv", "w_out", "conv_w"), landed):
        update(name, r)

    dwg1 = tn_matmul(da1, u1, "ffn1_dwg", "bf16").reshape(ff_shards)
    dwu1, (r_wg1,) = tn_matmul(db1, u1, "ffn1_dwu", "bf16", comm=([dwg1], ["scatter"]))
    dwd1, (r_wu1,) = tn_matmul(s1, dh1, "ffn1_dwd", "bf16", comm=([dwu1.reshape(ff_shards)], ["scatter"]))
    dwd1 = dwd1.reshape(ff_shards)
    dmeta = meta_grad(dh0, nb, lp)
    gates = [dga.reshape(LRU_WIDTH, LRU_BLOCK), dgx.reshape(LRU_WIDTH, LRU_BLOCK)]
    r_wd1, r_meta, r_vec, r_ga, r_gx = exchange(
        [dwd1, dmeta, pack_vectors(vec, loss)] + gates, ["scatter"] * 2 + ["gather"] * 3, "exchange_last")
    update("ffn1_w_gate", r_wg1)
    update("ffn1_w_up", r_wu1)
    update("ffn1_w_down", r_wd1)
    update("meta_tokens", r_meta)

    small = [name for name, _ in VECTORS] + GATES
    res, total_loss = adamw_small(r_vec, [r_ga, r_gx], [sh[nm] for nm in small], [m2[nm] for nm in small],
                                  [v2[nm] for nm in small])
    for i, name in enumerate(small):
        out[name] = [res[j][i] for j in range(4)]

    grad_x = dh0.reshape(nb, lp, d)[:, FIRST_FRAME:]
    loss = total_loss[0, 0]

    def as_given(a, name):
        return (a.T if name in COLUMN_SHARDED else a).reshape(w[name].shape)

    cols = [[as_given(out[name][j], name) for name in _WEIGHT_NAMES] for j in range(4)]
    return (loss, grad_x, *cols[0], *cols[1], *cols[2], *cols[3])


def kernel(x, meta_tokens, ffn1_norm, ffn1_w_gate, ffn1_w_up, ffn1_w_down, mix_norm, w_in, q_latent_norm, w_uq, kv_latent_norm, w_uk, w_uv, q_head_norm, k_head_norm, conv_w, conv_b, gate_a_w, gate_a_b, gate_x_w, gate_x_b, lru_lambda, attn_out_norm, lru_out_norm, w_out, ffn2_norm, ffn2_w_gate, ffn2_w_up, ffn2_w_down, final_norm, loss_target, m_meta_tokens, m_ffn1_norm, m_ffn1_w_gate, m_ffn1_w_up, m_ffn1_w_down, m_mix_norm, m_w_in, m_q_latent_norm, m_w_uq, m_kv_latent_norm, m_w_uk, m_w_uv, m_q_head_norm, m_k_head_norm, m_conv_w, m_conv_b, m_gate_a_w, m_gate_a_b, m_gate_x_w, m_gate_x_b, m_lru_lambda, m_attn_out_norm, m_lru_out_norm, m_w_out, m_ffn2_norm, m_ffn2_w_gate, m_ffn2_w_up, m_ffn2_w_down, m_final_norm, v_meta_tokens, v_ffn1_norm, v_ffn1_w_gate, v_ffn1_w_up, v_ffn1_w_down, v_mix_norm, v_w_in, v_q_latent_norm, v_w_uq, v_kv_latent_norm, v_w_uk, v_w_uv, v_q_head_norm, v_k_head_norm, v_conv_w, v_conv_b, v_gate_a_w, v_gate_a_b, v_gate_x_w, v_gate_x_b, v_lru_lambda, v_attn_out_norm, v_lru_out_norm, v_w_out, v_ffn2_norm, v_ffn2_w_gate, v_ffn2_w_up, v_ffn2_w_down, v_final_norm):
    args = locals()
    w = {name: args[name] for name in _WEIGHT_NAMES}
    m = {name: args["m_" + name] for name in _WEIGHT_NAMES}
    v = {name: args["v_" + name] for name in _WEIGHT_NAMES}
    return train_step(x, loss_target, w, m, v)
```

```python
import math

import jax
import jax.numpy as jnp
from jax import lax
from jax.experimental import pallas as pl
from jax.experimental.pallas import tpu as pltpu

F32 = jnp.float32
BF16 = jnp.bfloat16

D_MODEL = 1024
CHUNK = 64
CHUNK_SHIFT = 6
N_META = 16
PAD = CHUNK - N_META
FIRST_FRAME = PAD + N_META
MLA_HEADS = 4
D_NOPE = 128
D_ROPE = 64
D_QK = D_NOPE + D_ROPE
D_V = 128
HEAD_SLAB = 256
KV_RANK = 256
Q_RANK = 384
ROPE_THETA = 10000.0
LRU_WIDTH = 512
LRU_BLOCKS = 8
LRU_BLOCK = 64
LRU_TILE = 128
CONV_W = 4
C_RGLRU = 8.0
D_FF = 2816
MLA_IN = 768
EPS = 1e-6
NEG_INF = -1e30
N_DEV = 8
LANES = 128
VMEM_LIMIT = 52 * 1024 * 1024
ATTN_HEADS_PER_STEP = 2
TN_ROWS = 4224
TN_X_BYTES = 12 * 1024 * 1024
TN_Y_BYTES = 9 * 1024 * 1024 // 2

ADAM_LR = 0.001
ADAM_B1 = 0.9
ADAM_B2 = 0.999
ADAM_EPS = 1e-08
ADAM_WD = 0.01
ADAM_STEP = 10

VMEM_WHOLE = pl.BlockSpec(memory_space=pltpu.VMEM)
HBM_WHOLE = pl.BlockSpec(memory_space=pl.ANY)


def _params(sems):
    if sems is None:
        return pltpu.CompilerParams(vmem_limit_bytes=VMEM_LIMIT)
    return pltpu.CompilerParams(dimension_semantics=sems, vmem_limit_bytes=VMEM_LIMIT)


def _tile(n, cap, mult=16):
    best = None
    for t in range(mult, min(n, cap) + 1, mult):
        if n % t == 0:
            best = t
    assert best is not None, (n, cap, mult)
    return best


def _row(tm, d):
    return pl.BlockSpec((tm, d), lambda i: (i, 0))


def _fixed(shape):
    return pl.BlockSpec(shape, lambda i: (0,) * len(shape))


def _mesh_position():
    return lax.axis_index("x"), lax.axis_index("y"), lax.axis_index("c")


def _flat_index(x, y, c):
    return 4 * x + 2 * y + c


def _peers(x, y, c):
    out = []
    for k in range(1, N_DEV):
        fx, fy, fc = (k >> 2) & 1, (k >> 1) & 1, k & 1
        out.append((1 - x if fx else x, 1 - y if fy else y, 1 - c if fc else c))
    return out


def _comm_out_shapes(srcs, modes):
    return [jax.ShapeDtypeStruct((N_DEV,) + s.shape if md == "gather" else s.shape, s.dtype)
            for s, md in zip(srcs, modes)]


def _comm_scratch(n):
    per_peer = n * (N_DEV - 1)
    return [pltpu.SemaphoreType.DMA((per_peer,)), pltpu.SemaphoreType.DMA((per_peer,)), pltpu.SemaphoreType.DMA((n,))]


class _Copies:
    def __init__(self, own, first, relay):
        self.own, self.first, self.relay = own, first, relay

    def start(self):
        for cp in self.own + self.first:
            cp.start()

    def forward(self):
        for arrival, onward in self.relay:
            arrival.wait_recv()
            onward.start()

    def finish(self):
        arrivals = [a for a, _ in self.relay]
        onward = [f for _, f in self.relay]
        for cp in self.first + onward:
            if not any(cp is a for a in arrivals):
                cp.wait_recv()
        for cp in self.first + onward:
            cp.wait_send()
        for cp in self.own:
            cp.wait()


def _comm_copies(src_refs, dst_refs, modes, send, recv, local):
    x, y, c = _mesh_position()
    me = _flat_index(x, y, c)
    n = len(modes)
    sibling = (x, y, 1 - c)
    chips = [(1 - x, y), (x, 1 - y), (1 - x, 1 - y)]

    def remote(src, dst, k, t, to):
        return pltpu.make_async_remote_copy(src_ref=src, dst_ref=dst, send_sem=send.at[k * n + t],
                                            recv_sem=recv.at[k * n + t], device_id=to,
                                            device_id_type=pl.DeviceIdType.MESH)

    own, first, relay = [], [], []
    for t, (src, dst, md) in enumerate(zip(src_refs, dst_refs, modes)):
        if md == "scatter":
            own.append(pltpu.make_async_copy(src.at[me], dst.at[me], local.at[t]))
            for k, peer in enumerate(_peers(x, y, c)):
                first.append(remote(src.at[_flat_index(*peer)], dst.at[me], k, t, peer))
        else:
            own.append(pltpu.make_async_copy(src, dst.at[me], local.at[t]))
            first.append(remote(src, dst.at[me], 0, t, sibling))
            for j, chip in enumerate(chips):
                arrival = remote(src, dst.at[me], 1 + j, t, (*chip, c))
                landed = dst.at[_flat_index(*chip, c)]
                first.append(arrival)
                relay.append((arrival, remote(landed, landed, 4 + j, t, sibling)))
    return _Copies(own, first, relay)


def _hosted(body, n_in, n_out, modes, grid):
    t = len(modes)
    total = math.prod(grid)

    def wrapped(*refs):
        ins, csrc = refs[:n_in], refs[n_in:n_in + t]
        outs = refs[n_in + t:n_in + t + n_out]
        cdst = refs[n_in + t + n_out:n_in + 2 * t + n_out]
        scratch = refs[n_in + 2 * t + n_out:-3]
        copies = _comm_copies(csrc, cdst, modes, *refs[-3:])
        step = pl.program_id(0)
        for axis in range(1, len(grid)):
            step = step * grid[axis] + pl.program_id(axis)

        @pl.when(step == 0)
        def _():
            copies.start()

        body(*ins, *outs, *scratch)

        @pl.when(step == (total * 3) // 5)
        def _():
            copies.forward()

        @pl.when(step == total - 1)
        def _():
            copies.finish()

    return wrapped


def _call(body, name, grid, in_specs, out_specs, out_shape, sems, args, scratch=(), comm=None):
    if comm is None:
        outs = pl.pallas_call(body, name=name, grid=grid, in_specs=in_specs, out_specs=out_specs, out_shape=out_shape,
                              scratch_shapes=list(scratch), compiler_params=_params(sems))(*args)
        return outs, []
    srcs, modes = comm
    n = len(modes)
    res = pl.pallas_call(
        _hosted(body, len(in_specs), len(out_specs), modes, grid), name=name, grid=grid,
        in_specs=list(in_specs) + [HBM_WHOLE] * n, out_specs=list(out_specs) + [HBM_WHOLE] * n,
        out_shape=list(out_shape) + _comm_out_shapes(srcs, modes),
        scratch_shapes=list(scratch) + _comm_scratch(n),
        compiler_params=_params(("arbitrary",) * len(grid)))(*args, *srcs)
    return res[:len(out_specs)], res[len(out_specs):]


def exchange(srcs, modes, name):
    n = len(modes)

    def body(*refs):
        copies = _comm_copies(refs[:n], refs[n:2 * n], modes, *refs[2 * n:])
        copies.start()
        copies.forward()
        copies.finish()

    return pl.pallas_call(body, name=name, in_specs=[HBM_WHOLE] * n, out_specs=[HBM_WHOLE] * n,
                          out_shape=_comm_out_shapes(srcs, modes), scratch_shapes=_comm_scratch(n))(*srcs)


def _nn(a, b):
    return jnp.dot(a, b, preferred_element_type=F32)


def _nt(a, b):
    return lax.dot_general(a, b, (((1,), (1,)), ((), ())), preferred_element_type=F32)


def _tn(a, b):
    return lax.dot_general(a, b, (((0,), (0,)), ((), ())), preferred_element_type=F32)


def _sig(x):
    return 1.0 / (1.0 + jnp.exp(-x))


def _rms_r(x, n=None):
    n = x.shape[-1] if n is None else n
    return lax.rsqrt(jnp.sum(x * x, axis=-1, keepdims=True) * (1.0 / n) + EPS)


def _rms_bwd(x, r, g, dy, n=None):
    n = x.shape[-1] if n is None else n
    xhat = x * r
    dxhat = dy * g
    dx = r * (dxhat - xhat * (jnp.sum(dxhat * xhat, axis=-1, keepdims=True) * (1.0 / n)))
    return dx, jnp.sum(dy * xhat, axis=0, keepdims=True)


def _accumulate(ref, val, first):
    @pl.when(first)
    def _():
        ref[...] = val

    @pl.when(jnp.logical_not(first))
    def _():
        ref[...] += val


_GELU_C = math.sqrt(2.0 / math.pi)


def _gelu_and_grad(x):
    inner = _GELU_C * (x + 0.044715 * x * x * x)
    t = jnp.tanh(inner)
    gelu = 0.5 * x * (1.0 + t)
    dgelu = 0.5 * (1.0 + t) + 0.5 * x * (1.0 - t * t) * _GELU_C * (1.0 + 3.0 * 0.044715 * x * x)
    return gelu, dgelu


def _log1p_small(t):
    return jnp.where(t < 1e-3, t * (1.0 - t * (0.5 - t * (1.0 / 3.0))), jnp.log(1.0 + t))


def _softplus(x):
    return jnp.maximum(x, 0.0) + _log1p_small(jnp.exp(-jnp.abs(x)))


def _sig_tanh(x):
    return 0.5 + 0.5 * jnp.tanh(0.5 * x)


def _ff_chunks(f):
    return 2 if (f // 2) % LANES == 0 else 1


def _swiglu_half(x, g_ref, wg_ref, wu_ref, wd_ref, a_ref, b_ref, fc):
    f = wg_ref.shape[0]
    u = (x * _rms_r(x) * g_ref[...]).astype(BF16)
    acc = jnp.zeros(x.shape, F32)
    for c in range(f // fc):
        cols = slice(c * fc, (c + 1) * fc)
        a = _nt(u, wg_ref[cols, :])
        b = _nt(u, wu_ref[cols, :])
        s = (a * _sig(a) * b).astype(BF16)
        acc = acc + _nn(s, wd_ref[cols, :])
        a_ref[:, cols] = a.astype(BF16)
        b_ref[:, cols] = b.astype(BF16)
    return x + 0.5 * acc, u


def ffn_up(h, g, wg, wu, name, comm=None):
    n, d = h.shape
    f = wg.shape[0]
    tm = _tile(n, 528)
    fc = 2 * LANES if f % (2 * LANES) == 0 else f

    def body(h_ref, g_ref, wg_ref, wu_ref, u_ref, a_ref, b_ref, s_ref):
        x = h_ref[...]
        u = (x * _rms_r(x) * g_ref[...]).astype(BF16)
        u_ref[...] = u
        for c in range(f // fc):
            cols = slice(c * fc, (c + 1) * fc)
            a = _nt(u, wg_ref[cols, :])
            b = _nt(u, wu_ref[cols, :])
            a_ref[:, cols] = a.astype(BF16)
            b_ref[:, cols] = b.astype(BF16)
            s_ref[:, cols] = (0.5 * (a * _sig(a) * b)).astype(BF16)

    wide = jax.ShapeDtypeStruct((n, f), BF16)
    return _call(
        body, name, (n // tm,),
        [_row(tm, d), _fixed((1, d)), VMEM_WHOLE, VMEM_WHOLE],
        [_row(tm, d), _row(tm, f), _row(tm, f), _row(tm, f)],
        [jax.ShapeDtypeStruct((n, d), BF16), wide, wide, wide],
        ("parallel",), (h, g, wg, wu), comm=comm)


def ffn_down_inproj(h, s, wd, g, wm, wl, name, comm=None):
    n, d = h.shape
    f = wd.shape[0]
    tm = _tile(n, 528)

    def body(h_ref, s_ref, wd_ref, g_ref, wm_ref, wl_ref, ho_ref, u_ref, zm_ref, zl_ref):
        x = h_ref[...] + _nn(s_ref[...], wd_ref[...])
        ho_ref[...] = x
        u = (x * _rms_r(x) * g_ref[...]).astype(BF16)
        u_ref[...] = u
        zm_ref[...] = _nt(u, wm_ref[...])
        zl_ref[...] = _nt(u, wl_ref[...])

    return _call(
        body, name, (n // tm,),
        [_row(tm, d), _row(tm, f), VMEM_WHOLE, _fixed((1, d)), VMEM_WHOLE, VMEM_WHOLE],
        [_row(tm, d), _row(tm, d), _row(tm, MLA_IN), _row(tm, 2 * LRU_WIDTH)],
        [jax.ShapeDtypeStruct((n, d), F32), jax.ShapeDtypeStruct((n, d), BF16),
         jax.ShapeDtypeStruct((n, MLA_IN), F32), jax.ShapeDtypeStruct((n, 2 * LRU_WIDTH), F32)],
        ("parallel",), (h, s, wd, g, wm, wl), comm=comm)


def ffn_fwd_loss(h, g, wg, wu, wd, g_final, tgt, lp, name):
    n, d = h.shape
    f = wg.shape[0]
    tm = _tile(lp, 528)
    per_seq = lp // tm
    fc = 2 * LANES if f % (2 * LANES) == 0 else f

    def body(h_ref, g_ref, wg_ref, wu_ref, wd_ref, gf_ref, t_ref, dh_ref, u_ref, a_ref, b_ref, loss_ref, dgf_ref):
        i = pl.program_id(0)
        y, u_ref[...] = _swiglu_half(h_ref[...], g_ref, wg_ref, wu_ref, wd_ref, a_ref, b_ref, fc)
        dh_ref[...], part, dg = _loss_and_grad(y, gf_ref[...], t_ref[...], (i % per_seq) * tm)
        _accumulate(loss_ref, jnp.broadcast_to(part, (1, LANES)), i == 0)
        _accumulate(dgf_ref, dg, i == 0)

    outs, _ = _call(
        body, name, (n // tm,),
        [_row(tm, d), _fixed((1, d)), VMEM_WHOLE, VMEM_WHOLE, VMEM_WHOLE, _fixed((1, d)), _row(tm, d)],
        [_row(tm, d), _row(tm, d), _row(tm, f), _row(tm, f), _fixed((1, LANES)), _fixed((1, d))],
        [jax.ShapeDtypeStruct((n, d), F32), jax.ShapeDtypeStruct((n, d), BF16),
         jax.ShapeDtypeStruct((n, f), BF16), jax.ShapeDtypeStruct((n, f), BF16),
         jax.ShapeDtypeStruct((1, LANES), F32), jax.ShapeDtypeStruct((1, d), F32)],
        ("arbitrary",), (h, g, wg, wu, wd, g_final, tgt))
    return outs


def ffn_bwd_act(dh, h, g, a, b, wg, wu, wd, name, comm=None, emit_sh=True):
    n, d = h.shape
    f = wg.shape[0]
    tm = _tile(n, 352 if emit_sh else 384)
    nc = _ff_chunks(f)
    fc = f // nc

    def body(dh_ref, h_ref, g_ref, a_ref, b_ref, wg_ref, wu_ref, wd_ref, dhi_ref, da_ref, db_ref, *rest):
        dg_ref = rest[-1]
        x = h_ref[...]
        dy = dh_ref[...]
        r = _rms_r(x)
        dhh = (0.5 * dy).astype(BF16)
        du = jnp.zeros((tm, d), F32)
        for c in range(nc):
            cols = slice(c * fc, (c + 1) * fc)
            ds = _nt(dhh, wd_ref[cols, :])
            av = a_ref[:, cols].astype(F32)
            bv = b_ref[:, cols].astype(F32)
            sg = _sig(av)
            sil = av * sg
            da = (ds * bv * (sg * (1.0 + av * (1.0 - sg)))).astype(BF16)
            db = (ds * sil).astype(BF16)
            da_ref[:, cols] = da
            db_ref[:, cols] = db
            if emit_sh:
                rest[0][:, cols] = (0.5 * sil * bv).astype(BF16)
            du = du + _nn(da, wg_ref[cols, :]) + _nn(db, wu_ref[cols, :])
        dx, dg = _rms_bwd(x, r, g_ref[...], du)
        dhi_ref[...] = dy + dx
        _accumulate(dg_ref, dg, pl.program_id(0) == 0)

    wide = [jax.ShapeDtypeStruct((n, f), BF16)] * (3 if emit_sh else 2)
    return _call(
        body, name, (n // tm,),
        [_row(tm, d), _row(tm, d), _fixed((1, d)), _row(tm, f), _row(tm, f), VMEM_WHOLE, VMEM_WHOLE, VMEM_WHOLE],
        [_row(tm, d)] + [_row(tm, f)] * len(wide) + [_fixed((1, d))],
        [jax.ShapeDtypeStruct((n, d), F32)] + wide + [jax.ShapeDtypeStruct((1, d), F32)],
        ("arbitrary",), (dh, h, g, a, b, wg, wu, wd), comm=comm)


def tn_matmul(x, y, name, out="f32", comm=None):
    n, k = x.shape
    m = y.shape[1]
    tm = _tile(n, TN_ROWS)
    kc, mc = k, (512 if m % 512 == 0 else m)
    while tm * kc * x.dtype.itemsize > TN_X_BYTES and kc % (2 * LANES) == 0:
        kc //= 2
    while tm * mc * y.dtype.itemsize > TN_Y_BYTES and mc % (2 * LANES) == 0:
        mc //= 2
    steps = n // tm

    def body(x_ref, y_ref, o_ref, *acc):
        i = pl.program_id(2)
        part = _tn(x_ref[...].astype(BF16), y_ref[...].astype(BF16))
        if steps == 1:
            o_ref[...] = part.astype(o_ref.dtype)
        elif out == "f32":
            _accumulate(o_ref, part, i == 0)
        else:
            _accumulate(acc[0], part, i == 0)

            @pl.when(i == steps - 1)
            def _():
                o_ref[...] = acc[0][...].astype(BF16)

    out_shape = jax.ShapeDtypeStruct((k, m), F32 if out == "f32" else BF16)
    (res,), landed = _call(
        body, name, (k // kc, m // mc, steps),
        [pl.BlockSpec((tm, kc), lambda a, b, i: (i, a)), pl.BlockSpec((tm, mc), lambda a, b, i: (i, b))],
        [pl.BlockSpec((kc, mc), lambda a, b, i: (a, b))], [out_shape], ("parallel", "parallel", "arbitrary"), (x, y),
        scratch=[pltpu.VMEM((kc, mc), F32)] if (out == "bf16" and steps > 1) else [], comm=comm)
    return (res, landed) if comm is not None else res


def inproj_bwd(dzm, du, dgate, dh2, h, g, wm, wl):
    n, d = h.shape
    tm = _tile(n, 352)

    def body(dzm_ref, du_ref, dgt_ref, dh2_ref, h_ref, g_ref, wm_ref, wl_ref, dh_ref, dg_ref):
        x = h_ref[...]
        dun = (_nn(dzm_ref[...].astype(BF16), wm_ref[...])
               + _nn(du_ref[...].astype(BF16), wl_ref[:LRU_WIDTH, :])
               + _nn(dgt_ref[...].astype(BF16), wl_ref[LRU_WIDTH:, :]))
        dx, dg = _rms_bwd(x, _rms_r(x), g_ref[...], dun)
        dh_ref[...] = dh2_ref[...] + dx
        _accumulate(dg_ref, dg, pl.program_id(0) == 0)

    return pl.pallas_call(
        body, name="inproj_bwd", grid=(n // tm,),
        in_specs=[_row(tm, MLA_IN), _row(tm, LRU_WIDTH), _row(tm, LRU_WIDTH), _row(tm, d), _row(tm, d),
                  _fixed((1, d)), VMEM_WHOLE, VMEM_WHOLE],
        out_specs=[_row(tm, d), _fixed((1, d))],
        out_shape=[jax.ShapeDtypeStruct((n, d), F32), jax.ShapeDtypeStruct((1, d), F32)],
        compiler_params=_params(("arbitrary",)),
    )(dzm, du, dgate, dh2, h, g, wm, wl)


def _rope_tables(lp):
    pos = jnp.arange(lp, dtype=F32) - float(PAD)
    half = D_ROPE // 2
    inv_freq = ROPE_THETA ** (-jnp.arange(0, half, dtype=F32) / half)
    ang = pos[:, None] * inv_freq[None, :]
    cos, sin = jnp.cos(ang), jnp.sin(ang)
    one = jnp.ones((lp, D_NOPE), F32)
    z_nope = jnp.zeros((lp, D_NOPE), F32)
    z_half = jnp.zeros((lp, half), F32)
    z_tail = jnp.zeros((lp, HEAD_SLAB - D_QK), F32)
    cosr = jnp.concatenate([one, cos, cos, z_tail], axis=1)
    sin_up = jnp.concatenate([z_nope, z_half, sin, z_tail], axis=1)
    sin_dn = jnp.concatenate([z_nope, -sin, z_half, z_tail], axis=1)
    return cosr, sin_up, sin_dn


def _rope(x, cosr, sin_up, sin_dn):
    half = D_ROPE // 2
    return x * cosr + pltpu.roll(x, half, axis=1) * sin_up + pltpu.roll(x, HEAD_SLAB - half, axis=1) * sin_dn


def _rope_bwd(dy, cosr, sin_up, sin_dn):
    half = D_ROPE // 2
    return (dy * cosr + pltpu.roll(dy * sin_up, HEAD_SLAB - half, axis=1)
            + pltpu.roll(dy * sin_dn, half, axis=1))


def _k_rope_slab(zm_tile):
    tm = zm_tile.shape[0]
    krp = zm_tile[:, Q_RANK + KV_RANK:MLA_IN]
    return jnp.concatenate([jnp.zeros((tm, D_NOPE), F32), krp], axis=1)


def mla_prep_fwd(zm, gql, gkvl, wuq, wuk, wuv, gqh, gkh, tables, lp):
    n = zm.shape[0]
    tm = _tile(lp, 352)
    per_seq = lp // tm
    width = MLA_HEADS * HEAD_SLAB
    scale = 1.0 / math.sqrt(D_QK)

    def body(zm_ref, gql_ref, gkvl_ref, wuq_ref, wuk_ref, wuv_ref, gqh_ref, gkh_ref,
             cos_ref, up_ref, dn_ref, q_ref, k_ref, v_ref, qn_ref, cn_ref):
        z = zm_ref[...]
        cq = z[:, :Q_RANK]
        ckv = z[:, Q_RANK:Q_RANK + KV_RANK]
        qn = (cq * _rms_r(cq) * gql_ref[...]).astype(BF16)
        cn = (ckv * _rms_r(ckv) * gkvl_ref[...]).astype(BF16)
        qn_ref[...] = qn
        cn_ref[...] = cn
        q_raw = _nt(qn, wuq_ref[...])
        k_raw = _nt(cn, wuk_ref[...])
        v_ref[...] = _nt(cn, wuv_ref[...]).astype(BF16)
        kr_slab = _k_rope_slab(z)
        cosr, sin_up, sin_dn = cos_ref[...], up_ref[...], dn_ref[...]
        for hd in range(MLA_HEADS):
            cols = slice(hd * HEAD_SLAB, (hd + 1) * HEAD_SLAB)
            xq = q_raw[:, cols]
            yq = _rope(xq * _rms_r(xq, D_QK) * gqh_ref[...], cosr, sin_up, sin_dn)
            q_ref[:, cols] = (yq * scale).astype(BF16)
            xk = k_raw[:, cols] + kr_slab
            yk = _rope(xk * _rms_r(xk, D_QK) * gkh_ref[...], cosr, sin_up, sin_dn)
            k_ref[:, cols] = yk.astype(BF16)

    tab = pl.BlockSpec((tm, HEAD_SLAB), lambda i: (i % per_seq, 0))
    return pl.pallas_call(
        body, name="mla_prep_fwd", grid=(n // tm,),
        in_specs=[_row(tm, MLA_IN), _fixed((1, Q_RANK)), _fixed((1, KV_RANK)), VMEM_WHOLE, VMEM_WHOLE, VMEM_WHOLE,
                  _fixed((1, HEAD_SLAB)), _fixed((1, HEAD_SLAB)), tab, tab, tab],
        out_specs=[_row(tm, width), _row(tm, width), _row(tm, MLA_HEADS * D_V), _row(tm, Q_RANK), _row(tm, KV_RANK)],
        out_shape=[jax.ShapeDtypeStruct((n, width), BF16), jax.ShapeDtypeStruct((n, width), BF16),
                   jax.ShapeDtypeStruct((n, MLA_HEADS * D_V), BF16), jax.ShapeDtypeStruct((n, Q_RANK), BF16),
                   jax.ShapeDtypeStruct((n, KV_RANK), BF16)],
        compiler_params=_params(("parallel",)),
    )(zm, gql, gkvl, wuq, wuk, wuv, gqh, gkh, *tables)


def mla_prep_bwd(dq, dk, dv, zm, qn, cn, gql, gkvl, wuq, wuk, wuv, gqh, gkh, tables, lp, comm=None):
    n = zm.shape[0]
    tm = _tile(lp, 704)
    per_seq = lp // tm
    width = MLA_HEADS * HEAD_SLAB
    scale = 1.0 / math.sqrt(D_QK)

    def body(dq_ref, dk_ref, dv_ref, zm_ref, qn_ref, cn_ref, gql_ref, gkvl_ref, wuq_ref, wuk_ref, wuv_ref,
             gqh_ref, gkh_ref, cos_ref, up_ref, dn_ref,
             dzm_ref, dqr_ref, dkr_ref, dgql_ref, dgkvl_ref, dgqh_ref, dgkh_ref):
        z = zm_ref[...]
        cq = z[:, :Q_RANK]
        ckv = z[:, Q_RANK:Q_RANK + KV_RANK]
        q_raw = _nt(qn_ref[...], wuq_ref[...])
        k_raw = _nt(cn_ref[...], wuk_ref[...])
        kr_slab = _k_rope_slab(z)
        cosr, sin_up, sin_dn = cos_ref[...], up_ref[...], dn_ref[...]
        dgq = jnp.zeros((1, HEAD_SLAB), F32)
        dgk = jnp.zeros((1, HEAD_SLAB), F32)
        dkrp = jnp.zeros((tm, HEAD_SLAB - D_NOPE), F32)
        for hd in range(MLA_HEADS):
            cols = slice(hd * HEAD_SLAB, (hd + 1) * HEAD_SLAB)
            xq = q_raw[:, cols]
            dxn = _rope_bwd(dq_ref[:, cols] * scale, cosr, sin_up, sin_dn)
            dxq, dg = _rms_bwd(xq, _rms_r(xq, D_QK), gqh_ref[...], dxn, D_QK)
            dgq = dgq + dg
            dqr_ref[:, cols] = dxq.astype(BF16)
            xk = k_raw[:, cols] + kr_slab
            dxn = _rope_bwd(dk_ref[:, cols], cosr, sin_up, sin_dn)
            dxk, dg = _rms_bwd(xk, _rms_r(xk, D_QK), gkh_ref[...], dxn, D_QK)
            dgk = dgk + dg
            dkr_ref[:, cols] = dxk.astype(BF16)
            dkrp = dkrp + dxk[:, D_NOPE:]
        dqn = _nn(dqr_ref[...], wuq_ref[...])
        dcn = _nn(dkr_ref[...], wuk_ref[...]) + _nn(dv_ref[...].astype(BF16), wuv_ref[...])
        dcq, dg1 = _rms_bwd(cq, _rms_r(cq), gql_ref[...], dqn)
        dckv, dg2 = _rms_bwd(ckv, _rms_r(ckv), gkvl_ref[...], dcn)
        dzm_ref[:, :Q_RANK] = dcq
        dzm_ref[:, Q_RANK:Q_RANK + KV_RANK] = dckv
        dzm_ref[:, Q_RANK + KV_RANK:] = dkrp
        first = pl.program_id(0) == 0
        _accumulate(dgql_ref, dg1, first)
        _accumulate(dgkvl_ref, dg2, first)
        _accumulate(dgqh_ref, dgq, first)
        _accumulate(dgkh_ref, dgk, first)

    tab = pl.BlockSpec((tm, HEAD_SLAB), lambda i: (i % per_seq, 0))
    return _call(
        body, "mla_prep_bwd", (n // tm,),
        [_row(tm, width), _row(tm, width), _row(tm, MLA_HEADS * D_V), _row(tm, MLA_IN),
         _row(tm, Q_RANK), _row(tm, KV_RANK), _fixed((1, Q_RANK)), _fixed((1, KV_RANK)),
         VMEM_WHOLE, VMEM_WHOLE, VMEM_WHOLE, _fixed((1, HEAD_SLAB)), _fixed((1, HEAD_SLAB)), tab, tab, tab],
        [_row(tm, MLA_IN), _row(tm, width), _row(tm, width), _fixed((1, Q_RANK)), _fixed((1, KV_RANK)),
         _fixed((1, HEAD_SLAB)), _fixed((1, HEAD_SLAB))],
        [jax.ShapeDtypeStruct((n, MLA_IN), F32), jax.ShapeDtypeStruct((n, width), BF16),
         jax.ShapeDtypeStruct((n, width), BF16), jax.ShapeDtypeStruct((1, Q_RANK), F32),
         jax.ShapeDtypeStruct((1, KV_RANK), F32), jax.ShapeDtypeStruct((1, HEAD_SLAB), F32),
         jax.ShapeDtypeStruct((1, HEAD_SLAB), F32)],
        ("arbitrary",), (dq, dk, dv, zm, qn, cn, gql, gkvl, wuq, wuk, wuv, gqh, gkh, *tables), comm=comm)


def _attn_tile(lp):
    return _tile(lp, 704, CHUNK)


def _chunk_mask(i, j, t):
    qpos = i * t + lax.broadcasted_iota(jnp.int32, (t, t), 0)
    kpos = j * t + lax.broadcasted_iota(jnp.int32, (t, t), 1)
    same_or_earlier = jnp.right_shift(kpos, CHUNK_SHIFT) <= jnp.right_shift(qpos, CHUNK_SHIFT)
    return jnp.logical_and(same_or_earlier, kpos >= PAD)


def _masked_scores(s, i, j, t, diagonal):
    if diagonal:
        return jnp.where(_chunk_mask(i, j, t), s, NEG_INF)
    kpos = j * t + lax.broadcasted_iota(jnp.int32, (1, t), 1)
    return s + jnp.where(kpos < PAD, NEG_INF, 0.0)


def attn_fwd(q, k, v, nb, lp, comm=None):
    n = q.shape[0]
    t = _attn_tile(lp)
    nq = lp // t

    hp = ATTN_HEADS_PER_STEP

    def body(q_ref, k_ref, v_ref, o_ref, lse_ref):
        i = pl.program_id(2)
        qs = [q_ref[:, hh * HEAD_SLAB:(hh + 1) * HEAD_SLAB] for hh in range(hp)]

        def kv_step(j, carry, diagonal=False):
            off = pl.multiple_of(j * t, t)
            out = []
            for hh in range(hp):
                m, l, acc = carry[hh]
                kv = k_ref[pl.ds(off, t), hh * HEAD_SLAB:(hh + 1) * HEAD_SLAB]
                s = _masked_scores(_nt(qs[hh], kv), i, j, t, diagonal)
                m_new = jnp.maximum(m, jnp.max(s, axis=-1, keepdims=True))
                p = jnp.exp(s - m_new)
                alpha = jnp.exp(m - m_new)
                l = alpha * l + jnp.sum(p, axis=-1, keepdims=True)
                acc = alpha * acc + _nn(p.astype(BF16), v_ref[pl.ds(off, t), hh * D_V:(hh + 1) * D_V])
                out.append((m_new, l, acc))
            return tuple(out)

        init = tuple((jnp.full((t, 1), NEG_INF, F32), jnp.zeros((t, 1), F32), jnp.zeros((t, D_V), F32))
                     for _ in range(hp))
        done = kv_step(i, lax.fori_loop(0, i, kv_step, init), diagonal=True)
        for hh, (m, l, acc) in enumerate(done):
            o_ref[:, hh * D_V:(hh + 1) * D_V] = acc * (1.0 / l)
            lse_ref[hh] = jnp.broadcast_to(m + jnp.log(l), (t, LANES))

    return _call(
        body, "attn_fwd", (nb, MLA_HEADS // hp, nq),
        [pl.BlockSpec((t, hp * HEAD_SLAB), lambda b, h, i: (b * nq + i, h)),
         pl.BlockSpec((lp, hp * HEAD_SLAB), lambda b, h, i: (b, h)),
         pl.BlockSpec((lp, hp * D_V), lambda b, h, i: (b, h))],
        [pl.BlockSpec((t, hp * D_V), lambda b, h, i: (b * nq + i, h)),
         pl.BlockSpec((hp, t, LANES), lambda b, h, i: (h, b * nq + i, 0))],
        [jax.ShapeDtypeStruct((n, MLA_HEADS * D_V), F32), jax.ShapeDtypeStruct((MLA_HEADS, n, LANES), F32)],
        ("parallel", "parallel", "parallel"), (q, k, v), comm=comm)


def attn_bwd(q, k, v, o, do, lse, nb, lp, comm=None):
    n = q.shape[0]
    t = _attn_tile(lp)
    nq = lp // t

    def body(q_ref, k_ref, v_ref, o_ref, do_ref, lse_ref, dq_ref, dk_ref, dv_ref):
        dk_ref[...] = jnp.zeros_like(dk_ref)
        dv_ref[...] = jnp.zeros_like(dv_ref)

        def q_step(i, _):
            qoff = pl.multiple_of(i * t, t)
            qv = q_ref[pl.ds(qoff, t), :]
            dov = do_ref[pl.ds(qoff, t), :]
            delta = jnp.sum(o_ref[pl.ds(qoff, t), :] * dov, axis=-1, keepdims=True)
            lse_q = jnp.max(lse_ref[0, pl.ds(qoff, t), :], axis=-1, keepdims=True)
            do16 = dov.astype(BF16)

            def kv_step(j, dq_acc, diagonal=False):
                koff = pl.multiple_of(j * t, t)
                kv = k_ref[pl.ds(koff, t), :]
                s = _masked_scores(_nt(qv, kv), i, j, t, diagonal)
                p = jnp.exp(s - lse_q)
                dp = _nt(do16, v_ref[pl.ds(koff, t), :])
                ds16 = (p * (dp - delta)).astype(BF16)
                dv_ref[pl.ds(koff, t), :] += _tn(p.astype(BF16), do16)
                dk_ref[pl.ds(koff, t), :] += _tn(ds16, qv)
                return dq_acc + _nn(ds16, kv)

            earlier = lax.fori_loop(0, i, kv_step, jnp.zeros((t, HEAD_SLAB), F32))
            dq_ref[pl.ds(qoff, t), :] = kv_step(i, earlier, diagonal=True)
            return 0

        lax.fori_loop(0, nq, q_step, 0)

    wide = pl.BlockSpec((lp, HEAD_SLAB), lambda b, h: (b, h))
    thin = pl.BlockSpec((lp, D_V), lambda b, h: (b, h))
    width = MLA_HEADS * HEAD_SLAB
    return _call(
        body, "attn_bwd", (nb, MLA_HEADS),
        [wide, wide, thin, thin, thin, pl.BlockSpec((1, lp, LANES), lambda b, h: (h, b, 0))],
        [wide, wide, thin],
        [jax.ShapeDtypeStruct((n, width), F32), jax.ShapeDtypeStruct((n, width), F32),
         jax.ShapeDtypeStruct((n, MLA_HEADS * D_V), F32)],
        ("parallel", "parallel"), (q, k, v, o, do, lse), comm=comm)


def _seq_rows(nb, lp, width):
    rows = lax.broadcasted_iota(jnp.int32, (lp, width), 0)
    return jnp.concatenate([rows] * nb, axis=0) if nb > 1 else rows


def _lru_gates(u, w_ref, cb, wa, wx, ba, bx, lam):
    xc = (cb + w_ref[pl.ds(3, 1), :] * u + w_ref[pl.ds(2, 1), :] * pltpu.roll(u, 1, axis=0)
          + w_ref[pl.ds(1, 1), :] * pltpu.roll(u, 2, axis=0) + w_ref[pl.ds(0, 1), :] * pltpu.roll(u, 3, axis=0))
    xc16 = xc.astype(BF16)
    ra = _sig_tanh(_nn(xc16, wa) + ba)
    ia = _sig_tanh(_nn(xc16, wx) + bx)
    sp = _softplus(-lam)
    log_a = -C_RGLRU * ra * sp
    a = jnp.exp(log_a)
    x2 = 2.0 * log_a
    mult = jnp.sqrt(jnp.where(x2 > -1e-2, -x2 * (1.0 + x2 * (0.5 + x2 * (1.0 / 6.0))), 1.0 - a * a))
    return xc, xc16, ra, ia, sp, a, mult


def _scan_block_rows(width):
    return lax.broadcasted_iota(jnp.int32, (8, width), 0)


def lru_fwd(zl, conv_w, conv_b, wa, wx, ba, bx, lam, nb, lp, comm=None):
    n = zl.shape[0]
    w = LRU_TILE
    nt = LRU_WIDTH // w
    nblk = lp // 8

    def body(u_ref, gt_ref, cw_ref, cb_ref, wa_ref, wx_ref, ba_ref, bx_ref, lam_ref, y_ref, h_ref, a_s, b_s):
        u = u_ref[...]
        xc, _, _, ia, _, a, mult = _lru_gates(u, cw_ref, cb_ref[...], wa_ref[...], wx_ref[...],
                                              ba_ref[...], bx_ref[...], lam_ref[...])
        row = _seq_rows(nb, lp, w)
        mult = jnp.where(row == PAD, 1.0, mult)
        a_s[...] = a
        b_s[...] = jnp.where(row < PAD, 0.0, mult * (ia * xc))
        r8 = _scan_block_rows(w)

        def blk(i, carry):
            out = []
            for s_id in range(nb):
                off = pl.multiple_of(s_id * lp + i * 8, 8)
                av = a_s[pl.ds(off, 8), :]
                bv = b_s[pl.ds(off, 8), :]
                for sh in (1, 2, 4):
                    keep = r8 >= sh
                    bv = jnp.where(keep, av * pltpu.roll(bv, sh, axis=0) + bv, bv)
                    av = jnp.where(keep, av * pltpu.roll(av, sh, axis=0), av)
                hv = bv + av * carry[s_id]
                h_ref[pl.ds(off, 8), :] = hv
                out.append(jnp.sum(jnp.where(r8 == 7, hv, 0.0), axis=0, keepdims=True))
            return tuple(out)

        lax.fori_loop(0, nblk, blk, tuple(jnp.zeros((1, w), F32) for _ in range(nb)))
        gelu, _ = _gelu_and_grad(gt_ref[...])
        y_ref[...] = h_ref[...] * gelu

    col = lambda c: (0, c)
    return _call(
        body, "lru_fwd", (nt,),
        [pl.BlockSpec((n, w), col), pl.BlockSpec((n, w), lambda c: (0, nt + c)),
         pl.BlockSpec((CONV_W, w), col), pl.BlockSpec((1, w), col),
         pl.BlockSpec((w, w), lambda c: (c, c)), pl.BlockSpec((w, w), lambda c: (c, c)),
         pl.BlockSpec((1, w), col), pl.BlockSpec((1, w), col), pl.BlockSpec((1, w), col)],
        [pl.BlockSpec((n, w), col), pl.BlockSpec((n, w), col)],
        [jax.ShapeDtypeStruct((n, LRU_WIDTH), F32), jax.ShapeDtypeStruct((n, LRU_WIDTH), F32)],
        ("parallel",), (zl, zl, conv_w, conv_b, wa, wx, ba, bx, lam),
        scratch=[pltpu.VMEM((n, w), F32), pltpu.VMEM((n, w), F32)], comm=comm)


def lru_bwd(zl, hs, dy, conv_w, conv_b, wa, wx, ba, bx, lam, nb, lp, comm=None):
    n = zl.shape[0]
    w = LRU_TILE
    nt = LRU_WIDTH // w
    nblk = lp // 8

    def body(u_ref, gt_ref, h_ref, dy_ref, cw_ref, cb_ref, wa_ref, wx_ref, ba_ref, bx_ref, lam_ref,
             du_ref, dgt_ref, dcw_ref, dcb_ref, dba_ref, dbx_ref, dlam_ref, dwa_ref, dwx_ref,
             c_s, d_s, g_s, dwa_s, dwx_s):
        u = u_ref[...]
        lam = lam_ref[...]
        xc, xc16, ra, ia, sp, a, mult = _lru_gates(u, cw_ref, cb_ref[...], wa_ref[...], wx_ref[...],
                                                   ba_ref[...], bx_ref[...], lam)
        row = lax.broadcasted_iota(jnp.int32, (lp, w), 0)
        hv = h_ref[...]
        dyv = dy_ref[...]
        gelu, dgelu = _gelu_and_grad(gt_ref[...])
        dgt_ref[...] = jnp.where(row >= PAD, dyv * hv * dgelu, 0.0)
        c_s[...] = pltpu.roll(a, lp - 1, axis=0)
        d_s[...] = dyv * gelu
        r8 = _scan_block_rows(w)

        def blk(ii, carry):
            off = pl.multiple_of((nblk - 1 - ii) * 8, 8)
            cv = c_s[pl.ds(off, 8), :]
            dv = d_s[pl.ds(off, 8), :]
            for sh in (1, 2, 4):
                keep = r8 < 8 - sh
                dv = jnp.where(keep, cv * pltpu.roll(dv, 8 - sh, axis=0) + dv, dv)
                cv = jnp.where(keep, cv * pltpu.roll(cv, 8 - sh, axis=0), cv)
            gv = dv + cv * carry
            g_s[pl.ds(off, 8), :] = gv
            return jnp.sum(jnp.where(r8 == 0, gv, 0.0), axis=0, keepdims=True)

        lax.fori_loop(0, nblk, blk, jnp.zeros((1, w), F32))
        gv = g_s[...]
        first_row = row == PAD
        db = jnp.where(row >= PAD, gv, 0.0)
        da = jnp.where(row > PAD, gv * pltpu.roll(hv, 1, axis=0), 0.0)
        mult_eff = jnp.where(first_row, 1.0, mult)
        dmult = jnp.where(first_row, 0.0, db * (ia * xc))
        dia = db * mult_eff * xc
        dxc = db * mult_eff * ia
        dla = da * a - dmult * (a * a) / mult
        dra = dla * (-C_RGLRU * sp)
        dsp = jnp.sum(dla * (-C_RGLRU * ra), axis=0, keepdims=True)
        dpa = dra * ra * (1.0 - ra)
        dpx = dia * ia * (1.0 - ia)
        dpa16 = dpa.astype(BF16)
        dpx16 = dpx.astype(BF16)
        dxc = dxc + _nt(dpa16, wa_ref[...]) + _nt(dpx16, wx_ref[...])
        du = cw_ref[pl.ds(CONV_W - 1, 1), :] * dxc
        dcw = [jnp.sum(dxc * u, axis=0, keepdims=True)]
        for tap in range(1, CONV_W):
            dcw.insert(0, jnp.sum(dxc * pltpu.roll(u, tap, axis=0), axis=0, keepdims=True))
            du = du + cw_ref[pl.ds(CONV_W - 1 - tap, 1), :] * pltpu.roll(dxc, lp - tap, axis=0)
        du_ref[...] = jnp.where(row >= PAD, du, 0.0)
        first = pl.program_id(1) == 0
        _accumulate(dlam_ref, -_sig(-lam) * dsp, first)
        _accumulate(dba_ref, jnp.sum(dpa, axis=0, keepdims=True), first)
        _accumulate(dbx_ref, jnp.sum(dpx, axis=0, keepdims=True), first)
        _accumulate(dcb_ref, jnp.sum(dxc, axis=0, keepdims=True), first)
        _accumulate(dcw_ref, jnp.concatenate(dcw, axis=0), first)
        _accumulate(dwa_s, _tn(xc16, dpa16), first)
        _accumulate(dwx_s, _tn(xc16, dpx16), first)

        @pl.when(pl.program_id(1) == nb - 1)
        def _():
            for j in range(w // LRU_BLOCK):
                blk_rows = slice(j * LRU_BLOCK, (j + 1) * LRU_BLOCK)
                dwa_ref[0, blk_rows, :] = dwa_s[blk_rows, blk_rows]
                dwx_ref[0, blk_rows, :] = dwx_s[blk_rows, blk_rows]

    col = lambda c, b: (0, c)
    vec = pl.BlockSpec((1, w), col)
    mat = pl.BlockSpec((w, w), lambda c, b: (c, c))
    big = pl.BlockSpec((lp, w), lambda c, b: (b, c))
    dmat = pl.BlockSpec((1, w, LRU_BLOCK), lambda c, b: (c, 0, 0))
    return _call(
        body, "lru_bwd", (nt, nb),
        [big, pl.BlockSpec((lp, w), lambda c, b: (b, nt + c)), big, big,
         pl.BlockSpec((CONV_W, w), col), vec, mat, mat, vec, vec, vec],
        [big, big, pl.BlockSpec((CONV_W, w), col), vec, vec, vec, vec, dmat, dmat],
        [jax.ShapeDtypeStruct((n, LRU_WIDTH), F32), jax.ShapeDtypeStruct((n, LRU_WIDTH), F32),
         jax.ShapeDtypeStruct((CONV_W, LRU_WIDTH), F32), jax.ShapeDtypeStruct((1, LRU_WIDTH), F32),
         jax.ShapeDtypeStruct((1, LRU_WIDTH), F32), jax.ShapeDtypeStruct((1, LRU_WIDTH), F32),
         jax.ShapeDtypeStruct((1, LRU_WIDTH), F32), jax.ShapeDtypeStruct((nt, w, LRU_BLOCK), F32),
         jax.ShapeDtypeStruct((nt, w, LRU_BLOCK), F32)],
        ("parallel", "arbitrary"), (zl, zl, hs, dy, conv_w, conv_b, wa, wx, ba, bx, lam),
        scratch=[pltpu.VMEM((lp, w), F32), pltpu.VMEM((lp, w), F32), pltpu.VMEM((lp, w), F32),
                 pltpu.VMEM((w, w), F32), pltpu.VMEM((w, w), F32)], comm=comm)


def outproj_fwd(h, ya, yl, gao, glo, wout):
    n, d = h.shape
    half = ya.shape[1]
    tm = _tile(n, 704)

    def body(h_ref, ya_ref, yl_ref, gao_ref, glo_ref, w_ref, ho_ref, yn_ref):
        xa = ya_ref[...]
        xl = yl_ref[...]
        na = (xa * _rms_r(xa) * gao_ref[...]).astype(BF16)
        nl = (xl * _rms_r(xl) * glo_ref[...]).astype(BF16)
        yn_ref[:, :half] = na
        yn_ref[:, half:] = nl
        ho_ref[...] = h_ref[...] + _nn(na, w_ref[:half, :]) + _nn(nl, w_ref[half:, :])

    return pl.pallas_call(
        body, name="outproj_fwd", grid=(n // tm,),
        in_specs=[_row(tm, d), _row(tm, half), _row(tm, half), _fixed((1, half)), _fixed((1, half)), VMEM_WHOLE],
        out_specs=[_row(tm, d), _row(tm, 2 * half)],
        out_shape=[jax.ShapeDtypeStruct((n, d), F32), jax.ShapeDtypeStruct((n, 2 * half), BF16)],
        compiler_params=_params(("parallel",)),
    )(h, ya, yl, gao, glo, wout)


def outproj_bwd(dh, ya, yl, gao, glo, wout):
    n, d = dh.shape
    half = ya.shape[1]
    tm = _tile(n, 704)

    def body(dh_ref, ya_ref, yl_ref, gao_ref, glo_ref, w_ref, dya_ref, dyl_ref, dgao_ref, dglo_ref):
        d16 = dh_ref[...].astype(BF16)
        xa = ya_ref[...]
        xl = yl_ref[...]
        dxa, dga = _rms_bwd(xa, _rms_r(xa), gao_ref[...], _nt(d16, w_ref[:half, :]))
        dxl, dgl = _rms_bwd(xl, _rms_r(xl), glo_ref[...], _nt(d16, w_ref[half:, :]))
        dya_ref[...] = dxa
        dyl_ref[...] = dxl
        first = pl.program_id(0) == 0
        _accumulate(dgao_ref, dga, first)
        _accumulate(dglo_ref, dgl, first)

    return pl.pallas_call(
        body, name="outproj_bwd", grid=(n // tm,),
        in_specs=[_row(tm, d), _row(tm, half), _row(tm, half), _fixed((1, half)), _fixed((1, half)), VMEM_WHOLE],
        out_specs=[_row(tm, half), _row(tm, half), _fixed((1, half)), _fixed((1, half))],
        out_shape=[jax.ShapeDtypeStruct((n, half), F32), jax.ShapeDtypeStruct((n, half), F32),
                   jax.ShapeDtypeStruct((1, half), F32), jax.ShapeDtypeStruct((1, half), F32)],
        compiler_params=_params(("arbitrary",)),
    )(dh, ya, yl, gao, glo, wout)


def _loss_and_grad(x, gv, tgt, first_row):
    tm, d = x.shape
    r = _rms_r(x)
    row = first_row + lax.broadcasted_iota(jnp.int32, (tm, d), 0)
    diff = jnp.where(row >= FIRST_FRAME, x * r * gv - tgt, 0.0)
    part = 0.5 * jnp.sum(jnp.sum(diff * diff, axis=-1, keepdims=True) * (1.0 / d), axis=0, keepdims=True)
    dx, dg = _rms_bwd(x, r, gv, diff * (1.0 / d))
    return dx, part, dg


def assemble_cols(g, name):
    _, k, ns = g.shape

    def body(g_ref, o_ref):
        for j in range(N_DEV):
            o_ref[:, j * ns:(j + 1) * ns] = g_ref[j]

    return pl.pallas_call(body, name=name, out_shape=jax.ShapeDtypeStruct((k, N_DEV * ns), g.dtype),
                          compiler_params=_params(None))(g)


def split_cols(x, name):
    k, cols = x.shape
    ns = cols // N_DEV

    def body(x_ref, o_ref):
        for j in range(N_DEV):
            o_ref[j] = x_ref[:, j * ns:(j + 1) * ns]

    return pl.pallas_call(body, name=name, out_shape=jax.ShapeDtypeStruct((N_DEV, k, ns), x.dtype),
                          compiler_params=_params(None))(x)


def _slab_rows(w, per_head):
    k = w.shape[1]
    w = w.reshape(MLA_HEADS, per_head, k)
    return jnp.pad(w, ((0, 0), (0, HEAD_SLAB - per_head), (0, 0))).reshape(MLA_HEADS * HEAD_SLAB, k)


def _unslab_rows(w, per_head):
    k = w.shape[1]
    return w.reshape(MLA_HEADS, HEAD_SLAB, k)[:, :per_head].reshape(MLA_HEADS * per_head, k)


def meta_grad(dh0, nb, lp):
    d = dh0.shape[1]
    ns = d // N_DEV
    per_seq = lp // N_META

    def body(x_ref, o_ref):
        x = x_ref[...]
        for j in range(N_DEV):
            _accumulate(o_ref.at[j], x[:, j * ns:(j + 1) * ns], pl.program_id(0) == 0)

    return pl.pallas_call(
        body, name="meta_grad", grid=(nb,),
        in_specs=[pl.BlockSpec((N_META, d), lambda b: (b * per_seq + PAD // N_META, 0))],
        out_specs=pl.BlockSpec((N_DEV, N_META, ns), lambda b: (0, 0, 0)),
        out_shape=jax.ShapeDtypeStruct((N_DEV, N_META, ns), F32),
        compiler_params=_params(("arbitrary",)))(dh0)


VECTORS = [("ffn1_norm", 1024), ("mix_norm", 1024), ("q_latent_norm", 384), ("kv_latent_norm", 256),
           ("q_head_norm", 192), ("k_head_norm", 192), ("conv_b", 512), ("gate_a_b", 512), ("gate_x_b", 512),
           ("lru_lambda", 512), ("attn_out_norm", 512), ("lru_out_norm", 512), ("ffn2_norm", 1024),
           ("final_norm", 1024)]
VEC_ROWS = 16
LOSS_ROW = len(VECTORS)
GATES = ["gate_a_w", "gate_x_w"]


def pack_vectors(grads, loss):
    def body(*refs):
        o_ref = refs[-1]
        o_ref[...] = jnp.zeros_like(o_ref)
        for t, (ref, (_, cnt)) in enumerate(zip(refs[:-2], VECTORS)):
            o_ref[t:t + 1, :cnt] = ref[:, :cnt]
        o_ref[LOSS_ROW:LOSS_ROW + 1, :LANES] = refs[-2][...]

    return pl.pallas_call(body, name="pack_vectors", out_shape=jax.ShapeDtypeStruct((VEC_ROWS, D_MODEL), F32),
                          compiler_params=_params(None))(*[grads[name] for name, _ in VECTORS], loss)


def _adamw_update(w, g, m, v):
    c1 = 1.0 / (1.0 - ADAM_B1 ** ADAM_STEP)
    c2 = 1.0 / (1.0 - ADAM_B2 ** ADAM_STEP)
    mn = ADAM_B1 * m + (1.0 - ADAM_B1) * g
    vn = ADAM_B2 * v + (1.0 - ADAM_B2) * (g * g)
    delta = -ADAM_LR * ((mn * c1) / (jnp.sqrt(vn * c2) + ADAM_EPS) + ADAM_WD * w)
    return delta, mn, vn


def _sum_slots(ref, index=()):
    acc = ref[(0,) + index].astype(F32)
    for s in range(1, N_DEV):
        acc = acc + ref[(s,) + index].astype(F32)
    return acc


def adamw_sharded(r, w, m, v, name):
    rows, cols = w.shape
    tr = _tile(rows, 256, 16) if rows % 16 == 0 else rows

    def body(r_ref, w_ref, m_ref, v_ref, g_ref, d_ref, mo_ref, vo_ref):
        g = _sum_slots(r_ref)
        g_ref[...] = g
        d_ref[...], mo_ref[...], vo_ref[...] = _adamw_update(w_ref[...], g, m_ref[...], v_ref[...])

    spec = pl.BlockSpec((tr, cols), lambda i: (i, 0))
    shape = jax.ShapeDtypeStruct((rows, cols), F32)
    return pl.pallas_call(
        body, name=name, grid=(rows // tr,),
        in_specs=[pl.BlockSpec((N_DEV, tr, cols), lambda i: (0, i, 0))] + [spec] * 3,
        out_specs=[spec] * 4, out_shape=[shape] * 4,
        compiler_params=_params(("parallel",)),
    )(r, w, m, v)


def adamw_small(r_vec, r_gates, w, m, v):
    nt = len(VECTORS) + len(GATES)

    def body(*refs):
        rv_ref = refs[0]
        rg_refs = refs[1:1 + len(GATES)]
        base = 1 + len(GATES)
        w_refs, m_refs, v_refs = (refs[base + i * nt:base + (i + 1) * nt] for i in range(3))
        outs = refs[base + 3 * nt:]
        g_o, d_o, m_o, v_o = (outs[i * nt:(i + 1) * nt] for i in range(4))
        outs[4 * nt][...] = _sum_slots(rv_ref, (slice(LOSS_ROW, LOSS_ROW + 1), slice(0, LANES)))
        for t in range(nt):
            if t < len(VECTORS):
                cnt = VECTORS[t][1]
                g = _sum_slots(rv_ref, (slice(t, t + 1), slice(0, cnt)))
            else:
                g = _sum_slots(rg_refs[t - len(VECTORS)])
            g_o[t][...] = g
            d_o[t][...], m_o[t][...], v_o[t][...] = _adamw_update(w_refs[t][...], g, m_refs[t][...], v_refs[t][...])

    shapes = [jax.ShapeDtypeStruct(a.shape, F32) for a in w]
    res = pl.pallas_call(body, name="adamw_small", out_shape=shapes * 4 + [jax.ShapeDtypeStruct((1, LANES), F32)],
                         compiler_params=_params(None))(r_vec, *r_gates, *w, *m, *v)
    return [res[i * nt:(i + 1) * nt] for i in range(4)], res[4 * nt]


def _block_diag(w):
    nb, n, _ = w.shape
    eye = jnp.eye(nb, dtype=w.dtype)
    return (eye[:, None, :, None] * w[:, :, None, :]).reshape(nb * n, nb * n)


def _two_d(a):
    if a.ndim == 3:
        return a.reshape(a.shape[1], a.shape[2])
    if a.ndim == 4:
        return a.reshape(a.shape[1] * a.shape[2], a.shape[3])
    return a


_WEIGHT_NAMES = ['meta_tokens', 'ffn1_norm', 'ffn1_w_gate', 'ffn1_w_up', 'ffn1_w_down', 'mix_norm', 'w_in',
                 'q_latent_norm', 'w_uq', 'kv_latent_norm', 'w_uk', 'w_uv', 'q_head_norm', 'k_head_norm', 'conv_w',
                 'conv_b', 'gate_a_w', 'gate_a_b', 'gate_x_w', 'gate_x_b', 'lru_lambda', 'attn_out_norm',
                 'lru_out_norm', 'w_out', 'ffn2_norm', 'ffn2_w_gate', 'ffn2_w_up', 'ffn2_w_down', 'final_norm']


COLUMN_SHARDED = ("ffn1_w_gate", "ffn1_w_up", "ffn2_w_gate", "ffn2_w_up", "w_in", "w_uq", "w_uk", "w_uv")


def train_step(x, tgt, w, m, v):
    nb, seq, d = x.shape
    lp = PAD + N_META + seq
    n = nb * lp

    def local(a, name):
        a = _two_d(a)
        return a.T if name in COLUMN_SHARDED else a

    sh = {name: local(w[name], name) for name in _WEIGHT_NAMES}
    m2 = {name: local(m[name], name) for name in _WEIGHT_NAMES}
    v2 = {name: local(v[name], name) for name in _WEIGHT_NAMES}

    def b16(name):
        return sh[name].astype(BF16)

    out = {}

    def update(name, landed):
        out[name] = adamw_sharded(landed, sh[name], m2[name], v2[name], "adamw_" + name)

    g_wg1, g_wu1, g_meta, g_conv = exchange(
        [b16("ffn1_w_gate"), b16("ffn1_w_up"), sh["meta_tokens"], sh["conv_w"]], ["gather"] * 4, "gather_ffn1")
    wg1, wu1 = g_wg1.reshape(D_FF, d), g_wu1.reshape(D_FF, d)
    meta = assemble_cols(g_meta, "assemble_meta")
    conv_w = assemble_cols(g_conv, "assemble_conv")

    front = jnp.concatenate([jnp.zeros((PAD, d), F32), meta], axis=0)
    h0 = jnp.concatenate([jnp.broadcast_to(front[None], (nb, FIRST_FRAME, d)), x], axis=1).reshape(n, d)
    tgt_p = jnp.concatenate([jnp.zeros((nb, FIRST_FRAME, d), F32), tgt], axis=1).reshape(n, d)
    tables = _rope_tables(lp)
    zero_tail = jnp.zeros((1, HEAD_SLAB - D_QK), F32)
    gqh = jnp.concatenate([sh["q_head_norm"], zero_tail], axis=1)
    gkh = jnp.concatenate([sh["k_head_norm"], zero_tail], axis=1)
    wa = _block_diag(w["gate_a_w"][0]).astype(BF16)
    wx = _block_diag(w["gate_x_w"][0]).astype(BF16)

    (u1, a1, b1, s1), (g_wd1, g_in) = ffn_up(h0, sh["ffn1_norm"], wg1, wu1, "ffn1_up",
                                             comm=([b16("ffn1_w_down"), b16("w_in")], ["gather"] * 2))
    wd1 = g_wd1.reshape(D_FF, d)
    mla_rows = MLA_IN - D_ROPE
    w_in = g_in.reshape(mla_rows + 2 * LRU_WIDTH, d)
    wm = jnp.concatenate([w_in[:mla_rows], jnp.zeros((D_ROPE, d), BF16)], axis=0)
    wl = w_in[mla_rows:]
    (h1, u2, zm, zl), (g_uq, g_uk, g_uv, g_out) = ffn_down_inproj(
        h0, s1, wd1, sh["mix_norm"], wm, wl, "ffn1_down_inproj",
        comm=([b16("w_uq"), b16("w_uk"), b16("w_uv"), b16("w_out")], ["gather"] * 4))
    wuq = _slab_rows(g_uq.reshape(MLA_HEADS * D_QK, Q_RANK), D_QK)
    wuk = _slab_rows(g_uk.reshape(MLA_HEADS * D_NOPE, KV_RANK), D_NOPE)
    wuv = g_uv.reshape(MLA_HEADS * D_V, KV_RANK)
    w_out = g_out.reshape(d, d)

    q, k, vv, qn, cn = mla_prep_fwd(zm, sh["q_latent_norm"], sh["kv_latent_norm"], wuq, wuk, wuv, gqh, gkh, tables, lp)
    (y_mla, lse), (g_wu2, g_wd2) = attn_fwd(
        q, k, vv, nb, lp, comm=([b16("ffn2_w_up"), b16("ffn2_w_down")], ["gather"] * 2))
    (y_lru, hs), (g_wg2,) = lru_fwd(zl, conv_w, sh["conv_b"], wa, wx, sh["gate_a_b"], sh["gate_x_b"], sh["lru_lambda"],
                                    nb, lp, comm=([b16("ffn2_w_gate")], ["gather"]))
    wg2, wu2, wd2 = (g.reshape(D_FF, d) for g in (g_wg2, g_wu2, g_wd2))
    h2, yn = outproj_fwd(h1, y_mla, y_lru, sh["attn_out_norm"], sh["lru_out_norm"], w_out)
    dh3, u3, a3, b3, loss, g_final = ffn_fwd_loss(h2, sh["ffn2_norm"], wg2, wu2, wd2, sh["final_norm"], tgt_p, lp,
                                                  "ffn2_fwd_loss")

    vec = {"final_norm": g_final}
    (dh2, da3, db3, sh3, vec["ffn2_norm"]), _ = ffn_bwd_act(dh3, h2, sh["ffn2_norm"], a3, b3, wg2, wu2, wd2, "ffn2_bwd")
    ff_shards = (N_DEV, D_FF // N_DEV, d)
    dwg2 = tn_matmul(da3, u3, "ffn2_dwg", "bf16").reshape(ff_shards)
    dwu2 = tn_matmul(db3, u3, "ffn2_dwu", "bf16").reshape(ff_shards)
    dwd2 = tn_matmul(sh3, dh3, "ffn2_dwd", "bf16").reshape(ff_shards)

    dy_mla, dy_lru, vec["attn_out_norm"], vec["lru_out_norm"] = outproj_bwd(
        dh2, y_mla, y_lru, sh["attn_out_norm"], sh["lru_out_norm"], w_out)
    dw_out = tn_matmul(yn, dh2, "dw_out", "bf16").reshape(N_DEV, d // N_DEV, d)
    (du, dgate, dconv, vec["conv_b"], vec["gate_a_b"], vec["gate_x_b"], vec["lru_lambda"], dga, dgx), (r_wg2,) = lru_bwd(
        zl, hs, dy_lru, conv_w, sh["conv_b"], wa, wx, sh["gate_a_b"], sh["gate_x_b"], sh["lru_lambda"], nb, lp,
        comm=([dwg2], ["scatter"]))
    update("ffn2_w_gate", r_wg2)

    (dq, dk, dv), (r_wu2,) = attn_bwd(q, k, vv, y_mla, dy_mla, lse, nb, lp, comm=([dwu2], ["scatter"]))
    update("ffn2_w_up", r_wu2)

    (dzm, dqr, dkr, vec["q_latent_norm"], vec["kv_latent_norm"], vec["q_head_norm"], vec["k_head_norm"]), (r_wd2,) = (
        mla_prep_bwd(dq, dk, dv, zm, qn, cn, sh["q_latent_norm"], sh["kv_latent_norm"], wuq, wuk, wuv, gqh, gkh,
                     tables, lp, comm=([dwd2], ["scatter"])))
    update("ffn2_w_down", r_wd2)
    dwuq = _unslab_rows(tn_matmul(dqr, qn, "dw_uq"), D_QK).reshape(N_DEV, -1, Q_RANK)
    dwuk = _unslab_rows(tn_matmul(dkr, cn, "dw_uk"), D_NOPE).reshape(N_DEV, -1, KV_RANK)
    dwuv = tn_matmul(dv, cn, "dw_uv").reshape(N_DEV, -1, KV_RANK)
    dh1, vec["mix_norm"] = inproj_bwd(dzm, du, dgate, dh2, h1, sh["mix_norm"], wm, wl)
    dw_in = jnp.concatenate([tn_matmul(dzm, u2, "dw_in_mla", "bf16")[:mla_rows], tn_matmul(du, u2, "dw_in_u", "bf16"),
                             tn_matmul(dgate, u2, "dw_in_gate", "bf16")], axis=0).reshape(N_DEV, -1, d)

    (dh0, da1, db1, vec["ffn1_norm"]), landed = ffn_bwd_act(
        dh1, h0, sh["ffn1_norm"], a1, b1, wg1, wu1, wd1, "ffn1_bwd", emit_sh=False,
        comm=([dw_in, dwuq, dwuk, dwuv, dw_out, split_cols(dconv, "split_conv")], ["scatter"] * 6))
    for name, r in zip(("w_in", "w_uq", "w_uk", "w_uv", "w_out", "conv_w"), landed):
        update(name, r)

    dwg1 = tn_matmul(da1, u1, "ffn1_dwg", "bf16").reshape(ff_shards)
    dwu1, (r_wg1,) = tn_matmul(db1, u1, "ffn1_dwu", "bf16", comm=([dwg1], ["scatter"]))
    dwd1, (r_wu1,) = tn_matmul(s1, dh1, "ffn1_dwd", "bf16", comm=([dwu1.reshape(ff_shards)], ["scatter"]))
    dwd1 = dwd1.reshape(ff_shards)
    dmeta = meta_grad(dh0, nb, lp)
    gates = [dga.reshape(LRU_WIDTH, LRU_BLOCK), dgx.reshape(LRU_WIDTH, LRU_BLOCK)]
    r_wd1, r_meta, r_vec, r_ga, r_gx = exchange(
        [dwd1, dmeta, pack_vectors(vec, loss)] + gates, ["scatter"] * 2 + ["gather"] * 3, "exchange_last")
    update("ffn1_w_gate", r_wg1)
    update("ffn1_w_up", r_wu1)
    update("ffn1_w_down", r_wd1)
    update("meta_tokens", r_meta)

    small = [name for name, _ in VECTORS] + GATES
    res, total_loss = adamw_small(r_vec, [r_ga, r_gx], [sh[nm] for nm in small], [m2[nm] for nm in small],
                                  [v2[nm] for nm in small])
    for i, name in enumerate(small):
        out[name] = [res[j][i] for j in range(4)]

    grad_x = dh0.reshape(nb, lp, d)[:, FIRST_FRAME:]
    loss = total_loss[0, 0]

    def as_given(a, name):
        return (a.T if name in COLUMN_SHARDED else a).reshape(w[name].shape)

    cols = [[as_given(out[name][j], name) for name in _WEIGHT_NAMES] for j in range(4)]
    return (loss, grad_x, *cols[0], *cols[1], *cols[2], *cols[3])


def kernel(x, meta_tokens, ffn1_norm, ffn1_w_gate, ffn1_w_up, ffn1_w_down, mix_norm, w_in, q_latent_norm, w_uq, kv_latent_norm, w_uk, w_uv, q_head_norm, k_head_norm, conv_w, conv_b, gate_a_w, gate_a_b, gate_x_w, gate_x_b, lru_lambda, attn_out_norm, lru_out_norm, w_out, ffn2_norm, ffn2_w_gate, ffn2_w_up, ffn2_w_down, final_norm, loss_target, m_meta_tokens, m_ffn1_norm, m_ffn1_w_gate, m_ffn1_w_up, m_ffn1_w_down, m_mix_norm, m_w_in, m_q_latent_norm, m_w_uq, m_kv_latent_norm, m_w_uk, m_w_uv, m_q_head_norm, m_k_head_norm, m_conv_w, m_conv_b, m_gate_a_w, m_gate_a_b, m_gate_x_w, m_gate_x_b, m_lru_lambda, m_attn_out_norm, m_lru_out_norm, m_w_out, m_ffn2_norm, m_ffn2_w_gate, m_ffn2_w_up, m_ffn2_w_down, m_final_norm, v_meta_tokens, v_ffn1_norm, v_ffn1_w_gate, v_ffn1_w_up, v_ffn1_w_down, v_mix_norm, v_w_in, v_q_latent_norm, v_w_uq, v_kv_latent_norm, v_w_uk, v_w_uv, v_q_head_norm, v_k_head_norm, v_conv_w, v_conv_b, v_gate_a_w, v_gate_a_b, v_gate_x_w, v_gate_x_b, v_lru_lambda, v_attn_out_norm, v_lru_out_norm, v_w_out, v_ffn2_norm, v_ffn2_w_gate, v_ffn2_w_up, v_ffn2_w_down, v_final_norm):
    args = locals()
    w = {name: args[name] for name in _WEIGHT_NAMES}
    m = {name: args["m_" + name] for name in _WEIGHT_NAMES}
    v = {name: args["v_" + name] for name in _WEIGHT_NAMES}
    return train_step(x, loss_target, w, m, v)
```

```python
import math

import jax
import jax.numpy as jnp
from jax import lax
from jax.experimental import pallas as pl
from jax.experimental.pallas import tpu as pltpu

F32 = jnp.float32
BF16 = jnp.bfloat16

D_MODEL = 1024
CHUNK = 64
CHUNK_SHIFT = 6
N_META = 16
PAD = CHUNK - N_META
FIRST_FRAME = PAD + N_META
MLA_HEADS = 4
D_NOPE = 128
D_ROPE = 64
D_QK = D_NOPE + D_ROPE
D_V = 128
HEAD_SLAB = 256
KV_RANK = 256
Q_RANK = 384
ROPE_THETA = 10000.0
LRU_WIDTH = 512
LRU_BLOCKS = 8
LRU_BLOCK = 64
LRU_TILE = 128
CONV_W = 4
C_RGLRU = 8.0
D_FF = 2816
MLA_IN = 768
EPS = 1e-6
NEG_INF = -1e30
N_DEV = 8
LANES = 128
VMEM_LIMIT = 52 * 1024 * 1024
ATTN_HEADS_PER_STEP = 2
TN_ROWS = 4224
TN_X_BYTES = 12 * 1024 * 1024
TN_Y_BYTES = 9 * 1024 * 1024 // 2

ADAM_LR = 0.001
ADAM_B1 = 0.9
ADAM_B2 = 0.999
ADAM_EPS = 1e-08
ADAM_WD = 0.01
ADAM_STEP = 10

VMEM_WHOLE = pl.BlockSpec(memory_space=pltpu.VMEM)
HBM_WHOLE = pl.BlockSpec(memory_space=pl.ANY)


def _params(sems):
    if sems is None:
        return pltpu.CompilerParams(vmem_limit_bytes=VMEM_LIMIT)
    return pltpu.CompilerParams(dimension_semantics=sems, vmem_limit_bytes=VMEM_LIMIT)


def _tile(n, cap, mult=16):
    best = None
    for t in range(mult, min(n, cap) + 1, mult):
        if n % t == 0:
            best = t
    assert best is not None, (n, cap, mult)
    return best


def _row(tm, d):
    return pl.BlockSpec((tm, d), lambda i: (i, 0))


def _fixed(shape):
    return pl.BlockSpec(shape, lambda i: (0,) * len(shape))


def _mesh_position():
    return lax.axis_index("x"), lax.axis_index("y"), lax.axis_index("c")


def _flat_index(x, y, c):
    return 4 * x + 2 * y + c


def _peers(x, y, c):
    out = []
    for k in range(1, N_DEV):
        fx, fy, fc = (k >> 2) & 1, (k >> 1) & 1, k & 1
        out.append((1 - x if fx else x, 1 - y if fy else y, 1 - c if fc else c))
    return out


def _comm_out_shapes(srcs, modes):
    return [jax.ShapeDtypeStruct((N_DEV,) + s.shape if md == "gather" else s.shape, s.dtype)
            for s, md in zip(srcs, modes)]


def _comm_scratch(n):
    per_peer = n * (N_DEV - 1)
    return [pltpu.SemaphoreType.DMA((per_peer,)), pltpu.SemaphoreType.DMA((per_peer,)), pltpu.SemaphoreType.DMA((n,))]


class _Copies:
    def __init__(self, own, first, relay):
        self.own, self.first, self.relay = own, first, relay

    def start(self):
        for cp in self.own + self.first:
            cp.start()

    def forward(self):
        for arrival, onward in self.relay:
            arrival.wait_recv()
            onward.start()

    def finish(self):
        arrivals = [a for a, _ in self.relay]
        onward = [f for _, f in self.relay]
        for cp in self.first + onward:
            if not any(cp is a for a in arrivals):
                cp.wait_recv()
        for cp in self.first + onward:
            cp.wait_send()
        for cp in self.own:
            cp.wait()


def _comm_copies(src_refs, dst_refs, modes, send, recv, local):
    x, y, c = _mesh_position()
    me = _flat_index(x, y, c)
    n = len(modes)
    sibling = (x, y, 1 - c)
    chips = [(1 - x, y), (x, 1 - y), (1 - x, 1 - y)]

    def remote(src, dst, k, t, to):
        return pltpu.make_async_remote_copy(src_ref=src, dst_ref=dst, send_sem=send.at[k * n + t],
                                            recv_sem=recv.at[k * n + t], device_id=to,
                                            device_id_type=pl.DeviceIdType.MESH)

    own, first, relay = [], [], []
    for t, (src, dst, md) in enumerate(zip(src_refs, dst_refs, modes)):
        if md == "scatter":
            own.append(pltpu.make_async_copy(src.at[me], dst.at[me], local.at[t]))
            for k, peer in enumerate(_peers(x, y, c)):
                first.append(remote(src.at[_flat_index(*peer)], dst.at[me], k, t, peer))
        else:
            own.append(pltpu.make_async_copy(src, dst.at[me], local.at[t]))
            first.append(remote(src, dst.at[me], 0, t, sibling))
            for j, chip in enumerate(chips):
                arrival = remote(src, dst.at[me], 1 + j, t, (*chip, c))
                landed = dst.at[_flat_index(*chip, c)]
                first.append(arrival)
                relay.append((arrival, remote(landed, landed, 4 + j, t, sibling)))
    return _Copies(own, first, relay)


def _hosted(body, n_in, n_out, modes, grid):
    t = len(modes)
    total = math.prod(grid)

    def wrapped(*refs):
        ins, csrc = refs[:n_in], refs[n_in:n_in + t]
        outs = refs[n_in + t:n_in + t + n_out]
        cdst = refs[n_in + t + n_out:n_in + 2 * t + n_out]
        scratch = refs[n_in + 2 * t + n_out:-3]
        copies = _comm_copies(csrc, cdst, modes, *refs[-3:])
        step = pl.program_id(0)
        for axis in range(1, len(grid)):
            step = step * grid[axis] + pl.program_id(axis)

        @pl.when(step == 0)
        def _():
            copies.start()

        body(*ins, *outs, *scratch)

        @pl.when(step == (total * 3) // 5)
        def _():
            copies.forward()

        @pl.when(step == total - 1)
        def _():
            copies.finish()

    return wrapped


def _call(body, name, grid, in_specs, out_specs, out_shape, sems, args, scratch=(), comm=None):
    if comm is None:
        outs = pl.pallas_call(body, name=name, grid=grid, in_specs=in_specs, out_specs=out_specs, out_shape=out_shape,
                              scratch_shapes=list(scratch), compiler_params=_params(sems))(*args)
        return outs, []
    srcs, modes = comm
    n = len(modes)
    res = pl.pallas_call(
        _hosted(body, len(in_specs), len(out_specs), modes, grid), name=name, grid=grid,
        in_specs=list(in_specs) + [HBM_WHOLE] * n, out_specs=list(out_specs) + [HBM_WHOLE] * n,
        out_shape=list(out_shape) + _comm_out_shapes(srcs, modes),
        scratch_shapes=list(scratch) + _comm_scratch(n),
        compiler_params=_params(("arbitrary",) * len(grid)))(*args, *srcs)
    return res[:len(out_specs)], res[len(out_specs):]


def exchange(srcs, modes, name):
    n = len(modes)

    def body(*refs):
        copies = _comm_copies(refs[:n], refs[n:2 * n], modes, *refs[2 * n:])
        copies.start()
        copies.forward()
        copies.finish()

    return pl.pallas_call(body, name=name, in_specs=[HBM_WHOLE] * n, out_specs=[HBM_WHOLE] * n,
                          out_shape=_comm_out_shapes(srcs, modes), scratch_shapes=_comm_scratch(n))(*srcs)


def _nn(a, b):
    return jnp.dot(a, b, preferred_element_type=F32)


def _nt(a, b):
    return lax.dot_general(a, b, (((1,), (1,)), ((), ())), preferred_element_type=F32)


def _tn(a, b):
    return lax.dot_general(a, b, (((0,), (0,)), ((), ())), preferred_element_type=F32)


def _sig(x):
    return 1.0 / (1.0 + jnp.exp(-x))


def _rms_r(x, n=None):
    n = x.shape[-1] if n is None else n
    return lax.rsqrt(jnp.sum(x * x, axis=-1, keepdims=True) * (1.0 / n) + EPS)


def _rms_bwd(x, r, g, dy, n=None):
    n = x.shape[-1] if n is None else n
    xhat = x * r
    dxhat = dy * g
    dx = r * (dxhat - xhat * (jnp.sum(dxhat * xhat, axis=-1, keepdims=True) * (1.0 / n)))
    return dx, jnp.sum(dy * xhat, axis=0, keepdims=True)


def _accumulate(ref, val, first):
    @pl.when(first)
    def _():
        ref[...] = val

    @pl.when(jnp.logical_not(first))
    def _():
        ref[...] += val


_GELU_C = math.sqrt(2.0 / math.pi)


def _gelu_and_grad(x):
    inner = _GELU_C * (x + 0.044715 * x * x * x)
    t = jnp.tanh(inner)
    gelu = 0.5 * x * (1.0 + t)
    dgelu = 0.5 * (1.0 + t) + 0.5 * x * (1.0 - t * t) * _GELU_C * (1.0 + 3.0 * 0.044715 * x * x)
    return gelu, dgelu


def _log1p_small(t):
    return jnp.where(t < 1e-3, t * (1.0 - t * (0.5 - t * (1.0 / 3.0))), jnp.log(1.0 + t))


def _softplus(x):
    return jnp.maximum(x, 0.0) + _log1p_small(jnp.exp(-jnp.abs(x)))


def _sig_tanh(x):
    return 0.5 + 0.5 * jnp.tanh(0.5 * x)


def _ff_chunks(f):
    return 2 if (f // 2) % LANES == 0 else 1


def _swiglu_half(x, g_ref, wg_ref, wu_ref, wd_ref, a_ref, b_ref, fc):
    f = wg_ref.shape[0]
    u = (x * _rms_r(x) * g_ref[...]).astype(BF16)
    acc = jnp.zeros(x.shape, F32)
    for c in range(f // fc):
        cols = slice(c * fc, (c + 1) * fc)
        a = _nt(u, wg_ref[cols, :])
        b = _nt(u, wu_ref[cols, :])
        s = (a * _sig(a) * b).astype(BF16)
        acc = acc + _nn(s, wd_ref[cols, :])
        a_ref[:, cols] = a.astype(BF16)
        b_ref[:, cols] = b.astype(BF16)
    return x + 0.5 * acc, u


def ffn_up(h, g, wg, wu, name, comm=None):
    n, d = h.shape
    f = wg.shape[0]
    tm = _tile(n, 528)
    fc = 2 * LANES if f % (2 * LANES) == 0 else f

    def body(h_ref, g_ref, wg_ref, wu_ref, u_ref, a_ref, b_ref, s_ref):
        x = h_ref[...]
        u = (x * _rms_r(x) * g_ref[...]).astype(BF16)
        u_ref[...] = u
        for c in range(f // fc):
            cols = slice(c * fc, (c + 1) * fc)
            a = _nt(u, wg_ref[cols, :])
            b = _nt(u, wu_ref[cols, :])
            a_ref[:, cols] = a.astype(BF16)
            b_ref[:, cols] = b.astype(BF16)
            s_ref[:, cols] = (0.5 * (a * _sig(a) * b)).astype(BF16)

    wide = jax.ShapeDtypeStruct((n, f), BF16)
    return _call(
        body, name, (n // tm,),
        [_row(tm, d), _fixed((1, d)), VMEM_WHOLE, VMEM_WHOLE],
        [_row(tm, d), _row(tm, f), _row(tm, f), _row(tm, f)],
        [jax.ShapeDtypeStruct((n, d), BF16), wide, wide, wide],
        ("parallel",), (h, g, wg, wu), comm=comm)


def ffn_down_inproj(h, s, wd, g, wm, wl, name, comm=None):
    n, d = h.shape
    f = wd.shape[0]
    tm = _tile(n, 528)

    def body(h_ref, s_ref, wd_ref, g_ref, wm_ref, wl_ref, ho_ref, u_ref, zm_ref, zl_ref):
        x = h_ref[...] + _nn(s_ref[...], wd_ref[...])
        ho_ref[...] = x
        u = (x * _rms_r(x) * g_ref[...]).astype(BF16)
        u_ref[...] = u
        zm_ref[...] = _nt(u, wm_ref[...])
        zl_ref[...] = _nt(u, wl_ref[...])

    return _call(
        body, name, (n // tm,),
        [_row(tm, d), _row(tm, f), VMEM_WHOLE, _fixed((1, d)), VMEM_WHOLE, VMEM_WHOLE],
        [_row(tm, d), _row(tm, d), _row(tm, MLA_IN), _row(tm, 2 * LRU_WIDTH)],
        [jax.ShapeDtypeStruct((n, d), F32), jax.ShapeDtypeStruct((n, d), BF16),
         jax.ShapeDtypeStruct((n, MLA_IN), F32), jax.ShapeDtypeStruct((n, 2 * LRU_WIDTH), F32)],
        ("parallel",), (h, s, wd, g, wm, wl), comm=comm)


def ffn_fwd_loss(h, g, wg, wu, wd, g_final, tgt, lp, name):
    n, d = h.shape
    f = wg.shape[0]
    tm = _tile(lp, 528)
    per_seq = lp // tm
    fc = 2 * LANES if f % (2 * LANES) == 0 else f

    def body(h_ref, g_ref, wg_ref, wu_ref, wd_ref, gf_ref, t_ref, dh_ref, u_ref, a_ref, b_ref, loss_ref, dgf_ref):
        i = pl.program_id(0)
        y, u_ref[...] = _swiglu_half(h_ref[...], g_ref, wg_ref, wu_ref, wd_ref, a_ref, b_ref, fc)
        dh_ref[...], part, dg = _loss_and_grad(y, gf_ref[...], t_ref[...], (i % per_seq) * tm)
        _accumulate(loss_ref, jnp.broadcast_to(part, (1, LANES)), i == 0)
        _accumulate(dgf_ref, dg, i == 0)

    outs, _ = _call(
        body, name, (n // tm,),
        [_row(tm, d), _fixed((1, d)), VMEM_WHOLE, VMEM_WHOLE, VMEM_WHOLE, _fixed((1, d)), _row(tm, d)],
        [_row(tm, d), _row(tm, d), _row(tm, f), _row(tm, f), _fixed((1, LANES)), _fixed((1, d))],
        [jax.ShapeDtypeStruct((n, d), F32), jax.ShapeDtypeStruct((n, d), BF16),
         jax.ShapeDtypeStruct((n, f), BF16), jax.ShapeDtypeStruct((n, f), BF16),
         jax.ShapeDtypeStruct((1, LANES), F32), jax.ShapeDtypeStruct((1, d), F32)],
        ("arbitrary",), (h, g, wg, wu, wd, g_final, tgt))
    return outs


def ffn_bwd_act(dh, h, g, a, b, wg, wu, wd, name, comm=None, emit_sh=True):
    n, d = h.shape
    f = wg.shape[0]
    tm = _tile(n, 352 if emit_sh else 384)
    nc = _ff_chunks(f)
    fc = f // nc

    def body(dh_ref, h_ref, g_ref, a_ref, b_ref, wg_ref, wu_ref, wd_ref, dhi_ref, da_ref, db_ref, *rest):
        dg_ref = rest[-1]
        x = h_ref[...]
        dy = dh_ref[...]
        r = _rms_r(x)
        dhh = (0.5 * dy).astype(BF16)
        du = jnp.zeros((tm, d), F32)
        for c in range(nc):
            cols = slice(c * fc, (c + 1) * fc)
            ds = _nt(dhh, wd_ref[cols, :])
            av = a_ref[:, cols].astype(F32)
            bv = b_ref[:, cols].astype(F32)
            sg = _sig(av)
            sil = av * sg
            da = (ds * bv * (sg * (1.0 + av * (1.0 - sg)))).astype(BF16)
            db = (ds * sil).astype(BF16)
            da_ref[:, cols] = da
            db_ref[:, cols] = db
            if emit_sh:
                rest[0][:, cols] = (0.5 * sil * bv).astype(BF16)
            du = du + _nn(da, wg_ref[cols, :]) + _nn(db, wu_ref[cols, :])
        dx, dg = _rms_bwd(x, r, g_ref[...], du)
        dhi_ref[...] = dy + dx
        _accumulate(dg_ref, dg, pl.program_id(0) == 0)

    wide = [jax.ShapeDtypeStruct((n, f), BF16)] * (3 if emit_sh else 2)
    return _call(
        body, name, (n // tm,),
        [_row(tm, d), _row(tm, d), _fixed((1, d)), _row(tm, f), _row(tm, f), VMEM_WHOLE, VMEM_WHOLE, VMEM_WHOLE],
        [_row(tm, d)] + [_row(tm, f)] * len(wide) + [_fixed((1, d))],
        [jax.ShapeDtypeStruct((n, d), F32)] + wide + [jax.ShapeDtypeStruct((1, d), F32)],
        ("arbitrary",), (dh, h, g, a, b, wg, wu, wd), comm=comm)


def tn_matmul(x, y, name, out="f32", comm=None):
    n, k = x.shape
    m = y.shape[1]
    tm = _tile(n, TN_ROWS)
    kc, mc = k, (512 if m % 512 == 0 else m)
    while tm * kc * x.dtype.itemsize > TN_X_BYTES and kc % (2 * LANES) == 0:
        kc //= 2
    while tm * mc * y.dtype.itemsize > TN_Y_BYTES and mc % (2 * LANES) == 0:
        mc //= 2
    steps = n // tm

    def body(x_ref, y_ref, o_ref, *acc):
        i = pl.program_id(2)
        part = _tn(x_ref[...].astype(BF16), y_ref[...].astype(BF16))
        if steps == 1:
            o_ref[...] = part.astype(o_ref.dtype)
        elif out == "f32":
            _accumulate(o_ref, part, i == 0)
        else:
            _accumulate(acc[0], part, i == 0)

            @pl.when(i == steps - 1)
            def _():
                o_ref[...] = acc[0][...].astype(BF16)

    out_shape = jax.ShapeDtypeStruct((k, m), F32 if out == "f32" else BF16)
    (res,), landed = _call(
        body, name, (k // kc, m // mc, steps),
        [pl.BlockSpec((tm, kc), lambda a, b, i: (i, a)), pl.BlockSpec((tm, mc), lambda a, b, i: (i, b))],
        [pl.BlockSpec((kc, mc), lambda a, b, i: (a, b))], [out_shape], ("parallel", "parallel", "arbitrary"), (x, y),
        scratch=[pltpu.VMEM((kc, mc), F32)] if (out == "bf16" and steps > 1) else [], comm=comm)
    return (res, landed) if comm is not None else res


def inproj_bwd(dzm, du, dgate, dh2, h, g, wm, wl):
    n, d = h.shape
    tm = _tile(n, 352)

    def body(dzm_ref, du_ref, dgt_ref, dh2_ref, h_ref, g_ref, wm_ref, wl_ref, dh_ref, dg_ref):
        x = h_ref[...]
        dun = (_nn(dzm_ref[...].astype(BF16), wm_ref[...])
               + _nn(du_ref[...].astype(BF16), wl_ref[:LRU_WIDTH, :])
               + _nn(dgt_ref[...].astype(BF16), wl_ref[LRU_WIDTH:, :]))
        dx, dg = _rms_bwd(x, _rms_r(x), g_ref[...], dun)
        dh_ref[...] = dh2_ref[...] + dx
        _accumulate(dg_ref, dg, pl.program_id(0) == 0)

    return pl.pallas_call(
        body, name="inproj_bwd", grid=(n // tm,),
        in_specs=[_row(tm, MLA_IN), _row(tm, LRU_WIDTH), _row(tm, LRU_WIDTH), _row(tm, d), _row(tm, d),
                  _fixed((1, d)), VMEM_WHOLE, VMEM_WHOLE],
        out_specs=[_row(tm, d), _fixed((1, d))],
        out_shape=[jax.ShapeDtypeStruct((n, d), F32), jax.ShapeDtypeStruct((1, d), F32)],
        compiler_params=_params(("arbitrary",)),
    )(dzm, du, dgate, dh2, h, g, wm, wl)


def _rope_tables(lp):
    pos = jnp.arange(lp, dtype=F32) - float(PAD)
    half = D_ROPE // 2
    inv_freq = ROPE_THETA ** (-jnp.arange(0, half, dtype=F32) / half)
    ang = pos[:, None] * inv_freq[None, :]
    cos, sin = jnp.cos(ang), jnp.sin(ang)
    one = jnp.ones((lp, D_NOPE), F32)
    z_nope = jnp.zeros((lp, D_NOPE), F32)
    z_half = jnp.zeros((lp, half), F32)
    z_tail = jnp.zeros((lp, HEAD_SLAB - D_QK), F32)
    cosr = jnp.concatenate([one, cos, cos, z_tail], axis=1)
    sin_up = jnp.concatenate([z_nope, z_half, sin, z_tail], axis=1)
    sin_dn = jnp.concatenate([z_nope, -sin, z_half, z_tail], axis=1)
    return cosr, sin_up, sin_dn


def _rope(x, cosr, sin_up, sin_dn):
    half = D_ROPE // 2
    return x * cosr + pltpu.roll(x, half, axis=1) * sin_up + pltpu.roll(x, HEAD_SLAB - half, axis=1) * sin_dn


def _rope_bwd(dy, cosr, sin_up, sin_dn):
    half = D_ROPE // 2
    return (dy * cosr + pltpu.roll(dy * sin_up, HEAD_SLAB - half, axis=1)
            + pltpu.roll(dy * sin_dn, half, axis=1))


def _k_rope_slab(zm_tile):
    tm = zm_tile.shape[0]
    krp = zm_tile[:, Q_RANK + KV_RANK:MLA_IN]
    return jnp.concatenate([jnp.zeros((tm, D_NOPE), F32), krp], axis=1)


def mla_prep_fwd(zm, gql, gkvl, wuq, wuk, wuv, gqh, gkh, tables, lp):
    n = zm.shape[0]
    tm = _tile(lp, 352)
    per_seq = lp // tm
    width = MLA_HEADS * HEAD_SLAB
    scale = 1.0 / math.sqrt(D_QK)

    def body(zm_ref, gql_ref, gkvl_ref, wuq_ref, wuk_ref, wuv_ref, gqh_ref, gkh_ref,
             cos_ref, up_ref, dn_ref, q_ref, k_ref, v_ref, qn_ref, cn_ref):
        z = zm_ref[...]
        cq = z[:, :Q_RANK]
        ckv = z[:, Q_RANK:Q_RANK + KV_RANK]
        qn = (cq * _rms_r(cq) * gql_ref[...]).astype(BF16)
        cn = (ckv * _rms_r(ckv) * gkvl_ref[...]).astype(BF16)
        qn_ref[...] = qn
        cn_ref[...] = cn
        q_raw = _nt(qn, wuq_ref[...])
        k_raw = _nt(cn, wuk_ref[...])
        v_ref[...] = _nt(cn, wuv_ref[...]).astype(BF16)
        kr_slab = _k_rope_slab(z)
        cosr, sin_up, sin_dn = cos_ref[...], up_ref[...], dn_ref[...]
        for hd in range(MLA_HEADS):
            cols = slice(hd * HEAD_SLAB, (hd + 1) * HEAD_SLAB)
            xq = q_raw[:, cols]
            yq = _rope(xq * _rms_r(xq, D_QK) * gqh_ref[...], cosr, sin_up, sin_dn)
            q_ref[:, cols] = (yq * scale).astype(BF16)
            xk = k_raw[:, cols] + kr_slab
            yk = _rope(xk * _rms_r(xk, D_QK) * gkh_ref[...], cosr, sin_up, sin_dn)
            k_ref[:, cols] = yk.astype(BF16)

    tab = pl.BlockSpec((tm, HEAD_SLAB), lambda i: (i % per_seq, 0))
    return pl.pallas_call(
        body, name="mla_prep_fwd", grid=(n // tm,),
        in_specs=[_row(tm, MLA_IN), _fixed((1, Q_RANK)), _fixed((1, KV_RANK)), VMEM_WHOLE, VMEM_WHOLE, VMEM_WHOLE,
                  _fixed((1, HEAD_SLAB)), _fixed((1, HEAD_SLAB)), tab, tab, tab],
        out_specs=[_row(tm, width), _row(tm, width), _row(tm, MLA_HEADS * D_V), _row(tm, Q_RANK), _row(tm, KV_RANK)],
        out_shape=[jax.ShapeDtypeStruct((n, width), BF16), jax.ShapeDtypeStruct((n, width), BF16),
                   jax.ShapeDtypeStruct((n, MLA_HEADS * D_V), BF16), jax.ShapeDtypeStruct((n, Q_RANK), BF16),
                   jax.ShapeDtypeStruct((n, KV_RANK), BF16)],
        compiler_params=_params(("parallel",)),
    )(zm, gql, gkvl, wuq, wuk, wuv, gqh, gkh, *tables)


def mla_prep_bwd(dq, dk, dv, zm, qn, cn, gql, gkvl, wuq, wuk, wuv, gqh, gkh, tables, lp, comm=None):
    n = zm.shape[0]
    tm = _tile(lp, 704)
    per_seq = lp // tm
    width = MLA_HEADS * HEAD_SLAB
    scale = 1.0 / math.sqrt(D_QK)

    def body(dq_ref, dk_ref, dv_ref, zm_ref, qn_ref, cn_ref, gql_ref, gkvl_ref, wuq_ref, wuk_ref, wuv_ref,
             gqh_ref, gkh_ref, cos_ref, up_ref, dn_ref,
             dzm_ref, dqr_ref, dkr_ref, dgql_ref, dgkvl_ref, dgqh_ref, dgkh_ref):
        z = zm_ref[...]
        cq = z[:, :Q_RANK]
        ckv = z[:, Q_RANK:Q_RANK + KV_RANK]
        q_raw = _nt(qn_ref[...], wuq_ref[...])
        k_raw = _nt(cn_ref[...], wuk_ref[...])
        kr_slab = _k_rope_slab(z)
        cosr, sin_up, sin_dn = cos_ref[...], up_ref[...], dn_ref[...]
        dgq = jnp.zeros((1, HEAD_SLAB), F32)
        dgk = jnp.zeros((1, HEAD_SLAB), F32)
        dkrp = jnp.zeros((tm, HEAD_SLAB - D_NOPE), F32)
        for hd in range(MLA_HEADS):
            cols = slice(hd * HEAD_SLAB, (hd + 1) * HEAD_SLAB)
            xq = q_raw[:, cols]
            dxn = _rope_bwd(dq_ref[:, cols] * scale, cosr, sin_up, sin_dn)
            dxq, dg = _rms_bwd(xq, _rms_r(xq, D_QK), gqh_ref[...], dxn, D_QK)
            dgq = dgq + dg
            dqr_ref[:, cols] = dxq.astype(BF16)
            xk = k_raw[:, cols] + kr_slab
            dxn = _rope_bwd(dk_ref[:, cols], cosr, sin_up, sin_dn)
            dxk, dg = _rms_bwd(xk, _rms_r(xk, D_QK), gkh_ref[...], dxn, D_QK)
            dgk = dgk + dg
            dkr_ref[:, cols] = dxk.astype(BF16)
            dkrp = dkrp + dxk[:, D_NOPE:]
        dqn = _nn(dqr_ref[...], wuq_ref[...])
        dcn = _nn(dkr_ref[...], wuk_ref[...]) + _nn(dv_ref[...].astype(BF16), wuv_ref[...])
        dcq, dg1 = _rms_bwd(cq, _rms_r(cq), gql_ref[...], dqn)
        dckv, dg2 = _rms_bwd(ckv, _rms_r(ckv), gkvl_ref[...], dcn)
        dzm_ref[:, :Q_RANK] = dcq
        dzm_ref[:, Q_RANK:Q_RANK + KV_RANK] = dckv
        dzm_ref[:, Q_RANK + KV_RANK:] = dkrp
        first = pl.program_id(0) == 0
        _accumulate(dgql_ref, dg1, first)
        _accumulate(dgkvl_ref, dg2, first)
        _accumulate(dgqh_ref, dgq, first)
        _accumulate(dgkh_ref, dgk, first)

    tab = pl.BlockSpec((tm, HEAD_SLAB), lambda i: (i % per_seq, 0))
    return _call(
        body, "mla_prep_bwd", (n // tm,),
        [_row(tm, width), _row(tm, width), _row(tm, MLA_HEADS * D_V), _row(tm, MLA_IN),
         _row(tm, Q_RANK), _row(tm, KV_RANK), _fixed((1, Q_RANK)), _fixed((1, KV_RANK)),
         VMEM_WHOLE, VMEM_WHOLE, VMEM_WHOLE, _fixed((1, HEAD_SLAB)), _fixed((1, HEAD_SLAB)), tab, tab, tab],
        [_row(tm, MLA_IN), _row(tm, width), _row(tm, width), _fixed((1, Q_RANK)), _fixed((1, KV_RANK)),
         _fixed((1, HEAD_SLAB)), _fixed((1, HEAD_SLAB))],
        [jax.ShapeDtypeStruct((n, MLA_IN), F32), jax.ShapeDtypeStruct((n, width), BF16),
         jax.ShapeDtypeStruct((n, width), BF16), jax.ShapeDtypeStruct((1, Q_RANK), F32),
         jax.ShapeDtypeStruct((1, KV_RANK), F32), jax.ShapeDtypeStruct((1, HEAD_SLAB), F32),
         jax.ShapeDtypeStruct((1, HEAD_SLAB), F32)],
        ("arbitrary",), (dq, dk, dv, zm, qn, cn, gql, gkvl, wuq, wuk, wuv, gqh, gkh, *tables), comm=comm)


def _attn_tile(lp):
    return _tile(lp, 704, CHUNK)


def _chunk_mask(i, j, t):
    qpos = i * t + lax.broadcasted_iota(jnp.int32, (t, t), 0)
    kpos = j * t + lax.broadcasted_iota(jnp.int32, (t, t), 1)
    same_or_earlier = jnp.right_shift(kpos, CHUNK_SHIFT) <= jnp.right_shift(qpos, CHUNK_SHIFT)
    return jnp.logical_and(same_or_earlier, kpos >= PAD)


def _masked_scores(s, i, j, t, diagonal):
    if diagonal:
        return jnp.where(_chunk_mask(i, j, t), s, NEG_INF)
    kpos = j * t + lax.broadcasted_iota(jnp.int32, (1, t), 1)
    return s + jnp.where(kpos < PAD, NEG_INF, 0.0)


def attn_fwd(q, k, v, nb, lp, comm=None):
    n = q.shape[0]
    t = _attn_tile(lp)
    nq = lp // t

    hp = ATTN_HEADS_PER_STEP

    def body(q_ref, k_ref, v_ref, o_ref, lse_ref):
        i = pl.program_id(2)
        qs = [q_ref[:, hh * HEAD_SLAB:(hh + 1) * HEAD_SLAB] for hh in range(hp)]

        def kv_step(j, carry, diagonal=False):
            off = pl.multiple_of(j * t, t)
            out = []
            for hh in range(hp):
                m, l, acc = carry[hh]
                kv = k_ref[pl.ds(off, t), hh * HEAD_SLAB:(hh + 1) * HEAD_SLAB]
                s = _masked_scores(_nt(qs[hh], kv), i, j, t, diagonal)
                m_new = jnp.maximum(m, jnp.max(s, axis=-1, keepdims=True))
                p = jnp.exp(s - m_new)
                alpha = jnp.exp(m - m_new)
                l = alpha * l + jnp.sum(p, axis=-1, keepdims=True)
                acc = alpha * acc + _nn(p.astype(BF16), v_ref[pl.ds(off, t), hh * D_V:(hh + 1) * D_V])
                out.append((m_new, l, acc))
            return tuple(out)

        init = tuple((jnp.full((t, 1), NEG_INF, F32), jnp.zeros((t, 1), F32), jnp.zeros((t, D_V), F32))
                     for _ in range(hp))
        done = kv_step(i, lax.fori_loop(0, i, kv_step, init), diagonal=True)
        for hh, (m, l, acc) in enumerate(done):
            o_ref[:, hh * D_V:(hh + 1) * D_V] = acc * (1.0 / l)
            lse_ref[hh] = jnp.broadcast_to(m + jnp.log(l), (t, LANES))

    return _call(
        body, "attn_fwd", (nb, MLA_HEADS // hp, nq),
        [pl.BlockSpec((t, hp * HEAD_SLAB), lambda b, h, i: (b * nq + i, h)),
         pl.BlockSpec((lp, hp * HEAD_SLAB), lambda b, h, i: (b, h)),
         pl.BlockSpec((lp, hp * D_V), lambda b, h, i: (b, h))],
        [pl.BlockSpec((t, hp * D_V), lambda b, h, i: (b * nq + i, h)),
         pl.BlockSpec((hp, t, LANES), lambda b, h, i: (h, b * nq + i, 0))],
        [jax.ShapeDtypeStruct((n, MLA_HEADS * D_V), F32), jax.ShapeDtypeStruct((MLA_HEADS, n, LANES), F32)],
        ("parallel", "parallel", "parallel"), (q, k, v), comm=comm)


def attn_bwd(q, k, v, o, do, lse, nb, lp, comm=None):
    n = q.shape[0]
    t = _attn_tile(lp)
    nq = lp // t

    def body(q_ref, k_ref, v_ref, o_ref, do_ref, lse_ref, dq_ref, dk_ref, dv_ref):
        dk_ref[...] = jnp.zeros_like(dk_ref)
        dv_ref[...] = jnp.zeros_like(dv_ref)

        def q_step(i, _):
            qoff = pl.multiple_of(i * t, t)
            qv = q_ref[pl.ds(qoff, t), :]
            dov = do_ref[pl.ds(qoff, t), :]
            delta = jnp.sum(o_ref[pl.ds(qoff, t), :] * dov, axis=-1, keepdims=True)
            lse_q = jnp.max(lse_ref[0, pl.ds(qoff, t), :], axis=-1, keepdims=True)
            do16 = dov.astype(BF16)

            def kv_step(j, dq_acc, diagonal=False):
                koff = pl.multiple_of(j * t, t)
                kv = k_ref[pl.ds(koff, t), :]
                s = _masked_scores(_nt(qv, kv), i, j, t, diagonal)
                p = jnp.exp(s - lse_q)
                dp = _nt(do16, v_ref[pl.ds(koff, t), :])
                ds16 = (p * (dp - delta)).astype(BF16)
                dv_ref[pl.ds(koff, t), :] += _tn(p.astype(BF16), do16)
                dk_ref[pl.ds(koff, t), :] += _tn(ds16, qv)
                return dq_acc + _nn(ds16, kv)

            earlier = lax.fori_loop(0, i, kv_step, jnp.zeros((t, HEAD_SLAB), F32))
            dq_ref[pl.ds(qoff, t), :] = kv_step(i, earlier, diagonal=True)
            return 0

        lax.fori_loop(0, nq, q_step, 0)

    wide = pl.BlockSpec((lp, HEAD_SLAB), lambda b, h: (b, h))
    thin = pl.BlockSpec((lp, D_V), lambda b, h: (b, h))
    width = MLA_HEADS * HEAD_SLAB
    return _call(
        body, "attn_bwd", (nb, MLA_HEADS),
        [wide, wide, thin, thin, thin, pl.BlockSpec((1, lp, LANES), lambda b, h: (h, b, 0))],
        [wide, wide, thin],
        [jax.ShapeDtypeStruct((n, width), F32), jax.ShapeDtypeStruct((n, width), F32),
         jax.ShapeDtypeStruct((n, MLA_HEADS * D_V), F32)],
        ("parallel", "parallel"), (q, k, v, o, do, lse), comm=comm)


def _seq_rows(nb, lp, width):
    rows = lax.broadcasted_iota(jnp.int32, (lp, width), 0)
    return jnp.concatenate([rows] * nb, axis=0) if nb > 1 else rows


def _lru_gates(u, w_ref, cb, wa, wx, ba, bx, lam):
    xc = (cb + w_ref[pl.ds(3, 1), :] * u + w_ref[pl.ds(2, 1), :] * pltpu.roll(u, 1, axis=0)
          + w_ref[pl.ds(1, 1), :] * pltpu.roll(u, 2, axis=0) + w_ref[pl.ds(0, 1), :] * pltpu.roll(u, 3, axis=0))
    xc16 = xc.astype(BF16)
    ra = _sig_tanh(_nn(xc16, wa) + ba)
    ia = _sig_tanh(_nn(xc16, wx) + bx)
    sp = _softplus(-lam)
    log_a = -C_RGLRU * ra * sp
    a = jnp.exp(log_a)
    x2 = 2.0 * log_a
    mult = jnp.sqrt(jnp.where(x2 > -1e-2, -x2 * (1.0 + x2 * (0.5 + x2 * (1.0 / 6.0))), 1.0 - a * a))
    return xc, xc16, ra, ia, sp, a, mult


def _scan_block_rows(width):
    return lax.broadcasted_iota(jnp.int32, (8, width), 0)


def lru_fwd(zl, conv_w, conv_b, wa, wx, ba, bx, lam, nb, lp, comm=None):
    n = zl.shape[0]
    w = LRU_TILE
    nt = LRU_WIDTH // w
    nblk = lp // 8

    def body(u_ref, gt_ref, cw_ref, cb_ref, wa_ref, wx_ref, ba_ref, bx_ref, lam_ref, y_ref, h_ref, a_s, b_s):
        u = u_ref[...]
        xc, _, _, ia, _, a, mult = _lru_gates(u, cw_ref, cb_ref[...], wa_ref[...], wx_ref[...],
                                              ba_ref[...], bx_ref[...], lam_ref[...])
        row = _seq_rows(nb, lp, w)
        mult = jnp.where(row == PAD, 1.0, mult)
        a_s[...] = a
        b_s[...] = jnp.where(row < PAD, 0.0, mult * (ia * xc))
        r8 = _scan_block_rows(w)

        def blk(i, carry):
            out = []
            for s_id in range(nb):
                off = pl.multiple_of(s_id * lp + i * 8, 8)
                av = a_s[pl.ds(off, 8), :]
                bv = b_s[pl.ds(off, 8), :]
                for sh in (1, 2, 4):
                    keep = r8 >= sh
                    bv = jnp.where(keep, av * pltpu.roll(bv, sh, axis=0) + bv, bv)
                    av = jnp.where(keep, av * pltpu.roll(av, sh, axis=0), av)
                hv = bv + av * carry[s_id]
                h_ref[pl.ds(off, 8), :] = hv
                out.append(jnp.sum(jnp.where(r8 == 7, hv, 0.0), axis=0, keepdims=True))
            return tuple(out)

        lax.fori_loop(0, nblk, blk, tuple(jnp.zeros((1, w), F32) for _ in range(nb)))
        gelu, _ = _gelu_and_grad(gt_ref[...])
        y_ref[...] = h_ref[...] * gelu

    col = lambda c: (0, c)
    return _call(
        body, "lru_fwd", (nt,),
        [pl.BlockSpec((n, w), col), pl.BlockSpec((n, w), lambda c: (0, nt + c)),
         pl.BlockSpec((CONV_W, w), col), pl.BlockSpec((1, w), col),
         pl.BlockSpec((w, w), lambda c: (c, c)), pl.BlockSpec((w, w), lambda c: (c, c)),
         pl.BlockSpec((1, w), col), pl.BlockSpec((1, w), col), pl.BlockSpec((1, w), col)],
        [pl.BlockSpec((n, w), col), pl.BlockSpec((n, w), col)],
        [jax.ShapeDtypeStruct((n, LRU_WIDTH), F32), jax.ShapeDtypeStruct((n, LRU_WIDTH), F32)],
        ("parallel",), (zl, zl, conv_w, conv_b, wa, wx, ba, bx, lam),
        scratch=[pltpu.VMEM((n, w), F32), pltpu.VMEM((n, w), F32)], comm=comm)


def lru_bwd(zl, hs, dy, conv_w, conv_b, wa, wx, ba, bx, lam, nb, lp, comm=None):
    n = zl.shape[0]
    w = LRU_TILE
    nt = LRU_WIDTH // w
    nblk = lp // 8

    def body(u_ref, gt_ref, h_ref, dy_ref, cw_ref, cb_ref, wa_ref, wx_ref, ba_ref, bx_ref, lam_ref,
             du_ref, dgt_ref, dcw_ref, dcb_ref, dba_ref, dbx_ref, dlam_ref, dwa_ref, dwx_ref,
             c_s, d_s, g_s, dwa_s, dwx_s):
        u = u_ref[...]
        lam = lam_ref[...]
        xc, xc16, ra, ia, sp, a, mult = _lru_gates(u, cw_ref, cb_ref[...], wa_ref[...], wx_ref[...],
                                                   ba_ref[...], bx_ref[...], lam)
        row = lax.broadcasted_iota(jnp.int32, (lp, w), 0)
        hv = h_ref[...]
        dyv = dy_ref[...]
        gelu, dgelu = _gelu_and_grad(gt_ref[...])
        dgt_ref[...] = jnp.where(row >= PAD, dyv * hv * dgelu, 0.0)
        c_s[...] = pltpu.roll(a, lp - 1, axis=0)
        d_s[...] = dyv * gelu
        r8 = _scan_block_rows(w)

        def blk(ii, carry):
            off = pl.multiple_of((nblk - 1 - ii) * 8, 8)
            cv = c_s[pl.ds(off, 8), :]
            dv = d_s[pl.ds(off, 8), :]
            for sh in (1, 2, 4):
                keep = r8 < 8 - sh
                dv = jnp.where(keep, cv * pltpu.roll(dv, 8 - sh, axis=0) + dv, dv)
                cv = jnp.where(keep, cv * pltpu.roll(cv, 8 - sh, axis=0), cv)
            gv = dv + cv * carry
            g_s[pl.ds(off, 8), :] = gv
            return jnp.sum(jnp.where(r8 == 0, gv, 0.0), axis=0, keepdims=True)

        lax.fori_loop(0, nblk, blk, jnp.zeros((1, w), F32))
        gv = g_s[...]
        first_row = row == PAD
        db = jnp.where(row >= PAD, gv, 0.0)
        da = jnp.where(row > PAD, gv * pltpu.roll(hv, 1, axis=0), 0.0)
        mult_eff = jnp.where(first_row, 1.0, mult)
        dmult = jnp.where(first_row, 0.0, db * (ia * xc))
        dia = db * mult_eff * xc
        dxc = db * mult_eff * ia
        dla = da * a - dmult * (a * a) / mult
        dra = dla * (-C_RGLRU * sp)
        dsp = jnp.sum(dla * (-C_RGLRU * ra), axis=0, keepdims=True)
        dpa = dra * ra * (1.0 - ra)
        dpx = dia * ia * (1.0 - ia)
        dpa16 = dpa.astype(BF16)
        dpx16 = dpx.astype(BF16)
        dxc = dxc + _nt(dpa16, wa_ref[...]) + _nt(dpx16, wx_ref[...])
        du = cw_ref[pl.ds(CONV_W - 1, 1), :] * dxc
        dcw = [jnp.sum(dxc * u, axis=0, keepdims=True)]
        for tap in range(1, CONV_W):
            dcw.insert(0, jnp.sum(dxc * pltpu.roll(u, tap, axis=0), axis=0, keepdims=True))
            du = du + cw_ref[pl.ds(CONV_W - 1 - tap, 1), :] * pltpu.roll(dxc, lp - tap, axis=0)
        du_ref[...] = jnp.where(row >= PAD, du, 0.0)
        first = pl.program_id(1) == 0
        _accumulate(dlam_ref, -_sig(-lam) * dsp, first)
        _accumulate(dba_ref, jnp.sum(dpa, axis=0, keepdims=True), first)
        _accumulate(dbx_ref, jnp.sum(dpx, axis=0, keepdims=True), first)
        _accumulate(dcb_ref, jnp.sum(dxc, axis=0, keepdims=True), first)
        _accumulate(dcw_ref, jnp.concatenate(dcw, axis=0), first)
        _accumulate(dwa_s, _tn(xc16, dpa16), first)
        _accumulate(dwx_s, _tn(xc16, dpx16), first)

        @pl.when(pl.program_id(1) == nb - 1)
        def _():
            for j in range(w // LRU_BLOCK):
                blk_rows = slice(j * LRU_BLOCK, (j + 1) * LRU_BLOCK)
                dwa_ref[0, blk_rows, :] = dwa_s[blk_rows, blk_rows]
                dwx_ref[0, blk_rows, :] = dwx_s[blk_rows, blk_rows]

    col = lambda c, b: (0, c)
    vec = pl.BlockSpec((1, w), col)
    mat = pl.BlockSpec((w, w), lambda c, b: (c, c))
    big = pl.BlockSpec((lp, w), lambda c, b: (b, c))
    dmat = pl.BlockSpec((1, w, LRU_BLOCK), lambda c, b: (c, 0, 0))
    return _call(
        body, "lru_bwd", (nt, nb),
        [big, pl.BlockSpec((lp, w), lambda c, b: (b, nt + c)), big, big,
         pl.BlockSpec((CONV_W, w), col), vec, mat, mat, vec, vec, vec],
        [big, big, pl.BlockSpec((CONV_W, w), col), vec, vec, vec, vec, dmat, dmat],
        [jax.ShapeDtypeStruct((n, LRU_WIDTH), F32), jax.ShapeDtypeStruct((n, LRU_WIDTH), F32),
         jax.ShapeDtypeStruct((CONV_W, LRU_WIDTH), F32), jax.ShapeDtypeStruct((1, LRU_WIDTH), F32),
         jax.ShapeDtypeStruct((1, LRU_WIDTH), F32), jax.ShapeDtypeStruct((1, LRU_WIDTH), F32),
         jax.ShapeDtypeStruct((1, LRU_WIDTH), F32), jax.ShapeDtypeStruct((nt, w, LRU_BLOCK), F32),
         jax.ShapeDtypeStruct((nt, w, LRU_BLOCK), F32)],
        ("parallel", "arbitrary"), (zl, zl, hs, dy, conv_w, conv_b, wa, wx, ba, bx, lam),
        scratch=[pltpu.VMEM((lp, w), F32), pltpu.VMEM((lp, w), F32), pltpu.VMEM((lp, w), F32),
                 pltpu.VMEM((w, w), F32), pltpu.VMEM((w, w), F32)], comm=comm)


def outproj_fwd(h, ya, yl, gao, glo, wout):
    n, d = h.shape
    half = ya.shape[1]
    tm = _tile(n, 704)

    def body(h_ref, ya_ref, yl_ref, gao_ref, glo_ref, w_ref, ho_ref, yn_ref):
        xa = ya_ref[...]
        xl = yl_ref[...]
        na = (xa * _rms_r(xa) * gao_ref[...]).astype(BF16)
        nl = (xl * _rms_r(xl) * glo_ref[...]).astype(BF16)
        yn_ref[:, :half] = na
        yn_ref[:, half:] = nl
        ho_ref[...] = h_ref[...] + _nn(na, w_ref[:half, :]) + _nn(nl, w_ref[half:, :])

    return pl.pallas_call(
        body, name="outproj_fwd", grid=(n // tm,),
        in_specs=[_row(tm, d), _row(tm, half), _row(tm, half), _fixed((1, half)), _fixed((1, half)), VMEM_WHOLE],
        out_specs=[_row(tm, d), _row(tm, 2 * half)],
        out_shape=[jax.ShapeDtypeStruct((n, d), F32), jax.ShapeDtypeStruct((n, 2 * half), BF16)],
        compiler_params=_params(("parallel",)),
    )(h, ya, yl, gao, glo, wout)


def outproj_bwd(dh, ya, yl, gao, glo, wout):
    n, d = dh.shape
    half = ya.shape[1]
    tm = _tile(n, 704)

    def body(dh_ref, ya_ref, yl_ref, gao_ref, glo_ref, w_ref, dya_ref, dyl_ref, dgao_ref, dglo_ref):
        d16 = dh_ref[...].astype(BF16)
        xa = ya_ref[...]
        xl = yl_ref[...]
        dxa, dga = _rms_bwd(xa, _rms_r(xa), gao_ref[...], _nt(d16, w_ref[:half, :]))
        dxl, dgl = _rms_bwd(xl, _rms_r(xl), glo_ref[...], _nt(d16, w_ref[half:, :]))
        dya_ref[...] = dxa
        dyl_ref[...] = dxl
        first = pl.program_id(0) == 0
        _accumulate(dgao_ref, dga, first)
        _accumulate(dglo_ref, dgl, first)

    return pl.pallas_call(
        body, name="outproj_bwd", grid=(n // tm,),
        in_specs=[_row(tm, d), _row(tm, half), _row(tm, half), _fixed((1, half)), _fixed((1, half)), VMEM_WHOLE],
        out_specs=[_row(tm, half), _row(tm, half), _fixed((1, half)), _fixed((1, half))],
        out_shape=[jax.ShapeDtypeStruct((n, half), F32), jax.ShapeDtypeStruct((n, half), F32),
                   jax.ShapeDtypeStruct((1, half), F32), jax.ShapeDtypeStruct((1, half), F32)],
        compiler_params=_params(("arbitrary",)),
    )(dh, ya, yl, gao, glo, wout)


def _loss_and_grad(x, gv, tgt, first_row):
    tm, d = x.shape
    r = _rms_r(x)
    row = first_row + lax.broadcasted_iota(jnp.int32, (tm, d), 0)
    diff = jnp.where(row >= FIRST_FRAME, x * r * gv - tgt, 0.0)
    part = 0.5 * jnp.sum(jnp.sum(diff * diff, axis=-1, keepdims=True) * (1.0 / d), axis=0, keepdims=True)
    dx, dg = _rms_bwd(x, r, gv, diff * (1.0 / d))
    return dx, part, dg


def assemble_cols(g, name):
    _, k, ns = g.shape

    def body(g_ref, o_ref):
        for j in range(N_DEV):
            o_ref[:, j * ns:(j + 1) * ns] = g_ref[j]

    return pl.pallas_call(body, name=name, out_shape=jax.ShapeDtypeStruct((k, N_DEV * ns), g.dtype),
                          compiler_params=_params(None))(g)


def split_cols(x, name):
    k, cols = x.shape
    ns = cols // N_DEV

    def body(x_ref, o_ref):
        for j in range(N_DEV):
            o_ref[j] = x_ref[:, j * ns:(j + 1) * ns]

    return pl.pallas_call(body, name=name, out_shape=jax.ShapeDtypeStruct((N_DEV, k, ns), x.dtype),
                          compiler_params=_params(None))(x)


def _slab_rows(w, per_head):
    k = w.shape[1]
    w = w.reshape(MLA_HEADS, per_head, k)
    return jnp.pad(w, ((0, 0), (0, HEAD_SLAB - per_head), (0, 0))).reshape(MLA_HEADS * HEAD_SLAB, k)


def _unslab_rows(w, per_head):
    k = w.shape[1]
    return w.reshape(MLA_HEADS, HEAD_SLAB, k)[:, :per_head].reshape(MLA_HEADS * per_head, k)


def meta_grad(dh0, nb, lp):
    d = dh0.shape[1]
    ns = d // N_DEV
    per_seq = lp // N_META

    def body(x_ref, o_ref):
        x = x_ref[...]
        for j in range(N_DEV):
            _accumulate(o_ref.at[j], x[:, j * ns:(j + 1) * ns], pl.program_id(0) == 0)

    return pl.pallas_call(
        body, name="meta_grad", grid=(nb,),
        in_specs=[pl.BlockSpec((N_META, d), lambda b: (b * per_seq + PAD // N_META, 0))],
        out_specs=pl.BlockSpec((N_DEV, N_META, ns), lambda b: (0, 0, 0)),
        out_shape=jax.ShapeDtypeStruct((N_DEV, N_META, ns), F32),
        compiler_params=_params(("arbitrary",)))(dh0)


VECTORS = [("ffn1_norm", 1024), ("mix_norm", 1024), ("q_latent_norm", 384), ("kv_latent_norm", 256),
           ("q_head_norm", 192), ("k_head_norm", 192), ("conv_b", 512), ("gate_a_b", 512), ("gate_x_b", 512),
           ("lru_lambda", 512), ("attn_out_norm", 512), ("lru_out_norm", 512), ("ffn2_norm", 1024),
           ("final_norm", 1024)]
VEC_ROWS = 16
LOSS_ROW = len(VECTORS)
GATES = ["gate_a_w", "gate_x_w"]


def pack_vectors(grads, loss):
    def body(*refs):
        o_ref = refs[-1]
        o_ref[...] = jnp.zeros_like(o_ref)
        for t, (ref, (_, cnt)) in enumerate(zip(refs[:-2], VECTORS)):
            o_ref[t:t + 1, :cnt] = ref[:, :cnt]
        o_ref[LOSS_ROW:LOSS_ROW + 1, :LANES] = refs[-2][...]

    return pl.pallas_call(body, name="pack_vectors", out_shape=jax.ShapeDtypeStruct((VEC_ROWS, D_MODEL), F32),
                          compiler_params=_params(None))(*[grads[name] for name, _ in VECTORS], loss)


def _adamw_update(w, g, m, v):
    c1 = 1.0 / (1.0 - ADAM_B1 ** ADAM_STEP)
    c2 = 1.0 / (1.0 - ADAM_B2 ** ADAM_STEP)
    mn = ADAM_B1 * m + (1.0 - ADAM_B1) * g
    vn = ADAM_B2 * v + (1.0 - ADAM_B2) * (g * g)
    delta = -ADAM_LR * ((mn * c1) / (jnp.sqrt(vn * c2) + ADAM_EPS) + ADAM_WD * w)
    return delta, mn, vn


def _sum_slots(ref, index=()):
    acc = ref[(0,) + index].astype(F32)
    for s in range(1, N_DEV):
        acc = acc + ref[(s,) + index].astype(F32)
    return acc


def adamw_sharded(r, w, m, v, name):
    rows, cols = w.shape
    tr = _tile(rows, 256, 16) if rows % 16 == 0 else rows

    def body(r_ref, w_ref, m_ref, v_ref, g_ref, d_ref, mo_ref, vo_ref):
        g = _sum_slots(r_ref)
        g_ref[...] = g
        d_ref[...], mo_ref[...], vo_ref[...] = _adamw_update(w_ref[...], g, m_ref[...], v_ref[...])

    spec = pl.BlockSpec((tr, cols), lambda i: (i, 0))
    shape = jax.ShapeDtypeStruct((rows, cols), F32)
    return pl.pallas_call(
        body, name=name, grid=(rows // tr,),
        in_specs=[pl.BlockSpec((N_DEV, tr, cols), lambda i: (0, i, 0))] + [spec] * 3,
        out_specs=[spec] * 4, out_shape=[shape] * 4,
        compiler_params=_params(("parallel",)),
    )(r, w, m, v)


def adamw_small(r_vec, r_gates, w, m, v):
    nt = len(VECTORS) + len(GATES)

    def body(*refs):
        rv_ref = refs[0]
        rg_refs = refs[1:1 + len(GATES)]
        base = 1 + len(GATES)
        w_refs, m_refs, v_refs = (refs[base + i * nt:base + (i + 1) * nt] for i in range(3))
        outs = refs[base + 3 * nt:]
        g_o, d_o, m_o, v_o = (outs[i * nt:(i + 1) * nt] for i in range(4))
        outs[4 * nt][...] = _sum_slots(rv_ref, (slice(LOSS_ROW, LOSS_ROW + 1), slice(0, LANES)))
        for t in range(nt):
            if t < len(VECTORS):
                cnt = VECTORS[t][1]
                g = _sum_slots(rv_ref, (slice(t, t + 1), slice(0, cnt)))
            else:
                g = _sum_slots(rg_refs[t - len(VECTORS)])
            g_o[t][...] = g
            d_o[t][...], m_o[t][...], v_o[t][...] = _adamw_update(w_refs[t][...], g, m_refs[t][...], v_refs[t][...])

    shapes = [jax.ShapeDtypeStruct(a.shape, F32) for a in w]
    res = pl.pallas_call(body, name="adamw_small", out_shape=shapes * 4 + [jax.ShapeDtypeStruct((1, LANES), F32)],
                         compiler_params=_params(None))(r_vec, *r_gates, *w, *m, *v)
    return [res[i * nt:(i + 1) * nt] for i in range(4)], res[4 * nt]


def _block_diag(w):
    nb, n, _ = w.shape
    eye = jnp.eye(nb, dtype=w.dtype)
    return (eye[:, None, :, None] * w[:, :, None, :]).reshape(nb * n, nb * n)


def _two_d(a):
    if a.ndim == 3:
        return a.reshape(a.shape[1], a.shape[2])
    if a.ndim == 4:
        return a.reshape(a.shape[1] * a.shape[2], a.shape[3])
    return a


_WEIGHT_NAMES = ['meta_tokens', 'ffn1_norm', 'ffn1_w_gate', 'ffn1_w_up', 'ffn1_w_down', 'mix_norm', 'w_in',
                 'q_latent_norm', 'w_uq', 'kv_latent_norm', 'w_uk', 'w_uv', 'q_head_norm', 'k_head_norm', 'conv_w',
                 'conv_b', 'gate_a_w', 'gate_a_b', 'gate_x_w', 'gate_x_b', 'lru_lambda', 'attn_out_norm',
                 'lru_out_norm', 'w_out', 'ffn2_norm', 'ffn2_w_gate', 'ffn2_w_up', 'ffn2_w_down', 'final_norm']


COLUMN_SHARDED = ("ffn1_w_gate", "ffn1_w_up", "ffn2_w_gate", "ffn2_w_up", "w_in", "w_uq", "w_uk", "w_uv")


def train_step(x, tgt, w, m, v):
    nb, seq, d = x.shape
    lp = PAD + N_META + seq
    n = nb * lp

    def local(a, name):
        a = _two_d(a)
        return a.T if name in COLUMN_SHARDED else a

    sh = {name: local(w[name], name) for name in _WEIGHT_NAMES}
    m2 = {name: local(m[name], name) for name in _WEIGHT_NAMES}
    v2 = {name: local(v[name], name) for name in _WEIGHT_NAMES}

    def b16(name):
        return sh[name].astype(BF16)

    out = {}

    def update(name, landed):
        out[name] = adamw_sharded(landed, sh[name], m2[name], v2[name], "adamw_" + name)

    g_wg1, g_wu1, g_meta, g_conv = exchange(
        [b16("ffn1_w_gate"), b16("ffn1_w_up"), sh["meta_tokens"], sh["conv_w"]], ["gather"] * 4, "gather_ffn1")
    wg1, wu1 = g_wg1.reshape(D_FF, d), g_wu1.reshape(D_FF, d)
    meta = assemble_cols(g_meta, "assemble_meta")
    conv_w = assemble_cols(g_conv, "assemble_conv")

    front = jnp.concatenate([jnp.zeros((PAD, d), F32), meta], axis=0)
    h0 = jnp.concatenate([jnp.broadcast_to(front[None], (nb, FIRST_FRAME, d)), x], axis=1).reshape(n, d)
    tgt_p = jnp.concatenate([jnp.zeros((nb, FIRST_FRAME, d), F32), tgt], axis=1).reshape(n, d)
    tables = _rope_tables(lp)
    zero_tail = jnp.zeros((1, HEAD_SLAB - D_QK), F32)
    gqh = jnp.concatenate([sh["q_head_norm"], zero_tail], axis=1)
    gkh = jnp.concatenate([sh["k_head_norm"], zero_tail], axis=1)
    wa = _block_diag(w["gate_a_w"][0]).astype(BF16)
    wx = _block_diag(w["gate_x_w"][0]).astype(BF16)

    (u1, a1, b1, s1), (g_wd1, g_in) = ffn_up(h0, sh["ffn1_norm"], wg1, wu1, "ffn1_up",
                                             comm=([b16("ffn1_w_down"), b16("w_in")], ["gather"] * 2))
    wd1 = g_wd1.reshape(D_FF, d)
    mla_rows = MLA_IN - D_ROPE
    w_in = g_in.reshape(mla_rows + 2 * LRU_WIDTH, d)
    wm = jnp.concatenate([w_in[:mla_rows], jnp.zeros((D_ROPE, d), BF16)], axis=0)
    wl = w_in[mla_rows:]
    (h1, u2, zm, zl), (g_uq, g_uk, g_uv, g_out) = ffn_down_inproj(
        h0, s1, wd1, sh["mix_norm"], wm, wl, "ffn1_down_inproj",
        comm=([b16("w_uq"), b16("w_uk"), b16("w_uv"), b16("w_out")], ["gather"] * 4))
    wuq = _slab_rows(g_uq.reshape(MLA_HEADS * D_QK, Q_RANK), D_QK)
    wuk = _slab_rows(g_uk.reshape(MLA_HEADS * D_NOPE, KV_RANK), D_NOPE)
    wuv = g_uv.reshape(MLA_HEADS * D_V, KV_RANK)
    w_out = g_out.reshape(d, d)

    q, k, vv, qn, cn = mla_prep_fwd(zm, sh["q_latent_norm"], sh["kv_latent_norm"], wuq, wuk, wuv, gqh, gkh, tables, lp)
    (y_mla, lse), (g_wu2, g_wd2) = attn_fwd(
        q, k, vv, nb, lp, comm=([b16("ffn2_w_up"), b16("ffn2_w_down")], ["gather"] * 2))
    (y_lru, hs), (g_wg2,) = lru_fwd(zl, conv_w, sh["conv_b"], wa, wx, sh["gate_a_b"], sh["gate_x_b"], sh["lru_lambda"],
                                    nb, lp, comm=([b16("ffn2_w_gate")], ["gather"]))
    wg2, wu2, wd2 = (g.reshape(D_FF, d) for g in (g_wg2, g_wu2, g_wd2))
    h2, yn = outproj_fwd(h1, y_mla, y_lru, sh["attn_out_norm"], sh["lru_out_norm"], w_out)
    dh3, u3, a3, b3, loss, g_final = ffn_fwd_loss(h2, sh["ffn2_norm"], wg2, wu2, wd2, sh["final_norm"], tgt_p, lp,
                                                  "ffn2_fwd_loss")

    vec = {"final_norm": g_final}
    (dh2, da3, db3, sh3, vec["ffn2_norm"]), _ = ffn_bwd_act(dh3, h2, sh["ffn2_norm"], a3, b3, wg2, wu2, wd2, "ffn2_bwd")
    ff_shards = (N_DEV, D_FF // N_DEV, d)
    dwg2 = tn_matmul(da3, u3, "ffn2_dwg", "bf16").reshape(ff_shards)
    dwu2 = tn_matmul(db3, u3, "ffn2_dwu", "bf16").reshape(ff_shards)
    dwd2 = tn_matmul(sh3, dh3, "ffn2_dwd", "bf16").reshape(ff_shards)

    dy_mla, dy_lru, vec["attn_out_norm"], vec["lru_out_norm"] = outproj_bwd(
        dh2, y_mla, y_lru, sh["attn_out_norm"], sh["lru_out_norm"], w_out)
    dw_out = tn_matmul(yn, dh2, "dw_out", "bf16").reshape(N_DEV, d // N_DEV, d)
    (du, dgate, dconv, vec["conv_b"], vec["gate_a_b"], vec["gate_x_b"], vec["lru_lambda"], dga, dgx), (r_wg2,) = lru_bwd(
        zl, hs, dy_lru, conv_w, sh["conv_b"], wa, wx, sh["gate_a_b"], sh["gate_x_b"], sh["lru_lambda"], nb, lp,
        comm=([dwg2], ["scatter"]))
    update("ffn2_w_gate", r_wg2)

    (dq, dk, dv), (r_wu2,) = attn_bwd(q, k, vv, y_mla, dy_mla, lse, nb, lp, comm=([dwu2], ["scatter"]))
    update("ffn2_w_up", r_wu2)

    (dzm, dqr, dkr, vec["q_latent_norm"], vec["kv_latent_norm"], vec["q_head_norm"], vec["k_head_norm"]), (r_wd2,) = (
        mla_prep_bwd(dq, dk, dv, zm, qn, cn, sh["q_latent_norm"], sh["kv_latent_norm"], wuq, wuk, wuv, gqh, gkh,
                     tables, lp, comm=([dwd2], ["scatter"])))
    update("ffn2_w_down", r_wd2)
    dwuq = _unslab_rows(tn_matmul(dqr, qn, "dw_uq"), D_QK).reshape(N_DEV, -1, Q_RANK)
    dwuk = _unslab_rows(tn_matmul(dkr, cn, "dw_uk"), D_NOPE).reshape(N_DEV, -1, KV_RANK)
    dwuv = tn_matmul(dv, cn, "dw_uv").reshape(N_DEV, -1, KV_RANK)
    dh1, vec["mix_norm"] = inproj_bwd(dzm, du, dgate, dh2, h1, sh["mix_norm"], wm, wl)
    dw_in = jnp.concatenate([tn_matmul(dzm, u2, "dw_in_mla", "bf16")[:mla_rows], tn_matmul(du, u2, "dw_in_u", "bf16"),
                             tn_matmul(dgate, u2, "dw_in_gate", "bf16")], axis=0).reshape(N_DEV, -1, d)

    dwd1 = tn_matmul(s1, dh1, "ffn1_dwd", "bf16").reshape(ff_shards)
    (dh0, da1, db1, vec["ffn1_norm"]), landed = ffn_bwd_act(
        dh1, h0, sh["ffn1_norm"], a1, b1, wg1, wu1, wd1, "ffn1_bwd", emit_sh=False,
        comm=([dw_in, dwuq, dwuk, dwuv, dw_out, split_cols(dconv, "split_conv"), dwd1], ["scatter"] * 7))
    for name, r in zip(("w_in", "w_uq", "w_uk", "w_uv", "w_out", "conv_w", "ffn1_w_down"), landed):
        update(name, r)

    dwg1 = tn_matmul(da1, u1, "ffn1_dwg", "bf16").reshape(ff_shards)
    dwu1, (r_wg1,) = tn_matmul(db1, u1, "ffn1_dwu", "bf16", comm=([dwg1], ["scatter"]))
    dmeta = meta_grad(dh0, nb, lp)
    gates = [dga.reshape(LRU_WIDTH, LRU_BLOCK), dgx.reshape(LRU_WIDTH, LRU_BLOCK)]
    r_wu1, r_meta, r_vec, r_ga, r_gx = exchange(
        [dwu1.reshape(ff_shards), dmeta, pack_vectors(vec, loss)] + gates, ["scatter"] * 2 + ["gather"] * 3,
        "exchange_last")
    update("ffn1_w_gate", r_wg1)
    update("ffn1_w_up", r_wu1)
    update("meta_tokens", r_meta)

    small = [name for name, _ in VECTORS] + GATES
    res, total_loss = adamw_small(r_vec, [r_ga, r_gx], [sh[nm] for nm in small], [m2[nm] for nm in small],
                                  [v2[nm] for nm in small])
    for i, name in enumerate(small):
        out[name] = [res[j][i] for j in range(4)]

    grad_x = dh0.reshape(nb, lp, d)[:, FIRST_FRAME:]
    loss = total_loss[0, 0]

    def as_given(a, name):
        return (a.T if name in COLUMN_SHARDED else a).reshape(w[name].shape)

    cols = [[as_given(out[name][j], name) for name in _WEIGHT_NAMES] for j in range(4)]
    return (loss, grad_x, *cols[0], *cols[1], *cols[2], *cols[3])


def kernel(x, meta_tokens, ffn1_norm, ffn1_w_gate, ffn1_w_up, ffn1_w_down, mix_norm, w_in, q_latent_norm, w_uq, kv_latent_norm, w_uk, w_uv, q_head_norm, k_head_norm, conv_w, conv_b, gate_a_w, gate_a_b, gate_x_w, gate_x_b, lru_lambda, attn_out_norm, lru_out_norm, w_out, ffn2_norm, ffn2_w_gate, ffn2_w_up, ffn2_w_down, final_norm, loss_target, m_meta_tokens, m_ffn1_norm, m_ffn1_w_gate, m_ffn1_w_up, m_ffn1_w_down, m_mix_norm, m_w_in, m_q_latent_norm, m_w_uq, m_kv_latent_norm, m_w_uk, m_w_uv, m_q_head_norm, m_k_head_norm, m_conv_w, m_conv_b, m_gate_a_w, m_gate_a_b, m_gate_x_w, m_gate_x_b, m_lru_lambda, m_attn_out_norm, m_lru_out_norm, m_w_out, m_ffn2_norm, m_ffn2_w_gate, m_ffn2_w_up, m_ffn2_w_down, m_final_norm, v_meta_tokens, v_ffn1_norm, v_ffn1_w_gate, v_ffn1_w_up, v_ffn1_w_down, v_mix_norm, v_w_in, v_q_latent_norm, v_w_uq, v_kv_latent_norm, v_w_uk, v_w_uv, v_q_head_norm, v_k_head_norm, v_conv_w, v_conv_b, v_gate_a_w, v_gate_a_b, v_gate_x_w, v_gate_x_b, v_lru_lambda, v_attn_out_norm, v_lru_out_norm, v_w_out, v_ffn2_norm, v_ffn2_w_gate, v_ffn2_w_up, v_ffn2_w_down, v_final_norm):
    args = locals()
    w = {name: args[name] for name in _WEIGHT_NAMES}
    m = {name: args["m_" + name] for name in _WEIGHT_NAMES}
    v = {name: args["v_" + name] for name in _WEIGHT_NAMES}
    return train_step(x, loss_target, w, m, v)
```

```python
import math

import jax
import jax.numpy as jnp
from jax import lax
from jax.experimental import pallas as pl
from jax.experimental.pallas import tpu as pltpu

F32 = jnp.float32
BF16 = jnp.bfloat16

D_MODEL = 1024
CHUNK = 64
CHUNK_SHIFT = 6
N_META = 16
PAD = CHUNK - N_META
FIRST_FRAME = PAD + N_META
MLA_HEADS = 4
D_NOPE = 128
D_ROPE = 64
D_QK = D_NOPE + D_ROPE
D_V = 128
HEAD_SLAB = 256
KV_RANK = 256
Q_RANK = 384
ROPE_THETA = 10000.0
LRU_WIDTH = 512
LRU_BLOCKS = 8
LRU_BLOCK = 64
LRU_TILE = 128
CONV_W = 4
C_RGLRU = 8.0
D_FF = 2816
MLA_IN = 768
EPS = 1e-6
NEG_INF = -1e30
N_DEV = 8
LANES = 128
VMEM_LIMIT = 52 * 1024 * 1024
ATTN_HEADS_PER_STEP = 2
TN_ROWS = 4224
TN_X_BYTES = 12 * 1024 * 1024
TN_Y_BYTES = 9 * 1024 * 1024 // 2

ADAM_LR = 0.001
ADAM_B1 = 0.9
ADAM_B2 = 0.999
ADAM_EPS = 1e-08
ADAM_WD = 0.01
ADAM_STEP = 10

VMEM_WHOLE = pl.BlockSpec(memory_space=pltpu.VMEM)
HBM_WHOLE = pl.BlockSpec(memory_space=pl.ANY)


def _params(sems):
    if sems is None:
        return pltpu.CompilerParams(vmem_limit_bytes=VMEM_LIMIT)
    return pltpu.CompilerParams(dimension_semantics=sems, vmem_limit_bytes=VMEM_LIMIT)


def _tile(n, cap, mult=16):
    best = None
    for t in range(mult, min(n, cap) + 1, mult):
        if n % t == 0:
            best = t
    assert best is not None, (n, cap, mult)
    return best


def _row(tm, d):
    return pl.BlockSpec((tm, d), lambda i: (i, 0))


def _fixed(shape):
    return pl.BlockSpec(shape, lambda i: (0,) * len(shape))


def _mesh_position():
    return lax.axis_index("x"), lax.axis_index("y"), lax.axis_index("c")


def _flat_index(x, y, c):
    return 4 * x + 2 * y + c


def _peers(x, y, c):
    out = []
    for k in range(1, N_DEV):
        fx, fy, fc = (k >> 2) & 1, (k >> 1) & 1, k & 1
        out.append((1 - x if fx else x, 1 - y if fy else y, 1 - c if fc else c))
    return out


def _comm_out_shapes(srcs, modes):
    return [jax.ShapeDtypeStruct((N_DEV,) + s.shape if md == "gather" else s.shape, s.dtype)
            for s, md in zip(srcs, modes)]


def _comm_scratch(n):
    per_peer = n * (N_DEV - 1)
    return [pltpu.SemaphoreType.DMA((per_peer,)), pltpu.SemaphoreType.DMA((per_peer,)), pltpu.SemaphoreType.DMA((n,))]


class _Copies:
    def __init__(self, own, first, relay):
        self.own, self.first, self.relay = own, first, relay

    def start(self):
        for cp in self.own + self.first:
            cp.start()

    def forward(self):
        for arrival, onward in self.relay:
            arrival.wait_recv()
            onward.start()

    def finish(self):
        arrivals = [a for a, _ in self.relay]
        onward = [f for _, f in self.relay]
        for cp in self.first + onward:
            if not any(cp is a for a in arrivals):
                cp.wait_recv()
        for cp in self.first + onward:
            cp.wait_send()
        for cp in self.own:
            cp.wait()


def _comm_copies(src_refs, dst_refs, modes, send, recv, local):
    x, y, c = _mesh_position()
    me = _flat_index(x, y, c)
    n = len(modes)
    sibling = (x, y, 1 - c)
    chips = [(1 - x, y), (x, 1 - y), (1 - x, 1 - y)]

    def remote(src, dst, k, t, to):
        return pltpu.make_async_remote_copy(src_ref=src, dst_ref=dst, send_sem=send.at[k * n + t],
                                            recv_sem=recv.at[k * n + t], device_id=to,
                                            device_id_type=pl.DeviceIdType.MESH)

    own, first, relay = [], [], []
    for t, (src, dst, md) in enumerate(zip(src_refs, dst_refs, modes)):
        if md == "scatter":
            own.append(pltpu.make_async_copy(src.at[me], dst.at[me], local.at[t]))
            for k, peer in enumerate(_peers(x, y, c)):
                first.append(remote(src.at[_flat_index(*peer)], dst.at[me], k, t, peer))
        else:
            own.append(pltpu.make_async_copy(src, dst.at[me], local.at[t]))
            first.append(remote(src, dst.at[me], 0, t, sibling))
            for j, chip in enumerate(chips):
                arrival = remote(src, dst.at[me], 1 + j, t, (*chip, c))
                landed = dst.at[_flat_index(*chip, c)]
                first.append(arrival)
                relay.append((arrival, remote(landed, landed, 4 + j, t, sibling)))
    return _Copies(own, first, relay)


def _hosted(body, n_in, n_out, modes, grid):
    t = len(modes)
    total = math.prod(grid)

    def wrapped(*refs):
        ins, csrc = refs[:n_in], refs[n_in:n_in + t]
        outs = refs[n_in + t:n_in + t + n_out]
        cdst = refs[n_in + t + n_out:n_in + 2 * t + n_out]
        scratch = refs[n_in + 2 * t + n_out:-3]
        copies = _comm_copies(csrc, cdst, modes, *refs[-3:])
        step = pl.program_id(0)
        for axis in range(1, len(grid)):
            step = step * grid[axis] + pl.program_id(axis)

        @pl.when(step == 0)
        def _():
            copies.start()

        body(*ins, *outs, *scratch)

        @pl.when(step == (total * 3) // 5)
        def _():
            copies.forward()

        @pl.when(step == total - 1)
        def _():
            copies.finish()

    return wrapped


def _call(body, name, grid, in_specs, out_specs, out_shape, sems, args, scratch=(), comm=None):
    if comm is None:
        outs = pl.pallas_call(body, name=name, grid=grid, in_specs=in_specs, out_specs=out_specs, out_shape=out_shape,
                              scratch_shapes=list(scratch), compiler_params=_params(sems))(*args)
        return outs, []
    srcs, modes = comm
    n = len(modes)
    res = pl.pallas_call(
        _hosted(body, len(in_specs), len(out_specs), modes, grid), name=name, grid=grid,
        in_specs=list(in_specs) + [HBM_WHOLE] * n, out_specs=list(out_specs) + [HBM_WHOLE] * n,
        out_shape=list(out_shape) + _comm_out_shapes(srcs, modes),
        scratch_shapes=list(scratch) + _comm_scratch(n),
        compiler_params=_params(("arbitrary",) * len(grid)))(*args, *srcs)
    return res[:len(out_specs)], res[len(out_specs):]


def exchange(srcs, modes, name):
    n = len(modes)

    def body(*refs):
        copies = _comm_copies(refs[:n], refs[n:2 * n], modes, *refs[2 * n:])
        copies.start()
        copies.forward()
        copies.finish()

    return pl.pallas_call(body, name=name, in_specs=[HBM_WHOLE] * n, out_specs=[HBM_WHOLE] * n,
                          out_shape=_comm_out_shapes(srcs, modes), scratch_shapes=_comm_scratch(n))(*srcs)


def _nn(a, b):
    return jnp.dot(a, b, preferred_element_type=F32)


def _nt(a, b):
    return lax.dot_general(a, b, (((1,), (1,)), ((), ())), preferred_element_type=F32)


def _tn(a, b):
    return lax.dot_general(a, b, (((0,), (0,)), ((), ())), preferred_element_type=F32)


def _sig(x):
    return 1.0 / (1.0 + jnp.exp(-x))


def _rms_r(x, n=None):
    n = x.shape[-1] if n is None else n
    return lax.rsqrt(jnp.sum(x * x, axis=-1, keepdims=True) * (1.0 / n) + EPS)


def _rms_bwd(x, r, g, dy, n=None):
    n = x.shape[-1] if n is None else n
    xhat = x * r
    dxhat = dy * g
    dx = r * (dxhat - xhat * (jnp.sum(dxhat * xhat, axis=-1, keepdims=True) * (1.0 / n)))
    return dx, jnp.sum(dy * xhat, axis=0, keepdims=True)


def _accumulate(ref, val, first):
    @pl.when(first)
    def _():
        ref[...] = val

    @pl.when(jnp.logical_not(first))
    def _():
        ref[...] += val


_GELU_C = math.sqrt(2.0 / math.pi)


def _gelu_and_grad(x):
    inner = _GELU_C * (x + 0.044715 * x * x * x)
    t = jnp.tanh(inner)
    gelu = 0.5 * x * (1.0 + t)
    dgelu = 0.5 * (1.0 + t) + 0.5 * x * (1.0 - t * t) * _GELU_C * (1.0 + 3.0 * 0.044715 * x * x)
    return gelu, dgelu


def _log1p_small(t):
    return jnp.where(t < 1e-3, t * (1.0 - t * (0.5 - t * (1.0 / 3.0))), jnp.log(1.0 + t))


def _softplus(x):
    return jnp.maximum(x, 0.0) + _log1p_small(jnp.exp(-jnp.abs(x)))


def _sig_tanh(x):
    return 0.5 + 0.5 * jnp.tanh(0.5 * x)


def _ff_chunks(f):
    return 2 if (f // 2) % LANES == 0 else 1


def _swiglu_half(x, g_ref, wg_ref, wu_ref, wd_ref, a_ref, b_ref, fc):
    f = wg_ref.shape[0]
    u = (x * _rms_r(x) * g_ref[...]).astype(BF16)
    acc = jnp.zeros(x.shape, F32)
    for c in range(f // fc):
        cols = slice(c * fc, (c + 1) * fc)
        a = _nt(u, wg_ref[cols, :])
        b = _nt(u, wu_ref[cols, :])
        s = (a * _sig(a) * b).astype(BF16)
        acc = acc + _nn(s, wd_ref[cols, :])
        a_ref[:, cols] = a.astype(BF16)
        b_ref[:, cols] = b.astype(BF16)
    return x + 0.5 * acc, u


def ffn_up(h, g, wg, wu, name, comm=None):
    n, d = h.shape
    f = wg.shape[0]
    tm = _tile(n, 528)
    fc = 2 * LANES if f % (2 * LANES) == 0 else f

    def body(h_ref, g_ref, wg_ref, wu_ref, u_ref, a_ref, b_ref, s_ref):
        x = h_ref[...]
        u = (x * _rms_r(x) * g_ref[...]).astype(BF16)
        u_ref[...] = u
        for c in range(f // fc):
            cols = slice(c * fc, (c + 1) * fc)
            a = _nt(u, wg_ref[cols, :])
            b = _nt(u, wu_ref[cols, :])
            a_ref[:, cols] = a.astype(BF16)
            b_ref[:, cols] = b.astype(BF16)
            s_ref[:, cols] = (0.5 * (a * _sig(a) * b)).astype(BF16)

    wide = jax.ShapeDtypeStruct((n, f), BF16)
    return _call(
        body, name, (n // tm,),
        [_row(tm, d), _fixed((1, d)), VMEM_WHOLE, VMEM_WHOLE],
        [_row(tm, d), _row(tm, f), _row(tm, f), _row(tm, f)],
        [jax.ShapeDtypeStruct((n, d), BF16), wide, wide, wide],
        ("parallel",), (h, g, wg, wu), comm=comm)


def ffn_down_inproj(h, s, wd, g, wm, wl, name, comm=None):
    n, d = h.shape
    f = wd.shape[0]
    tm = _tile(n, 528)

    def body(h_ref, s_ref, wd_ref, g_ref, wm_ref, wl_ref, ho_ref, u_ref, zm_ref, zl_ref):
        x = h_ref[...] + _nn(s_ref[...], wd_ref[...])
        ho_ref[...] = x
        u = (x * _rms_r(x) * g_ref[...]).astype(BF16)
        u_ref[...] = u
        zm_ref[...] = _nt(u, wm_ref[...])
        zl_ref[...] = _nt(u, wl_ref[...])

    return _call(
        body, name, (n // tm,),
        [_row(tm, d), _row(tm, f), VMEM_WHOLE, _fixed((1, d)), VMEM_WHOLE, VMEM_WHOLE],
        [_row(tm, d), _row(tm, d), _row(tm, MLA_IN), _row(tm, 2 * LRU_WIDTH)],
        [jax.ShapeDtypeStruct((n, d), F32), jax.ShapeDtypeStruct((n, d), BF16),
         jax.ShapeDtypeStruct((n, MLA_IN), F32), jax.ShapeDtypeStruct((n, 2 * LRU_WIDTH), F32)],
        ("parallel",), (h, s, wd, g, wm, wl), comm=comm)


def ffn_fwd_loss(h, g, wg, wu, wd, g_final, tgt, lp, name):
    n, d = h.shape
    f = wg.shape[0]
    tm = _tile(lp, 528)
    per_seq = lp // tm
    fc = 2 * LANES if f % (2 * LANES) == 0 else f

    def body(h_ref, g_ref, wg_ref, wu_ref, wd_ref, gf_ref, t_ref, dh_ref, u_ref, a_ref, b_ref, loss_ref, dgf_ref):
        i = pl.program_id(0)
        y, u_ref[...] = _swiglu_half(h_ref[...], g_ref, wg_ref, wu_ref, wd_ref, a_ref, b_ref, fc)
        dh_ref[...], part, dg = _loss_and_grad(y, gf_ref[...], t_ref[...], (i % per_seq) * tm)
        _accumulate(loss_ref, jnp.broadcast_to(part, (1, LANES)), i == 0)
        _accumulate(dgf_ref, dg, i == 0)

    outs, _ = _call(
        body, name, (n // tm,),
        [_row(tm, d), _fixed((1, d)), VMEM_WHOLE, VMEM_WHOLE, VMEM_WHOLE, _fixed((1, d)), _row(tm, d)],
        [_row(tm, d), _row(tm, d), _row(tm, f), _row(tm, f), _fixed((1, LANES)), _fixed((1, d))],
        [jax.ShapeDtypeStruct((n, d), F32), jax.ShapeDtypeStruct((n, d), BF16),
         jax.ShapeDtypeStruct((n, f), BF16), jax.ShapeDtypeStruct((n, f), BF16),
         jax.ShapeDtypeStruct((1, LANES), F32), jax.ShapeDtypeStruct((1, d), F32)],
        ("arbitrary",), (h, g, wg, wu, wd, g_final, tgt))
    return outs


def ffn_bwd_act(dh, h, g, a, b, wg, wu, wd, name, comm=None, emit_sh=True):
    n, d = h.shape
    f = wg.shape[0]
    tm = _tile(n, 352 if emit_sh else 384)
    nc = _ff_chunks(f)
    fc = f // nc

    def body(dh_ref, h_ref, g_ref, a_ref, b_ref, wg_ref, wu_ref, wd_ref, dhi_ref, da_ref, db_ref, *rest):
        dg_ref = rest[-1]
        x = h_ref[...]
        dy = dh_ref[...]
        r = _rms_r(x)
        dhh = (0.5 * dy).astype(BF16)
        du = jnp.zeros((tm, d), F32)
        for c in range(nc):
            cols = slice(c * fc, (c + 1) * fc)
            ds = _nt(dhh, wd_ref[cols, :])
            av = a_ref[:, cols].astype(F32)
            bv = b_ref[:, cols].astype(F32)
            sg = _sig(av)
            sil = av * sg
            da = (ds * bv * (sg * (1.0 + av * (1.0 - sg)))).astype(BF16)
            db = (ds * sil).astype(BF16)
            da_ref[:, cols] = da
            db_ref[:, cols] = db
            if emit_sh:
                rest[0][:, cols] = (0.5 * sil * bv).astype(BF16)
            du = du + _nn(da, wg_ref[cols, :]) + _nn(db, wu_ref[cols, :])
        dx, dg = _rms_bwd(x, r, g_ref[...], du)
        dhi_ref[...] = dy + dx
        _accumulate(dg_ref, dg, pl.program_id(0) == 0)

    wide = [jax.ShapeDtypeStruct((n, f), BF16)] * (3 if emit_sh else 2)
    return _call(
        body, name, (n // tm,),
        [_row(tm, d), _row(tm, d), _fixed((1, d)), _row(tm, f), _row(tm, f), VMEM_WHOLE, VMEM_WHOLE, VMEM_WHOLE],
        [_row(tm, d)] + [_row(tm, f)] * len(wide) + [_fixed((1, d))],
        [jax.ShapeDtypeStruct((n, d), F32)] + wide + [jax.ShapeDtypeStruct((1, d), F32)],
        ("arbitrary",), (dh, h, g, a, b, wg, wu, wd), comm=comm)


def tn_matmul(x, y, name, out="f32", comm=None):
    n, k = x.shape
    m = y.shape[1]
    tm = _tile(n, TN_ROWS)
    kc, mc = k, (512 if m % 512 == 0 else m)
    while tm * kc * x.dtype.itemsize > TN_X_BYTES and kc % (2 * LANES) == 0:
        kc //= 2
    while tm * mc * y.dtype.itemsize > TN_Y_BYTES and mc % (2 * LANES) == 0:
        mc //= 2
    steps = n // tm

    def body(x_ref, y_ref, o_ref, *acc):
        i = pl.program_id(2)
        part = _tn(x_ref[...].astype(BF16), y_ref[...].astype(BF16))
        if steps == 1:
            o_ref[...] = part.astype(o_ref.dtype)
        elif out == "f32":
            _accumulate(o_ref, part, i == 0)
        else:
            _accumulate(acc[0], part, i == 0)

            @pl.when(i == steps - 1)
            def _():
                o_ref[...] = acc[0][...].astype(BF16)

    out_shape = jax.ShapeDtypeStruct((k, m), F32 if out == "f32" else BF16)
    (res,), landed = _call(
        body, name, (k // kc, m // mc, steps),
        [pl.BlockSpec((tm, kc), lambda a, b, i: (i, a)), pl.BlockSpec((tm, mc), lambda a, b, i: (i, b))],
        [pl.BlockSpec((kc, mc), lambda a, b, i: (a, b))], [out_shape], ("parallel", "parallel", "arbitrary"), (x, y),
        scratch=[pltpu.VMEM((kc, mc), F32)] if (out == "bf16" and steps > 1) else [], comm=comm)
    return (res, landed) if comm is not None else res


def inproj_bwd(dzm, du, dgate, dh2, h, g, wm, wl, comm=None):
    n, d = h.shape
    tm = _tile(n, 352)

    def body(dzm_ref, du_ref, dgt_ref, dh2_ref, h_ref, g_ref, wm_ref, wl_ref, dh_ref, dg_ref):
        x = h_ref[...]
        dun = (_nn(dzm_ref[...].astype(BF16), wm_ref[...])
               + _nn(du_ref[...].astype(BF16), wl_ref[:LRU_WIDTH, :])
               + _nn(dgt_ref[...].astype(BF16), wl_ref[LRU_WIDTH:, :]))
        dx, dg = _rms_bwd(x, _rms_r(x), g_ref[...], dun)
        dh_ref[...] = dh2_ref[...] + dx
        _accumulate(dg_ref, dg, pl.program_id(0) == 0)

    return _call(
        body, "inproj_bwd", (n // tm,),
        [_row(tm, MLA_IN), _row(tm, LRU_WIDTH), _row(tm, LRU_WIDTH), _row(tm, d), _row(tm, d),
         _fixed((1, d)), VMEM_WHOLE, VMEM_WHOLE],
        [_row(tm, d), _fixed((1, d))],
        [jax.ShapeDtypeStruct((n, d), F32), jax.ShapeDtypeStruct((1, d), F32)],
        ("arbitrary",), (dzm, du, dgate, dh2, h, g, wm, wl), comm=comm)


def _rope_tables(lp):
    pos = jnp.arange(lp, dtype=F32) - float(PAD)
    half = D_ROPE // 2
    inv_freq = ROPE_THETA ** (-jnp.arange(0, half, dtype=F32) / half)
    ang = pos[:, None] * inv_freq[None, :]
    cos, sin = jnp.cos(ang), jnp.sin(ang)
    one = jnp.ones((lp, D_NOPE), F32)
    z_nope = jnp.zeros((lp, D_NOPE), F32)
    z_half = jnp.zeros((lp, half), F32)
    z_tail = jnp.zeros((lp, HEAD_SLAB - D_QK), F32)
    cosr = jnp.concatenate([one, cos, cos, z_tail], axis=1)
    sin_up = jnp.concatenate([z_nope, z_half, sin, z_tail], axis=1)
    sin_dn = jnp.concatenate([z_nope, -sin, z_half, z_tail], axis=1)
    return cosr, sin_up, sin_dn


def _rope(x, cosr, sin_up, sin_dn):
    half = D_ROPE // 2
    return x * cosr + pltpu.roll(x, half, axis=1) * sin_up + pltpu.roll(x, HEAD_SLAB - half, axis=1) * sin_dn


def _rope_bwd(dy, cosr, sin_up, sin_dn):
    half = D_ROPE // 2
    return (dy * cosr + pltpu.roll(dy * sin_up, HEAD_SLAB - half, axis=1)
            + pltpu.roll(dy * sin_dn, half, axis=1))


def _k_rope_slab(zm_tile):
    tm = zm_tile.shape[0]
    krp = zm_tile[:, Q_RANK + KV_RANK:MLA_IN]
    return jnp.concatenate([jnp.zeros((tm, D_NOPE), F32), krp], axis=1)


def mla_prep_fwd(zm, gql, gkvl, wuq, wuk, wuv, gqh, gkh, tables, lp):
    n = zm.shape[0]
    tm = _tile(lp, 352)
    per_seq = lp // tm
    width = MLA_HEADS * HEAD_SLAB
    scale = 1.0 / math.sqrt(D_QK)

    def body(zm_ref, gql_ref, gkvl_ref, wuq_ref, wuk_ref, wuv_ref, gqh_ref, gkh_ref,
             cos_ref, up_ref, dn_ref, q_ref, k_ref, v_ref, qn_ref, cn_ref):
        z = zm_ref[...]
        cq = z[:, :Q_RANK]
        ckv = z[:, Q_RANK:Q_RANK + KV_RANK]
        qn = (cq * _rms_r(cq) * gql_ref[...]).astype(BF16)
        cn = (ckv * _rms_r(ckv) * gkvl_ref[...]).astype(BF16)
        qn_ref[...] = qn
        cn_ref[...] = cn
        q_raw = _nt(qn, wuq_ref[...])
        k_raw = _nt(cn, wuk_ref[...])
        v_ref[...] = _nt(cn, wuv_ref[...]).astype(BF16)
        kr_slab = _k_rope_slab(z)
        cosr, sin_up, sin_dn = cos_ref[...], up_ref[...], dn_ref[...]
        for hd in range(MLA_HEADS):
            cols = slice(hd * HEAD_SLAB, (hd + 1) * HEAD_SLAB)
            xq = q_raw[:, cols]
            yq = _rope(xq * _rms_r(xq, D_QK) * gqh_ref[...], cosr, sin_up, sin_dn)
            q_ref[:, cols] = (yq * scale).astype(BF16)
            xk = k_raw[:, cols] + kr_slab
            yk = _rope(xk * _rms_r(xk, D_QK) * gkh_ref[...], cosr, sin_up, sin_dn)
            k_ref[:, cols] = yk.astype(BF16)

    tab = pl.BlockSpec((tm, HEAD_SLAB), lambda i: (i % per_seq, 0))
    return pl.pallas_call(
        body, name="mla_prep_fwd", grid=(n // tm,),
        in_specs=[_row(tm, MLA_IN), _fixed((1, Q_RANK)), _fixed((1, KV_RANK)), VMEM_WHOLE, VMEM_WHOLE, VMEM_WHOLE,
                  _fixed((1, HEAD_SLAB)), _fixed((1, HEAD_SLAB)), tab, tab, tab],
        out_specs=[_row(tm, width), _row(tm, width), _row(tm, MLA_HEADS * D_V), _row(tm, Q_RANK), _row(tm, KV_RANK)],
        out_shape=[jax.ShapeDtypeStruct((n, width), BF16), jax.ShapeDtypeStruct((n, width), BF16),
                   jax.ShapeDtypeStruct((n, MLA_HEADS * D_V), BF16), jax.ShapeDtypeStruct((n, Q_RANK), BF16),
                   jax.ShapeDtypeStruct((n, KV_RANK), BF16)],
        compiler_params=_params(("parallel",)),
    )(zm, gql, gkvl, wuq, wuk, wuv, gqh, gkh, *tables)


def mla_prep_bwd(dq, dk, dv, zm, qn, cn, gql, gkvl, wuq, wuk, wuv, gqh, gkh, tables, lp, comm=None):
    n = zm.shape[0]
    tm = _tile(lp, 704)
    per_seq = lp // tm
    width = MLA_HEADS * HEAD_SLAB
    scale = 1.0 / math.sqrt(D_QK)

    def body(dq_ref, dk_ref, dv_ref, zm_ref, qn_ref, cn_ref, gql_ref, gkvl_ref, wuq_ref, wuk_ref, wuv_ref,
             gqh_ref, gkh_ref, cos_ref, up_ref, dn_ref,
             dzm_ref, dqr_ref, dkr_ref, dgql_ref, dgkvl_ref, dgqh_ref, dgkh_ref):
        z = zm_ref[...]
        cq = z[:, :Q_RANK]
        ckv = z[:, Q_RANK:Q_RANK + KV_RANK]
        q_raw = _nt(qn_ref[...], wuq_ref[...])
        k_raw = _nt(cn_ref[...], wuk_ref[...])
        kr_slab = _k_rope_slab(z)
        cosr, sin_up, sin_dn = cos_ref[...], up_ref[...], dn_ref[...]
        dgq = jnp.zeros((1, HEAD_SLAB), F32)
        dgk = jnp.zeros((1, HEAD_SLAB), F32)
        dkrp = jnp.zeros((tm, HEAD_SLAB - D_NOPE), F32)
        for hd in range(MLA_HEADS):
            cols = slice(hd * HEAD_SLAB, (hd + 1) * HEAD_SLAB)
            xq = q_raw[:, cols]
            dxn = _rope_bwd(dq_ref[:, cols] * scale, cosr, sin_up, sin_dn)
            dxq, dg = _rms_bwd(xq, _rms_r(xq, D_QK), gqh_ref[...], dxn, D_QK)
            dgq = dgq + dg
            dqr_ref[:, cols] = dxq.astype(BF16)
            xk = k_raw[:, cols] + kr_slab
            dxn = _rope_bwd(dk_ref[:, cols], cosr, sin_up, sin_dn)
            dxk, dg = _rms_bwd(xk, _rms_r(xk, D_QK), gkh_ref[...], dxn, D_QK)
            dgk = dgk + dg
            dkr_ref[:, cols] = dxk.astype(BF16)
            dkrp = dkrp + dxk[:, D_NOPE:]
        dqn = _nn(dqr_ref[...], wuq_ref[...])
        dcn = _nn(dkr_ref[...], wuk_ref[...]) + _nn(dv_ref[...].astype(BF16), wuv_ref[...])
        dcq, dg1 = _rms_bwd(cq, _rms_r(cq), gql_ref[...], dqn)
        dckv, dg2 = _rms_bwd(ckv, _rms_r(ckv), gkvl_ref[...], dcn)
        dzm_ref[:, :Q_RANK] = dcq
        dzm_ref[:, Q_RANK:Q_RANK + KV_RANK] = dckv
        dzm_ref[:, Q_RANK + KV_RANK:] = dkrp
        first = pl.program_id(0) == 0
        _accumulate(dgql_ref, dg1, first)
        _accumulate(dgkvl_ref, dg2, first)
        _accumulate(dgqh_ref, dgq, first)
        _accumulate(dgkh_ref, dgk, first)

    tab = pl.BlockSpec((tm, HEAD_SLAB), lambda i: (i % per_seq, 0))
    return _call(
        body, "mla_prep_bwd", (n // tm,),
        [_row(tm, width), _row(tm, width), _row(tm, MLA_HEADS * D_V), _row(tm, MLA_IN),
         _row(tm, Q_RANK), _row(tm, KV_RANK), _fixed((1, Q_RANK)), _fixed((1, KV_RANK)),
         VMEM_WHOLE, VMEM_WHOLE, VMEM_WHOLE, _fixed((1, HEAD_SLAB)), _fixed((1, HEAD_SLAB)), tab, tab, tab],
        [_row(tm, MLA_IN), _row(tm, width), _row(tm, width), _fixed((1, Q_RANK)), _fixed((1, KV_RANK)),
         _fixed((1, HEAD_SLAB)), _fixed((1, HEAD_SLAB))],
        [jax.ShapeDtypeStruct((n, MLA_IN), F32), jax.ShapeDtypeStruct((n, width), BF16),
         jax.ShapeDtypeStruct((n, width), BF16), jax.ShapeDtypeStruct((1, Q_RANK), F32),
         jax.ShapeDtypeStruct((1, KV_RANK), F32), jax.ShapeDtypeStruct((1, HEAD_SLAB), F32),
         jax.ShapeDtypeStruct((1, HEAD_SLAB), F32)],
        ("arbitrary",), (dq, dk, dv, zm, qn, cn, gql, gkvl, wuq, wuk, wuv, gqh, gkh, *tables), comm=comm)


def _attn_tile(lp):
    return _tile(lp, 704, CHUNK)


def _chunk_mask(i, j, t):
    qpos = i * t + lax.broadcasted_iota(jnp.int32, (t, t), 0)
    kpos = j * t + lax.broadcasted_iota(jnp.int32, (t, t), 1)
    same_or_earlier = jnp.right_shift(kpos, CHUNK_SHIFT) <= jnp.right_shift(qpos, CHUNK_SHIFT)
    return jnp.logical_and(same_or_earlier, kpos >= PAD)


def _masked_scores(s, i, j, t, diagonal):
    if diagonal:
        return jnp.where(_chunk_mask(i, j, t), s, NEG_INF)
    kpos = j * t + lax.broadcasted_iota(jnp.int32, (1, t), 1)
    return s + jnp.where(kpos < PAD, NEG_INF, 0.0)


def attn_fwd(q, k, v, nb, lp, comm=None):
    n = q.shape[0]
    t = _attn_tile(lp)
    nq = lp // t

    hp = ATTN_HEADS_PER_STEP

    def body(q_ref, k_ref, v_ref, o_ref, lse_ref):
        i = pl.program_id(2)
        qs = [q_ref[:, hh * HEAD_SLAB:(hh + 1) * HEAD_SLAB] for hh in range(hp)]

        def kv_step(j, carry, diagonal=False):
            off = pl.multiple_of(j * t, t)
            out = []
            for hh in range(hp):
                m, l, acc = carry[hh]
                kv = k_ref[pl.ds(off, t), hh * HEAD_SLAB:(hh + 1) * HEAD_SLAB]
                s = _masked_scores(_nt(qs[hh], kv), i, j, t, diagonal)
                m_new = jnp.maximum(m, jnp.max(s, axis=-1, keepdims=True))
                p = jnp.exp(s - m_new)
                alpha = jnp.exp(m - m_new)
                l = alpha * l + jnp.sum(p, axis=-1, keepdims=True)
                acc = alpha * acc + _nn(p.astype(BF16), v_ref[pl.ds(off, t), hh * D_V:(hh + 1) * D_V])
                out.append((m_new, l, acc))
            return tuple(out)

        init = tuple((jnp.full((t, 1), NEG_INF, F32), jnp.zeros((t, 1), F32), jnp.zeros((t, D_V), F32))
                     for _ in range(hp))
        done = kv_step(i, lax.fori_loop(0, i, kv_step, init), diagonal=True)
        for hh, (m, l, acc) in enumerate(done):
            o_ref[:, hh * D_V:(hh + 1) * D_V] = acc * (1.0 / l)
            lse_ref[hh] = jnp.broadcast_to(m + jnp.log(l), (t, LANES))

    return _call(
        body, "attn_fwd", (nb, MLA_HEADS // hp, nq),
        [pl.BlockSpec((t, hp * HEAD_SLAB), lambda b, h, i: (b * nq + i, h)),
         pl.BlockSpec((lp, hp * HEAD_SLAB), lambda b, h, i: (b, h)),
         pl.BlockSpec((lp, hp * D_V), lambda b, h, i: (b, h))],
        [pl.BlockSpec((t, hp * D_V), lambda b, h, i: (b * nq + i, h)),
         pl.BlockSpec((hp, t, LANES), lambda b, h, i: (h, b * nq + i, 0))],
        [jax.ShapeDtypeStruct((n, MLA_HEADS * D_V), F32), jax.ShapeDtypeStruct((MLA_HEADS, n, LANES), F32)],
        ("parallel", "parallel", "parallel"), (q, k, v), comm=comm)


def attn_bwd(q, k, v, o, do, lse, nb, lp, comm=None):
    n = q.shape[0]
    t = _attn_tile(lp)
    nq = lp // t

    def body(q_ref, k_ref, v_ref, o_ref, do_ref, lse_ref, dq_ref, dk_ref, dv_ref):
        dk_ref[...] = jnp.zeros_like(dk_ref)
        dv_ref[...] = jnp.zeros_like(dv_ref)

        def q_step(i, _):
            qoff = pl.multiple_of(i * t, t)
            qv = q_ref[pl.ds(qoff, t), :]
            dov = do_ref[pl.ds(qoff, t), :]
            delta = jnp.sum(o_ref[pl.ds(qoff, t), :] * dov, axis=-1, keepdims=True)
            lse_q = jnp.max(lse_ref[0, pl.ds(qoff, t), :], axis=-1, keepdims=True)
            do16 = dov.astype(BF16)

            def kv_step(j, dq_acc, diagonal=False):
                koff = pl.multiple_of(j * t, t)
                kv = k_ref[pl.ds(koff, t), :]
                s = _masked_scores(_nt(qv, kv), i, j, t, diagonal)
                p = jnp.exp(s - lse_q)
                dp = _nt(do16, v_ref[pl.ds(koff, t), :])
                ds16 = (p * (dp - delta)).astype(BF16)
                dv_ref[pl.ds(koff, t), :] += _tn(p.astype(BF16), do16)
                dk_ref[pl.ds(koff, t), :] += _tn(ds16, qv)
                return dq_acc + _nn(ds16, kv)

            earlier = lax.fori_loop(0, i, kv_step, jnp.zeros((t, HEAD_SLAB), F32))
            dq_ref[pl.ds(qoff, t), :] = kv_step(i, earlier, diagonal=True)
            return 0

        lax.fori_loop(0, nq, q_step, 0)

    wide = pl.BlockSpec((lp, HEAD_SLAB), lambda b, h: (b, h))
    thin = pl.BlockSpec((lp, D_V), lambda b, h: (b, h))
    width = MLA_HEADS * HEAD_SLAB
    return _call(
        body, "attn_bwd", (nb, MLA_HEADS),
        [wide, wide, thin, thin, thin, pl.BlockSpec((1, lp, LANES), lambda b, h: (h, b, 0))],
        [wide, wide, thin],
        [jax.ShapeDtypeStruct((n, width), F32), jax.ShapeDtypeStruct((n, width), F32),
         jax.ShapeDtypeStruct((n, MLA_HEADS * D_V), F32)],
        ("parallel", "parallel"), (q, k, v, o, do, lse), comm=comm)


def _seq_rows(nb, lp, width):
    rows = lax.broadcasted_iota(jnp.int32, (lp, width), 0)
    return jnp.concatenate([rows] * nb, axis=0) if nb > 1 else rows


def _lru_gates(u, w_ref, cb, wa, wx, ba, bx, lam):
    xc = (cb + w_ref[pl.ds(3, 1), :] * u + w_ref[pl.ds(2, 1), :] * pltpu.roll(u, 1, axis=0)
          + w_ref[pl.ds(1, 1), :] * pltpu.roll(u, 2, axis=0) + w_ref[pl.ds(0, 1), :] * pltpu.roll(u, 3, axis=0))
    xc16 = xc.astype(BF16)
    ra = _sig_tanh(_nn(xc16, wa) + ba)
    ia = _sig_tanh(_nn(xc16, wx) + bx)
    sp = _softplus(-lam)
    log_a = -C_RGLRU * ra * sp
    a = jnp.exp(log_a)
    x2 = 2.0 * log_a
    mult = jnp.sqrt(jnp.where(x2 > -1e-2, -x2 * (1.0 + x2 * (0.5 + x2 * (1.0 / 6.0))), 1.0 - a * a))
    return xc, xc16, ra, ia, sp, a, mult


def _scan_block_rows(width):
    return lax.broadcasted_iota(jnp.int32, (8, width), 0)


def lru_fwd(zl, conv_w, conv_b, wa, wx, ba, bx, lam, nb, lp, comm=None):
    n = zl.shape[0]
    w = LRU_TILE
    nt = LRU_WIDTH // w
    nblk = lp // 8

    def body(u_ref, gt_ref, cw_ref, cb_ref, wa_ref, wx_ref, ba_ref, bx_ref, lam_ref, y_ref, h_ref, a_s, b_s):
        u = u_ref[...]
        xc, _, _, ia, _, a, mult = _lru_gates(u, cw_ref, cb_ref[...], wa_ref[...], wx_ref[...],
                                              ba_ref[...], bx_ref[...], lam_ref[...])
        row = _seq_rows(nb, lp, w)
        mult = jnp.where(row == PAD, 1.0, mult)
        a_s[...] = a
        b_s[...] = jnp.where(row < PAD, 0.0, mult * (ia * xc))
        r8 = _scan_block_rows(w)

        def blk(i, carry):
            out = []
            for s_id in range(nb):
                off = pl.multiple_of(s_id * lp + i * 8, 8)
                av = a_s[pl.ds(off, 8), :]
                bv = b_s[pl.ds(off, 8), :]
                for sh in (1, 2, 4):
                    keep = r8 >= sh
                    bv = jnp.where(keep, av * pltpu.roll(bv, sh, axis=0) + bv, bv)
                    av = jnp.where(keep, av * pltpu.roll(av, sh, axis=0), av)
                hv = bv + av * carry[s_id]
                h_ref[pl.ds(off, 8), :] = hv
                out.append(jnp.sum(jnp.where(r8 == 7, hv, 0.0), axis=0, keepdims=True))
            return tuple(out)

        lax.fori_loop(0, nblk, blk, tuple(jnp.zeros((1, w), F32) for _ in range(nb)))
        gelu, _ = _gelu_and_grad(gt_ref[...])
        y_ref[...] = h_ref[...] * gelu

    col = lambda c: (0, c)
    return _call(
        body, "lru_fwd", (nt,),
        [pl.BlockSpec((n, w), col), pl.BlockSpec((n, w), lambda c: (0, nt + c)),
         pl.BlockSpec((CONV_W, w), col), pl.BlockSpec((1, w), col),
         pl.BlockSpec((w, w), lambda c: (c, c)), pl.BlockSpec((w, w), lambda c: (c, c)),
         pl.BlockSpec((1, w), col), pl.BlockSpec((1, w), col), pl.BlockSpec((1, w), col)],
        [pl.BlockSpec((n, w), col), pl.BlockSpec((n, w), col)],
        [jax.ShapeDtypeStruct((n, LRU_WIDTH), F32), jax.ShapeDtypeStruct((n, LRU_WIDTH), F32)],
        ("parallel",), (zl, zl, conv_w, conv_b, wa, wx, ba, bx, lam),
        scratch=[pltpu.VMEM((n, w), F32), pltpu.VMEM((n, w), F32)], comm=comm)


def lru_bwd(zl, hs, dy, conv_w, conv_b, wa, wx, ba, bx, lam, nb, lp, comm=None):
    n = zl.shape[0]
    w = LRU_TILE
    nt = LRU_WIDTH // w
    nblk = lp // 8

    def body(u_ref, gt_ref, h_ref, dy_ref, cw_ref, cb_ref, wa_ref, wx_ref, ba_ref, bx_ref, lam_ref,
             du_ref, dgt_ref, dcw_ref, dcb_ref, dba_ref, dbx_ref, dlam_ref, dwa_ref, dwx_ref,
             c_s, d_s, g_s, dwa_s, dwx_s):
        u = u_ref[...]
        lam = lam_ref[...]
        xc, xc16, ra, ia, sp, a, mult = _lru_gates(u, cw_ref, cb_ref[...], wa_ref[...], wx_ref[...],
                                                   ba_ref[...], bx_ref[...], lam)
        row = lax.broadcasted_iota(jnp.int32, (lp, w), 0)
        hv = h_ref[...]
        dyv = dy_ref[...]
        gelu, dgelu = _gelu_and_grad(gt_ref[...])
        dgt_ref[...] = jnp.where(row >= PAD, dyv * hv * dgelu, 0.0)
        c_s[...] = pltpu.roll(a, lp - 1, axis=0)
        d_s[...] = dyv * gelu
        r8 = _scan_block_rows(w)

        def blk(ii, carry):
            off = pl.multiple_of((nblk - 1 - ii) * 8, 8)
            cv = c_s[pl.ds(off, 8), :]
            dv = d_s[pl.ds(off, 8), :]
            for sh in (1, 2, 4):
                keep = r8 < 8 - sh
                dv = jnp.where(keep, cv * pltpu.roll(dv, 8 - sh, axis=0) + dv, dv)
                cv = jnp.where(keep, cv * pltpu.roll(cv, 8 - sh, axis=0), cv)
            gv = dv + cv * carry
            g_s[pl.ds(off, 8), :] = gv
            return jnp.sum(jnp.where(r8 == 0, gv, 0.0), axis=0, keepdims=True)

        lax.fori_loop(0, nblk, blk, jnp.zeros((1, w), F32))
        gv = g_s[...]
        first_row = row == PAD
        db = jnp.where(row >= PAD, gv, 0.0)
        da = jnp.where(row > PAD, gv * pltpu.roll(hv, 1, axis=0), 0.0)
        mult_eff = jnp.where(first_row, 1.0, mult)
        dmult = jnp.where(first_row, 0.0, db * (ia * xc))
        dia = db * mult_eff * xc
        dxc = db * mult_eff * ia
        dla = da * a - dmult * (a * a) / mult
        dra = dla * (-C_RGLRU * sp)
        dsp = jnp.sum(dla * (-C_RGLRU * ra), axis=0, keepdims=True)
        dpa = dra * ra * (1.0 - ra)
        dpx = dia * ia * (1.0 - ia)
        dpa16 = dpa.astype(BF16)
        dpx16 = dpx.astype(BF16)
        dxc = dxc + _nt(dpa16, wa_ref[...]) + _nt(dpx16, wx_ref[...])
        du = cw_ref[pl.ds(CONV_W - 1, 1), :] * dxc
        dcw = [jnp.sum(dxc * u, axis=0, keepdims=True)]
        for tap in range(1, CONV_W):
            dcw.insert(0, jnp.sum(dxc * pltpu.roll(u, tap, axis=0), axis=0, keepdims=True))
            du = du + cw_ref[pl.ds(CONV_W - 1 - tap, 1), :] * pltpu.roll(dxc, lp - tap, axis=0)
        du_ref[...] = jnp.where(row >= PAD, du, 0.0)
        first = pl.program_id(1) == 0
        _accumulate(dlam_ref, -_sig(-lam) * dsp, first)
        _accumulate(dba_ref, jnp.sum(dpa, axis=0, keepdims=True), first)
        _accumulate(dbx_ref, jnp.sum(dpx, axis=0, keepdims=True), first)
        _accumulate(dcb_ref, jnp.sum(dxc, axis=0, keepdims=True), first)
        _accumulate(dcw_ref, jnp.concatenate(dcw, axis=0), first)
        _accumulate(dwa_s, _tn(xc16, dpa16), first)
        _accumulate(dwx_s, _tn(xc16, dpx16), first)

        @pl.when(pl.program_id(1) == nb - 1)
        def _():
            for j in range(w // LRU_BLOCK):
                blk_rows = slice(j * LRU_BLOCK, (j + 1) * LRU_BLOCK)
                dwa_ref[0, blk_rows, :] = dwa_s[blk_rows, blk_rows]
                dwx_ref[0, blk_rows, :] = dwx_s[blk_rows, blk_rows]

    col = lambda c, b: (0, c)
    vec = pl.BlockSpec((1, w), col)
    mat = pl.BlockSpec((w, w), lambda c, b: (c, c))
    big = pl.BlockSpec((lp, w), lambda c, b: (b, c))
    dmat = pl.BlockSpec((1, w, LRU_BLOCK), lambda c, b: (c, 0, 0))
    return _call(
        body, "lru_bwd", (nt, nb),
        [big, pl.BlockSpec((lp, w), lambda c, b: (b, nt + c)), big, big,
         pl.BlockSpec((CONV_W, w), col), vec, mat, mat, vec, vec, vec],
        [big, big, pl.BlockSpec((CONV_W, w), col), vec, vec, vec, vec, dmat, dmat],
        [jax.ShapeDtypeStruct((n, LRU_WIDTH), F32), jax.ShapeDtypeStruct((n, LRU_WIDTH), F32),
         jax.ShapeDtypeStruct((CONV_W, LRU_WIDTH), F32), jax.ShapeDtypeStruct((1, LRU_WIDTH), F32),
         jax.ShapeDtypeStruct((1, LRU_WIDTH), F32), jax.ShapeDtypeStruct((1, LRU_WIDTH), F32),
         jax.ShapeDtypeStruct((1, LRU_WIDTH), F32), jax.ShapeDtypeStruct((nt, w, LRU_BLOCK), F32),
         jax.ShapeDtypeStruct((nt, w, LRU_BLOCK), F32)],
        ("parallel", "arbitrary"), (zl, zl, hs, dy, conv_w, conv_b, wa, wx, ba, bx, lam),
        scratch=[pltpu.VMEM((lp, w), F32), pltpu.VMEM((lp, w), F32), pltpu.VMEM((lp, w), F32),
                 pltpu.VMEM((w, w), F32), pltpu.VMEM((w, w), F32)], comm=comm)


def outproj_fwd(h, ya, yl, gao, glo, wout):
    n, d = h.shape
    half = ya.shape[1]
    tm = _tile(n, 704)

    def body(h_ref, ya_ref, yl_ref, gao_ref, glo_ref, w_ref, ho_ref, yn_ref):
        xa = ya_ref[...]
        xl = yl_ref[...]
        na = (xa * _rms_r(xa) * gao_ref[...]).astype(BF16)
        nl = (xl * _rms_r(xl) * glo_ref[...]).astype(BF16)
        yn_ref[:, :half] = na
        yn_ref[:, half:] = nl
        ho_ref[...] = h_ref[...] + _nn(na, w_ref[:half, :]) + _nn(nl, w_ref[half:, :])

    return pl.pallas_call(
        body, name="outproj_fwd", grid=(n // tm,),
        in_specs=[_row(tm, d), _row(tm, half), _row(tm, half), _fixed((1, half)), _fixed((1, half)), VMEM_WHOLE],
        out_specs=[_row(tm, d), _row(tm, 2 * half)],
        out_shape=[jax.ShapeDtypeStruct((n, d), F32), jax.ShapeDtypeStruct((n, 2 * half), BF16)],
        compiler_params=_params(("parallel",)),
    )(h, ya, yl, gao, glo, wout)


def outproj_bwd(dh, ya, yl, gao, glo, wout):
    n, d = dh.shape
    half = ya.shape[1]
    tm = _tile(n, 704)

    def body(dh_ref, ya_ref, yl_ref, gao_ref, glo_ref, w_ref, dya_ref, dyl_ref, dgao_ref, dglo_ref):
        d16 = dh_ref[...].astype(BF16)
        xa = ya_ref[...]
        xl = yl_ref[...]
        dxa, dga = _rms_bwd(xa, _rms_r(xa), gao_ref[...], _nt(d16, w_ref[:half, :]))
        dxl, dgl = _rms_bwd(xl, _rms_r(xl), glo_ref[...], _nt(d16, w_ref[half:, :]))
        dya_ref[...] = dxa
        dyl_ref[...] = dxl
        first = pl.program_id(0) == 0
        _accumulate(dgao_ref, dga, first)
        _accumulate(dglo_ref, dgl, first)

    return pl.pallas_call(
        body, name="outproj_bwd", grid=(n // tm,),
        in_specs=[_row(tm, d), _row(tm, half), _row(tm, half), _fixed((1, half)), _fixed((1, half)), VMEM_WHOLE],
        out_specs=[_row(tm, half), _row(tm, half), _fixed((1, half)), _fixed((1, half))],
        out_shape=[jax.ShapeDtypeStruct((n, half), F32), jax.ShapeDtypeStruct((n, half), F32),
                   jax.ShapeDtypeStruct((1, half), F32), jax.ShapeDtypeStruct((1, half), F32)],
        compiler_params=_params(("arbitrary",)),
    )(dh, ya, yl, gao, glo, wout)


def _loss_and_grad(x, gv, tgt, first_row):
    tm, d = x.shape
    r = _rms_r(x)
    row = first_row + lax.broadcasted_iota(jnp.int32, (tm, d), 0)
    diff = jnp.where(row >= FIRST_FRAME, x * r * gv - tgt, 0.0)
    part = 0.5 * jnp.sum(jnp.sum(diff * diff, axis=-1, keepdims=True) * (1.0 / d), axis=0, keepdims=True)
    dx, dg = _rms_bwd(x, r, gv, diff * (1.0 / d))
    return dx, part, dg


def assemble_cols(g, name):
    _, k, ns = g.shape

    def body(g_ref, o_ref):
        for j in range(N_DEV):
            o_ref[:, j * ns:(j + 1) * ns] = g_ref[j]

    return pl.pallas_call(body, name=name, out_shape=jax.ShapeDtypeStruct((k, N_DEV * ns), g.dtype),
                          compiler_params=_params(None))(g)


def split_cols(x, name):
    k, cols = x.shape
    ns = cols // N_DEV

    def body(x_ref, o_ref):
        for j in range(N_DEV):
            o_ref[j] = x_ref[:, j * ns:(j + 1) * ns]

    return pl.pallas_call(body, name=name, out_shape=jax.ShapeDtypeStruct((N_DEV, k, ns), x.dtype),
                          compiler_params=_params(None))(x)


def _slab_rows(w, per_head):
    k = w.shape[1]
    w = w.reshape(MLA_HEADS, per_head, k)
    return jnp.pad(w, ((0, 0), (0, HEAD_SLAB - per_head), (0, 0))).reshape(MLA_HEADS * HEAD_SLAB, k)


def _unslab_rows(w, per_head):
    k = w.shape[1]
    return w.reshape(MLA_HEADS, HEAD_SLAB, k)[:, :per_head].reshape(MLA_HEADS * per_head, k)


def meta_grad(dh0, nb, lp):
    d = dh0.shape[1]
    ns = d // N_DEV
    per_seq = lp // N_META

    def body(x_ref, o_ref):
        x = x_ref[...]
        for j in range(N_DEV):
            _accumulate(o_ref.at[j], x[:, j * ns:(j + 1) * ns], pl.program_id(0) == 0)

    return pl.pallas_call(
        body, name="meta_grad", grid=(nb,),
        in_specs=[pl.BlockSpec((N_META, d), lambda b: (b * per_seq + PAD // N_META, 0))],
        out_specs=pl.BlockSpec((N_DEV, N_META, ns), lambda b: (0, 0, 0)),
        out_shape=jax.ShapeDtypeStruct((N_DEV, N_META, ns), F32),
        compiler_params=_params(("arbitrary",)))(dh0)


VECTORS = [("ffn1_norm", 1024), ("mix_norm", 1024), ("q_latent_norm", 384), ("kv_latent_norm", 256),
           ("q_head_norm", 192), ("k_head_norm", 192), ("conv_b", 512), ("gate_a_b", 512), ("gate_x_b", 512),
           ("lru_lambda", 512), ("attn_out_norm", 512), ("lru_out_norm", 512), ("ffn2_norm", 1024),
           ("final_norm", 1024)]
VEC_ROWS = 16
LOSS_ROW = len(VECTORS)
GATES = ["gate_a_w", "gate_x_w"]


def pack_vectors(grads, loss):
    def body(*refs):
        o_ref = refs[-1]
        o_ref[...] = jnp.zeros_like(o_ref)
        for t, (ref, (_, cnt)) in enumerate(zip(refs[:-2], VECTORS)):
            o_ref[t:t + 1, :cnt] = ref[:, :cnt]
        o_ref[LOSS_ROW:LOSS_ROW + 1, :LANES] = refs[-2][...]

    return pl.pallas_call(body, name="pack_vectors", out_shape=jax.ShapeDtypeStruct((VEC_ROWS, D_MODEL), F32),
                          compiler_params=_params(None))(*[grads[name] for name, _ in VECTORS], loss)


def _adamw_update(w, g, m, v):
    c1 = 1.0 / (1.0 - ADAM_B1 ** ADAM_STEP)
    c2 = 1.0 / (1.0 - ADAM_B2 ** ADAM_STEP)
    mn = ADAM_B1 * m + (1.0 - ADAM_B1) * g
    vn = ADAM_B2 * v + (1.0 - ADAM_B2) * (g * g)
    delta = -ADAM_LR * ((mn * c1) / (jnp.sqrt(vn * c2) + ADAM_EPS) + ADAM_WD * w)
    return delta, mn, vn


def _sum_slots(ref, index=()):
    acc = ref[(0,) + index].astype(F32)
    for s in range(1, N_DEV):
        acc = acc + ref[(s,) + index].astype(F32)
    return acc


def adamw_sharded(r, w, m, v, name):
    rows, cols = w.shape
    tr = _tile(rows, 256, 16) if rows % 16 == 0 else rows

    def body(r_ref, w_ref, m_ref, v_ref, g_ref, d_ref, mo_ref, vo_ref):
        g = _sum_slots(r_ref)
        g_ref[...] = g
        d_ref[...], mo_ref[...], vo_ref[...] = _adamw_update(w_ref[...], g, m_ref[...], v_ref[...])

    spec = pl.BlockSpec((tr, cols), lambda i: (i, 0))
    shape = jax.ShapeDtypeStruct((rows, cols), F32)
    return pl.pallas_call(
        body, name=name, grid=(rows // tr,),
        in_specs=[pl.BlockSpec((N_DEV, tr, cols), lambda i: (0, i, 0))] + [spec] * 3,
        out_specs=[spec] * 4, out_shape=[shape] * 4,
        compiler_params=_params(("parallel",)),
    )(r, w, m, v)


def adamw_small(r_vec, r_gates, w, m, v):
    nt = len(VECTORS) + len(GATES)

    def body(*refs):
        rv_ref = refs[0]
        rg_refs = refs[1:1 + len(GATES)]
        base = 1 + len(GATES)
        w_refs, m_refs, v_refs = (refs[base + i * nt:base + (i + 1) * nt] for i in range(3))
        outs = refs[base + 3 * nt:]
        g_o, d_o, m_o, v_o = (outs[i * nt:(i + 1) * nt] for i in range(4))
        outs[4 * nt][...] = _sum_slots(rv_ref, (slice(LOSS_ROW, LOSS_ROW + 1), slice(0, LANES)))
        for t in range(nt):
            if t < len(VECTORS):
                cnt = VECTORS[t][1]
                g = _sum_slots(rv_ref, (slice(t, t + 1), slice(0, cnt)))
            else:
                g = _sum_slots(rg_refs[t - len(VECTORS)])
            g_o[t][...] = g
            d_o[t][...], m_o[t][...], v_o[t][...] = _adamw_update(w_refs[t][...], g, m_refs[t][...], v_refs[t][...])

    shapes = [jax.ShapeDtypeStruct(a.shape, F32) for a in w]
    res = pl.pallas_call(body, name="adamw_small", out_shape=shapes * 4 + [jax.ShapeDtypeStruct((1, LANES), F32)],
                         compiler_params=_params(None))(r_vec, *r_gates, *w, *m, *v)
    return [res[i * nt:(i + 1) * nt] for i in range(4)], res[4 * nt]


def _block_diag(w):
    nb, n, _ = w.shape
    eye = jnp.eye(nb, dtype=w.dtype)
    return (eye[:, None, :, None] * w[:, :, None, :]).reshape(nb * n, nb * n)


def _two_d(a):
    if a.ndim == 3:
        return a.reshape(a.shape[1], a.shape[2])
    if a.ndim == 4:
        return a.reshape(a.shape[1] * a.shape[2], a.shape[3])
    return a


_WEIGHT_NAMES = ['meta_tokens', 'ffn1_norm', 'ffn1_w_gate', 'ffn1_w_up', 'ffn1_w_down', 'mix_norm', 'w_in',
                 'q_latent_norm', 'w_uq', 'kv_latent_norm', 'w_uk', 'w_uv', 'q_head_norm', 'k_head_norm', 'conv_w',
                 'conv_b', 'gate_a_w', 'gate_a_b', 'gate_x_w', 'gate_x_b', 'lru_lambda', 'attn_out_norm',
                 'lru_out_norm', 'w_out', 'ffn2_norm', 'ffn2_w_gate', 'ffn2_w_up', 'ffn2_w_down', 'final_norm']


COLUMN_SHARDED = ("ffn1_w_gate", "ffn1_w_up", "ffn2_w_gate", "ffn2_w_up", "w_in", "w_uq", "w_uk", "w_uv")


def train_step(x, tgt, w, m, v):
    nb, seq, d = x.shape
    lp = PAD + N_META + seq
    n = nb * lp

    def local(a, name):
        a = _two_d(a)
        return a.T if name in COLUMN_SHARDED else a

    sh = {name: local(w[name], name) for name in _WEIGHT_NAMES}
    m2 = {name: local(m[name], name) for name in _WEIGHT_NAMES}
    v2 = {name: local(v[name], name) for name in _WEIGHT_NAMES}

    def b16(name):
        return sh[name].astype(BF16)

    out = {}

    def update(name, landed):
        out[name] = adamw_sharded(landed, sh[name], m2[name], v2[name], "adamw_" + name)

    g_wg1, g_wu1, g_meta, g_conv = exchange(
        [b16("ffn1_w_gate"), b16("ffn1_w_up"), sh["meta_tokens"], sh["conv_w"]], ["gather"] * 4, "gather_ffn1")
    wg1, wu1 = g_wg1.reshape(D_FF, d), g_wu1.reshape(D_FF, d)
    meta = assemble_cols(g_meta, "assemble_meta")
    conv_w = assemble_cols(g_conv, "assemble_conv")

    front = jnp.concatenate([jnp.zeros((PAD, d), F32), meta], axis=0)
    h0 = jnp.concatenate([jnp.broadcast_to(front[None], (nb, FIRST_FRAME, d)), x], axis=1).reshape(n, d)
    tgt_p = jnp.concatenate([jnp.zeros((nb, FIRST_FRAME, d), F32), tgt], axis=1).reshape(n, d)
    tables = _rope_tables(lp)
    zero_tail = jnp.zeros((1, HEAD_SLAB - D_QK), F32)
    gqh = jnp.concatenate([sh["q_head_norm"], zero_tail], axis=1)
    gkh = jnp.concatenate([sh["k_head_norm"], zero_tail], axis=1)
    wa = _block_diag(w["gate_a_w"][0]).astype(BF16)
    wx = _block_diag(w["gate_x_w"][0]).astype(BF16)

    (u1, a1, b1, s1), (g_wd1, g_in) = ffn_up(h0, sh["ffn1_norm"], wg1, wu1, "ffn1_up",
                                             comm=([b16("ffn1_w_down"), b16("w_in")], ["gather"] * 2))
    wd1 = g_wd1.reshape(D_FF, d)
    mla_rows = MLA_IN - D_ROPE
    w_in = g_in.reshape(mla_rows + 2 * LRU_WIDTH, d)
    wm = jnp.concatenate([w_in[:mla_rows], jnp.zeros((D_ROPE, d), BF16)], axis=0)
    wl = w_in[mla_rows:]
    (h1, u2, zm, zl), (g_uq, g_uk, g_uv, g_out) = ffn_down_inproj(
        h0, s1, wd1, sh["mix_norm"], wm, wl, "ffn1_down_inproj",
        comm=([b16("w_uq"), b16("w_uk"), b16("w_uv"), b16("w_out")], ["gather"] * 4))
    wuq = _slab_rows(g_uq.reshape(MLA_HEADS * D_QK, Q_RANK), D_QK)
    wuk = _slab_rows(g_uk.reshape(MLA_HEADS * D_NOPE, KV_RANK), D_NOPE)
    wuv = g_uv.reshape(MLA_HEADS * D_V, KV_RANK)
    w_out = g_out.reshape(d, d)

    q, k, vv, qn, cn = mla_prep_fwd(zm, sh["q_latent_norm"], sh["kv_latent_norm"], wuq, wuk, wuv, gqh, gkh, tables, lp)
    (y_mla, lse), (g_wu2, g_wd2) = attn_fwd(
        q, k, vv, nb, lp, comm=([b16("ffn2_w_up"), b16("ffn2_w_down")], ["gather"] * 2))
    (y_lru, hs), (g_wg2,) = lru_fwd(zl, conv_w, sh["conv_b"], wa, wx, sh["gate_a_b"], sh["gate_x_b"], sh["lru_lambda"],
                                    nb, lp, comm=([b16("ffn2_w_gate")], ["gather"]))
    wg2, wu2, wd2 = (g.reshape(D_FF, d) for g in (g_wg2, g_wu2, g_wd2))
    h2, yn = outproj_fwd(h1, y_mla, y_lru, sh["attn_out_norm"], sh["lru_out_norm"], w_out)
    dh3, u3, a3, b3, loss, g_final = ffn_fwd_loss(h2, sh["ffn2_norm"], wg2, wu2, wd2, sh["final_norm"], tgt_p, lp,
                                                  "ffn2_fwd_loss")

    vec = {"final_norm": g_final}
    (dh2, da3, db3, sh3, vec["ffn2_norm"]), _ = ffn_bwd_act(dh3, h2, sh["ffn2_norm"], a3, b3, wg2, wu2, wd2, "ffn2_bwd")
    ff_shards = (N_DEV, D_FF // N_DEV, d)
    dwg2 = tn_matmul(da3, u3, "ffn2_dwg", "bf16").reshape(ff_shards)
    dwu2 = tn_matmul(db3, u3, "ffn2_dwu", "bf16").reshape(ff_shards)
    dwd2 = tn_matmul(sh3, dh3, "ffn2_dwd", "bf16").reshape(ff_shards)

    dy_mla, dy_lru, vec["attn_out_norm"], vec["lru_out_norm"] = outproj_bwd(
        dh2, y_mla, y_lru, sh["attn_out_norm"], sh["lru_out_norm"], w_out)
    dw_out = tn_matmul(yn, dh2, "dw_out", "bf16").reshape(N_DEV, d // N_DEV, d)
    (du, dgate, dconv, vec["conv_b"], vec["gate_a_b"], vec["gate_x_b"], vec["lru_lambda"], dga, dgx), landed = lru_bwd(
        zl, hs, dy_lru, conv_w, sh["conv_b"], wa, wx, sh["gate_a_b"], sh["gate_x_b"], sh["lru_lambda"], nb, lp,
        comm=([dwg2, dw_out], ["scatter"] * 2))
    update("ffn2_w_gate", landed[0])
    update("w_out", landed[1])

    (dq, dk, dv), (r_wu2,) = attn_bwd(q, k, vv, y_mla, dy_mla, lse, nb, lp, comm=([dwu2], ["scatter"]))
    update("ffn2_w_up", r_wu2)

    (dzm, dqr, dkr, vec["q_latent_norm"], vec["kv_latent_norm"], vec["q_head_norm"], vec["k_head_norm"]), (r_wd2,) = (
        mla_prep_bwd(dq, dk, dv, zm, qn, cn, sh["q_latent_norm"], sh["kv_latent_norm"], wuq, wuk, wuv, gqh, gkh,
                     tables, lp, comm=([dwd2], ["scatter"])))
    update("ffn2_w_down", r_wd2)
    dwuq = _unslab_rows(tn_matmul(dqr, qn, "dw_uq", "bf16"), D_QK).reshape(N_DEV, -1, Q_RANK)
    dwuk = _unslab_rows(tn_matmul(dkr, cn, "dw_uk", "bf16"), D_NOPE).reshape(N_DEV, -1, KV_RANK)
    dwuv = tn_matmul(dv, cn, "dw_uv", "bf16").reshape(N_DEV, -1, KV_RANK)
    (dh1, vec["mix_norm"]), landed = inproj_bwd(dzm, du, dgate, dh2, h1, sh["mix_norm"], wm, wl,
                                                comm=([dwuq, dwuk, dwuv], ["scatter"] * 3))
    for name, r in zip(("w_uq", "w_uk", "w_uv"), landed):
        update(name, r)
    dw_in = jnp.concatenate([tn_matmul(dzm, u2, "dw_in_mla", "bf16")[:mla_rows], tn_matmul(du, u2, "dw_in_u", "bf16"),
                             tn_matmul(dgate, u2, "dw_in_gate", "bf16")], axis=0).reshape(N_DEV, -1, d)

    dwd1 = tn_matmul(s1, dh1, "ffn1_dwd", "bf16").reshape(ff_shards)
    (dh0, da1, db1, vec["ffn1_norm"]), landed = ffn_bwd_act(
        dh1, h0, sh["ffn1_norm"], a1, b1, wg1, wu1, wd1, "ffn1_bwd", emit_sh=False,
        comm=([dw_in, split_cols(dconv, "split_conv"), dwd1], ["scatter"] * 3))
    for name, r in zip(("w_in", "conv_w", "ffn1_w_down"), landed):
        update(name, r)

    dwg1 = tn_matmul(da1, u1, "ffn1_dwg", "bf16").reshape(ff_shards)
    dwu1, (r_wg1,) = tn_matmul(db1, u1, "ffn1_dwu", "bf16", comm=([dwg1], ["scatter"]))
    dmeta = meta_grad(dh0, nb, lp)
    gates = [dga.reshape(LRU_WIDTH, LRU_BLOCK), dgx.reshape(LRU_WIDTH, LRU_BLOCK)]
    r_wu1, r_meta, r_vec, r_ga, r_gx = exchange(
        [dwu1.reshape(ff_shards), dmeta, pack_vectors(vec, loss)] + gates, ["scatter"] * 2 + ["gather"] * 3,
        "exchange_last")
    update("ffn1_w_gate", r_wg1)
    update("ffn1_w_up", r_wu1)
    update("meta_tokens", r_meta)

    small = [name for name, _ in VECTORS] + GATES
    res, total_loss = adamw_small(r_vec, [r_ga, r_gx], [sh[nm] for nm in small], [m2[nm] for nm in small],
                                  [v2[nm] for nm in small])
    for i, name in enumerate(small):
        out[name] = [res[j][i] for j in range(4)]

    grad_x = dh0.reshape(nb, lp, d)[:, FIRST_FRAME:]
    loss = total_loss[0, 0]

    def as_given(a, name):
        return (a.T if name in COLUMN_SHARDED else a).reshape(w[name].shape)

    cols = [[as_given(out[name][j], name) for name in _WEIGHT_NAMES] for j in range(4)]
    return (loss, grad_x, *cols[0], *cols[1], *cols[2], *cols[3])


def kernel(x, meta_tokens, ffn1_norm, ffn1_w_gate, ffn1_w_up, ffn1_w_down, mix_norm, w_in, q_latent_norm, w_uq, kv_latent_norm, w_uk, w_uv, q_head_norm, k_head_norm, conv_w, conv_b, gate_a_w, gate_a_b, gate_x_w, gate_x_b, lru_lambda, attn_out_norm, lru_out_norm, w_out, ffn2_norm, ffn2_w_gate, ffn2_w_up, ffn2_w_down, final_norm, loss_target, m_meta_tokens, m_ffn1_norm, m_ffn1_w_gate, m_ffn1_w_up, m_ffn1_w_down, m_mix_norm, m_w_in, m_q_latent_norm, m_w_uq, m_kv_latent_norm, m_w_uk, m_w_uv, m_q_head_norm, m_k_head_norm, m_conv_w, m_conv_b, m_gate_a_w, m_gate_a_b, m_gate_x_w, m_gate_x_b, m_lru_lambda, m_attn_out_norm, m_lru_out_norm, m_w_out, m_ffn2_norm, m_ffn2_w_gate, m_ffn2_w_up, m_ffn2_w_down, m_final_norm, v_meta_tokens, v_ffn1_norm, v_ffn1_w_gate, v_ffn1_w_up, v_ffn1_w_down, v_mix_norm, v_w_in, v_q_latent_norm, v_w_uq, v_kv_latent_norm, v_w_uk, v_w_uv, v_q_head_norm, v_k_head_norm, v_conv_w, v_conv_b, v_gate_a_w, v_gate_a_b, v_gate_x_w, v_gate_x_b, v_lru_lambda, v_attn_out_norm, v_lru_out_norm, v_w_out, v_ffn2_norm, v_ffn2_w_gate, v_ffn2_w_up, v_ffn2_w_down, v_final_norm):
    args = locals()
    w = {name: args[name] for name in _WEIGHT_NAMES}
    m = {name: args["m_" + name] for name in _WEIGHT_NAMES}
    v = {name: args["v_" + name] for name in _WEIGHT_NAMES}
    return train_step(x, loss_target, w, m, v)
```

```python
import math

import jax
import jax.numpy as jnp
from jax import lax
from jax.experimental import pallas as pl
from jax.experimental.pallas import tpu as pltpu

F32 = jnp.float32
BF16 = jnp.bfloat16

D_MODEL = 1024
CHUNK = 64
CHUNK_SHIFT = 6
N_META = 16
PAD = CHUNK - N_META
FIRST_FRAME = PAD + N_META
MLA_HEADS = 4
D_NOPE = 128
D_ROPE = 64
D_QK = D_NOPE + D_ROPE
D_V = 128
HEAD_SLAB = 256
KV_RANK = 256
Q_RANK = 384
ROPE_THETA = 10000.0
LRU_WIDTH = 512
LRU_BLOCKS = 8
LRU_BLOCK = 64
LRU_TILE = 128
CONV_W = 4
C_RGLRU = 8.0
D_FF = 2816
MLA_IN = 768
EPS = 1e-6
NEG_INF = -1e30
N_DEV = 8
LANES = 128
VMEM_LIMIT = 52 * 1024 * 1024
ATTN_HEADS_PER_STEP = 2
TN_ROWS = 4224
TN_X_BYTES = 12 * 1024 * 1024
TN_Y_BYTES = 9 * 1024 * 1024 // 2

ADAM_LR = 0.001
ADAM_B1 = 0.9
ADAM_B2 = 0.999
ADAM_EPS = 1e-08
ADAM_WD = 0.01
ADAM_STEP = 10

VMEM_WHOLE = pl.BlockSpec(memory_space=pltpu.VMEM)
HBM_WHOLE = pl.BlockSpec(memory_space=pl.ANY)


def _params(sems):
    if sems is None:
        return pltpu.CompilerParams(vmem_limit_bytes=VMEM_LIMIT)
    return pltpu.CompilerParams(dimension_semantics=sems, vmem_limit_bytes=VMEM_LIMIT)


def _tile(n, cap, mult=16):
    best = None
    for t in range(mult, min(n, cap) + 1, mult):
        if n % t == 0:
            best = t
    assert best is not None, (n, cap, mult)
    return best


def _row(tm, d):
    return pl.BlockSpec((tm, d), lambda i: (i, 0))


def _fixed(shape):
    return pl.BlockSpec(shape, lambda i: (0,) * len(shape))


def _mesh_position():
    return lax.axis_index("x"), lax.axis_index("y"), lax.axis_index("c")


def _flat_index(x, y, c):
    return 4 * x + 2 * y + c


def _peers(x, y, c):
    out = []
    for k in range(1, N_DEV):
        fx, fy, fc = (k >> 2) & 1, (k >> 1) & 1, k & 1
        out.append((1 - x if fx else x, 1 - y if fy else y, 1 - c if fc else c))
    return out


def _comm_out_shapes(srcs, modes):
    return [jax.ShapeDtypeStruct((N_DEV,) + s.shape if md == "gather" else s.shape, s.dtype)
            for s, md in zip(srcs, modes)]


def _comm_scratch(n):
    per_peer = n * (N_DEV - 1)
    return [pltpu.SemaphoreType.DMA((per_peer,)), pltpu.SemaphoreType.DMA((per_peer,)), pltpu.SemaphoreType.DMA((n,))]


class _Copies:
    def __init__(self, own, first, relay):
        self.own, self.first, self.relay = own, first, relay

    def start(self):
        for cp in self.own + self.first:
            cp.start()

    def forward(self):
        for arrival, onward in self.relay:
            arrival.wait_recv()
            onward.start()

    def finish(self):
        arrivals = [a for a, _ in self.relay]
        onward = [f for _, f in self.relay]
        for cp in self.first + onward:
            if not any(cp is a for a in arrivals):
                cp.wait_recv()
        for cp in self.first + onward:
            cp.wait_send()
        for cp in self.own:
            cp.wait()


def _comm_copies(src_refs, dst_refs, modes, send, recv, local):
    x, y, c = _mesh_position()
    me = _flat_index(x, y, c)
    n = len(modes)
    sibling = (x, y, 1 - c)
    chips = [(1 - x, y), (x, 1 - y), (1 - x, 1 - y)]

    def remote(src, dst, k, t, to):
        return pltpu.make_async_remote_copy(src_ref=src, dst_ref=dst, send_sem=send.at[k * n + t],
                                            recv_sem=recv.at[k * n + t], device_id=to,
                                            device_id_type=pl.DeviceIdType.MESH)

    own, first, relay = [], [], []
    for t, (src, dst, md) in enumerate(zip(src_refs, dst_refs, modes)):
        if md == "scatter":
            own.append(pltpu.make_async_copy(src.at[me], dst.at[me], local.at[t]))
            for k, peer in enumerate(_peers(x, y, c)):
                first.append(remote(src.at[_flat_index(*peer)], dst.at[me], k, t, peer))
        else:
            own.append(pltpu.make_async_copy(src, dst.at[me], local.at[t]))
            first.append(remote(src, dst.at[me], 0, t, sibling))
            for j, chip in enumerate(chips):
                arrival = remote(src, dst.at[me], 1 + j, t, (*chip, c))
                landed = dst.at[_flat_index(*chip, c)]
                first.append(arrival)
                relay.append((arrival, remote(landed, landed, 4 + j, t, sibling)))
    return _Copies(own, first, relay)


def _hosted(body, n_in, n_out, modes, grid):
    t = len(modes)
    total = math.prod(grid)

    def wrapped(*refs):
        ins, csrc = refs[:n_in], refs[n_in:n_in + t]
        outs = refs[n_in + t:n_in + t + n_out]
        cdst = refs[n_in + t + n_out:n_in + 2 * t + n_out]
        scratch = refs[n_in + 2 * t + n_out:-3]
        copies = _comm_copies(csrc, cdst, modes, *refs[-3:])
        step = pl.program_id(0)
        for axis in range(1, len(grid)):
            step = step * grid[axis] + pl.program_id(axis)

        @pl.when(step == 0)
        def _():
            copies.start()

        body(*ins, *outs, *scratch)

        @pl.when(step == (total * 3) // 5)
        def _():
            copies.forward()

        @pl.when(step == total - 1)
        def _():
            copies.finish()

    return wrapped


def _call(body, name, grid, in_specs, out_specs, out_shape, sems, args, scratch=(), comm=None):
    if comm is None:
        outs = pl.pallas_call(body, name=name, grid=grid, in_specs=in_specs, out_specs=out_specs, out_shape=out_shape,
                              scratch_shapes=list(scratch), compiler_params=_params(sems))(*args)
        return outs, []
    srcs, modes = comm
    n = len(modes)
    res = pl.pallas_call(
        _hosted(body, len(in_specs), len(out_specs), modes, grid), name=name, grid=grid,
        in_specs=list(in_specs) + [HBM_WHOLE] * n, out_specs=list(out_specs) + [HBM_WHOLE] * n,
        out_shape=list(out_shape) + _comm_out_shapes(srcs, modes),
        scratch_shapes=list(scratch) + _comm_scratch(n),
        compiler_params=_params(("arbitrary",) * len(grid)))(*args, *srcs)
    return res[:len(out_specs)], res[len(out_specs):]


def exchange(srcs, modes, name):
    n = len(modes)

    def body(*refs):
        copies = _comm_copies(refs[:n], refs[n:2 * n], modes, *refs[2 * n:])
        copies.start()
        copies.forward()
        copies.finish()

    return pl.pallas_call(body, name=name, in_specs=[HBM_WHOLE] * n, out_specs=[HBM_WHOLE] * n,
                          out_shape=_comm_out_shapes(srcs, modes), scratch_shapes=_comm_scratch(n))(*srcs)


def _nn(a, b):
    return jnp.dot(a, b, preferred_element_type=F32)


def _nt(a, b):
    return lax.dot_general(a, b, (((1,), (1,)), ((), ())), preferred_element_type=F32)


def _tn(a, b):
    return lax.dot_general(a, b, (((0,), (0,)), ((), ())), preferred_element_type=F32)


def _sig(x):
    return 1.0 / (1.0 + jnp.exp(-x))


def _rms_r(x, n=None):
    n = x.shape[-1] if n is None else n
    return lax.rsqrt(jnp.sum(x * x, axis=-1, keepdims=True) * (1.0 / n) + EPS)


def _rms_bwd(x, r, g, dy, n=None):
    n = x.shape[-1] if n is None else n
    xhat = x * r
    dxhat = dy * g
    dx = r * (dxhat - xhat * (jnp.sum(dxhat * xhat, axis=-1, keepdims=True) * (1.0 / n)))
    return dx, jnp.sum(dy * xhat, axis=0, keepdims=True)


def _accumulate(ref, val, first):
    @pl.when(first)
    def _():
        ref[...] = val

    @pl.when(jnp.logical_not(first))
    def _():
        ref[...] += val


_GELU_C = math.sqrt(2.0 / math.pi)


def _gelu_and_grad(x):
    inner = _GELU_C * (x + 0.044715 * x * x * x)
    t = jnp.tanh(inner)
    gelu = 0.5 * x * (1.0 + t)
    dgelu = 0.5 * (1.0 + t) + 0.5 * x * (1.0 - t * t) * _GELU_C * (1.0 + 3.0 * 0.044715 * x * x)
    return gelu, dgelu


def _log1p_small(t):
    return jnp.where(t < 1e-3, t * (1.0 - t * (0.5 - t * (1.0 / 3.0))), jnp.log(1.0 + t))


def _softplus(x):
    return jnp.maximum(x, 0.0) + _log1p_small(jnp.exp(-jnp.abs(x)))


def _sig_tanh(x):
    return 0.5 + 0.5 * jnp.tanh(0.5 * x)


def _ff_chunks(f):
    return 2 if (f // 2) % LANES == 0 else 1


def _swiglu_half(x, g_ref, wg_ref, wu_ref, wd_ref, a_ref, b_ref, fc):
    f = wg_ref.shape[0]
    u = (x * _rms_r(x) * g_ref[...]).astype(BF16)
    acc = jnp.zeros(x.shape, F32)
    for c in range(f // fc):
        cols = slice(c * fc, (c + 1) * fc)
        a = _nt(u, wg_ref[cols, :])
        b = _nt(u, wu_ref[cols, :])
        s = (a * _sig(a) * b).astype(BF16)
        acc = acc + _nn(s, wd_ref[cols, :])
        a_ref[:, cols] = a.astype(BF16)
        b_ref[:, cols] = b.astype(BF16)
    return x + 0.5 * acc, u


def ffn_up(h, g, wg, wu, name, comm=None):
    n, d = h.shape
    f = wg.shape[0]
    tm = _tile(n, 528)
    fc = 2 * LANES if f % (2 * LANES) == 0 else f

    def body(h_ref, g_ref, wg_ref, wu_ref, u_ref, a_ref, b_ref, s_ref):
        x = h_ref[...]
        u = (x * _rms_r(x) * g_ref[...]).astype(BF16)
        u_ref[...] = u
        for c in range(f // fc):
            cols = slice(c * fc, (c + 1) * fc)
            a = _nt(u, wg_ref[cols, :])
            b = _nt(u, wu_ref[cols, :])
            a_ref[:, cols] = a.astype(BF16)
            b_ref[:, cols] = b.astype(BF16)
            s_ref[:, cols] = (0.5 * (a * _sig(a) * b)).astype(BF16)

    wide = jax.ShapeDtypeStruct((n, f), BF16)
    return _call(
        body, name, (n // tm,),
        [_row(tm, d), _fixed((1, d)), VMEM_WHOLE, VMEM_WHOLE],
        [_row(tm, d), _row(tm, f), _row(tm, f), _row(tm, f)],
        [jax.ShapeDtypeStruct((n, d), BF16), wide, wide, wide],
        ("parallel",), (h, g, wg, wu), comm=comm)


def ffn_down_inproj(h, s, wd, g, wm, wl, name, comm=None):
    n, d = h.shape
    f = wd.shape[0]
    tm = _tile(n, 528)

    def body(h_ref, s_ref, wd_ref, g_ref, wm_ref, wl_ref, ho_ref, u_ref, zm_ref, zl_ref):
        x = h_ref[...] + _nn(s_ref[...], wd_ref[...])
        ho_ref[...] = x
        u = (x * _rms_r(x) * g_ref[...]).astype(BF16)
        u_ref[...] = u
        zm_ref[...] = _nt(u, wm_ref[...])
        zl_ref[...] = _nt(u, wl_ref[...])

    return _call(
        body, name, (n // tm,),
        [_row(tm, d), _row(tm, f), VMEM_WHOLE, _fixed((1, d)), VMEM_WHOLE, VMEM_WHOLE],
        [_row(tm, d), _row(tm, d), _row(tm, MLA_IN), _row(tm, 2 * LRU_WIDTH)],
        [jax.ShapeDtypeStruct((n, d), F32), jax.ShapeDtypeStruct((n, d), BF16),
         jax.ShapeDtypeStruct((n, MLA_IN), F32), jax.ShapeDtypeStruct((n, 2 * LRU_WIDTH), F32)],
        ("parallel",), (h, s, wd, g, wm, wl), comm=comm)


def ffn_fwd_loss(h, g, wg, wu, wd, g_final, tgt, lp, name):
    n, d = h.shape
    f = wg.shape[0]
    tm = _tile(lp, 528)
    per_seq = lp // tm
    fc = 2 * LANES if f % (2 * LANES) == 0 else f

    def body(h_ref, g_ref, wg_ref, wu_ref, wd_ref, gf_ref, t_ref, dh_ref, u_ref, a_ref, b_ref, loss_ref, dgf_ref):
        i = pl.program_id(0)
        y, u_ref[...] = _swiglu_half(h_ref[...], g_ref, wg_ref, wu_ref, wd_ref, a_ref, b_ref, fc)
        dh_ref[...], part, dg = _loss_and_grad(y, gf_ref[...], t_ref[...], (i % per_seq) * tm)
        _accumulate(loss_ref, jnp.broadcast_to(part, (1, LANES)), i == 0)
        _accumulate(dgf_ref, dg, i == 0)

    outs, _ = _call(
        body, name, (n // tm,),
        [_row(tm, d), _fixed((1, d)), VMEM_WHOLE, VMEM_WHOLE, VMEM_WHOLE, _fixed((1, d)), _row(tm, d)],
        [_row(tm, d), _row(tm, d), _row(tm, f), _row(tm, f), _fixed((1, LANES)), _fixed((1, d))],
        [jax.ShapeDtypeStruct((n, d), F32), jax.ShapeDtypeStruct((n, d), BF16),
         jax.ShapeDtypeStruct((n, f), BF16), jax.ShapeDtypeStruct((n, f), BF16),
         jax.ShapeDtypeStruct((1, LANES), F32), jax.ShapeDtypeStruct((1, d), F32)],
        ("arbitrary",), (h, g, wg, wu, wd, g_final, tgt))
    return outs


def ffn_bwd_act(dh, h, g, a, b, wg, wu, wd, name, comm=None, emit_sh=True):
    n, d = h.shape
    f = wg.shape[0]
    tm = _tile(n, 352 if emit_sh else 384)
    nc = _ff_chunks(f)
    fc = f // nc

    def body(dh_ref, h_ref, g_ref, a_ref, b_ref, wg_ref, wu_ref, wd_ref, dhi_ref, da_ref, db_ref, *rest):
        dg_ref = rest[-1]
        x = h_ref[...]
        dy = dh_ref[...]
        r = _rms_r(x)
        dhh = (0.5 * dy).astype(BF16)
        du = jnp.zeros((tm, d), F32)
        for c in range(nc):
            cols = slice(c * fc, (c + 1) * fc)
            ds = _nt(dhh, wd_ref[cols, :])
            av = a_ref[:, cols].astype(F32)
            bv = b_ref[:, cols].astype(F32)
            sg = _sig(av)
            sil = av * sg
            da = (ds * bv * (sg * (1.0 + av * (1.0 - sg)))).astype(BF16)
            db = (ds * sil).astype(BF16)
            da_ref[:, cols] = da
            db_ref[:, cols] = db
            if emit_sh:
                rest[0][:, cols] = (0.5 * sil * bv).astype(BF16)
            du = du + _nn(da, wg_ref[cols, :]) + _nn(db, wu_ref[cols, :])
        dx, dg = _rms_bwd(x, r, g_ref[...], du)
        dhi_ref[...] = dy + dx
        _accumulate(dg_ref, dg, pl.program_id(0) == 0)

    wide = [jax.ShapeDtypeStruct((n, f), BF16)] * (3 if emit_sh else 2)
    return _call(
        body, name, (n // tm,),
        [_row(tm, d), _row(tm, d), _fixed((1, d)), _row(tm, f), _row(tm, f), VMEM_WHOLE, VMEM_WHOLE, VMEM_WHOLE],
        [_row(tm, d)] + [_row(tm, f)] * len(wide) + [_fixed((1, d))],
        [jax.ShapeDtypeStruct((n, d), F32)] + wide + [jax.ShapeDtypeStruct((1, d), F32)],
        ("arbitrary",), (dh, h, g, a, b, wg, wu, wd), comm=comm)


def tn_matmul(x, y, name, out="f32", comm=None):
    n, k = x.shape
    m = y.shape[1]
    tm = _tile(n, TN_ROWS)
    kc, mc = k, (512 if m % 512 == 0 else m)
    while tm * kc * x.dtype.itemsize > TN_X_BYTES and kc % (2 * LANES) == 0:
        kc //= 2
    while tm * mc * y.dtype.itemsize > TN_Y_BYTES and mc % (2 * LANES) == 0:
        mc //= 2
    steps = n // tm

    def body(x_ref, y_ref, o_ref, *acc):
        i = pl.program_id(2)
        part = _tn(x_ref[...].astype(BF16), y_ref[...].astype(BF16))
        if steps == 1:
            o_ref[...] = part.astype(o_ref.dtype)
        elif out == "f32":
            _accumulate(o_ref, part, i == 0)
        else:
            _accumulate(acc[0], part, i == 0)

            @pl.when(i == steps - 1)
            def _():
                o_ref[...] = acc[0][...].astype(BF16)

    out_shape = jax.ShapeDtypeStruct((k, m), F32 if out == "f32" else BF16)
    (res,), landed = _call(
        body, name, (k // kc, m // mc, steps),
        [pl.BlockSpec((tm, kc), lambda a, b, i: (i, a)), pl.BlockSpec((tm, mc), lambda a, b, i: (i, b))],
        [pl.BlockSpec((kc, mc), lambda a, b, i: (a, b))], [out_shape], ("parallel", "parallel", "arbitrary"), (x, y),
        scratch=[pltpu.VMEM((kc, mc), F32)] if (out == "bf16" and steps > 1) else [], comm=comm)
    return (res, landed) if comm is not None else res


def inproj_bwd(dzm, du, dgate, dh2, h, g, wm, wl, comm=None):
    n, d = h.shape
    tm = _tile(n, 352)

    def body(dzm_ref, du_ref, dgt_ref, dh2_ref, h_ref, g_ref, wm_ref, wl_ref, dh_ref, dg_ref):
        x = h_ref[...]
        dun = (_nn(dzm_ref[...].astype(BF16), wm_ref[...])
               + _nn(du_ref[...].astype(BF16), wl_ref[:LRU_WIDTH, :])
               + _nn(dgt_ref[...].astype(BF16), wl_ref[LRU_WIDTH:, :]))
        dx, dg = _rms_bwd(x, _rms_r(x), g_ref[...], dun)
        dh_ref[...] = dh2_ref[...] + dx
        _accumulate(dg_ref, dg, pl.program_id(0) == 0)

    return _call(
        body, "inproj_bwd", (n // tm,),
        [_row(tm, MLA_IN), _row(tm, LRU_WIDTH), _row(tm, LRU_WIDTH), _row(tm, d), _row(tm, d),
         _fixed((1, d)), VMEM_WHOLE, VMEM_WHOLE],
        [_row(tm, d), _fixed((1, d))],
        [jax.ShapeDtypeStruct((n, d), F32), jax.ShapeDtypeStruct((1, d), F32)],
        ("arbitrary",), (dzm, du, dgate, dh2, h, g, wm, wl), comm=comm)


def _rope_tables(lp):
    pos = jnp.arange(lp, dtype=F32) - float(PAD)
    half = D_ROPE // 2
    inv_freq = ROPE_THETA ** (-jnp.arange(0, half, dtype=F32) / half)
    ang = pos[:, None] * inv_freq[None, :]
    cos, sin = jnp.cos(ang), jnp.sin(ang)
    one = jnp.ones((lp, D_NOPE), F32)
    z_nope = jnp.zeros((lp, D_NOPE), F32)
    z_half = jnp.zeros((lp, half), F32)
    z_tail = jnp.zeros((lp, HEAD_SLAB - D_QK), F32)
    cosr = jnp.concatenate([one, cos, cos, z_tail], axis=1)
    sin_up = jnp.concatenate([z_nope, z_half, sin, z_tail], axis=1)
    sin_dn = jnp.concatenate([z_nope, -sin, z_half, z_tail], axis=1)
    return cosr, sin_up, sin_dn


def _rope(x, cosr, sin_up, sin_dn):
    half = D_ROPE // 2
    return x * cosr + pltpu.roll(x, half, axis=1) * sin_up + pltpu.roll(x, HEAD_SLAB - half, axis=1) * sin_dn


def _rope_bwd(dy, cosr, sin_up, sin_dn):
    half = D_ROPE // 2
    return (dy * cosr + pltpu.roll(dy * sin_up, HEAD_SLAB - half, axis=1)
            + pltpu.roll(dy * sin_dn, half, axis=1))


def _k_rope_slab(zm_tile):
    tm = zm_tile.shape[0]
    krp = zm_tile[:, Q_RANK + KV_RANK:MLA_IN]
    return jnp.concatenate([jnp.zeros((tm, D_NOPE), F32), krp], axis=1)


def mla_prep_fwd(zm, gql, gkvl, wuq, wuk, wuv, gqh, gkh, tables, lp):
    n = zm.shape[0]
    tm = _tile(lp, 352)
    per_seq = lp // tm
    width = MLA_HEADS * HEAD_SLAB
    scale = 1.0 / math.sqrt(D_QK)

    def body(zm_ref, gql_ref, gkvl_ref, wuq_ref, wuk_ref, wuv_ref, gqh_ref, gkh_ref,
             cos_ref, up_ref, dn_ref, q_ref, k_ref, v_ref, qn_ref, cn_ref):
        z = zm_ref[...]
        cq = z[:, :Q_RANK]
        ckv = z[:, Q_RANK:Q_RANK + KV_RANK]
        qn = (cq * _rms_r(cq) * gql_ref[...]).astype(BF16)
        cn = (ckv * _rms_r(ckv) * gkvl_ref[...]).astype(BF16)
        qn_ref[...] = qn
        cn_ref[...] = cn
        q_raw = _nt(qn, wuq_ref[...])
        k_raw = _nt(cn, wuk_ref[...])
        v_ref[...] = _nt(cn, wuv_ref[...]).astype(BF16)
        kr_slab = _k_rope_slab(z)
        cosr, sin_up, sin_dn = cos_ref[...], up_ref[...], dn_ref[...]
        for hd in range(MLA_HEADS):
            cols = slice(hd * HEAD_SLAB, (hd + 1) * HEAD_SLAB)
            xq = q_raw[:, cols]
            yq = _rope(xq * _rms_r(xq, D_QK) * gqh_ref[...], cosr, sin_up, sin_dn)
            q_ref[:, cols] = (yq * scale).astype(BF16)
            xk = k_raw[:, cols] + kr_slab
            yk = _rope(xk * _rms_r(xk, D_QK) * gkh_ref[...], cosr, sin_up, sin_dn)
            k_ref[:, cols] = yk.astype(BF16)

    tab = pl.BlockSpec((tm, HEAD_SLAB), lambda i: (i % per_seq, 0))
    return pl.pallas_call(
        body, name="mla_prep_fwd", grid=(n // tm,),
        in_specs=[_row(tm, MLA_IN), _fixed((1, Q_RANK)), _fixed((1, KV_RANK)), VMEM_WHOLE, VMEM_WHOLE, VMEM_WHOLE,
                  _fixed((1, HEAD_SLAB)), _fixed((1, HEAD_SLAB)), tab, tab, tab],
        out_specs=[_row(tm, width), _row(tm, width), _row(tm, MLA_HEADS * D_V), _row(tm, Q_RANK), _row(tm, KV_RANK)],
        out_shape=[jax.ShapeDtypeStruct((n, width), BF16), jax.ShapeDtypeStruct((n, width), BF16),
                   jax.ShapeDtypeStruct((n, MLA_HEADS * D_V), BF16), jax.ShapeDtypeStruct((n, Q_RANK), BF16),
                   jax.ShapeDtypeStruct((n, KV_RANK), BF16)],
        compiler_params=_params(("parallel",)),
    )(zm, gql, gkvl, wuq, wuk, wuv, gqh, gkh, *tables)


def mla_prep_bwd(dq, dk, dv, zm, qn, cn, gql, gkvl, wuq, wuk, wuv, gqh, gkh, tables, lp, comm=None):
    n = zm.shape[0]
    tm = _tile(lp, 704)
    per_seq = lp // tm
    width = MLA_HEADS * HEAD_SLAB
    scale = 1.0 / math.sqrt(D_QK)

    def body(dq_ref, dk_ref, dv_ref, zm_ref, qn_ref, cn_ref, gql_ref, gkvl_ref, wuq_ref, wuk_ref, wuv_ref,
             gqh_ref, gkh_ref, cos_ref, up_ref, dn_ref,
             dzm_ref, dqr_ref, dkr_ref, dgql_ref, dgkvl_ref, dgqh_ref, dgkh_ref):
        z = zm_ref[...]
        cq = z[:, :Q_RANK]
        ckv = z[:, Q_RANK:Q_RANK + KV_RANK]
        q_raw = _nt(qn_ref[...], wuq_ref[...])
        k_raw = _nt(cn_ref[...], wuk_ref[...])
        kr_slab = _k_rope_slab(z)
        cosr, sin_up, sin_dn = cos_ref[...], up_ref[...], dn_ref[...]
        dgq = jnp.zeros((1, HEAD_SLAB), F32)
        dgk = jnp.zeros((1, HEAD_SLAB), F32)
        dkrp = jnp.zeros((tm, HEAD_SLAB - D_NOPE), F32)
        for hd in range(MLA_HEADS):
            cols = slice(hd * HEAD_SLAB, (hd + 1) * HEAD_SLAB)
            xq = q_raw[:, cols]
            dxn = _rope_bwd(dq_ref[:, cols] * scale, cosr, sin_up, sin_dn)
            dxq, dg = _rms_bwd(xq, _rms_r(xq, D_QK), gqh_ref[...], dxn, D_QK)
            dgq = dgq + dg
            dqr_ref[:, cols] = dxq.astype(BF16)
            xk = k_raw[:, cols] + kr_slab
            dxn = _rope_bwd(dk_ref[:, cols], cosr, sin_up, sin_dn)
            dxk, dg = _rms_bwd(xk, _rms_r(xk, D_QK), gkh_ref[...], dxn, D_QK)
            dgk = dgk + dg
            dkr_ref[:, cols] = dxk.astype(BF16)
            dkrp = dkrp + dxk[:, D_NOPE:]
        dqn = _nn(dqr_ref[...], wuq_ref[...])
        dcn = _nn(dkr_ref[...], wuk_ref[...]) + _nn(dv_ref[...].astype(BF16), wuv_ref[...])
        dcq, dg1 = _rms_bwd(cq, _rms_r(cq), gql_ref[...], dqn)
        dckv, dg2 = _rms_bwd(ckv, _rms_r(ckv), gkvl_ref[...], dcn)
        dzm_ref[:, :Q_RANK] = dcq
        dzm_ref[:, Q_RANK:Q_RANK + KV_RANK] = dckv
        dzm_ref[:, Q_RANK + KV_RANK:] = dkrp
        first = pl.program_id(0) == 0
        _accumulate(dgql_ref, dg1, first)
        _accumulate(dgkvl_ref, dg2, first)
        _accumulate(dgqh_ref, dgq, first)
        _accumulate(dgkh_ref, dgk, first)

    tab = pl.BlockSpec((tm, HEAD_SLAB), lambda i: (i % per_seq, 0))
    return _call(
        body, "mla_prep_bwd", (n // tm,),
        [_row(tm, width), _row(tm, width), _row(tm, MLA_HEADS * D_V), _row(tm, MLA_IN),
         _row(tm, Q_RANK), _row(tm, KV_RANK), _fixed((1, Q_RANK)), _fixed((1, KV_RANK)),
         VMEM_WHOLE, VMEM_WHOLE, VMEM_WHOLE, _fixed((1, HEAD_SLAB)), _fixed((1, HEAD_SLAB)), tab, tab, tab],
        [_row(tm, MLA_IN), _row(tm, width), _row(tm, width), _fixed((1, Q_RANK)), _fixed((1, KV_RANK)),
         _fixed((1, HEAD_SLAB)), _fixed((1, HEAD_SLAB))],
        [jax.ShapeDtypeStruct((n, MLA_IN), F32), jax.ShapeDtypeStruct((n, width), BF16),
         jax.ShapeDtypeStruct((n, width), BF16), jax.ShapeDtypeStruct((1, Q_RANK), F32),
         jax.ShapeDtypeStruct((1, KV_RANK), F32), jax.ShapeDtypeStruct((1, HEAD_SLAB), F32),
         jax.ShapeDtypeStruct((1, HEAD_SLAB), F32)],
        ("arbitrary",), (dq, dk, dv, zm, qn, cn, gql, gkvl, wuq, wuk, wuv, gqh, gkh, *tables), comm=comm)


def _attn_tile(lp):
    return _tile(lp, 704, CHUNK)


def _chunk_mask(i, j, t):
    qpos = i * t + lax.broadcasted_iota(jnp.int32, (t, t), 0)
    kpos = j * t + lax.broadcasted_iota(jnp.int32, (t, t), 1)
    same_or_earlier = jnp.right_shift(kpos, CHUNK_SHIFT) <= jnp.right_shift(qpos, CHUNK_SHIFT)
    return jnp.logical_and(same_or_earlier, kpos >= PAD)


def _masked_scores(s, i, j, t, diagonal):
    if diagonal:
        return jnp.where(_chunk_mask(i, j, t), s, NEG_INF)
    kpos = j * t + lax.broadcasted_iota(jnp.int32, (1, t), 1)
    return s + jnp.where(kpos < PAD, NEG_INF, 0.0)


def attn_fwd(q, k, v, nb, lp, comm=None):
    n = q.shape[0]
    t = _attn_tile(lp)
    nq = lp // t

    hp = ATTN_HEADS_PER_STEP

    def body(q_ref, k_ref, v_ref, o_ref, lse_ref):
        i = pl.program_id(2)
        qs = [q_ref[:, hh * HEAD_SLAB:(hh + 1) * HEAD_SLAB] for hh in range(hp)]

        def kv_step(j, carry, diagonal=False):
            off = pl.multiple_of(j * t, t)
            out = []
            for hh in range(hp):
                m, l, acc = carry[hh]
                kv = k_ref[pl.ds(off, t), hh * HEAD_SLAB:(hh + 1) * HEAD_SLAB]
                s = _masked_scores(_nt(qs[hh], kv), i, j, t, diagonal)
                m_new = jnp.maximum(m, jnp.max(s, axis=-1, keepdims=True))
                p = jnp.exp(s - m_new)
                alpha = jnp.exp(m - m_new)
                l = alpha * l + jnp.sum(p, axis=-1, keepdims=True)
                acc = alpha * acc + _nn(p.astype(BF16), v_ref[pl.ds(off, t), hh * D_V:(hh + 1) * D_V])
                out.append((m_new, l, acc))
            return tuple(out)

        init = tuple((jnp.full((t, 1), NEG_INF, F32), jnp.zeros((t, 1), F32), jnp.zeros((t, D_V), F32))
                     for _ in range(hp))
        done = kv_step(i, lax.fori_loop(0, i, kv_step, init), diagonal=True)
        for hh, (m, l, acc) in enumerate(done):
            o_ref[:, hh * D_V:(hh + 1) * D_V] = acc * (1.0 / l)
            lse_ref[hh] = jnp.broadcast_to(m + jnp.log(l), (t, LANES))

    return _call(
        body, "attn_fwd", (nb, MLA_HEADS // hp, nq),
        [pl.BlockSpec((t, hp * HEAD_SLAB), lambda b, h, i: (b * nq + i, h)),
         pl.BlockSpec((lp, hp * HEAD_SLAB), lambda b, h, i: (b, h)),
         pl.BlockSpec((lp, hp * D_V), lambda b, h, i: (b, h))],
        [pl.BlockSpec((t, hp * D_V), lambda b, h, i: (b * nq + i, h)),
         pl.BlockSpec((hp, t, LANES), lambda b, h, i: (h, b * nq + i, 0))],
        [jax.ShapeDtypeStruct((n, MLA_HEADS * D_V), F32), jax.ShapeDtypeStruct((MLA_HEADS, n, LANES), F32)],
        ("parallel", "parallel", "parallel"), (q, k, v), comm=comm)


def attn_bwd(q, k, v, o, do, lse, nb, lp, comm=None):
    n = q.shape[0]
    t = _attn_tile(lp)
    nq = lp // t

    def body(q_ref, k_ref, v_ref, o_ref, do_ref, lse_ref, dq_ref, dk_ref, dv_ref):
        dk_ref[...] = jnp.zeros_like(dk_ref)
        dv_ref[...] = jnp.zeros_like(dv_ref)

        def q_step(i, _):
            qoff = pl.multiple_of(i * t, t)
            qv = q_ref[pl.ds(qoff, t), :]
            dov = do_ref[pl.ds(qoff, t), :]
            delta = jnp.sum(o_ref[pl.ds(qoff, t), :] * dov, axis=-1, keepdims=True)
            lse_q = jnp.max(lse_ref[0, pl.ds(qoff, t), :], axis=-1, keepdims=True)
            do16 = dov.astype(BF16)

            def kv_step(j, dq_acc, diagonal=False):
                koff = pl.multiple_of(j * t, t)
                kv = k_ref[pl.ds(koff, t), :]
                s = _masked_scores(_nt(qv, kv), i, j, t, diagonal)
                p = jnp.exp(s - lse_q)
                dp = _nt(do16, v_ref[pl.ds(koff, t), :])
                ds16 = (p * (dp - delta)).astype(BF16)
                dv_ref[pl.ds(koff, t), :] += _tn(p.astype(BF16), do16)
                dk_ref[pl.ds(koff, t), :] += _tn(ds16, qv)
                return dq_acc + _nn(ds16, kv)

            earlier = lax.fori_loop(0, i, kv_step, jnp.zeros((t, HEAD_SLAB), F32))
            dq_ref[pl.ds(qoff, t), :] = kv_step(i, earlier, diagonal=True)
            return 0

        lax.fori_loop(0, nq, q_step, 0)

    wide = pl.BlockSpec((lp, HEAD_SLAB), lambda b, h: (b, h))
    thin = pl.BlockSpec((lp, D_V), lambda b, h: (b, h))
    width = MLA_HEADS * HEAD_SLAB
    return _call(
        body, "attn_bwd", (nb, MLA_HEADS),
        [wide, wide, thin, thin, thin, pl.BlockSpec((1, lp, LANES), lambda b, h: (h, b, 0))],
        [wide, wide, thin],
        [jax.ShapeDtypeStruct((n, width), F32), jax.ShapeDtypeStruct((n, width), F32),
         jax.ShapeDtypeStruct((n, MLA_HEADS * D_V), F32)],
        ("parallel", "parallel"), (q, k, v, o, do, lse), comm=comm)


def _seq_rows(nb, lp, width):
    rows = lax.broadcasted_iota(jnp.int32, (lp, width), 0)
    return jnp.concatenate([rows] * nb, axis=0) if nb > 1 else rows


def _lru_gates(u, w_ref, cb, wa, wx, ba, bx, lam):
    xc = (cb + w_ref[pl.ds(3, 1), :] * u + w_ref[pl.ds(2, 1), :] * pltpu.roll(u, 1, axis=0)
          + w_ref[pl.ds(1, 1), :] * pltpu.roll(u, 2, axis=0) + w_ref[pl.ds(0, 1), :] * pltpu.roll(u, 3, axis=0))
    xc16 = xc.astype(BF16)
    ra = _sig_tanh(_nn(xc16, wa) + ba)
    ia = _sig_tanh(_nn(xc16, wx) + bx)
    sp = _softplus(-lam)
    log_a = -C_RGLRU * ra * sp
    a = jnp.exp(log_a)
    x2 = 2.0 * log_a
    mult = jnp.sqrt(jnp.where(x2 > -1e-2, -x2 * (1.0 + x2 * (0.5 + x2 * (1.0 / 6.0))), 1.0 - a * a))
    return xc, xc16, ra, ia, sp, a, mult


def _scan_block_rows(width):
    return lax.broadcasted_iota(jnp.int32, (8, width), 0)


def lru_fwd(zl, conv_w, conv_b, wa, wx, ba, bx, lam, nb, lp, comm=None):
    n = zl.shape[0]
    w = LRU_TILE
    nt = LRU_WIDTH // w
    nblk = lp // 8

    def body(u_ref, gt_ref, cw_ref, cb_ref, wa_ref, wx_ref, ba_ref, bx_ref, lam_ref, y_ref, h_ref, a_s, b_s):
        u = u_ref[...]
        xc, _, _, ia, _, a, mult = _lru_gates(u, cw_ref, cb_ref[...], wa_ref[...], wx_ref[...],
                                              ba_ref[...], bx_ref[...], lam_ref[...])
        row = _seq_rows(nb, lp, w)
        mult = jnp.where(row == PAD, 1.0, mult)
        a_s[...] = a
        b_s[...] = jnp.where(row < PAD, 0.0, mult * (ia * xc))
        r8 = _scan_block_rows(w)

        def blk(i, carry):
            out = []
            for s_id in range(nb):
                off = pl.multiple_of(s_id * lp + i * 8, 8)
                av = a_s[pl.ds(off, 8), :]
                bv = b_s[pl.ds(off, 8), :]
                for sh in (1, 2, 4):
                    keep = r8 >= sh
                    bv = jnp.where(keep, av * pltpu.roll(bv, sh, axis=0) + bv, bv)
                    av = jnp.where(keep, av * pltpu.roll(av, sh, axis=0), av)
                hv = bv + av * carry[s_id]
                h_ref[pl.ds(off, 8), :] = hv
                out.append(jnp.sum(jnp.where(r8 == 7, hv, 0.0), axis=0, keepdims=True))
            return tuple(out)

        lax.fori_loop(0, nblk, blk, tuple(jnp.zeros((1, w), F32) for _ in range(nb)))
        gelu, _ = _gelu_and_grad(gt_ref[...])
        y_ref[...] = h_ref[...] * gelu

    col = lambda c: (0, c)
    return _call(
        body, "lru_fwd", (nt,),
        [pl.BlockSpec((n, w), col), pl.BlockSpec((n, w), lambda c: (0, nt + c)),
         pl.BlockSpec((CONV_W, w), col), pl.BlockSpec((1, w), col),
         pl.BlockSpec((w, w), lambda c: (c, c)), pl.BlockSpec((w, w), lambda c: (c, c)),
         pl.BlockSpec((1, w), col), pl.BlockSpec((1, w), col), pl.BlockSpec((1, w), col)],
        [pl.BlockSpec((n, w), col), pl.BlockSpec((n, w), col)],
        [jax.ShapeDtypeStruct((n, LRU_WIDTH), F32), jax.ShapeDtypeStruct((n, LRU_WIDTH), F32)],
        ("parallel",), (zl, zl, conv_w, conv_b, wa, wx, ba, bx, lam),
        scratch=[pltpu.VMEM((n, w), F32), pltpu.VMEM((n, w), F32)], comm=comm)


def lru_bwd(zl, hs, dy, conv_w, conv_b, wa, wx, ba, bx, lam, nb, lp, comm=None):
    n = zl.shape[0]
    w = LRU_TILE
    nt = LRU_WIDTH // w
    nblk = lp // 8

    def body(u_ref, gt_ref, h_ref, dy_ref, cw_ref, cb_ref, wa_ref, wx_ref, ba_ref, bx_ref, lam_ref,
             du_ref, dgt_ref, dcw_ref, dcb_ref, dba_ref, dbx_ref, dlam_ref, dwa_ref, dwx_ref,
             c_s, d_s, g_s, dwa_s, dwx_s):
        u = u_ref[...]
        lam = lam_ref[...]
        xc, xc16, ra, ia, sp, a, mult = _lru_gates(u, cw_ref, cb_ref[...], wa_ref[...], wx_ref[...],
                                                   ba_ref[...], bx_ref[...], lam)
        row = lax.broadcasted_iota(jnp.int32, (lp, w), 0)
        hv = h_ref[...]
        dyv = dy_ref[...]
        gelu, dgelu = _gelu_and_grad(gt_ref[...])
        dgt_ref[...] = jnp.where(row >= PAD, dyv * hv * dgelu, 0.0)
        c_s[...] = pltpu.roll(a, lp - 1, axis=0)
        d_s[...] = dyv * gelu
        r8 = _scan_block_rows(w)

        def blk(ii, carry):
            off = pl.multiple_of((nblk - 1 - ii) * 8, 8)
            cv = c_s[pl.ds(off, 8), :]
            dv = d_s[pl.ds(off, 8), :]
            for sh in (1, 2, 4):
                keep = r8 < 8 - sh
                dv = jnp.where(keep, cv * pltpu.roll(dv, 8 - sh, axis=0) + dv, dv)
                cv = jnp.where(keep, cv * pltpu.roll(cv, 8 - sh, axis=0), cv)
            gv = dv + cv * carry
            g_s[pl.ds(off, 8), :] = gv
            return jnp.sum(jnp.where(r8 == 0, gv, 0.0), axis=0, keepdims=True)

        lax.fori_loop(0, nblk, blk, jnp.zeros((1, w), F32))
        gv = g_s[...]
        first_row = row == PAD
        db = jnp.where(row >= PAD, gv, 0.0)
        da = jnp.where(row > PAD, gv * pltpu.roll(hv, 1, axis=0), 0.0)
        mult_eff = jnp.where(first_row, 1.0, mult)
        dmult = jnp.where(first_row, 0.0, db * (ia * xc))
        dia = db * mult_eff * xc
        dxc = db * mult_eff * ia
        dla = da * a - dmult * (a * a) / mult
        dra = dla * (-C_RGLRU * sp)
        dsp = jnp.sum(dla * (-C_RGLRU * ra), axis=0, keepdims=True)
        dpa = dra * ra * (1.0 - ra)
        dpx = dia * ia * (1.0 - ia)
        dpa16 = dpa.astype(BF16)
        dpx16 = dpx.astype(BF16)
        dxc = dxc + _nt(dpa16, wa_ref[...]) + _nt(dpx16, wx_ref[...])
        du = cw_ref[pl.ds(CONV_W - 1, 1), :] * dxc
        dcw = [jnp.sum(dxc * u, axis=0, keepdims=True)]
        for tap in range(1, CONV_W):
            dcw.insert(0, jnp.sum(dxc * pltpu.roll(u, tap, axis=0), axis=0, keepdims=True))
            du = du + cw_ref[pl.ds(CONV_W - 1 - tap, 1), :] * pltpu.roll(dxc, lp - tap, axis=0)
        du_ref[...] = jnp.where(row >= PAD, du, 0.0)
        first = pl.program_id(1) == 0
        _accumulate(dlam_ref, -_sig(-lam) * dsp, first)
        _accumulate(dba_ref, jnp.sum(dpa, axis=0, keepdims=True), first)
        _accumulate(dbx_ref, jnp.sum(dpx, axis=0, keepdims=True), first)
        _accumulate(dcb_ref, jnp.sum(dxc, axis=0, keepdims=True), first)
        _accumulate(dcw_ref, jnp.concatenate(dcw, axis=0), first)
        _accumulate(dwa_s, _tn(xc16, dpa16), first)
        _accumulate(dwx_s, _tn(xc16, dpx16), first)

        @pl.when(pl.program_id(1) == nb - 1)
        def _():
            for j in range(w // LRU_BLOCK):
                blk_rows = slice(j * LRU_BLOCK, (j + 1) * LRU_BLOCK)
                dwa_ref[0, blk_rows, :] = dwa_s[blk_rows, blk_rows]
                dwx_ref[0, blk_rows, :] = dwx_s[blk_rows, blk_rows]

    col = lambda c, b: (0, c)
    vec = pl.BlockSpec((1, w), col)
    mat = pl.BlockSpec((w, w), lambda c, b: (c, c))
    big = pl.BlockSpec((lp, w), lambda c, b: (b, c))
    dmat = pl.BlockSpec((1, w, LRU_BLOCK), lambda c, b: (c, 0, 0))
    return _call(
        body, "lru_bwd", (nt, nb),
        [big, pl.BlockSpec((lp, w), lambda c, b: (b, nt + c)), big, big,
         pl.BlockSpec((CONV_W, w), col), vec, mat, mat, vec, vec, vec],
        [big, big, pl.BlockSpec((CONV_W, w), col), vec, vec, vec, vec, dmat, dmat],
        [jax.ShapeDtypeStruct((n, LRU_WIDTH), F32), jax.ShapeDtypeStruct((n, LRU_WIDTH), F32),
         jax.ShapeDtypeStruct((CONV_W, LRU_WIDTH), F32), jax.ShapeDtypeStruct((1, LRU_WIDTH), F32),
         jax.ShapeDtypeStruct((1, LRU_WIDTH), F32), jax.ShapeDtypeStruct((1, LRU_WIDTH), F32),
         jax.ShapeDtypeStruct((1, LRU_WIDTH), F32), jax.ShapeDtypeStruct((nt, w, LRU_BLOCK), F32),
         jax.ShapeDtypeStruct((nt, w, LRU_BLOCK), F32)],
        ("parallel", "arbitrary"), (zl, zl, hs, dy, conv_w, conv_b, wa, wx, ba, bx, lam),
        scratch=[pltpu.VMEM((lp, w), F32), pltpu.VMEM((lp, w), F32), pltpu.VMEM((lp, w), F32),
                 pltpu.VMEM((w, w), F32), pltpu.VMEM((w, w), F32)], comm=comm)


def outproj_fwd(h, ya, yl, gao, glo, wout):
    n, d = h.shape
    half = ya.shape[1]
    tm = _tile(n, 704)

    def body(h_ref, ya_ref, yl_ref, gao_ref, glo_ref, w_ref, ho_ref, yn_ref):
        xa = ya_ref[...]
        xl = yl_ref[...]
        na = (xa * _rms_r(xa) * gao_ref[...]).astype(BF16)
        nl = (xl * _rms_r(xl) * glo_ref[...]).astype(BF16)
        yn_ref[:, :half] = na
        yn_ref[:, half:] = nl
        ho_ref[...] = h_ref[...] + _nn(na, w_ref[:half, :]) + _nn(nl, w_ref[half:, :])

    return pl.pallas_call(
        body, name="outproj_fwd", grid=(n // tm,),
        in_specs=[_row(tm, d), _row(tm, half), _row(tm, half), _fixed((1, half)), _fixed((1, half)), VMEM_WHOLE],
        out_specs=[_row(tm, d), _row(tm, 2 * half)],
        out_shape=[jax.ShapeDtypeStruct((n, d), F32), jax.ShapeDtypeStruct((n, 2 * half), BF16)],
        compiler_params=_params(("parallel",)),
    )(h, ya, yl, gao, glo, wout)


def outproj_bwd(dh, ya, yl, gao, glo, wout):
    n, d = dh.shape
    half = ya.shape[1]
    tm = _tile(n, 704)

    def body(dh_ref, ya_ref, yl_ref, gao_ref, glo_ref, w_ref, dya_ref, dyl_ref, dgao_ref, dglo_ref):
        d16 = dh_ref[...].astype(BF16)
        xa = ya_ref[...]
        xl = yl_ref[...]
        dxa, dga = _rms_bwd(xa, _rms_r(xa), gao_ref[...], _nt(d16, w_ref[:half, :]))
        dxl, dgl = _rms_bwd(xl, _rms_r(xl), glo_ref[...], _nt(d16, w_ref[half:, :]))
        dya_ref[...] = dxa
        dyl_ref[...] = dxl
        first = pl.program_id(0) == 0
        _accumulate(dgao_ref, dga, first)
        _accumulate(dglo_ref, dgl, first)

    return pl.pallas_call(
        body, name="outproj_bwd", grid=(n // tm,),
        in_specs=[_row(tm, d), _row(tm, half), _row(tm, half), _fixed((1, half)), _fixed((1, half)), VMEM_WHOLE],
        out_specs=[_row(tm, half), _row(tm, half), _fixed((1, half)), _fixed((1, half))],
        out_shape=[jax.ShapeDtypeStruct((n, half), F32), jax.ShapeDtypeStruct((n, half), F32),
                   jax.ShapeDtypeStruct((1, half), F32), jax.ShapeDtypeStruct((1, half), F32)],
        compiler_params=_params(("arbitrary",)),
    )(dh, ya, yl, gao, glo, wout)


def _loss_and_grad(x, gv, tgt, first_row):
    tm, d = x.shape
    r = _rms_r(x)
    row = first_row + lax.broadcasted_iota(jnp.int32, (tm, d), 0)
    diff = jnp.where(row >= FIRST_FRAME, x * r * gv - tgt, 0.0)
    part = 0.5 * jnp.sum(jnp.sum(diff * diff, axis=-1, keepdims=True) * (1.0 / d), axis=0, keepdims=True)
    dx, dg = _rms_bwd(x, r, gv, diff * (1.0 / d))
    return dx, part, dg


def assemble_cols(g, name):
    _, k, ns = g.shape

    def body(g_ref, o_ref):
        for j in range(N_DEV):
            o_ref[:, j * ns:(j + 1) * ns] = g_ref[j]

    return pl.pallas_call(body, name=name, out_shape=jax.ShapeDtypeStruct((k, N_DEV * ns), g.dtype),
                          compiler_params=_params(None))(g)


def split_cols(x, name):
    k, cols = x.shape
    ns = cols // N_DEV

    def body(x_ref, o_ref):
        for j in range(N_DEV):
            o_ref[j] = x_ref[:, j * ns:(j + 1) * ns]

    return pl.pallas_call(body, name=name, out_shape=jax.ShapeDtypeStruct((N_DEV, k, ns), x.dtype),
                          compiler_params=_params(None))(x)


def _slab_rows(w, per_head):
    k = w.shape[1]
    w = w.reshape(MLA_HEADS, per_head, k)
    return jnp.pad(w, ((0, 0), (0, HEAD_SLAB - per_head), (0, 0))).reshape(MLA_HEADS * HEAD_SLAB, k)


def _unslab_rows(w, per_head):
    k = w.shape[1]
    return w.reshape(MLA_HEADS, HEAD_SLAB, k)[:, :per_head].reshape(MLA_HEADS * per_head, k)


def meta_grad(dh0, nb, lp):
    d = dh0.shape[1]
    ns = d // N_DEV
    per_seq = lp // N_META

    def body(x_ref, o_ref):
        x = x_ref[...]
        for j in range(N_DEV):
            _accumulate(o_ref.at[j], x[:, j * ns:(j + 1) * ns], pl.program_id(0) == 0)

    return pl.pallas_call(
        body, name="meta_grad", grid=(nb,),
        in_specs=[pl.BlockSpec((N_META, d), lambda b: (b * per_seq + PAD // N_META, 0))],
        out_specs=pl.BlockSpec((N_DEV, N_META, ns), lambda b: (0, 0, 0)),
        out_shape=jax.ShapeDtypeStruct((N_DEV, N_META, ns), F32),
        compiler_params=_params(("arbitrary",)))(dh0)


VECTORS = [("ffn1_norm", 1024), ("mix_norm", 1024), ("q_latent_norm", 384), ("kv_latent_norm", 256),
           ("q_head_norm", 192), ("k_head_norm", 192), ("conv_b", 512), ("gate_a_b", 512), ("gate_x_b", 512),
           ("lru_lambda", 512), ("attn_out_norm", 512), ("lru_out_norm", 512), ("ffn2_norm", 1024),
           ("final_norm", 1024)]
VEC_ROWS = 16
LOSS_ROW = len(VECTORS)
GATES = ["gate_a_w", "gate_x_w"]


def pack_vectors(grads, loss):
    def body(*refs):
        o_ref = refs[-1]
        o_ref[...] = jnp.zeros_like(o_ref)
        for t, (ref, (_, cnt)) in enumerate(zip(refs[:-2], VECTORS)):
            o_ref[t:t + 1, :cnt] = ref[:, :cnt]
        o_ref[LOSS_ROW:LOSS_ROW + 1, :LANES] = refs[-2][...]

    return pl.pallas_call(body, name="pack_vectors", out_shape=jax.ShapeDtypeStruct((VEC_ROWS, D_MODEL), F32),
                          compiler_params=_params(None))(*[grads[name] for name, _ in VECTORS], loss)


def _adamw_update(w, g, m, v):
    c1 = 1.0 / (1.0 - ADAM_B1 ** ADAM_STEP)
    c2 = 1.0 / (1.0 - ADAM_B2 ** ADAM_STEP)
    mn = ADAM_B1 * m + (1.0 - ADAM_B1) * g
    vn = ADAM_B2 * v + (1.0 - ADAM_B2) * (g * g)
    delta = -ADAM_LR * ((mn * c1) / (jnp.sqrt(vn * c2) + ADAM_EPS) + ADAM_WD * w)
    return delta, mn, vn


def _sum_slots(ref, index=()):
    acc = ref[(0,) + index].astype(F32)
    for s in range(1, N_DEV):
        acc = acc + ref[(s,) + index].astype(F32)
    return acc


def adamw_sharded(r, w, m, v, name):
    rows, cols = w.shape
    tr = _tile(rows, 256, 16) if rows % 16 == 0 else rows

    def body(r_ref, w_ref, m_ref, v_ref, g_ref, d_ref, mo_ref, vo_ref):
        g = _sum_slots(r_ref)
        g_ref[...] = g
        d_ref[...], mo_ref[...], vo_ref[...] = _adamw_update(w_ref[...], g, m_ref[...], v_ref[...])

    spec = pl.BlockSpec((tr, cols), lambda i: (i, 0))
    shape = jax.ShapeDtypeStruct((rows, cols), F32)
    return pl.pallas_call(
        body, name=name, grid=(rows // tr,),
        in_specs=[pl.BlockSpec((N_DEV, tr, cols), lambda i: (0, i, 0))] + [spec] * 3,
        out_specs=[spec] * 4, out_shape=[shape] * 4,
        compiler_params=_params(("parallel",)),
    )(r, w, m, v)


def adamw_small(r_vec, r_gates, w, m, v):
    nt = len(VECTORS) + len(GATES)

    def body(*refs):
        rv_ref = refs[0]
        rg_refs = refs[1:1 + len(GATES)]
        base = 1 + len(GATES)
        w_refs, m_refs, v_refs = (refs[base + i * nt:base + (i + 1) * nt] for i in range(3))
        outs = refs[base + 3 * nt:]
        g_o, d_o, m_o, v_o = (outs[i * nt:(i + 1) * nt] for i in range(4))
        outs[4 * nt][...] = _sum_slots(rv_ref, (slice(LOSS_ROW, LOSS_ROW + 1), slice(0, LANES)))
        for t in range(nt):
            if t < len(VECTORS):
                cnt = VECTORS[t][1]
                g = _sum_slots(rv_ref, (slice(t, t + 1), slice(0, cnt)))
            else:
                g = _sum_slots(rg_refs[t - len(VECTORS)])
            g_o[t][...] = g
            d_o[t][...], m_o[t][...], v_o[t][...] = _adamw_update(w_refs[t][...], g, m_refs[t][...], v_refs[t][...])

    shapes = [jax.ShapeDtypeStruct(a.shape, F32) for a in w]
    res = pl.pallas_call(body, name="adamw_small", out_shape=shapes * 4 + [jax.ShapeDtypeStruct((1, LANES), F32)],
                         compiler_params=_params(None))(r_vec, *r_gates, *w, *m, *v)
    return [res[i * nt:(i + 1) * nt] for i in range(4)], res[4 * nt]


def _block_diag(w):
    nb, n, _ = w.shape
    eye = jnp.eye(nb, dtype=w.dtype)
    return (eye[:, None, :, None] * w[:, :, None, :]).reshape(nb * n, nb * n)


def _two_d(a):
    if a.ndim == 3:
        return a.reshape(a.shape[1], a.shape[2])
    if a.ndim == 4:
        return a.reshape(a.shape[1] * a.shape[2], a.shape[3])
    return a


_WEIGHT_NAMES = ['meta_tokens', 'ffn1_norm', 'ffn1_w_gate', 'ffn1_w_up', 'ffn1_w_down', 'mix_norm', 'w_in',
                 'q_latent_norm', 'w_uq', 'kv_latent_norm', 'w_uk', 'w_uv', 'q_head_norm', 'k_head_norm', 'conv_w',
                 'conv_b', 'gate_a_w', 'gate_a_b', 'gate_x_w', 'gate_x_b', 'lru_lambda', 'attn_out_norm',
                 'lru_out_norm', 'w_out', 'ffn2_norm', 'ffn2_w_gate', 'ffn2_w_up', 'ffn2_w_down', 'final_norm']


COLUMN_SHARDED = ("ffn1_w_gate", "ffn1_w_up", "ffn2_w_gate", "ffn2_w_up", "w_in", "w_uq", "w_uk", "w_uv")


def train_step(x, tgt, w, m, v):
    nb, seq, d = x.shape
    lp = PAD + N_META + seq
    n = nb * lp

    def local(a, name):
        a = _two_d(a)
        return a.T if name in COLUMN_SHARDED else a

    sh = {name: local(w[name], name) for name in _WEIGHT_NAMES}
    m2 = {name: local(m[name], name) for name in _WEIGHT_NAMES}
    v2 = {name: local(v[name], name) for name in _WEIGHT_NAMES}

    def b16(name):
        return sh[name].astype(BF16)

    out = {}

    def update(name, landed):
        out[name] = adamw_sharded(landed, sh[name], m2[name], v2[name], "adamw_" + name)

    g_meta, g_conv, g_wg1, g_wu1 = exchange(
        [sh["meta_tokens"], sh["conv_w"], b16("ffn1_w_gate"), b16("ffn1_w_up")], ["gather"] * 4, "gather_ffn1")
    wg1, wu1 = g_wg1.reshape(D_FF, d), g_wu1.reshape(D_FF, d)
    meta = assemble_cols(g_meta, "assemble_meta")
    conv_w = assemble_cols(g_conv, "assemble_conv")

    front = jnp.concatenate([jnp.zeros((PAD, d), F32), meta], axis=0)
    h0 = jnp.concatenate([jnp.broadcast_to(front[None], (nb, FIRST_FRAME, d)), x], axis=1).reshape(n, d)
    tgt_p = jnp.concatenate([jnp.zeros((nb, FIRST_FRAME, d), F32), tgt], axis=1).reshape(n, d)
    tables = _rope_tables(lp)
    zero_tail = jnp.zeros((1, HEAD_SLAB - D_QK), F32)
    gqh = jnp.concatenate([sh["q_head_norm"], zero_tail], axis=1)
    gkh = jnp.concatenate([sh["k_head_norm"], zero_tail], axis=1)
    wa = _block_diag(w["gate_a_w"][0]).astype(BF16)
    wx = _block_diag(w["gate_x_w"][0]).astype(BF16)

    (u1, a1, b1, s1), (g_wd1, g_in) = ffn_up(h0, sh["ffn1_norm"], wg1, wu1, "ffn1_up",
                                             comm=([b16("ffn1_w_down"), b16("w_in")], ["gather"] * 2))
    wd1 = g_wd1.reshape(D_FF, d)
    mla_rows = MLA_IN - D_ROPE
    w_in = g_in.reshape(mla_rows + 2 * LRU_WIDTH, d)
    wm = jnp.concatenate([w_in[:mla_rows], jnp.zeros((D_ROPE, d), BF16)], axis=0)
    wl = w_in[mla_rows:]
    (h1, u2, zm, zl), (g_uq, g_uk, g_uv, g_out) = ffn_down_inproj(
        h0, s1, wd1, sh["mix_norm"], wm, wl, "ffn1_down_inproj",
        comm=([b16("w_uq"), b16("w_uk"), b16("w_uv"), b16("w_out")], ["gather"] * 4))
    wuq = _slab_rows(g_uq.reshape(MLA_HEADS * D_QK, Q_RANK), D_QK)
    wuk = _slab_rows(g_uk.reshape(MLA_HEADS * D_NOPE, KV_RANK), D_NOPE)
    wuv = g_uv.reshape(MLA_HEADS * D_V, KV_RANK)
    w_out = g_out.reshape(d, d)

    q, k, vv, qn, cn = mla_prep_fwd(zm, sh["q_latent_norm"], sh["kv_latent_norm"], wuq, wuk, wuv, gqh, gkh, tables, lp)
    (y_mla, lse), (g_wu2, g_wd2) = attn_fwd(
        q, k, vv, nb, lp, comm=([b16("ffn2_w_up"), b16("ffn2_w_down")], ["gather"] * 2))
    (y_lru, hs), (g_wg2,) = lru_fwd(zl, conv_w, sh["conv_b"], wa, wx, sh["gate_a_b"], sh["gate_x_b"], sh["lru_lambda"],
                                    nb, lp, comm=([b16("ffn2_w_gate")], ["gather"]))
    wg2, wu2, wd2 = (g.reshape(D_FF, d) for g in (g_wg2, g_wu2, g_wd2))
    h2, yn = outproj_fwd(h1, y_mla, y_lru, sh["attn_out_norm"], sh["lru_out_norm"], w_out)
    dh3, u3, a3, b3, loss, g_final = ffn_fwd_loss(h2, sh["ffn2_norm"], wg2, wu2, wd2, sh["final_norm"], tgt_p, lp,
                                                  "ffn2_fwd_loss")

    vec = {"final_norm": g_final}
    (dh2, da3, db3, sh3, vec["ffn2_norm"]), _ = ffn_bwd_act(dh3, h2, sh["ffn2_norm"], a3, b3, wg2, wu2, wd2, "ffn2_bwd")
    ff_shards = (N_DEV, D_FF // N_DEV, d)
    dwg2 = tn_matmul(da3, u3, "ffn2_dwg", "bf16").reshape(ff_shards)
    dwu2 = tn_matmul(db3, u3, "ffn2_dwu", "bf16").reshape(ff_shards)
    dwd2 = tn_matmul(sh3, dh3, "ffn2_dwd", "bf16").reshape(ff_shards)

    dy_mla, dy_lru, vec["attn_out_norm"], vec["lru_out_norm"] = outproj_bwd(
        dh2, y_mla, y_lru, sh["attn_out_norm"], sh["lru_out_norm"], w_out)
    dw_out = tn_matmul(yn, dh2, "dw_out", "bf16").reshape(N_DEV, d // N_DEV, d)
    (du, dgate, dconv, vec["conv_b"], vec["gate_a_b"], vec["gate_x_b"], vec["lru_lambda"], dga, dgx), landed = lru_bwd(
        zl, hs, dy_lru, conv_w, sh["conv_b"], wa, wx, sh["gate_a_b"], sh["gate_x_b"], sh["lru_lambda"], nb, lp,
        comm=([dwg2, dw_out], ["scatter"] * 2))
    update("ffn2_w_gate", landed[0])
    update("w_out", landed[1])

    (dq, dk, dv), (r_wu2,) = attn_bwd(q, k, vv, y_mla, dy_mla, lse, nb, lp, comm=([dwu2], ["scatter"]))
    update("ffn2_w_up", r_wu2)

    (dzm, dqr, dkr, vec["q_latent_norm"], vec["kv_latent_norm"], vec["q_head_norm"], vec["k_head_norm"]), (r_wd2,) = (
        mla_prep_bwd(dq, dk, dv, zm, qn, cn, sh["q_latent_norm"], sh["kv_latent_norm"], wuq, wuk, wuv, gqh, gkh,
                     tables, lp, comm=([dwd2], ["scatter"])))
    update("ffn2_w_down", r_wd2)
    dwuq = _unslab_rows(tn_matmul(dqr, qn, "dw_uq", "bf16"), D_QK).reshape(N_DEV, -1, Q_RANK)
    dwuk = _unslab_rows(tn_matmul(dkr, cn, "dw_uk", "bf16"), D_NOPE).reshape(N_DEV, -1, KV_RANK)
    dwuv = tn_matmul(dv, cn, "dw_uv", "bf16").reshape(N_DEV, -1, KV_RANK)
    (dh1, vec["mix_norm"]), landed = inproj_bwd(dzm, du, dgate, dh2, h1, sh["mix_norm"], wm, wl,
                                                comm=([dwuq, dwuk, dwuv], ["scatter"] * 3))
    for name, r in zip(("w_uq", "w_uk", "w_uv"), landed):
        update(name, r)
    dw_in = jnp.concatenate([tn_matmul(dzm, u2, "dw_in_mla", "bf16")[:mla_rows], tn_matmul(du, u2, "dw_in_u", "bf16"),
                             tn_matmul(dgate, u2, "dw_in_gate", "bf16")], axis=0).reshape(N_DEV, -1, d)

    dwd1 = tn_matmul(s1, dh1, "ffn1_dwd", "bf16").reshape(ff_shards)
    (dh0, da1, db1, vec["ffn1_norm"]), landed = ffn_bwd_act(
        dh1, h0, sh["ffn1_norm"], a1, b1, wg1, wu1, wd1, "ffn1_bwd", emit_sh=False,
        comm=([dw_in, split_cols(dconv, "split_conv"), dwd1], ["scatter"] * 3))
    for name, r in zip(("w_in", "conv_w", "ffn1_w_down"), landed):
        update(name, r)

    dwg1 = tn_matmul(da1, u1, "ffn1_dwg", "bf16").reshape(ff_shards)
    dwu1, (r_wg1,) = tn_matmul(db1, u1, "ffn1_dwu", "bf16", comm=([dwg1], ["scatter"]))
    dmeta = meta_grad(dh0, nb, lp)
    gates = [dga.reshape(LRU_WIDTH, LRU_BLOCK), dgx.reshape(LRU_WIDTH, LRU_BLOCK)]
    r_vec, r_ga, r_gx, r_meta, r_wu1 = exchange(
        [pack_vectors(vec, loss)] + gates + [dmeta, dwu1.reshape(ff_shards)], ["gather"] * 3 + ["scatter"] * 2,
        "exchange_last")
    update("ffn1_w_gate", r_wg1)
    update("ffn1_w_up", r_wu1)
    update("meta_tokens", r_meta)

    small = [name for name, _ in VECTORS] + GATES
    res, total_loss = adamw_small(r_vec, [r_ga, r_gx], [sh[nm] for nm in small], [m2[nm] for nm in small],
                                  [v2[nm] for nm in small])
    for i, name in enumerate(small):
        out[name] = [res[j][i] for j in range(4)]

    grad_x = dh0.reshape(nb, lp, d)[:, FIRST_FRAME:]
    loss = total_loss[0, 0]

    def as_given(a, name):
        return (a.T if name in COLUMN_SHARDED else a).reshape(w[name].shape)

    cols = [[as_given(out[name][j], name) for name in _WEIGHT_NAMES] for j in range(4)]
    return (loss, grad_x, *cols[0], *cols[1], *cols[2], *cols[3])


def kernel(x, meta_tokens, ffn1_norm, ffn1_w_gate, ffn1_w_up, ffn1_w_down, mix_norm, w_in, q_latent_norm, w_uq, kv_latent_norm, w_uk, w_uv, q_head_norm, k_head_norm, conv_w, conv_b, gate_a_w, gate_a_b, gate_x_w, gate_x_b, lru_lambda, attn_out_norm, lru_out_norm, w_out, ffn2_norm, ffn2_w_gate, ffn2_w_up, ffn2_w_down, final_norm, loss_target, m_meta_tokens, m_ffn1_norm, m_ffn1_w_gate, m_ffn1_w_up, m_ffn1_w_down, m_mix_norm, m_w_in, m_q_latent_norm, m_w_uq, m_kv_latent_norm, m_w_uk, m_w_uv, m_q_head_norm, m_k_head_norm, m_conv_w, m_conv_b, m_gate_a_w, m_gate_a_b, m_gate_x_w, m_gate_x_b, m_lru_lambda, m_attn_out_norm, m_lru_out_norm, m_w_out, m_ffn2_norm, m_ffn2_w_gate, m_ffn2_w_up, m_ffn2_w_down, m_final_norm, v_meta_tokens, v_ffn1_norm, v_ffn1_w_gate, v_ffn1_w_up, v_ffn1_w_down, v_mix_norm, v_w_in, v_q_latent_norm, v_w_uq, v_kv_latent_norm, v_w_uk, v_w_uv, v_q_head_norm, v_k_head_norm, v_conv_w, v_conv_b, v_gate_a_w, v_gate_a_b, v_gate_x_w, v_gate_x_b, v_lru_lambda, v_attn_out_norm, v_lru_out_norm, v_w_out, v_ffn2_norm, v_ffn2_w_gate, v_ffn2_w_up, v_ffn2_w_down, v_final_norm):
    args = locals()
    w = {name: args[name] for name in _WEIGHT_NAMES}
    m = {name: args["m_" + name] for name in _WEIGHT_NAMES}
    v = {name: args["v_" + name] for name in _WEIGHT_NAMES}
    return train_step(x, loss_target, w, m, v)
```

```python
import math

import jax
import jax.numpy as jnp
from jax import lax
from jax.experimental import pallas as pl
from jax.experimental.pallas import tpu as pltpu

F32 = jnp.float32
BF16 = jnp.bfloat16

D_MODEL = 1024
CHUNK = 64
CHUNK_SHIFT = 6
N_META = 16
PAD = CHUNK - N_META
FIRST_FRAME = PAD + N_META
MLA_HEADS = 4
D_NOPE = 128
D_ROPE = 64
D_QK = D_NOPE + D_ROPE
D_V = 128
HEAD_SLAB = 256
KV_RANK = 256
Q_RANK = 384
ROPE_THETA = 10000.0
LRU_WIDTH = 512
LRU_BLOCKS = 8
LRU_BLOCK = 64
LRU_TILE = 128
CONV_W = 4
C_RGLRU = 8.0
D_FF = 2816
MLA_IN = 768
EPS = 1e-6
NEG_INF = -1e30
N_DEV = 8
LANES = 128
VMEM_LIMIT = 52 * 1024 * 1024
ATTN_HEADS_PER_STEP = 2
TN_ROWS = 4224
TN_X_BYTES = 12 * 1024 * 1024
TN_Y_BYTES = 9 * 1024 * 1024 // 2

ADAM_LR = 0.001
ADAM_B1 = 0.9
ADAM_B2 = 0.999
ADAM_EPS = 1e-08
ADAM_WD = 0.01
ADAM_STEP = 10

VMEM_WHOLE = pl.BlockSpec(memory_space=pltpu.VMEM)
HBM_WHOLE = pl.BlockSpec(memory_space=pl.ANY)


def _params(sems):
    if sems is None:
        return pltpu.CompilerParams(vmem_limit_bytes=VMEM_LIMIT)
    return pltpu.CompilerParams(dimension_semantics=sems, vmem_limit_bytes=VMEM_LIMIT)


def _tile(n, cap, mult=16):
    best = None
    for t in range(mult, min(n, cap) + 1, mult):
        if n % t == 0:
            best = t
    assert best is not None, (n, cap, mult)
    return best


def _row(tm, d):
    return pl.BlockSpec((tm, d), lambda i: (i, 0))


def _fixed(shape):
    return pl.BlockSpec(shape, lambda i: (0,) * len(shape))


def _mesh_position():
    return lax.axis_index("x"), lax.axis_index("y"), lax.axis_index("c")


def _flat_index(x, y, c):
    return 4 * x + 2 * y + c


def _peers(x, y, c):
    out = []
    for k in range(1, N_DEV):
        fx, fy, fc = (k >> 2) & 1, (k >> 1) & 1, k & 1
        out.append((1 - x if fx else x, 1 - y if fy else y, 1 - c if fc else c))
    return out


def _comm_out_shapes(srcs, modes):
    return [jax.ShapeDtypeStruct((N_DEV,) + s.shape if md == "gather" else s.shape, s.dtype)
            for s, md in zip(srcs, modes)]


def _comm_scratch(n):
    per_peer = n * (N_DEV - 1)
    return [pltpu.SemaphoreType.DMA((per_peer,)), pltpu.SemaphoreType.DMA((per_peer,)), pltpu.SemaphoreType.DMA((n,))]


class _Copies:
    def __init__(self, own, first, relay):
        self.own, self.first, self.relay = own, first, relay

    def start(self):
        for cp in self.own + self.first:
            cp.start()

    def forward(self):
        for arrival, onward in self.relay:
            arrival.wait_recv()
            onward.start()

    def finish(self):
        arrivals = [a for a, _ in self.relay]
        onward = [f for _, f in self.relay]
        for cp in self.first + onward:
            if not any(cp is a for a in arrivals):
                cp.wait_recv()
        for cp in self.first + onward:
            cp.wait_send()
        for cp in self.own:
            cp.wait()


def _comm_copies(src_refs, dst_refs, modes, send, recv, local):
    x, y, c = _mesh_position()
    me = _flat_index(x, y, c)
    n = len(modes)
    sibling = (x, y, 1 - c)
    chips = [(1 - x, y), (x, 1 - y), (1 - x, 1 - y)]

    def remote(src, dst, k, t, to):
        return pltpu.make_async_remote_copy(src_ref=src, dst_ref=dst, send_sem=send.at[k * n + t],
                                            recv_sem=recv.at[k * n + t], device_id=to,
                                            device_id_type=pl.DeviceIdType.MESH)

    own, first, relay = [], [], []
    for t, (src, dst, md) in enumerate(zip(src_refs, dst_refs, modes)):
        if md == "scatter":
            own.append(pltpu.make_async_copy(src.at[me], dst.at[me], local.at[t]))
            for k, peer in enumerate(_peers(x, y, c)):
                first.append(remote(src.at[_flat_index(*peer)], dst.at[me], k, t, peer))
        else:
            own.append(pltpu.make_async_copy(src, dst.at[me], local.at[t]))
            first.append(remote(src, dst.at[me], 0, t, sibling))
            for j, chip in enumerate(chips):
                arrival = remote(src, dst.at[me], 1 + j, t, (*chip, c))
                landed = dst.at[_flat_index(*chip, c)]
                first.append(arrival)
                relay.append((arrival, remote(landed, landed, 4 + j, t, sibling)))
    return _Copies(own, first, relay)


def _hosted(body, n_in, n_out, modes, grid):
    t = len(modes)
    total = math.prod(grid)

    def wrapped(*refs):
        ins, csrc = refs[:n_in], refs[n_in:n_in + t]
        outs = refs[n_in + t:n_in + t + n_out]
        cdst = refs[n_in + t + n_out:n_in + 2 * t + n_out]
        scratch = refs[n_in + 2 * t + n_out:-3]
        copies = _comm_copies(csrc, cdst, modes, *refs[-3:])
        step = pl.program_id(0)
        for axis in range(1, len(grid)):
            step = step * grid[axis] + pl.program_id(axis)

        @pl.when(step == 0)
        def _():
            copies.start()

        body(*ins, *outs, *scratch)

        @pl.when(step == (total * 3) // 5)
        def _():
            copies.forward()

        @pl.when(step == total - 1)
        def _():
            copies.finish()

    return wrapped


def _call(body, name, grid, in_specs, out_specs, out_shape, sems, args, scratch=(), comm=None):
    if comm is None:
        outs = pl.pallas_call(body, name=name, grid=grid, in_specs=in_specs, out_specs=out_specs, out_shape=out_shape,
                              scratch_shapes=list(scratch), compiler_params=_params(sems))(*args)
        return outs, []
    srcs, modes = comm
    n = len(modes)
    res = pl.pallas_call(
        _hosted(body, len(in_specs), len(out_specs), modes, grid), name=name, grid=grid,
        in_specs=list(in_specs) + [HBM_WHOLE] * n, out_specs=list(out_specs) + [HBM_WHOLE] * n,
        out_shape=list(out_shape) + _comm_out_shapes(srcs, modes),
        scratch_shapes=list(scratch) + _comm_scratch(n),
        compiler_params=_params(("arbitrary",) * len(grid)))(*args, *srcs)
    return res[:len(out_specs)], res[len(out_specs):]


def exchange(srcs, modes, name):
    n = len(modes)

    def body(*refs):
        copies = _comm_copies(refs[:n], refs[n:2 * n], modes, *refs[2 * n:])
        copies.start()
        copies.forward()
        copies.finish()

    return pl.pallas_call(body, name=name, in_specs=[HBM_WHOLE] * n, out_specs=[HBM_WHOLE] * n,
                          out_shape=_comm_out_shapes(srcs, modes), scratch_shapes=_comm_scratch(n))(*srcs)


def _nn(a, b):
    return jnp.dot(a, b, preferred_element_type=F32)


def _nt(a, b):
    return lax.dot_general(a, b, (((1,), (1,)), ((), ())), preferred_element_type=F32)


def _tn(a, b):
    return lax.dot_general(a, b, (((0,), (0,)), ((), ())), preferred_element_type=F32)


def _sig(x):
    return 1.0 / (1.0 + jnp.exp(-x))


def _rms_r(x, n=None):
    n = x.shape[-1] if n is None else n
    return lax.rsqrt(jnp.sum(x * x, axis=-1, keepdims=True) * (1.0 / n) + EPS)


def _rms_bwd(x, r, g, dy, n=None):
    n = x.shape[-1] if n is None else n
    xhat = x * r
    dxhat = dy * g
    dx = r * (dxhat - xhat * (jnp.sum(dxhat * xhat, axis=-1, keepdims=True) * (1.0 / n)))
    return dx, jnp.sum(dy * xhat, axis=0, keepdims=True)


def _accumulate(ref, val, first):
    @pl.when(first)
    def _():
        ref[...] = val

    @pl.when(jnp.logical_not(first))
    def _():
        ref[...] += val


_GELU_C = math.sqrt(2.0 / math.pi)


def _gelu_and_grad(x):
    inner = _GELU_C * (x + 0.044715 * x * x * x)
    t = jnp.tanh(inner)
    gelu = 0.5 * x * (1.0 + t)
    dgelu = 0.5 * (1.0 + t) + 0.5 * x * (1.0 - t * t) * _GELU_C * (1.0 + 3.0 * 0.044715 * x * x)
    return gelu, dgelu


def _log1p_small(t):
    return jnp.where(t < 1e-3, t * (1.0 - t * (0.5 - t * (1.0 / 3.0))), jnp.log(1.0 + t))


def _softplus(x):
    return jnp.maximum(x, 0.0) + _log1p_small(jnp.exp(-jnp.abs(x)))


def _sig_tanh(x):
    return 0.5 + 0.5 * jnp.tanh(0.5 * x)


def _ff_chunks(f):
    return 2 if (f // 2) % LANES == 0 else 1


def _swiglu_half(x, g_ref, wg_ref, wu_ref, wd_ref, a_ref, b_ref, fc):
    f = wg_ref.shape[0]
    u = (x * _rms_r(x) * g_ref[...]).astype(BF16)
    acc = jnp.zeros(x.shape, F32)
    for c in range(f // fc):
        cols = slice(c * fc, (c + 1) * fc)
        a = _nt(u, wg_ref[cols, :])
        b = _nt(u, wu_ref[cols, :])
        s = (a * _sig(a) * b).astype(BF16)
        acc = acc + _nn(s, wd_ref[cols, :])
        a_ref[:, cols] = a.astype(BF16)
        b_ref[:, cols] = b.astype(BF16)
    return x + 0.5 * acc, u


def ffn_up(h, g, wg, wu, name, comm=None):
    n, d = h.shape
    f = wg.shape[0]
    tm = _tile(n, 528)
    fc = 2 * LANES if f % (2 * LANES) == 0 else f

    def body(h_ref, g_ref, wg_ref, wu_ref, u_ref, a_ref, b_ref, s_ref):
        x = h_ref[...]
        u = (x * _rms_r(x) * g_ref[...]).astype(BF16)
        u_ref[...] = u
        for c in range(f // fc):
            cols = slice(c * fc, (c + 1) * fc)
            a = _nt(u, wg_ref[cols, :])
            b = _nt(u, wu_ref[cols, :])
            a_ref[:, cols] = a.astype(BF16)
            b_ref[:, cols] = b.astype(BF16)
            s_ref[:, cols] = (0.5 * (a * _sig(a) * b)).astype(BF16)

    wide = jax.ShapeDtypeStruct((n, f), BF16)
    return _call(
        body, name, (n // tm,),
        [_row(tm, d), _fixed((1, d)), VMEM_WHOLE, VMEM_WHOLE],
        [_row(tm, d), _row(tm, f), _row(tm, f), _row(tm, f)],
        [jax.ShapeDtypeStruct((n, d), BF16), wide, wide, wide],
        ("parallel",), (h, g, wg, wu), comm=comm)


def ffn_down_inproj(h, s, wd, g, wm, wl, name, comm=None):
    n, d = h.shape
    f = wd.shape[0]
    tm = _tile(n, 528)

    def body(h_ref, s_ref, wd_ref, g_ref, wm_ref, wl_ref, ho_ref, u_ref, zm_ref, zl_ref):
        x = h_ref[...] + _nn(s_ref[...], wd_ref[...])
        ho_ref[...] = x
        u = (x * _rms_r(x) * g_ref[...]).astype(BF16)
        u_ref[...] = u
        zm_ref[...] = _nt(u, wm_ref[...])
        zl_ref[...] = _nt(u, wl_ref[...])

    return _call(
        body, name, (n // tm,),
        [_row(tm, d), _row(tm, f), VMEM_WHOLE, _fixed((1, d)), VMEM_WHOLE, VMEM_WHOLE],
        [_row(tm, d), _row(tm, d), _row(tm, MLA_IN), _row(tm, 2 * LRU_WIDTH)],
        [jax.ShapeDtypeStruct((n, d), F32), jax.ShapeDtypeStruct((n, d), BF16),
         jax.ShapeDtypeStruct((n, MLA_IN), F32), jax.ShapeDtypeStruct((n, 2 * LRU_WIDTH), F32)],
        ("parallel",), (h, s, wd, g, wm, wl), comm=comm)


def ffn_fwd_loss(h, g, wg, wu, wd, g_final, tgt, lp, name):
    n, d = h.shape
    f = wg.shape[0]
    tm = _tile(lp, 528)
    per_seq = lp // tm
    fc = 2 * LANES if f % (2 * LANES) == 0 else f

    def body(h_ref, g_ref, wg_ref, wu_ref, wd_ref, gf_ref, t_ref, dh_ref, u_ref, a_ref, b_ref, loss_ref, dgf_ref):
        i = pl.program_id(0)
        y, u_ref[...] = _swiglu_half(h_ref[...], g_ref, wg_ref, wu_ref, wd_ref, a_ref, b_ref, fc)
        dh_ref[...], part, dg = _loss_and_grad(y, gf_ref[...], t_ref[...], (i % per_seq) * tm)
        _accumulate(loss_ref, jnp.broadcast_to(part, (1, LANES)), i == 0)
        _accumulate(dgf_ref, dg, i == 0)

    outs, _ = _call(
        body, name, (n // tm,),
        [_row(tm, d), _fixed((1, d)), VMEM_WHOLE, VMEM_WHOLE, VMEM_WHOLE, _fixed((1, d)), _row(tm, d)],
        [_row(tm, d), _row(tm, d), _row(tm, f), _row(tm, f), _fixed((1, LANES)), _fixed((1, d))],
        [jax.ShapeDtypeStruct((n, d), F32), jax.ShapeDtypeStruct((n, d), BF16),
         jax.ShapeDtypeStruct((n, f), BF16), jax.ShapeDtypeStruct((n, f), BF16),
         jax.ShapeDtypeStruct((1, LANES), F32), jax.ShapeDtypeStruct((1, d), F32)],
        ("arbitrary",), (h, g, wg, wu, wd, g_final, tgt))
    return outs


def ffn_bwd_act(dh, h, g, a, b, wg, wu, wd, name, comm=None, emit_sh=True):
    n, d = h.shape
    f = wg.shape[0]
    tm = _tile(n, 352 if emit_sh else 384)
    nc = _ff_chunks(f)
    fc = f // nc

    def body(dh_ref, h_ref, g_ref, a_ref, b_ref, wg_ref, wu_ref, wd_ref, dhi_ref, da_ref, db_ref, *rest):
        dg_ref = rest[-1]
        x = h_ref[...]
        dy = dh_ref[...]
        r = _rms_r(x)
        dhh = (0.5 * dy).astype(BF16)
        du = jnp.zeros((tm, d), F32)
        for c in range(nc):
            cols = slice(c * fc, (c + 1) * fc)
            ds = _nt(dhh, wd_ref[cols, :])
            av = a_ref[:, cols].astype(F32)
            bv = b_ref[:, cols].astype(F32)
            sg = _sig(av)
            sil = av * sg
            da = (ds * bv * (sg * (1.0 + av * (1.0 - sg)))).astype(BF16)
            db = (ds * sil).astype(BF16)
            da_ref[:, cols] = da
            db_ref[:, cols] = db
            if emit_sh:
                rest[0][:, cols] = (0.5 * sil * bv).astype(BF16)
            du = du + _nn(da, wg_ref[cols, :]) + _nn(db, wu_ref[cols, :])
        dx, dg = _rms_bwd(x, r, g_ref[...], du)
        dhi_ref[...] = dy + dx
        _accumulate(dg_ref, dg, pl.program_id(0) == 0)

    wide = [jax.ShapeDtypeStruct((n, f), BF16)] * (3 if emit_sh else 2)
    return _call(
        body, name, (n // tm,),
        [_row(tm, d), _row(tm, d), _fixed((1, d)), _row(tm, f), _row(tm, f), VMEM_WHOLE, VMEM_WHOLE, VMEM_WHOLE],
        [_row(tm, d)] + [_row(tm, f)] * len(wide) + [_fixed((1, d))],
        [jax.ShapeDtypeStruct((n, d), F32)] + wide + [jax.ShapeDtypeStruct((1, d), F32)],
        ("arbitrary",), (dh, h, g, a, b, wg, wu, wd), comm=comm)


def tn_matmul(x, y, name, out="f32", comm=None):
    n, k = x.shape
    m = y.shape[1]
    tm = _tile(n, TN_ROWS)
    kc, mc = k, (512 if m % 512 == 0 else m)
    while tm * kc * x.dtype.itemsize > TN_X_BYTES and kc % (2 * LANES) == 0:
        kc //= 2
    while tm * mc * y.dtype.itemsize > TN_Y_BYTES and mc % (2 * LANES) == 0:
        mc //= 2
    steps = n // tm

    def body(x_ref, y_ref, o_ref, *acc):
        i = pl.program_id(2)
        part = _tn(x_ref[...].astype(BF16), y_ref[...].astype(BF16))
        if steps == 1:
            o_ref[...] = part.astype(o_ref.dtype)
        elif out == "f32":
            _accumulate(o_ref, part, i == 0)
        else:
            _accumulate(acc[0], part, i == 0)

            @pl.when(i == steps - 1)
            def _():
                o_ref[...] = acc[0][...].astype(BF16)

    out_shape = jax.ShapeDtypeStruct((k, m), F32 if out == "f32" else BF16)
    (res,), landed = _call(
        body, name, (k // kc, m // mc, steps),
        [pl.BlockSpec((tm, kc), lambda a, b, i: (i, a)), pl.BlockSpec((tm, mc), lambda a, b, i: (i, b))],
        [pl.BlockSpec((kc, mc), lambda a, b, i: (a, b))], [out_shape], ("parallel", "parallel", "arbitrary"), (x, y),
        scratch=[pltpu.VMEM((kc, mc), F32)] if (out == "bf16" and steps > 1) else [], comm=comm)
    return (res, landed) if comm is not None else res


def inproj_bwd(dzm, du, dgate, dh2, h, g, wm, wl, comm=None):
    n, d = h.shape
    tm = _tile(n, 352)

    def body(dzm_ref, du_ref, dgt_ref, dh2_ref, h_ref, g_ref, wm_ref, wl_ref, dh_ref, dg_ref):
        x = h_ref[...]
        dun = (_nn(dzm_ref[...].astype(BF16), wm_ref[...])
               + _nn(du_ref[...].astype(BF16), wl_ref[:LRU_WIDTH, :])
               + _nn(dgt_ref[...].astype(BF16), wl_ref[LRU_WIDTH:, :]))
        dx, dg = _rms_bwd(x, _rms_r(x), g_ref[...], dun)
        dh_ref[...] = dh2_ref[...] + dx
        _accumulate(dg_ref, dg, pl.program_id(0) == 0)

    return _call(
        body, "inproj_bwd", (n // tm,),
        [_row(tm, MLA_IN), _row(tm, LRU_WIDTH), _row(tm, LRU_WIDTH), _row(tm, d), _row(tm, d),
         _fixed((1, d)), VMEM_WHOLE, VMEM_WHOLE],
        [_row(tm, d), _fixed((1, d))],
        [jax.ShapeDtypeStruct((n, d), F32), jax.ShapeDtypeStruct((1, d), F32)],
        ("arbitrary",), (dzm, du, dgate, dh2, h, g, wm, wl), comm=comm)


def _rope_tables(lp):
    pos = jnp.arange(lp, dtype=F32) - float(PAD)
    half = D_ROPE // 2
    inv_freq = ROPE_THETA ** (-jnp.arange(0, half, dtype=F32) / half)
    ang = pos[:, None] * inv_freq[None, :]
    cos, sin = jnp.cos(ang), jnp.sin(ang)
    one = jnp.ones((lp, D_NOPE), F32)
    z_nope = jnp.zeros((lp, D_NOPE), F32)
    z_half = jnp.zeros((lp, half), F32)
    z_tail = jnp.zeros((lp, HEAD_SLAB - D_QK), F32)
    cosr = jnp.concatenate([one, cos, cos, z_tail], axis=1)
    sin_up = jnp.concatenate([z_nope, z_half, sin, z_tail], axis=1)
    sin_dn = jnp.concatenate([z_nope, -sin, z_half, z_tail], axis=1)
    return cosr, sin_up, sin_dn


def _rope(x, cosr, sin_up, sin_dn):
    half = D_ROPE // 2
    return x * cosr + pltpu.roll(x, half, axis=1) * sin_up + pltpu.roll(x, HEAD_SLAB - half, axis=1) * sin_dn


def _rope_bwd(dy, cosr, sin_up, sin_dn):
    half = D_ROPE // 2
    return (dy * cosr + pltpu.roll(dy * sin_up, HEAD_SLAB - half, axis=1)
            + pltpu.roll(dy * sin_dn, half, axis=1))


def _k_rope_slab(zm_tile):
    tm = zm_tile.shape[0]
    krp = zm_tile[:, Q_RANK + KV_RANK:MLA_IN]
    return jnp.concatenate([jnp.zeros((tm, D_NOPE), F32), krp], axis=1)


def mla_prep_fwd(zm, gql, gkvl, wuq, wuk, wuv, gqh, gkh, tables, lp):
    n = zm.shape[0]
    tm = _tile(lp, 352)
    per_seq = lp // tm
    width = MLA_HEADS * HEAD_SLAB
    scale = 1.0 / math.sqrt(D_QK)

    def body(zm_ref, gql_ref, gkvl_ref, wuq_ref, wuk_ref, wuv_ref, gqh_ref, gkh_ref,
             cos_ref, up_ref, dn_ref, q_ref, k_ref, v_ref, qn_ref, cn_ref):
        z = zm_ref[...]
        cq = z[:, :Q_RANK]
        ckv = z[:, Q_RANK:Q_RANK + KV_RANK]
        qn = (cq * _rms_r(cq) * gql_ref[...]).astype(BF16)
        cn = (ckv * _rms_r(ckv) * gkvl_ref[...]).astype(BF16)
        qn_ref[...] = qn
        cn_ref[...] = cn
        q_raw = _nt(qn, wuq_ref[...])
        k_raw = _nt(cn, wuk_ref[...])
        v_ref[...] = _nt(cn, wuv_ref[...]).astype(BF16)
        kr_slab = _k_rope_slab(z)
        cosr, sin_up, sin_dn = cos_ref[...], up_ref[...], dn_ref[...]
        for hd in range(MLA_HEADS):
            cols = slice(hd * HEAD_SLAB, (hd + 1) * HEAD_SLAB)
            xq = q_raw[:, cols]
            yq = _rope(xq * _rms_r(xq, D_QK) * gqh_ref[...], cosr, sin_up, sin_dn)
            q_ref[:, cols] = (yq * scale).astype(BF16)
            xk = k_raw[:, cols] + kr_slab
            yk = _rope(xk * _rms_r(xk, D_QK) * gkh_ref[...], cosr, sin_up, sin_dn)
            k_ref[:, cols] = yk.astype(BF16)

    tab = pl.BlockSpec((tm, HEAD_SLAB), lambda i: (i % per_seq, 0))
    return pl.pallas_call(
        body, name="mla_prep_fwd", grid=(n // tm,),
        in_specs=[_row(tm, MLA_IN), _fixed((1, Q_RANK)), _fixed((1, KV_RANK)), VMEM_WHOLE, VMEM_WHOLE, VMEM_WHOLE,
                  _fixed((1, HEAD_SLAB)), _fixed((1, HEAD_SLAB)), tab, tab, tab],
        out_specs=[_row(tm, width), _row(tm, width), _row(tm, MLA_HEADS * D_V), _row(tm, Q_RANK), _row(tm, KV_RANK)],
        out_shape=[jax.ShapeDtypeStruct((n, width), BF16), jax.ShapeDtypeStruct((n, width), BF16),
                   jax.ShapeDtypeStruct((n, MLA_HEADS * D_V), BF16), jax.ShapeDtypeStruct((n, Q_RANK), BF16),
                   jax.ShapeDtypeStruct((n, KV_RANK), BF16)],
        compiler_params=_params(("parallel",)),
    )(zm, gql, gkvl, wuq, wuk, wuv, gqh, gkh, *tables)


def mla_prep_bwd(dq, dk, dv, zm, qn, cn, gql, gkvl, wuq, wuk, wuv, gqh, gkh, tables, lp, comm=None):
    n = zm.shape[0]
    tm = _tile(lp, 704)
    per_seq = lp // tm
    width = MLA_HEADS * HEAD_SLAB
    scale = 1.0 / math.sqrt(D_QK)

    def body(dq_ref, dk_ref, dv_ref, zm_ref, qn_ref, cn_ref, gql_ref, gkvl_ref, wuq_ref, wuk_ref, wuv_ref,
             gqh_ref, gkh_ref, cos_ref, up_ref, dn_ref,
             dzm_ref, dqr_ref, dkr_ref, dgql_ref, dgkvl_ref, dgqh_ref, dgkh_ref):
        z = zm_ref[...]
        cq = z[:, :Q_RANK]
        ckv = z[:, Q_RANK:Q_RANK + KV_RANK]
        q_raw = _nt(qn_ref[...], wuq_ref[...])
        k_raw = _nt(cn_ref[...], wuk_ref[...])
        kr_slab = _k_rope_slab(z)
        cosr, sin_up, sin_dn = cos_ref[...], up_ref[...], dn_ref[...]
        dgq = jnp.zeros((1, HEAD_SLAB), F32)
        dgk = jnp.zeros((1, HEAD_SLAB), F32)
        dkrp = jnp.zeros((tm, HEAD_SLAB - D_NOPE), F32)
        for hd in range(MLA_HEADS):
            cols = slice(hd * HEAD_SLAB, (hd + 1) * HEAD_SLAB)
            xq = q_raw[:, cols]
            dxn = _rope_bwd(dq_ref[:, cols] * scale, cosr, sin_up, sin_dn)
            dxq, dg = _rms_bwd(xq, _rms_r(xq, D_QK), gqh_ref[...], dxn, D_QK)
            dgq = dgq + dg
            dqr_ref[:, cols] = dxq.astype(BF16)
            xk = k_raw[:, cols] + kr_slab
            dxn = _rope_bwd(dk_ref[:, cols], cosr, sin_up, sin_dn)
            dxk, dg = _rms_bwd(xk, _rms_r(xk, D_QK), gkh_ref[...], dxn, D_QK)
            dgk = dgk + dg
            dkr_ref[:, cols] = dxk.astype(BF16)
            dkrp = dkrp + dxk[:, D_NOPE:]
        dqn = _nn(dqr_ref[...], wuq_ref[...])
        dcn = _nn(dkr_ref[...], wuk_ref[...]) + _nn(dv_ref[...].astype(BF16), wuv_ref[...])
        dcq, dg1 = _rms_bwd(cq, _rms_r(cq), gql_ref[...], dqn)
        dckv, dg2 = _rms_bwd(ckv, _rms_r(ckv), gkvl_ref[...], dcn)
        dzm_ref[:, :Q_RANK] = dcq
        dzm_ref[:, Q_RANK:Q_RANK + KV_RANK] = dckv
        dzm_ref[:, Q_RANK + KV_RANK:] = dkrp
        first = pl.program_id(0) == 0
        _accumulate(dgql_ref, dg1, first)
        _accumulate(dgkvl_ref, dg2, first)
        _accumulate(dgqh_ref, dgq, first)
        _accumulate(dgkh_ref, dgk, first)

    tab = pl.BlockSpec((tm, HEAD_SLAB), lambda i: (i % per_seq, 0))
    return _call(
        body, "mla_prep_bwd", (n // tm,),
        [_row(tm, width), _row(tm, width), _row(tm, MLA_HEADS * D_V), _row(tm, MLA_IN),
         _row(tm, Q_RANK), _row(tm, KV_RANK), _fixed((1, Q_RANK)), _fixed((1, KV_RANK)),
         VMEM_WHOLE, VMEM_WHOLE, VMEM_WHOLE, _fixed((1, HEAD_SLAB)), _fixed((1, HEAD_SLAB)), tab, tab, tab],
        [_row(tm, MLA_IN), _row(tm, width), _row(tm, width), _fixed((1, Q_RANK)), _fixed((1, KV_RANK)),
         _fixed((1, HEAD_SLAB)), _fixed((1, HEAD_SLAB))],
        [jax.ShapeDtypeStruct((n, MLA_IN), F32), jax.ShapeDtypeStruct((n, width), BF16),
         jax.ShapeDtypeStruct((n, width), BF16), jax.ShapeDtypeStruct((1, Q_RANK), F32),
         jax.ShapeDtypeStruct((1, KV_RANK), F32), jax.ShapeDtypeStruct((1, HEAD_SLAB), F32),
         jax.ShapeDtypeStruct((1, HEAD_SLAB), F32)],
        ("arbitrary",), (dq, dk, dv, zm, qn, cn, gql, gkvl, wuq, wuk, wuv, gqh, gkh, *tables), comm=comm)


def _attn_tile(lp):
    return _tile(lp, 704, CHUNK)


def _chunk_mask(i, j, t):
    qpos = i * t + lax.broadcasted_iota(jnp.int32, (t, t), 0)
    kpos = j * t + lax.broadcasted_iota(jnp.int32, (t, t), 1)
    same_or_earlier = jnp.right_shift(kpos, CHUNK_SHIFT) <= jnp.right_shift(qpos, CHUNK_SHIFT)
    return jnp.logical_and(same_or_earlier, kpos >= PAD)


def _masked_scores(s, i, j, t, diagonal):
    if diagonal:
        return jnp.where(_chunk_mask(i, j, t), s, NEG_INF)
    kpos = j * t + lax.broadcasted_iota(jnp.int32, (1, t), 1)
    return s + jnp.where(kpos < PAD, NEG_INF, 0.0)


def attn_fwd(q, k, v, nb, lp, comm=None):
    n = q.shape[0]
    t = _attn_tile(lp)
    nq = lp // t

    hp = ATTN_HEADS_PER_STEP

    def body(q_ref, k_ref, v_ref, o_ref, lse_ref):
        i = pl.program_id(2)
        qs = [q_ref[:, hh * HEAD_SLAB:(hh + 1) * HEAD_SLAB] for hh in range(hp)]

        def kv_step(j, carry, diagonal=False):
            off = pl.multiple_of(j * t, t)
            out = []
            for hh in range(hp):
                m, l, acc = carry[hh]
                kv = k_ref[pl.ds(off, t), hh * HEAD_SLAB:(hh + 1) * HEAD_SLAB]
                s = _masked_scores(_nt(qs[hh], kv), i, j, t, diagonal)
                m_new = jnp.maximum(m, jnp.max(s, axis=-1, keepdims=True))
                p = jnp.exp(s - m_new)
                alpha = jnp.exp(m - m_new)
                l = alpha * l + jnp.sum(p, axis=-1, keepdims=True)
                acc = alpha * acc + _nn(p.astype(BF16), v_ref[pl.ds(off, t), hh * D_V:(hh + 1) * D_V])
                out.append((m_new, l, acc))
            return tuple(out)

        init = tuple((jnp.full((t, 1), NEG_INF, F32), jnp.zeros((t, 1), F32), jnp.zeros((t, D_V), F32))
                     for _ in range(hp))
        done = kv_step(i, lax.fori_loop(0, i, kv_step, init), diagonal=True)
        for hh, (m, l, acc) in enumerate(done):
            o_ref[:, hh * D_V:(hh + 1) * D_V] = acc * (1.0 / l)
            lse_ref[hh] = jnp.broadcast_to(m + jnp.log(l), (t, LANES))

    return _call(
        body, "attn_fwd", (nb, MLA_HEADS // hp, nq),
        [pl.BlockSpec((t, hp * HEAD_SLAB), lambda b, h, i: (b * nq + i, h)),
         pl.BlockSpec((lp, hp * HEAD_SLAB), lambda b, h, i: (b, h)),
         pl.BlockSpec((lp, hp * D_V), lambda b, h, i: (b, h))],
        [pl.BlockSpec((t, hp * D_V), lambda b, h, i: (b * nq + i, h)),
         pl.BlockSpec((hp, t, LANES), lambda b, h, i: (h, b * nq + i, 0))],
        [jax.ShapeDtypeStruct((n, MLA_HEADS * D_V), F32), jax.ShapeDtypeStruct((MLA_HEADS, n, LANES), F32)],
        ("parallel", "parallel", "parallel"), (q, k, v), comm=comm)


def attn_bwd(q, k, v, o, do, lse, nb, lp, comm=None):
    n = q.shape[0]
    t = _attn_tile(lp)
    nq = lp // t

    def body(q_ref, k_ref, v_ref, o_ref, do_ref, lse_ref, dq_ref, dk_ref, dv_ref):
        dk_ref[...] = jnp.zeros_like(dk_ref)
        dv_ref[...] = jnp.zeros_like(dv_ref)

        def q_step(i, _):
            qoff = pl.multiple_of(i * t, t)
            qv = q_ref[pl.ds(qoff, t), :]
            dov = do_ref[pl.ds(qoff, t), :]
            delta = jnp.sum(o_ref[pl.ds(qoff, t), :] * dov, axis=-1, keepdims=True)
            lse_q = jnp.max(lse_ref[0, pl.ds(qoff, t), :], axis=-1, keepdims=True)
            do16 = dov.astype(BF16)

            def kv_step(j, dq_acc, diagonal=False):
                koff = pl.multiple_of(j * t, t)
                kv = k_ref[pl.ds(koff, t), :]
                s = _masked_scores(_nt(qv, kv), i, j, t, diagonal)
                p = jnp.exp(s - lse_q)
                dp = _nt(do16, v_ref[pl.ds(koff, t), :])
                ds16 = (p * (dp - delta)).astype(BF16)
                dv_ref[pl.ds(koff, t), :] += _tn(p.astype(BF16), do16)
                dk_ref[pl.ds(koff, t), :] += _tn(ds16, qv)
                return dq_acc + _nn(ds16, kv)

            earlier = lax.fori_loop(0, i, kv_step, jnp.zeros((t, HEAD_SLAB), F32))
            dq_ref[pl.ds(qoff, t), :] = kv_step(i, earlier, diagonal=True)
            return 0

        lax.fori_loop(0, nq, q_step, 0)

    wide = pl.BlockSpec((lp, HEAD_SLAB), lambda b, h: (b, h))
    thin = pl.BlockSpec((lp, D_V), lambda b, h: (b, h))
    width = MLA_HEADS * HEAD_SLAB
    return _call(
        body, "attn_bwd", (nb, MLA_HEADS),
        [wide, wide, thin, thin, thin, pl.BlockSpec((1, lp, LANES), lambda b, h: (h, b, 0))],
        [wide, wide, thin],
        [jax.ShapeDtypeStruct((n, width), F32), jax.ShapeDtypeStruct((n, width), F32),
         jax.ShapeDtypeStruct((n, MLA_HEADS * D_V), F32)],
        ("parallel", "parallel"), (q, k, v, o, do, lse), comm=comm)


def _seq_rows(nb, lp, width):
    rows = lax.broadcasted_iota(jnp.int32, (lp, width), 0)
    return jnp.concatenate([rows] * nb, axis=0) if nb > 1 else rows


def _lru_gates(u, w_ref, cb, wa, wx, ba, bx, lam):
    xc = (cb + w_ref[pl.ds(3, 1), :] * u + w_ref[pl.ds(2, 1), :] * pltpu.roll(u, 1, axis=0)
          + w_ref[pl.ds(1, 1), :] * pltpu.roll(u, 2, axis=0) + w_ref[pl.ds(0, 1), :] * pltpu.roll(u, 3, axis=0))
    xc16 = xc.astype(BF16)
    ra = _sig_tanh(_nn(xc16, wa) + ba)
    ia = _sig_tanh(_nn(xc16, wx) + bx)
    sp = _softplus(-lam)
    log_a = -C_RGLRU * ra * sp
    a = jnp.exp(log_a)
    x2 = 2.0 * log_a
    mult = jnp.sqrt(jnp.where(x2 > -1e-2, -x2 * (1.0 + x2 * (0.5 + x2 * (1.0 / 6.0))), 1.0 - a * a))
    return xc, xc16, ra, ia, sp, a, mult


def _scan_block_rows(width):
    return lax.broadcasted_iota(jnp.int32, (8, width), 0)


def lru_fwd(zl, conv_w, conv_b, wa, wx, ba, bx, lam, nb, lp, comm=None):
    n = zl.shape[0]
    w = LRU_TILE
    nt = LRU_WIDTH // w
    nblk = lp // 8

    def body(u_ref, gt_ref, cw_ref, cb_ref, wa_ref, wx_ref, ba_ref, bx_ref, lam_ref, y_ref, h_ref, a_s, b_s):
        u = u_ref[...]
        xc, _, _, ia, _, a, mult = _lru_gates(u, cw_ref, cb_ref[...], wa_ref[...], wx_ref[...],
                                              ba_ref[...], bx_ref[...], lam_ref[...])
        row = _seq_rows(nb, lp, w)
        mult = jnp.where(row == PAD, 1.0, mult)
        a_s[...] = a
        b_s[...] = jnp.where(row < PAD, 0.0, mult * (ia * xc))
        r8 = _scan_block_rows(w)

        def blk(i, carry):
            out = []
            for s_id in range(nb):
                off = pl.multiple_of(s_id * lp + i * 8, 8)
                av = a_s[pl.ds(off, 8), :]
                bv = b_s[pl.ds(off, 8), :]
                for sh in (1, 2, 4):
                    keep = r8 >= sh
                    bv = jnp.where(keep, av * pltpu.roll(bv, sh, axis=0) + bv, bv)
                    av = jnp.where(keep, av * pltpu.roll(av, sh, axis=0), av)
                hv = bv + av * carry[s_id]
                h_ref[pl.ds(off, 8), :] = hv
                out.append(jnp.sum(jnp.where(r8 == 7, hv, 0.0), axis=0, keepdims=True))
            return tuple(out)

        lax.fori_loop(0, nblk, blk, tuple(jnp.zeros((1, w), F32) for _ in range(nb)))
        gelu, _ = _gelu_and_grad(gt_ref[...])
        y_ref[...] = h_ref[...] * gelu

    col = lambda c: (0, c)
    return _call(
        body, "lru_fwd", (nt,),
        [pl.BlockSpec((n, w), col), pl.BlockSpec((n, w), lambda c: (0, nt + c)),
         pl.BlockSpec((CONV_W, w), col), pl.BlockSpec((1, w), col),
         pl.BlockSpec((w, w), lambda c: (c, c)), pl.BlockSpec((w, w), lambda c: (c, c)),
         pl.BlockSpec((1, w), col), pl.BlockSpec((1, w), col), pl.BlockSpec((1, w), col)],
        [pl.BlockSpec((n, w), col), pl.BlockSpec((n, w), col)],
        [jax.ShapeDtypeStruct((n, LRU_WIDTH), F32), jax.ShapeDtypeStruct((n, LRU_WIDTH), F32)],
        ("parallel",), (zl, zl, conv_w, conv_b, wa, wx, ba, bx, lam),
        scratch=[pltpu.VMEM((n, w), F32), pltpu.VMEM((n, w), F32)], comm=comm)


def lru_bwd(zl, hs, dy, conv_w, conv_b, wa, wx, ba, bx, lam, nb, lp, comm=None):
    n = zl.shape[0]
    w = LRU_TILE
    nt = LRU_WIDTH // w
    nblk = lp // 8

    def body(u_ref, gt_ref, h_ref, dy_ref, cw_ref, cb_ref, wa_ref, wx_ref, ba_ref, bx_ref, lam_ref,
             du_ref, dgt_ref, dcw_ref, dcb_ref, dba_ref, dbx_ref, dlam_ref, dwa_ref, dwx_ref,
             c_s, d_s, g_s, dwa_s, dwx_s):
        u = u_ref[...]
        lam = lam_ref[...]
        xc, xc16, ra, ia, sp, a, mult = _lru_gates(u, cw_ref, cb_ref[...], wa_ref[...], wx_ref[...],
                                                   ba_ref[...], bx_ref[...], lam)
        row = lax.broadcasted_iota(jnp.int32, (lp, w), 0)
        hv = h_ref[...]
        dyv = dy_ref[...]
        gelu, dgelu = _gelu_and_grad(gt_ref[...])
        dgt_ref[...] = jnp.where(row >= PAD, dyv * hv * dgelu, 0.0)
        c_s[...] = pltpu.roll(a, lp - 1, axis=0)
        d_s[...] = dyv * gelu
        r8 = _scan_block_rows(w)

        def blk(ii, carry):
            off = pl.multiple_of((nblk - 1 - ii) * 8, 8)
            cv = c_s[pl.ds(off, 8), :]
            dv = d_s[pl.ds(off, 8), :]
            for sh in (1, 2, 4):
                keep = r8 < 8 - sh
                dv = jnp.where(keep, cv * pltpu.roll(dv, 8 - sh, axis=0) + dv, dv)
                cv = jnp.where(keep, cv * pltpu.roll(cv, 8 - sh, axis=0), cv)
            gv = dv + cv * carry
            g_s[pl.ds(off, 8), :] = gv
            return jnp.sum(jnp.where(r8 == 0, gv, 0.0), axis=0, keepdims=True)

        lax.fori_loop(0, nblk, blk, jnp.zeros((1, w), F32))
        gv = g_s[...]
        first_row = row == PAD
        db = jnp.where(row >= PAD, gv, 0.0)
        da = jnp.where(row > PAD, gv * pltpu.roll(hv, 1, axis=0), 0.0)
        mult_eff = jnp.where(first_row, 1.0, mult)
        dmult = jnp.where(first_row, 0.0, db * (ia * xc))
        dia = db * mult_eff * xc
        dxc = db * mult_eff * ia
        dla = da * a - dmult * (a * a) / mult
        dra = dla * (-C_RGLRU * sp)
        dsp = jnp.sum(dla * (-C_RGLRU * ra), axis=0, keepdims=True)
        dpa = dra * ra * (1.0 - ra)
        dpx = dia * ia * (1.0 - ia)
        dpa16 = dpa.astype(BF16)
        dpx16 = dpx.astype(BF16)
        dxc = dxc + _nt(dpa16, wa_ref[...]) + _nt(dpx16, wx_ref[...])
        du = cw_ref[pl.ds(CONV_W - 1, 1), :] * dxc
        dcw = [jnp.sum(dxc * u, axis=0, keepdims=True)]
        for tap in range(1, CONV_W):
            dcw.insert(0, jnp.sum(dxc * pltpu.roll(u, tap, axis=0), axis=0, keepdims=True))
            du = du + cw_ref[pl.ds(CONV_W - 1 - tap, 1), :] * pltpu.roll(dxc, lp - tap, axis=0)
        du_ref[...] = jnp.where(row >= PAD, du, 0.0)
        first = pl.program_id(1) == 0
        _accumulate(dlam_ref, -_sig(-lam) * dsp, first)
        _accumulate(dba_ref, jnp.sum(dpa, axis=0, keepdims=True), first)
        _accumulate(dbx_ref, jnp.sum(dpx, axis=0, keepdims=True), first)
        _accumulate(dcb_ref, jnp.sum(dxc, axis=0, keepdims=True), first)
        _accumulate(dcw_ref, jnp.concatenate(dcw, axis=0), first)
        _accumulate(dwa_s, _tn(xc16, dpa16), first)
        _accumulate(dwx_s, _tn(xc16, dpx16), first)

        @pl.when(pl.program_id(1) == nb - 1)
        def _():
            for j in range(w // LRU_BLOCK):
                blk_rows = slice(j * LRU_BLOCK, (j + 1) * LRU_BLOCK)
                dwa_ref[0, blk_rows, :] = dwa_s[blk_rows, blk_rows]
                dwx_ref[0, blk_rows, :] = dwx_s[blk_rows, blk_rows]

    col = lambda c, b: (0, c)
    vec = pl.BlockSpec((1, w), col)
    mat = pl.BlockSpec((w, w), lambda c, b: (c, c))
    big = pl.BlockSpec((lp, w), lambda c, b: (b, c))
    dmat = pl.BlockSpec((1, w, LRU_BLOCK), lambda c, b: (c, 0, 0))
    return _call(
        body, "lru_bwd", (nt, nb),
        [big, pl.BlockSpec((lp, w), lambda c, b: (b, nt + c)), big, big,
         pl.BlockSpec((CONV_W, w), col), vec, mat, mat, vec, vec, vec],
        [big, big, pl.BlockSpec((CONV_W, w), col), vec, vec, vec, vec, dmat, dmat],
        [jax.ShapeDtypeStruct((n, LRU_WIDTH), F32), jax.ShapeDtypeStruct((n, LRU_WIDTH), F32),
         jax.ShapeDtypeStruct((CONV_W, LRU_WIDTH), F32), jax.ShapeDtypeStruct((1, LRU_WIDTH), F32),
         jax.ShapeDtypeStruct((1, LRU_WIDTH), F32), jax.ShapeDtypeStruct((1, LRU_WIDTH), F32),
         jax.ShapeDtypeStruct((1, LRU_WIDTH), F32), jax.ShapeDtypeStruct((nt, w, LRU_BLOCK), F32),
         jax.ShapeDtypeStruct((nt, w, LRU_BLOCK), F32)],
        ("parallel", "arbitrary"), (zl, zl, hs, dy, conv_w, conv_b, wa, wx, ba, bx, lam),
        scratch=[pltpu.VMEM((lp, w), F32), pltpu.VMEM((lp, w), F32), pltpu.VMEM((lp, w), F32),
                 pltpu.VMEM((w, w), F32), pltpu.VMEM((w, w), F32)], comm=comm)


def outproj_fwd(h, ya, yl, gao, glo, wout):
    n, d = h.shape
    half = ya.shape[1]
    tm = _tile(n, 704)

    def body(h_ref, ya_ref, yl_ref, gao_ref, glo_ref, w_ref, ho_ref, yn_ref):
        xa = ya_ref[...]
        xl = yl_ref[...]
        na = (xa * _rms_r(xa) * gao_ref[...]).astype(BF16)
        nl = (xl * _rms_r(xl) * glo_ref[...]).astype(BF16)
        yn_ref[:, :half] = na
        yn_ref[:, half:] = nl
        ho_ref[...] = h_ref[...] + _nn(na, w_ref[:half, :]) + _nn(nl, w_ref[half:, :])

    return pl.pallas_call(
        body, name="outproj_fwd", grid=(n // tm,),
        in_specs=[_row(tm, d), _row(tm, half), _row(tm, half), _fixed((1, half)), _fixed((1, half)), VMEM_WHOLE],
        out_specs=[_row(tm, d), _row(tm, 2 * half)],
        out_shape=[jax.ShapeDtypeStruct((n, d), F32), jax.ShapeDtypeStruct((n, 2 * half), BF16)],
        compiler_params=_params(("parallel",)),
    )(h, ya, yl, gao, glo, wout)


def outproj_bwd(dh, ya, yl, gao, glo, wout):
    n, d = dh.shape
    half = ya.shape[1]
    tm = _tile(n, 704)

    def body(dh_ref, ya_ref, yl_ref, gao_ref, glo_ref, w_ref, dya_ref, dyl_ref, dgao_ref, dglo_ref):
        d16 = dh_ref[...].astype(BF16)
        xa = ya_ref[...]
        xl = yl_ref[...]
        dxa, dga = _rms_bwd(xa, _rms_r(xa), gao_ref[...], _nt(d16, w_ref[:half, :]))
        dxl, dgl = _rms_bwd(xl, _rms_r(xl), glo_ref[...], _nt(d16, w_ref[half:, :]))
        dya_ref[...] = dxa
        dyl_ref[...] = dxl
        first = pl.program_id(0) == 0
        _accumulate(dgao_ref, dga, first)
        _accumulate(dglo_ref, dgl, first)

    return pl.pallas_call(
        body, name="outproj_bwd", grid=(n // tm,),
        in_specs=[_row(tm, d), _row(tm, half), _row(tm, half), _fixed((1, half)), _fixed((1, half)), VMEM_WHOLE],
        out_specs=[_row(tm, half), _row(tm, half), _fixed((1, half)), _fixed((1, half))],
        out_shape=[jax.ShapeDtypeStruct((n, half), F32), jax.ShapeDtypeStruct((n, half), F32),
                   jax.ShapeDtypeStruct((1, half), F32), jax.ShapeDtypeStruct((1, half), F32)],
        compiler_params=_params(("arbitrary",)),
    )(dh, ya, yl, gao, glo, wout)


def _loss_and_grad(x, gv, tgt, first_row):
    tm, d = x.shape
    r = _rms_r(x)
    row = first_row + lax.broadcasted_iota(jnp.int32, (tm, d), 0)
    diff = jnp.where(row >= FIRST_FRAME, x * r * gv - tgt, 0.0)
    part = 0.5 * jnp.sum(jnp.sum(diff * diff, axis=-1, keepdims=True) * (1.0 / d), axis=0, keepdims=True)
    dx, dg = _rms_bwd(x, r, gv, diff * (1.0 / d))
    return dx, part, dg


def assemble_cols(g, name):
    _, k, ns = g.shape

    def body(g_ref, o_ref):
        for j in range(N_DEV):
            o_ref[:, j * ns:(j + 1) * ns] = g_ref[j]

    return pl.pallas_call(body, name=name, out_shape=jax.ShapeDtypeStruct((k, N_DEV * ns), g.dtype),
                          compiler_params=_params(None))(g)


def split_cols(x, name):
    k, cols = x.shape
    ns = cols // N_DEV

    def body(x_ref, o_ref):
        for j in range(N_DEV):
            o_ref[j] = x_ref[:, j * ns:(j + 1) * ns]

    return pl.pallas_call(body, name=name, out_shape=jax.ShapeDtypeStruct((N_DEV, k, ns), x.dtype),
                          compiler_params=_params(None))(x)


def _slab_rows(w, per_head):
    k = w.shape[1]
    w = w.reshape(MLA_HEADS, per_head, k)
    return jnp.pad(w, ((0, 0), (0, HEAD_SLAB - per_head), (0, 0))).reshape(MLA_HEADS * HEAD_SLAB, k)


def _unslab_rows(w, per_head):
    k = w.shape[1]
    return w.reshape(MLA_HEADS, HEAD_SLAB, k)[:, :per_head].reshape(MLA_HEADS * per_head, k)


def meta_grad(dh0, nb, lp):
    d = dh0.shape[1]
    ns = d // N_DEV
    per_seq = lp // N_META

    def body(x_ref, o_ref):
        x = x_ref[...]
        for j in range(N_DEV):
            _accumulate(o_ref.at[j], x[:, j * ns:(j + 1) * ns], pl.program_id(0) == 0)

    return pl.pallas_call(
        body, name="meta_grad", grid=(nb,),
        in_specs=[pl.BlockSpec((N_META, d), lambda b: (b * per_seq + PAD // N_META, 0))],
        out_specs=pl.BlockSpec((N_DEV, N_META, ns), lambda b: (0, 0, 0)),
        out_shape=jax.ShapeDtypeStruct((N_DEV, N_META, ns), F32),
        compiler_params=_params(("arbitrary",)))(dh0)


VECTORS = [("ffn1_norm", 1024), ("mix_norm", 1024), ("q_latent_norm", 384), ("kv_latent_norm", 256),
           ("q_head_norm", 192), ("k_head_norm", 192), ("conv_b", 512), ("gate_a_b", 512), ("gate_x_b", 512),
           ("lru_lambda", 512), ("attn_out_norm", 512), ("lru_out_norm", 512), ("ffn2_norm", 1024),
           ("final_norm", 1024)]
VEC_ROWS = 16
LOSS_ROW = len(VECTORS)
GATES = ["gate_a_w", "gate_x_w"]


def pack_vectors(grads, loss):
    def body(*refs):
        o_ref = refs[-1]
        o_ref[...] = jnp.zeros_like(o_ref)
        for t, (ref, (_, cnt)) in enumerate(zip(refs[:-2], VECTORS)):
            o_ref[t:t + 1, :cnt] = ref[:, :cnt]
        o_ref[LOSS_ROW:LOSS_ROW + 1, :LANES] = refs[-2][...]

    return pl.pallas_call(body, name="pack_vectors", out_shape=jax.ShapeDtypeStruct((VEC_ROWS, D_MODEL), F32),
                          compiler_params=_params(None))(*[grads[name] for name, _ in VECTORS], loss)


def _adamw_update(w, g, m, v):
    c1 = 1.0 / (1.0 - ADAM_B1 ** ADAM_STEP)
    c2 = 1.0 / (1.0 - ADAM_B2 ** ADAM_STEP)
    mn = ADAM_B1 * m + (1.0 - ADAM_B1) * g
    vn = ADAM_B2 * v + (1.0 - ADAM_B2) * (g * g)
    delta = -ADAM_LR * ((mn * c1) / (jnp.sqrt(vn * c2) + ADAM_EPS) + ADAM_WD * w)
    return delta, mn, vn


def _sum_slots(ref, index=()):
    acc = ref[(0,) + index].astype(F32)
    for s in range(1, N_DEV):
        acc = acc + ref[(s,) + index].astype(F32)
    return acc


def adamw_sharded(r, w, m, v, name):
    rows, cols = w.shape
    tr = _tile(rows, 256, 16) if rows % 16 == 0 else rows

    def body(r_ref, w_ref, m_ref, v_ref, g_ref, d_ref, mo_ref, vo_ref):
        g = _sum_slots(r_ref)
        g_ref[...] = g
        d_ref[...], mo_ref[...], vo_ref[...] = _adamw_update(w_ref[...], g, m_ref[...], v_ref[...])

    spec = pl.BlockSpec((tr, cols), lambda i: (i, 0))
    shape = jax.ShapeDtypeStruct((rows, cols), F32)
    return pl.pallas_call(
        body, name=name, grid=(rows // tr,),
        in_specs=[pl.BlockSpec((N_DEV, tr, cols), lambda i: (0, i, 0))] + [spec] * 3,
        out_specs=[spec] * 4, out_shape=[shape] * 4,
        compiler_params=_params(("parallel",)),
    )(r, w, m, v)


def adamw_stack(items, name, comm=None):
    rows, cols = items[0][1].shape
    tr = _tile(rows, 32, 16)
    steps = rows // tr
    nt = len(items)

    def body(*refs):
        ins, outs = refs[:4 * nt], refs[4 * nt:]
        for t in range(nt):
            @pl.when(pl.program_id(0) == t)
            def _(t=t):
                r_ref, w_ref, m_ref, v_ref = ins[4 * t:4 * t + 4]
                g_ref, d_ref, mo_ref, vo_ref = outs[4 * t:4 * t + 4]
                g = _sum_slots(r_ref)
                g_ref[...] = g
                d_ref[...], mo_ref[...], vo_ref[...] = _adamw_update(w_ref[...], g, m_ref[...], v_ref[...])

    def tile_of(t):
        return lambda k, i: jnp.where(k < t, 0, jnp.where(k == t, i, steps - 1))

    in_specs, out_specs, args = [], [], []
    for t, (r, w, m, v) in enumerate(items):
        pick = tile_of(t)
        in_specs.append(pl.BlockSpec((N_DEV, tr, cols), lambda k, i, pick=pick: (0, pick(k, i), 0)))
        in_specs += [pl.BlockSpec((tr, cols), lambda k, i, pick=pick: (pick(k, i), 0))] * 3
        out_specs += [pl.BlockSpec((tr, cols), lambda k, i, pick=pick: (pick(k, i), 0))] * 4
        args += [r, w, m, v]
    res, landed = _call(body, name, (nt, steps), in_specs, out_specs,
                        [jax.ShapeDtypeStruct((rows, cols), F32)] * (4 * nt), ("arbitrary", "arbitrary"), args,
                        comm=comm)
    return [res[4 * t:4 * t + 4] for t in range(nt)], landed


def adamw_small(r_vec, r_gates, w, m, v):
    nt = len(VECTORS) + len(GATES)

    def body(*refs):
        rv_ref = refs[0]
        rg_refs = refs[1:1 + len(GATES)]
        base = 1 + len(GATES)
        w_refs, m_refs, v_refs = (refs[base + i * nt:base + (i + 1) * nt] for i in range(3))
        outs = refs[base + 3 * nt:]
        g_o, d_o, m_o, v_o = (outs[i * nt:(i + 1) * nt] for i in range(4))
        outs[4 * nt][...] = _sum_slots(rv_ref, (slice(LOSS_ROW, LOSS_ROW + 1), slice(0, LANES)))
        for t in range(nt):
            if t < len(VECTORS):
                cnt = VECTORS[t][1]
                g = _sum_slots(rv_ref, (slice(t, t + 1), slice(0, cnt)))
            else:
                g = _sum_slots(rg_refs[t - len(VECTORS)])
            g_o[t][...] = g
            d_o[t][...], m_o[t][...], v_o[t][...] = _adamw_update(w_refs[t][...], g, m_refs[t][...], v_refs[t][...])

    shapes = [jax.ShapeDtypeStruct(a.shape, F32) for a in w]
    res = pl.pallas_call(body, name="adamw_small", out_shape=shapes * 4 + [jax.ShapeDtypeStruct((1, LANES), F32)],
                         compiler_params=_params(None))(r_vec, *r_gates, *w, *m, *v)
    return [res[i * nt:(i + 1) * nt] for i in range(4)], res[4 * nt]


def _block_diag(w):
    nb, n, _ = w.shape
    eye = jnp.eye(nb, dtype=w.dtype)
    return (eye[:, None, :, None] * w[:, :, None, :]).reshape(nb * n, nb * n)


def _two_d(a):
    if a.ndim == 3:
        return a.reshape(a.shape[1], a.shape[2])
    if a.ndim == 4:
        return a.reshape(a.shape[1] * a.shape[2], a.shape[3])
    return a


_WEIGHT_NAMES = ['meta_tokens', 'ffn1_norm', 'ffn1_w_gate', 'ffn1_w_up', 'ffn1_w_down', 'mix_norm', 'w_in',
                 'q_latent_norm', 'w_uq', 'kv_latent_norm', 'w_uk', 'w_uv', 'q_head_norm', 'k_head_norm', 'conv_w',
                 'conv_b', 'gate_a_w', 'gate_a_b', 'gate_x_w', 'gate_x_b', 'lru_lambda', 'attn_out_norm',
                 'lru_out_norm', 'w_out', 'ffn2_norm', 'ffn2_w_gate', 'ffn2_w_up', 'ffn2_w_down', 'final_norm']


LATE_UPDATES = ("ffn2_w_gate", "ffn2_w_up", "ffn2_w_down", "ffn1_w_down")
COLUMN_SHARDED = ("ffn1_w_gate", "ffn1_w_up", "ffn2_w_gate", "ffn2_w_up", "w_in", "w_uq", "w_uk", "w_uv")


def train_step(x, tgt, w, m, v):
    nb, seq, d = x.shape
    lp = PAD + N_META + seq
    n = nb * lp

    def local(a, name):
        a = _two_d(a)
        return a.T if name in COLUMN_SHARDED else a

    sh = {name: local(w[name], name) for name in _WEIGHT_NAMES}
    m2 = {name: local(m[name], name) for name in _WEIGHT_NAMES}
    v2 = {name: local(v[name], name) for name in _WEIGHT_NAMES}

    def b16(name):
        return sh[name].astype(BF16)

    out = {}

    late = {}

    def update(name, landed):
        if name in LATE_UPDATES:
            late[name] = landed
        else:
            out[name] = adamw_sharded(landed, sh[name], m2[name], v2[name], "adamw_" + name)

    g_meta, g_conv, g_wg1, g_wu1 = exchange(
        [sh["meta_tokens"], sh["conv_w"], b16("ffn1_w_gate"), b16("ffn1_w_up")], ["gather"] * 4, "gather_ffn1")
    wg1, wu1 = g_wg1.reshape(D_FF, d), g_wu1.reshape(D_FF, d)
    meta = assemble_cols(g_meta, "assemble_meta")
    conv_w = assemble_cols(g_conv, "assemble_conv")

    front = jnp.concatenate([jnp.zeros((PAD, d), F32), meta], axis=0)
    h0 = jnp.concatenate([jnp.broadcast_to(front[None], (nb, FIRST_FRAME, d)), x], axis=1).reshape(n, d)
    tgt_p = jnp.concatenate([jnp.zeros((nb, FIRST_FRAME, d), F32), tgt], axis=1).reshape(n, d)
    tables = _rope_tables(lp)
    zero_tail = jnp.zeros((1, HEAD_SLAB - D_QK), F32)
    gqh = jnp.concatenate([sh["q_head_norm"], zero_tail], axis=1)
    gkh = jnp.concatenate([sh["k_head_norm"], zero_tail], axis=1)
    wa = _block_diag(w["gate_a_w"][0]).astype(BF16)
    wx = _block_diag(w["gate_x_w"][0]).astype(BF16)

    (u1, a1, b1, s1), (g_wd1, g_in) = ffn_up(h0, sh["ffn1_norm"], wg1, wu1, "ffn1_up",
                                             comm=([b16("ffn1_w_down"), b16("w_in")], ["gather"] * 2))
    wd1 = g_wd1.reshape(D_FF, d)
    mla_rows = MLA_IN - D_ROPE
    w_in = g_in.reshape(mla_rows + 2 * LRU_WIDTH, d)
    wm = jnp.concatenate([w_in[:mla_rows], jnp.zeros((D_ROPE, d), BF16)], axis=0)
    wl = w_in[mla_rows:]
    (h1, u2, zm, zl), (g_uq, g_uk, g_uv, g_out) = ffn_down_inproj(
        h0, s1, wd1, sh["mix_norm"], wm, wl, "ffn1_down_inproj",
        comm=([b16("w_uq"), b16("w_uk"), b16("w_uv"), b16("w_out")], ["gather"] * 4))
    wuq = _slab_rows(g_uq.reshape(MLA_HEADS * D_QK, Q_RANK), D_QK)
    wuk = _slab_rows(g_uk.reshape(MLA_HEADS * D_NOPE, KV_RANK), D_NOPE)
    wuv = g_uv.reshape(MLA_HEADS * D_V, KV_RANK)
    w_out = g_out.reshape(d, d)

    q, k, vv, qn, cn = mla_prep_fwd(zm, sh["q_latent_norm"], sh["kv_latent_norm"], wuq, wuk, wuv, gqh, gkh, tables, lp)
    (y_mla, lse), (g_wu2, g_wd2) = attn_fwd(
        q, k, vv, nb, lp, comm=([b16("ffn2_w_up"), b16("ffn2_w_down")], ["gather"] * 2))
    (y_lru, hs), (g_wg2,) = lru_fwd(zl, conv_w, sh["conv_b"], wa, wx, sh["gate_a_b"], sh["gate_x_b"], sh["lru_lambda"],
                                    nb, lp, comm=([b16("ffn2_w_gate")], ["gather"]))
    wg2, wu2, wd2 = (g.reshape(D_FF, d) for g in (g_wg2, g_wu2, g_wd2))
    h2, yn = outproj_fwd(h1, y_mla, y_lru, sh["attn_out_norm"], sh["lru_out_norm"], w_out)
    dh3, u3, a3, b3, loss, g_final = ffn_fwd_loss(h2, sh["ffn2_norm"], wg2, wu2, wd2, sh["final_norm"], tgt_p, lp,
                                                  "ffn2_fwd_loss")

    vec = {"final_norm": g_final}
    (dh2, da3, db3, sh3, vec["ffn2_norm"]), _ = ffn_bwd_act(dh3, h2, sh["ffn2_norm"], a3, b3, wg2, wu2, wd2, "ffn2_bwd")
    ff_shards = (N_DEV, D_FF // N_DEV, d)
    dwg2 = tn_matmul(da3, u3, "ffn2_dwg", "bf16").reshape(ff_shards)
    dwu2 = tn_matmul(db3, u3, "ffn2_dwu", "bf16").reshape(ff_shards)
    dwd2 = tn_matmul(sh3, dh3, "ffn2_dwd", "bf16").reshape(ff_shards)

    dy_mla, dy_lru, vec["attn_out_norm"], vec["lru_out_norm"] = outproj_bwd(
        dh2, y_mla, y_lru, sh["attn_out_norm"], sh["lru_out_norm"], w_out)
    dw_out = tn_matmul(yn, dh2, "dw_out", "bf16").reshape(N_DEV, d // N_DEV, d)
    (du, dgate, dconv, vec["conv_b"], vec["gate_a_b"], vec["gate_x_b"], vec["lru_lambda"], dga, dgx), landed = lru_bwd(
        zl, hs, dy_lru, conv_w, sh["conv_b"], wa, wx, sh["gate_a_b"], sh["gate_x_b"], sh["lru_lambda"], nb, lp,
        comm=([dwg2, dw_out], ["scatter"] * 2))
    update("ffn2_w_gate", landed[0])
    update("w_out", landed[1])

    (dq, dk, dv), (r_wu2,) = attn_bwd(q, k, vv, y_mla, dy_mla, lse, nb, lp, comm=([dwu2], ["scatter"]))
    update("ffn2_w_up", r_wu2)

    (dzm, dqr, dkr, vec["q_latent_norm"], vec["kv_latent_norm"], vec["q_head_norm"], vec["k_head_norm"]), (r_wd2,) = (
        mla_prep_bwd(dq, dk, dv, zm, qn, cn, sh["q_latent_norm"], sh["kv_latent_norm"], wuq, wuk, wuv, gqh, gkh,
                     tables, lp, comm=([dwd2], ["scatter"])))
    update("ffn2_w_down", r_wd2)
    dwuq = _unslab_rows(tn_matmul(dqr, qn, "dw_uq", "bf16"), D_QK).reshape(N_DEV, -1, Q_RANK)
    dwuk = _unslab_rows(tn_matmul(dkr, cn, "dw_uk", "bf16"), D_NOPE).reshape(N_DEV, -1, KV_RANK)
    dwuv = tn_matmul(dv, cn, "dw_uv", "bf16").reshape(N_DEV, -1, KV_RANK)
    (dh1, vec["mix_norm"]), landed = inproj_bwd(dzm, du, dgate, dh2, h1, sh["mix_norm"], wm, wl,
                                                comm=([dwuq, dwuk, dwuv], ["scatter"] * 3))
    for name, r in zip(("w_uq", "w_uk", "w_uv"), landed):
        update(name, r)
    dw_in = jnp.concatenate([tn_matmul(dzm, u2, "dw_in_mla", "bf16")[:mla_rows], tn_matmul(du, u2, "dw_in_u", "bf16"),
                             tn_matmul(dgate, u2, "dw_in_gate", "bf16")], axis=0).reshape(N_DEV, -1, d)

    dwd1 = tn_matmul(s1, dh1, "ffn1_dwd", "bf16").reshape(ff_shards)
    (dh0, da1, db1, vec["ffn1_norm"]), landed = ffn_bwd_act(
        dh1, h0, sh["ffn1_norm"], a1, b1, wg1, wu1, wd1, "ffn1_bwd", emit_sh=False,
        comm=([dw_in, split_cols(dconv, "split_conv"), dwd1], ["scatter"] * 3))
    for name, r in zip(("w_in", "conv_w", "ffn1_w_down"), landed):
        update(name, r)

    dwg1 = tn_matmul(da1, u1, "ffn1_dwg", "bf16").reshape(ff_shards)
    dwu1, (r_wg1,) = tn_matmul(db1, u1, "ffn1_dwu", "bf16", comm=([dwg1], ["scatter"]))
    dmeta = meta_grad(dh0, nb, lp)
    gates = [dga.reshape(LRU_WIDTH, LRU_BLOCK), dgx.reshape(LRU_WIDTH, LRU_BLOCK)]
    updates, (r_vec, r_ga, r_gx, r_meta, r_wu1) = adamw_stack(
        [(late[nm], sh[nm], m2[nm], v2[nm]) for nm in LATE_UPDATES], "adamw_stack_exchange_last",
        comm=([pack_vectors(vec, loss)] + gates + [dmeta, dwu1.reshape(ff_shards)], ["gather"] * 3 + ["scatter"] * 2))
    for nm, res in zip(LATE_UPDATES, updates):
        out[nm] = res
    update("ffn1_w_gate", r_wg1)
    update("ffn1_w_up", r_wu1)
    update("meta_tokens", r_meta)

    small = [name for name, _ in VECTORS] + GATES
    res, total_loss = adamw_small(r_vec, [r_ga, r_gx], [sh[nm] for nm in small], [m2[nm] for nm in small],
                                  [v2[nm] for nm in small])
    for i, name in enumerate(small):
        out[name] = [res[j][i] for j in range(4)]

    grad_x = dh0.reshape(nb, lp, d)[:, FIRST_FRAME:]
    loss = total_loss[0, 0]

    def as_given(a, name):
        return (a.T if name in COLUMN_SHARDED else a).reshape(w[name].shape)

    cols = [[as_given(out[name][j], name) for name in _WEIGHT_NAMES] for j in range(4)]
    return (loss, grad_x, *cols[0], *cols[1], *cols[2], *cols[3])


def kernel(x, meta_tokens, ffn1_norm, ffn1_w_gate, ffn1_w_up, ffn1_w_down, mix_norm, w_in, q_latent_norm, w_uq, kv_latent_norm, w_uk, w_uv, q_head_norm, k_head_norm, conv_w, conv_b, gate_a_w, gate_a_b, gate_x_w, gate_x_b, lru_lambda, attn_out_norm, lru_out_norm, w_out, ffn2_norm, ffn2_w_gate, ffn2_w_up, ffn2_w_down, final_norm, loss_target, m_meta_tokens, m_ffn1_norm, m_ffn1_w_gate, m_ffn1_w_up, m_ffn1_w_down, m_mix_norm, m_w_in, m_q_latent_norm, m_w_uq, m_kv_latent_norm, m_w_uk, m_w_uv, m_q_head_norm, m_k_head_norm, m_conv_w, m_conv_b, m_gate_a_w, m_gate_a_b, m_gate_x_w, m_gate_x_b, m_lru_lambda, m_attn_out_norm, m_lru_out_norm, m_w_out, m_ffn2_norm, m_ffn2_w_gate, m_ffn2_w_up, m_ffn2_w_down, m_final_norm, v_meta_tokens, v_ffn1_norm, v_ffn1_w_gate, v_ffn1_w_up, v_ffn1_w_down, v_mix_norm, v_w_in, v_q_latent_norm, v_w_uq, v_kv_latent_norm, v_w_uk, v_w_uv, v_q_head_norm, v_k_head_norm, v_conv_w, v_conv_b, v_gate_a_w, v_gate_a_b, v_gate_x_w, v_gate_x_b, v_lru_lambda, v_attn_out_norm, v_lru_out_norm, v_w_out, v_ffn2_norm, v_ffn2_w_gate, v_ffn2_w_up, v_ffn2_w_down, v_final_norm):
    args = locals()
    w = {name: args[name] for name in _WEIGHT_NAMES}
    m = {name: args["m_" + name] for name in _WEIGHT_NAMES}
    v = {name: args["v_" + name] for name in _WEIGHT_NAMES}
    return train_step(x, loss_target, w, m, v)
```

```python
import math

import jax
import jax.numpy as jnp
from jax import lax
from jax.experimental import pallas as pl
from jax.experimental.pallas import tpu as pltpu

F32 = jnp.float32
BF16 = jnp.bfloat16

D_MODEL = 1024
CHUNK = 64
CHUNK_SHIFT = 6
N_META = 16
PAD = CHUNK - N_META
FIRST_FRAME = PAD + N_META
MLA_HEADS = 4
D_NOPE = 128
D_ROPE = 64
D_QK = D_NOPE + D_ROPE
D_V = 128
HEAD_SLAB = 256
KV_RANK = 256
Q_RANK = 384
ROPE_THETA = 10000.0
LRU_WIDTH = 512
LRU_BLOCKS = 8
LRU_BLOCK = 64
LRU_TILE = 128
CONV_W = 4
C_RGLRU = 8.0
D_FF = 2816
MLA_IN = 768
EPS = 1e-6
NEG_INF = -1e30
N_DEV = 8
LANES = 128
VMEM_LIMIT = 52 * 1024 * 1024
ATTN_HEADS_PER_STEP = 2
TN_ROWS = 4224
TN_X_BYTES = 12 * 1024 * 1024
TN_Y_BYTES = 9 * 1024 * 1024 // 2

ADAM_LR = 0.001
ADAM_B1 = 0.9
ADAM_B2 = 0.999
ADAM_EPS = 1e-08
ADAM_WD = 0.01
ADAM_STEP = 10

VMEM_WHOLE = pl.BlockSpec(memory_space=pltpu.VMEM)
HBM_WHOLE = pl.BlockSpec(memory_space=pl.ANY)


def _params(sems):
    if sems is None:
        return pltpu.CompilerParams(vmem_limit_bytes=VMEM_LIMIT)
    return pltpu.CompilerParams(dimension_semantics=sems, vmem_limit_bytes=VMEM_LIMIT)


def _tile(n, cap, mult=16):
    best = None
    for t in range(mult, min(n, cap) + 1, mult):
        if n % t == 0:
            best = t
    assert best is not None, (n, cap, mult)
    return best


def _row(tm, d):
    return pl.BlockSpec((tm, d), lambda i: (i, 0))


def _fixed(shape):
    return pl.BlockSpec(shape, lambda i: (0,) * len(shape))


def _mesh_position():
    return lax.axis_index("x"), lax.axis_index("y"), lax.axis_index("c")


def _flat_index(x, y, c):
    return 4 * x + 2 * y + c


def _peers(x, y, c):
    out = []
    for k in range(1, N_DEV):
        fx, fy, fc = (k >> 2) & 1, (k >> 1) & 1, k & 1
        out.append((1 - x if fx else x, 1 - y if fy else y, 1 - c if fc else c))
    return out


def _comm_out_shapes(srcs, modes):
    return [jax.ShapeDtypeStruct((N_DEV,) + s.shape if md == "gather" else s.shape, s.dtype)
            for s, md in zip(srcs, modes)]


def _comm_scratch(n):
    per_peer = n * (N_DEV - 1)
    return [pltpu.SemaphoreType.DMA((per_peer,)), pltpu.SemaphoreType.DMA((per_peer,)), pltpu.SemaphoreType.DMA((n,))]


class _Copies:
    def __init__(self, own, first, relay):
        self.own, self.first, self.relay = own, first, relay

    def start(self):
        for cp in self.own + self.first:
            cp.start()

    def forward(self):
        for arrival, onward in self.relay:
            arrival.wait_recv()
            onward.start()

    def finish(self):
        arrivals = [a for a, _ in self.relay]
        onward = [f for _, f in self.relay]
        for cp in self.first + onward:
            if not any(cp is a for a in arrivals):
                cp.wait_recv()
        for cp in self.first + onward:
            cp.wait_send()
        for cp in self.own:
            cp.wait()


def _comm_copies(src_refs, dst_refs, modes, send, recv, local):
    x, y, c = _mesh_position()
    me = _flat_index(x, y, c)
    n = len(modes)
    sibling = (x, y, 1 - c)
    chips = [(1 - x, y), (x, 1 - y), (1 - x, 1 - y)]

    def remote(src, dst, k, t, to):
        return pltpu.make_async_remote_copy(src_ref=src, dst_ref=dst, send_sem=send.at[k * n + t],
                                            recv_sem=recv.at[k * n + t], device_id=to,
                                            device_id_type=pl.DeviceIdType.MESH)

    own, first, relay = [], [], []
    for t, (src, dst, md) in enumerate(zip(src_refs, dst_refs, modes)):
        if md == "scatter":
            own.append(pltpu.make_async_copy(src.at[me], dst.at[me], local.at[t]))
            for k, peer in enumerate(_peers(x, y, c)):
                first.append(remote(src.at[_flat_index(*peer)], dst.at[me], k, t, peer))
        else:
            own.append(pltpu.make_async_copy(src, dst.at[me], local.at[t]))
            first.append(remote(src, dst.at[me], 0, t, sibling))
            for j, chip in enumerate(chips):
                arrival = remote(src, dst.at[me], 1 + j, t, (*chip, c))
                landed = dst.at[_flat_index(*chip, c)]
                first.append(arrival)
                relay.append((arrival, remote(landed, landed, 4 + j, t, sibling)))
    return _Copies(own, first, relay)


def _hosted(body, n_in, n_out, modes, grid):
    t = len(modes)
    total = math.prod(grid)

    def wrapped(*refs):
        ins, csrc = refs[:n_in], refs[n_in:n_in + t]
        outs = refs[n_in + t:n_in + t + n_out]
        cdst = refs[n_in + t + n_out:n_in + 2 * t + n_out]
        scratch = refs[n_in + 2 * t + n_out:-3]
        copies = _comm_copies(csrc, cdst, modes, *refs[-3:])
        step = pl.program_id(0)
        for axis in range(1, len(grid)):
            step = step * grid[axis] + pl.program_id(axis)

        @pl.when(step == 0)
        def _():
            copies.start()

        body(*ins, *outs, *scratch)

        @pl.when(step == (total * 4) // 5)
        def _():
            copies.forward()

        @pl.when(step == total - 1)
        def _():
            copies.finish()

    return wrapped


def _call(body, name, grid, in_specs, out_specs, out_shape, sems, args, scratch=(), comm=None):
    if comm is None:
        outs = pl.pallas_call(body, name=name, grid=grid, in_specs=in_specs, out_specs=out_specs, out_shape=out_shape,
                              scratch_shapes=list(scratch), compiler_params=_params(sems))(*args)
        return outs, []
    srcs, modes = comm
    n = len(modes)
    res = pl.pallas_call(
        _hosted(body, len(in_specs), len(out_specs), modes, grid), name=name, grid=grid,
        in_specs=list(in_specs) + [HBM_WHOLE] * n, out_specs=list(out_specs) + [HBM_WHOLE] * n,
        out_shape=list(out_shape) + _comm_out_shapes(srcs, modes),
        scratch_shapes=list(scratch) + _comm_scratch(n),
        compiler_params=_params(("arbitrary",) * len(grid)))(*args, *srcs)
    return res[:len(out_specs)], res[len(out_specs):]


def exchange(srcs, modes, name):
    n = len(modes)

    def body(*refs):
        copies = _comm_copies(refs[:n], refs[n:2 * n], modes, *refs[2 * n:])
        copies.start()
        copies.forward()
        copies.finish()

    return pl.pallas_call(body, name=name, in_specs=[HBM_WHOLE] * n, out_specs=[HBM_WHOLE] * n,
                          out_shape=_comm_out_shapes(srcs, modes), scratch_shapes=_comm_scratch(n))(*srcs)


def _nn(a, b):
    return jnp.dot(a, b, preferred_element_type=F32)


def _nt(a, b):
    return lax.dot_general(a, b, (((1,), (1,)), ((), ())), preferred_element_type=F32)


def _tn(a, b):
    return lax.dot_general(a, b, (((0,), (0,)), ((), ())), preferred_element_type=F32)


def _sig(x):
    return 1.0 / (1.0 + jnp.exp(-x))


def _rms_r(x, n=None):
    n = x.shape[-1] if n is None else n
    return lax.rsqrt(jnp.sum(x * x, axis=-1, keepdims=True) * (1.0 / n) + EPS)


def _rms_bwd(x, r, g, dy, n=None):
    n = x.shape[-1] if n is None else n
    xhat = x * r
    dxhat = dy * g
    dx = r * (dxhat - xhat * (jnp.sum(dxhat * xhat, axis=-1, keepdims=True) * (1.0 / n)))
    return dx, jnp.sum(dy * xhat, axis=0, keepdims=True)


def _accumulate(ref, val, first):
    @pl.when(first)
    def _():
        ref[...] = val

    @pl.when(jnp.logical_not(first))
    def _():
        ref[...] += val


_GELU_C = math.sqrt(2.0 / math.pi)


def _gelu_and_grad(x):
    inner = _GELU_C * (x + 0.044715 * x * x * x)
    t = jnp.tanh(inner)
    gelu = 0.5 * x * (1.0 + t)
    dgelu = 0.5 * (1.0 + t) + 0.5 * x * (1.0 - t * t) * _GELU_C * (1.0 + 3.0 * 0.044715 * x * x)
    return gelu, dgelu


def _log1p_small(t):
    return jnp.where(t < 1e-3, t * (1.0 - t * (0.5 - t * (1.0 / 3.0))), jnp.log(1.0 + t))


def _softplus(x):
    return jnp.maximum(x, 0.0) + _log1p_small(jnp.exp(-jnp.abs(x)))


def _sig_tanh(x):
    return 0.5 + 0.5 * jnp.tanh(0.5 * x)


def _ff_chunks(f):
    return 2 if (f // 2) % LANES == 0 else 1


def _swiglu_half(x, g_ref, wg_ref, wu_ref, wd_ref, a_ref, b_ref, fc):
    f = wg_ref.shape[0]
    u = (x * _rms_r(x) * g_ref[...]).astype(BF16)
    acc = jnp.zeros(x.shape, F32)
    for c in range(f // fc):
        cols = slice(c * fc, (c + 1) * fc)
        a = _nt(u, wg_ref[cols, :])
        b = _nt(u, wu_ref[cols, :])
        s = (a * _sig(a) * b).astype(BF16)
        acc = acc + _nn(s, wd_ref[cols, :])
        a_ref[:, cols] = a.astype(BF16)
        b_ref[:, cols] = b.astype(BF16)
    return x + 0.5 * acc, u


def ffn_up(h, g, wg, wu, name, comm=None):
    n, d = h.shape
    f = wg.shape[0]
    tm = _tile(n, 528)
    fc = 2 * LANES if f % (2 * LANES) == 0 else f

    def body(h_ref, g_ref, wg_ref, wu_ref, u_ref, a_ref, b_ref, s_ref):
        x = h_ref[...]
        u = (x * _rms_r(x) * g_ref[...]).astype(BF16)
        u_ref[...] = u
        for c in range(f // fc):
            cols = slice(c * fc, (c + 1) * fc)
            a = _nt(u, wg_ref[cols, :])
            b = _nt(u, wu_ref[cols, :])
            a_ref[:, cols] = a.astype(BF16)
            b_ref[:, cols] = b.astype(BF16)
            s_ref[:, cols] = (0.5 * (a * _sig(a) * b)).astype(BF16)

    wide = jax.ShapeDtypeStruct((n, f), BF16)
    return _call(
        body, name, (n // tm,),
        [_row(tm, d), _fixed((1, d)), VMEM_WHOLE, VMEM_WHOLE],
        [_row(tm, d), _row(tm, f), _row(tm, f), _row(tm, f)],
        [jax.ShapeDtypeStruct((n, d), BF16), wide, wide, wide],
        ("parallel",), (h, g, wg, wu), comm=comm)


def ffn_down_inproj(h, s, wd, g, wm, wl, name, comm=None):
    n, d = h.shape
    f = wd.shape[0]
    tm = _tile(n, 528)

    def body(h_ref, s_ref, wd_ref, g_ref, wm_ref, wl_ref, ho_ref, u_ref, zm_ref, zl_ref):
        x = h_ref[...] + _nn(s_ref[...], wd_ref[...])
        ho_ref[...] = x
        u = (x * _rms_r(x) * g_ref[...]).astype(BF16)
        u_ref[...] = u
        zm_ref[...] = _nt(u, wm_ref[...])
        zl_ref[...] = _nt(u, wl_ref[...])

    return _call(
        body, name, (n // tm,),
        [_row(tm, d), _row(tm, f), VMEM_WHOLE, _fixed((1, d)), VMEM_WHOLE, VMEM_WHOLE],
        [_row(tm, d), _row(tm, d), _row(tm, MLA_IN), _row(tm, 2 * LRU_WIDTH)],
        [jax.ShapeDtypeStruct((n, d), F32), jax.ShapeDtypeStruct((n, d), BF16),
         jax.ShapeDtypeStruct((n, MLA_IN), F32), jax.ShapeDtypeStruct((n, 2 * LRU_WIDTH), F32)],
        ("parallel",), (h, s, wd, g, wm, wl), comm=comm)


def ffn_fwd_loss(h, g, wg, wu, wd, g_final, tgt, lp, name):
    n, d = h.shape
    f = wg.shape[0]
    tm = _tile(lp, 528)
    per_seq = lp // tm
    fc = 2 * LANES if f % (2 * LANES) == 0 else f

    def body(h_ref, g_ref, wg_ref, wu_ref, wd_ref, gf_ref, t_ref, dh_ref, u_ref, a_ref, b_ref, loss_ref, dgf_ref):
        i = pl.program_id(0)
        y, u_ref[...] = _swiglu_half(h_ref[...], g_ref, wg_ref, wu_ref, wd_ref, a_ref, b_ref, fc)
        dh_ref[...], part, dg = _loss_and_grad(y, gf_ref[...], t_ref[...], (i % per_seq) * tm)
        _accumulate(loss_ref, jnp.broadcast_to(part, (1, LANES)), i == 0)
        _accumulate(dgf_ref, dg, i == 0)

    outs, _ = _call(
        body, name, (n // tm,),
        [_row(tm, d), _fixed((1, d)), VMEM_WHOLE, VMEM_WHOLE, VMEM_WHOLE, _fixed((1, d)), _row(tm, d)],
        [_row(tm, d), _row(tm, d), _row(tm, f), _row(tm, f), _fixed((1, LANES)), _fixed((1, d))],
        [jax.ShapeDtypeStruct((n, d), F32), jax.ShapeDtypeStruct((n, d), BF16),
         jax.ShapeDtypeStruct((n, f), BF16), jax.ShapeDtypeStruct((n, f), BF16),
         jax.ShapeDtypeStruct((1, LANES), F32), jax.ShapeDtypeStruct((1, d), F32)],
        ("arbitrary",), (h, g, wg, wu, wd, g_final, tgt))
    return outs


def ffn_bwd_act(dh, h, g, a, b, wg, wu, wd, name, comm=None, emit_sh=True):
    n, d = h.shape
    f = wg.shape[0]
    tm = _tile(n, 352 if emit_sh else 384)
    nc = _ff_chunks(f)
    fc = f // nc

    def body(dh_ref, h_ref, g_ref, a_ref, b_ref, wg_ref, wu_ref, wd_ref, dhi_ref, da_ref, db_ref, *rest):
        dg_ref = rest[-1]
        x = h_ref[...]
        dy = dh_ref[...]
        r = _rms_r(x)
        dhh = (0.5 * dy).astype(BF16)
        du = jnp.zeros((tm, d), F32)
        for c in range(nc):
            cols = slice(c * fc, (c + 1) * fc)
            ds = _nt(dhh, wd_ref[cols, :])
            av = a_ref[:, cols].astype(F32)
            bv = b_ref[:, cols].astype(F32)
            sg = _sig(av)
            sil = av * sg
            da = (ds * bv * (sg * (1.0 + av * (1.0 - sg)))).astype(BF16)
            db = (ds * sil).astype(BF16)
            da_ref[:, cols] = da
            db_ref[:, cols] = db
            if emit_sh:
                rest[0][:, cols] = (0.5 * sil * bv).astype(BF16)
            du = du + _nn(da, wg_ref[cols, :]) + _nn(db, wu_ref[cols, :])
        dx, dg = _rms_bwd(x, r, g_ref[...], du)
        dhi_ref[...] = dy + dx
        _accumulate(dg_ref, dg, pl.program_id(0) == 0)

    wide = [jax.ShapeDtypeStruct((n, f), BF16)] * (3 if emit_sh else 2)
    return _call(
        body, name, (n // tm,),
        [_row(tm, d), _row(tm, d), _fixed((1, d)), _row(tm, f), _row(tm, f), VMEM_WHOLE, VMEM_WHOLE, VMEM_WHOLE],
        [_row(tm, d)] + [_row(tm, f)] * len(wide) + [_fixed((1, d))],
        [jax.ShapeDtypeStruct((n, d), F32)] + wide + [jax.ShapeDtypeStruct((1, d), F32)],
        ("arbitrary",), (dh, h, g, a, b, wg, wu, wd), comm=comm)


def tn_matmul(x, y, name, out="f32", comm=None):
    n, k = x.shape
    m = y.shape[1]
    tm = _tile(n, TN_ROWS)
    kc, mc = k, (512 if m % 512 == 0 else m)
    while tm * kc * x.dtype.itemsize > TN_X_BYTES and kc % (2 * LANES) == 0:
        kc //= 2
    while tm * mc * y.dtype.itemsize > TN_Y_BYTES and mc % (2 * LANES) == 0:
        mc //= 2
    steps = n // tm

    def body(x_ref, y_ref, o_ref, *acc):
        i = pl.program_id(2)
        part = _tn(x_ref[...].astype(BF16), y_ref[...].astype(BF16))
        if steps == 1:
            o_ref[...] = part.astype(o_ref.dtype)
        elif out == "f32":
            _accumulate(o_ref, part, i == 0)
        else:
            _accumulate(acc[0], part, i == 0)

            @pl.when(i == steps - 1)
            def _():
                o_ref[...] = acc[0][...].astype(BF16)

    out_shape = jax.ShapeDtypeStruct((k, m), F32 if out == "f32" else BF16)
    (res,), landed = _call(
        body, name, (k // kc, m // mc, steps),
        [pl.BlockSpec((tm, kc), lambda a, b, i: (i, a)), pl.BlockSpec((tm, mc), lambda a, b, i: (i, b))],
        [pl.BlockSpec((kc, mc), lambda a, b, i: (a, b))], [out_shape], ("parallel", "parallel", "arbitrary"), (x, y),
        scratch=[pltpu.VMEM((kc, mc), F32)] if (out == "bf16" and steps > 1) else [], comm=comm)
    return (res, landed) if comm is not None else res


def inproj_bwd(dzm, du, dgate, dh2, h, g, wm, wl, comm=None):
    n, d = h.shape
    tm = _tile(n, 352)

    def body(dzm_ref, du_ref, dgt_ref, dh2_ref, h_ref, g_ref, wm_ref, wl_ref, dh_ref, dg_ref):
        x = h_ref[...]
        dun = (_nn(dzm_ref[...].astype(BF16), wm_ref[...])
               + _nn(du_ref[...].astype(BF16), wl_ref[:LRU_WIDTH, :])
               + _nn(dgt_ref[...].astype(BF16), wl_ref[LRU_WIDTH:, :]))
        dx, dg = _rms_bwd(x, _rms_r(x), g_ref[...], dun)
        dh_ref[...] = dh2_ref[...] + dx
        _accumulate(dg_ref, dg, pl.program_id(0) == 0)

    return _call(
        body, "inproj_bwd", (n // tm,),
        [_row(tm, MLA_IN), _row(tm, LRU_WIDTH), _row(tm, LRU_WIDTH), _row(tm, d), _row(tm, d),
         _fixed((1, d)), VMEM_WHOLE, VMEM_WHOLE],
        [_row(tm, d), _fixed((1, d))],
        [jax.ShapeDtypeStruct((n, d), F32), jax.ShapeDtypeStruct((1, d), F32)],
        ("arbitrary",), (dzm, du, dgate, dh2, h, g, wm, wl), comm=comm)


def _rope_tables(lp):
    pos = jnp.arange(lp, dtype=F32) - float(PAD)
    half = D_ROPE // 2
    inv_freq = ROPE_THETA ** (-jnp.arange(0, half, dtype=F32) / half)
    ang = pos[:, None] * inv_freq[None, :]
    cos, sin = jnp.cos(ang), jnp.sin(ang)
    one = jnp.ones((lp, D_NOPE), F32)
    z_nope = jnp.zeros((lp, D_NOPE), F32)
    z_half = jnp.zeros((lp, half), F32)
    z_tail = jnp.zeros((lp, HEAD_SLAB - D_QK), F32)
    cosr = jnp.concatenate([one, cos, cos, z_tail], axis=1)
    sin_up = jnp.concatenate([z_nope, z_half, sin, z_tail], axis=1)
    sin_dn = jnp.concatenate([z_nope, -sin, z_half, z_tail], axis=1)
    return cosr, sin_up, sin_dn


def _rope(x, cosr, sin_up, sin_dn):
    half = D_ROPE // 2
    return x * cosr + pltpu.roll(x, half, axis=1) * sin_up + pltpu.roll(x, HEAD_SLAB - half, axis=1) * sin_dn


def _rope_bwd(dy, cosr, sin_up, sin_dn):
    half = D_ROPE // 2
    return (dy * cosr + pltpu.roll(dy * sin_up, HEAD_SLAB - half, axis=1)
            + pltpu.roll(dy * sin_dn, half, axis=1))


def _k_rope_slab(zm_tile):
    tm = zm_tile.shape[0]
    krp = zm_tile[:, Q_RANK + KV_RANK:MLA_IN]
    return jnp.concatenate([jnp.zeros((tm, D_NOPE), F32), krp], axis=1)


def mla_prep_fwd(zm, gql, gkvl, wuq, wuk, wuv, gqh, gkh, tables, lp):
    n = zm.shape[0]
    tm = _tile(lp, 352)
    per_seq = lp // tm
    width = MLA_HEADS * HEAD_SLAB
    scale = 1.0 / math.sqrt(D_QK)

    def body(zm_ref, gql_ref, gkvl_ref, wuq_ref, wuk_ref, wuv_ref, gqh_ref, gkh_ref,
             cos_ref, up_ref, dn_ref, q_ref, k_ref, v_ref, qn_ref, cn_ref):
        z = zm_ref[...]
        cq = z[:, :Q_RANK]
        ckv = z[:, Q_RANK:Q_RANK + KV_RANK]
        qn = (cq * _rms_r(cq) * gql_ref[...]).astype(BF16)
        cn = (ckv * _rms_r(ckv) * gkvl_ref[...]).astype(BF16)
        qn_ref[...] = qn
        cn_ref[...] = cn
        q_raw = _nt(qn, wuq_ref[...])
        k_raw = _nt(cn, wuk_ref[...])
        v_ref[...] = _nt(cn, wuv_ref[...]).astype(BF16)
        kr_slab = _k_rope_slab(z)
        cosr, sin_up, sin_dn = cos_ref[...], up_ref[...], dn_ref[...]
        for hd in range(MLA_HEADS):
            cols = slice(hd * HEAD_SLAB, (hd + 1) * HEAD_SLAB)
            xq = q_raw[:, cols]
            yq = _rope(xq * _rms_r(xq, D_QK) * gqh_ref[...], cosr, sin_up, sin_dn)
            q_ref[:, cols] = (yq * scale).astype(BF16)
            xk = k_raw[:, cols] + kr_slab
            yk = _rope(xk * _rms_r(xk, D_QK) * gkh_ref[...], cosr, sin_up, sin_dn)
            k_ref[:, cols] = yk.astype(BF16)

    tab = pl.BlockSpec((tm, HEAD_SLAB), lambda i: (i % per_seq, 0))
    return pl.pallas_call(
        body, name="mla_prep_fwd", grid=(n // tm,),
        in_specs=[_row(tm, MLA_IN), _fixed((1, Q_RANK)), _fixed((1, KV_RANK)), VMEM_WHOLE, VMEM_WHOLE, VMEM_WHOLE,
                  _fixed((1, HEAD_SLAB)), _fixed((1, HEAD_SLAB)), tab, tab, tab],
        out_specs=[_row(tm, width), _row(tm, width), _row(tm, MLA_HEADS * D_V), _row(tm, Q_RANK), _row(tm, KV_RANK)],
        out_shape=[jax.ShapeDtypeStruct((n, width), BF16), jax.ShapeDtypeStruct((n, width), BF16),
                   jax.ShapeDtypeStruct((n, MLA_HEADS * D_V), BF16), jax.ShapeDtypeStruct((n, Q_RANK), BF16),
                   jax.ShapeDtypeStruct((n, KV_RANK), BF16)],
        compiler_params=_params(("parallel",)),
    )(zm, gql, gkvl, wuq, wuk, wuv, gqh, gkh, *tables)


def mla_prep_bwd(dq, dk, dv, zm, qn, cn, gql, gkvl, wuq, wuk, wuv, gqh, gkh, tables, lp, comm=None):
    n = zm.shape[0]
    tm = _tile(lp, 704)
    per_seq = lp // tm
    width = MLA_HEADS * HEAD_SLAB
    scale = 1.0 / math.sqrt(D_QK)

    def body(dq_ref, dk_ref, dv_ref, zm_ref, qn_ref, cn_ref, gql_ref, gkvl_ref, wuq_ref, wuk_ref, wuv_ref,
             gqh_ref, gkh_ref, cos_ref, up_ref, dn_ref,
             dzm_ref, dqr_ref, dkr_ref, dgql_ref, dgkvl_ref, dgqh_ref, dgkh_ref):
        z = zm_ref[...]
        cq = z[:, :Q_RANK]
        ckv = z[:, Q_RANK:Q_RANK + KV_RANK]
        q_raw = _nt(qn_ref[...], wuq_ref[...])
        k_raw = _nt(cn_ref[...], wuk_ref[...])
        kr_slab = _k_rope_slab(z)
        cosr, sin_up, sin_dn = cos_ref[...], up_ref[...], dn_ref[...]
        dgq = jnp.zeros((1, HEAD_SLAB), F32)
        dgk = jnp.zeros((1, HEAD_SLAB), F32)
        dkrp = jnp.zeros((tm, HEAD_SLAB - D_NOPE), F32)
        for hd in range(MLA_HEADS):
            cols = slice(hd * HEAD_SLAB, (hd + 1) * HEAD_SLAB)
            xq = q_raw[:, cols]
            dxn = _rope_bwd(dq_ref[:, cols] * scale, cosr, sin_up, sin_dn)
            dxq, dg = _rms_bwd(xq, _rms_r(xq, D_QK), gqh_ref[...], dxn, D_QK)
            dgq = dgq + dg
            dqr_ref[:, cols] = dxq.astype(BF16)
            xk = k_raw[:, cols] + kr_slab
            dxn = _rope_bwd(dk_ref[:, cols], cosr, sin_up, sin_dn)
            dxk, dg = _rms_bwd(xk, _rms_r(xk, D_QK), gkh_ref[...], dxn, D_QK)
            dgk = dgk + dg
            dkr_ref[:, cols] = dxk.astype(BF16)
            dkrp = dkrp + dxk[:, D_NOPE:]
        dqn = _nn(dqr_ref[...], wuq_ref[...])
        dcn = _nn(dkr_ref[...], wuk_ref[...]) + _nn(dv_ref[...].astype(BF16), wuv_ref[...])
        dcq, dg1 = _rms_bwd(cq, _rms_r(cq), gql_ref[...], dqn)
        dckv, dg2 = _rms_bwd(ckv, _rms_r(ckv), gkvl_ref[...], dcn)
        dzm_ref[:, :Q_RANK] = dcq
        dzm_ref[:, Q_RANK:Q_RANK + KV_RANK] = dckv
        dzm_ref[:, Q_RANK + KV_RANK:] = dkrp
        first = pl.program_id(0) == 0
        _accumulate(dgql_ref, dg1, first)
        _accumulate(dgkvl_ref, dg2, first)
        _accumulate(dgqh_ref, dgq, first)
        _accumulate(dgkh_ref, dgk, first)

    tab = pl.BlockSpec((tm, HEAD_SLAB), lambda i: (i % per_seq, 0))
    return _call(
        body, "mla_prep_bwd", (n // tm,),
        [_row(tm, width), _row(tm, width), _row(tm, MLA_HEADS * D_V), _row(tm, MLA_IN),
         _row(tm, Q_RANK), _row(tm, KV_RANK), _fixed((1, Q_RANK)), _fixed((1, KV_RANK)),
         VMEM_WHOLE, VMEM_WHOLE, VMEM_WHOLE, _fixed((1, HEAD_SLAB)), _fixed((1, HEAD_SLAB)), tab, tab, tab],
        [_row(tm, MLA_IN), _row(tm, width), _row(tm, width), _fixed((1, Q_RANK)), _fixed((1, KV_RANK)),
         _fixed((1, HEAD_SLAB)), _fixed((1, HEAD_SLAB))],
        [jax.ShapeDtypeStruct((n, MLA_IN), F32), jax.ShapeDtypeStruct((n, width), BF16),
         jax.ShapeDtypeStruct((n, width), BF16), jax.ShapeDtypeStruct((1, Q_RANK), F32),
         jax.ShapeDtypeStruct((1, KV_RANK), F32), jax.ShapeDtypeStruct((1, HEAD_SLAB), F32),
         jax.ShapeDtypeStruct((1, HEAD_SLAB), F32)],
        ("arbitrary",), (dq, dk, dv, zm, qn, cn, gql, gkvl, wuq, wuk, wuv, gqh, gkh, *tables), comm=comm)


def _attn_tile(lp):
    return _tile(lp, 704, CHUNK)


def _chunk_mask(i, j, t):
    qpos = i * t + lax.broadcasted_iota(jnp.int32, (t, t), 0)
    kpos = j * t + lax.broadcasted_iota(jnp.int32, (t, t), 1)
    same_or_earlier = jnp.right_shift(kpos, CHUNK_SHIFT) <= jnp.right_shift(qpos, CHUNK_SHIFT)
    return jnp.logical_and(same_or_earlier, kpos >= PAD)


def _masked_scores(s, i, j, t, diagonal):
    if diagonal:
        return jnp.where(_chunk_mask(i, j, t), s, NEG_INF)
    kpos = j * t + lax.broadcasted_iota(jnp.int32, (1, t), 1)
    return s + jnp.where(kpos < PAD, NEG_INF, 0.0)


def attn_fwd(q, k, v, nb, lp, comm=None):
    n = q.shape[0]
    t = _attn_tile(lp)
    nq = lp // t

    hp = ATTN_HEADS_PER_STEP

    def body(q_ref, k_ref, v_ref, o_ref, lse_ref):
        i = pl.program_id(2)
        qs = [q_ref[:, hh * HEAD_SLAB:(hh + 1) * HEAD_SLAB] for hh in range(hp)]

        def kv_step(j, carry, diagonal=False):
            off = pl.multiple_of(j * t, t)
            out = []
            for hh in range(hp):
                m, l, acc = carry[hh]
                kv = k_ref[pl.ds(off, t), hh * HEAD_SLAB:(hh + 1) * HEAD_SLAB]
                s = _masked_scores(_nt(qs[hh], kv), i, j, t, diagonal)
                m_new = jnp.maximum(m, jnp.max(s, axis=-1, keepdims=True))
                p = jnp.exp(s - m_new)
                alpha = jnp.exp(m - m_new)
                l = alpha * l + jnp.sum(p, axis=-1, keepdims=True)
                acc = alpha * acc + _nn(p.astype(BF16), v_ref[pl.ds(off, t), hh * D_V:(hh + 1) * D_V])
                out.append((m_new, l, acc))
            return tuple(out)

        init = tuple((jnp.full((t, 1), NEG_INF, F32), jnp.zeros((t, 1), F32), jnp.zeros((t, D_V), F32))
                     for _ in range(hp))
        done = kv_step(i, lax.fori_loop(0, i, kv_step, init), diagonal=True)
        for hh, (m, l, acc) in enumerate(done):
            o_ref[:, hh * D_V:(hh + 1) * D_V] = acc * (1.0 / l)
            lse_ref[hh] = jnp.broadcast_to(m + jnp.log(l), (t, LANES))

    return _call(
        body, "attn_fwd", (nb, MLA_HEADS // hp, nq),
        [pl.BlockSpec((t, hp * HEAD_SLAB), lambda b, h, i: (b * nq + i, h)),
         pl.BlockSpec((lp, hp * HEAD_SLAB), lambda b, h, i: (b, h)),
         pl.BlockSpec((lp, hp * D_V), lambda b, h, i: (b, h))],
        [pl.BlockSpec((t, hp * D_V), lambda b, h, i: (b * nq + i, h)),
         pl.BlockSpec((hp, t, LANES), lambda b, h, i: (h, b * nq + i, 0))],
        [jax.ShapeDtypeStruct((n, MLA_HEADS * D_V), F32), jax.ShapeDtypeStruct((MLA_HEADS, n, LANES), F32)],
        ("parallel", "parallel", "parallel"), (q, k, v), comm=comm)


def attn_bwd(q, k, v, o, do, lse, nb, lp, comm=None):
    n = q.shape[0]
    t = _attn_tile(lp)
    nq = lp // t

    def body(q_ref, k_ref, v_ref, o_ref, do_ref, lse_ref, dq_ref, dk_ref, dv_ref):
        dk_ref[...] = jnp.zeros_like(dk_ref)
        dv_ref[...] = jnp.zeros_like(dv_ref)

        def q_step(i, _):
            qoff = pl.multiple_of(i * t, t)
            qv = q_ref[pl.ds(qoff, t), :]
            dov = do_ref[pl.ds(qoff, t), :]
            delta = jnp.sum(o_ref[pl.ds(qoff, t), :] * dov, axis=-1, keepdims=True)
            lse_q = jnp.max(lse_ref[0, pl.ds(qoff, t), :], axis=-1, keepdims=True)
            do16 = dov.astype(BF16)

            def kv_step(j, dq_acc, diagonal=False):
                koff = pl.multiple_of(j * t, t)
                kv = k_ref[pl.ds(koff, t), :]
                s = _masked_scores(_nt(qv, kv), i, j, t, diagonal)
                p = jnp.exp(s - lse_q)
                dp = _nt(do16, v_ref[pl.ds(koff, t), :])
                ds16 = (p * (dp - delta)).astype(BF16)
                dv_ref[pl.ds(koff, t), :] += _tn(p.astype(BF16), do16)
                dk_ref[pl.ds(koff, t), :] += _tn(ds16, qv)
                return dq_acc + _nn(ds16, kv)

            earlier = lax.fori_loop(0, i, kv_step, jnp.zeros((t, HEAD_SLAB), F32))
            dq_ref[pl.ds(qoff, t), :] = kv_step(i, earlier, diagonal=True)
            return 0

        lax.fori_loop(0, nq, q_step, 0)

    wide = pl.BlockSpec((lp, HEAD_SLAB), lambda b, h: (b, h))
    thin = pl.BlockSpec((lp, D_V), lambda b, h: (b, h))
    width = MLA_HEADS * HEAD_SLAB
    return _call(
        body, "attn_bwd", (nb, MLA_HEADS),
        [wide, wide, thin, thin, thin, pl.BlockSpec((1, lp, LANES), lambda b, h: (h, b, 0))],
        [wide, wide, thin],
        [jax.ShapeDtypeStruct((n, width), F32), jax.ShapeDtypeStruct((n, width), F32),
         jax.ShapeDtypeStruct((n, MLA_HEADS * D_V), F32)],
        ("parallel", "parallel"), (q, k, v, o, do, lse), comm=comm)


def _seq_rows(nb, lp, width):
    rows = lax.broadcasted_iota(jnp.int32, (lp, width), 0)
    return jnp.concatenate([rows] * nb, axis=0) if nb > 1 else rows


def _lru_gates(u, w_ref, cb, wa, wx, ba, bx, lam):
    xc = (cb + w_ref[pl.ds(3, 1), :] * u + w_ref[pl.ds(2, 1), :] * pltpu.roll(u, 1, axis=0)
          + w_ref[pl.ds(1, 1), :] * pltpu.roll(u, 2, axis=0) + w_ref[pl.ds(0, 1), :] * pltpu.roll(u, 3, axis=0))
    xc16 = xc.astype(BF16)
    ra = _sig_tanh(_nn(xc16, wa) + ba)
    ia = _sig_tanh(_nn(xc16, wx) + bx)
    sp = _softplus(-lam)
    log_a = -C_RGLRU * ra * sp
    a = jnp.exp(log_a)
    x2 = 2.0 * log_a
    mult = jnp.sqrt(jnp.where(x2 > -1e-2, -x2 * (1.0 + x2 * (0.5 + x2 * (1.0 / 6.0))), 1.0 - a * a))
    return xc, xc16, ra, ia, sp, a, mult


def _scan_block_rows(width):
    return lax.broadcasted_iota(jnp.int32, (8, width), 0)


def lru_fwd(zl, conv_w, conv_b, wa, wx, ba, bx, lam, nb, lp, comm=None):
    n = zl.shape[0]
    w = LRU_TILE
    nt = LRU_WIDTH // w
    nblk = lp // 8

    def body(u_ref, gt_ref, cw_ref, cb_ref, wa_ref, wx_ref, ba_ref, bx_ref, lam_ref, y_ref, h_ref, a_s, b_s):
        u = u_ref[...]
        xc, _, _, ia, _, a, mult = _lru_gates(u, cw_ref, cb_ref[...], wa_ref[...], wx_ref[...],
                                              ba_ref[...], bx_ref[...], lam_ref[...])
        row = _seq_rows(nb, lp, w)
        mult = jnp.where(row == PAD, 1.0, mult)
        a_s[...] = a
        b_s[...] = jnp.where(row < PAD, 0.0, mult * (ia * xc))
        r8 = _scan_block_rows(w)

        def blk(i, carry):
            out = []
            for s_id in range(nb):
                off = pl.multiple_of(s_id * lp + i * 8, 8)
                av = a_s[pl.ds(off, 8), :]
                bv = b_s[pl.ds(off, 8), :]
                for sh in (1, 2, 4):
                    keep = r8 >= sh
                    bv = jnp.where(keep, av * pltpu.roll(bv, sh, axis=0) + bv, bv)
                    av = jnp.where(keep, av * pltpu.roll(av, sh, axis=0), av)
                hv = bv + av * carry[s_id]
                h_ref[pl.ds(off, 8), :] = hv
                out.append(jnp.sum(jnp.where(r8 == 7, hv, 0.0), axis=0, keepdims=True))
            return tuple(out)

        lax.fori_loop(0, nblk, blk, tuple(jnp.zeros((1, w), F32) for _ in range(nb)))
        gelu, _ = _gelu_and_grad(gt_ref[...])
        y_ref[...] = h_ref[...] * gelu

    col = lambda c: (0, c)
    return _call(
        body, "lru_fwd", (nt,),
        [pl.BlockSpec((n, w), col), pl.BlockSpec((n, w), lambda c: (0, nt + c)),
         pl.BlockSpec((CONV_W, w), col), pl.BlockSpec((1, w), col),
         pl.BlockSpec((w, w), lambda c: (c, c)), pl.BlockSpec((w, w), lambda c: (c, c)),
         pl.BlockSpec((1, w), col), pl.BlockSpec((1, w), col), pl.BlockSpec((1, w), col)],
        [pl.BlockSpec((n, w), col), pl.BlockSpec((n, w), col)],
        [jax.ShapeDtypeStruct((n, LRU_WIDTH), F32), jax.ShapeDtypeStruct((n, LRU_WIDTH), F32)],
        ("parallel",), (zl, zl, conv_w, conv_b, wa, wx, ba, bx, lam),
        scratch=[pltpu.VMEM((n, w), F32), pltpu.VMEM((n, w), F32)], comm=comm)


def lru_bwd(zl, hs, dy, conv_w, conv_b, wa, wx, ba, bx, lam, nb, lp, comm=None):
    n = zl.shape[0]
    w = LRU_TILE
    nt = LRU_WIDTH // w
    nblk = lp // 8

    def body(u_ref, gt_ref, h_ref, dy_ref, cw_ref, cb_ref, wa_ref, wx_ref, ba_ref, bx_ref, lam_ref,
             du_ref, dgt_ref, dcw_ref, dcb_ref, dba_ref, dbx_ref, dlam_ref, dwa_ref, dwx_ref,
             c_s, d_s, g_s, dwa_s, dwx_s):
        u = u_ref[...]
        lam = lam_ref[...]
        xc, xc16, ra, ia, sp, a, mult = _lru_gates(u, cw_ref, cb_ref[...], wa_ref[...], wx_ref[...],
                                                   ba_ref[...], bx_ref[...], lam)
        row = lax.broadcasted_iota(jnp.int32, (lp, w), 0)
        hv = h_ref[...]
        dyv = dy_ref[...]
        gelu, dgelu = _gelu_and_grad(gt_ref[...])
        dgt_ref[...] = jnp.where(row >= PAD, dyv * hv * dgelu, 0.0)
        c_s[...] = pltpu.roll(a, lp - 1, axis=0)
        d_s[...] = dyv * gelu
        r8 = _scan_block_rows(w)

        def blk(ii, carry):
            off = pl.multiple_of((nblk - 1 - ii) * 8, 8)
            cv = c_s[pl.ds(off, 8), :]
            dv = d_s[pl.ds(off, 8), :]
            for sh in (1, 2, 4):
                keep = r8 < 8 - sh
                dv = jnp.where(keep, cv * pltpu.roll(dv, 8 - sh, axis=0) + dv, dv)
                cv = jnp.where(keep, cv * pltpu.roll(cv, 8 - sh, axis=0), cv)
            gv = dv + cv * carry
            g_s[pl.ds(off, 8), :] = gv
            return jnp.sum(jnp.where(r8 == 0, gv, 0.0), axis=0, keepdims=True)

        lax.fori_loop(0, nblk, blk, jnp.zeros((1, w), F32))
        gv = g_s[...]
        first_row = row == PAD
        db = jnp.where(row >= PAD, gv, 0.0)
        da = jnp.where(row > PAD, gv * pltpu.roll(hv, 1, axis=0), 0.0)
        mult_eff = jnp.where(first_row, 1.0, mult)
        dmult = jnp.where(first_row, 0.0, db * (ia * xc))
        dia = db * mult_eff * xc
        dxc = db * mult_eff * ia
        dla = da * a - dmult * (a * a) / mult
        dra = dla * (-C_RGLRU * sp)
        dsp = jnp.sum(dla * (-C_RGLRU * ra), axis=0, keepdims=True)
        dpa = dra * ra * (1.0 - ra)
        dpx = dia * ia * (1.0 - ia)
        dpa16 = dpa.astype(BF16)
        dpx16 = dpx.astype(BF16)
        dxc = dxc + _nt(dpa16, wa_ref[...]) + _nt(dpx16, wx_ref[...])
        du = cw_ref[pl.ds(CONV_W - 1, 1), :] * dxc
        dcw = [jnp.sum(dxc * u, axis=0, keepdims=True)]
        for tap in range(1, CONV_W):
            dcw.insert(0, jnp.sum(dxc * pltpu.roll(u, tap, axis=0), axis=0, keepdims=True))
            du = du + cw_ref[pl.ds(CONV_W - 1 - tap, 1), :] * pltpu.roll(dxc, lp - tap, axis=0)
        du_ref[...] = jnp.where(row >= PAD, du, 0.0)
        first = pl.program_id(1) == 0
        _accumulate(dlam_ref, -_sig(-lam) * dsp, first)
        _accumulate(dba_ref, jnp.sum(dpa, axis=0, keepdims=True), first)
        _accumulate(dbx_ref, jnp.sum(dpx, axis=0, keepdims=True), first)
        _accumulate(dcb_ref, jnp.sum(dxc, axis=0, keepdims=True), first)
        _accumulate(dcw_ref, jnp.concatenate(dcw, axis=0), first)
        _accumulate(dwa_s, _tn(xc16, dpa16), first)
        _accumulate(dwx_s, _tn(xc16, dpx16), first)

        @pl.when(pl.program_id(1) == nb - 1)
        def _():
            for j in range(w // LRU_BLOCK):
                blk_rows = slice(j * LRU_BLOCK, (j + 1) * LRU_BLOCK)
                dwa_ref[0, blk_rows, :] = dwa_s[blk_rows, blk_rows]
                dwx_ref[0, blk_rows, :] = dwx_s[blk_rows, blk_rows]

    col = lambda c, b: (0, c)
    vec = pl.BlockSpec((1, w), col)
    mat = pl.BlockSpec((w, w), lambda c, b: (c, c))
    big = pl.BlockSpec((lp, w), lambda c, b: (b, c))
    dmat = pl.BlockSpec((1, w, LRU_BLOCK), lambda c, b: (c, 0, 0))
    return _call(
        body, "lru_bwd", (nt, nb),
        [big, pl.BlockSpec((lp, w), lambda c, b: (b, nt + c)), big, big,
         pl.BlockSpec((CONV_W, w), col), vec, mat, mat, vec, vec, vec],
        [big, big, pl.BlockSpec((CONV_W, w), col), vec, vec, vec, vec, dmat, dmat],
        [jax.ShapeDtypeStruct((n, LRU_WIDTH), F32), jax.ShapeDtypeStruct((n, LRU_WIDTH), F32),
         jax.ShapeDtypeStruct((CONV_W, LRU_WIDTH), F32), jax.ShapeDtypeStruct((1, LRU_WIDTH), F32),
         jax.ShapeDtypeStruct((1, LRU_WIDTH), F32), jax.ShapeDtypeStruct((1, LRU_WIDTH), F32),
         jax.ShapeDtypeStruct((1, LRU_WIDTH), F32), jax.ShapeDtypeStruct((nt, w, LRU_BLOCK), F32),
         jax.ShapeDtypeStruct((nt, w, LRU_BLOCK), F32)],
        ("parallel", "arbitrary"), (zl, zl, hs, dy, conv_w, conv_b, wa, wx, ba, bx, lam),
        scratch=[pltpu.VMEM((lp, w), F32), pltpu.VMEM((lp, w), F32), pltpu.VMEM((lp, w), F32),
                 pltpu.VMEM((w, w), F32), pltpu.VMEM((w, w), F32)], comm=comm)


def outproj_fwd(h, ya, yl, gao, glo, wout):
    n, d = h.shape
    half = ya.shape[1]
    tm = _tile(n, 704)

    def body(h_ref, ya_ref, yl_ref, gao_ref, glo_ref, w_ref, ho_ref, yn_ref):
        xa = ya_ref[...]
        xl = yl_ref[...]
        na = (xa * _rms_r(xa) * gao_ref[...]).astype(BF16)
        nl = (xl * _rms_r(xl) * glo_ref[...]).astype(BF16)
        yn_ref[:, :half] = na
        yn_ref[:, half:] = nl
        ho_ref[...] = h_ref[...] + _nn(na, w_ref[:half, :]) + _nn(nl, w_ref[half:, :])

    return pl.pallas_call(
        body, name="outproj_fwd", grid=(n // tm,),
        in_specs=[_row(tm, d), _row(tm, half), _row(tm, half), _fixed((1, half)), _fixed((1, half)), VMEM_WHOLE],
        out_specs=[_row(tm, d), _row(tm, 2 * half)],
        out_shape=[jax.ShapeDtypeStruct((n, d), F32), jax.ShapeDtypeStruct((n, 2 * half), BF16)],
        compiler_params=_params(("parallel",)),
    )(h, ya, yl, gao, glo, wout)


def outproj_bwd(dh, ya, yl, gao, glo, wout):
    n, d = dh.shape
    half = ya.shape[1]
    tm = _tile(n, 704)

    def body(dh_ref, ya_ref, yl_ref, gao_ref, glo_ref, w_ref, dya_ref, dyl_ref, dgao_ref, dglo_ref):
        d16 = dh_ref[...].astype(BF16)
        xa = ya_ref[...]
        xl = yl_ref[...]
        dxa, dga = _rms_bwd(xa, _rms_r(xa), gao_ref[...], _nt(d16, w_ref[:half, :]))
        dxl, dgl = _rms_bwd(xl, _rms_r(xl), glo_ref[...], _nt(d16, w_ref[half:, :]))
        dya_ref[...] = dxa
        dyl_ref[...] = dxl
        first = pl.program_id(0) == 0
        _accumulate(dgao_ref, dga, first)
        _accumulate(dglo_ref, dgl, first)

    return pl.pallas_call(
        body, name="outproj_bwd", grid=(n // tm,),
        in_specs=[_row(tm, d), _row(tm, half), _row(tm, half), _fixed((1, half)), _fixed((1, half)), VMEM_WHOLE],
        out_specs=[_row(tm, half), _row(tm, half), _fixed((1, half)), _fixed((1, half))],
        out_shape=[jax.ShapeDtypeStruct((n, half), F32), jax.ShapeDtypeStruct((n, half), F32),
                   jax.ShapeDtypeStruct((1, half), F32), jax.ShapeDtypeStruct((1, half), F32)],
        compiler_params=_params(("arbitrary",)),
    )(dh, ya, yl, gao, glo, wout)


def _loss_and_grad(x, gv, tgt, first_row):
    tm, d = x.shape
    r = _rms_r(x)
    row = first_row + lax.broadcasted_iota(jnp.int32, (tm, d), 0)
    diff = jnp.where(row >= FIRST_FRAME, x * r * gv - tgt, 0.0)
    part = 0.5 * jnp.sum(jnp.sum(diff * diff, axis=-1, keepdims=True) * (1.0 / d), axis=0, keepdims=True)
    dx, dg = _rms_bwd(x, r, gv, diff * (1.0 / d))
    return dx, part, dg


def assemble_cols(g, name):
    _, k, ns = g.shape

    def body(g_ref, o_ref):
        for j in range(N_DEV):
            o_ref[:, j * ns:(j + 1) * ns] = g_ref[j]

    return pl.pallas_call(body, name=name, out_shape=jax.ShapeDtypeStruct((k, N_DEV * ns), g.dtype),
                          compiler_params=_params(None))(g)


def split_cols(x, name):
    k, cols = x.shape
    ns = cols // N_DEV

    def body(x_ref, o_ref):
        for j in range(N_DEV):
            o_ref[j] = x_ref[:, j * ns:(j + 1) * ns]

    return pl.pallas_call(body, name=name, out_shape=jax.ShapeDtypeStruct((N_DEV, k, ns), x.dtype),
                          compiler_params=_params(None))(x)


def _slab_rows(w, per_head):
    k = w.shape[1]
    w = w.reshape(MLA_HEADS, per_head, k)
    return jnp.pad(w, ((0, 0), (0, HEAD_SLAB - per_head), (0, 0))).reshape(MLA_HEADS * HEAD_SLAB, k)


def _unslab_rows(w, per_head):
    k = w.shape[1]
    return w.reshape(MLA_HEADS, HEAD_SLAB, k)[:, :per_head].reshape(MLA_HEADS * per_head, k)


def meta_grad(dh0, nb, lp):
    d = dh0.shape[1]
    ns = d // N_DEV
    per_seq = lp // N_META

    def body(x_ref, o_ref):
        x = x_ref[...]
        for j in range(N_DEV):
            _accumulate(o_ref.at[j], x[:, j * ns:(j + 1) * ns], pl.program_id(0) == 0)

    return pl.pallas_call(
        body, name="meta_grad", grid=(nb,),
        in_specs=[pl.BlockSpec((N_META, d), lambda b: (b * per_seq + PAD // N_META, 0))],
        out_specs=pl.BlockSpec((N_DEV, N_META, ns), lambda b: (0, 0, 0)),
        out_shape=jax.ShapeDtypeStruct((N_DEV, N_META, ns), F32),
        compiler_params=_params(("arbitrary",)))(dh0)


VECTORS = [("ffn1_norm", 1024), ("mix_norm", 1024), ("q_latent_norm", 384), ("kv_latent_norm", 256),
           ("q_head_norm", 192), ("k_head_norm", 192), ("conv_b", 512), ("gate_a_b", 512), ("gate_x_b", 512),
           ("lru_lambda", 512), ("attn_out_norm", 512), ("lru_out_norm", 512), ("ffn2_norm", 1024),
           ("final_norm", 1024)]
VEC_ROWS = 16
LOSS_ROW = len(VECTORS)
GATES = ["gate_a_w", "gate_x_w"]


def pack_vectors(grads, loss):
    def body(*refs):
        o_ref = refs[-1]
        o_ref[...] = jnp.zeros_like(o_ref)
        for t, (ref, (_, cnt)) in enumerate(zip(refs[:-2], VECTORS)):
            o_ref[t:t + 1, :cnt] = ref[:, :cnt]
        o_ref[LOSS_ROW:LOSS_ROW + 1, :LANES] = refs[-2][...]

    return pl.pallas_call(body, name="pack_vectors", out_shape=jax.ShapeDtypeStruct((VEC_ROWS, D_MODEL), F32),
                          compiler_params=_params(None))(*[grads[name] for name, _ in VECTORS], loss)


def _adamw_update(w, g, m, v):
    c1 = 1.0 / (1.0 - ADAM_B1 ** ADAM_STEP)
    c2 = 1.0 / (1.0 - ADAM_B2 ** ADAM_STEP)
    mn = ADAM_B1 * m + (1.0 - ADAM_B1) * g
    vn = ADAM_B2 * v + (1.0 - ADAM_B2) * (g * g)
    delta = -ADAM_LR * ((mn * c1) / (jnp.sqrt(vn * c2) + ADAM_EPS) + ADAM_WD * w)
    return delta, mn, vn


def _sum_slots(ref, index=()):
    acc = ref[(0,) + index].astype(F32)
    for s in range(1, N_DEV):
        acc = acc + ref[(s,) + index].astype(F32)
    return acc


def adamw_sharded(r, w, m, v, name):
    rows, cols = w.shape
    tr = _tile(rows, 256, 16) if rows % 16 == 0 else rows

    def body(r_ref, w_ref, m_ref, v_ref, g_ref, d_ref, mo_ref, vo_ref):
        g = _sum_slots(r_ref)
        g_ref[...] = g
        d_ref[...], mo_ref[...], vo_ref[...] = _adamw_update(w_ref[...], g, m_ref[...], v_ref[...])

    spec = pl.BlockSpec((tr, cols), lambda i: (i, 0))
    shape = jax.ShapeDtypeStruct((rows, cols), F32)
    return pl.pallas_call(
        body, name=name, grid=(rows // tr,),
        in_specs=[pl.BlockSpec((N_DEV, tr, cols), lambda i: (0, i, 0))] + [spec] * 3,
        out_specs=[spec] * 4, out_shape=[shape] * 4,
        compiler_params=_params(("parallel",)),
    )(r, w, m, v)


def adamw_small(r_vec, r_gates, w, m, v):
    nt = len(VECTORS) + len(GATES)

    def body(*refs):
        rv_ref = refs[0]
        rg_refs = refs[1:1 + len(GATES)]
        base = 1 + len(GATES)
        w_refs, m_refs, v_refs = (refs[base + i * nt:base + (i + 1) * nt] for i in range(3))
        outs = refs[base + 3 * nt:]
        g_o, d_o, m_o, v_o = (outs[i * nt:(i + 1) * nt] for i in range(4))
        outs[4 * nt][...] = _sum_slots(rv_ref, (slice(LOSS_ROW, LOSS_ROW + 1), slice(0, LANES)))
        for t in range(nt):
            if t < len(VECTORS):
                cnt = VECTORS[t][1]
                g = _sum_slots(rv_ref, (slice(t, t + 1), slice(0, cnt)))
            else:
                g = _sum_slots(rg_refs[t - len(VECTORS)])
            g_o[t][...] = g
            d_o[t][...], m_o[t][...], v_o[t][...] = _adamw_update(w_refs[t][...], g, m_refs[t][...], v_refs[t][...])

    shapes = [jax.ShapeDtypeStruct(a.shape, F32) for a in w]
    res = pl.pallas_call(body, name="adamw_small", out_shape=shapes * 4 + [jax.ShapeDtypeStruct((1, LANES), F32)],
                         compiler_params=_params(None))(r_vec, *r_gates, *w, *m, *v)
    return [res[i * nt:(i + 1) * nt] for i in range(4)], res[4 * nt]


def _block_diag(w):
    nb, n, _ = w.shape
    eye = jnp.eye(nb, dtype=w.dtype)
    return (eye[:, None, :, None] * w[:, :, None, :]).reshape(nb * n, nb * n)


def _two_d(a):
    if a.ndim == 3:
        return a.reshape(a.shape[1], a.shape[2])
    if a.ndim == 4:
        return a.reshape(a.shape[1] * a.shape[2], a.shape[3])
    return a


_WEIGHT_NAMES = ['meta_tokens', 'ffn1_norm', 'ffn1_w_gate', 'ffn1_w_up', 'ffn1_w_down', 'mix_norm', 'w_in',
                 'q_latent_norm', 'w_uq', 'kv_latent_norm', 'w_uk', 'w_uv', 'q_head_norm', 'k_head_norm', 'conv_w',
                 'conv_b', 'gate_a_w', 'gate_a_b', 'gate_x_w', 'gate_x_b', 'lru_lambda', 'attn_out_norm',
                 'lru_out_norm', 'w_out', 'ffn2_norm', 'ffn2_w_gate', 'ffn2_w_up', 'ffn2_w_down', 'final_norm']


COLUMN_SHARDED = ("ffn1_w_gate", "ffn1_w_up", "ffn2_w_gate", "ffn2_w_up", "w_in", "w_uq", "w_uk", "w_uv")


def train_step(x, tgt, w, m, v):
    nb, seq, d = x.shape
    lp = PAD + N_META + seq
    n = nb * lp

    def local(a, name):
        a = _two_d(a)
        return a.T if name in COLUMN_SHARDED else a

    sh = {name: local(w[name], name) for name in _WEIGHT_NAMES}
    m2 = {name: local(m[name], name) for name in _WEIGHT_NAMES}
    v2 = {name: local(v[name], name) for name in _WEIGHT_NAMES}

    def b16(name):
        return sh[name].astype(BF16)

    out = {}

    def update(name, landed):
        out[name] = adamw_sharded(landed, sh[name], m2[name], v2[name], "adamw_" + name)

    g_meta, g_conv, g_wg1, g_wu1 = exchange(
        [sh["meta_tokens"], sh["conv_w"], b16("ffn1_w_gate"), b16("ffn1_w_up")], ["gather"] * 4, "gather_ffn1")
    wg1, wu1 = g_wg1.reshape(D_FF, d), g_wu1.reshape(D_FF, d)
    meta = assemble_cols(g_meta, "assemble_meta")
    conv_w = assemble_cols(g_conv, "assemble_conv")

    front = jnp.concatenate([jnp.zeros((PAD, d), F32), meta], axis=0)
    h0 = jnp.concatenate([jnp.broadcast_to(front[None], (nb, FIRST_FRAME, d)), x], axis=1).reshape(n, d)
    tgt_p = jnp.concatenate([jnp.zeros((nb, FIRST_FRAME, d), F32), tgt], axis=1).reshape(n, d)
    tables = _rope_tables(lp)
    zero_tail = jnp.zeros((1, HEAD_SLAB - D_QK), F32)
    gqh = jnp.concatenate([sh["q_head_norm"], zero_tail], axis=1)
    gkh = jnp.concatenate([sh["k_head_norm"], zero_tail], axis=1)
    wa = _block_diag(w["gate_a_w"][0]).astype(BF16)
    wx = _block_diag(w["gate_x_w"][0]).astype(BF16)

    (u1, a1, b1, s1), (g_wd1, g_in) = ffn_up(h0, sh["ffn1_norm"], wg1, wu1, "ffn1_up",
                                             comm=([b16("ffn1_w_down"), b16("w_in")], ["gather"] * 2))
    wd1 = g_wd1.reshape(D_FF, d)
    mla_rows = MLA_IN - D_ROPE
    w_in = g_in.reshape(mla_rows + 2 * LRU_WIDTH, d)
    wm = jnp.concatenate([w_in[:mla_rows], jnp.zeros((D_ROPE, d), BF16)], axis=0)
    wl = w_in[mla_rows:]
    (h1, u2, zm, zl), (g_uq, g_uk, g_uv, g_out) = ffn_down_inproj(
        h0, s1, wd1, sh["mix_norm"], wm, wl, "ffn1_down_inproj",
        comm=([b16("w_uq"), b16("w_uk"), b16("w_uv"), b16("w_out")], ["gather"] * 4))
    wuq = _slab_rows(g_uq.reshape(MLA_HEADS * D_QK, Q_RANK), D_QK)
    wuk = _slab_rows(g_uk.reshape(MLA_HEADS * D_NOPE, KV_RANK), D_NOPE)
    wuv = g_uv.reshape(MLA_HEADS * D_V, KV_RANK)
    w_out = g_out.reshape(d, d)

    q, k, vv, qn, cn = mla_prep_fwd(zm, sh["q_latent_norm"], sh["kv_latent_norm"], wuq, wuk, wuv, gqh, gkh, tables, lp)
    (y_mla, lse), (g_wu2, g_wd2) = attn_fwd(
        q, k, vv, nb, lp, comm=([b16("ffn2_w_up"), b16("ffn2_w_down")], ["gather"] * 2))
    (y_lru, hs), (g_wg2,) = lru_fwd(zl, conv_w, sh["conv_b"], wa, wx, sh["gate_a_b"], sh["gate_x_b"], sh["lru_lambda"],
                                    nb, lp, comm=([b16("ffn2_w_gate")], ["gather"]))
    wg2, wu2, wd2 = (g.reshape(D_FF, d) for g in (g_wg2, g_wu2, g_wd2))
    h2, yn = outproj_fwd(h1, y_mla, y_lru, sh["attn_out_norm"], sh["lru_out_norm"], w_out)
    dh3, u3, a3, b3, loss, g_final = ffn_fwd_loss(h2, sh["ffn2_norm"], wg2, wu2, wd2, sh["final_norm"], tgt_p, lp,
                                                  "ffn2_fwd_loss")

    vec = {"final_norm": g_final}
    (dh2, da3, db3, sh3, vec["ffn2_norm"]), _ = ffn_bwd_act(dh3, h2, sh["ffn2_norm"], a3, b3, wg2, wu2, wd2, "ffn2_bwd")
    ff_shards = (N_DEV, D_FF // N_DEV, d)
    dwg2 = tn_matmul(da3, u3, "ffn2_dwg", "bf16").reshape(ff_shards)
    dwu2 = tn_matmul(db3, u3, "ffn2_dwu", "bf16").reshape(ff_shards)
    dwd2 = tn_matmul(sh3, dh3, "ffn2_dwd", "bf16").reshape(ff_shards)

    dy_mla, dy_lru, vec["attn_out_norm"], vec["lru_out_norm"] = outproj_bwd(
        dh2, y_mla, y_lru, sh["attn_out_norm"], sh["lru_out_norm"], w_out)
    dw_out = tn_matmul(yn, dh2, "dw_out", "bf16").reshape(N_DEV, d // N_DEV, d)
    (du, dgate, dconv, vec["conv_b"], vec["gate_a_b"], vec["gate_x_b"], vec["lru_lambda"], dga, dgx), landed = lru_bwd(
        zl, hs, dy_lru, conv_w, sh["conv_b"], wa, wx, sh["gate_a_b"], sh["gate_x_b"], sh["lru_lambda"], nb, lp,
        comm=([dwg2, dw_out], ["scatter"] * 2))
    update("ffn2_w_gate", landed[0])
    update("w_out", landed[1])

    (dq, dk, dv), (r_wu2,) = attn_bwd(q, k, vv, y_mla, dy_mla, lse, nb, lp, comm=([dwu2], ["scatter"]))
    update("ffn2_w_up", r_wu2)

    (dzm, dqr, dkr, vec["q_latent_norm"], vec["kv_latent_norm"], vec["q_head_norm"], vec["k_head_norm"]), (r_wd2,) = (
        mla_prep_bwd(dq, dk, dv, zm, qn, cn, sh["q_latent_norm"], sh["kv_latent_norm"], wuq, wuk, wuv, gqh, gkh,
                     tables, lp, comm=([dwd2], ["scatter"])))
    update("ffn2_w_down", r_wd2)
    dwuq = _unslab_rows(tn_matmul(dqr, qn, "dw_uq", "bf16"), D_QK).reshape(N_DEV, -1, Q_RANK)
    dwuk = _unslab_rows(tn_matmul(dkr, cn, "dw_uk", "bf16"), D_NOPE).reshape(N_DEV, -1, KV_RANK)
    dwuv = tn_matmul(dv, cn, "dw_uv", "bf16").reshape(N_DEV, -1, KV_RANK)
    (dh1, vec["mix_norm"]), landed = inproj_bwd(dzm, du, dgate, dh2, h1, sh["mix_norm"], wm, wl,
                                                comm=([dwuq, dwuk, dwuv], ["scatter"] * 3))
    for name, r in zip(("w_uq", "w_uk", "w_uv"), landed):
        update(name, r)
    dw_in = jnp.concatenate([tn_matmul(dzm, u2, "dw_in_mla", "bf16")[:mla_rows], tn_matmul(du, u2, "dw_in_u", "bf16"),
                             tn_matmul(dgate, u2, "dw_in_gate", "bf16")], axis=0).reshape(N_DEV, -1, d)

    dwd1 = tn_matmul(s1, dh1, "ffn1_dwd", "bf16").reshape(ff_shards)
    (dh0, da1, db1, vec["ffn1_norm"]), landed = ffn_bwd_act(
        dh1, h0, sh["ffn1_norm"], a1, b1, wg1, wu1, wd1, "ffn1_bwd", emit_sh=False,
        comm=([dw_in, split_cols(dconv, "split_conv"), dwd1], ["scatter"] * 3))
    for name, r in zip(("w_in", "conv_w", "ffn1_w_down"), landed):
        update(name, r)

    dwg1 = tn_matmul(da1, u1, "ffn1_dwg", "bf16").reshape(ff_shards)
    dwu1, (r_wg1,) = tn_matmul(db1, u1, "ffn1_dwu", "bf16", comm=([dwg1], ["scatter"]))
    dmeta = meta_grad(dh0, nb, lp)
    gates = [dga.reshape(LRU_WIDTH, LRU_BLOCK), dgx.reshape(LRU_WIDTH, LRU_BLOCK)]
    r_vec, r_ga, r_gx, r_meta, r_wu1 = exchange(
        [pack_vectors(vec, loss)] + gates + [dmeta, dwu1.reshape(ff_shards)], ["gather"] * 3 + ["scatter"] * 2,
        "exchange_last")
    update("ffn1_w_gate", r_wg1)
    update("ffn1_w_up", r_wu1)
    update("meta_tokens", r_meta)

    small = [name for name, _ in VECTORS] + GATES
    res, total_loss = adamw_small(r_vec, [r_ga, r_gx], [sh[nm] for nm in small], [m2[nm] for nm in small],
                                  [v2[nm] for nm in small])
    for i, name in enumerate(small):
        out[name] = [res[j][i] for j in range(4)]

    grad_x = dh0.reshape(nb, lp, d)[:, FIRST_FRAME:]
    loss = total_loss[0, 0]

    def as_given(a, name):
        return (a.T if name in COLUMN_SHARDED else a).reshape(w[name].shape)

    cols = [[as_given(out[name][j], name) for name in _WEIGHT_NAMES] for j in range(4)]
    return (loss, grad_x, *cols[0], *cols[1], *cols[2], *cols[3])


def kernel(x, meta_tokens, ffn1_norm, ffn1_w_gate, ffn1_w_up, ffn1_w_down, mix_norm, w_in, q_latent_norm, w_uq, kv_latent_norm, w_uk, w_uv, q_head_norm, k_head_norm, conv_w, conv_b, gate_a_w, gate_a_b, gate_x_w, gate_x_b, lru_lambda, attn_out_norm, lru_out_norm, w_out, ffn2_norm, ffn2_w_gate, ffn2_w_up, ffn2_w_down, final_norm, loss_target, m_meta_tokens, m_ffn1_norm, m_ffn1_w_gate, m_ffn1_w_up, m_ffn1_w_down, m_mix_norm, m_w_in, m_q_latent_norm, m_w_uq, m_kv_latent_norm, m_w_uk, m_w_uv, m_q_head_norm, m_k_head_norm, m_conv_w, m_conv_b, m_gate_a_w, m_gate_a_b, m_gate_x_w, m_gate_x_b, m_lru_lambda, m_attn_out_norm, m_lru_out_norm, m_w_out, m_ffn2_norm, m_ffn2_w_gate, m_ffn2_w_up, m_ffn2_w_down, m_final_norm, v_meta_tokens, v_ffn1_norm, v_ffn1_w_gate, v_ffn1_w_up, v_ffn1_w_down, v_mix_norm, v_w_in, v_q_latent_norm, v_w_uq, v_kv_latent_norm, v_w_uk, v_w_uv, v_q_head_norm, v_k_head_norm, v_conv_w, v_conv_b, v_gate_a_w, v_gate_a_b, v_gate_x_w, v_gate_x_b, v_lru_lambda, v_attn_out_norm, v_lru_out_norm, v_w_out, v_ffn2_norm, v_ffn2_w_gate, v_ffn2_w_up, v_ffn2_w_down, v_final_norm):
    args = locals()
    w = {name: args[name] for name in _WEIGHT_NAMES}
    m = {name: args["m_" + name] for name in _WEIGHT_NAMES}
    v = {name: args["v_" + name] for name in _WEIGHT_NAMES}
    return train_step(x, loss_target, w, m, v)
```

```python
import math

import jax
import jax.numpy as jnp
from jax import lax
from jax.experimental import pallas as pl
from jax.experimental.pallas import tpu as pltpu

F32 = jnp.float32
BF16 = jnp.bfloat16

D_MODEL = 1024
CHUNK = 64
CHUNK_SHIFT = 6
N_META = 16
PAD = CHUNK - N_META
FIRST_FRAME = PAD + N_META
MLA_HEADS = 4
D_NOPE = 128
D_ROPE = 64
D_QK = D_NOPE + D_ROPE
D_V = 128
HEAD_SLAB = 256
KV_RANK = 256
Q_RANK = 384
ROPE_THETA = 10000.0
LRU_WIDTH = 512
LRU_BLOCKS = 8
LRU_BLOCK = 64
LRU_TILE = 128
CONV_W = 4
C_RGLRU = 8.0
D_FF = 2816
MLA_IN = 768
EPS = 1e-6
NEG_INF = -1e30
N_DEV = 8
LANES = 128
VMEM_LIMIT = 52 * 1024 * 1024
ATTN_HEADS_PER_STEP = 2
TN_ROWS = 4224
TN_X_BYTES = 12 * 1024 * 1024
TN_Y_BYTES = 9 * 1024 * 1024 // 2

ADAM_LR = 0.001
ADAM_B1 = 0.9
ADAM_B2 = 0.999
ADAM_EPS = 1e-08
ADAM_WD = 0.01
ADAM_STEP = 10

VMEM_WHOLE = pl.BlockSpec(memory_space=pltpu.VMEM)
HBM_WHOLE = pl.BlockSpec(memory_space=pl.ANY)


def _params(sems):
    if sems is None:
        return pltpu.CompilerParams(vmem_limit_bytes=VMEM_LIMIT)
    return pltpu.CompilerParams(dimension_semantics=sems, vmem_limit_bytes=VMEM_LIMIT)


def _tile(n, cap, mult=16):
    best = None
    for t in range(mult, min(n, cap) + 1, mult):
        if n % t == 0:
            best = t
    assert best is not None, (n, cap, mult)
    return best


def _row(tm, d):
    return pl.BlockSpec((tm, d), lambda i: (i, 0))


def _fixed(shape):
    return pl.BlockSpec(shape, lambda i: (0,) * len(shape))


def _mesh_position():
    return lax.axis_index("x"), lax.axis_index("y"), lax.axis_index("c")


def _flat_index(x, y, c):
    return 4 * x + 2 * y + c


def _peers(x, y, c):
    out = []
    for k in range(1, N_DEV):
        fx, fy, fc = (k >> 2) & 1, (k >> 1) & 1, k & 1
        out.append((1 - x if fx else x, 1 - y if fy else y, 1 - c if fc else c))
    return out


def _comm_out_shapes(srcs, modes):
    return [jax.ShapeDtypeStruct((N_DEV,) + s.shape if md == "gather" else s.shape, s.dtype)
            for s, md in zip(srcs, modes)]


def _comm_scratch(n):
    per_peer = n * (N_DEV - 1)
    return [pltpu.SemaphoreType.DMA((per_peer,)), pltpu.SemaphoreType.DMA((per_peer,)), pltpu.SemaphoreType.DMA((n,))]


class _Copies:
    def __init__(self, own, first, relay):
        self.own, self.first, self.relay = own, first, relay

    def start(self):
        for cp in self.own + self.first:
            cp.start()

    def forward(self):
        for arrival, onward in self.relay:
            arrival.wait_recv()
            onward.start()

    def finish(self):
        arrivals = [a for a, _ in self.relay]
        onward = [f for _, f in self.relay]
        for cp in self.first + onward:
            if not any(cp is a for a in arrivals):
                cp.wait_recv()
        for cp in self.first + onward:
            cp.wait_send()
        for cp in self.own:
            cp.wait()


def _comm_copies(src_refs, dst_refs, modes, send, recv, local):
    x, y, c = _mesh_position()
    me = _flat_index(x, y, c)
    n = len(modes)
    sibling = (x, y, 1 - c)
    chips = [(1 - x, y), (x, 1 - y), (1 - x, 1 - y)]

    def remote(src, dst, k, t, to):
        return pltpu.make_async_remote_copy(src_ref=src, dst_ref=dst, send_sem=send.at[k * n + t],
                                            recv_sem=recv.at[k * n + t], device_id=to,
                                            device_id_type=pl.DeviceIdType.MESH)

    own, first, relay = [], [], []
    for t, (src, dst, md) in enumerate(zip(src_refs, dst_refs, modes)):
        if md == "scatter":
            own.append(pltpu.make_async_copy(src.at[me], dst.at[me], local.at[t]))
            for k, peer in enumerate(_peers(x, y, c)):
                first.append(remote(src.at[_flat_index(*peer)], dst.at[me], k, t, peer))
        else:
            own.append(pltpu.make_async_copy(src, dst.at[me], local.at[t]))
            first.append(remote(src, dst.at[me], 0, t, sibling))
            for j, chip in enumerate(chips):
                arrival = remote(src, dst.at[me], 1 + j, t, (*chip, c))
                landed = dst.at[_flat_index(*chip, c)]
                first.append(arrival)
                relay.append((arrival, remote(landed, landed, 4 + j, t, sibling)))
    return _Copies(own, first, relay)


def _hosted(body, n_in, n_out, modes, grid):
    t = len(modes)
    total = math.prod(grid)

    def wrapped(*refs):
        ins, csrc = refs[:n_in], refs[n_in:n_in + t]
        outs = refs[n_in + t:n_in + t + n_out]
        cdst = refs[n_in + t + n_out:n_in + 2 * t + n_out]
        scratch = refs[n_in + 2 * t + n_out:-3]
        copies = _comm_copies(csrc, cdst, modes, *refs[-3:])
        step = pl.program_id(0)
        for axis in range(1, len(grid)):
            step = step * grid[axis] + pl.program_id(axis)

        @pl.when(step == 0)
        def _():
            copies.start()

        body(*ins, *outs, *scratch)

        @pl.when(step == (total * 4) // 5)
        def _():
            copies.forward()

        @pl.when(step == total - 1)
        def _():
            copies.finish()

    return wrapped


def _call(body, name, grid, in_specs, out_specs, out_shape, sems, args, scratch=(), comm=None):
    if comm is None:
        outs = pl.pallas_call(body, name=name, grid=grid, in_specs=in_specs, out_specs=out_specs, out_shape=out_shape,
                              scratch_shapes=list(scratch), compiler_params=_params(sems))(*args)
        return outs, []
    srcs, modes = comm
    n = len(modes)
    res = pl.pallas_call(
        _hosted(body, len(in_specs), len(out_specs), modes, grid), name=name, grid=grid,
        in_specs=list(in_specs) + [HBM_WHOLE] * n, out_specs=list(out_specs) + [HBM_WHOLE] * n,
        out_shape=list(out_shape) + _comm_out_shapes(srcs, modes),
        scratch_shapes=list(scratch) + _comm_scratch(n),
        compiler_params=_params(("arbitrary",) * len(grid)))(*args, *srcs)
    return res[:len(out_specs)], res[len(out_specs):]


def exchange(srcs, modes, name):
    n = len(modes)

    def body(*refs):
        copies = _comm_copies(refs[:n], refs[n:2 * n], modes, *refs[2 * n:])
        copies.start()
        copies.forward()
        copies.finish()

    return pl.pallas_call(body, name=name, in_specs=[HBM_WHOLE] * n, out_specs=[HBM_WHOLE] * n,
                          out_shape=_comm_out_shapes(srcs, modes), scratch_shapes=_comm_scratch(n))(*srcs)


def _nn(a, b):
    return jnp.dot(a, b, preferred_element_type=F32)


def _nt(a, b):
    return lax.dot_general(a, b, (((1,), (1,)), ((), ())), preferred_element_type=F32)


def _tn(a, b):
    return lax.dot_general(a, b, (((0,), (0,)), ((), ())), preferred_element_type=F32)


def _sig(x):
    return 1.0 / (1.0 + jnp.exp(-x))


def _rms_r(x, n=None):
    n = x.shape[-1] if n is None else n
    return lax.rsqrt(jnp.sum(x * x, axis=-1, keepdims=True) * (1.0 / n) + EPS)


def _rms_bwd(x, r, g, dy, n=None):
    n = x.shape[-1] if n is None else n
    xhat = x * r
    dxhat = dy * g
    dx = r * (dxhat - xhat * (jnp.sum(dxhat * xhat, axis=-1, keepdims=True) * (1.0 / n)))
    return dx, jnp.sum(dy * xhat, axis=0, keepdims=True)


def _accumulate(ref, val, first):
    @pl.when(first)
    def _():
        ref[...] = val

    @pl.when(jnp.logical_not(first))
    def _():
        ref[...] += val


_GELU_C = math.sqrt(2.0 / math.pi)


def _gelu_and_grad(x):
    inner = _GELU_C * (x + 0.044715 * x * x * x)
    t = jnp.tanh(inner)
    gelu = 0.5 * x * (1.0 + t)
    dgelu = 0.5 * (1.0 + t) + 0.5 * x * (1.0 - t * t) * _GELU_C * (1.0 + 3.0 * 0.044715 * x * x)
    return gelu, dgelu


def _log1p_small(t):
    return jnp.where(t < 1e-3, t * (1.0 - t * (0.5 - t * (1.0 / 3.0))), jnp.log(1.0 + t))


def _softplus(x):
    return jnp.maximum(x, 0.0) + _log1p_small(jnp.exp(-jnp.abs(x)))


def _sig_tanh(x):
    return 0.5 + 0.5 * jnp.tanh(0.5 * x)


def _ff_chunks(f):
    return 2 if (f // 2) % LANES == 0 else 1


def _swiglu_half(x, g_ref, wg_ref, wu_ref, wd_ref, a_ref, b_ref, fc):
    f = wg_ref.shape[0]
    u = (x * _rms_r(x) * g_ref[...]).astype(BF16)
    acc = jnp.zeros(x.shape, F32)
    for c in range(f // fc):
        cols = slice(c * fc, (c + 1) * fc)
        a = _nt(u, wg_ref[cols, :])
        b = _nt(u, wu_ref[cols, :])
        s = (a * _sig(a) * b).astype(BF16)
        acc = acc + _nn(s, wd_ref[cols, :])
        a_ref[:, cols] = a.astype(BF16)
        b_ref[:, cols] = b.astype(BF16)
    return x + 0.5 * acc, u


def ffn_up(h, g, wg, wu, name, comm=None):
    n, d = h.shape
    f = wg.shape[0]
    tm = _tile(n, 528)
    fc = 2 * LANES if f % (2 * LANES) == 0 else f

    def body(h_ref, g_ref, wg_ref, wu_ref, u_ref, a_ref, b_ref, s_ref):
        x = h_ref[...]
        u = (x * _rms_r(x) * g_ref[...]).astype(BF16)
        u_ref[...] = u
        for c in range(f // fc):
            cols = slice(c * fc, (c + 1) * fc)
            a = _nt(u, wg_ref[cols, :])
            b = _nt(u, wu_ref[cols, :])
            a_ref[:, cols] = a.astype(BF16)
            b_ref[:, cols] = b.astype(BF16)
            s_ref[:, cols] = (0.5 * (a * _sig(a) * b)).astype(BF16)

    wide = jax.ShapeDtypeStruct((n, f), BF16)
    return _call(
        body, name, (n // tm,),
        [_row(tm, d), _fixed((1, d)), VMEM_WHOLE, VMEM_WHOLE],
        [_row(tm, d), _row(tm, f), _row(tm, f), _row(tm, f)],
        [jax.ShapeDtypeStruct((n, d), BF16), wide, wide, wide],
        ("parallel",), (h, g, wg, wu), comm=comm)


def ffn_down_inproj(h, s, wd, g, wm, wl, name, comm=None):
    n, d = h.shape
    f = wd.shape[0]
    tm = _tile(n, 528)

    def body(h_ref, s_ref, wd_ref, g_ref, wm_ref, wl_ref, ho_ref, u_ref, zm_ref, zl_ref):
        x = h_ref[...] + _nn(s_ref[...], wd_ref[...])
        ho_ref[...] = x
        u = (x * _rms_r(x) * g_ref[...]).astype(BF16)
        u_ref[...] = u
        zm_ref[...] = _nt(u, wm_ref[...])
        zl_ref[...] = _nt(u, wl_ref[...])

    return _call(
        body, name, (n // tm,),
        [_row(tm, d), _row(tm, f), VMEM_WHOLE, _fixed((1, d)), VMEM_WHOLE, VMEM_WHOLE],
        [_row(tm, d), _row(tm, d), _row(tm, MLA_IN), _row(tm, 2 * LRU_WIDTH)],
        [jax.ShapeDtypeStruct((n, d), F32), jax.ShapeDtypeStruct((n, d), BF16),
         jax.ShapeDtypeStruct((n, MLA_IN), F32), jax.ShapeDtypeStruct((n, 2 * LRU_WIDTH), F32)],
        ("parallel",), (h, s, wd, g, wm, wl), comm=comm)


def ffn_fwd_loss(h, g, wg, wu, wd, g_final, tgt, lp, name):
    n, d = h.shape
    f = wg.shape[0]
    tm = _tile(lp, 528)
    per_seq = lp // tm
    fc = 2 * LANES if f % (2 * LANES) == 0 else f

    def body(h_ref, g_ref, wg_ref, wu_ref, wd_ref, gf_ref, t_ref, dh_ref, u_ref, a_ref, b_ref, loss_ref, dgf_ref):
        i = pl.program_id(0)
        y, u_ref[...] = _swiglu_half(h_ref[...], g_ref, wg_ref, wu_ref, wd_ref, a_ref, b_ref, fc)
        dh_ref[...], part, dg = _loss_and_grad(y, gf_ref[...], t_ref[...], (i % per_seq) * tm)
        _accumulate(loss_ref, jnp.broadcast_to(part, (1, LANES)), i == 0)
        _accumulate(dgf_ref, dg, i == 0)

    outs, _ = _call(
        body, name, (n // tm,),
        [_row(tm, d), _fixed((1, d)), VMEM_WHOLE, VMEM_WHOLE, VMEM_WHOLE, _fixed((1, d)), _row(tm, d)],
        [_row(tm, d), _row(tm, d), _row(tm, f), _row(tm, f), _fixed((1, LANES)), _fixed((1, d))],
        [jax.ShapeDtypeStruct((n, d), F32), jax.ShapeDtypeStruct((n, d), BF16),
         jax.ShapeDtypeStruct((n, f), BF16), jax.ShapeDtypeStruct((n, f), BF16),
         jax.ShapeDtypeStruct((1, LANES), F32), jax.ShapeDtypeStruct((1, d), F32)],
        ("arbitrary",), (h, g, wg, wu, wd, g_final, tgt))
    return outs


def ffn_bwd_act(dh, h, g, a, b, wg, wu, wd, name, comm=None, emit_sh=True):
    n, d = h.shape
    f = wg.shape[0]
    tm = _tile(n, 352 if emit_sh else 384)
    nc = _ff_chunks(f)
    fc = f // nc

    def body(dh_ref, h_ref, g_ref, a_ref, b_ref, wg_ref, wu_ref, wd_ref, dhi_ref, da_ref, db_ref, *rest):
        dg_ref = rest[-1]
        x = h_ref[...]
        dy = dh_ref[...]
        r = _rms_r(x)
        dhh = (0.5 * dy).astype(BF16)
        du = jnp.zeros((tm, d), F32)
        for c in range(nc):
            cols = slice(c * fc, (c + 1) * fc)
            ds = _nt(dhh, wd_ref[cols, :])
            av = a_ref[:, cols].astype(F32)
            bv = b_ref[:, cols].astype(F32)
            sg = _sig(av)
            sil = av * sg
            da = (ds * bv * (sg * (1.0 + av * (1.0 - sg)))).astype(BF16)
            db = (ds * sil).astype(BF16)
            da_ref[:, cols] = da
            db_ref[:, cols] = db
            if emit_sh:
                rest[0][:, cols] = (0.5 * sil * bv).astype(BF16)
            du = du + _nn(da, wg_ref[cols, :]) + _nn(db, wu_ref[cols, :])
        dx, dg = _rms_bwd(x, r, g_ref[...], du)
        dhi_ref[...] = dy + dx
        _accumulate(dg_ref, dg, pl.program_id(0) == 0)

    wide = [jax.ShapeDtypeStruct((n, f), BF16)] * (3 if emit_sh else 2)
    return _call(
        body, name, (n // tm,),
        [_row(tm, d), _row(tm, d), _fixed((1, d)), _row(tm, f), _row(tm, f), VMEM_WHOLE, VMEM_WHOLE, VMEM_WHOLE],
        [_row(tm, d)] + [_row(tm, f)] * len(wide) + [_fixed((1, d))],
        [jax.ShapeDtypeStruct((n, d), F32)] + wide + [jax.ShapeDtypeStruct((1, d), F32)],
        ("arbitrary",), (dh, h, g, a, b, wg, wu, wd), comm=comm)


def tn_matmul(x, y, name, out="f32", comm=None):
    n, k = x.shape
    m = y.shape[1]
    tm = _tile(n, TN_ROWS)
    kc, mc = k, (512 if m % 512 == 0 else m)
    while tm * kc * x.dtype.itemsize > TN_X_BYTES and kc % (2 * LANES) == 0:
        kc //= 2
    while tm * mc * y.dtype.itemsize > TN_Y_BYTES and mc % (2 * LANES) == 0:
        mc //= 2
    steps = n // tm

    def body(x_ref, y_ref, o_ref, *acc):
        i = pl.program_id(2)
        part = _tn(x_ref[...].astype(BF16), y_ref[...].astype(BF16))
        if steps == 1:
            o_ref[...] = part.astype(o_ref.dtype)
        elif out == "f32":
            _accumulate(o_ref, part, i == 0)
        else:
            _accumulate(acc[0], part, i == 0)

            @pl.when(i == steps - 1)
            def _():
                o_ref[...] = acc[0][...].astype(BF16)

    out_shape = jax.ShapeDtypeStruct((k, m), F32 if out == "f32" else BF16)
    (res,), landed = _call(
        body, name, (k // kc, m // mc, steps),
        [pl.BlockSpec((tm, kc), lambda a, b, i: (i, a)), pl.BlockSpec((tm, mc), lambda a, b, i: (i, b))],
        [pl.BlockSpec((kc, mc), lambda a, b, i: (a, b))], [out_shape], ("parallel", "parallel", "arbitrary"), (x, y),
        scratch=[pltpu.VMEM((kc, mc), F32)] if (out == "bf16" and steps > 1) else [], comm=comm)
    return (res, landed) if comm is not None else res


def inproj_bwd(dzm, du, dgate, dh2, h, g, wm, wl, comm=None):
    n, d = h.shape
    tm = _tile(n, 352)

    def body(dzm_ref, du_ref, dgt_ref, dh2_ref, h_ref, g_ref, wm_ref, wl_ref, dh_ref, dg_ref):
        x = h_ref[...]
        dun = (_nn(dzm_ref[...].astype(BF16), wm_ref[...])
               + _nn(du_ref[...].astype(BF16), wl_ref[:LRU_WIDTH, :])
               + _nn(dgt_ref[...].astype(BF16), wl_ref[LRU_WIDTH:, :]))
        dx, dg = _rms_bwd(x, _rms_r(x), g_ref[...], dun)
        dh_ref[...] = dh2_ref[...] + dx
        _accumulate(dg_ref, dg, pl.program_id(0) == 0)

    return _call(
        body, "inproj_bwd", (n // tm,),
        [_row(tm, MLA_IN), _row(tm, LRU_WIDTH), _row(tm, LRU_WIDTH), _row(tm, d), _row(tm, d),
         _fixed((1, d)), VMEM_WHOLE, VMEM_WHOLE],
        [_row(tm, d), _fixed((1, d))],
        [jax.ShapeDtypeStruct((n, d), F32), jax.ShapeDtypeStruct((1, d), F32)],
        ("arbitrary",), (dzm, du, dgate, dh2, h, g, wm, wl), comm=comm)


def _rope_tables(lp):
    pos = jnp.arange(lp, dtype=F32) - float(PAD)
    half = D_ROPE // 2
    inv_freq = ROPE_THETA ** (-jnp.arange(0, half, dtype=F32) / half)
    ang = pos[:, None] * inv_freq[None, :]
    cos, sin = jnp.cos(ang), jnp.sin(ang)
    one = jnp.ones((lp, D_NOPE), F32)
    z_nope = jnp.zeros((lp, D_NOPE), F32)
    z_half = jnp.zeros((lp, half), F32)
    z_tail = jnp.zeros((lp, HEAD_SLAB - D_QK), F32)
    cosr = jnp.concatenate([one, cos, cos, z_tail], axis=1)
    sin_up = jnp.concatenate([z_nope, z_half, sin, z_tail], axis=1)
    sin_dn = jnp.concatenate([z_nope, -sin, z_half, z_tail], axis=1)
    return cosr, sin_up, sin_dn


def _rope(x, cosr, sin_up, sin_dn):
    half = D_ROPE // 2
    return x * cosr + pltpu.roll(x, half, axis=1) * sin_up + pltpu.roll(x, HEAD_SLAB - half, axis=1) * sin_dn


def _rope_bwd(dy, cosr, sin_up, sin_dn):
    half = D_ROPE // 2
    return (dy * cosr + pltpu.roll(dy * sin_up, HEAD_SLAB - half, axis=1)
            + pltpu.roll(dy * sin_dn, half, axis=1))


def _k_rope_slab(zm_tile):
    tm = zm_tile.shape[0]
    krp = zm_tile[:, Q_RANK + KV_RANK:MLA_IN]
    return jnp.concatenate([jnp.zeros((tm, D_NOPE), F32), krp], axis=1)


def mla_prep_fwd(zm, gql, gkvl, wuq, wuk, wuv, gqh, gkh, tables, lp):
    n = zm.shape[0]
    tm = _tile(lp, 352)
    per_seq = lp // tm
    width = MLA_HEADS * HEAD_SLAB
    scale = 1.0 / math.sqrt(D_QK)

    def body(zm_ref, gql_ref, gkvl_ref, wuq_ref, wuk_ref, wuv_ref, gqh_ref, gkh_ref,
             cos_ref, up_ref, dn_ref, q_ref, k_ref, v_ref, qn_ref, cn_ref):
        z = zm_ref[...]
        cq = z[:, :Q_RANK]
        ckv = z[:, Q_RANK:Q_RANK + KV_RANK]
        qn = (cq * _rms_r(cq) * gql_ref[...]).astype(BF16)
        cn = (ckv * _rms_r(ckv) * gkvl_ref[...]).astype(BF16)
        qn_ref[...] = qn
        cn_ref[...] = cn
        q_raw = _nt(qn, wuq_ref[...])
        k_raw = _nt(cn, wuk_ref[...])
        v_ref[...] = _nt(cn, wuv_ref[...]).astype(BF16)
        kr_slab = _k_rope_slab(z)
        cosr, sin_up, sin_dn = cos_ref[...], up_ref[...], dn_ref[...]
        for hd in range(MLA_HEADS):
            cols = slice(hd * HEAD_SLAB, (hd + 1) * HEAD_SLAB)
            xq = q_raw[:, cols]
            yq = _rope(xq * _rms_r(xq, D_QK) * gqh_ref[...], cosr, sin_up, sin_dn)
            q_ref[:, cols] = (yq * scale).astype(BF16)
            xk = k_raw[:, cols] + kr_slab
            yk = _rope(xk * _rms_r(xk, D_QK) * gkh_ref[...], cosr, sin_up, sin_dn)
            k_ref[:, cols] = yk.astype(BF16)

    tab = pl.BlockSpec((tm, HEAD_SLAB), lambda i: (i % per_seq, 0))
    return pl.pallas_call(
        body, name="mla_prep_fwd", grid=(n // tm,),
        in_specs=[_row(tm, MLA_IN), _fixed((1, Q_RANK)), _fixed((1, KV_RANK)), VMEM_WHOLE, VMEM_WHOLE, VMEM_WHOLE,
                  _fixed((1, HEAD_SLAB)), _fixed((1, HEAD_SLAB)), tab, tab, tab],
        out_specs=[_row(tm, width), _row(tm, width), _row(tm, MLA_HEADS * D_V), _row(tm, Q_RANK), _row(tm, KV_RANK)],
        out_shape=[jax.ShapeDtypeStruct((n, width), BF16), jax.ShapeDtypeStruct((n, width), BF16),
                   jax.ShapeDtypeStruct((n, MLA_HEADS * D_V), BF16), jax.ShapeDtypeStruct((n, Q_RANK), BF16),
                   jax.ShapeDtypeStruct((n, KV_RANK), BF16)],
        compiler_params=_params(("parallel",)),
    )(zm, gql, gkvl, wuq, wuk, wuv, gqh, gkh, *tables)


def mla_prep_bwd(dq, dk, dv, zm, qn, cn, gql, gkvl, wuq, wuk, wuv, gqh, gkh, tables, lp, comm=None):
    n = zm.shape[0]
    tm = _tile(lp, 704)
    per_seq = lp // tm
    width = MLA_HEADS * HEAD_SLAB
    scale = 1.0 / math.sqrt(D_QK)

    def body(dq_ref, dk_ref, dv_ref, zm_ref, qn_ref, cn_ref, gql_ref, gkvl_ref, wuq_ref, wuk_ref, wuv_ref,
             gqh_ref, gkh_ref, cos_ref, up_ref, dn_ref,
             dzm_ref, dqr_ref, dkr_ref, dgql_ref, dgkvl_ref, dgqh_ref, dgkh_ref):
        z = zm_ref[...]
        cq = z[:, :Q_RANK]
        ckv = z[:, Q_RANK:Q_RANK + KV_RANK]
        q_raw = _nt(qn_ref[...], wuq_ref[...])
        k_raw = _nt(cn_ref[...], wuk_ref[...])
        kr_slab = _k_rope_slab(z)
        cosr, sin_up, sin_dn = cos_ref[...], up_ref[...], dn_ref[...]
        dgq = jnp.zeros((1, HEAD_SLAB), F32)
        dgk = jnp.zeros((1, HEAD_SLAB), F32)
        dkrp = jnp.zeros((tm, HEAD_SLAB - D_NOPE), F32)
        for hd in range(MLA_HEADS):
            cols = slice(hd * HEAD_SLAB, (hd + 1) * HEAD_SLAB)
            xq = q_raw[:, cols]
            dxn = _rope_bwd(dq_ref[:, cols] * scale, cosr, sin_up, sin_dn)
            dxq, dg = _rms_bwd(xq, _rms_r(xq, D_QK), gqh_ref[...], dxn, D_QK)
            dgq = dgq + dg
            dqr_ref[:, cols] = dxq.astype(BF16)
            xk = k_raw[:, cols] + kr_slab
            dxn = _rope_bwd(dk_ref[:, cols], cosr, sin_up, sin_dn)
            dxk, dg = _rms_bwd(xk, _rms_r(xk, D_QK), gkh_ref[...], dxn, D_QK)
            dgk = dgk + dg
            dkr_ref[:, cols] = dxk.astype(BF16)
            dkrp = dkrp + dxk[:, D_NOPE:]
        dqn = _nn(dqr_ref[...], wuq_ref[...])
        dcn = _nn(dkr_ref[...], wuk_ref[...]) + _nn(dv_ref[...].astype(BF16), wuv_ref[...])
        dcq, dg1 = _rms_bwd(cq, _rms_r(cq), gql_ref[...], dqn)
        dckv, dg2 = _rms_bwd(ckv, _rms_r(ckv), gkvl_ref[...], dcn)
        dzm_ref[:, :Q_RANK] = dcq
        dzm_ref[:, Q_RANK:Q_RANK + KV_RANK] = dckv
        dzm_ref[:, Q_RANK + KV_RANK:] = dkrp
        first = pl.program_id(0) == 0
        _accumulate(dgql_ref, dg1, first)
        _accumulate(dgkvl_ref, dg2, first)
        _accumulate(dgqh_ref, dgq, first)
        _accumulate(dgkh_ref, dgk, first)

    tab = pl.BlockSpec((tm, HEAD_SLAB), lambda i: (i % per_seq, 0))
    return _call(
        body, "mla_prep_bwd", (n // tm,),
        [_row(tm, width), _row(tm, width), _row(tm, MLA_HEADS * D_V), _row(tm, MLA_IN),
         _row(tm, Q_RANK), _row(tm, KV_RANK), _fixed((1, Q_RANK)), _fixed((1, KV_RANK)),
         VMEM_WHOLE, VMEM_WHOLE, VMEM_WHOLE, _fixed((1, HEAD_SLAB)), _fixed((1, HEAD_SLAB)), tab, tab, tab],
        [_row(tm, MLA_IN), _row(tm, width), _row(tm, width), _fixed((1, Q_RANK)), _fixed((1, KV_RANK)),
         _fixed((1, HEAD_SLAB)), _fixed((1, HEAD_SLAB))],
        [jax.ShapeDtypeStruct((n, MLA_IN), F32), jax.ShapeDtypeStruct((n, width), BF16),
         jax.ShapeDtypeStruct((n, width), BF16), jax.ShapeDtypeStruct((1, Q_RANK), F32),
         jax.ShapeDtypeStruct((1, KV_RANK), F32), jax.ShapeDtypeStruct((1, HEAD_SLAB), F32),
         jax.ShapeDtypeStruct((1, HEAD_SLAB), F32)],
        ("arbitrary",), (dq, dk, dv, zm, qn, cn, gql, gkvl, wuq, wuk, wuv, gqh, gkh, *tables), comm=comm)


def _attn_tile(lp):
    return _tile(lp, 704, CHUNK)


def _chunk_mask(i, j, t):
    qpos = i * t + lax.broadcasted_iota(jnp.int32, (t, t), 0)
    kpos = j * t + lax.broadcasted_iota(jnp.int32, (t, t), 1)
    same_or_earlier = jnp.right_shift(kpos, CHUNK_SHIFT) <= jnp.right_shift(qpos, CHUNK_SHIFT)
    return jnp.logical_and(same_or_earlier, kpos >= PAD)


def _masked_scores(s, i, j, t, diagonal):
    if diagonal:
        return jnp.where(_chunk_mask(i, j, t), s, NEG_INF)
    kpos = j * t + lax.broadcasted_iota(jnp.int32, (1, t), 1)
    return s + jnp.where(kpos < PAD, NEG_INF, 0.0)


def attn_fwd(q, k, v, nb, lp, comm=None):
    n = q.shape[0]
    t = _attn_tile(lp)
    nq = lp // t

    hp = ATTN_HEADS_PER_STEP

    def body(q_ref, k_ref, v_ref, o_ref, lse_ref):
        i = pl.program_id(2)
        qs = [q_ref[:, hh * HEAD_SLAB:(hh + 1) * HEAD_SLAB] for hh in range(hp)]

        def kv_step(j, carry, diagonal=False):
            off = pl.multiple_of(j * t, t)
            out = []
            for hh in range(hp):
                m, l, acc = carry[hh]
                kv = k_ref[pl.ds(off, t), hh * HEAD_SLAB:(hh + 1) * HEAD_SLAB]
                s = _masked_scores(_nt(qs[hh], kv), i, j, t, diagonal)
                m_new = jnp.maximum(m, jnp.max(s, axis=-1, keepdims=True))
                p = jnp.exp(s - m_new)
                alpha = jnp.exp(m - m_new)
                l = alpha * l + jnp.sum(p, axis=-1, keepdims=True)
                acc = alpha * acc + _nn(p.astype(BF16), v_ref[pl.ds(off, t), hh * D_V:(hh + 1) * D_V])
                out.append((m_new, l, acc))
            return tuple(out)

        init = tuple((jnp.full((t, 1), NEG_INF, F32), jnp.zeros((t, 1), F32), jnp.zeros((t, D_V), F32))
                     for _ in range(hp))
        done = kv_step(i, lax.fori_loop(0, i, kv_step, init), diagonal=True)
        for hh, (m, l, acc) in enumerate(done):
            o_ref[:, hh * D_V:(hh + 1) * D_V] = acc * (1.0 / l)
            lse_ref[hh] = jnp.broadcast_to(m + jnp.log(l), (t, LANES))

    return _call(
        body, "attn_fwd", (nb, MLA_HEADS // hp, nq),
        [pl.BlockSpec((t, hp * HEAD_SLAB), lambda b, h, i: (b * nq + i, h)),
         pl.BlockSpec((lp, hp * HEAD_SLAB), lambda b, h, i: (b, h)),
         pl.BlockSpec((lp, hp * D_V), lambda b, h, i: (b, h))],
        [pl.BlockSpec((t, hp * D_V), lambda b, h, i: (b * nq + i, h)),
         pl.BlockSpec((hp, t, LANES), lambda b, h, i: (h, b * nq + i, 0))],
        [jax.ShapeDtypeStruct((n, MLA_HEADS * D_V), F32), jax.ShapeDtypeStruct((MLA_HEADS, n, LANES), F32)],
        ("parallel", "parallel", "parallel"), (q, k, v), comm=comm)


def attn_bwd(q, k, v, o, do, lse, nb, lp, comm=None):
    n = q.shape[0]
    t = _attn_tile(lp)
    nq = lp // t

    def body(q_ref, k_ref, v_ref, o_ref, do_ref, lse_ref, dq_ref, dk_ref, dv_ref):
        dk_ref[...] = jnp.zeros_like(dk_ref)
        dv_ref[...] = jnp.zeros_like(dv_ref)

        def q_step(i, _):
            qoff = pl.multiple_of(i * t, t)
            qv = q_ref[pl.ds(qoff, t), :]
            dov = do_ref[pl.ds(qoff, t), :]
            delta = jnp.sum(o_ref[pl.ds(qoff, t), :] * dov, axis=-1, keepdims=True)
            lse_q = jnp.max(lse_ref[0, pl.ds(qoff, t), :], axis=-1, keepdims=True)
            do16 = dov.astype(BF16)

            def kv_step(j, dq_acc, diagonal=False):
                koff = pl.multiple_of(j * t, t)
                kv = k_ref[pl.ds(koff, t), :]
                s = _masked_scores(_nt(qv, kv), i, j, t, diagonal)
                p = jnp.exp(s - lse_q)
                dp = _nt(do16, v_ref[pl.ds(koff, t), :])
                ds16 = (p * (dp - delta)).astype(BF16)
                dv_ref[pl.ds(koff, t), :] += _tn(p.astype(BF16), do16)
                dk_ref[pl.ds(koff, t), :] += _tn(ds16, qv)
                return dq_acc + _nn(ds16, kv)

            earlier = lax.fori_loop(0, i, kv_step, jnp.zeros((t, HEAD_SLAB), F32))
            dq_ref[pl.ds(qoff, t), :] = kv_step(i, earlier, diagonal=True)
            return 0

        lax.fori_loop(0, nq, q_step, 0)

    wide = pl.BlockSpec((lp, HEAD_SLAB), lambda b, h: (b, h))
    thin = pl.BlockSpec((lp, D_V), lambda b, h: (b, h))
    width = MLA_HEADS * HEAD_SLAB
    return _call(
        body, "attn_bwd", (nb, MLA_HEADS),
        [wide, wide, thin, thin, thin, pl.BlockSpec((1, lp, LANES), lambda b, h: (h, b, 0))],
        [wide, wide, thin],
        [jax.ShapeDtypeStruct((n, width), F32), jax.ShapeDtypeStruct((n, width), F32),
         jax.ShapeDtypeStruct((n, MLA_HEADS * D_V), F32)],
        ("parallel", "parallel"), (q, k, v, o, do, lse), comm=comm)


def _seq_rows(nb, lp, width):
    rows = lax.broadcasted_iota(jnp.int32, (lp, width), 0)
    return jnp.concatenate([rows] * nb, axis=0) if nb > 1 else rows


def _lru_gates(u, w_ref, cb, wa, wx, ba, bx, lam):
    xc = (cb + w_ref[pl.ds(3, 1), :] * u + w_ref[pl.ds(2, 1), :] * pltpu.roll(u, 1, axis=0)
          + w_ref[pl.ds(1, 1), :] * pltpu.roll(u, 2, axis=0) + w_ref[pl.ds(0, 1), :] * pltpu.roll(u, 3, axis=0))
    xc16 = xc.astype(BF16)
    ra = _sig_tanh(_nn(xc16, wa) + ba)
    ia = _sig_tanh(_nn(xc16, wx) + bx)
    sp = _softplus(-lam)
    log_a = -C_RGLRU * ra * sp
    a = jnp.exp(log_a)
    x2 = 2.0 * log_a
    mult = jnp.sqrt(jnp.where(x2 > -1e-2, -x2 * (1.0 + x2 * (0.5 + x2 * (1.0 / 6.0))), 1.0 - a * a))
    return xc, xc16, ra, ia, sp, a, mult


def _scan_block_rows(width):
    return lax.broadcasted_iota(jnp.int32, (8, width), 0)


def lru_fwd(zl, conv_w, conv_b, wa, wx, ba, bx, lam, nb, lp, comm=None):
    n = zl.shape[0]
    w = LRU_TILE
    nt = LRU_WIDTH // w
    nblk = lp // 8

    def body(u_ref, gt_ref, cw_ref, cb_ref, wa_ref, wx_ref, ba_ref, bx_ref, lam_ref, y_ref, h_ref, a_s, b_s):
        u = u_ref[...]
        xc, _, _, ia, _, a, mult = _lru_gates(u, cw_ref, cb_ref[...], wa_ref[...], wx_ref[...],
                                              ba_ref[...], bx_ref[...], lam_ref[...])
        row = _seq_rows(nb, lp, w)
        mult = jnp.where(row == PAD, 1.0, mult)
        a_s[...] = a
        b_s[...] = jnp.where(row < PAD, 0.0, mult * (ia * xc))
        r8 = _scan_block_rows(w)

        def blk(i, carry):
            out = []
            for s_id in range(nb):
                off = pl.multiple_of(s_id * lp + i * 8, 8)
                av = a_s[pl.ds(off, 8), :]
                bv = b_s[pl.ds(off, 8), :]
                for sh in (1, 2, 4):
                    keep = r8 >= sh
                    bv = jnp.where(keep, av * pltpu.roll(bv, sh, axis=0) + bv, bv)
                    av = jnp.where(keep, av * pltpu.roll(av, sh, axis=0), av)
                hv = bv + av * carry[s_id]
                h_ref[pl.ds(off, 8), :] = hv
                out.append(jnp.sum(jnp.where(r8 == 7, hv, 0.0), axis=0, keepdims=True))
            return tuple(out)

        lax.fori_loop(0, nblk, blk, tuple(jnp.zeros((1, w), F32) for _ in range(nb)))
        gelu, _ = _gelu_and_grad(gt_ref[...])
        y_ref[...] = h_ref[...] * gelu

    col = lambda c: (0, c)
    return _call(
        body, "lru_fwd", (nt,),
        [pl.BlockSpec((n, w), col), pl.BlockSpec((n, w), lambda c: (0, nt + c)),
         pl.BlockSpec((CONV_W, w), col), pl.BlockSpec((1, w), col),
         pl.BlockSpec((w, w), lambda c: (c, c)), pl.BlockSpec((w, w), lambda c: (c, c)),
         pl.BlockSpec((1, w), col), pl.BlockSpec((1, w), col), pl.BlockSpec((1, w), col)],
        [pl.BlockSpec((n, w), col), pl.BlockSpec((n, w), col)],
        [jax.ShapeDtypeStruct((n, LRU_WIDTH), F32), jax.ShapeDtypeStruct((n, LRU_WIDTH), F32)],
        ("parallel",), (zl, zl, conv_w, conv_b, wa, wx, ba, bx, lam),
        scratch=[pltpu.VMEM((n, w), F32), pltpu.VMEM((n, w), F32)], comm=comm)


def lru_bwd(zl, hs, dy, conv_w, conv_b, wa, wx, ba, bx, lam, nb, lp, comm=None):
    n = zl.shape[0]
    w = LRU_TILE
    nt = LRU_WIDTH // w
    nblk = lp // 8

    def body(u_ref, gt_ref, h_ref, dy_ref, cw_ref, cb_ref, wa_ref, wx_ref, ba_ref, bx_ref, lam_ref,
             du_ref, dgt_ref, dcw_ref, dcb_ref, dba_ref, dbx_ref, dlam_ref, dwa_ref, dwx_ref,
             c_s, d_s, g_s, dwa_s, dwx_s):
        u = u_ref[...]
        lam = lam_ref[...]
        xc, xc16, ra, ia, sp, a, mult = _lru_gates(u, cw_ref, cb_ref[...], wa_ref[...], wx_ref[...],
                                                   ba_ref[...], bx_ref[...], lam)
        row = lax.broadcasted_iota(jnp.int32, (lp, w), 0)
        hv = h_ref[...]
        dyv = dy_ref[...]
        gelu, dgelu = _gelu_and_grad(gt_ref[...])
        dgt_ref[...] = jnp.where(row >= PAD, dyv * hv * dgelu, 0.0)
        c_s[...] = pltpu.roll(a, lp - 1, axis=0)
        d_s[...] = dyv * gelu
        r8 = _scan_block_rows(w)

        def blk(ii, carry):
            off = pl.multiple_of((nblk - 1 - ii) * 8, 8)
            cv = c_s[pl.ds(off, 8), :]
            dv = d_s[pl.ds(off, 8), :]
            for sh in (1, 2, 4):
                keep = r8 < 8 - sh
                dv = jnp.where(keep, cv * pltpu.roll(dv, 8 - sh, axis=0) + dv, dv)
                cv = jnp.where(keep, cv * pltpu.roll(cv, 8 - sh, axis=0), cv)
            gv = dv + cv * carry
            g_s[pl.ds(off, 8), :] = gv
            return jnp.sum(jnp.where(r8 == 0, gv, 0.0), axis=0, keepdims=True)

        lax.fori_loop(0, nblk, blk, jnp.zeros((1, w), F32))
        gv = g_s[...]
        first_row = row == PAD
        db = jnp.where(row >= PAD, gv, 0.0)
        da = jnp.where(row > PAD, gv * pltpu.roll(hv, 1, axis=0), 0.0)
        mult_eff = jnp.where(first_row, 1.0, mult)
        dmult = jnp.where(first_row, 0.0, db * (ia * xc))
        dia = db * mult_eff * xc
        dxc = db * mult_eff * ia
        dla = da * a - dmult * (a * a) / mult
        dra = dla * (-C_RGLRU * sp)
        dsp = jnp.sum(dla * (-C_RGLRU * ra), axis=0, keepdims=True)
        dpa = dra * ra * (1.0 - ra)
        dpx = dia * ia * (1.0 - ia)
        dpa16 = dpa.astype(BF16)
        dpx16 = dpx.astype(BF16)
        dxc = dxc + _nt(dpa16, wa_ref[...]) + _nt(dpx16, wx_ref[...])
        du = cw_ref[pl.ds(CONV_W - 1, 1), :] * dxc
        dcw = [jnp.sum(dxc * u, axis=0, keepdims=True)]
        for tap in range(1, CONV_W):
            dcw.insert(0, jnp.sum(dxc * pltpu.roll(u, tap, axis=0), axis=0, keepdims=True))
            du = du + cw_ref[pl.ds(CONV_W - 1 - tap, 1), :] * pltpu.roll(dxc, lp - tap, axis=0)
        du_ref[...] = jnp.where(row >= PAD, du, 0.0)
        first = pl.program_id(1) == 0
        _accumulate(dlam_ref, -_sig(-lam) * dsp, first)
        _accumulate(dba_ref, jnp.sum(dpa, axis=0, keepdims=True), first)
        _accumulate(dbx_ref, jnp.sum(dpx, axis=0, keepdims=True), first)
        _accumulate(dcb_ref, jnp.sum(dxc, axis=0, keepdims=True), first)
        _accumulate(dcw_ref, jnp.concatenate(dcw, axis=0), first)
        _accumulate(dwa_s, _tn(xc16, dpa16), first)
        _accumulate(dwx_s, _tn(xc16, dpx16), first)

        @pl.when(pl.program_id(1) == nb - 1)
        def _():
            for j in range(w // LRU_BLOCK):
                blk_rows = slice(j * LRU_BLOCK, (j + 1) * LRU_BLOCK)
                dwa_ref[0, blk_rows, :] = dwa_s[blk_rows, blk_rows]
                dwx_ref[0, blk_rows, :] = dwx_s[blk_rows, blk_rows]

    col = lambda c, b: (0, c)
    vec = pl.BlockSpec((1, w), col)
    mat = pl.BlockSpec((w, w), lambda c, b: (c, c))
    big = pl.BlockSpec((lp, w), lambda c, b: (b, c))
    dmat = pl.BlockSpec((1, w, LRU_BLOCK), lambda c, b: (c, 0, 0))
    return _call(
        body, "lru_bwd", (nt, nb),
        [big, pl.BlockSpec((lp, w), lambda c, b: (b, nt + c)), big, big,
         pl.BlockSpec((CONV_W, w), col), vec, mat, mat, vec, vec, vec],
        [big, big, pl.BlockSpec((CONV_W, w), col), vec, vec, vec, vec, dmat, dmat],
        [jax.ShapeDtypeStruct((n, LRU_WIDTH), F32), jax.ShapeDtypeStruct((n, LRU_WIDTH), F32),
         jax.ShapeDtypeStruct((CONV_W, LRU_WIDTH), F32), jax.ShapeDtypeStruct((1, LRU_WIDTH), F32),
         jax.ShapeDtypeStruct((1, LRU_WIDTH), F32), jax.ShapeDtypeStruct((1, LRU_WIDTH), F32),
         jax.ShapeDtypeStruct((1, LRU_WIDTH), F32), jax.ShapeDtypeStruct((nt, w, LRU_BLOCK), F32),
         jax.ShapeDtypeStruct((nt, w, LRU_BLOCK), F32)],
        ("parallel", "arbitrary"), (zl, zl, hs, dy, conv_w, conv_b, wa, wx, ba, bx, lam),
        scratch=[pltpu.VMEM((lp, w), F32), pltpu.VMEM((lp, w), F32), pltpu.VMEM((lp, w), F32),
                 pltpu.VMEM((w, w), F32), pltpu.VMEM((w, w), F32)], comm=comm)


def outproj_fwd(h, ya, yl, gao, glo, wout):
    n, d = h.shape
    half = ya.shape[1]
    tm = _tile(n, 704)

    def body(h_ref, ya_ref, yl_ref, gao_ref, glo_ref, w_ref, ho_ref, yn_ref):
        xa = ya_ref[...]
        xl = yl_ref[...]
        na = (xa * _rms_r(xa) * gao_ref[...]).astype(BF16)
        nl = (xl * _rms_r(xl) * glo_ref[...]).astype(BF16)
        yn_ref[:, :half] = na
        yn_ref[:, half:] = nl
        ho_ref[...] = h_ref[...] + _nn(na, w_ref[:half, :]) + _nn(nl, w_ref[half:, :])

    return pl.pallas_call(
        body, name="outproj_fwd", grid=(n // tm,),
        in_specs=[_row(tm, d), _row(tm, half), _row(tm, half), _fixed((1, half)), _fixed((1, half)), VMEM_WHOLE],
        out_specs=[_row(tm, d), _row(tm, 2 * half)],
        out_shape=[jax.ShapeDtypeStruct((n, d), F32), jax.ShapeDtypeStruct((n, 2 * half), BF16)],
        compiler_params=_params(("parallel",)),
    )(h, ya, yl, gao, glo, wout)


def outproj_bwd(dh, ya, yl, gao, glo, wout):
    n, d = dh.shape
    half = ya.shape[1]
    tm = _tile(n, 704)

    def body(dh_ref, ya_ref, yl_ref, gao_ref, glo_ref, w_ref, dya_ref, dyl_ref, dgao_ref, dglo_ref):
        d16 = dh_ref[...].astype(BF16)
        xa = ya_ref[...]
        xl = yl_ref[...]
        dxa, dga = _rms_bwd(xa, _rms_r(xa), gao_ref[...], _nt(d16, w_ref[:half, :]))
        dxl, dgl = _rms_bwd(xl, _rms_r(xl), glo_ref[...], _nt(d16, w_ref[half:, :]))
        dya_ref[...] = dxa
        dyl_ref[...] = dxl
        first = pl.program_id(0) == 0
        _accumulate(dgao_ref, dga, first)
        _accumulate(dglo_ref, dgl, first)

    return pl.pallas_call(
        body, name="outproj_bwd", grid=(n // tm,),
        in_specs=[_row(tm, d), _row(tm, half), _row(tm, half), _fixed((1, half)), _fixed((1, half)), VMEM_WHOLE],
        out_specs=[_row(tm, half), _row(tm, half), _fixed((1, half)), _fixed((1, half))],
        out_shape=[jax.ShapeDtypeStruct((n, half), F32), jax.ShapeDtypeStruct((n, half), F32),
                   jax.ShapeDtypeStruct((1, half), F32), jax.ShapeDtypeStruct((1, half), F32)],
        compiler_params=_params(("arbitrary",)),
    )(dh, ya, yl, gao, glo, wout)


def _loss_and_grad(x, gv, tgt, first_row):
    tm, d = x.shape
    r = _rms_r(x)
    row = first_row + lax.broadcasted_iota(jnp.int32, (tm, d), 0)
    diff = jnp.where(row >= FIRST_FRAME, x * r * gv - tgt, 0.0)
    part = 0.5 * jnp.sum(jnp.sum(diff * diff, axis=-1, keepdims=True) * (1.0 / d), axis=0, keepdims=True)
    dx, dg = _rms_bwd(x, r, gv, diff * (1.0 / d))
    return dx, part, dg


def assemble_cols(g, name):
    _, k, ns = g.shape

    def body(g_ref, o_ref):
        for j in range(N_DEV):
            o_ref[:, j * ns:(j + 1) * ns] = g_ref[j]

    return pl.pallas_call(body, name=name, out_shape=jax.ShapeDtypeStruct((k, N_DEV * ns), g.dtype),
                          compiler_params=_params(None))(g)


def split_cols(x, name):
    k, cols = x.shape
    ns = cols // N_DEV

    def body(x_ref, o_ref):
        for j in range(N_DEV):
            o_ref[j] = x_ref[:, j * ns:(j + 1) * ns]

    return pl.pallas_call(body, name=name, out_shape=jax.ShapeDtypeStruct((N_DEV, k, ns), x.dtype),
                          compiler_params=_params(None))(x)


def _slab_rows(w, per_head):
    k = w.shape[1]
    w = w.reshape(MLA_HEADS, per_head, k)
    return jnp.pad(w, ((0, 0), (0, HEAD_SLAB - per_head), (0, 0))).reshape(MLA_HEADS * HEAD_SLAB, k)


def _unslab_rows(w, per_head):
    k = w.shape[1]
    return w.reshape(MLA_HEADS, HEAD_SLAB, k)[:, :per_head].reshape(MLA_HEADS * per_head, k)


def meta_grad(dh0, nb, lp):
    d = dh0.shape[1]
    ns = d // N_DEV
    per_seq = lp // N_META

    def body(x_ref, o_ref):
        x = x_ref[...]
        for j in range(N_DEV):
            _accumulate(o_ref.at[j], x[:, j * ns:(j + 1) * ns], pl.program_id(0) == 0)

    return pl.pallas_call(
        body, name="meta_grad", grid=(nb,),
        in_specs=[pl.BlockSpec((N_META, d), lambda b: (b * per_seq + PAD // N_META, 0))],
        out_specs=pl.BlockSpec((N_DEV, N_META, ns), lambda b: (0, 0, 0)),
        out_shape=jax.ShapeDtypeStruct((N_DEV, N_META, ns), F32),
        compiler_params=_params(("arbitrary",)))(dh0)


VECTORS = [("ffn1_norm", 1024), ("mix_norm", 1024), ("q_latent_norm", 384), ("kv_latent_norm", 256),
           ("q_head_norm", 192), ("k_head_norm", 192), ("conv_b", 512), ("gate_a_b", 512), ("gate_x_b", 512),
           ("lru_lambda", 512), ("attn_out_norm", 512), ("lru_out_norm", 512), ("ffn2_norm", 1024),
           ("final_norm", 1024)]
VEC_ROWS = 16
LOSS_ROW = len(VECTORS)
GATES = ["gate_a_w", "gate_x_w"]


def pack_vectors(grads, loss):
    def body(*refs):
        o_ref = refs[-1]
        o_ref[...] = jnp.zeros_like(o_ref)
        for t, (ref, (_, cnt)) in enumerate(zip(refs[:-2], VECTORS)):
            o_ref[t:t + 1, :cnt] = ref[:, :cnt]
        o_ref[LOSS_ROW:LOSS_ROW + 1, :LANES] = refs[-2][...]

    return pl.pallas_call(body, name="pack_vectors", out_shape=jax.ShapeDtypeStruct((VEC_ROWS, D_MODEL), F32),
                          compiler_params=_params(None))(*[grads[name] for name, _ in VECTORS], loss)


def _adamw_update(w, g, m, v):
    c1 = 1.0 / (1.0 - ADAM_B1 ** ADAM_STEP)
    c2 = 1.0 / (1.0 - ADAM_B2 ** ADAM_STEP)
    mn = ADAM_B1 * m + (1.0 - ADAM_B1) * g
    vn = ADAM_B2 * v + (1.0 - ADAM_B2) * (g * g)
    delta = -ADAM_LR * ((mn * c1) / (jnp.sqrt(vn * c2) + ADAM_EPS) + ADAM_WD * w)
    return delta, mn, vn


def _sum_slots(ref, index=()):
    acc = ref[(0,) + index].astype(F32)
    for s in range(1, N_DEV):
        acc = acc + ref[(s,) + index].astype(F32)
    return acc


def adamw_sharded(r, w, m, v, name):
    rows, cols = w.shape
    tr = _tile(rows, 256, 16) if rows % 16 == 0 else rows

    def body(r_ref, w_ref, m_ref, v_ref, g_ref, d_ref, mo_ref, vo_ref):
        g = _sum_slots(r_ref)
        g_ref[...] = g
        d_ref[...], mo_ref[...], vo_ref[...] = _adamw_update(w_ref[...], g, m_ref[...], v_ref[...])

    spec = pl.BlockSpec((tr, cols), lambda i: (i, 0))
    shape = jax.ShapeDtypeStruct((rows, cols), F32)
    return pl.pallas_call(
        body, name=name, grid=(rows // tr,),
        in_specs=[pl.BlockSpec((N_DEV, tr, cols), lambda i: (0, i, 0))] + [spec] * 3,
        out_specs=[spec] * 4, out_shape=[shape] * 4,
        compiler_params=_params(("parallel",)),
    )(r, w, m, v)


def adamw_small(r_vec, r_gates, w, m, v):
    nt = len(VECTORS) + len(GATES)

    def body(*refs):
        rv_ref = refs[0]
        rg_refs = refs[1:1 + len(GATES)]
        base = 1 + len(GATES)
        w_refs, m_refs, v_refs = (refs[base + i * nt:base + (i + 1) * nt] for i in range(3))
        outs = refs[base + 3 * nt:]
        g_o, d_o, m_o, v_o = (outs[i * nt:(i + 1) * nt] for i in range(4))
        outs[4 * nt][...] = _sum_slots(rv_ref, (slice(LOSS_ROW, LOSS_ROW + 1), slice(0, LANES)))
        for t in range(nt):
            if t < len(VECTORS):
                cnt = VECTORS[t][1]
                g = _sum_slots(rv_ref, (slice(t, t + 1), slice(0, cnt)))
            else:
                g = _sum_slots(rg_refs[t - len(VECTORS)])
            g_o[t][...] = g
            d_o[t][...], m_o[t][...], v_o[t][...] = _adamw_update(w_refs[t][...], g, m_refs[t][...], v_refs[t][...])

    shapes = [jax.ShapeDtypeStruct(a.shape, F32) for a in w]
    res = pl.pallas_call(body, name="adamw_small", out_shape=shapes * 4 + [jax.ShapeDtypeStruct((1, LANES), F32)],
                         compiler_params=_params(None))(r_vec, *r_gates, *w, *m, *v)
    return [res[i * nt:(i + 1) * nt] for i in range(4)], res[4 * nt]


def _block_diag(w):
    nb, n, _ = w.shape
    eye = jnp.eye(nb, dtype=w.dtype)
    return (eye[:, None, :, None] * w[:, :, None, :]).reshape(nb * n, nb * n)


def _two_d(a):
    if a.ndim == 3:
        return a.reshape(a.shape[1], a.shape[2])
    if a.ndim == 4:
        return a.reshape(a.shape[1] * a.shape[2], a.shape[3])
    return a


_WEIGHT_NAMES = ['meta_tokens', 'ffn1_norm', 'ffn1_w_gate', 'ffn1_w_up', 'ffn1_w_down', 'mix_norm', 'w_in',
                 'q_latent_norm', 'w_uq', 'kv_latent_norm', 'w_uk', 'w_uv', 'q_head_norm', 'k_head_norm', 'conv_w',
                 'conv_b', 'gate_a_w', 'gate_a_b', 'gate_x_w', 'gate_x_b', 'lru_lambda', 'attn_out_norm',
                 'lru_out_norm', 'w_out', 'ffn2_norm', 'ffn2_w_gate', 'ffn2_w_up', 'ffn2_w_down', 'final_norm']


COLUMN_SHARDED = ("ffn1_w_gate", "ffn1_w_up", "ffn2_w_gate", "ffn2_w_up", "w_in", "w_uq", "w_uk", "w_uv")


def train_step(x, tgt, w, m, v):
    nb, seq, d = x.shape
    lp = PAD + N_META + seq
    n = nb * lp

    def local(a, name):
        a = _two_d(a)
        return a.T if name in COLUMN_SHARDED else a

    sh = {name: local(w[name], name) for name in _WEIGHT_NAMES}
    m2 = {name: local(m[name], name) for name in _WEIGHT_NAMES}
    v2 = {name: local(v[name], name) for name in _WEIGHT_NAMES}

    def b16(name):
        return sh[name].astype(BF16)

    out = {}

    def update(name, landed):
        out[name] = adamw_sharded(landed, sh[name], m2[name], v2[name], "adamw_" + name)

    g_meta, g_conv, g_wg1, g_wu1 = exchange(
        [sh["meta_tokens"], sh["conv_w"], b16("ffn1_w_gate"), b16("ffn1_w_up")], ["gather"] * 4, "gather_ffn1")
    wg1, wu1 = g_wg1.reshape(D_FF, d), g_wu1.reshape(D_FF, d)
    meta = assemble_cols(g_meta, "assemble_meta")
    conv_w = assemble_cols(g_conv, "assemble_conv")

    front = jnp.concatenate([jnp.zeros((PAD, d), F32), meta], axis=0)
    h0 = jnp.concatenate([jnp.broadcast_to(front[None], (nb, FIRST_FRAME, d)), x], axis=1).reshape(n, d)
    tgt_p = jnp.concatenate([jnp.zeros((nb, FIRST_FRAME, d), F32), tgt], axis=1).reshape(n, d)
    tables = _rope_tables(lp)
    zero_tail = jnp.zeros((1, HEAD_SLAB - D_QK), F32)
    gqh = jnp.concatenate([sh["q_head_norm"], zero_tail], axis=1)
    gkh = jnp.concatenate([sh["k_head_norm"], zero_tail], axis=1)
    wa = _block_diag(w["gate_a_w"][0]).astype(BF16)
    wx = _block_diag(w["gate_x_w"][0]).astype(BF16)

    (u1, a1, b1, s1), (g_uq, g_uk, g_uv, g_in, g_wd1) = ffn_up(
        h0, sh["ffn1_norm"], wg1, wu1, "ffn1_up",
        comm=([b16("w_uq"), b16("w_uk"), b16("w_uv"), b16("w_in"), b16("ffn1_w_down")], ["gather"] * 5))
    wd1 = g_wd1.reshape(D_FF, d)
    mla_rows = MLA_IN - D_ROPE
    w_in = g_in.reshape(mla_rows + 2 * LRU_WIDTH, d)
    wm = jnp.concatenate([w_in[:mla_rows], jnp.zeros((D_ROPE, d), BF16)], axis=0)
    wl = w_in[mla_rows:]
    (h1, u2, zm, zl), _ = ffn_down_inproj(h0, s1, wd1, sh["mix_norm"], wm, wl, "ffn1_down_inproj")
    wuq = _slab_rows(g_uq.reshape(MLA_HEADS * D_QK, Q_RANK), D_QK)
    wuk = _slab_rows(g_uk.reshape(MLA_HEADS * D_NOPE, KV_RANK), D_NOPE)
    wuv = g_uv.reshape(MLA_HEADS * D_V, KV_RANK)

    q, k, vv, qn, cn = mla_prep_fwd(zm, sh["q_latent_norm"], sh["kv_latent_norm"], wuq, wuk, wuv, gqh, gkh, tables, lp)
    (y_mla, lse), (g_out, g_wu2, g_wd2) = attn_fwd(
        q, k, vv, nb, lp, comm=([b16("w_out"), b16("ffn2_w_up"), b16("ffn2_w_down")], ["gather"] * 3))
    w_out = g_out.reshape(d, d)
    (y_lru, hs), (g_wg2,) = lru_fwd(zl, conv_w, sh["conv_b"], wa, wx, sh["gate_a_b"], sh["gate_x_b"], sh["lru_lambda"],
                                    nb, lp, comm=([b16("ffn2_w_gate")], ["gather"]))
    wg2, wu2, wd2 = (g.reshape(D_FF, d) for g in (g_wg2, g_wu2, g_wd2))
    h2, yn = outproj_fwd(h1, y_mla, y_lru, sh["attn_out_norm"], sh["lru_out_norm"], w_out)
    dh3, u3, a3, b3, loss, g_final = ffn_fwd_loss(h2, sh["ffn2_norm"], wg2, wu2, wd2, sh["final_norm"], tgt_p, lp,
                                                  "ffn2_fwd_loss")

    vec = {"final_norm": g_final}
    (dh2, da3, db3, sh3, vec["ffn2_norm"]), _ = ffn_bwd_act(dh3, h2, sh["ffn2_norm"], a3, b3, wg2, wu2, wd2, "ffn2_bwd")
    ff_shards = (N_DEV, D_FF // N_DEV, d)
    dwg2 = tn_matmul(da3, u3, "ffn2_dwg", "bf16").reshape(ff_shards)
    dwu2 = tn_matmul(db3, u3, "ffn2_dwu", "bf16").reshape(ff_shards)
    dwd2 = tn_matmul(sh3, dh3, "ffn2_dwd", "bf16").reshape(ff_shards)

    dy_mla, dy_lru, vec["attn_out_norm"], vec["lru_out_norm"] = outproj_bwd(
        dh2, y_mla, y_lru, sh["attn_out_norm"], sh["lru_out_norm"], w_out)
    dw_out = tn_matmul(yn, dh2, "dw_out", "bf16").reshape(N_DEV, d // N_DEV, d)
    (du, dgate, dconv, vec["conv_b"], vec["gate_a_b"], vec["gate_x_b"], vec["lru_lambda"], dga, dgx), landed = lru_bwd(
        zl, hs, dy_lru, conv_w, sh["conv_b"], wa, wx, sh["gate_a_b"], sh["gate_x_b"], sh["lru_lambda"], nb, lp,
        comm=([dwg2, dw_out], ["scatter"] * 2))
    update("ffn2_w_gate", landed[0])
    update("w_out", landed[1])

    (dq, dk, dv), (r_wu2,) = attn_bwd(q, k, vv, y_mla, dy_mla, lse, nb, lp, comm=([dwu2], ["scatter"]))
    update("ffn2_w_up", r_wu2)

    (dzm, dqr, dkr, vec["q_latent_norm"], vec["kv_latent_norm"], vec["q_head_norm"], vec["k_head_norm"]), (r_wd2,) = (
        mla_prep_bwd(dq, dk, dv, zm, qn, cn, sh["q_latent_norm"], sh["kv_latent_norm"], wuq, wuk, wuv, gqh, gkh,
                     tables, lp, comm=([dwd2], ["scatter"])))
    update("ffn2_w_down", r_wd2)
    dwuq = _unslab_rows(tn_matmul(dqr, qn, "dw_uq", "bf16"), D_QK).reshape(N_DEV, -1, Q_RANK)
    dwuk = _unslab_rows(tn_matmul(dkr, cn, "dw_uk", "bf16"), D_NOPE).reshape(N_DEV, -1, KV_RANK)
    dwuv = tn_matmul(dv, cn, "dw_uv", "bf16").reshape(N_DEV, -1, KV_RANK)
    (dh1, vec["mix_norm"]), landed = inproj_bwd(dzm, du, dgate, dh2, h1, sh["mix_norm"], wm, wl,
                                                comm=([dwuq, dwuk, dwuv], ["scatter"] * 3))
    for name, r in zip(("w_uq", "w_uk", "w_uv"), landed):
        update(name, r)
    dw_in = jnp.concatenate([tn_matmul(dzm, u2, "dw_in_mla", "bf16")[:mla_rows], tn_matmul(du, u2, "dw_in_u", "bf16"),
                             tn_matmul(dgate, u2, "dw_in_gate", "bf16")], axis=0).reshape(N_DEV, -1, d)

    dwd1 = tn_matmul(s1, dh1, "ffn1_dwd", "bf16").reshape(ff_shards)
    (dh0, da1, db1, vec["ffn1_norm"]), landed = ffn_bwd_act(
        dh1, h0, sh["ffn1_norm"], a1, b1, wg1, wu1, wd1, "ffn1_bwd", emit_sh=False,
        comm=([dw_in, split_cols(dconv, "split_conv"), dwd1], ["scatter"] * 3))
    for name, r in zip(("w_in", "conv_w", "ffn1_w_down"), landed):
        update(name, r)

    dwg1 = tn_matmul(da1, u1, "ffn1_dwg", "bf16").reshape(ff_shards)
    dwu1, (r_wg1,) = tn_matmul(db1, u1, "ffn1_dwu", "bf16", comm=([dwg1], ["scatter"]))
    dmeta = meta_grad(dh0, nb, lp)
    gates = [dga.reshape(LRU_WIDTH, LRU_BLOCK), dgx.reshape(LRU_WIDTH, LRU_BLOCK)]
    r_vec, r_ga, r_gx, r_meta, r_wu1 = exchange(
        [pack_vectors(vec, loss)] + gates + [dmeta, dwu1.reshape(ff_shards)], ["gather"] * 3 + ["scatter"] * 2,
        "exchange_last")
    update("ffn1_w_gate", r_wg1)
    update("ffn1_w_up", r_wu1)
    update("meta_tokens", r_meta)

    small = [name for name, _ in VECTORS] + GATES
    res, total_loss = adamw_small(r_vec, [r_ga, r_gx], [sh[nm] for nm in small], [m2[nm] for nm in small],
                                  [v2[nm] for nm in small])
    for i, name in enumerate(small):
        out[name] = [res[j][i] for j in range(4)]

    grad_x = dh0.reshape(nb, lp, d)[:, FIRST_FRAME:]
    loss = total_loss[0, 0]

    def as_given(a, name):
        return (a.T if name in COLUMN_SHARDED else a).reshape(w[name].shape)

    cols = [[as_given(out[name][j], name) for name in _WEIGHT_NAMES] for j in range(4)]
    return (loss, grad_x, *cols[0], *cols[1], *cols[2], *cols[3])


def kernel(x, meta_tokens, ffn1_norm, ffn1_w_gate, ffn1_w_up, ffn1_w_down, mix_norm, w_in, q_latent_norm, w_uq, kv_latent_norm, w_uk, w_uv, q_head_norm, k_head_norm, conv_w, conv_b, gate_a_w, gate_a_b, gate_x_w, gate_x_b, lru_lambda, attn_out_norm, lru_out_norm, w_out, ffn2_norm, ffn2_w_gate, ffn2_w_up, ffn2_w_down, final_norm, loss_target, m_meta_tokens, m_ffn1_norm, m_ffn1_w_gate, m_ffn1_w_up, m_ffn1_w_down, m_mix_norm, m_w_in, m_q_latent_norm, m_w_uq, m_kv_latent_norm, m_w_uk, m_w_uv, m_q_head_norm, m_k_head_norm, m_conv_w, m_conv_b, m_gate_a_w, m_gate_a_b, m_gate_x_w, m_gate_x_b, m_lru_lambda, m_attn_out_norm, m_lru_out_norm, m_w_out, m_ffn2_norm, m_ffn2_w_gate, m_ffn2_w_up, m_ffn2_w_down, m_final_norm, v_meta_tokens, v_ffn1_norm, v_ffn1_w_gate, v_ffn1_w_up, v_ffn1_w_down, v_mix_norm, v_w_in, v_q_latent_norm, v_w_uq, v_kv_latent_norm, v_w_uk, v_w_uv, v_q_head_norm, v_k_head_norm, v_conv_w, v_conv_b, v_gate_a_w, v_gate_a_b, v_gate_x_w, v_gate_x_b, v_lru_lambda, v_attn_out_norm, v_lru_out_norm, v_w_out, v_ffn2_norm, v_ffn2_w_gate, v_ffn2_w_up, v_ffn2_w_down, v_final_norm):
    args = locals()
    w = {name: args[name] for name in _WEIGHT_NAMES}
    m = {name: args["m_" + name] for name in _WEIGHT_NAMES}
    v = {name: args["v_" + name] for name in _WEIGHT_NAMES}
    return train_step(x, loss_target, w, m, v)
```

```python
import math

import jax
import jax.numpy as jnp
from jax import lax
from jax.experimental import pallas as pl
from jax.experimental.pallas import tpu as pltpu

F32 = jnp.float32
BF16 = jnp.bfloat16

D_MODEL = 1024
CHUNK = 64
CHUNK_SHIFT = 6
N_META = 16
PAD = CHUNK - N_META
FIRST_FRAME = PAD + N_META
MLA_HEADS = 4
D_NOPE = 128
D_ROPE = 64
D_QK = D_NOPE + D_ROPE
D_V = 128
HEAD_SLAB = 256
KV_RANK = 256
Q_RANK = 384
ROPE_THETA = 10000.0
LRU_WIDTH = 512
LRU_BLOCKS = 8
LRU_BLOCK = 64
LRU_TILE = 128
CONV_W = 4
C_RGLRU = 8.0
D_FF = 2816
MLA_IN = 768
EPS = 1e-6
NEG_INF = -1e30
N_DEV = 8
LANES = 128
VMEM_LIMIT = 52 * 1024 * 1024
ATTN_HEADS_PER_STEP = 2
TN_ROWS = 4224
TN_X_BYTES = 12 * 1024 * 1024
TN_Y_BYTES = 9 * 1024 * 1024 // 2

ADAM_LR = 0.001
ADAM_B1 = 0.9
ADAM_B2 = 0.999
ADAM_EPS = 1e-08
ADAM_WD = 0.01
ADAM_STEP = 10

VMEM_WHOLE = pl.BlockSpec(memory_space=pltpu.VMEM)
HBM_WHOLE = pl.BlockSpec(memory_space=pl.ANY)


def _params(sems):
    if sems is None:
        return pltpu.CompilerParams(vmem_limit_bytes=VMEM_LIMIT)
    return pltpu.CompilerParams(dimension_semantics=sems, vmem_limit_bytes=VMEM_LIMIT)


def _tile(n, cap, mult=16):
    best = None
    for t in range(mult, min(n, cap) + 1, mult):
        if n % t == 0:
            best = t
    assert best is not None, (n, cap, mult)
    return best


def _row(tm, d):
    return pl.BlockSpec((tm, d), lambda i: (i, 0))


def _fixed(shape):
    return pl.BlockSpec(shape, lambda i: (0,) * len(shape))


def _mesh_position():
    return lax.axis_index("x"), lax.axis_index("y"), lax.axis_index("c")


def _flat_index(x, y, c):
    return 4 * x + 2 * y + c


def _peers(x, y, c):
    out = []
    for k in range(1, N_DEV):
        fx, fy, fc = (k >> 2) & 1, (k >> 1) & 1, k & 1
        out.append((1 - x if fx else x, 1 - y if fy else y, 1 - c if fc else c))
    return out


def _comm_out_shapes(srcs, modes):
    return [jax.ShapeDtypeStruct((N_DEV,) + s.shape if md == "gather" else s.shape, s.dtype)
            for s, md in zip(srcs, modes)]


def _comm_scratch(n):
    per_peer = n * (N_DEV - 1)
    return [pltpu.SemaphoreType.DMA((per_peer,)), pltpu.SemaphoreType.DMA((per_peer,)), pltpu.SemaphoreType.DMA((n,))]


class _Copies:
    def __init__(self, own, first, relay):
        self.own, self.first, self.relay = own, first, relay

    def start(self):
        for cp in self.own + self.first:
            cp.start()

    def forward(self):
        for arrival, onward in self.relay:
            arrival.wait_recv()
            onward.start()

    def finish(self):
        arrivals = [a for a, _ in self.relay]
        onward = [f for _, f in self.relay]
        for cp in self.first + onward:
            if not any(cp is a for a in arrivals):
                cp.wait_recv()
        for cp in self.first + onward:
            cp.wait_send()
        for cp in self.own:
            cp.wait()


def _comm_copies(src_refs, dst_refs, modes, send, recv, local):
    x, y, c = _mesh_position()
    me = _flat_index(x, y, c)
    n = len(modes)
    sibling = (x, y, 1 - c)
    chips = [(1 - x, y), (x, 1 - y), (1 - x, 1 - y)]

    def remote(src, dst, k, t, to):
        return pltpu.make_async_remote_copy(src_ref=src, dst_ref=dst, send_sem=send.at[k * n + t],
                                            recv_sem=recv.at[k * n + t], device_id=to,
                                            device_id_type=pl.DeviceIdType.MESH)

    own, first, relay = [], [], []
    for t, (src, dst, md) in enumerate(zip(src_refs, dst_refs, modes)):
        if md == "scatter":
            own.append(pltpu.make_async_copy(src.at[me], dst.at[me], local.at[t]))
            for k, peer in enumerate(_peers(x, y, c)):
                first.append(remote(src.at[_flat_index(*peer)], dst.at[me], k, t, peer))
        else:
            own.append(pltpu.make_async_copy(src, dst.at[me], local.at[t]))
            first.append(remote(src, dst.at[me], 0, t, sibling))
            for j, chip in enumerate(chips):
                arrival = remote(src, dst.at[me], 1 + j, t, (*chip, c))
                landed = dst.at[_flat_index(*chip, c)]
                first.append(arrival)
                relay.append((arrival, remote(landed, landed, 4 + j, t, sibling)))
    return _Copies(own, first, relay)


def _hosted(body, n_in, n_out, modes, grid):
    t = len(modes)
    total = math.prod(grid)

    def wrapped(*refs):
        ins, csrc = refs[:n_in], refs[n_in:n_in + t]
        outs = refs[n_in + t:n_in + t + n_out]
        cdst = refs[n_in + t + n_out:n_in + 2 * t + n_out]
        scratch = refs[n_in + 2 * t + n_out:-3]
        copies = _comm_copies(csrc, cdst, modes, *refs[-3:])
        step = pl.program_id(0)
        for axis in range(1, len(grid)):
            step = step * grid[axis] + pl.program_id(axis)

        @pl.when(step == 0)
        def _():
            copies.start()

        body(*ins, *outs, *scratch)

        @pl.when(step == (total * 4) // 5)
        def _():
            copies.forward()

        @pl.when(step == total - 1)
        def _():
            copies.finish()

    return wrapped


def _call(body, name, grid, in_specs, out_specs, out_shape, sems, args, scratch=(), comm=None):
    if comm is None:
        outs = pl.pallas_call(body, name=name, grid=grid, in_specs=in_specs, out_specs=out_specs, out_shape=out_shape,
                              scratch_shapes=list(scratch), compiler_params=_params(sems))(*args)
        return outs, []
    srcs, modes = comm
    n = len(modes)
    res = pl.pallas_call(
        _hosted(body, len(in_specs), len(out_specs), modes, grid), name=name, grid=grid,
        in_specs=list(in_specs) + [HBM_WHOLE] * n, out_specs=list(out_specs) + [HBM_WHOLE] * n,
        out_shape=list(out_shape) + _comm_out_shapes(srcs, modes),
        scratch_shapes=list(scratch) + _comm_scratch(n),
        compiler_params=_params(("arbitrary",) * len(grid)))(*args, *srcs)
    return res[:len(out_specs)], res[len(out_specs):]


def exchange(srcs, modes, name):
    n = len(modes)

    def body(*refs):
        copies = _comm_copies(refs[:n], refs[n:2 * n], modes, *refs[2 * n:])
        copies.start()
        copies.forward()
        copies.finish()

    return pl.pallas_call(body, name=name, in_specs=[HBM_WHOLE] * n, out_specs=[HBM_WHOLE] * n,
                          out_shape=_comm_out_shapes(srcs, modes), scratch_shapes=_comm_scratch(n))(*srcs)


def _nn(a, b):
    return jnp.dot(a, b, preferred_element_type=F32)


def _nt(a, b):
    return lax.dot_general(a, b, (((1,), (1,)), ((), ())), preferred_element_type=F32)


def _tn(a, b):
    return lax.dot_general(a, b, (((0,), (0,)), ((), ())), preferred_element_type=F32)


def _sig(x):
    return 1.0 / (1.0 + jnp.exp(-x))


def _rms_r(x, n=None):
    n = x.shape[-1] if n is None else n
    return lax.rsqrt(jnp.sum(x * x, axis=-1, keepdims=True) * (1.0 / n) + EPS)


def _rms_bwd(x, r, g, dy, n=None):
    n = x.shape[-1] if n is None else n
    xhat = x * r
    dxhat = dy * g
    dx = r * (dxhat - xhat * (jnp.sum(dxhat * xhat, axis=-1, keepdims=True) * (1.0 / n)))
    return dx, jnp.sum(dy * xhat, axis=0, keepdims=True)


def _accumulate(ref, val, first):
    @pl.when(first)
    def _():
        ref[...] = val

    @pl.when(jnp.logical_not(first))
    def _():
        ref[...] += val


_GELU_C = math.sqrt(2.0 / math.pi)


def _gelu_and_grad(x):
    inner = _GELU_C * (x + 0.044715 * x * x * x)
    t = jnp.tanh(inner)
    gelu = 0.5 * x * (1.0 + t)
    dgelu = 0.5 * (1.0 + t) + 0.5 * x * (1.0 - t * t) * _GELU_C * (1.0 + 3.0 * 0.044715 * x * x)
    return gelu, dgelu


def _log1p_small(t):
    return jnp.where(t < 1e-3, t * (1.0 - t * (0.5 - t * (1.0 / 3.0))), jnp.log(1.0 + t))


def _softplus(x):
    return jnp.maximum(x, 0.0) + _log1p_small(jnp.exp(-jnp.abs(x)))


def _sig_tanh(x):
    return 0.5 + 0.5 * jnp.tanh(0.5 * x)


def _ff_chunks(f):
    return 2 if (f // 2) % LANES == 0 else 1


def _swiglu_half(x, g_ref, wg_ref, wu_ref, wd_ref, a_ref, b_ref, fc):
    f = wg_ref.shape[0]
    u = (x * _rms_r(x) * g_ref[...]).astype(BF16)
    acc = jnp.zeros(x.shape, F32)
    for c in range(f // fc):
        cols = slice(c * fc, (c + 1) * fc)
        a = _nt(u, wg_ref[cols, :])
        b = _nt(u, wu_ref[cols, :])
        s = (a * _sig(a) * b).astype(BF16)
        acc = acc + _nn(s, wd_ref[cols, :])
        a_ref[:, cols] = a.astype(BF16)
        b_ref[:, cols] = b.astype(BF16)
    return x + 0.5 * acc, u


def ffn_up(h, g, wg, wu, name, comm=None):
    n, d = h.shape
    f = wg.shape[0]
    tm = _tile(n, 528)
    fc = 2 * LANES if f % (2 * LANES) == 0 else f

    def body(h_ref, g_ref, wg_ref, wu_ref, u_ref, a_ref, b_ref, s_ref):
        x = h_ref[...]
        u = (x * _rms_r(x) * g_ref[...]).astype(BF16)
        u_ref[...] = u
        for c in range(f // fc):
            cols = slice(c * fc, (c + 1) * fc)
            a = _nt(u, wg_ref[cols, :])
            b = _nt(u, wu_ref[cols, :])
            a_ref[:, cols] = a.astype(BF16)
            b_ref[:, cols] = b.astype(BF16)
            s_ref[:, cols] = (0.5 * (a * _sig(a) * b)).astype(BF16)

    wide = jax.ShapeDtypeStruct((n, f), BF16)
    return _call(
        body, name, (n // tm,),
        [_row(tm, d), _fixed((1, d)), VMEM_WHOLE, VMEM_WHOLE],
        [_row(tm, d), _row(tm, f), _row(tm, f), _row(tm, f)],
        [jax.ShapeDtypeStruct((n, d), BF16), wide, wide, wide],
        ("parallel",), (h, g, wg, wu), comm=comm)


def ffn_down_inproj(h, s, wd, g, wm, wl, name, comm=None):
    n, d = h.shape
    f = wd.shape[0]
    tm = _tile(n, 528)

    def body(h_ref, s_ref, wd_ref, g_ref, wm_ref, wl_ref, ho_ref, u_ref, zm_ref, zl_ref):
        x = h_ref[...] + _nn(s_ref[...], wd_ref[...])
        ho_ref[...] = x
        u = (x * _rms_r(x) * g_ref[...]).astype(BF16)
        u_ref[...] = u
        zm_ref[...] = _nt(u, wm_ref[...])
        zl_ref[...] = _nt(u, wl_ref[...])

    return _call(
        body, name, (n // tm,),
        [_row(tm, d), _row(tm, f), VMEM_WHOLE, _fixed((1, d)), VMEM_WHOLE, VMEM_WHOLE],
        [_row(tm, d), _row(tm, d), _row(tm, MLA_IN), _row(tm, 2 * LRU_WIDTH)],
        [jax.ShapeDtypeStruct((n, d), F32), jax.ShapeDtypeStruct((n, d), BF16),
         jax.ShapeDtypeStruct((n, MLA_IN), F32), jax.ShapeDtypeStruct((n, 2 * LRU_WIDTH), F32)],
        ("parallel",), (h, s, wd, g, wm, wl), comm=comm)


def ffn_fwd_loss(h, g, wg, wu, wd, g_final, tgt, lp, name):
    n, d = h.shape
    f = wg.shape[0]
    tm = _tile(lp, 528)
    per_seq = lp // tm
    fc = 2 * LANES if f % (2 * LANES) == 0 else f

    def body(h_ref, g_ref, wg_ref, wu_ref, wd_ref, gf_ref, t_ref, dh_ref, u_ref, a_ref, b_ref, loss_ref, dgf_ref):
        i = pl.program_id(0)
        y, u_ref[...] = _swiglu_half(h_ref[...], g_ref, wg_ref, wu_ref, wd_ref, a_ref, b_ref, fc)
        dh_ref[...], part, dg = _loss_and_grad(y, gf_ref[...], t_ref[...], (i % per_seq) * tm)
        _accumulate(loss_ref, jnp.broadcast_to(part, (1, LANES)), i == 0)
        _accumulate(dgf_ref, dg, i == 0)

    outs, _ = _call(
        body, name, (n // tm,),
        [_row(tm, d), _fixed((1, d)), VMEM_WHOLE, VMEM_WHOLE, VMEM_WHOLE, _fixed((1, d)), _row(tm, d)],
        [_row(tm, d), _row(tm, d), _row(tm, f), _row(tm, f), _fixed((1, LANES)), _fixed((1, d))],
        [jax.ShapeDtypeStruct((n, d), F32), jax.ShapeDtypeStruct((n, d), BF16),
         jax.ShapeDtypeStruct((n, f), BF16), jax.ShapeDtypeStruct((n, f), BF16),
         jax.ShapeDtypeStruct((1, LANES), F32), jax.ShapeDtypeStruct((1, d), F32)],
        ("arbitrary",), (h, g, wg, wu, wd, g_final, tgt))
    return outs


def ffn_bwd_act(dh, h, g, a, b, wg, wu, wd, name, comm=None, emit_sh=True):
    n, d = h.shape
    f = wg.shape[0]
    tm = _tile(n, 352 if emit_sh else 384)
    nc = _ff_chunks(f)
    fc = f // nc

    def body(dh_ref, h_ref, g_ref, a_ref, b_ref, wg_ref, wu_ref, wd_ref, dhi_ref, da_ref, db_ref, *rest):
        dg_ref = rest[-1]
        x = h_ref[...]
        dy = dh_ref[...]
        r = _rms_r(x)
        dhh = (0.5 * dy).astype(BF16)
        du = jnp.zeros((tm, d), F32)
        for c in range(nc):
            cols = slice(c * fc, (c + 1) * fc)
            ds = _nt(dhh, wd_ref[cols, :])
            av = a_ref[:, cols].astype(F32)
            bv = b_ref[:, cols].astype(F32)
            sg = _sig(av)
            sil = av * sg
            da = (ds * bv * (sg * (1.0 + av * (1.0 - sg)))).astype(BF16)
            db = (ds * sil).astype(BF16)
            da_ref[:, cols] = da
            db_ref[:, cols] = db
            if emit_sh:
                rest[0][:, cols] = (0.5 * sil * bv).astype(BF16)
            du = du + _nn(da, wg_ref[cols, :]) + _nn(db, wu_ref[cols, :])
        dx, dg = _rms_bwd(x, r, g_ref[...], du)
        dhi_ref[...] = dy + dx
        _accumulate(dg_ref, dg, pl.program_id(0) == 0)

    wide = [jax.ShapeDtypeStruct((n, f), BF16)] * (3 if emit_sh else 2)
    return _call(
        body, name, (n // tm,),
        [_row(tm, d), _row(tm, d), _fixed((1, d)), _row(tm, f), _row(tm, f), VMEM_WHOLE, VMEM_WHOLE, VMEM_WHOLE],
        [_row(tm, d)] + [_row(tm, f)] * len(wide) + [_fixed((1, d))],
        [jax.ShapeDtypeStruct((n, d), F32)] + wide + [jax.ShapeDtypeStruct((1, d), F32)],
        ("arbitrary",), (dh, h, g, a, b, wg, wu, wd), comm=comm)


def tn_matmul(x, y, name, out="f32", comm=None):
    n, k = x.shape
    m = y.shape[1]
    tm = _tile(n, TN_ROWS)
    kc, mc = k, (512 if m % 512 == 0 else m)
    while tm * kc * x.dtype.itemsize > TN_X_BYTES and kc % (2 * LANES) == 0:
        kc //= 2
    while tm * mc * y.dtype.itemsize > TN_Y_BYTES and mc % (2 * LANES) == 0:
        mc //= 2
    steps = n // tm

    def body(x_ref, y_ref, o_ref, *acc):
        i = pl.program_id(2)
        part = _tn(x_ref[...].astype(BF16), y_ref[...].astype(BF16))
        if steps == 1:
            o_ref[...] = part.astype(o_ref.dtype)
        elif out == "f32":
            _accumulate(o_ref, part, i == 0)
        else:
            _accumulate(acc[0], part, i == 0)

            @pl.when(i == steps - 1)
            def _():
                o_ref[...] = acc[0][...].astype(BF16)

    out_shape = jax.ShapeDtypeStruct((k, m), F32 if out == "f32" else BF16)
    (res,), landed = _call(
        body, name, (k // kc, m // mc, steps),
        [pl.BlockSpec((tm, kc), lambda a, b, i: (i, a)), pl.BlockSpec((tm, mc), lambda a, b, i: (i, b))],
        [pl.BlockSpec((kc, mc), lambda a, b, i: (a, b))], [out_shape], ("parallel", "parallel", "arbitrary"), (x, y),
        scratch=[pltpu.VMEM((kc, mc), F32)] if (out == "bf16" and steps > 1) else [], comm=comm)
    return (res, landed) if comm is not None else res


def inproj_bwd(dzm, du, dgate, dh2, h, g, wm, wl, comm=None):
    n, d = h.shape
    tm = _tile(n, 352)

    def body(dzm_ref, du_ref, dgt_ref, dh2_ref, h_ref, g_ref, wm_ref, wl_ref, dh_ref, dg_ref):
        x = h_ref[...]
        dun = (_nn(dzm_ref[...].astype(BF16), wm_ref[...])
               + _nn(du_ref[...].astype(BF16), wl_ref[:LRU_WIDTH, :])
               + _nn(dgt_ref[...].astype(BF16), wl_ref[LRU_WIDTH:, :]))
        dx, dg = _rms_bwd(x, _rms_r(x), g_ref[...], dun)
        dh_ref[...] = dh2_ref[...] + dx
        _accumulate(dg_ref, dg, pl.program_id(0) == 0)

    return _call(
        body, "inproj_bwd", (n // tm,),
        [_row(tm, MLA_IN), _row(tm, LRU_WIDTH), _row(tm, LRU_WIDTH), _row(tm, d), _row(tm, d),
         _fixed((1, d)), VMEM_WHOLE, VMEM_WHOLE],
        [_row(tm, d), _fixed((1, d))],
        [jax.ShapeDtypeStruct((n, d), F32), jax.ShapeDtypeStruct((1, d), F32)],
        ("arbitrary",), (dzm, du, dgate, dh2, h, g, wm, wl), comm=comm)


def _rope_tables(lp):
    pos = jnp.arange(lp, dtype=F32) - float(PAD)
    half = D_ROPE // 2
    inv_freq = ROPE_THETA ** (-jnp.arange(0, half, dtype=F32) / half)
    ang = pos[:, None] * inv_freq[None, :]
    cos, sin = jnp.cos(ang), jnp.sin(ang)
    one = jnp.ones((lp, D_NOPE), F32)
    z_nope = jnp.zeros((lp, D_NOPE), F32)
    z_half = jnp.zeros((lp, half), F32)
    z_tail = jnp.zeros((lp, HEAD_SLAB - D_QK), F32)
    cosr = jnp.concatenate([one, cos, cos, z_tail], axis=1)
    sin_up = jnp.concatenate([z_nope, z_half, sin, z_tail], axis=1)
    sin_dn = jnp.concatenate([z_nope, -sin, z_half, z_tail], axis=1)
    return cosr, sin_up, sin_dn


def _rope(x, cosr, sin_up, sin_dn):
    half = D_ROPE // 2
    return x * cosr + pltpu.roll(x, half, axis=1) * sin_up + pltpu.roll(x, HEAD_SLAB - half, axis=1) * sin_dn


def _rope_bwd(dy, cosr, sin_up, sin_dn):
    half = D_ROPE // 2
    return (dy * cosr + pltpu.roll(dy * sin_up, HEAD_SLAB - half, axis=1)
            + pltpu.roll(dy * sin_dn, half, axis=1))


def _k_rope_slab(zm_tile):
    tm = zm_tile.shape[0]
    krp = zm_tile[:, Q_RANK + KV_RANK:MLA_IN]
    return jnp.concatenate([jnp.zeros((tm, D_NOPE), F32), krp], axis=1)


def mla_prep_fwd(zm, gql, gkvl, wuq, wuk, wuv, gqh, gkh, tables, lp):
    n = zm.shape[0]
    tm = _tile(lp, 352)
    per_seq = lp // tm
    width = MLA_HEADS * HEAD_SLAB
    scale = 1.0 / math.sqrt(D_QK)

    def body(zm_ref, gql_ref, gkvl_ref, wuq_ref, wuk_ref, wuv_ref, gqh_ref, gkh_ref,
             cos_ref, up_ref, dn_ref, q_ref, k_ref, v_ref, qn_ref, cn_ref):
        z = zm_ref[...]
        cq = z[:, :Q_RANK]
        ckv = z[:, Q_RANK:Q_RANK + KV_RANK]
        qn = (cq * _rms_r(cq) * gql_ref[...]).astype(BF16)
        cn = (ckv * _rms_r(ckv) * gkvl_ref[...]).astype(BF16)
        qn_ref[...] = qn
        cn_ref[...] = cn
        q_raw = _nt(qn, wuq_ref[...])
        k_raw = _nt(cn, wuk_ref[...])
        v_ref[...] = _nt(cn, wuv_ref[...]).astype(BF16)
        kr_slab = _k_rope_slab(z)
        cosr, sin_up, sin_dn = cos_ref[...], up_ref[...], dn_ref[...]
        for hd in range(MLA_HEADS):
            cols = slice(hd * HEAD_SLAB, (hd + 1) * HEAD_SLAB)
            xq = q_raw[:, cols]
            yq = _rope(xq * _rms_r(xq, D_QK) * gqh_ref[...], cosr, sin_up, sin_dn)
            q_ref[:, cols] = (yq * scale).astype(BF16)
            xk = k_raw[:, cols] + kr_slab
            yk = _rope(xk * _rms_r(xk, D_QK) * gkh_ref[...], cosr, sin_up, sin_dn)
            k_ref[:, cols] = yk.astype(BF16)

    tab = pl.BlockSpec((tm, HEAD_SLAB), lambda i: (i % per_seq, 0))
    return pl.pallas_call(
        body, name="mla_prep_fwd", grid=(n // tm,),
        in_specs=[_row(tm, MLA_IN), _fixed((1, Q_RANK)), _fixed((1, KV_RANK)), VMEM_WHOLE, VMEM_WHOLE, VMEM_WHOLE,
                  _fixed((1, HEAD_SLAB)), _fixed((1, HEAD_SLAB)), tab, tab, tab],
        out_specs=[_row(tm, width), _row(tm, width), _row(tm, MLA_HEADS * D_V), _row(tm, Q_RANK), _row(tm, KV_RANK)],
        out_shape=[jax.ShapeDtypeStruct((n, width), BF16), jax.ShapeDtypeStruct((n, width), BF16),
                   jax.ShapeDtypeStruct((n, MLA_HEADS * D_V), BF16), jax.ShapeDtypeStruct((n, Q_RANK), BF16),
                   jax.ShapeDtypeStruct((n, KV_RANK), BF16)],
        compiler_params=_params(("parallel",)),
    )(zm, gql, gkvl, wuq, wuk, wuv, gqh, gkh, *tables)


def mla_prep_bwd(dq, dk, dv, zm, qn, cn, gql, gkvl, wuq, wuk, wuv, gqh, gkh, tables, lp, comm=None):
    n = zm.shape[0]
    tm = _tile(lp, 704)
    per_seq = lp // tm
    width = MLA_HEADS * HEAD_SLAB
    scale = 1.0 / math.sqrt(D_QK)

    def body(dq_ref, dk_ref, dv_ref, zm_ref, qn_ref, cn_ref, gql_ref, gkvl_ref, wuq_ref, wuk_ref, wuv_ref,
             gqh_ref, gkh_ref, cos_ref, up_ref, dn_ref,
             dzm_ref, dqr_ref, dkr_ref, dgql_ref, dgkvl_ref, dgqh_ref, dgkh_ref):
        z = zm_ref[...]
        cq = z[:, :Q_RANK]
        ckv = z[:, Q_RANK:Q_RANK + KV_RANK]
        q_raw = _nt(qn_ref[...], wuq_ref[...])
        k_raw = _nt(cn_ref[...], wuk_ref[...])
        kr_slab = _k_rope_slab(z)
        cosr, sin_up, sin_dn = cos_ref[...], up_ref[...], dn_ref[...]
        dgq = jnp.zeros((1, HEAD_SLAB), F32)
        dgk = jnp.zeros((1, HEAD_SLAB), F32)
        dkrp = jnp.zeros((tm, HEAD_SLAB - D_NOPE), F32)
        for hd in range(MLA_HEADS):
            cols = slice(hd * HEAD_SLAB, (hd + 1) * HEAD_SLAB)
            xq = q_raw[:, cols]
            dxn = _rope_bwd(dq_ref[:, cols] * scale, cosr, sin_up, sin_dn)
            dxq, dg = _rms_bwd(xq, _rms_r(xq, D_QK), gqh_ref[...], dxn, D_QK)
            dgq = dgq + dg
            dqr_ref[:, cols] = dxq.astype(BF16)
            xk = k_raw[:, cols] + kr_slab
            dxn = _rope_bwd(dk_ref[:, cols], cosr, sin_up, sin_dn)
            dxk, dg = _rms_bwd(xk, _rms_r(xk, D_QK), gkh_ref[...], dxn, D_QK)
            dgk = dgk + dg
            dkr_ref[:, cols] = dxk.astype(BF16)
            dkrp = dkrp + dxk[:, D_NOPE:]
        dqn = _nn(dqr_ref[...], wuq_ref[...])
        dcn = _nn(dkr_ref[...], wuk_ref[...]) + _nn(dv_ref[...].astype(BF16), wuv_ref[...])
        dcq, dg1 = _rms_bwd(cq, _rms_r(cq), gql_ref[...], dqn)
        dckv, dg2 = _rms_bwd(ckv, _rms_r(ckv), gkvl_ref[...], dcn)
        dzm_ref[:, :Q_RANK] = dcq.astype(BF16)
        dzm_ref[:, Q_RANK:Q_RANK + KV_RANK] = dckv.astype(BF16)
        dzm_ref[:, Q_RANK + KV_RANK:] = dkrp.astype(BF16)
        first = pl.program_id(0) == 0
        _accumulate(dgql_ref, dg1, first)
        _accumulate(dgkvl_ref, dg2, first)
        _accumulate(dgqh_ref, dgq, first)
        _accumulate(dgkh_ref, dgk, first)

    tab = pl.BlockSpec((tm, HEAD_SLAB), lambda i: (i % per_seq, 0))
    return _call(
        body, "mla_prep_bwd", (n // tm,),
        [_row(tm, width), _row(tm, width), _row(tm, MLA_HEADS * D_V), _row(tm, MLA_IN),
         _row(tm, Q_RANK), _row(tm, KV_RANK), _fixed((1, Q_RANK)), _fixed((1, KV_RANK)),
         VMEM_WHOLE, VMEM_WHOLE, VMEM_WHOLE, _fixed((1, HEAD_SLAB)), _fixed((1, HEAD_SLAB)), tab, tab, tab],
        [_row(tm, MLA_IN), _row(tm, width), _row(tm, width), _fixed((1, Q_RANK)), _fixed((1, KV_RANK)),
         _fixed((1, HEAD_SLAB)), _fixed((1, HEAD_SLAB))],
        [jax.ShapeDtypeStruct((n, MLA_IN), BF16), jax.ShapeDtypeStruct((n, width), BF16),
         jax.ShapeDtypeStruct((n, width), BF16), jax.ShapeDtypeStruct((1, Q_RANK), F32),
         jax.ShapeDtypeStruct((1, KV_RANK), F32), jax.ShapeDtypeStruct((1, HEAD_SLAB), F32),
         jax.ShapeDtypeStruct((1, HEAD_SLAB), F32)],
        ("arbitrary",), (dq, dk, dv, zm, qn, cn, gql, gkvl, wuq, wuk, wuv, gqh, gkh, *tables), comm=comm)


def _attn_tile(lp):
    return _tile(lp, 704, CHUNK)


def _chunk_mask(i, j, t):
    qpos = i * t + lax.broadcasted_iota(jnp.int32, (t, t), 0)
    kpos = j * t + lax.broadcasted_iota(jnp.int32, (t, t), 1)
    same_or_earlier = jnp.right_shift(kpos, CHUNK_SHIFT) <= jnp.right_shift(qpos, CHUNK_SHIFT)
    return jnp.logical_and(same_or_earlier, kpos >= PAD)


def _masked_scores(s, i, j, t, diagonal):
    if diagonal:
        return jnp.where(_chunk_mask(i, j, t), s, NEG_INF)
    kpos = j * t + lax.broadcasted_iota(jnp.int32, (1, t), 1)
    return s + jnp.where(kpos < PAD, NEG_INF, 0.0)


def attn_fwd(q, k, v, nb, lp, comm=None):
    n = q.shape[0]
    t = _attn_tile(lp)
    nq = lp // t

    hp = ATTN_HEADS_PER_STEP

    def body(q_ref, k_ref, v_ref, o_ref, lse_ref):
        i = pl.program_id(2)
        qs = [q_ref[:, hh * HEAD_SLAB:(hh + 1) * HEAD_SLAB] for hh in range(hp)]

        def kv_step(j, carry, diagonal=False):
            off = pl.multiple_of(j * t, t)
            out = []
            for hh in range(hp):
                m, l, acc = carry[hh]
                kv = k_ref[pl.ds(off, t), hh * HEAD_SLAB:(hh + 1) * HEAD_SLAB]
                s = _masked_scores(_nt(qs[hh], kv), i, j, t, diagonal)
                m_new = jnp.maximum(m, jnp.max(s, axis=-1, keepdims=True))
                p = jnp.exp(s - m_new)
                alpha = jnp.exp(m - m_new)
                l = alpha * l + jnp.sum(p, axis=-1, keepdims=True)
                acc = alpha * acc + _nn(p.astype(BF16), v_ref[pl.ds(off, t), hh * D_V:(hh + 1) * D_V])
                out.append((m_new, l, acc))
            return tuple(out)

        init = tuple((jnp.full((t, 1), NEG_INF, F32), jnp.zeros((t, 1), F32), jnp.zeros((t, D_V), F32))
                     for _ in range(hp))
        done = kv_step(i, lax.fori_loop(0, i, kv_step, init), diagonal=True)
        for hh, (m, l, acc) in enumerate(done):
            o_ref[:, hh * D_V:(hh + 1) * D_V] = acc * (1.0 / l)
            lse_ref[hh] = jnp.broadcast_to(m + jnp.log(l), (t, LANES))

    return _call(
        body, "attn_fwd", (nb, MLA_HEADS // hp, nq),
        [pl.BlockSpec((t, hp * HEAD_SLAB), lambda b, h, i: (b * nq + i, h)),
         pl.BlockSpec((lp, hp * HEAD_SLAB), lambda b, h, i: (b, h)),
         pl.BlockSpec((lp, hp * D_V), lambda b, h, i: (b, h))],
        [pl.BlockSpec((t, hp * D_V), lambda b, h, i: (b * nq + i, h)),
         pl.BlockSpec((hp, t, LANES), lambda b, h, i: (h, b * nq + i, 0))],
        [jax.ShapeDtypeStruct((n, MLA_HEADS * D_V), F32), jax.ShapeDtypeStruct((MLA_HEADS, n, LANES), F32)],
        ("parallel", "parallel", "parallel"), (q, k, v), comm=comm)


def attn_bwd(q, k, v, o, do, lse, nb, lp, comm=None):
    n = q.shape[0]
    t = _attn_tile(lp)
    nq = lp // t

    def body(q_ref, k_ref, v_ref, o_ref, do_ref, lse_ref, dq_ref, dk_ref, dv_ref):
        dk_ref[...] = jnp.zeros_like(dk_ref)
        dv_ref[...] = jnp.zeros_like(dv_ref)

        def q_step(i, _):
            qoff = pl.multiple_of(i * t, t)
            qv = q_ref[pl.ds(qoff, t), :]
            dov = do_ref[pl.ds(qoff, t), :]
            delta = jnp.sum(o_ref[pl.ds(qoff, t), :] * dov, axis=-1, keepdims=True)
            lse_q = jnp.max(lse_ref[0, pl.ds(qoff, t), :], axis=-1, keepdims=True)
            do16 = dov.astype(BF16)

            def kv_step(j, dq_acc, diagonal=False):
                koff = pl.multiple_of(j * t, t)
                kv = k_ref[pl.ds(koff, t), :]
                s = _masked_scores(_nt(qv, kv), i, j, t, diagonal)
                p = jnp.exp(s - lse_q)
                dp = _nt(do16, v_ref[pl.ds(koff, t), :])
                ds16 = (p * (dp - delta)).astype(BF16)
                dv_ref[pl.ds(koff, t), :] += _tn(p.astype(BF16), do16)
                dk_ref[pl.ds(koff, t), :] += _tn(ds16, qv)
                return dq_acc + _nn(ds16, kv)

            earlier = lax.fori_loop(0, i, kv_step, jnp.zeros((t, HEAD_SLAB), F32))
            dq_ref[pl.ds(qoff, t), :] = kv_step(i, earlier, diagonal=True)
            return 0

        lax.fori_loop(0, nq, q_step, 0)

    wide = pl.BlockSpec((lp, HEAD_SLAB), lambda b, h: (b, h))
    thin = pl.BlockSpec((lp, D_V), lambda b, h: (b, h))
    width = MLA_HEADS * HEAD_SLAB
    return _call(
        body, "attn_bwd", (nb, MLA_HEADS),
        [wide, wide, thin, thin, thin, pl.BlockSpec((1, lp, LANES), lambda b, h: (h, b, 0))],
        [wide, wide, thin],
        [jax.ShapeDtypeStruct((n, width), F32), jax.ShapeDtypeStruct((n, width), F32),
         jax.ShapeDtypeStruct((n, MLA_HEADS * D_V), F32)],
        ("parallel", "parallel"), (q, k, v, o, do, lse), comm=comm)


def _seq_rows(nb, lp, width):
    rows = lax.broadcasted_iota(jnp.int32, (lp, width), 0)
    return jnp.concatenate([rows] * nb, axis=0) if nb > 1 else rows


def _lru_gates(u, w_ref, cb, wa, wx, ba, bx, lam):
    xc = (cb + w_ref[pl.ds(3, 1), :] * u + w_ref[pl.ds(2, 1), :] * pltpu.roll(u, 1, axis=0)
          + w_ref[pl.ds(1, 1), :] * pltpu.roll(u, 2, axis=0) + w_ref[pl.ds(0, 1), :] * pltpu.roll(u, 3, axis=0))
    xc16 = xc.astype(BF16)
    ra = _sig_tanh(_nn(xc16, wa) + ba)
    ia = _sig_tanh(_nn(xc16, wx) + bx)
    sp = _softplus(-lam)
    log_a = -C_RGLRU * ra * sp
    a = jnp.exp(log_a)
    x2 = 2.0 * log_a
    mult = jnp.sqrt(jnp.where(x2 > -1e-2, -x2 * (1.0 + x2 * (0.5 + x2 * (1.0 / 6.0))), 1.0 - a * a))
    return xc, xc16, ra, ia, sp, a, mult


def _scan_block_rows(width):
    return lax.broadcasted_iota(jnp.int32, (8, width), 0)


def lru_fwd(zl, conv_w, conv_b, wa, wx, ba, bx, lam, nb, lp, comm=None):
    n = zl.shape[0]
    w = LRU_TILE
    nt = LRU_WIDTH // w
    nblk = lp // 8

    def body(u_ref, gt_ref, cw_ref, cb_ref, wa_ref, wx_ref, ba_ref, bx_ref, lam_ref, y_ref, h_ref, a_s, b_s):
        u = u_ref[...]
        xc, _, _, ia, _, a, mult = _lru_gates(u, cw_ref, cb_ref[...], wa_ref[...], wx_ref[...],
                                              ba_ref[...], bx_ref[...], lam_ref[...])
        row = _seq_rows(nb, lp, w)
        mult = jnp.where(row == PAD, 1.0, mult)
        a_s[...] = a
        b_s[...] = jnp.where(row < PAD, 0.0, mult * (ia * xc))
        r8 = _scan_block_rows(w)

        def blk(i, carry):
            out = []
            for s_id in range(nb):
                off = pl.multiple_of(s_id * lp + i * 8, 8)
                av = a_s[pl.ds(off, 8), :]
                bv = b_s[pl.ds(off, 8), :]
                for sh in (1, 2, 4):
                    keep = r8 >= sh
                    bv = jnp.where(keep, av * pltpu.roll(bv, sh, axis=0) + bv, bv)
                    av = jnp.where(keep, av * pltpu.roll(av, sh, axis=0), av)
                hv = bv + av * carry[s_id]
                h_ref[pl.ds(off, 8), :] = hv
                out.append(jnp.sum(jnp.where(r8 == 7, hv, 0.0), axis=0, keepdims=True))
            return tuple(out)

        lax.fori_loop(0, nblk, blk, tuple(jnp.zeros((1, w), F32) for _ in range(nb)))
        gelu, _ = _gelu_and_grad(gt_ref[...])
        y_ref[...] = h_ref[...] * gelu

    col = lambda c: (0, c)
    return _call(
        body, "lru_fwd", (nt,),
        [pl.BlockSpec((n, w), col), pl.BlockSpec((n, w), lambda c: (0, nt + c)),
         pl.BlockSpec((CONV_W, w), col), pl.BlockSpec((1, w), col),
         pl.BlockSpec((w, w), lambda c: (c, c)), pl.BlockSpec((w, w), lambda c: (c, c)),
         pl.BlockSpec((1, w), col), pl.BlockSpec((1, w), col), pl.BlockSpec((1, w), col)],
        [pl.BlockSpec((n, w), col), pl.BlockSpec((n, w), col)],
        [jax.ShapeDtypeStruct((n, LRU_WIDTH), F32), jax.ShapeDtypeStruct((n, LRU_WIDTH), F32)],
        ("parallel",), (zl, zl, conv_w, conv_b, wa, wx, ba, bx, lam),
        scratch=[pltpu.VMEM((n, w), F32), pltpu.VMEM((n, w), F32)], comm=comm)


def lru_bwd(zl, hs, dy, conv_w, conv_b, wa, wx, ba, bx, lam, nb, lp, comm=None):
    n = zl.shape[0]
    w = LRU_TILE
    nt = LRU_WIDTH // w
    nblk = lp // 8

    def body(u_ref, gt_ref, h_ref, dy_ref, cw_ref, cb_ref, wa_ref, wx_ref, ba_ref, bx_ref, lam_ref,
             du_ref, dgt_ref, dcw_ref, dcb_ref, dba_ref, dbx_ref, dlam_ref, dwa_ref, dwx_ref,
             c_s, d_s, g_s, dwa_s, dwx_s):
        u = u_ref[...]
        lam = lam_ref[...]
        xc, xc16, ra, ia, sp, a, mult = _lru_gates(u, cw_ref, cb_ref[...], wa_ref[...], wx_ref[...],
                                                   ba_ref[...], bx_ref[...], lam)
        row = lax.broadcasted_iota(jnp.int32, (lp, w), 0)
        hv = h_ref[...]
        dyv = dy_ref[...]
        gelu, dgelu = _gelu_and_grad(gt_ref[...])
        dgt_ref[...] = jnp.where(row >= PAD, dyv * hv * dgelu, 0.0).astype(BF16)
        c_s[...] = pltpu.roll(a, lp - 1, axis=0)
        d_s[...] = dyv * gelu
        r8 = _scan_block_rows(w)

        def blk(ii, carry):
            off = pl.multiple_of((nblk - 1 - ii) * 8, 8)
            cv = c_s[pl.ds(off, 8), :]
            dv = d_s[pl.ds(off, 8), :]
            for sh in (1, 2, 4):
                keep = r8 < 8 - sh
                dv = jnp.where(keep, cv * pltpu.roll(dv, 8 - sh, axis=0) + dv, dv)
                cv = jnp.where(keep, cv * pltpu.roll(cv, 8 - sh, axis=0), cv)
            gv = dv + cv * carry
            g_s[pl.ds(off, 8), :] = gv
            return jnp.sum(jnp.where(r8 == 0, gv, 0.0), axis=0, keepdims=True)

        lax.fori_loop(0, nblk, blk, jnp.zeros((1, w), F32))
        gv = g_s[...]
        first_row = row == PAD
        db = jnp.where(row >= PAD, gv, 0.0)
        da = jnp.where(row > PAD, gv * pltpu.roll(hv, 1, axis=0), 0.0)
        mult_eff = jnp.where(first_row, 1.0, mult)
        dmult = jnp.where(first_row, 0.0, db * (ia * xc))
        dia = db * mult_eff * xc
        dxc = db * mult_eff * ia
        dla = da * a - dmult * (a * a) / mult
        dra = dla * (-C_RGLRU * sp)
        dsp = jnp.sum(dla * (-C_RGLRU * ra), axis=0, keepdims=True)
        dpa = dra * ra * (1.0 - ra)
        dpx = dia * ia * (1.0 - ia)
        dpa16 = dpa.astype(BF16)
        dpx16 = dpx.astype(BF16)
        dxc = dxc + _nt(dpa16, wa_ref[...]) + _nt(dpx16, wx_ref[...])
        du = cw_ref[pl.ds(CONV_W - 1, 1), :] * dxc
        dcw = [jnp.sum(dxc * u, axis=0, keepdims=True)]
        for tap in range(1, CONV_W):
            dcw.insert(0, jnp.sum(dxc * pltpu.roll(u, tap, axis=0), axis=0, keepdims=True))
            du = du + cw_ref[pl.ds(CONV_W - 1 - tap, 1), :] * pltpu.roll(dxc, lp - tap, axis=0)
        du_ref[...] = jnp.where(row >= PAD, du, 0.0).astype(BF16)
        first = pl.program_id(1) == 0
        _accumulate(dlam_ref, -_sig(-lam) * dsp, first)
        _accumulate(dba_ref, jnp.sum(dpa, axis=0, keepdims=True), first)
        _accumulate(dbx_ref, jnp.sum(dpx, axis=0, keepdims=True), first)
        _accumulate(dcb_ref, jnp.sum(dxc, axis=0, keepdims=True), first)
        _accumulate(dcw_ref, jnp.concatenate(dcw, axis=0), first)
        _accumulate(dwa_s, _tn(xc16, dpa16), first)
        _accumulate(dwx_s, _tn(xc16, dpx16), first)

        @pl.when(pl.program_id(1) == nb - 1)
        def _():
            for j in range(w // LRU_BLOCK):
                blk_rows = slice(j * LRU_BLOCK, (j + 1) * LRU_BLOCK)
                dwa_ref[0, blk_rows, :] = dwa_s[blk_rows, blk_rows]
                dwx_ref[0, blk_rows, :] = dwx_s[blk_rows, blk_rows]

    col = lambda c, b: (0, c)
    vec = pl.BlockSpec((1, w), col)
    mat = pl.BlockSpec((w, w), lambda c, b: (c, c))
    big = pl.BlockSpec((lp, w), lambda c, b: (b, c))
    dmat = pl.BlockSpec((1, w, LRU_BLOCK), lambda c, b: (c, 0, 0))
    return _call(
        body, "lru_bwd", (nt, nb),
        [big, pl.BlockSpec((lp, w), lambda c, b: (b, nt + c)), big, big,
         pl.BlockSpec((CONV_W, w), col), vec, mat, mat, vec, vec, vec],
        [big, big, pl.BlockSpec((CONV_W, w), col), vec, vec, vec, vec, dmat, dmat],
        [jax.ShapeDtypeStruct((n, LRU_WIDTH), BF16), jax.ShapeDtypeStruct((n, LRU_WIDTH), BF16),
         jax.ShapeDtypeStruct((CONV_W, LRU_WIDTH), F32), jax.ShapeDtypeStruct((1, LRU_WIDTH), F32),
         jax.ShapeDtypeStruct((1, LRU_WIDTH), F32), jax.ShapeDtypeStruct((1, LRU_WIDTH), F32),
         jax.ShapeDtypeStruct((1, LRU_WIDTH), F32), jax.ShapeDtypeStruct((nt, w, LRU_BLOCK), F32),
         jax.ShapeDtypeStruct((nt, w, LRU_BLOCK), F32)],
        ("parallel", "arbitrary"), (zl, zl, hs, dy, conv_w, conv_b, wa, wx, ba, bx, lam),
        scratch=[pltpu.VMEM((lp, w), F32), pltpu.VMEM((lp, w), F32), pltpu.VMEM((lp, w), F32),
                 pltpu.VMEM((w, w), F32), pltpu.VMEM((w, w), F32)], comm=comm)


def outproj_fwd(h, ya, yl, gao, glo, wout):
    n, d = h.shape
    half = ya.shape[1]
    tm = _tile(n, 704)

    def body(h_ref, ya_ref, yl_ref, gao_ref, glo_ref, w_ref, ho_ref, yn_ref):
        xa = ya_ref[...]
        xl = yl_ref[...]
        na = (xa * _rms_r(xa) * gao_ref[...]).astype(BF16)
        nl = (xl * _rms_r(xl) * glo_ref[...]).astype(BF16)
        yn_ref[:, :half] = na
        yn_ref[:, half:] = nl
        ho_ref[...] = h_ref[...] + _nn(na, w_ref[:half, :]) + _nn(nl, w_ref[half:, :])

    return pl.pallas_call(
        body, name="outproj_fwd", grid=(n // tm,),
        in_specs=[_row(tm, d), _row(tm, half), _row(tm, half), _fixed((1, half)), _fixed((1, half)), VMEM_WHOLE],
        out_specs=[_row(tm, d), _row(tm, 2 * half)],
        out_shape=[jax.ShapeDtypeStruct((n, d), F32), jax.ShapeDtypeStruct((n, 2 * half), BF16)],
        compiler_params=_params(("parallel",)),
    )(h, ya, yl, gao, glo, wout)


def outproj_bwd(dh, ya, yl, gao, glo, wout):
    n, d = dh.shape
    half = ya.shape[1]
    tm = _tile(n, 704)

    def body(dh_ref, ya_ref, yl_ref, gao_ref, glo_ref, w_ref, dya_ref, dyl_ref, dgao_ref, dglo_ref):
        d16 = dh_ref[...].astype(BF16)
        xa = ya_ref[...]
        xl = yl_ref[...]
        dxa, dga = _rms_bwd(xa, _rms_r(xa), gao_ref[...], _nt(d16, w_ref[:half, :]))
        dxl, dgl = _rms_bwd(xl, _rms_r(xl), glo_ref[...], _nt(d16, w_ref[half:, :]))
        dya_ref[...] = dxa
        dyl_ref[...] = dxl
        first = pl.program_id(0) == 0
        _accumulate(dgao_ref, dga, first)
        _accumulate(dglo_ref, dgl, first)

    return pl.pallas_call(
        body, name="outproj_bwd", grid=(n // tm,),
        in_specs=[_row(tm, d), _row(tm, half), _row(tm, half), _fixed((1, half)), _fixed((1, half)), VMEM_WHOLE],
        out_specs=[_row(tm, half), _row(tm, half), _fixed((1, half)), _fixed((1, half))],
        out_shape=[jax.ShapeDtypeStruct((n, half), F32), jax.ShapeDtypeStruct((n, half), F32),
                   jax.ShapeDtypeStruct((1, half), F32), jax.ShapeDtypeStruct((1, half), F32)],
        compiler_params=_params(("arbitrary",)),
    )(dh, ya, yl, gao, glo, wout)


def _loss_and_grad(x, gv, tgt, first_row):
    tm, d = x.shape
    r = _rms_r(x)
    row = first_row + lax.broadcasted_iota(jnp.int32, (tm, d), 0)
    diff = jnp.where(row >= FIRST_FRAME, x * r * gv - tgt, 0.0)
    part = 0.5 * jnp.sum(jnp.sum(diff * diff, axis=-1, keepdims=True) * (1.0 / d), axis=0, keepdims=True)
    dx, dg = _rms_bwd(x, r, gv, diff * (1.0 / d))
    return dx, part, dg


def assemble_cols(g, name):
    _, k, ns = g.shape

    def body(g_ref, o_ref):
        for j in range(N_DEV):
            o_ref[:, j * ns:(j + 1) * ns] = g_ref[j]

    return pl.pallas_call(body, name=name, out_shape=jax.ShapeDtypeStruct((k, N_DEV * ns), g.dtype),
                          compiler_params=_params(None))(g)


def split_cols(x, name):
    k, cols = x.shape
    ns = cols // N_DEV

    def body(x_ref, o_ref):
        for j in range(N_DEV):
            o_ref[j] = x_ref[:, j * ns:(j + 1) * ns]

    return pl.pallas_call(body, name=name, out_shape=jax.ShapeDtypeStruct((N_DEV, k, ns), x.dtype),
                          compiler_params=_params(None))(x)


def _slab_rows(w, per_head):
    k = w.shape[1]
    w = w.reshape(MLA_HEADS, per_head, k)
    return jnp.pad(w, ((0, 0), (0, HEAD_SLAB - per_head), (0, 0))).reshape(MLA_HEADS * HEAD_SLAB, k)


def _unslab_rows(w, per_head):
    k = w.shape[1]
    return w.reshape(MLA_HEADS, HEAD_SLAB, k)[:, :per_head].reshape(MLA_HEADS * per_head, k)


def meta_grad(dh0, nb, lp):
    d = dh0.shape[1]
    ns = d // N_DEV
    per_seq = lp // N_META

    def body(x_ref, o_ref):
        x = x_ref[...]
        for j in range(N_DEV):
            _accumulate(o_ref.at[j], x[:, j * ns:(j + 1) * ns], pl.program_id(0) == 0)

    return pl.pallas_call(
        body, name="meta_grad", grid=(nb,),
        in_specs=[pl.BlockSpec((N_META, d), lambda b: (b * per_seq + PAD // N_META, 0))],
        out_specs=pl.BlockSpec((N_DEV, N_META, ns), lambda b: (0, 0, 0)),
        out_shape=jax.ShapeDtypeStruct((N_DEV, N_META, ns), F32),
        compiler_params=_params(("arbitrary",)))(dh0)


VECTORS = [("ffn1_norm", 1024), ("mix_norm", 1024), ("q_latent_norm", 384), ("kv_latent_norm", 256),
           ("q_head_norm", 192), ("k_head_norm", 192), ("conv_b", 512), ("gate_a_b", 512), ("gate_x_b", 512),
           ("lru_lambda", 512), ("attn_out_norm", 512), ("lru_out_norm", 512), ("ffn2_norm", 1024),
           ("final_norm", 1024)]
VEC_ROWS = 16
LOSS_ROW = len(VECTORS)
GATES = ["gate_a_w", "gate_x_w"]


def pack_vectors(grads, loss):
    def body(*refs):
        o_ref = refs[-1]
        o_ref[...] = jnp.zeros_like(o_ref)
        for t, (ref, (_, cnt)) in enumerate(zip(refs[:-2], VECTORS)):
            o_ref[t:t + 1, :cnt] = ref[:, :cnt]
        o_ref[LOSS_ROW:LOSS_ROW + 1, :LANES] = refs[-2][...]

    return pl.pallas_call(body, name="pack_vectors", out_shape=jax.ShapeDtypeStruct((VEC_ROWS, D_MODEL), F32),
                          compiler_params=_params(None))(*[grads[name] for name, _ in VECTORS], loss)


def _adamw_update(w, g, m, v):
    c1 = 1.0 / (1.0 - ADAM_B1 ** ADAM_STEP)
    c2 = 1.0 / (1.0 - ADAM_B2 ** ADAM_STEP)
    mn = ADAM_B1 * m + (1.0 - ADAM_B1) * g
    vn = ADAM_B2 * v + (1.0 - ADAM_B2) * (g * g)
    delta = -ADAM_LR * ((mn * c1) / (jnp.sqrt(vn * c2) + ADAM_EPS) + ADAM_WD * w)
    return delta, mn, vn


def _sum_slots(ref, index=()):
    acc = ref[(0,) + index].astype(F32)
    for s in range(1, N_DEV):
        acc = acc + ref[(s,) + index].astype(F32)
    return acc


def adamw_sharded(r, w, m, v, name):
    rows, cols = w.shape
    tr = _tile(rows, 256, 16) if rows % 16 == 0 else rows

    def body(r_ref, w_ref, m_ref, v_ref, g_ref, d_ref, mo_ref, vo_ref):
        g = _sum_slots(r_ref)
        g_ref[...] = g
        d_ref[...], mo_ref[...], vo_ref[...] = _adamw_update(w_ref[...], g, m_ref[...], v_ref[...])

    spec = pl.BlockSpec((tr, cols), lambda i: (i, 0))
    shape = jax.ShapeDtypeStruct((rows, cols), F32)
    return pl.pallas_call(
        body, name=name, grid=(rows // tr,),
        in_specs=[pl.BlockSpec((N_DEV, tr, cols), lambda i: (0, i, 0))] + [spec] * 3,
        out_specs=[spec] * 4, out_shape=[shape] * 4,
        compiler_params=_params(("parallel",)),
    )(r, w, m, v)


def adamw_small(r_vec, r_gates, w, m, v):
    nt = len(VECTORS) + len(GATES)

    def body(*refs):
        rv_ref = refs[0]
        rg_refs = refs[1:1 + len(GATES)]
        base = 1 + len(GATES)
        w_refs, m_refs, v_refs = (refs[base + i * nt:base + (i + 1) * nt] for i in range(3))
        outs = refs[base + 3 * nt:]
        g_o, d_o, m_o, v_o = (outs[i * nt:(i + 1) * nt] for i in range(4))
        outs[4 * nt][...] = _sum_slots(rv_ref, (slice(LOSS_ROW, LOSS_ROW + 1), slice(0, LANES)))
        for t in range(nt):
            if t < len(VECTORS):
                cnt = VECTORS[t][1]
                g = _sum_slots(rv_ref, (slice(t, t + 1), slice(0, cnt)))
            else:
                g = _sum_slots(rg_refs[t - len(VECTORS)])
            g_o[t][...] = g
            d_o[t][...], m_o[t][...], v_o[t][...] = _adamw_update(w_refs[t][...], g, m_refs[t][...], v_refs[t][...])

    shapes = [jax.ShapeDtypeStruct(a.shape, F32) for a in w]
    res = pl.pallas_call(body, name="adamw_small", out_shape=shapes * 4 + [jax.ShapeDtypeStruct((1, LANES), F32)],
                         compiler_params=_params(None))(r_vec, *r_gates, *w, *m, *v)
    return [res[i * nt:(i + 1) * nt] for i in range(4)], res[4 * nt]


def _block_diag(w):
    nb, n, _ = w.shape
    eye = jnp.eye(nb, dtype=w.dtype)
    return (eye[:, None, :, None] * w[:, :, None, :]).reshape(nb * n, nb * n)


def _two_d(a):
    if a.ndim == 3:
        return a.reshape(a.shape[1], a.shape[2])
    if a.ndim == 4:
        return a.reshape(a.shape[1] * a.shape[2], a.shape[3])
    return a


_WEIGHT_NAMES = ['meta_tokens', 'ffn1_norm', 'ffn1_w_gate', 'ffn1_w_up', 'ffn1_w_down', 'mix_norm', 'w_in',
                 'q_latent_norm', 'w_uq', 'kv_latent_norm', 'w_uk', 'w_uv', 'q_head_norm', 'k_head_norm', 'conv_w',
                 'conv_b', 'gate_a_w', 'gate_a_b', 'gate_x_w', 'gate_x_b', 'lru_lambda', 'attn_out_norm',
                 'lru_out_norm', 'w_out', 'ffn2_norm', 'ffn2_w_gate', 'ffn2_w_up', 'ffn2_w_down', 'final_norm']


COLUMN_SHARDED = ("ffn1_w_gate", "ffn1_w_up", "ffn2_w_gate", "ffn2_w_up", "w_in", "w_uq", "w_uk", "w_uv")


def train_step(x, tgt, w, m, v):
    nb, seq, d = x.shape
    lp = PAD + N_META + seq
    n = nb * lp

    def local(a, name):
        a = _two_d(a)
        return a.T if name in COLUMN_SHARDED else a

    sh = {name: local(w[name], name) for name in _WEIGHT_NAMES}
    m2 = {name: local(m[name], name) for name in _WEIGHT_NAMES}
    v2 = {name: local(v[name], name) for name in _WEIGHT_NAMES}

    def b16(name):
        return sh[name].astype(BF16)

    out = {}

    def update(name, landed):
        out[name] = adamw_sharded(landed, sh[name], m2[name], v2[name], "adamw_" + name)

    g_meta, g_conv, g_wg1, g_wu1 = exchange(
        [sh["meta_tokens"], sh["conv_w"], b16("ffn1_w_gate"), b16("ffn1_w_up")], ["gather"] * 4, "gather_ffn1")
    wg1, wu1 = g_wg1.reshape(D_FF, d), g_wu1.reshape(D_FF, d)
    meta = assemble_cols(g_meta, "assemble_meta")
    conv_w = assemble_cols(g_conv, "assemble_conv")

    front = jnp.concatenate([jnp.zeros((PAD, d), F32), meta], axis=0)
    h0 = jnp.concatenate([jnp.broadcast_to(front[None], (nb, FIRST_FRAME, d)), x], axis=1).reshape(n, d)
    tgt_p = jnp.concatenate([jnp.zeros((nb, FIRST_FRAME, d), F32), tgt], axis=1).reshape(n, d)
    tables = _rope_tables(lp)
    zero_tail = jnp.zeros((1, HEAD_SLAB - D_QK), F32)
    gqh = jnp.concatenate([sh["q_head_norm"], zero_tail], axis=1)
    gkh = jnp.concatenate([sh["k_head_norm"], zero_tail], axis=1)
    wa = _block_diag(w["gate_a_w"][0]).astype(BF16)
    wx = _block_diag(w["gate_x_w"][0]).astype(BF16)

    (u1, a1, b1, s1), (g_wd1, g_in) = ffn_up(h0, sh["ffn1_norm"], wg1, wu1, "ffn1_up",
                                             comm=([b16("ffn1_w_down"), b16("w_in")], ["gather"] * 2))
    wd1 = g_wd1.reshape(D_FF, d)
    mla_rows = MLA_IN - D_ROPE
    w_in = g_in.reshape(mla_rows + 2 * LRU_WIDTH, d)
    wm = jnp.concatenate([w_in[:mla_rows], jnp.zeros((D_ROPE, d), BF16)], axis=0)
    wl = w_in[mla_rows:]
    (h1, u2, zm, zl), (g_uq, g_uk, g_uv, g_out) = ffn_down_inproj(
        h0, s1, wd1, sh["mix_norm"], wm, wl, "ffn1_down_inproj",
        comm=([b16("w_uq"), b16("w_uk"), b16("w_uv"), b16("w_out")], ["gather"] * 4))
    wuq = _slab_rows(g_uq.reshape(MLA_HEADS * D_QK, Q_RANK), D_QK)
    wuk = _slab_rows(g_uk.reshape(MLA_HEADS * D_NOPE, KV_RANK), D_NOPE)
    wuv = g_uv.reshape(MLA_HEADS * D_V, KV_RANK)
    w_out = g_out.reshape(d, d)

    q, k, vv, qn, cn = mla_prep_fwd(zm, sh["q_latent_norm"], sh["kv_latent_norm"], wuq, wuk, wuv, gqh, gkh, tables, lp)
    (y_mla, lse), (g_wu2, g_wd2) = attn_fwd(
        q, k, vv, nb, lp, comm=([b16("ffn2_w_up"), b16("ffn2_w_down")], ["gather"] * 2))
    (y_lru, hs), (g_wg2,) = lru_fwd(zl, conv_w, sh["conv_b"], wa, wx, sh["gate_a_b"], sh["gate_x_b"], sh["lru_lambda"],
                                    nb, lp, comm=([b16("ffn2_w_gate")], ["gather"]))
    wg2, wu2, wd2 = (g.reshape(D_FF, d) for g in (g_wg2, g_wu2, g_wd2))
    h2, yn = outproj_fwd(h1, y_mla, y_lru, sh["attn_out_norm"], sh["lru_out_norm"], w_out)
    dh3, u3, a3, b3, loss, g_final = ffn_fwd_loss(h2, sh["ffn2_norm"], wg2, wu2, wd2, sh["final_norm"], tgt_p, lp,
                                                  "ffn2_fwd_loss")

    vec = {"final_norm": g_final}
    (dh2, da3, db3, sh3, vec["ffn2_norm"]), _ = ffn_bwd_act(dh3, h2, sh["ffn2_norm"], a3, b3, wg2, wu2, wd2, "ffn2_bwd")
    ff_shards = (N_DEV, D_FF // N_DEV, d)
    dwg2 = tn_matmul(da3, u3, "ffn2_dwg", "bf16").reshape(ff_shards)
    dwu2 = tn_matmul(db3, u3, "ffn2_dwu", "bf16").reshape(ff_shards)
    dwd2 = tn_matmul(sh3, dh3, "ffn2_dwd", "bf16").reshape(ff_shards)

    dy_mla, dy_lru, vec["attn_out_norm"], vec["lru_out_norm"] = outproj_bwd(
        dh2, y_mla, y_lru, sh["attn_out_norm"], sh["lru_out_norm"], w_out)
    dw_out = tn_matmul(yn, dh2, "dw_out", "bf16").reshape(N_DEV, d // N_DEV, d)
    (du, dgate, dconv, vec["conv_b"], vec["gate_a_b"], vec["gate_x_b"], vec["lru_lambda"], dga, dgx), landed = lru_bwd(
        zl, hs, dy_lru, conv_w, sh["conv_b"], wa, wx, sh["gate_a_b"], sh["gate_x_b"], sh["lru_lambda"], nb, lp,
        comm=([dwg2, dw_out], ["scatter"] * 2))
    update("ffn2_w_gate", landed[0])
    update("w_out", landed[1])

    (dq, dk, dv), (r_wu2,) = attn_bwd(q, k, vv, y_mla, dy_mla, lse, nb, lp, comm=([dwu2], ["scatter"]))
    update("ffn2_w_up", r_wu2)

    (dzm, dqr, dkr, vec["q_latent_norm"], vec["kv_latent_norm"], vec["q_head_norm"], vec["k_head_norm"]), (r_wd2,) = (
        mla_prep_bwd(dq, dk, dv, zm, qn, cn, sh["q_latent_norm"], sh["kv_latent_norm"], wuq, wuk, wuv, gqh, gkh,
                     tables, lp, comm=([dwd2], ["scatter"])))
    update("ffn2_w_down", r_wd2)
    dwuq = _unslab_rows(tn_matmul(dqr, qn, "dw_uq", "bf16"), D_QK).reshape(N_DEV, -1, Q_RANK)
    dwuk = _unslab_rows(tn_matmul(dkr, cn, "dw_uk", "bf16"), D_NOPE).reshape(N_DEV, -1, KV_RANK)
    dwuv = tn_matmul(dv, cn, "dw_uv", "bf16").reshape(N_DEV, -1, KV_RANK)
    (dh1, vec["mix_norm"]), landed = inproj_bwd(dzm, du, dgate, dh2, h1, sh["mix_norm"], wm, wl,
                                                comm=([dwuq, dwuk, dwuv], ["scatter"] * 3))
    for name, r in zip(("w_uq", "w_uk", "w_uv"), landed):
        update(name, r)
    dw_in = jnp.concatenate([tn_matmul(dzm, u2, "dw_in_mla", "bf16")[:mla_rows], tn_matmul(du, u2, "dw_in_u", "bf16"),
                             tn_matmul(dgate, u2, "dw_in_gate", "bf16")], axis=0).reshape(N_DEV, -1, d)

    dwd1 = tn_matmul(s1, dh1, "ffn1_dwd", "bf16").reshape(ff_shards)
    (dh0, da1, db1, vec["ffn1_norm"]), landed = ffn_bwd_act(
        dh1, h0, sh["ffn1_norm"], a1, b1, wg1, wu1, wd1, "ffn1_bwd", emit_sh=False,
        comm=([dw_in, split_cols(dconv, "split_conv"), dwd1], ["scatter"] * 3))
    for name, r in zip(("w_in", "conv_w", "ffn1_w_down"), landed):
        update(name, r)

    dwg1 = tn_matmul(da1, u1, "ffn1_dwg", "bf16").reshape(ff_shards)
    dwu1, (r_wg1,) = tn_matmul(db1, u1, "ffn1_dwu", "bf16", comm=([dwg1], ["scatter"]))
    dmeta = meta_grad(dh0, nb, lp)
    gates = [dga.reshape(LRU_WIDTH, LRU_BLOCK), dgx.reshape(LRU_WIDTH, LRU_BLOCK)]
    r_vec, r_ga, r_gx, r_meta, r_wu1 = exchange(
        [pack_vectors(vec, loss)] + gates + [dmeta, dwu1.reshape(ff_shards)], ["gather"] * 3 + ["scatter"] * 2,
        "exchange_last")
    update("ffn1_w_gate", r_wg1)
    update("ffn1_w_up", r_wu1)
    update("meta_tokens", r_meta)

    small = [name for name, _ in VECTORS] + GATES
    res, total_loss = adamw_small(r_vec, [r_ga, r_gx], [sh[nm] for nm in small], [m2[nm] for nm in small],
                                  [v2[nm] for nm in small])
    for i, name in enumerate(small):
        out[name] = [res[j][i] for j in range(4)]

    grad_x = dh0.reshape(nb, lp, d)[:, FIRST_FRAME:]
    loss = total_loss[0, 0]

    def as_given(a, name):
        return (a.T if name in COLUMN_SHARDED else a).reshape(w[name].shape)

    cols = [[as_given(out[name][j], name) for name in _WEIGHT_NAMES] for j in range(4)]
    return (loss, grad_x, *cols[0], *cols[1], *cols[2], *cols[3])


def kernel(x, meta_tokens, ffn1_norm, ffn1_w_gate, ffn1_w_up, ffn1_w_down, mix_norm, w_in, q_latent_norm, w_uq, kv_latent_norm, w_uk, w_uv, q_head_norm, k_head_norm, conv_w, conv_b, gate_a_w, gate_a_b, gate_x_w, gate_x_b, lru_lambda, attn_out_norm, lru_out_norm, w_out, ffn2_norm, ffn2_w_gate, ffn2_w_up, ffn2_w_down, final_norm, loss_target, m_meta_tokens, m_ffn1_norm, m_ffn1_w_gate, m_ffn1_w_up, m_ffn1_w_down, m_mix_norm, m_w_in, m_q_latent_norm, m_w_uq, m_kv_latent_norm, m_w_uk, m_w_uv, m_q_head_norm, m_k_head_norm, m_conv_w, m_conv_b, m_gate_a_w, m_gate_a_b, m_gate_x_w, m_gate_x_b, m_lru_lambda, m_attn_out_norm, m_lru_out_norm, m_w_out, m_ffn2_norm, m_ffn2_w_gate, m_ffn2_w_up, m_ffn2_w_down, m_final_norm, v_meta_tokens, v_ffn1_norm, v_ffn1_w_gate, v_ffn1_w_up, v_ffn1_w_down, v_mix_norm, v_w_in, v_q_latent_norm, v_w_uq, v_kv_latent_norm, v_w_uk, v_w_uv, v_q_head_norm, v_k_head_norm, v_conv_w, v_conv_b, v_gate_a_w, v_gate_a_b, v_gate_x_w, v_gate_x_b, v_lru_lambda, v_attn_out_norm, v_lru_out_norm, v_w_out, v_ffn2_norm, v_ffn2_w_gate, v_ffn2_w_up, v_ffn2_w_down, v_final_norm):
    args = locals()
    w = {name: args[name] for name in _WEIGHT_NAMES}
    m = {name: args["m_" + name] for name in _WEIGHT_NAMES}
    v = {name: args["v_" + name] for name in _WEIGHT_NAMES}
    return train_step(x, loss_target, w, m, v)
```

```python
import math

import jax
import jax.numpy as jnp
from jax import lax
from jax.experimental import pallas as pl
from jax.experimental.pallas import tpu as pltpu

F32 = jnp.float32
BF16 = jnp.bfloat16

D_MODEL = 1024
CHUNK = 64
CHUNK_SHIFT = 6
N_META = 16
PAD = CHUNK - N_META
FIRST_FRAME = PAD + N_META
MLA_HEADS = 4
D_NOPE = 128
D_ROPE = 64
D_QK = D_NOPE + D_ROPE
D_V = 128
HEAD_SLAB = 256
KV_RANK = 256
Q_RANK = 384
ROPE_THETA = 10000.0
LRU_WIDTH = 512
LRU_BLOCKS = 8
LRU_BLOCK = 64
LRU_TILE = 128
CONV_W = 4
C_RGLRU = 8.0
D_FF = 2816
MLA_IN = 768
EPS = 1e-6
NEG_INF = -1e30
N_DEV = 8
LANES = 128
VMEM_LIMIT = 52 * 1024 * 1024
ATTN_HEADS_PER_STEP = 4
TN_ROWS = 4224
TN_X_BYTES = 12 * 1024 * 1024
TN_Y_BYTES = 9 * 1024 * 1024 // 2

ADAM_LR = 0.001
ADAM_B1 = 0.9
ADAM_B2 = 0.999
ADAM_EPS = 1e-08
ADAM_WD = 0.01
ADAM_STEP = 10

VMEM_WHOLE = pl.BlockSpec(memory_space=pltpu.VMEM)
HBM_WHOLE = pl.BlockSpec(memory_space=pl.ANY)


def _params(sems):
    if sems is None:
        return pltpu.CompilerParams(vmem_limit_bytes=VMEM_LIMIT)
    return pltpu.CompilerParams(dimension_semantics=sems, vmem_limit_bytes=VMEM_LIMIT)


def _tile(n, cap, mult=16):
    best = None
    for t in range(mult, min(n, cap) + 1, mult):
        if n % t == 0:
            best = t
    assert best is not None, (n, cap, mult)
    return best


def _row(tm, d):
    return pl.BlockSpec((tm, d), lambda i: (i, 0))


def _fixed(shape):
    return pl.BlockSpec(shape, lambda i: (0,) * len(shape))


def _mesh_position():
    return lax.axis_index("x"), lax.axis_index("y"), lax.axis_index("c")


def _flat_index(x, y, c):
    return 4 * x + 2 * y + c


def _peers(x, y, c):
    out = []
    for k in range(1, N_DEV):
        fx, fy, fc = (k >> 2) & 1, (k >> 1) & 1, k & 1
        out.append((1 - x if fx else x, 1 - y if fy else y, 1 - c if fc else c))
    return out


def _comm_out_shapes(srcs, modes):
    return [jax.ShapeDtypeStruct((N_DEV,) + s.shape if md == "gather" else s.shape, s.dtype)
            for s, md in zip(srcs, modes)]


def _comm_scratch(n):
    per_peer = n * (N_DEV - 1)
    return [pltpu.SemaphoreType.DMA((per_peer,)), pltpu.SemaphoreType.DMA((per_peer,)), pltpu.SemaphoreType.DMA((n,))]


class _Copies:
    def __init__(self, own, first, relay):
        self.own, self.first, self.relay = own, first, relay

    def start(self):
        for cp in self.own + self.first:
            cp.start()

    def forward(self):
        for arrival, onward in self.relay:
            arrival.wait_recv()
            onward.start()

    def finish(self):
        arrivals = [a for a, _ in self.relay]
        onward = [f for _, f in self.relay]
        for cp in self.first + onward:
            if not any(cp is a for a in arrivals):
                cp.wait_recv()
        for cp in self.first + onward:
            cp.wait_send()
        for cp in self.own:
            cp.wait()


def _comm_copies(src_refs, dst_refs, modes, send, recv, local):
    x, y, c = _mesh_position()
    me = _flat_index(x, y, c)
    n = len(modes)
    sibling = (x, y, 1 - c)
    chips = [(1 - x, y), (x, 1 - y), (1 - x, 1 - y)]

    def remote(src, dst, k, t, to):
        return pltpu.make_async_remote_copy(src_ref=src, dst_ref=dst, send_sem=send.at[k * n + t],
                                            recv_sem=recv.at[k * n + t], device_id=to,
                                            device_id_type=pl.DeviceIdType.MESH)

    own, first, relay = [], [], []
    for t, (src, dst, md) in enumerate(zip(src_refs, dst_refs, modes)):
        if md == "scatter":
            own.append(pltpu.make_async_copy(src.at[me], dst.at[me], local.at[t]))
            for k, peer in enumerate(_peers(x, y, c)):
                first.append(remote(src.at[_flat_index(*peer)], dst.at[me], k, t, peer))
        else:
            own.append(pltpu.make_async_copy(src, dst.at[me], local.at[t]))
            first.append(remote(src, dst.at[me], 0, t, sibling))
            for j, chip in enumerate(chips):
                arrival = remote(src, dst.at[me], 1 + j, t, (*chip, c))
                landed = dst.at[_flat_index(*chip, c)]
                first.append(arrival)
                relay.append((arrival, remote(landed, landed, 4 + j, t, sibling)))
    return _Copies(own, first, relay)


def _hosted(body, n_in, n_out, modes, grid):
    t = len(modes)
    total = math.prod(grid)

    def wrapped(*refs):
        ins, csrc = refs[:n_in], refs[n_in:n_in + t]
        outs = refs[n_in + t:n_in + t + n_out]
        cdst = refs[n_in + t + n_out:n_in + 2 * t + n_out]
        scratch = refs[n_in + 2 * t + n_out:-3]
        copies = _comm_copies(csrc, cdst, modes, *refs[-3:])
        step = pl.program_id(0)
        for axis in range(1, len(grid)):
            step = step * grid[axis] + pl.program_id(axis)

        @pl.when(step == 0)
        def _():
            copies.start()

        body(*ins, *outs, *scratch)

        @pl.when(step == (total * 4) // 5)
        def _():
            copies.forward()

        @pl.when(step == total - 1)
        def _():
            copies.finish()

    return wrapped


def _call(body, name, grid, in_specs, out_specs, out_shape, sems, args, scratch=(), comm=None):
    if comm is None:
        outs = pl.pallas_call(body, name=name, grid=grid, in_specs=in_specs, out_specs=out_specs, out_shape=out_shape,
                              scratch_shapes=list(scratch), compiler_params=_params(sems))(*args)
        return outs, []
    srcs, modes = comm
    n = len(modes)
    res = pl.pallas_call(
        _hosted(body, len(in_specs), len(out_specs), modes, grid), name=name, grid=grid,
        in_specs=list(in_specs) + [HBM_WHOLE] * n, out_specs=list(out_specs) + [HBM_WHOLE] * n,
        out_shape=list(out_shape) + _comm_out_shapes(srcs, modes),
        scratch_shapes=list(scratch) + _comm_scratch(n),
        compiler_params=_params(("arbitrary",) * len(grid)))(*args, *srcs)
    return res[:len(out_specs)], res[len(out_specs):]


def exchange(srcs, modes, name):
    n = len(modes)

    def body(*refs):
        copies = _comm_copies(refs[:n], refs[n:2 * n], modes, *refs[2 * n:])
        copies.start()
        copies.forward()
        copies.finish()

    return pl.pallas_call(body, name=name, in_specs=[HBM_WHOLE] * n, out_specs=[HBM_WHOLE] * n,
                          out_shape=_comm_out_shapes(srcs, modes), scratch_shapes=_comm_scratch(n))(*srcs)


def _nn(a, b):
    return jnp.dot(a, b, preferred_element_type=F32)


def _nt(a, b):
    return lax.dot_general(a, b, (((1,), (1,)), ((), ())), preferred_element_type=F32)


def _tn(a, b):
    return lax.dot_general(a, b, (((0,), (0,)), ((), ())), preferred_element_type=F32)


def _sig(x):
    return 1.0 / (1.0 + jnp.exp(-x))


def _rms_r(x, n=None):
    n = x.shape[-1] if n is None else n
    return lax.rsqrt(jnp.sum(x * x, axis=-1, keepdims=True) * (1.0 / n) + EPS)


def _rms_bwd(x, r, g, dy, n=None):
    n = x.shape[-1] if n is None else n
    xhat = x * r
    dxhat = dy * g
    dx = r * (dxhat - xhat * (jnp.sum(dxhat * xhat, axis=-1, keepdims=True) * (1.0 / n)))
    return dx, jnp.sum(dy * xhat, axis=0, keepdims=True)


def _accumulate(ref, val, first):
    @pl.when(first)
    def _():
        ref[...] = val

    @pl.when(jnp.logical_not(first))
    def _():
        ref[...] += val


_GELU_C = math.sqrt(2.0 / math.pi)


def _gelu_and_grad(x):
    inner = _GELU_C * (x + 0.044715 * x * x * x)
    t = jnp.tanh(inner)
    gelu = 0.5 * x * (1.0 + t)
    dgelu = 0.5 * (1.0 + t) + 0.5 * x * (1.0 - t * t) * _GELU_C * (1.0 + 3.0 * 0.044715 * x * x)
    return gelu, dgelu


def _log1p_small(t):
    return jnp.where(t < 1e-3, t * (1.0 - t * (0.5 - t * (1.0 / 3.0))), jnp.log(1.0 + t))


def _softplus(x):
    return jnp.maximum(x, 0.0) + _log1p_small(jnp.exp(-jnp.abs(x)))


def _sig_tanh(x):
    return 0.5 + 0.5 * jnp.tanh(0.5 * x)


def _ff_chunks(f):
    return 2 if (f // 2) % LANES == 0 else 1


def _swiglu_half(x, g_ref, wg_ref, wu_ref, wd_ref, a_ref, b_ref, fc):
    f = wg_ref.shape[0]
    u = (x * _rms_r(x) * g_ref[...]).astype(BF16)
    acc = jnp.zeros(x.shape, F32)
    for c in range(f // fc):
        cols = slice(c * fc, (c + 1) * fc)
        a = _nt(u, wg_ref[cols, :])
        b = _nt(u, wu_ref[cols, :])
        s = (a * _sig(a) * b).astype(BF16)
        acc = acc + _nn(s, wd_ref[cols, :])
        a_ref[:, cols] = a.astype(BF16)
        b_ref[:, cols] = b.astype(BF16)
    return x + 0.5 * acc, u


def ffn_up(h, g, wg, wu, name, comm=None):
    n, d = h.shape
    f = wg.shape[0]
    tm = _tile(n, 528)
    fc = 2 * LANES if f % (2 * LANES) == 0 else f

    def body(h_ref, g_ref, wg_ref, wu_ref, u_ref, a_ref, b_ref, s_ref):
        x = h_ref[...]
        u = (x * _rms_r(x) * g_ref[...]).astype(BF16)
        u_ref[...] = u
        for c in range(f // fc):
            cols = slice(c * fc, (c + 1) * fc)
            a = _nt(u, wg_ref[cols, :])
            b = _nt(u, wu_ref[cols, :])
            a_ref[:, cols] = a.astype(BF16)
            b_ref[:, cols] = b.astype(BF16)
            s_ref[:, cols] = (0.5 * (a * _sig(a) * b)).astype(BF16)

    wide = jax.ShapeDtypeStruct((n, f), BF16)
    return _call(
        body, name, (n // tm,),
        [_row(tm, d), _fixed((1, d)), VMEM_WHOLE, VMEM_WHOLE],
        [_row(tm, d), _row(tm, f), _row(tm, f), _row(tm, f)],
        [jax.ShapeDtypeStruct((n, d), BF16), wide, wide, wide],
        ("parallel",), (h, g, wg, wu), comm=comm)


def ffn_down_inproj(h, s, wd, g, wm, wl, name, comm=None):
    n, d = h.shape
    f = wd.shape[0]
    tm = _tile(n, 528)

    def body(h_ref, s_ref, wd_ref, g_ref, wm_ref, wl_ref, ho_ref, u_ref, zm_ref, zl_ref):
        x = h_ref[...] + _nn(s_ref[...], wd_ref[...])
        ho_ref[...] = x
        u = (x * _rms_r(x) * g_ref[...]).astype(BF16)
        u_ref[...] = u
        zm_ref[...] = _nt(u, wm_ref[...])
        zl_ref[...] = _nt(u, wl_ref[...])

    return _call(
        body, name, (n // tm,),
        [_row(tm, d), _row(tm, f), VMEM_WHOLE, _fixed((1, d)), VMEM_WHOLE, VMEM_WHOLE],
        [_row(tm, d), _row(tm, d), _row(tm, MLA_IN), _row(tm, 2 * LRU_WIDTH)],
        [jax.ShapeDtypeStruct((n, d), F32), jax.ShapeDtypeStruct((n, d), BF16),
         jax.ShapeDtypeStruct((n, MLA_IN), F32), jax.ShapeDtypeStruct((n, 2 * LRU_WIDTH), F32)],
        ("parallel",), (h, s, wd, g, wm, wl), comm=comm)


def ffn_fwd_loss(h, g, wg, wu, wd, g_final, tgt, lp, name):
    n, d = h.shape
    f = wg.shape[0]
    tm = _tile(lp, 528)
    per_seq = lp // tm
    fc = 2 * LANES if f % (2 * LANES) == 0 else f

    def body(h_ref, g_ref, wg_ref, wu_ref, wd_ref, gf_ref, t_ref, dh_ref, u_ref, a_ref, b_ref, loss_ref, dgf_ref):
        i = pl.program_id(0)
        y, u_ref[...] = _swiglu_half(h_ref[...], g_ref, wg_ref, wu_ref, wd_ref, a_ref, b_ref, fc)
        dh_ref[...], part, dg = _loss_and_grad(y, gf_ref[...], t_ref[...], (i % per_seq) * tm)
        _accumulate(loss_ref, jnp.broadcast_to(part, (1, LANES)), i == 0)
        _accumulate(dgf_ref, dg, i == 0)

    outs, _ = _call(
        body, name, (n // tm,),
        [_row(tm, d), _fixed((1, d)), VMEM_WHOLE, VMEM_WHOLE, VMEM_WHOLE, _fixed((1, d)), _row(tm, d)],
        [_row(tm, d), _row(tm, d), _row(tm, f), _row(tm, f), _fixed((1, LANES)), _fixed((1, d))],
        [jax.ShapeDtypeStruct((n, d), F32), jax.ShapeDtypeStruct((n, d), BF16),
         jax.ShapeDtypeStruct((n, f), BF16), jax.ShapeDtypeStruct((n, f), BF16),
         jax.ShapeDtypeStruct((1, LANES), F32), jax.ShapeDtypeStruct((1, d), F32)],
        ("arbitrary",), (h, g, wg, wu, wd, g_final, tgt))
    return outs


def ffn_bwd_act(dh, h, g, a, b, wg, wu, wd, name, comm=None, emit_sh=True):
    n, d = h.shape
    f = wg.shape[0]
    tm = _tile(n, 352 if emit_sh else 384)
    nc = _ff_chunks(f)
    fc = f // nc

    def body(dh_ref, h_ref, g_ref, a_ref, b_ref, wg_ref, wu_ref, wd_ref, dhi_ref, da_ref, db_ref, *rest):
        dg_ref = rest[-1]
        x = h_ref[...]
        dy = dh_ref[...]
        r = _rms_r(x)
        dhh = (0.5 * dy).astype(BF16)
        du = jnp.zeros((tm, d), F32)
        for c in range(nc):
            cols = slice(c * fc, (c + 1) * fc)
            ds = _nt(dhh, wd_ref[cols, :])
            av = a_ref[:, cols].astype(F32)
            bv = b_ref[:, cols].astype(F32)
            sg = _sig(av)
            sil = av * sg
            da = (ds * bv * (sg * (1.0 + av * (1.0 - sg)))).astype(BF16)
            db = (ds * sil).astype(BF16)
            da_ref[:, cols] = da
            db_ref[:, cols] = db
            if emit_sh:
                rest[0][:, cols] = (0.5 * sil * bv).astype(BF16)
            du = du + _nn(da, wg_ref[cols, :]) + _nn(db, wu_ref[cols, :])
        dx, dg = _rms_bwd(x, r, g_ref[...], du)
        dhi_ref[...] = dy + dx
        _accumulate(dg_ref, dg, pl.program_id(0) == 0)

    wide = [jax.ShapeDtypeStruct((n, f), BF16)] * (3 if emit_sh else 2)
    return _call(
        body, name, (n // tm,),
        [_row(tm, d), _row(tm, d), _fixed((1, d)), _row(tm, f), _row(tm, f), VMEM_WHOLE, VMEM_WHOLE, VMEM_WHOLE],
        [_row(tm, d)] + [_row(tm, f)] * len(wide) + [_fixed((1, d))],
        [jax.ShapeDtypeStruct((n, d), F32)] + wide + [jax.ShapeDtypeStruct((1, d), F32)],
        ("arbitrary",), (dh, h, g, a, b, wg, wu, wd), comm=comm)


def tn_matmul(x, y, name, out="f32", comm=None):
    n, k = x.shape
    m = y.shape[1]
    tm = _tile(n, TN_ROWS)
    kc, mc = k, (512 if m % 512 == 0 else m)
    while tm * kc * x.dtype.itemsize > TN_X_BYTES and kc % (2 * LANES) == 0:
        kc //= 2
    while tm * mc * y.dtype.itemsize > TN_Y_BYTES and mc % (2 * LANES) == 0:
        mc //= 2
    steps = n // tm

    def body(x_ref, y_ref, o_ref, *acc):
        i = pl.program_id(2)
        part = _tn(x_ref[...].astype(BF16), y_ref[...].astype(BF16))
        if steps == 1:
            o_ref[...] = part.astype(o_ref.dtype)
        elif out == "f32":
            _accumulate(o_ref, part, i == 0)
        else:
            _accumulate(acc[0], part, i == 0)

            @pl.when(i == steps - 1)
            def _():
                o_ref[...] = acc[0][...].astype(BF16)

    out_shape = jax.ShapeDtypeStruct((k, m), F32 if out == "f32" else BF16)
    (res,), landed = _call(
        body, name, (k // kc, m // mc, steps),
        [pl.BlockSpec((tm, kc), lambda a, b, i: (i, a)), pl.BlockSpec((tm, mc), lambda a, b, i: (i, b))],
        [pl.BlockSpec((kc, mc), lambda a, b, i: (a, b))], [out_shape], ("parallel", "parallel", "arbitrary"), (x, y),
        scratch=[pltpu.VMEM((kc, mc), F32)] if (out == "bf16" and steps > 1) else [], comm=comm)
    return (res, landed) if comm is not None else res


def inproj_bwd(dzm, du, dgate, dh2, h, g, wm, wl, comm=None):
    n, d = h.shape
    tm = _tile(n, 352)

    def body(dzm_ref, du_ref, dgt_ref, dh2_ref, h_ref, g_ref, wm_ref, wl_ref, dh_ref, dg_ref):
        x = h_ref[...]
        dun = (_nn(dzm_ref[...].astype(BF16), wm_ref[...])
               + _nn(du_ref[...].astype(BF16), wl_ref[:LRU_WIDTH, :])
               + _nn(dgt_ref[...].astype(BF16), wl_ref[LRU_WIDTH:, :]))
        dx, dg = _rms_bwd(x, _rms_r(x), g_ref[...], dun)
        dh_ref[...] = dh2_ref[...] + dx
        _accumulate(dg_ref, dg, pl.program_id(0) == 0)

    return _call(
        body, "inproj_bwd", (n // tm,),
        [_row(tm, MLA_IN), _row(tm, LRU_WIDTH), _row(tm, LRU_WIDTH), _row(tm, d), _row(tm, d),
         _fixed((1, d)), VMEM_WHOLE, VMEM_WHOLE],
        [_row(tm, d), _fixed((1, d))],
        [jax.ShapeDtypeStruct((n, d), F32), jax.ShapeDtypeStruct((1, d), F32)],
        ("arbitrary",), (dzm, du, dgate, dh2, h, g, wm, wl), comm=comm)


def _rope_tables(lp):
    pos = jnp.arange(lp, dtype=F32) - float(PAD)
    half = D_ROPE // 2
    inv_freq = ROPE_THETA ** (-jnp.arange(0, half, dtype=F32) / half)
    ang = pos[:, None] * inv_freq[None, :]
    cos, sin = jnp.cos(ang), jnp.sin(ang)
    one = jnp.ones((lp, D_NOPE), F32)
    z_nope = jnp.zeros((lp, D_NOPE), F32)
    z_half = jnp.zeros((lp, half), F32)
    z_tail = jnp.zeros((lp, HEAD_SLAB - D_QK), F32)
    cosr = jnp.concatenate([one, cos, cos, z_tail], axis=1)
    sin_up = jnp.concatenate([z_nope, z_half, sin, z_tail], axis=1)
    sin_dn = jnp.concatenate([z_nope, -sin, z_half, z_tail], axis=1)
    return cosr, sin_up, sin_dn


def _rope(x, cosr, sin_up, sin_dn):
    half = D_ROPE // 2
    return x * cosr + pltpu.roll(x, half, axis=1) * sin_up + pltpu.roll(x, HEAD_SLAB - half, axis=1) * sin_dn


def _rope_bwd(dy, cosr, sin_up, sin_dn):
    half = D_ROPE // 2
    return (dy * cosr + pltpu.roll(dy * sin_up, HEAD_SLAB - half, axis=1)
            + pltpu.roll(dy * sin_dn, half, axis=1))


def _k_rope_slab(zm_tile):
    tm = zm_tile.shape[0]
    krp = zm_tile[:, Q_RANK + KV_RANK:MLA_IN]
    return jnp.concatenate([jnp.zeros((tm, D_NOPE), F32), krp], axis=1)


def mla_prep_fwd(zm, gql, gkvl, wuq, wuk, wuv, gqh, gkh, tables, lp):
    n = zm.shape[0]
    tm = _tile(lp, 352)
    per_seq = lp // tm
    width = MLA_HEADS * HEAD_SLAB
    scale = 1.0 / math.sqrt(D_QK)

    def body(zm_ref, gql_ref, gkvl_ref, wuq_ref, wuk_ref, wuv_ref, gqh_ref, gkh_ref,
             cos_ref, up_ref, dn_ref, q_ref, k_ref, v_ref, qn_ref, cn_ref):
        z = zm_ref[...]
        cq = z[:, :Q_RANK]
        ckv = z[:, Q_RANK:Q_RANK + KV_RANK]
        qn = (cq * _rms_r(cq) * gql_ref[...]).astype(BF16)
        cn = (ckv * _rms_r(ckv) * gkvl_ref[...]).astype(BF16)
        qn_ref[...] = qn
        cn_ref[...] = cn
        q_raw = _nt(qn, wuq_ref[...])
        k_raw = _nt(cn, wuk_ref[...])
        v_ref[...] = _nt(cn, wuv_ref[...]).astype(BF16)
        kr_slab = _k_rope_slab(z)
        cosr, sin_up, sin_dn = cos_ref[...], up_ref[...], dn_ref[...]
        for hd in range(MLA_HEADS):
            cols = slice(hd * HEAD_SLAB, (hd + 1) * HEAD_SLAB)
            xq = q_raw[:, cols]
            yq = _rope(xq * _rms_r(xq, D_QK) * gqh_ref[...], cosr, sin_up, sin_dn)
            q_ref[:, cols] = (yq * scale).astype(BF16)
            xk = k_raw[:, cols] + kr_slab
            yk = _rope(xk * _rms_r(xk, D_QK) * gkh_ref[...], cosr, sin_up, sin_dn)
            k_ref[:, cols] = yk.astype(BF16)

    tab = pl.BlockSpec((tm, HEAD_SLAB), lambda i: (i % per_seq, 0))
    return pl.pallas_call(
        body, name="mla_prep_fwd", grid=(n // tm,),
        in_specs=[_row(tm, MLA_IN), _fixed((1, Q_RANK)), _fixed((1, KV_RANK)), VMEM_WHOLE, VMEM_WHOLE, VMEM_WHOLE,
                  _fixed((1, HEAD_SLAB)), _fixed((1, HEAD_SLAB)), tab, tab, tab],
        out_specs=[_row(tm, width), _row(tm, width), _row(tm, MLA_HEADS * D_V), _row(tm, Q_RANK), _row(tm, KV_RANK)],
        out_shape=[jax.ShapeDtypeStruct((n, width), BF16), jax.ShapeDtypeStruct((n, width), BF16),
                   jax.ShapeDtypeStruct((n, MLA_HEADS * D_V), BF16), jax.ShapeDtypeStruct((n, Q_RANK), BF16),
                   jax.ShapeDtypeStruct((n, KV_RANK), BF16)],
        compiler_params=_params(("parallel",)),
    )(zm, gql, gkvl, wuq, wuk, wuv, gqh, gkh, *tables)


def mla_prep_bwd(dq, dk, dv, zm, qn, cn, gql, gkvl, wuq, wuk, wuv, gqh, gkh, tables, lp, comm=None):
    n = zm.shape[0]
    tm = _tile(lp, 704)
    per_seq = lp // tm
    width = MLA_HEADS * HEAD_SLAB
    scale = 1.0 / math.sqrt(D_QK)

    def body(dq_ref, dk_ref, dv_ref, zm_ref, qn_ref, cn_ref, gql_ref, gkvl_ref, wuq_ref, wuk_ref, wuv_ref,
             gqh_ref, gkh_ref, cos_ref, up_ref, dn_ref,
             dzm_ref, dqr_ref, dkr_ref, dgql_ref, dgkvl_ref, dgqh_ref, dgkh_ref):
        z = zm_ref[...]
        cq = z[:, :Q_RANK]
        ckv = z[:, Q_RANK:Q_RANK + KV_RANK]
        q_raw = _nt(qn_ref[...], wuq_ref[...])
        k_raw = _nt(cn_ref[...], wuk_ref[...])
        kr_slab = _k_rope_slab(z)
        cosr, sin_up, sin_dn = cos_ref[...], up_ref[...], dn_ref[...]
        dgq = jnp.zeros((1, HEAD_SLAB), F32)
        dgk = jnp.zeros((1, HEAD_SLAB), F32)
        dkrp = jnp.zeros((tm, HEAD_SLAB - D_NOPE), F32)
        for hd in range(MLA_HEADS):
            cols = slice(hd * HEAD_SLAB, (hd + 1) * HEAD_SLAB)
            xq = q_raw[:, cols]
            dxn = _rope_bwd(dq_ref[:, cols] * scale, cosr, sin_up, sin_dn)
            dxq, dg = _rms_bwd(xq, _rms_r(xq, D_QK), gqh_ref[...], dxn, D_QK)
            dgq = dgq + dg
            dqr_ref[:, cols] = dxq.astype(BF16)
            xk = k_raw[:, cols] + kr_slab
            dxn = _rope_bwd(dk_ref[:, cols], cosr, sin_up, sin_dn)
            dxk, dg = _rms_bwd(xk, _rms_r(xk, D_QK), gkh_ref[...], dxn, D_QK)
            dgk = dgk + dg
            dkr_ref[:, cols] = dxk.astype(BF16)
            dkrp = dkrp + dxk[:, D_NOPE:]
        dqn = _nn(dqr_ref[...], wuq_ref[...])
        dcn = _nn(dkr_ref[...], wuk_ref[...]) + _nn(dv_ref[...].astype(BF16), wuv_ref[...])
        dcq, dg1 = _rms_bwd(cq, _rms_r(cq), gql_ref[...], dqn)
        dckv, dg2 = _rms_bwd(ckv, _rms_r(ckv), gkvl_ref[...], dcn)
        dzm_ref[:, :Q_RANK] = dcq.astype(BF16)
        dzm_ref[:, Q_RANK:Q_RANK + KV_RANK] = dckv.astype(BF16)
        dzm_ref[:, Q_RANK + KV_RANK:] = dkrp.astype(BF16)
        first = pl.program_id(0) == 0
        _accumulate(dgql_ref, dg1, first)
        _accumulate(dgkvl_ref, dg2, first)
        _accumulate(dgqh_ref, dgq, first)
        _accumulate(dgkh_ref, dgk, first)

    tab = pl.BlockSpec((tm, HEAD_SLAB), lambda i: (i % per_seq, 0))
    return _call(
        body, "mla_prep_bwd", (n // tm,),
        [_row(tm, width), _row(tm, width), _row(tm, MLA_HEADS * D_V), _row(tm, MLA_IN),
         _row(tm, Q_RANK), _row(tm, KV_RANK), _fixed((1, Q_RANK)), _fixed((1, KV_RANK)),
         VMEM_WHOLE, VMEM_WHOLE, VMEM_WHOLE, _fixed((1, HEAD_SLAB)), _fixed((1, HEAD_SLAB)), tab, tab, tab],
        [_row(tm, MLA_IN), _row(tm, width), _row(tm, width), _fixed((1, Q_RANK)), _fixed((1, KV_RANK)),
         _fixed((1, HEAD_SLAB)), _fixed((1, HEAD_SLAB))],
        [jax.ShapeDtypeStruct((n, MLA_IN), BF16), jax.ShapeDtypeStruct((n, width), BF16),
         jax.ShapeDtypeStruct((n, width), BF16), jax.ShapeDtypeStruct((1, Q_RANK), F32),
         jax.ShapeDtypeStruct((1, KV_RANK), F32), jax.ShapeDtypeStruct((1, HEAD_SLAB), F32),
         jax.ShapeDtypeStruct((1, HEAD_SLAB), F32)],
        ("arbitrary",), (dq, dk, dv, zm, qn, cn, gql, gkvl, wuq, wuk, wuv, gqh, gkh, *tables), comm=comm)


def _attn_tile(lp):
    return _tile(lp, 704, CHUNK)


def _chunk_mask(i, j, t):
    qpos = i * t + lax.broadcasted_iota(jnp.int32, (t, t), 0)
    kpos = j * t + lax.broadcasted_iota(jnp.int32, (t, t), 1)
    same_or_earlier = jnp.right_shift(kpos, CHUNK_SHIFT) <= jnp.right_shift(qpos, CHUNK_SHIFT)
    return jnp.logical_and(same_or_earlier, kpos >= PAD)


def _masked_scores(s, i, j, t, diagonal):
    if diagonal:
        return jnp.where(_chunk_mask(i, j, t), s, NEG_INF)
    kpos = j * t + lax.broadcasted_iota(jnp.int32, (1, t), 1)
    return s + jnp.where(kpos < PAD, NEG_INF, 0.0)


def attn_fwd(q, k, v, nb, lp, comm=None):
    n = q.shape[0]
    t = _attn_tile(lp)
    nq = lp // t

    hp = ATTN_HEADS_PER_STEP

    def body(q_ref, k_ref, v_ref, o_ref, lse_ref):
        i = pl.program_id(2)
        qs = [q_ref[:, hh * HEAD_SLAB:(hh + 1) * HEAD_SLAB] for hh in range(hp)]

        def kv_step(j, carry, diagonal=False):
            off = pl.multiple_of(j * t, t)
            out = []
            for hh in range(hp):
                m, l, acc = carry[hh]
                kv = k_ref[pl.ds(off, t), hh * HEAD_SLAB:(hh + 1) * HEAD_SLAB]
                s = _masked_scores(_nt(qs[hh], kv), i, j, t, diagonal)
                m_new = jnp.maximum(m, jnp.max(s, axis=-1, keepdims=True))
                p = jnp.exp(s - m_new)
                alpha = jnp.exp(m - m_new)
                l = alpha * l + jnp.sum(p, axis=-1, keepdims=True)
                acc = alpha * acc + _nn(p.astype(BF16), v_ref[pl.ds(off, t), hh * D_V:(hh + 1) * D_V])
                out.append((m_new, l, acc))
            return tuple(out)

        init = tuple((jnp.full((t, 1), NEG_INF, F32), jnp.zeros((t, 1), F32), jnp.zeros((t, D_V), F32))
                     for _ in range(hp))
        done = kv_step(i, lax.fori_loop(0, i, kv_step, init), diagonal=True)
        for hh, (m, l, acc) in enumerate(done):
            o_ref[:, hh * D_V:(hh + 1) * D_V] = acc * (1.0 / l)
            lse_ref[hh] = jnp.broadcast_to(m + jnp.log(l), (t, LANES))

    return _call(
        body, "attn_fwd", (nb, MLA_HEADS // hp, nq),
        [pl.BlockSpec((t, hp * HEAD_SLAB), lambda b, h, i: (b * nq + i, h)),
         pl.BlockSpec((lp, hp * HEAD_SLAB), lambda b, h, i: (b, h)),
         pl.BlockSpec((lp, hp * D_V), lambda b, h, i: (b, h))],
        [pl.BlockSpec((t, hp * D_V), lambda b, h, i: (b * nq + i, h)),
         pl.BlockSpec((hp, t, LANES), lambda b, h, i: (h, b * nq + i, 0))],
        [jax.ShapeDtypeStruct((n, MLA_HEADS * D_V), F32), jax.ShapeDtypeStruct((MLA_HEADS, n, LANES), F32)],
        ("parallel", "parallel", "parallel"), (q, k, v), comm=comm)


def attn_bwd(q, k, v, o, do, lse, nb, lp, comm=None):
    n = q.shape[0]
    t = _attn_tile(lp)
    nq = lp // t

    def body(q_ref, k_ref, v_ref, o_ref, do_ref, lse_ref, dq_ref, dk_ref, dv_ref):
        dk_ref[...] = jnp.zeros_like(dk_ref)
        dv_ref[...] = jnp.zeros_like(dv_ref)

        def q_step(i, _):
            qoff = pl.multiple_of(i * t, t)
            qv = q_ref[pl.ds(qoff, t), :]
            dov = do_ref[pl.ds(qoff, t), :]
            delta = jnp.sum(o_ref[pl.ds(qoff, t), :] * dov, axis=-1, keepdims=True)
            lse_q = jnp.max(lse_ref[0, pl.ds(qoff, t), :], axis=-1, keepdims=True)
            do16 = dov.astype(BF16)

            def kv_step(j, dq_acc, diagonal=False):
                koff = pl.multiple_of(j * t, t)
                kv = k_ref[pl.ds(koff, t), :]
                s = _masked_scores(_nt(qv, kv), i, j, t, diagonal)
                p = jnp.exp(s - lse_q)
                dp = _nt(do16, v_ref[pl.ds(koff, t), :])
                ds16 = (p * (dp - delta)).astype(BF16)
                dv_ref[pl.ds(koff, t), :] += _tn(p.astype(BF16), do16)
                dk_ref[pl.ds(koff, t), :] += _tn(ds16, qv)
                return dq_acc + _nn(ds16, kv)

            earlier = lax.fori_loop(0, i, kv_step, jnp.zeros((t, HEAD_SLAB), F32))
            dq_ref[pl.ds(qoff, t), :] = kv_step(i, earlier, diagonal=True)
            return 0

        lax.fori_loop(0, nq, q_step, 0)

    wide = pl.BlockSpec((lp, HEAD_SLAB), lambda b, h: (b, h))
    thin = pl.BlockSpec((lp, D_V), lambda b, h: (b, h))
    width = MLA_HEADS * HEAD_SLAB
    return _call(
        body, "attn_bwd", (nb, MLA_HEADS),
        [wide, wide, thin, thin, thin, pl.BlockSpec((1, lp, LANES), lambda b, h: (h, b, 0))],
        [wide, wide, thin],
        [jax.ShapeDtypeStruct((n, width), F32), jax.ShapeDtypeStruct((n, width), F32),
         jax.ShapeDtypeStruct((n, MLA_HEADS * D_V), F32)],
        ("parallel", "parallel"), (q, k, v, o, do, lse), comm=comm)


def _seq_rows(nb, lp, width):
    rows = lax.broadcasted_iota(jnp.int32, (lp, width), 0)
    return jnp.concatenate([rows] * nb, axis=0) if nb > 1 else rows


def _lru_gates(u, w_ref, cb, wa, wx, ba, bx, lam):
    xc = (cb + w_ref[pl.ds(3, 1), :] * u + w_ref[pl.ds(2, 1), :] * pltpu.roll(u, 1, axis=0)
          + w_ref[pl.ds(1, 1), :] * pltpu.roll(u, 2, axis=0) + w_ref[pl.ds(0, 1), :] * pltpu.roll(u, 3, axis=0))
    xc16 = xc.astype(BF16)
    ra = _sig_tanh(_nn(xc16, wa) + ba)
    ia = _sig_tanh(_nn(xc16, wx) + bx)
    sp = _softplus(-lam)
    log_a = -C_RGLRU * ra * sp
    a = jnp.exp(log_a)
    x2 = 2.0 * log_a
    mult = jnp.sqrt(jnp.where(x2 > -1e-2, -x2 * (1.0 + x2 * (0.5 + x2 * (1.0 / 6.0))), 1.0 - a * a))
    return xc, xc16, ra, ia, sp, a, mult


def _scan_block_rows(width):
    return lax.broadcasted_iota(jnp.int32, (8, width), 0)


def lru_fwd(zl, conv_w, conv_b, wa, wx, ba, bx, lam, nb, lp, comm=None):
    n = zl.shape[0]
    w = LRU_TILE
    nt = LRU_WIDTH // w
    nblk = lp // 8

    def body(u_ref, gt_ref, cw_ref, cb_ref, wa_ref, wx_ref, ba_ref, bx_ref, lam_ref, y_ref, h_ref, a_s, b_s):
        u = u_ref[...]
        xc, _, _, ia, _, a, mult = _lru_gates(u, cw_ref, cb_ref[...], wa_ref[...], wx_ref[...],
                                              ba_ref[...], bx_ref[...], lam_ref[...])
        row = _seq_rows(nb, lp, w)
        mult = jnp.where(row == PAD, 1.0, mult)
        a_s[...] = a
        b_s[...] = jnp.where(row < PAD, 0.0, mult * (ia * xc))
        r8 = _scan_block_rows(w)

        def blk(i, carry):
            out = []
            for s_id in range(nb):
                off = pl.multiple_of(s_id * lp + i * 8, 8)
                av = a_s[pl.ds(off, 8), :]
                bv = b_s[pl.ds(off, 8), :]
                for sh in (1, 2, 4):
                    keep = r8 >= sh
                    bv = jnp.where(keep, av * pltpu.roll(bv, sh, axis=0) + bv, bv)
                    av = jnp.where(keep, av * pltpu.roll(av, sh, axis=0), av)
                hv = bv + av * carry[s_id]
                h_ref[pl.ds(off, 8), :] = hv
                out.append(jnp.sum(jnp.where(r8 == 7, hv, 0.0), axis=0, keepdims=True))
            return tuple(out)

        lax.fori_loop(0, nblk, blk, tuple(jnp.zeros((1, w), F32) for _ in range(nb)))
        gelu, _ = _gelu_and_grad(gt_ref[...])
        y_ref[...] = h_ref[...] * gelu

    col = lambda c: (0, c)
    return _call(
        body, "lru_fwd", (nt,),
        [pl.BlockSpec((n, w), col), pl.BlockSpec((n, w), lambda c: (0, nt + c)),
         pl.BlockSpec((CONV_W, w), col), pl.BlockSpec((1, w), col),
         pl.BlockSpec((w, w), lambda c: (c, c)), pl.BlockSpec((w, w), lambda c: (c, c)),
         pl.BlockSpec((1, w), col), pl.BlockSpec((1, w), col), pl.BlockSpec((1, w), col)],
        [pl.BlockSpec((n, w), col), pl.BlockSpec((n, w), col)],
        [jax.ShapeDtypeStruct((n, LRU_WIDTH), F32), jax.ShapeDtypeStruct((n, LRU_WIDTH), F32)],
        ("parallel",), (zl, zl, conv_w, conv_b, wa, wx, ba, bx, lam),
        scratch=[pltpu.VMEM((n, w), F32), pltpu.VMEM((n, w), F32)], comm=comm)


def lru_bwd(zl, hs, dy, conv_w, conv_b, wa, wx, ba, bx, lam, nb, lp, comm=None):
    n = zl.shape[0]
    w = LRU_TILE
    nt = LRU_WIDTH // w
    nblk = lp // 8

    def body(u_ref, gt_ref, h_ref, dy_ref, cw_ref, cb_ref, wa_ref, wx_ref, ba_ref, bx_ref, lam_ref,
             du_ref, dgt_ref, dcw_ref, dcb_ref, dba_ref, dbx_ref, dlam_ref, dwa_ref, dwx_ref,
             c_s, d_s, g_s, dwa_s, dwx_s):
        u = u_ref[...]
        lam = lam_ref[...]
        xc, xc16, ra, ia, sp, a, mult = _lru_gates(u, cw_ref, cb_ref[...], wa_ref[...], wx_ref[...],
                                                   ba_ref[...], bx_ref[...], lam)
        row = lax.broadcasted_iota(jnp.int32, (lp, w), 0)
        hv = h_ref[...]
        dyv = dy_ref[...]
        gelu, dgelu = _gelu_and_grad(gt_ref[...])
        dgt_ref[...] = jnp.where(row >= PAD, dyv * hv * dgelu, 0.0).astype(BF16)
        c_s[...] = pltpu.roll(a, lp - 1, axis=0)
        d_s[...] = dyv * gelu
        r8 = _scan_block_rows(w)

        def blk(ii, carry):
            off = pl.multiple_of((nblk - 1 - ii) * 8, 8)
            cv = c_s[pl.ds(off, 8), :]
            dv = d_s[pl.ds(off, 8), :]
            for sh in (1, 2, 4):
                keep = r8 < 8 - sh
                dv = jnp.where(keep, cv * pltpu.roll(dv, 8 - sh, axis=0) + dv, dv)
                cv = jnp.where(keep, cv * pltpu.roll(cv, 8 - sh, axis=0), cv)
            gv = dv + cv * carry
            g_s[pl.ds(off, 8), :] = gv
            return jnp.sum(jnp.where(r8 == 0, gv, 0.0), axis=0, keepdims=True)

        lax.fori_loop(0, nblk, blk, jnp.zeros((1, w), F32))
        gv = g_s[...]
        first_row = row == PAD
        db = jnp.where(row >= PAD, gv, 0.0)
        da = jnp.where(row > PAD, gv * pltpu.roll(hv, 1, axis=0), 0.0)
        mult_eff = jnp.where(first_row, 1.0, mult)
        dmult = jnp.where(first_row, 0.0, db * (ia * xc))
        dia = db * mult_eff * xc
        dxc = db * mult_eff * ia
        dla = da * a - dmult * (a * a) / mult
        dra = dla * (-C_RGLRU * sp)
        dsp = jnp.sum(dla * (-C_RGLRU * ra), axis=0, keepdims=True)
        dpa = dra * ra * (1.0 - ra)
        dpx = dia * ia * (1.0 - ia)
        dpa16 = dpa.astype(BF16)
        dpx16 = dpx.astype(BF16)
        dxc = dxc + _nt(dpa16, wa_ref[...]) + _nt(dpx16, wx_ref[...])
        du = cw_ref[pl.ds(CONV_W - 1, 1), :] * dxc
        dcw = [jnp.sum(dxc * u, axis=0, keepdims=True)]
        for tap in range(1, CONV_W):
            dcw.insert(0, jnp.sum(dxc * pltpu.roll(u, tap, axis=0), axis=0, keepdims=True))
            du = du + cw_ref[pl.ds(CONV_W - 1 - tap, 1), :] * pltpu.roll(dxc, lp - tap, axis=0)
        du_ref[...] = jnp.where(row >= PAD, du, 0.0).astype(BF16)
        first = pl.program_id(1) == 0
        _accumulate(dlam_ref, -_sig(-lam) * dsp, first)
        _accumulate(dba_ref, jnp.sum(dpa, axis=0, keepdims=True), first)
        _accumulate(dbx_ref, jnp.sum(dpx, axis=0, keepdims=True), first)
        _accumulate(dcb_ref, jnp.sum(dxc, axis=0, keepdims=True), first)
        _accumulate(dcw_ref, jnp.concatenate(dcw, axis=0), first)
        _accumulate(dwa_s, _tn(xc16, dpa16), first)
        _accumulate(dwx_s, _tn(xc16, dpx16), first)

        @pl.when(pl.program_id(1) == nb - 1)
        def _():
            for j in range(w // LRU_BLOCK):
                blk_rows = slice(j * LRU_BLOCK, (j + 1) * LRU_BLOCK)
                dwa_ref[0, blk_rows, :] = dwa_s[blk_rows, blk_rows]
                dwx_ref[0, blk_rows, :] = dwx_s[blk_rows, blk_rows]

    col = lambda c, b: (0, c)
    vec = pl.BlockSpec((1, w), col)
    mat = pl.BlockSpec((w, w), lambda c, b: (c, c))
    big = pl.BlockSpec((lp, w), lambda c, b: (b, c))
    dmat = pl.BlockSpec((1, w, LRU_BLOCK), lambda c, b: (c, 0, 0))
    return _call(
        body, "lru_bwd", (nt, nb),
        [big, pl.BlockSpec((lp, w), lambda c, b: (b, nt + c)), big, big,
         pl.BlockSpec((CONV_W, w), col), vec, mat, mat, vec, vec, vec],
        [big, big, pl.BlockSpec((CONV_W, w), col), vec, vec, vec, vec, dmat, dmat],
        [jax.ShapeDtypeStruct((n, LRU_WIDTH), BF16), jax.ShapeDtypeStruct((n, LRU_WIDTH), BF16),
         jax.ShapeDtypeStruct((CONV_W, LRU_WIDTH), F32), jax.ShapeDtypeStruct((1, LRU_WIDTH), F32),
         jax.ShapeDtypeStruct((1, LRU_WIDTH), F32), jax.ShapeDtypeStruct((1, LRU_WIDTH), F32),
         jax.ShapeDtypeStruct((1, LRU_WIDTH), F32), jax.ShapeDtypeStruct((nt, w, LRU_BLOCK), F32),
         jax.ShapeDtypeStruct((nt, w, LRU_BLOCK), F32)],
        ("parallel", "arbitrary"), (zl, zl, hs, dy, conv_w, conv_b, wa, wx, ba, bx, lam),
        scratch=[pltpu.VMEM((lp, w), F32), pltpu.VMEM((lp, w), F32), pltpu.VMEM((lp, w), F32),
                 pltpu.VMEM((w, w), F32), pltpu.VMEM((w, w), F32)], comm=comm)


def outproj_fwd(h, ya, yl, gao, glo, wout):
    n, d = h.shape
    half = ya.shape[1]
    tm = _tile(n, 704)

    def body(h_ref, ya_ref, yl_ref, gao_ref, glo_ref, w_ref, ho_ref, yn_ref):
        xa = ya_ref[...]
        xl = yl_ref[...]
        na = (xa * _rms_r(xa) * gao_ref[...]).astype(BF16)
        nl = (xl * _rms_r(xl) * glo_ref[...]).astype(BF16)
        yn_ref[:, :half] = na
        yn_ref[:, half:] = nl
        ho_ref[...] = h_ref[...] + _nn(na, w_ref[:half, :]) + _nn(nl, w_ref[half:, :])

    return pl.pallas_call(
        body, name="outproj_fwd", grid=(n // tm,),
        in_specs=[_row(tm, d), _row(tm, half), _row(tm, half), _fixed((1, half)), _fixed((1, half)), VMEM_WHOLE],
        out_specs=[_row(tm, d), _row(tm, 2 * half)],
        out_shape=[jax.ShapeDtypeStruct((n, d), F32), jax.ShapeDtypeStruct((n, 2 * half), BF16)],
        compiler_params=_params(("parallel",)),
    )(h, ya, yl, gao, glo, wout)


def outproj_bwd(dh, ya, yl, gao, glo, wout):
    n, d = dh.shape
    half = ya.shape[1]
    tm = _tile(n, 704)

    def body(dh_ref, ya_ref, yl_ref, gao_ref, glo_ref, w_ref, dya_ref, dyl_ref, dgao_ref, dglo_ref):
        d16 = dh_ref[...].astype(BF16)
        xa = ya_ref[...]
        xl = yl_ref[...]
        dxa, dga = _rms_bwd(xa, _rms_r(xa), gao_ref[...], _nt(d16, w_ref[:half, :]))
        dxl, dgl = _rms_bwd(xl, _rms_r(xl), glo_ref[...], _nt(d16, w_ref[half:, :]))
        dya_ref[...] = dxa
        dyl_ref[...] = dxl
        first = pl.program_id(0) == 0
        _accumulate(dgao_ref, dga, first)
        _accumulate(dglo_ref, dgl, first)

    return pl.pallas_call(
        body, name="outproj_bwd", grid=(n // tm,),
        in_specs=[_row(tm, d), _row(tm, half), _row(tm, half), _fixed((1, half)), _fixed((1, half)), VMEM_WHOLE],
        out_specs=[_row(tm, half), _row(tm, half), _fixed((1, half)), _fixed((1, half))],
        out_shape=[jax.ShapeDtypeStruct((n, half), F32), jax.ShapeDtypeStruct((n, half), F32),
                   jax.ShapeDtypeStruct((1, half), F32), jax.ShapeDtypeStruct((1, half), F32)],
        compiler_params=_params(("arbitrary",)),
    )(dh, ya, yl, gao, glo, wout)


def _loss_and_grad(x, gv, tgt, first_row):
    tm, d = x.shape
    r = _rms_r(x)
    row = first_row + lax.broadcasted_iota(jnp.int32, (tm, d), 0)
    diff = jnp.where(row >= FIRST_FRAME, x * r * gv - tgt, 0.0)
    part = 0.5 * jnp.sum(jnp.sum(diff * diff, axis=-1, keepdims=True) * (1.0 / d), axis=0, keepdims=True)
    dx, dg = _rms_bwd(x, r, gv, diff * (1.0 / d))
    return dx, part, dg


def assemble_cols(g, name):
    _, k, ns = g.shape

    def body(g_ref, o_ref):
        for j in range(N_DEV):
            o_ref[:, j * ns:(j + 1) * ns] = g_ref[j]

    return pl.pallas_call(body, name=name, out_shape=jax.ShapeDtypeStruct((k, N_DEV * ns), g.dtype),
                          compiler_params=_params(None))(g)


def split_cols(x, name):
    k, cols = x.shape
    ns = cols // N_DEV

    def body(x_ref, o_ref):
        for j in range(N_DEV):
            o_ref[j] = x_ref[:, j * ns:(j + 1) * ns]

    return pl.pallas_call(body, name=name, out_shape=jax.ShapeDtypeStruct((N_DEV, k, ns), x.dtype),
                          compiler_params=_params(None))(x)


def _slab_rows(w, per_head):
    k = w.shape[1]
    w = w.reshape(MLA_HEADS, per_head, k)
    return jnp.pad(w, ((0, 0), (0, HEAD_SLAB - per_head), (0, 0))).reshape(MLA_HEADS * HEAD_SLAB, k)


def _unslab_rows(w, per_head):
    k = w.shape[1]
    return w.reshape(MLA_HEADS, HEAD_SLAB, k)[:, :per_head].reshape(MLA_HEADS * per_head, k)


def meta_grad(dh0, nb, lp):
    d = dh0.shape[1]
    ns = d // N_DEV
    per_seq = lp // N_META

    def body(x_ref, o_ref):
        x = x_ref[...]
        for j in range(N_DEV):
            _accumulate(o_ref.at[j], x[:, j * ns:(j + 1) * ns], pl.program_id(0) == 0)

    return pl.pallas_call(
        body, name="meta_grad", grid=(nb,),
        in_specs=[pl.BlockSpec((N_META, d), lambda b: (b * per_seq + PAD // N_META, 0))],
        out_specs=pl.BlockSpec((N_DEV, N_META, ns), lambda b: (0, 0, 0)),
        out_shape=jax.ShapeDtypeStruct((N_DEV, N_META, ns), F32),
        compiler_params=_params(("arbitrary",)))(dh0)


VECTORS = [("ffn1_norm", 1024), ("mix_norm", 1024), ("q_latent_norm", 384), ("kv_latent_norm", 256),
           ("q_head_norm", 192), ("k_head_norm", 192), ("conv_b", 512), ("gate_a_b", 512), ("gate_x_b", 512),
           ("lru_lambda", 512), ("attn_out_norm", 512), ("lru_out_norm", 512), ("ffn2_norm", 1024),
           ("final_norm", 1024)]
VEC_ROWS = 16
LOSS_ROW = len(VECTORS)
GATES = ["gate_a_w", "gate_x_w"]


def pack_vectors(grads, loss):
    def body(*refs):
        o_ref = refs[-1]
        o_ref[...] = jnp.zeros_like(o_ref)
        for t, (ref, (_, cnt)) in enumerate(zip(refs[:-2], VECTORS)):
            o_ref[t:t + 1, :cnt] = ref[:, :cnt]
        o_ref[LOSS_ROW:LOSS_ROW + 1, :LANES] = refs[-2][...]

    return pl.pallas_call(body, name="pack_vectors", out_shape=jax.ShapeDtypeStruct((VEC_ROWS, D_MODEL), F32),
                          compiler_params=_params(None))(*[grads[name] for name, _ in VECTORS], loss)


def _adamw_update(w, g, m, v):
    c1 = 1.0 / (1.0 - ADAM_B1 ** ADAM_STEP)
    c2 = 1.0 / (1.0 - ADAM_B2 ** ADAM_STEP)
    mn = ADAM_B1 * m + (1.0 - ADAM_B1) * g
    vn = ADAM_B2 * v + (1.0 - ADAM_B2) * (g * g)
    delta = -ADAM_LR * ((mn * c1) / (jnp.sqrt(vn * c2) + ADAM_EPS) + ADAM_WD * w)
    return delta, mn, vn


def _sum_slots(ref, index=()):
    acc = ref[(0,) + index].astype(F32)
    for s in range(1, N_DEV):
        acc = acc + ref[(s,) + index].astype(F32)
    return acc


def adamw_sharded(r, w, m, v, name):
    rows, cols = w.shape
    tr = _tile(rows, 256, 16) if rows % 16 == 0 else rows

    def body(r_ref, w_ref, m_ref, v_ref, g_ref, d_ref, mo_ref, vo_ref):
        g = _sum_slots(r_ref)
        g_ref[...] = g
        d_ref[...], mo_ref[...], vo_ref[...] = _adamw_update(w_ref[...], g, m_ref[...], v_ref[...])

    spec = pl.BlockSpec((tr, cols), lambda i: (i, 0))
    shape = jax.ShapeDtypeStruct((rows, cols), F32)
    return pl.pallas_call(
        body, name=name, grid=(rows // tr,),
        in_specs=[pl.BlockSpec((N_DEV, tr, cols), lambda i: (0, i, 0))] + [spec] * 3,
        out_specs=[spec] * 4, out_shape=[shape] * 4,
        compiler_params=_params(("parallel",)),
    )(r, w, m, v)


def adamw_small(r_vec, r_gates, w, m, v):
    nt = len(VECTORS) + len(GATES)

    def body(*refs):
        rv_ref = refs[0]
        rg_refs = refs[1:1 + len(GATES)]
        base = 1 + len(GATES)
        w_refs, m_refs, v_refs = (refs[base + i * nt:base + (i + 1) * nt] for i in range(3))
        outs = refs[base + 3 * nt:]
        g_o, d_o, m_o, v_o = (outs[i * nt:(i + 1) * nt] for i in range(4))
        outs[4 * nt][...] = _sum_slots(rv_ref, (slice(LOSS_ROW, LOSS_ROW + 1), slice(0, LANES)))
        for t in range(nt):
            if t < len(VECTORS):
                cnt = VECTORS[t][1]
                g = _sum_slots(rv_ref, (slice(t, t + 1), slice(0, cnt)))
            else:
                g = _sum_slots(rg_refs[t - len(VECTORS)])
            g_o[t][...] = g
            d_o[t][...], m_o[t][...], v_o[t][...] = _adamw_update(w_refs[t][...], g, m_refs[t][...], v_refs[t][...])

    shapes = [jax.ShapeDtypeStruct(a.shape, F32) for a in w]
    res = pl.pallas_call(body, name="adamw_small", out_shape=shapes * 4 + [jax.ShapeDtypeStruct((1, LANES), F32)],
                         compiler_params=_params(None))(r_vec, *r_gates, *w, *m, *v)
    return [res[i * nt:(i + 1) * nt] for i in range(4)], res[4 * nt]


def _block_diag(w):
    nb, n, _ = w.shape
    eye = jnp.eye(nb, dtype=w.dtype)
    return (eye[:, None, :, None] * w[:, :, None, :]).reshape(nb * n, nb * n)


def _two_d(a):
    if a.ndim == 3:
        return a.reshape(a.shape[1], a.shape[2])
    if a.ndim == 4:
        return a.reshape(a.shape[1] * a.shape[2], a.shape[3])
    return a


_WEIGHT_NAMES = ['meta_tokens', 'ffn1_norm', 'ffn1_w_gate', 'ffn1_w_up', 'ffn1_w_down', 'mix_norm', 'w_in',
                 'q_latent_norm', 'w_uq', 'kv_latent_norm', 'w_uk', 'w_uv', 'q_head_norm', 'k_head_norm', 'conv_w',
                 'conv_b', 'gate_a_w', 'gate_a_b', 'gate_x_w', 'gate_x_b', 'lru_lambda', 'attn_out_norm',
                 'lru_out_norm', 'w_out', 'ffn2_norm', 'ffn2_w_gate', 'ffn2_w_up', 'ffn2_w_down', 'final_norm']


COLUMN_SHARDED = ("ffn1_w_gate", "ffn1_w_up", "ffn2_w_gate", "ffn2_w_up", "w_in", "w_uq", "w_uk", "w_uv")


def train_step(x, tgt, w, m, v):
    nb, seq, d = x.shape
    lp = PAD + N_META + seq
    n = nb * lp

    def local(a, name):
        a = _two_d(a)
        return a.T if name in COLUMN_SHARDED else a

    sh = {name: local(w[name], name) for name in _WEIGHT_NAMES}
    m2 = {name: local(m[name], name) for name in _WEIGHT_NAMES}
    v2 = {name: local(v[name], name) for name in _WEIGHT_NAMES}

    def b16(name):
        return sh[name].astype(BF16)

    out = {}

    def update(name, landed):
        out[name] = adamw_sharded(landed, sh[name], m2[name], v2[name], "adamw_" + name)

    g_meta, g_conv, g_wg1, g_wu1 = exchange(
        [sh["meta_tokens"], sh["conv_w"], b16("ffn1_w_gate"), b16("ffn1_w_up")], ["gather"] * 4, "gather_ffn1")
    wg1, wu1 = g_wg1.reshape(D_FF, d), g_wu1.reshape(D_FF, d)
    meta = assemble_cols(g_meta, "assemble_meta")
    conv_w = assemble_cols(g_conv, "assemble_conv")

    front = jnp.concatenate([jnp.zeros((PAD, d), F32), meta], axis=0)
    h0 = jnp.concatenate([jnp.broadcast_to(front[None], (nb, FIRST_FRAME, d)), x], axis=1).reshape(n, d)
    tgt_p = jnp.concatenate([jnp.zeros((nb, FIRST_FRAME, d), F32), tgt], axis=1).reshape(n, d)
    tables = _rope_tables(lp)
    zero_tail = jnp.zeros((1, HEAD_SLAB - D_QK), F32)
    gqh = jnp.concatenate([sh["q_head_norm"], zero_tail], axis=1)
    gkh = jnp.concatenate([sh["k_head_norm"], zero_tail], axis=1)
    wa = _block_diag(w["gate_a_w"][0]).astype(BF16)
    wx = _block_diag(w["gate_x_w"][0]).astype(BF16)

    (u1, a1, b1, s1), (g_wd1, g_in) = ffn_up(h0, sh["ffn1_norm"], wg1, wu1, "ffn1_up",
                                             comm=([b16("ffn1_w_down"), b16("w_in")], ["gather"] * 2))
    wd1 = g_wd1.reshape(D_FF, d)
    mla_rows = MLA_IN - D_ROPE
    w_in = g_in.reshape(mla_rows + 2 * LRU_WIDTH, d)
    wm = jnp.concatenate([w_in[:mla_rows], jnp.zeros((D_ROPE, d), BF16)], axis=0)
    wl = w_in[mla_rows:]
    (h1, u2, zm, zl), (g_uq, g_uk, g_uv, g_out) = ffn_down_inproj(
        h0, s1, wd1, sh["mix_norm"], wm, wl, "ffn1_down_inproj",
        comm=([b16("w_uq"), b16("w_uk"), b16("w_uv"), b16("w_out")], ["gather"] * 4))
    wuq = _slab_rows(g_uq.reshape(MLA_HEADS * D_QK, Q_RANK), D_QK)
    wuk = _slab_rows(g_uk.reshape(MLA_HEADS * D_NOPE, KV_RANK), D_NOPE)
    wuv = g_uv.reshape(MLA_HEADS * D_V, KV_RANK)
    w_out = g_out.reshape(d, d)

    q, k, vv, qn, cn = mla_prep_fwd(zm, sh["q_latent_norm"], sh["kv_latent_norm"], wuq, wuk, wuv, gqh, gkh, tables, lp)
    (y_mla, lse), (g_wu2, g_wd2) = attn_fwd(
        q, k, vv, nb, lp, comm=([b16("ffn2_w_up"), b16("ffn2_w_down")], ["gather"] * 2))
    (y_lru, hs), (g_wg2,) = lru_fwd(zl, conv_w, sh["conv_b"], wa, wx, sh["gate_a_b"], sh["gate_x_b"], sh["lru_lambda"],
                                    nb, lp, comm=([b16("ffn2_w_gate")], ["gather"]))
    wg2, wu2, wd2 = (g.reshape(D_FF, d) for g in (g_wg2, g_wu2, g_wd2))
    h2, yn = outproj_fwd(h1, y_mla, y_lru, sh["attn_out_norm"], sh["lru_out_norm"], w_out)
    dh3, u3, a3, b3, loss, g_final = ffn_fwd_loss(h2, sh["ffn2_norm"], wg2, wu2, wd2, sh["final_norm"], tgt_p, lp,
                                                  "ffn2_fwd_loss")

    vec = {"final_norm": g_final}
    (dh2, da3, db3, sh3, vec["ffn2_norm"]), _ = ffn_bwd_act(dh3, h2, sh["ffn2_norm"], a3, b3, wg2, wu2, wd2, "ffn2_bwd")
    ff_shards = (N_DEV, D_FF // N_DEV, d)
    dwg2 = tn_matmul(da3, u3, "ffn2_dwg", "bf16").reshape(ff_shards)
    dwu2 = tn_matmul(db3, u3, "ffn2_dwu", "bf16").reshape(ff_shards)
    dwd2 = tn_matmul(sh3, dh3, "ffn2_dwd", "bf16").reshape(ff_shards)

    dy_mla, dy_lru, vec["attn_out_norm"], vec["lru_out_norm"] = outproj_bwd(
        dh2, y_mla, y_lru, sh["attn_out_norm"], sh["lru_out_norm"], w_out)
    dw_out = tn_matmul(yn, dh2, "dw_out", "bf16").reshape(N_DEV, d // N_DEV, d)
    (du, dgate, dconv, vec["conv_b"], vec["gate_a_b"], vec["gate_x_b"], vec["lru_lambda"], dga, dgx), landed = lru_bwd(
        zl, hs, dy_lru, conv_w, sh["conv_b"], wa, wx, sh["gate_a_b"], sh["gate_x_b"], sh["lru_lambda"], nb, lp,
        comm=([dwg2, dw_out], ["scatter"] * 2))
    update("ffn2_w_gate", landed[0])
    update("w_out", landed[1])

    (dq, dk, dv), (r_wu2,) = attn_bwd(q, k, vv, y_mla, dy_mla, lse, nb, lp, comm=([dwu2], ["scatter"]))
    update("ffn2_w_up", r_wu2)

    (dzm, dqr, dkr, vec["q_latent_norm"], vec["kv_latent_norm"], vec["q_head_norm"], vec["k_head_norm"]), (r_wd2,) = (
        mla_prep_bwd(dq, dk, dv, zm, qn, cn, sh["q_latent_norm"], sh["kv_latent_norm"], wuq, wuk, wuv, gqh, gkh,
                     tables, lp, comm=([dwd2], ["scatter"])))
    update("ffn2_w_down", r_wd2)
    dwuq = _unslab_rows(tn_matmul(dqr, qn, "dw_uq", "bf16"), D_QK).reshape(N_DEV, -1, Q_RANK)
    dwuk = _unslab_rows(tn_matmul(dkr, cn, "dw_uk", "bf16"), D_NOPE).reshape(N_DEV, -1, KV_RANK)
    dwuv = tn_matmul(dv, cn, "dw_uv", "bf16").reshape(N_DEV, -1, KV_RANK)
    (dh1, vec["mix_norm"]), landed = inproj_bwd(dzm, du, dgate, dh2, h1, sh["mix_norm"], wm, wl,
                                                comm=([dwuq, dwuk, dwuv], ["scatter"] * 3))
    for name, r in zip(("w_uq", "w_uk", "w_uv"), landed):
        update(name, r)
    dw_in = jnp.concatenate([tn_matmul(dzm, u2, "dw_in_mla", "bf16")[:mla_rows], tn_matmul(du, u2, "dw_in_u", "bf16"),
                             tn_matmul(dgate, u2, "dw_in_gate", "bf16")], axis=0).reshape(N_DEV, -1, d)

    dwd1 = tn_matmul(s1, dh1, "ffn1_dwd", "bf16").reshape(ff_shards)
    (dh0, da1, db1, vec["ffn1_norm"]), landed = ffn_bwd_act(
        dh1, h0, sh["ffn1_norm"], a1, b1, wg1, wu1, wd1, "ffn1_bwd", emit_sh=False,
        comm=([dw_in, split_cols(dconv, "split_conv"), dwd1], ["scatter"] * 3))
    for name, r in zip(("w_in", "conv_w", "ffn1_w_down"), landed):
        update(name, r)

    dwg1 = tn_matmul(da1, u1, "ffn1_dwg", "bf16").reshape(ff_shards)
    dwu1, (r_wg1,) = tn_matmul(db1, u1, "ffn1_dwu", "bf16", comm=([dwg1], ["scatter"]))
    dmeta = meta_grad(dh0, nb, lp)
    gates = [dga.reshape(LRU_WIDTH, LRU_BLOCK), dgx.reshape(LRU_WIDTH, LRU_BLOCK)]
    r_vec, r_ga, r_gx, r_meta, r_wu1 = exchange(
        [pack_vectors(vec, loss)] + gates + [dmeta, dwu1.reshape(ff_shards)], ["gather"] * 3 + ["scatter"] * 2,
        "exchange_last")
    update("ffn1_w_gate", r_wg1)
    update("ffn1_w_up", r_wu1)
    update("meta_tokens", r_meta)

    small = [name for name, _ in VECTORS] + GATES
    res, total_loss = adamw_small(r_vec, [r_ga, r_gx], [sh[nm] for nm in small], [m2[nm] for nm in small],
                                  [v2[nm] for nm in small])
    for i, name in enumerate(small):
        out[name] = [res[j][i] for j in range(4)]

    grad_x = dh0.reshape(nb, lp, d)[:, FIRST_FRAME:]
    loss = total_loss[0, 0]

    def as_given(a, name):
        return (a.T if name in COLUMN_SHARDED else a).reshape(w[name].shape)

    cols = [[as_given(out[name][j], name) for name in _WEIGHT_NAMES] for j in range(4)]
    return (loss, grad_x, *cols[0], *cols[1], *cols[2], *cols[3])


def kernel(x, meta_tokens, ffn1_norm, ffn1_w_gate, ffn1_w_up, ffn1_w_down, mix_norm, w_in, q_latent_norm, w_uq, kv_latent_norm, w_uk, w_uv, q_head_norm, k_head_norm, conv_w, conv_b, gate_a_w, gate_a_b, gate_x_w, gate_x_b, lru_lambda, attn_out_norm, lru_out_norm, w_out, ffn2_norm, ffn2_w_gate, ffn2_w_up, ffn2_w_down, final_norm, loss_target, m_meta_tokens, m_ffn1_norm, m_ffn1_w_gate, m_ffn1_w_up, m_ffn1_w_down, m_mix_norm, m_w_in, m_q_latent_norm, m_w_uq, m_kv_latent_norm, m_w_uk, m_w_uv, m_q_head_norm, m_k_head_norm, m_conv_w, m_conv_b, m_gate_a_w, m_gate_a_b, m_gate_x_w, m_gate_x_b, m_lru_lambda, m_attn_out_norm, m_lru_out_norm, m_w_out, m_ffn2_norm, m_ffn2_w_gate, m_ffn2_w_up, m_ffn2_w_down, m_final_norm, v_meta_tokens, v_ffn1_norm, v_ffn1_w_gate, v_ffn1_w_up, v_ffn1_w_down, v_mix_norm, v_w_in, v_q_latent_norm, v_w_uq, v_kv_latent_norm, v_w_uk, v_w_uv, v_q_head_norm, v_k_head_norm, v_conv_w, v_conv_b, v_gate_a_w, v_gate_a_b, v_gate_x_w, v_gate_x_b, v_lru_lambda, v_attn_out_norm, v_lru_out_norm, v_w_out, v_ffn2_norm, v_ffn2_w_gate, v_ffn2_w_up, v_ffn2_w_down, v_final_norm):
    args = locals()
    w = {name: args[name] for name in _WEIGHT_NAMES}
    m = {name: args["m_" + name] for name in _WEIGHT_NAMES}
    v = {name: args["v_" + name] for name in _WEIGHT_NAMES}
    return train_step(x, loss_target, w, m, v)
```

```python
import math

import jax
import jax.numpy as jnp
from jax import lax
from jax.experimental import pallas as pl
from jax.experimental.pallas import tpu as pltpu

F32 = jnp.float32
BF16 = jnp.bfloat16

D_MODEL = 1024
CHUNK = 64
CHUNK_SHIFT = 6
N_META = 16
PAD = CHUNK - N_META
FIRST_FRAME = PAD + N_META
MLA_HEADS = 4
D_NOPE = 128
D_ROPE = 64
D_QK = D_NOPE + D_ROPE
D_V = 128
HEAD_SLAB = 256
KV_RANK = 256
Q_RANK = 384
ROPE_THETA = 10000.0
LRU_WIDTH = 512
LRU_BLOCKS = 8
LRU_BLOCK = 64
LRU_TILE = 128
CONV_W = 4
C_RGLRU = 8.0
D_FF = 2816
MLA_IN = 768
EPS = 1e-6
NEG_INF = -1e30
N_DEV = 8
LANES = 128
VMEM_LIMIT = 52 * 1024 * 1024
ATTN_HEADS_PER_STEP = 4
TN_ROWS = 4224
TN_X_BYTES = 12 * 1024 * 1024
TN_Y_BYTES = 9 * 1024 * 1024 // 2

ADAM_LR = 0.001
ADAM_B1 = 0.9
ADAM_B2 = 0.999
ADAM_EPS = 1e-08
ADAM_WD = 0.01
ADAM_STEP = 10

VMEM_WHOLE = pl.BlockSpec(memory_space=pltpu.VMEM)
HBM_WHOLE = pl.BlockSpec(memory_space=pl.ANY)


def _params(sems):
    if sems is None:
        return pltpu.CompilerParams(vmem_limit_bytes=VMEM_LIMIT)
    return pltpu.CompilerParams(dimension_semantics=sems, vmem_limit_bytes=VMEM_LIMIT)


def _tile(n, cap, mult=16):
    best = None
    for t in range(mult, min(n, cap) + 1, mult):
        if n % t == 0:
            best = t
    assert best is not None, (n, cap, mult)
    return best


def _row(tm, d):
    return pl.BlockSpec((tm, d), lambda i: (i, 0))


def _fixed(shape):
    return pl.BlockSpec(shape, lambda i: (0,) * len(shape))


def _mesh_position():
    return lax.axis_index("x"), lax.axis_index("y"), lax.axis_index("c")


def _flat_index(x, y, c):
    return 4 * x + 2 * y + c


def _peers(x, y, c):
    out = []
    for k in range(1, N_DEV):
        fx, fy, fc = (k >> 2) & 1, (k >> 1) & 1, k & 1
        out.append((1 - x if fx else x, 1 - y if fy else y, 1 - c if fc else c))
    return out


def _comm_out_shapes(srcs, modes):
    return [jax.ShapeDtypeStruct(s.shape if md == "scatter" else (N_DEV,) + s.shape, s.dtype)
            for s, md in zip(srcs, modes)]


def _comm_scratch(n):
    per_peer = n * (N_DEV - 1)
    return [pltpu.SemaphoreType.DMA((per_peer,)), pltpu.SemaphoreType.DMA((per_peer,)), pltpu.SemaphoreType.DMA((n,))]


class _Copies:
    def __init__(self, own, first, relay):
        self.own, self.first, self.relay = own, first, relay

    def start(self):
        for cp in self.own + self.first:
            cp.start()

    def forward(self):
        for arrival, onward in self.relay:
            arrival.wait_recv()
            onward.start()

    def finish(self):
        arrivals = [a for a, _ in self.relay]
        onward = [f for _, f in self.relay]
        for cp in self.first + onward:
            if not any(cp is a for a in arrivals):
                cp.wait_recv()
        for cp in self.first + onward:
            cp.wait_send()
        for cp in self.own:
            cp.wait()


def _comm_copies(src_refs, dst_refs, modes, send, recv, local):
    x, y, c = _mesh_position()
    me = _flat_index(x, y, c)
    n = len(modes)
    sibling = (x, y, 1 - c)
    chips = [(1 - x, y), (x, 1 - y), (1 - x, 1 - y)]

    def remote(src, dst, k, t, to):
        return pltpu.make_async_remote_copy(src_ref=src, dst_ref=dst, send_sem=send.at[k * n + t],
                                            recv_sem=recv.at[k * n + t], device_id=to,
                                            device_id_type=pl.DeviceIdType.MESH)

    own, first, relay = [], [], []
    for t, (src, dst, md) in enumerate(zip(src_refs, dst_refs, modes)):
        if md == "scatter":
            own.append(pltpu.make_async_copy(src.at[me], dst.at[me], local.at[t]))
            for k, peer in enumerate(_peers(x, y, c)):
                first.append(remote(src.at[_flat_index(*peer)], dst.at[me], k, t, peer))
        elif md == "spread":
            own.append(pltpu.make_async_copy(src, dst.at[me], local.at[t]))
            for k, peer in enumerate(_peers(x, y, c)):
                first.append(remote(src, dst.at[me], k, t, peer))
        else:
            own.append(pltpu.make_async_copy(src, dst.at[me], local.at[t]))
            first.append(remote(src, dst.at[me], 0, t, sibling))
            for j, chip in enumerate(chips):
                arrival = remote(src, dst.at[me], 1 + j, t, (*chip, c))
                landed = dst.at[_flat_index(*chip, c)]
                first.append(arrival)
                relay.append((arrival, remote(landed, landed, 4 + j, t, sibling)))
    return _Copies(own, first, relay)


def _hosted(body, n_in, n_out, modes, grid):
    t = len(modes)
    total = math.prod(grid)

    def wrapped(*refs):
        ins, csrc = refs[:n_in], refs[n_in:n_in + t]
        outs = refs[n_in + t:n_in + t + n_out]
        cdst = refs[n_in + t + n_out:n_in + 2 * t + n_out]
        scratch = refs[n_in + 2 * t + n_out:-3]
        copies = _comm_copies(csrc, cdst, modes, *refs[-3:])
        step = pl.program_id(0)
        for axis in range(1, len(grid)):
            step = step * grid[axis] + pl.program_id(axis)

        @pl.when(step == 0)
        def _():
            copies.start()

        body(*ins, *outs, *scratch)

        @pl.when(step == (total * 4) // 5)
        def _():
            copies.forward()

        @pl.when(step == total - 1)
        def _():
            copies.finish()

    return wrapped


def _call(body, name, grid, in_specs, out_specs, out_shape, sems, args, scratch=(), comm=None):
    if comm is None:
        outs = pl.pallas_call(body, name=name, grid=grid, in_specs=in_specs, out_specs=out_specs, out_shape=out_shape,
                              scratch_shapes=list(scratch), compiler_params=_params(sems))(*args)
        return outs, []
    srcs, modes = comm
    n = len(modes)
    res = pl.pallas_call(
        _hosted(body, len(in_specs), len(out_specs), modes, grid), name=name, grid=grid,
        in_specs=list(in_specs) + [HBM_WHOLE] * n, out_specs=list(out_specs) + [HBM_WHOLE] * n,
        out_shape=list(out_shape) + _comm_out_shapes(srcs, modes),
        scratch_shapes=list(scratch) + _comm_scratch(n),
        compiler_params=_params(("arbitrary",) * len(grid)))(*args, *srcs)
    return res[:len(out_specs)], res[len(out_specs):]


def exchange(srcs, modes, name):
    n = len(modes)

    def body(*refs):
        copies = _comm_copies(refs[:n], refs[n:2 * n], modes, *refs[2 * n:])
        copies.start()
        copies.forward()
        copies.finish()

    return pl.pallas_call(body, name=name, in_specs=[HBM_WHOLE] * n, out_specs=[HBM_WHOLE] * n,
                          out_shape=_comm_out_shapes(srcs, modes), scratch_shapes=_comm_scratch(n))(*srcs)


def _nn(a, b):
    return jnp.dot(a, b, preferred_element_type=F32)


def _nt(a, b):
    return lax.dot_general(a, b, (((1,), (1,)), ((), ())), preferred_element_type=F32)


def _tn(a, b):
    return lax.dot_general(a, b, (((0,), (0,)), ((), ())), preferred_element_type=F32)


def _sig(x):
    return 1.0 / (1.0 + jnp.exp(-x))


def _rms_r(x, n=None):
    n = x.shape[-1] if n is None else n
    return lax.rsqrt(jnp.sum(x * x, axis=-1, keepdims=True) * (1.0 / n) + EPS)


def _rms_bwd(x, r, g, dy, n=None):
    n = x.shape[-1] if n is None else n
    xhat = x * r
    dxhat = dy * g
    dx = r * (dxhat - xhat * (jnp.sum(dxhat * xhat, axis=-1, keepdims=True) * (1.0 / n)))
    return dx, jnp.sum(dy * xhat, axis=0, keepdims=True)


def _accumulate(ref, val, first):
    @pl.when(first)
    def _():
        ref[...] = val

    @pl.when(jnp.logical_not(first))
    def _():
        ref[...] += val


_GELU_C = math.sqrt(2.0 / math.pi)


def _gelu_and_grad(x):
    inner = _GELU_C * (x + 0.044715 * x * x * x)
    t = jnp.tanh(inner)
    gelu = 0.5 * x * (1.0 + t)
    dgelu = 0.5 * (1.0 + t) + 0.5 * x * (1.0 - t * t) * _GELU_C * (1.0 + 3.0 * 0.044715 * x * x)
    return gelu, dgelu


def _log1p_small(t):
    return jnp.where(t < 1e-3, t * (1.0 - t * (0.5 - t * (1.0 / 3.0))), jnp.log(1.0 + t))


def _softplus(x):
    return jnp.maximum(x, 0.0) + _log1p_small(jnp.exp(-jnp.abs(x)))


def _sig_tanh(x):
    return 0.5 + 0.5 * jnp.tanh(0.5 * x)


def _ff_chunks(f):
    return 2 if (f // 2) % LANES == 0 else 1


def _swiglu_half(x, g_ref, wg_ref, wu_ref, wd_ref, a_ref, b_ref, fc):
    f = wg_ref.shape[0]
    u = (x * _rms_r(x) * g_ref[...]).astype(BF16)
    acc = jnp.zeros(x.shape, F32)
    for c in range(f // fc):
        cols = slice(c * fc, (c + 1) * fc)
        a = _nt(u, wg_ref[cols, :])
        b = _nt(u, wu_ref[cols, :])
        s = (a * _sig(a) * b).astype(BF16)
        acc = acc + _nn(s, wd_ref[cols, :])
        a_ref[:, cols] = a.astype(BF16)
        b_ref[:, cols] = b.astype(BF16)
    return x + 0.5 * acc, u


def ffn_up(h, g, wg, wu, name, comm=None):
    n, d = h.shape
    f = wg.shape[0]
    tm = _tile(n, 528)
    fc = 2 * LANES if f % (2 * LANES) == 0 else f

    def body(h_ref, g_ref, wg_ref, wu_ref, u_ref, a_ref, b_ref, s_ref):
        x = h_ref[...]
        u = (x * _rms_r(x) * g_ref[...]).astype(BF16)
        u_ref[...] = u
        for c in range(f // fc):
            cols = slice(c * fc, (c + 1) * fc)
            a = _nt(u, wg_ref[cols, :])
            b = _nt(u, wu_ref[cols, :])
            a_ref[:, cols] = a.astype(BF16)
            b_ref[:, cols] = b.astype(BF16)
            s_ref[:, cols] = (0.5 * (a * _sig(a) * b)).astype(BF16)

    wide = jax.ShapeDtypeStruct((n, f), BF16)
    return _call(
        body, name, (n // tm,),
        [_row(tm, d), _fixed((1, d)), VMEM_WHOLE, VMEM_WHOLE],
        [_row(tm, d), _row(tm, f), _row(tm, f), _row(tm, f)],
        [jax.ShapeDtypeStruct((n, d), BF16), wide, wide, wide],
        ("parallel",), (h, g, wg, wu), comm=comm)


def ffn_down_inproj(h, s, wd, g, wm, wl, name, comm=None):
    n, d = h.shape
    f = wd.shape[0]
    tm = _tile(n, 528)

    def body(h_ref, s_ref, wd_ref, g_ref, wm_ref, wl_ref, ho_ref, u_ref, zm_ref, zl_ref):
        x = h_ref[...] + _nn(s_ref[...], wd_ref[...])
        ho_ref[...] = x
        u = (x * _rms_r(x) * g_ref[...]).astype(BF16)
        u_ref[...] = u
        zm_ref[...] = _nt(u, wm_ref[...])
        zl_ref[...] = _nt(u, wl_ref[...])

    return _call(
        body, name, (n // tm,),
        [_row(tm, d), _row(tm, f), VMEM_WHOLE, _fixed((1, d)), VMEM_WHOLE, VMEM_WHOLE],
        [_row(tm, d), _row(tm, d), _row(tm, MLA_IN), _row(tm, 2 * LRU_WIDTH)],
        [jax.ShapeDtypeStruct((n, d), F32), jax.ShapeDtypeStruct((n, d), BF16),
         jax.ShapeDtypeStruct((n, MLA_IN), F32), jax.ShapeDtypeStruct((n, 2 * LRU_WIDTH), F32)],
        ("parallel",), (h, s, wd, g, wm, wl), comm=comm)


def ffn_fwd_loss(h, g, wg, wu, wd, g_final, tgt, lp, name):
    n, d = h.shape
    f = wg.shape[0]
    tm = _tile(lp, 528)
    per_seq = lp // tm
    fc = 2 * LANES if f % (2 * LANES) == 0 else f

    def body(h_ref, g_ref, wg_ref, wu_ref, wd_ref, gf_ref, t_ref, dh_ref, u_ref, a_ref, b_ref, loss_ref, dgf_ref):
        i = pl.program_id(0)
        y, u_ref[...] = _swiglu_half(h_ref[...], g_ref, wg_ref, wu_ref, wd_ref, a_ref, b_ref, fc)
        dh_ref[...], part, dg = _loss_and_grad(y, gf_ref[...], t_ref[...], (i % per_seq) * tm)
        _accumulate(loss_ref, jnp.broadcast_to(part, (1, LANES)), i == 0)
        _accumulate(dgf_ref, dg, i == 0)

    outs, _ = _call(
        body, name, (n // tm,),
        [_row(tm, d), _fixed((1, d)), VMEM_WHOLE, VMEM_WHOLE, VMEM_WHOLE, _fixed((1, d)), _row(tm, d)],
        [_row(tm, d), _row(tm, d), _row(tm, f), _row(tm, f), _fixed((1, LANES)), _fixed((1, d))],
        [jax.ShapeDtypeStruct((n, d), F32), jax.ShapeDtypeStruct((n, d), BF16),
         jax.ShapeDtypeStruct((n, f), BF16), jax.ShapeDtypeStruct((n, f), BF16),
         jax.ShapeDtypeStruct((1, LANES), F32), jax.ShapeDtypeStruct((1, d), F32)],
        ("arbitrary",), (h, g, wg, wu, wd, g_final, tgt))
    return outs


def ffn_bwd_act(dh, h, g, a, b, wg, wu, wd, name, comm=None, emit_sh=True):
    n, d = h.shape
    f = wg.shape[0]
    tm = _tile(n, 352 if emit_sh else 384)
    nc = _ff_chunks(f)
    fc = f // nc

    def body(dh_ref, h_ref, g_ref, a_ref, b_ref, wg_ref, wu_ref, wd_ref, dhi_ref, da_ref, db_ref, *rest):
        dg_ref = rest[-1]
        x = h_ref[...]
        dy = dh_ref[...]
        r = _rms_r(x)
        dhh = (0.5 * dy).astype(BF16)
        du = jnp.zeros((tm, d), F32)
        for c in range(nc):
            cols = slice(c * fc, (c + 1) * fc)
            ds = _nt(dhh, wd_ref[cols, :])
            av = a_ref[:, cols].astype(F32)
            bv = b_ref[:, cols].astype(F32)
            sg = _sig(av)
            sil = av * sg
            da = (ds * bv * (sg * (1.0 + av * (1.0 - sg)))).astype(BF16)
            db = (ds * sil).astype(BF16)
            da_ref[:, cols] = da
            db_ref[:, cols] = db
            if emit_sh:
                rest[0][:, cols] = (0.5 * sil * bv).astype(BF16)
            du = du + _nn(da, wg_ref[cols, :]) + _nn(db, wu_ref[cols, :])
        dx, dg = _rms_bwd(x, r, g_ref[...], du)
        dhi_ref[...] = dy + dx
        _accumulate(dg_ref, dg, pl.program_id(0) == 0)

    wide = [jax.ShapeDtypeStruct((n, f), BF16)] * (3 if emit_sh else 2)
    return _call(
        body, name, (n // tm,),
        [_row(tm, d), _row(tm, d), _fixed((1, d)), _row(tm, f), _row(tm, f), VMEM_WHOLE, VMEM_WHOLE, VMEM_WHOLE],
        [_row(tm, d)] + [_row(tm, f)] * len(wide) + [_fixed((1, d))],
        [jax.ShapeDtypeStruct((n, d), F32)] + wide + [jax.ShapeDtypeStruct((1, d), F32)],
        ("arbitrary",), (dh, h, g, a, b, wg, wu, wd), comm=comm)


def tn_matmul(x, y, name, out="f32", comm=None):
    n, k = x.shape
    m = y.shape[1]
    tm = _tile(n, TN_ROWS)
    kc, mc = k, (512 if m % 512 == 0 else m)
    while tm * kc * x.dtype.itemsize > TN_X_BYTES and kc % (2 * LANES) == 0:
        kc //= 2
    while tm * mc * y.dtype.itemsize > TN_Y_BYTES and mc % (2 * LANES) == 0:
        mc //= 2
    steps = n // tm

    def body(x_ref, y_ref, o_ref, *acc):
        i = pl.program_id(2)
        part = _tn(x_ref[...].astype(BF16), y_ref[...].astype(BF16))
        if steps == 1:
            o_ref[...] = part.astype(o_ref.dtype)
        elif out == "f32":
            _accumulate(o_ref, part, i == 0)
        else:
            _accumulate(acc[0], part, i == 0)

            @pl.when(i == steps - 1)
            def _():
                o_ref[...] = acc[0][...].astype(BF16)

    out_shape = jax.ShapeDtypeStruct((k, m), F32 if out == "f32" else BF16)
    (res,), landed = _call(
        body, name, (k // kc, m // mc, steps),
        [pl.BlockSpec((tm, kc), lambda a, b, i: (i, a)), pl.BlockSpec((tm, mc), lambda a, b, i: (i, b))],
        [pl.BlockSpec((kc, mc), lambda a, b, i: (a, b))], [out_shape], ("parallel", "parallel", "arbitrary"), (x, y),
        scratch=[pltpu.VMEM((kc, mc), F32)] if (out == "bf16" and steps > 1) else [], comm=comm)
    return (res, landed) if comm is not None else res


def inproj_bwd(dzm, du, dgate, dh2, h, g, wm, wl, comm=None):
    n, d = h.shape
    tm = _tile(n, 352)

    def body(dzm_ref, du_ref, dgt_ref, dh2_ref, h_ref, g_ref, wm_ref, wl_ref, dh_ref, dg_ref):
        x = h_ref[...]
        dun = (_nn(dzm_ref[...].astype(BF16), wm_ref[...])
               + _nn(du_ref[...].astype(BF16), wl_ref[:LRU_WIDTH, :])
               + _nn(dgt_ref[...].astype(BF16), wl_ref[LRU_WIDTH:, :]))
        dx, dg = _rms_bwd(x, _rms_r(x), g_ref[...], dun)
        dh_ref[...] = dh2_ref[...] + dx
        _accumulate(dg_ref, dg, pl.program_id(0) == 0)

    return _call(
        body, "inproj_bwd", (n // tm,),
        [_row(tm, MLA_IN), _row(tm, LRU_WIDTH), _row(tm, LRU_WIDTH), _row(tm, d), _row(tm, d),
         _fixed((1, d)), VMEM_WHOLE, VMEM_WHOLE],
        [_row(tm, d), _fixed((1, d))],
        [jax.ShapeDtypeStruct((n, d), F32), jax.ShapeDtypeStruct((1, d), F32)],
        ("arbitrary",), (dzm, du, dgate, dh2, h, g, wm, wl), comm=comm)


def _rope_tables(lp):
    pos = jnp.arange(lp, dtype=F32) - float(PAD)
    half = D_ROPE // 2
    inv_freq = ROPE_THETA ** (-jnp.arange(0, half, dtype=F32) / half)
    ang = pos[:, None] * inv_freq[None, :]
    cos, sin = jnp.cos(ang), jnp.sin(ang)
    one = jnp.ones((lp, D_NOPE), F32)
    z_nope = jnp.zeros((lp, D_NOPE), F32)
    z_half = jnp.zeros((lp, half), F32)
    z_tail = jnp.zeros((lp, HEAD_SLAB - D_QK), F32)
    cosr = jnp.concatenate([one, cos, cos, z_tail], axis=1)
    sin_up = jnp.concatenate([z_nope, z_half, sin, z_tail], axis=1)
    sin_dn = jnp.concatenate([z_nope, -sin, z_half, z_tail], axis=1)
    return cosr, sin_up, sin_dn


def _rope(x, cosr, sin_up, sin_dn):
    half = D_ROPE // 2
    return x * cosr + pltpu.roll(x, half, axis=1) * sin_up + pltpu.roll(x, HEAD_SLAB - half, axis=1) * sin_dn


def _rope_bwd(dy, cosr, sin_up, sin_dn):
    half = D_ROPE // 2
    return (dy * cosr + pltpu.roll(dy * sin_up, HEAD_SLAB - half, axis=1)
            + pltpu.roll(dy * sin_dn, half, axis=1))


def _k_rope_slab(zm_tile):
    tm = zm_tile.shape[0]
    krp = zm_tile[:, Q_RANK + KV_RANK:MLA_IN]
    return jnp.concatenate([jnp.zeros((tm, D_NOPE), F32), krp], axis=1)


def mla_prep_fwd(zm, gql, gkvl, wuq, wuk, wuv, gqh, gkh, tables, lp):
    n = zm.shape[0]
    tm = _tile(lp, 352)
    per_seq = lp // tm
    width = MLA_HEADS * HEAD_SLAB
    scale = 1.0 / math.sqrt(D_QK)

    def body(zm_ref, gql_ref, gkvl_ref, wuq_ref, wuk_ref, wuv_ref, gqh_ref, gkh_ref,
             cos_ref, up_ref, dn_ref, q_ref, k_ref, v_ref, qn_ref, cn_ref):
        z = zm_ref[...]
        cq = z[:, :Q_RANK]
        ckv = z[:, Q_RANK:Q_RANK + KV_RANK]
        qn = (cq * _rms_r(cq) * gql_ref[...]).astype(BF16)
        cn = (ckv * _rms_r(ckv) * gkvl_ref[...]).astype(BF16)
        qn_ref[...] = qn
        cn_ref[...] = cn
        q_raw = _nt(qn, wuq_ref[...])
        k_raw = _nt(cn, wuk_ref[...])
        v_ref[...] = _nt(cn, wuv_ref[...]).astype(BF16)
        kr_slab = _k_rope_slab(z)
        cosr, sin_up, sin_dn = cos_ref[...], up_ref[...], dn_ref[...]
        for hd in range(MLA_HEADS):
            cols = slice(hd * HEAD_SLAB, (hd + 1) * HEAD_SLAB)
            xq = q_raw[:, cols]
            yq = _rope(xq * _rms_r(xq, D_QK) * gqh_ref[...], cosr, sin_up, sin_dn)
            q_ref[:, cols] = (yq * scale).astype(BF16)
            xk = k_raw[:, cols] + kr_slab
            yk = _rope(xk * _rms_r(xk, D_QK) * gkh_ref[...], cosr, sin_up, sin_dn)
            k_ref[:, cols] = yk.astype(BF16)

    tab = pl.BlockSpec((tm, HEAD_SLAB), lambda i: (i % per_seq, 0))
    return pl.pallas_call(
        body, name="mla_prep_fwd", grid=(n // tm,),
        in_specs=[_row(tm, MLA_IN), _fixed((1, Q_RANK)), _fixed((1, KV_RANK)), VMEM_WHOLE, VMEM_WHOLE, VMEM_WHOLE,
                  _fixed((1, HEAD_SLAB)), _fixed((1, HEAD_SLAB)), tab, tab, tab],
        out_specs=[_row(tm, width), _row(tm, width), _row(tm, MLA_HEADS * D_V), _row(tm, Q_RANK), _row(tm, KV_RANK)],
        out_shape=[jax.ShapeDtypeStruct((n, width), BF16), jax.ShapeDtypeStruct((n, width), BF16),
                   jax.ShapeDtypeStruct((n, MLA_HEADS * D_V), BF16), jax.ShapeDtypeStruct((n, Q_RANK), BF16),
                   jax.ShapeDtypeStruct((n, KV_RANK), BF16)],
        compiler_params=_params(("parallel",)),
    )(zm, gql, gkvl, wuq, wuk, wuv, gqh, gkh, *tables)


def mla_prep_bwd(dq, dk, dv, zm, qn, cn, gql, gkvl, wuq, wuk, wuv, gqh, gkh, tables, lp, comm=None):
    n = zm.shape[0]
    tm = _tile(lp, 704)
    per_seq = lp // tm
    width = MLA_HEADS * HEAD_SLAB
    scale = 1.0 / math.sqrt(D_QK)

    def body(dq_ref, dk_ref, dv_ref, zm_ref, qn_ref, cn_ref, gql_ref, gkvl_ref, wuq_ref, wuk_ref, wuv_ref,
             gqh_ref, gkh_ref, cos_ref, up_ref, dn_ref,
             dzm_ref, dqr_ref, dkr_ref, dgql_ref, dgkvl_ref, dgqh_ref, dgkh_ref):
        z = zm_ref[...]
        cq = z[:, :Q_RANK]
        ckv = z[:, Q_RANK:Q_RANK + KV_RANK]
        q_raw = _nt(qn_ref[...], wuq_ref[...])
        k_raw = _nt(cn_ref[...], wuk_ref[...])
        kr_slab = _k_rope_slab(z)
        cosr, sin_up, sin_dn = cos_ref[...], up_ref[...], dn_ref[...]
        dgq = jnp.zeros((1, HEAD_SLAB), F32)
        dgk = jnp.zeros((1, HEAD_SLAB), F32)
        dkrp = jnp.zeros((tm, HEAD_SLAB - D_NOPE), F32)
        for hd in range(MLA_HEADS):
            cols = slice(hd * HEAD_SLAB, (hd + 1) * HEAD_SLAB)
            xq = q_raw[:, cols]
            dxn = _rope_bwd(dq_ref[:, cols] * scale, cosr, sin_up, sin_dn)
            dxq, dg = _rms_bwd(xq, _rms_r(xq, D_QK), gqh_ref[...], dxn, D_QK)
            dgq = dgq + dg
            dqr_ref[:, cols] = dxq.astype(BF16)
            xk = k_raw[:, cols] + kr_slab
            dxn = _rope_bwd(dk_ref[:, cols], cosr, sin_up, sin_dn)
            dxk, dg = _rms_bwd(xk, _rms_r(xk, D_QK), gkh_ref[...], dxn, D_QK)
            dgk = dgk + dg
            dkr_ref[:, cols] = dxk.astype(BF16)
            dkrp = dkrp + dxk[:, D_NOPE:]
        dqn = _nn(dqr_ref[...], wuq_ref[...])
        dcn = _nn(dkr_ref[...], wuk_ref[...]) + _nn(dv_ref[...].astype(BF16), wuv_ref[...])
        dcq, dg1 = _rms_bwd(cq, _rms_r(cq), gql_ref[...], dqn)
        dckv, dg2 = _rms_bwd(ckv, _rms_r(ckv), gkvl_ref[...], dcn)
        dzm_ref[:, :Q_RANK] = dcq.astype(BF16)
        dzm_ref[:, Q_RANK:Q_RANK + KV_RANK] = dckv.astype(BF16)
        dzm_ref[:, Q_RANK + KV_RANK:] = dkrp.astype(BF16)
        first = pl.program_id(0) == 0
        _accumulate(dgql_ref, dg1, first)
        _accumulate(dgkvl_ref, dg2, first)
        _accumulate(dgqh_ref, dgq, first)
        _accumulate(dgkh_ref, dgk, first)

    tab = pl.BlockSpec((tm, HEAD_SLAB), lambda i: (i % per_seq, 0))
    return _call(
        body, "mla_prep_bwd", (n // tm,),
        [_row(tm, width), _row(tm, width), _row(tm, MLA_HEADS * D_V), _row(tm, MLA_IN),
         _row(tm, Q_RANK), _row(tm, KV_RANK), _fixed((1, Q_RANK)), _fixed((1, KV_RANK)),
         VMEM_WHOLE, VMEM_WHOLE, VMEM_WHOLE, _fixed((1, HEAD_SLAB)), _fixed((1, HEAD_SLAB)), tab, tab, tab],
        [_row(tm, MLA_IN), _row(tm, width), _row(tm, width), _fixed((1, Q_RANK)), _fixed((1, KV_RANK)),
         _fixed((1, HEAD_SLAB)), _fixed((1, HEAD_SLAB))],
        [jax.ShapeDtypeStruct((n, MLA_IN), BF16), jax.ShapeDtypeStruct((n, width), BF16),
         jax.ShapeDtypeStruct((n, width), BF16), jax.ShapeDtypeStruct((1, Q_RANK), F32),
         jax.ShapeDtypeStruct((1, KV_RANK), F32), jax.ShapeDtypeStruct((1, HEAD_SLAB), F32),
         jax.ShapeDtypeStruct((1, HEAD_SLAB), F32)],
        ("arbitrary",), (dq, dk, dv, zm, qn, cn, gql, gkvl, wuq, wuk, wuv, gqh, gkh, *tables), comm=comm)


def _attn_tile(lp):
    return _tile(lp, 704, CHUNK)


def _chunk_mask(i, j, t):
    qpos = i * t + lax.broadcasted_iota(jnp.int32, (t, t), 0)
    kpos = j * t + lax.broadcasted_iota(jnp.int32, (t, t), 1)
    same_or_earlier = jnp.right_shift(kpos, CHUNK_SHIFT) <= jnp.right_shift(qpos, CHUNK_SHIFT)
    return jnp.logical_and(same_or_earlier, kpos >= PAD)


def _masked_scores(s, i, j, t, diagonal):
    if diagonal:
        return jnp.where(_chunk_mask(i, j, t), s, NEG_INF)
    kpos = j * t + lax.broadcasted_iota(jnp.int32, (1, t), 1)
    return s + jnp.where(kpos < PAD, NEG_INF, 0.0)


def attn_fwd(q, k, v, nb, lp, comm=None):
    n = q.shape[0]
    t = _attn_tile(lp)
    nq = lp // t

    hp = ATTN_HEADS_PER_STEP

    def body(q_ref, k_ref, v_ref, o_ref, lse_ref):
        i = pl.program_id(2)
        qs = [q_ref[:, hh * HEAD_SLAB:(hh + 1) * HEAD_SLAB] for hh in range(hp)]

        def kv_step(j, carry, diagonal=False):
            off = pl.multiple_of(j * t, t)
            out = []
            for hh in range(hp):
                m, l, acc = carry[hh]
                kv = k_ref[pl.ds(off, t), hh * HEAD_SLAB:(hh + 1) * HEAD_SLAB]
                s = _masked_scores(_nt(qs[hh], kv), i, j, t, diagonal)
                m_new = jnp.maximum(m, jnp.max(s, axis=-1, keepdims=True))
                p = jnp.exp(s - m_new)
                alpha = jnp.exp(m - m_new)
                l = alpha * l + jnp.sum(p, axis=-1, keepdims=True)
                acc = alpha * acc + _nn(p.astype(BF16), v_ref[pl.ds(off, t), hh * D_V:(hh + 1) * D_V])
                out.append((m_new, l, acc))
            return tuple(out)

        init = tuple((jnp.full((t, 1), NEG_INF, F32), jnp.zeros((t, 1), F32), jnp.zeros((t, D_V), F32))
                     for _ in range(hp))
        done = kv_step(i, lax.fori_loop(0, i, kv_step, init), diagonal=True)
        for hh, (m, l, acc) in enumerate(done):
            o_ref[:, hh * D_V:(hh + 1) * D_V] = acc * (1.0 / l)
            lse_ref[hh] = jnp.broadcast_to(m + jnp.log(l), (t, LANES))

    return _call(
        body, "attn_fwd", (nb, MLA_HEADS // hp, nq),
        [pl.BlockSpec((t, hp * HEAD_SLAB), lambda b, h, i: (b * nq + i, h)),
         pl.BlockSpec((lp, hp * HEAD_SLAB), lambda b, h, i: (b, h)),
         pl.BlockSpec((lp, hp * D_V), lambda b, h, i: (b, h))],
        [pl.BlockSpec((t, hp * D_V), lambda b, h, i: (b * nq + i, h)),
         pl.BlockSpec((hp, t, LANES), lambda b, h, i: (h, b * nq + i, 0))],
        [jax.ShapeDtypeStruct((n, MLA_HEADS * D_V), F32), jax.ShapeDtypeStruct((MLA_HEADS, n, LANES), F32)],
        ("parallel", "parallel", "parallel"), (q, k, v), comm=comm)


def attn_bwd(q, k, v, o, do, lse, nb, lp, comm=None):
    n = q.shape[0]
    t = _attn_tile(lp)
    nq = lp // t

    def body(q_ref, k_ref, v_ref, o_ref, do_ref, lse_ref, dq_ref, dk_ref, dv_ref):
        dk_ref[...] = jnp.zeros_like(dk_ref)
        dv_ref[...] = jnp.zeros_like(dv_ref)

        def q_step(i, _):
            qoff = pl.multiple_of(i * t, t)
            qv = q_ref[pl.ds(qoff, t), :]
            dov = do_ref[pl.ds(qoff, t), :]
            delta = jnp.sum(o_ref[pl.ds(qoff, t), :] * dov, axis=-1, keepdims=True)
            lse_q = jnp.max(lse_ref[0, pl.ds(qoff, t), :], axis=-1, keepdims=True)
            do16 = dov.astype(BF16)

            def kv_step(j, dq_acc, diagonal=False):
                koff = pl.multiple_of(j * t, t)
                kv = k_ref[pl.ds(koff, t), :]
                s = _masked_scores(_nt(qv, kv), i, j, t, diagonal)
                p = jnp.exp(s - lse_q)
                dp = _nt(do16, v_ref[pl.ds(koff, t), :])
                ds16 = (p * (dp - delta)).astype(BF16)
                dv_ref[pl.ds(koff, t), :] += _tn(p.astype(BF16), do16)
                dk_ref[pl.ds(koff, t), :] += _tn(ds16, qv)
                return dq_acc + _nn(ds16, kv)

            earlier = lax.fori_loop(0, i, kv_step, jnp.zeros((t, HEAD_SLAB), F32))
            dq_ref[pl.ds(qoff, t), :] = kv_step(i, earlier, diagonal=True)
            return 0

        lax.fori_loop(0, nq, q_step, 0)

    wide = pl.BlockSpec((lp, HEAD_SLAB), lambda b, h: (b, h))
    thin = pl.BlockSpec((lp, D_V), lambda b, h: (b, h))
    width = MLA_HEADS * HEAD_SLAB
    return _call(
        body, "attn_bwd", (nb, MLA_HEADS),
        [wide, wide, thin, thin, thin, pl.BlockSpec((1, lp, LANES), lambda b, h: (h, b, 0))],
        [wide, wide, thin],
        [jax.ShapeDtypeStruct((n, width), F32), jax.ShapeDtypeStruct((n, width), F32),
         jax.ShapeDtypeStruct((n, MLA_HEADS * D_V), F32)],
        ("parallel", "parallel"), (q, k, v, o, do, lse), comm=comm)


def _seq_rows(nb, lp, width):
    rows = lax.broadcasted_iota(jnp.int32, (lp, width), 0)
    return jnp.concatenate([rows] * nb, axis=0) if nb > 1 else rows


def _lru_gates(u, w_ref, cb, wa, wx, ba, bx, lam):
    xc = (cb + w_ref[pl.ds(3, 1), :] * u + w_ref[pl.ds(2, 1), :] * pltpu.roll(u, 1, axis=0)
          + w_ref[pl.ds(1, 1), :] * pltpu.roll(u, 2, axis=0) + w_ref[pl.ds(0, 1), :] * pltpu.roll(u, 3, axis=0))
    xc16 = xc.astype(BF16)
    ra = _sig_tanh(_nn(xc16, wa) + ba)
    ia = _sig_tanh(_nn(xc16, wx) + bx)
    sp = _softplus(-lam)
    log_a = -C_RGLRU * ra * sp
    a = jnp.exp(log_a)
    x2 = 2.0 * log_a
    mult = jnp.sqrt(jnp.where(x2 > -1e-2, -x2 * (1.0 + x2 * (0.5 + x2 * (1.0 / 6.0))), 1.0 - a * a))
    return xc, xc16, ra, ia, sp, a, mult


def _scan_block_rows(width):
    return lax.broadcasted_iota(jnp.int32, (8, width), 0)


def lru_fwd(zl, conv_w, conv_b, wa, wx, ba, bx, lam, nb, lp, comm=None):
    n = zl.shape[0]
    w = LRU_TILE
    nt = LRU_WIDTH // w
    nblk = lp // 8

    def body(u_ref, gt_ref, cw_ref, cb_ref, wa_ref, wx_ref, ba_ref, bx_ref, lam_ref, y_ref, h_ref, a_s, b_s):
        u = u_ref[...]
        xc, _, _, ia, _, a, mult = _lru_gates(u, cw_ref, cb_ref[...], wa_ref[...], wx_ref[...],
                                              ba_ref[...], bx_ref[...], lam_ref[...])
        row = _seq_rows(nb, lp, w)
        mult = jnp.where(row == PAD, 1.0, mult)
        a_s[...] = a
        b_s[...] = jnp.where(row < PAD, 0.0, mult * (ia * xc))
        r8 = _scan_block_rows(w)

        def blk(i, carry):
            out = []
            for s_id in range(nb):
                off = pl.multiple_of(s_id * lp + i * 8, 8)
                av = a_s[pl.ds(off, 8), :]
                bv = b_s[pl.ds(off, 8), :]
                for sh in (1, 2, 4):
                    keep = r8 >= sh
                    bv = jnp.where(keep, av * pltpu.roll(bv, sh, axis=0) + bv, bv)
                    av = jnp.where(keep, av * pltpu.roll(av, sh, axis=0), av)
                hv = bv + av * carry[s_id]
                h_ref[pl.ds(off, 8), :] = hv
                out.append(jnp.sum(jnp.where(r8 == 7, hv, 0.0), axis=0, keepdims=True))
            return tuple(out)

        lax.fori_loop(0, nblk, blk, tuple(jnp.zeros((1, w), F32) for _ in range(nb)))
        gelu, _ = _gelu_and_grad(gt_ref[...])
        y_ref[...] = h_ref[...] * gelu

    col = lambda c: (0, c)
    return _call(
        body, "lru_fwd", (nt,),
        [pl.BlockSpec((n, w), col), pl.BlockSpec((n, w), lambda c: (0, nt + c)),
         pl.BlockSpec((CONV_W, w), col), pl.BlockSpec((1, w), col),
         pl.BlockSpec((w, w), lambda c: (c, c)), pl.BlockSpec((w, w), lambda c: (c, c)),
         pl.BlockSpec((1, w), col), pl.BlockSpec((1, w), col), pl.BlockSpec((1, w), col)],
        [pl.BlockSpec((n, w), col), pl.BlockSpec((n, w), col)],
        [jax.ShapeDtypeStruct((n, LRU_WIDTH), F32), jax.ShapeDtypeStruct((n, LRU_WIDTH), F32)],
        ("parallel",), (zl, zl, conv_w, conv_b, wa, wx, ba, bx, lam),
        scratch=[pltpu.VMEM((n, w), F32), pltpu.VMEM((n, w), F32)], comm=comm)


def lru_bwd(zl, hs, dy, conv_w, conv_b, wa, wx, ba, bx, lam, nb, lp, comm=None):
    n = zl.shape[0]
    w = LRU_TILE
    nt = LRU_WIDTH // w
    nblk = lp // 8

    def body(u_ref, gt_ref, h_ref, dy_ref, cw_ref, cb_ref, wa_ref, wx_ref, ba_ref, bx_ref, lam_ref,
             du_ref, dgt_ref, dcw_ref, dcb_ref, dba_ref, dbx_ref, dlam_ref, dwa_ref, dwx_ref,
             c_s, d_s, g_s, dwa_s, dwx_s):
        u = u_ref[...]
        lam = lam_ref[...]
        xc, xc16, ra, ia, sp, a, mult = _lru_gates(u, cw_ref, cb_ref[...], wa_ref[...], wx_ref[...],
                                                   ba_ref[...], bx_ref[...], lam)
        row = lax.broadcasted_iota(jnp.int32, (lp, w), 0)
        hv = h_ref[...]
        dyv = dy_ref[...]
        gelu, dgelu = _gelu_and_grad(gt_ref[...])
        dgt_ref[...] = jnp.where(row >= PAD, dyv * hv * dgelu, 0.0).astype(BF16)
        c_s[...] = pltpu.roll(a, lp - 1, axis=0)
        d_s[...] = dyv * gelu
        r8 = _scan_block_rows(w)

        def blk(ii, carry):
            off = pl.multiple_of((nblk - 1 - ii) * 8, 8)
            cv = c_s[pl.ds(off, 8), :]
            dv = d_s[pl.ds(off, 8), :]
            for sh in (1, 2, 4):
                keep = r8 < 8 - sh
                dv = jnp.where(keep, cv * pltpu.roll(dv, 8 - sh, axis=0) + dv, dv)
                cv = jnp.where(keep, cv * pltpu.roll(cv, 8 - sh, axis=0), cv)
            gv = dv + cv * carry
            g_s[pl.ds(off, 8), :] = gv
            return jnp.sum(jnp.where(r8 == 0, gv, 0.0), axis=0, keepdims=True)

        lax.fori_loop(0, nblk, blk, jnp.zeros((1, w), F32))
        gv = g_s[...]
        first_row = row == PAD
        db = jnp.where(row >= PAD, gv, 0.0)
        da = jnp.where(row > PAD, gv * pltpu.roll(hv, 1, axis=0), 0.0)
        mult_eff = jnp.where(first_row, 1.0, mult)
        dmult = jnp.where(first_row, 0.0, db * (ia * xc))
        dia = db * mult_eff * xc
        dxc = db * mult_eff * ia
        dla = da * a - dmult * (a * a) / mult
        dra = dla * (-C_RGLRU * sp)
        dsp = jnp.sum(dla * (-C_RGLRU * ra), axis=0, keepdims=True)
        dpa = dra * ra * (1.0 - ra)
        dpx = dia * ia * (1.0 - ia)
        dpa16 = dpa.astype(BF16)
        dpx16 = dpx.astype(BF16)
        dxc = dxc + _nt(dpa16, wa_ref[...]) + _nt(dpx16, wx_ref[...])
        du = cw_ref[pl.ds(CONV_W - 1, 1), :] * dxc
        dcw = [jnp.sum(dxc * u, axis=0, keepdims=True)]
        for tap in range(1, CONV_W):
            dcw.insert(0, jnp.sum(dxc * pltpu.roll(u, tap, axis=0), axis=0, keepdims=True))
            du = du + cw_ref[pl.ds(CONV_W - 1 - tap, 1), :] * pltpu.roll(dxc, lp - tap, axis=0)
        du_ref[...] = jnp.where(row >= PAD, du, 0.0).astype(BF16)
        first = pl.program_id(1) == 0
        _accumulate(dlam_ref, -_sig(-lam) * dsp, first)
        _accumulate(dba_ref, jnp.sum(dpa, axis=0, keepdims=True), first)
        _accumulate(dbx_ref, jnp.sum(dpx, axis=0, keepdims=True), first)
        _accumulate(dcb_ref, jnp.sum(dxc, axis=0, keepdims=True), first)
        _accumulate(dcw_ref, jnp.concatenate(dcw, axis=0), first)
        _accumulate(dwa_s, _tn(xc16, dpa16), first)
        _accumulate(dwx_s, _tn(xc16, dpx16), first)

        @pl.when(pl.program_id(1) == nb - 1)
        def _():
            for j in range(w // LRU_BLOCK):
                blk_rows = slice(j * LRU_BLOCK, (j + 1) * LRU_BLOCK)
                dwa_ref[0, blk_rows, :] = dwa_s[blk_rows, blk_rows]
                dwx_ref[0, blk_rows, :] = dwx_s[blk_rows, blk_rows]

    col = lambda c, b: (0, c)
    vec = pl.BlockSpec((1, w), col)
    mat = pl.BlockSpec((w, w), lambda c, b: (c, c))
    big = pl.BlockSpec((lp, w), lambda c, b: (b, c))
    dmat = pl.BlockSpec((1, w, LRU_BLOCK), lambda c, b: (c, 0, 0))
    return _call(
        body, "lru_bwd", (nt, nb),
        [big, pl.BlockSpec((lp, w), lambda c, b: (b, nt + c)), big, big,
         pl.BlockSpec((CONV_W, w), col), vec, mat, mat, vec, vec, vec],
        [big, big, pl.BlockSpec((CONV_W, w), col), vec, vec, vec, vec, dmat, dmat],
        [jax.ShapeDtypeStruct((n, LRU_WIDTH), BF16), jax.ShapeDtypeStruct((n, LRU_WIDTH), BF16),
         jax.ShapeDtypeStruct((CONV_W, LRU_WIDTH), F32), jax.ShapeDtypeStruct((1, LRU_WIDTH), F32),
         jax.ShapeDtypeStruct((1, LRU_WIDTH), F32), jax.ShapeDtypeStruct((1, LRU_WIDTH), F32),
         jax.ShapeDtypeStruct((1, LRU_WIDTH), F32), jax.ShapeDtypeStruct((nt, w, LRU_BLOCK), F32),
         jax.ShapeDtypeStruct((nt, w, LRU_BLOCK), F32)],
        ("parallel", "arbitrary"), (zl, zl, hs, dy, conv_w, conv_b, wa, wx, ba, bx, lam),
        scratch=[pltpu.VMEM((lp, w), F32), pltpu.VMEM((lp, w), F32), pltpu.VMEM((lp, w), F32),
                 pltpu.VMEM((w, w), F32), pltpu.VMEM((w, w), F32)], comm=comm)


def outproj_fwd(h, ya, yl, gao, glo, wout):
    n, d = h.shape
    half = ya.shape[1]
    tm = _tile(n, 704)

    def body(h_ref, ya_ref, yl_ref, gao_ref, glo_ref, w_ref, ho_ref, yn_ref):
        xa = ya_ref[...]
        xl = yl_ref[...]
        na = (xa * _rms_r(xa) * gao_ref[...]).astype(BF16)
        nl = (xl * _rms_r(xl) * glo_ref[...]).astype(BF16)
        yn_ref[:, :half] = na
        yn_ref[:, half:] = nl
        ho_ref[...] = h_ref[...] + _nn(na, w_ref[:half, :]) + _nn(nl, w_ref[half:, :])

    return pl.pallas_call(
        body, name="outproj_fwd", grid=(n // tm,),
        in_specs=[_row(tm, d), _row(tm, half), _row(tm, half), _fixed((1, half)), _fixed((1, half)), VMEM_WHOLE],
        out_specs=[_row(tm, d), _row(tm, 2 * half)],
        out_shape=[jax.ShapeDtypeStruct((n, d), F32), jax.ShapeDtypeStruct((n, 2 * half), BF16)],
        compiler_params=_params(("parallel",)),
    )(h, ya, yl, gao, glo, wout)


def outproj_bwd(dh, ya, yl, gao, glo, wout):
    n, d = dh.shape
    half = ya.shape[1]
    tm = _tile(n, 704)

    def body(dh_ref, ya_ref, yl_ref, gao_ref, glo_ref, w_ref, dya_ref, dyl_ref, dgao_ref, dglo_ref):
        d16 = dh_ref[...].astype(BF16)
        xa = ya_ref[...]
        xl = yl_ref[...]
        dxa, dga = _rms_bwd(xa, _rms_r(xa), gao_ref[...], _nt(d16, w_ref[:half, :]))
        dxl, dgl = _rms_bwd(xl, _rms_r(xl), glo_ref[...], _nt(d16, w_ref[half:, :]))
        dya_ref[...] = dxa
        dyl_ref[...] = dxl
        first = pl.program_id(0) == 0
        _accumulate(dgao_ref, dga, first)
        _accumulate(dglo_ref, dgl, first)

    return pl.pallas_call(
        body, name="outproj_bwd", grid=(n // tm,),
        in_specs=[_row(tm, d), _row(tm, half), _row(tm, half), _fixed((1, half)), _fixed((1, half)), VMEM_WHOLE],
        out_specs=[_row(tm, half), _row(tm, half), _fixed((1, half)), _fixed((1, half))],
        out_shape=[jax.ShapeDtypeStruct((n, half), F32), jax.ShapeDtypeStruct((n, half), F32),
                   jax.ShapeDtypeStruct((1, half), F32), jax.ShapeDtypeStruct((1, half), F32)],
        compiler_params=_params(("arbitrary",)),
    )(dh, ya, yl, gao, glo, wout)


def _loss_and_grad(x, gv, tgt, first_row):
    tm, d = x.shape
    r = _rms_r(x)
    row = first_row + lax.broadcasted_iota(jnp.int32, (tm, d), 0)
    diff = jnp.where(row >= FIRST_FRAME, x * r * gv - tgt, 0.0)
    part = 0.5 * jnp.sum(jnp.sum(diff * diff, axis=-1, keepdims=True) * (1.0 / d), axis=0, keepdims=True)
    dx, dg = _rms_bwd(x, r, gv, diff * (1.0 / d))
    return dx, part, dg


def assemble_cols(g, name):
    _, k, ns = g.shape

    def body(g_ref, o_ref):
        for j in range(N_DEV):
            o_ref[:, j * ns:(j + 1) * ns] = g_ref[j]

    return pl.pallas_call(body, name=name, out_shape=jax.ShapeDtypeStruct((k, N_DEV * ns), g.dtype),
                          compiler_params=_params(None))(g)


def split_cols(x, name):
    k, cols = x.shape
    ns = cols // N_DEV

    def body(x_ref, o_ref):
        for j in range(N_DEV):
            o_ref[j] = x_ref[:, j * ns:(j + 1) * ns]

    return pl.pallas_call(body, name=name, out_shape=jax.ShapeDtypeStruct((N_DEV, k, ns), x.dtype),
                          compiler_params=_params(None))(x)


def _slab_rows(w, per_head):
    k = w.shape[1]
    w = w.reshape(MLA_HEADS, per_head, k)
    return jnp.pad(w, ((0, 0), (0, HEAD_SLAB - per_head), (0, 0))).reshape(MLA_HEADS * HEAD_SLAB, k)


def _unslab_rows(w, per_head):
    k = w.shape[1]
    return w.reshape(MLA_HEADS, HEAD_SLAB, k)[:, :per_head].reshape(MLA_HEADS * per_head, k)


def meta_grad(dh0, nb, lp):
    d = dh0.shape[1]
    ns = d // N_DEV
    per_seq = lp // N_META

    def body(x_ref, o_ref):
        x = x_ref[...]
        for j in range(N_DEV):
            _accumulate(o_ref.at[j], x[:, j * ns:(j + 1) * ns], pl.program_id(0) == 0)

    return pl.pallas_call(
        body, name="meta_grad", grid=(nb,),
        in_specs=[pl.BlockSpec((N_META, d), lambda b: (b * per_seq + PAD // N_META, 0))],
        out_specs=pl.BlockSpec((N_DEV, N_META, ns), lambda b: (0, 0, 0)),
        out_shape=jax.ShapeDtypeStruct((N_DEV, N_META, ns), F32),
        compiler_params=_params(("arbitrary",)))(dh0)


VECTORS = [("ffn1_norm", 1024), ("mix_norm", 1024), ("q_latent_norm", 384), ("kv_latent_norm", 256),
           ("q_head_norm", 192), ("k_head_norm", 192), ("conv_b", 512), ("gate_a_b", 512), ("gate_x_b", 512),
           ("lru_lambda", 512), ("attn_out_norm", 512), ("lru_out_norm", 512), ("ffn2_norm", 1024),
           ("final_norm", 1024)]
VEC_ROWS = 16
LOSS_ROW = len(VECTORS)
GATES = ["gate_a_w", "gate_x_w"]


def pack_vectors(grads, loss):
    def body(*refs):
        o_ref = refs[-1]
        o_ref[...] = jnp.zeros_like(o_ref)
        for t, (ref, (_, cnt)) in enumerate(zip(refs[:-2], VECTORS)):
            o_ref[t:t + 1, :cnt] = ref[:, :cnt]
        o_ref[LOSS_ROW:LOSS_ROW + 1, :LANES] = refs[-2][...]

    return pl.pallas_call(body, name="pack_vectors", out_shape=jax.ShapeDtypeStruct((VEC_ROWS, D_MODEL), F32),
                          compiler_params=_params(None))(*[grads[name] for name, _ in VECTORS], loss)


def _adamw_update(w, g, m, v):
    c1 = 1.0 / (1.0 - ADAM_B1 ** ADAM_STEP)
    c2 = 1.0 / (1.0 - ADAM_B2 ** ADAM_STEP)
    mn = ADAM_B1 * m + (1.0 - ADAM_B1) * g
    vn = ADAM_B2 * v + (1.0 - ADAM_B2) * (g * g)
    delta = -ADAM_LR * ((mn * c1) / (jnp.sqrt(vn * c2) + ADAM_EPS) + ADAM_WD * w)
    return delta, mn, vn


def _sum_slots(ref, index=()):
    acc = ref[(0,) + index].astype(F32)
    for s in range(1, N_DEV):
        acc = acc + ref[(s,) + index].astype(F32)
    return acc


def adamw_sharded(r, w, m, v, name):
    rows, cols = w.shape
    tr = _tile(rows, 256, 16) if rows % 16 == 0 else rows

    def body(r_ref, w_ref, m_ref, v_ref, g_ref, d_ref, mo_ref, vo_ref):
        g = _sum_slots(r_ref)
        g_ref[...] = g
        d_ref[...], mo_ref[...], vo_ref[...] = _adamw_update(w_ref[...], g, m_ref[...], v_ref[...])

    spec = pl.BlockSpec((tr, cols), lambda i: (i, 0))
    shape = jax.ShapeDtypeStruct((rows, cols), F32)
    return pl.pallas_call(
        body, name=name, grid=(rows // tr,),
        in_specs=[pl.BlockSpec((N_DEV, tr, cols), lambda i: (0, i, 0))] + [spec] * 3,
        out_specs=[spec] * 4, out_shape=[shape] * 4,
        compiler_params=_params(("parallel",)),
    )(r, w, m, v)


def adamw_small(r_vec, r_gates, w, m, v):
    nt = len(VECTORS) + len(GATES)

    def body(*refs):
        rv_ref = refs[0]
        rg_refs = refs[1:1 + len(GATES)]
        base = 1 + len(GATES)
        w_refs, m_refs, v_refs = (refs[base + i * nt:base + (i + 1) * nt] for i in range(3))
        outs = refs[base + 3 * nt:]
        g_o, d_o, m_o, v_o = (outs[i * nt:(i + 1) * nt] for i in range(4))
        outs[4 * nt][...] = _sum_slots(rv_ref, (slice(LOSS_ROW, LOSS_ROW + 1), slice(0, LANES)))
        for t in range(nt):
            if t < len(VECTORS):
                cnt = VECTORS[t][1]
                g = _sum_slots(rv_ref, (slice(t, t + 1), slice(0, cnt)))
            else:
                g = _sum_slots(rg_refs[t - len(VECTORS)])
            g_o[t][...] = g
            d_o[t][...], m_o[t][...], v_o[t][...] = _adamw_update(w_refs[t][...], g, m_refs[t][...], v_refs[t][...])

    shapes = [jax.ShapeDtypeStruct(a.shape, F32) for a in w]
    res = pl.pallas_call(body, name="adamw_small", out_shape=shapes * 4 + [jax.ShapeDtypeStruct((1, LANES), F32)],
                         compiler_params=_params(None))(r_vec, *r_gates, *w, *m, *v)
    return [res[i * nt:(i + 1) * nt] for i in range(4)], res[4 * nt]


def _block_diag(w):
    nb, n, _ = w.shape
    eye = jnp.eye(nb, dtype=w.dtype)
    return (eye[:, None, :, None] * w[:, :, None, :]).reshape(nb * n, nb * n)


def _two_d(a):
    if a.ndim == 3:
        return a.reshape(a.shape[1], a.shape[2])
    if a.ndim == 4:
        return a.reshape(a.shape[1] * a.shape[2], a.shape[3])
    return a


_WEIGHT_NAMES = ['meta_tokens', 'ffn1_norm', 'ffn1_w_gate', 'ffn1_w_up', 'ffn1_w_down', 'mix_norm', 'w_in',
                 'q_latent_norm', 'w_uq', 'kv_latent_norm', 'w_uk', 'w_uv', 'q_head_norm', 'k_head_norm', 'conv_w',
                 'conv_b', 'gate_a_w', 'gate_a_b', 'gate_x_w', 'gate_x_b', 'lru_lambda', 'attn_out_norm',
                 'lru_out_norm', 'w_out', 'ffn2_norm', 'ffn2_w_gate', 'ffn2_w_up', 'ffn2_w_down', 'final_norm']


COLUMN_SHARDED = ("ffn1_w_gate", "ffn1_w_up", "ffn2_w_gate", "ffn2_w_up", "w_in", "w_uq", "w_uk", "w_uv")


def train_step(x, tgt, w, m, v):
    nb, seq, d = x.shape
    lp = PAD + N_META + seq
    n = nb * lp

    def local(a, name):
        a = _two_d(a)
        return a.T if name in COLUMN_SHARDED else a

    sh = {name: local(w[name], name) for name in _WEIGHT_NAMES}
    m2 = {name: local(m[name], name) for name in _WEIGHT_NAMES}
    v2 = {name: local(v[name], name) for name in _WEIGHT_NAMES}

    def b16(name):
        return sh[name].astype(BF16)

    out = {}

    def update(name, landed):
        out[name] = adamw_sharded(landed, sh[name], m2[name], v2[name], "adamw_" + name)

    g_meta, g_conv, g_wg1, g_wu1 = exchange(
        [sh["meta_tokens"], sh["conv_w"], b16("ffn1_w_gate"), b16("ffn1_w_up")], ["spread"] * 2 + ["gather"] * 2,
        "gather_ffn1")
    wg1, wu1 = g_wg1.reshape(D_FF, d), g_wu1.reshape(D_FF, d)
    meta = assemble_cols(g_meta, "assemble_meta")
    conv_w = assemble_cols(g_conv, "assemble_conv")

    front = jnp.concatenate([jnp.zeros((PAD, d), F32), meta], axis=0)
    h0 = jnp.concatenate([jnp.broadcast_to(front[None], (nb, FIRST_FRAME, d)), x], axis=1).reshape(n, d)
    tgt_p = jnp.concatenate([jnp.zeros((nb, FIRST_FRAME, d), F32), tgt], axis=1).reshape(n, d)
    tables = _rope_tables(lp)
    zero_tail = jnp.zeros((1, HEAD_SLAB - D_QK), F32)
    gqh = jnp.concatenate([sh["q_head_norm"], zero_tail], axis=1)
    gkh = jnp.concatenate([sh["k_head_norm"], zero_tail], axis=1)
    wa = _block_diag(w["gate_a_w"][0]).astype(BF16)
    wx = _block_diag(w["gate_x_w"][0]).astype(BF16)

    (u1, a1, b1, s1), (g_wd1, g_in) = ffn_up(h0, sh["ffn1_norm"], wg1, wu1, "ffn1_up",
                                             comm=([b16("ffn1_w_down"), b16("w_in")], ["gather"] * 2))
    wd1 = g_wd1.reshape(D_FF, d)
    mla_rows = MLA_IN - D_ROPE
    w_in = g_in.reshape(mla_rows + 2 * LRU_WIDTH, d)
    wm = jnp.concatenate([w_in[:mla_rows], jnp.zeros((D_ROPE, d), BF16)], axis=0)
    wl = w_in[mla_rows:]
    (h1, u2, zm, zl), (g_uq, g_uk, g_uv, g_out) = ffn_down_inproj(
        h0, s1, wd1, sh["mix_norm"], wm, wl, "ffn1_down_inproj",
        comm=([b16("w_uq"), b16("w_uk"), b16("w_uv"), b16("w_out")], ["gather"] * 4))
    wuq = _slab_rows(g_uq.reshape(MLA_HEADS * D_QK, Q_RANK), D_QK)
    wuk = _slab_rows(g_uk.reshape(MLA_HEADS * D_NOPE, KV_RANK), D_NOPE)
    wuv = g_uv.reshape(MLA_HEADS * D_V, KV_RANK)
    w_out = g_out.reshape(d, d)

    q, k, vv, qn, cn = mla_prep_fwd(zm, sh["q_latent_norm"], sh["kv_latent_norm"], wuq, wuk, wuv, gqh, gkh, tables, lp)
    (y_mla, lse), (g_wu2, g_wd2) = attn_fwd(
        q, k, vv, nb, lp, comm=([b16("ffn2_w_up"), b16("ffn2_w_down")], ["gather"] * 2))
    (y_lru, hs), (g_wg2,) = lru_fwd(zl, conv_w, sh["conv_b"], wa, wx, sh["gate_a_b"], sh["gate_x_b"], sh["lru_lambda"],
                                    nb, lp, comm=([b16("ffn2_w_gate")], ["gather"]))
    wg2, wu2, wd2 = (g.reshape(D_FF, d) for g in (g_wg2, g_wu2, g_wd2))
    h2, yn = outproj_fwd(h1, y_mla, y_lru, sh["attn_out_norm"], sh["lru_out_norm"], w_out)
    dh3, u3, a3, b3, loss, g_final = ffn_fwd_loss(h2, sh["ffn2_norm"], wg2, wu2, wd2, sh["final_norm"], tgt_p, lp,
                                                  "ffn2_fwd_loss")

    vec = {"final_norm": g_final}
    (dh2, da3, db3, sh3, vec["ffn2_norm"]), _ = ffn_bwd_act(dh3, h2, sh["ffn2_norm"], a3, b3, wg2, wu2, wd2, "ffn2_bwd")
    ff_shards = (N_DEV, D_FF // N_DEV, d)
    dwg2 = tn_matmul(da3, u3, "ffn2_dwg", "bf16").reshape(ff_shards)
    dwu2 = tn_matmul(db3, u3, "ffn2_dwu", "bf16").reshape(ff_shards)
    dwd2 = tn_matmul(sh3, dh3, "ffn2_dwd", "bf16").reshape(ff_shards)

    dy_mla, dy_lru, vec["attn_out_norm"], vec["lru_out_norm"] = outproj_bwd(
        dh2, y_mla, y_lru, sh["attn_out_norm"], sh["lru_out_norm"], w_out)
    dw_out = tn_matmul(yn, dh2, "dw_out", "bf16").reshape(N_DEV, d // N_DEV, d)
    (du, dgate, dconv, vec["conv_b"], vec["gate_a_b"], vec["gate_x_b"], vec["lru_lambda"], dga, dgx), landed = lru_bwd(
        zl, hs, dy_lru, conv_w, sh["conv_b"], wa, wx, sh["gate_a_b"], sh["gate_x_b"], sh["lru_lambda"], nb, lp,
        comm=([dwg2, dw_out], ["scatter"] * 2))
    update("ffn2_w_gate", landed[0])
    update("w_out", landed[1])

    (dq, dk, dv), (r_wu2,) = attn_bwd(q, k, vv, y_mla, dy_mla, lse, nb, lp, comm=([dwu2], ["scatter"]))
    update("ffn2_w_up", r_wu2)

    (dzm, dqr, dkr, vec["q_latent_norm"], vec["kv_latent_norm"], vec["q_head_norm"], vec["k_head_norm"]), (r_wd2,) = (
        mla_prep_bwd(dq, dk, dv, zm, qn, cn, sh["q_latent_norm"], sh["kv_latent_norm"], wuq, wuk, wuv, gqh, gkh,
                     tables, lp, comm=([dwd2], ["scatter"])))
    update("ffn2_w_down", r_wd2)
    dwuq = _unslab_rows(tn_matmul(dqr, qn, "dw_uq", "bf16"), D_QK).reshape(N_DEV, -1, Q_RANK)
    dwuk = _unslab_rows(tn_matmul(dkr, cn, "dw_uk", "bf16"), D_NOPE).reshape(N_DEV, -1, KV_RANK)
    dwuv = tn_matmul(dv, cn, "dw_uv", "bf16").reshape(N_DEV, -1, KV_RANK)
    (dh1, vec["mix_norm"]), landed = inproj_bwd(dzm, du, dgate, dh2, h1, sh["mix_norm"], wm, wl,
                                                comm=([dwuq, dwuk, dwuv], ["scatter"] * 3))
    for name, r in zip(("w_uq", "w_uk", "w_uv"), landed):
        update(name, r)
    dw_in = jnp.concatenate([tn_matmul(dzm, u2, "dw_in_mla", "bf16")[:mla_rows], tn_matmul(du, u2, "dw_in_u", "bf16"),
                             tn_matmul(dgate, u2, "dw_in_gate", "bf16")], axis=0).reshape(N_DEV, -1, d)

    dwd1 = tn_matmul(s1, dh1, "ffn1_dwd", "bf16").reshape(ff_shards)
    (dh0, da1, db1, vec["ffn1_norm"]), landed = ffn_bwd_act(
        dh1, h0, sh["ffn1_norm"], a1, b1, wg1, wu1, wd1, "ffn1_bwd", emit_sh=False,
        comm=([dw_in, split_cols(dconv, "split_conv"), dwd1], ["scatter"] * 3))
    for name, r in zip(("w_in", "conv_w", "ffn1_w_down"), landed):
        update(name, r)

    dwg1 = tn_matmul(da1, u1, "ffn1_dwg", "bf16").reshape(ff_shards)
    dwu1, (r_wg1,) = tn_matmul(db1, u1, "ffn1_dwu", "bf16", comm=([dwg1], ["scatter"]))
    dmeta = meta_grad(dh0, nb, lp)
    gates = [dga.reshape(LRU_WIDTH, LRU_BLOCK), dgx.reshape(LRU_WIDTH, LRU_BLOCK)]
    r_vec, r_ga, r_gx, r_meta, r_wu1 = exchange(
        [pack_vectors(vec, loss)] + gates + [dmeta, dwu1.reshape(ff_shards)], ["spread"] * 3 + ["scatter"] * 2,
        "exchange_last")
    update("ffn1_w_gate", r_wg1)
    update("ffn1_w_up", r_wu1)
    update("meta_tokens", r_meta)

    small = [name for name, _ in VECTORS] + GATES
    res, total_loss = adamw_small(r_vec, [r_ga, r_gx], [sh[nm] for nm in small], [m2[nm] for nm in small],
                                  [v2[nm] for nm in small])
    for i, name in enumerate(small):
        out[name] = [res[j][i] for j in range(4)]

    grad_x = dh0.reshape(nb, lp, d)[:, FIRST_FRAME:]
    loss = total_loss[0, 0]

    def as_given(a, name):
        return (a.T if name in COLUMN_SHARDED else a).reshape(w[name].shape)

    cols = [[as_given(out[name][j], name) for name in _WEIGHT_NAMES] for j in range(4)]
    return (loss, grad_x, *cols[0], *cols[1], *cols[2], *cols[3])


def kernel(x, meta_tokens, ffn1_norm, ffn1_w_gate, ffn1_w_up, ffn1_w_down, mix_norm, w_in, q_latent_norm, w_uq, kv_latent_norm, w_uk, w_uv, q_head_norm, k_head_norm, conv_w, conv_b, gate_a_w, gate_a_b, gate_x_w, gate_x_b, lru_lambda, attn_out_norm, lru_out_norm, w_out, ffn2_norm, ffn2_w_gate, ffn2_w_up, ffn2_w_down, final_norm, loss_target, m_meta_tokens, m_ffn1_norm, m_ffn1_w_gate, m_ffn1_w_up, m_ffn1_w_down, m_mix_norm, m_w_in, m_q_latent_norm, m_w_uq, m_kv_latent_norm, m_w_uk, m_w_uv, m_q_head_norm, m_k_head_norm, m_conv_w, m_conv_b, m_gate_a_w, m_gate_a_b, m_gate_x_w, m_gate_x_b, m_lru_lambda, m_attn_out_norm, m_lru_out_norm, m_w_out, m_ffn2_norm, m_ffn2_w_gate, m_ffn2_w_up, m_ffn2_w_down, m_final_norm, v_meta_tokens, v_ffn1_norm, v_ffn1_w_gate, v_ffn1_w_up, v_ffn1_w_down, v_mix_norm, v_w_in, v_q_latent_norm, v_w_uq, v_kv_latent_norm, v_w_uk, v_w_uv, v_q_head_norm, v_k_head_norm, v_conv_w, v_conv_b, v_gate_a_w, v_gate_a_b, v_gate_x_w, v_gate_x_b, v_lru_lambda, v_attn_out_norm, v_lru_out_norm, v_w_out, v_ffn2_norm, v_ffn2_w_gate, v_ffn2_w_up, v_ffn2_w_down, v_final_norm):
    args = locals()
    w = {name: args[name] for name in _WEIGHT_NAMES}
    m = {name: args["m_" + name] for name in _WEIGHT_NAMES}
    v = {name: args["v_" + name] for name in _WEIGHT_NAMES}
    return train_step(x, loss_target, w, m, v)
```

```python
import math

import jax
import jax.numpy as jnp
from jax import lax
from jax.experimental import pallas as pl
from jax.experimental.pallas import tpu as pltpu

F32 = jnp.float32
BF16 = jnp.bfloat16

D_MODEL = 1024
CHUNK = 64
CHUNK_SHIFT = 6
N_META = 16
PAD = CHUNK - N_META
FIRST_FRAME = PAD + N_META
MLA_HEADS = 4
D_NOPE = 128
D_ROPE = 64
D_QK = D_NOPE + D_ROPE
D_V = 128
HEAD_SLAB = 256
KV_RANK = 256
Q_RANK = 384
ROPE_THETA = 10000.0
LRU_WIDTH = 512
LRU_BLOCKS = 8
LRU_BLOCK = 64
LRU_TILE = 128
CONV_W = 4
C_RGLRU = 8.0
D_FF = 2816
MLA_IN = 768
EPS = 1e-6
NEG_INF = -1e30
N_DEV = 8
LANES = 128
VMEM_LIMIT = 52 * 1024 * 1024
ATTN_HEADS_PER_STEP = 4
TN_ROWS = 4224
TN_X_BYTES = 12 * 1024 * 1024
TN_Y_BYTES = 9 * 1024 * 1024

ADAM_LR = 0.001
ADAM_B1 = 0.9
ADAM_B2 = 0.999
ADAM_EPS = 1e-08
ADAM_WD = 0.01
ADAM_STEP = 10

VMEM_WHOLE = pl.BlockSpec(memory_space=pltpu.VMEM)
HBM_WHOLE = pl.BlockSpec(memory_space=pl.ANY)


def _params(sems):
    if sems is None:
        return pltpu.CompilerParams(vmem_limit_bytes=VMEM_LIMIT)
    return pltpu.CompilerParams(dimension_semantics=sems, vmem_limit_bytes=VMEM_LIMIT)


def _tile(n, cap, mult=16):
    best = None
    for t in range(mult, min(n, cap) + 1, mult):
        if n % t == 0:
            best = t
    assert best is not None, (n, cap, mult)
    return best


def _row(tm, d):
    return pl.BlockSpec((tm, d), lambda i: (i, 0))


def _fixed(shape):
    return pl.BlockSpec(shape, lambda i: (0,) * len(shape))


def _mesh_position():
    return lax.axis_index("x"), lax.axis_index("y"), lax.axis_index("c")


def _flat_index(x, y, c):
    return 4 * x + 2 * y + c


def _peers(x, y, c):
    out = []
    for k in range(1, N_DEV):
        fx, fy, fc = (k >> 2) & 1, (k >> 1) & 1, k & 1
        out.append((1 - x if fx else x, 1 - y if fy else y, 1 - c if fc else c))
    return out


def _comm_out_shapes(srcs, modes):
    return [jax.ShapeDtypeStruct((N_DEV,) + s.shape if md == "gather" else s.shape, s.dtype)
            for s, md in zip(srcs, modes)]


def _comm_scratch(n):
    per_peer = n * (N_DEV - 1)
    return [pltpu.SemaphoreType.DMA((per_peer,)), pltpu.SemaphoreType.DMA((per_peer,)), pltpu.SemaphoreType.DMA((n,))]


class _Copies:
    def __init__(self, own, first, relay):
        self.own, self.first, self.relay = own, first, relay

    def start(self):
        for cp in self.own + self.first:
            cp.start()

    def forward(self):
        for arrival, onward in self.relay:
            arrival.wait_recv()
            onward.start()

    def finish(self):
        arrivals = [a for a, _ in self.relay]
        onward = [f for _, f in self.relay]
        for cp in self.first + onward:
            if not any(cp is a for a in arrivals):
                cp.wait_recv()
        for cp in self.first + onward:
            cp.wait_send()
        for cp in self.own:
            cp.wait()


def _comm_copies(src_refs, dst_refs, modes, send, recv, local):
    x, y, c = _mesh_position()
    me = _flat_index(x, y, c)
    n = len(modes)
    sibling = (x, y, 1 - c)
    chips = [(1 - x, y), (x, 1 - y), (1 - x, 1 - y)]

    def remote(src, dst, k, t, to):
        return pltpu.make_async_remote_copy(src_ref=src, dst_ref=dst, send_sem=send.at[k * n + t],
                                            recv_sem=recv.at[k * n + t], device_id=to,
                                            device_id_type=pl.DeviceIdType.MESH)

    own, first, relay = [], [], []
    for t, (src, dst, md) in enumerate(zip(src_refs, dst_refs, modes)):
        if md == "scatter":
            own.append(pltpu.make_async_copy(src.at[me], dst.at[me], local.at[t]))
            for k, peer in enumerate(_peers(x, y, c)):
                first.append(remote(src.at[_flat_index(*peer)], dst.at[me], k, t, peer))
        else:
            own.append(pltpu.make_async_copy(src, dst.at[me], local.at[t]))
            first.append(remote(src, dst.at[me], 0, t, sibling))
            for j, chip in enumerate(chips):
                arrival = remote(src, dst.at[me], 1 + j, t, (*chip, c))
                landed = dst.at[_flat_index(*chip, c)]
                first.append(arrival)
                relay.append((arrival, remote(landed, landed, 4 + j, t, sibling)))
    return _Copies(own, first, relay)


def _hosted(body, n_in, n_out, modes, grid):
    t = len(modes)
    total = math.prod(grid)

    def wrapped(*refs):
        ins, csrc = refs[:n_in], refs[n_in:n_in + t]
        outs = refs[n_in + t:n_in + t + n_out]
        cdst = refs[n_in + t + n_out:n_in + 2 * t + n_out]
        scratch = refs[n_in + 2 * t + n_out:-3]
        copies = _comm_copies(csrc, cdst, modes, *refs[-3:])
        step = pl.program_id(0)
        for axis in range(1, len(grid)):
            step = step * grid[axis] + pl.program_id(axis)

        @pl.when(step == 0)
        def _():
            copies.start()

        body(*ins, *outs, *scratch)

        @pl.when(step == (total * 4) // 5)
        def _():
            copies.forward()

        @pl.when(step == total - 1)
        def _():
            copies.finish()

    return wrapped


def _call(body, name, grid, in_specs, out_specs, out_shape, sems, args, scratch=(), comm=None):
    if comm is None:
        outs = pl.pallas_call(body, name=name, grid=grid, in_specs=in_specs, out_specs=out_specs, out_shape=out_shape,
                              scratch_shapes=list(scratch), compiler_params=_params(sems))(*args)
        return outs, []
    srcs, modes = comm
    n = len(modes)
    res = pl.pallas_call(
        _hosted(body, len(in_specs), len(out_specs), modes, grid), name=name, grid=grid,
        in_specs=list(in_specs) + [HBM_WHOLE] * n, out_specs=list(out_specs) + [HBM_WHOLE] * n,
        out_shape=list(out_shape) + _comm_out_shapes(srcs, modes),
        scratch_shapes=list(scratch) + _comm_scratch(n),
        compiler_params=_params(("arbitrary",) * len(grid)))(*args, *srcs)
    return res[:len(out_specs)], res[len(out_specs):]


def exchange(srcs, modes, name):
    n = len(modes)

    def body(*refs):
        copies = _comm_copies(refs[:n], refs[n:2 * n], modes, *refs[2 * n:])
        copies.start()
        copies.forward()
        copies.finish()

    return pl.pallas_call(body, name=name, in_specs=[HBM_WHOLE] * n, out_specs=[HBM_WHOLE] * n,
                          out_shape=_comm_out_shapes(srcs, modes), scratch_shapes=_comm_scratch(n))(*srcs)


def _nn(a, b):
    return jnp.dot(a, b, preferred_element_type=F32)


def _nt(a, b):
    return lax.dot_general(a, b, (((1,), (1,)), ((), ())), preferred_element_type=F32)


def _tn(a, b):
    return lax.dot_general(a, b, (((0,), (0,)), ((), ())), preferred_element_type=F32)


def _sig(x):
    return 1.0 / (1.0 + jnp.exp(-x))


def _rms_r(x, n=None):
    n = x.shape[-1] if n is None else n
    return lax.rsqrt(jnp.sum(x * x, axis=-1, keepdims=True) * (1.0 / n) + EPS)


def _rms_bwd(x, r, g, dy, n=None):
    n = x.shape[-1] if n is None else n
    xhat = x * r
    dxhat = dy * g
    dx = r * (dxhat - xhat * (jnp.sum(dxhat * xhat, axis=-1, keepdims=True) * (1.0 / n)))
    return dx, jnp.sum(dy * xhat, axis=0, keepdims=True)


def _accumulate(ref, val, first):
    @pl.when(first)
    def _():
        ref[...] = val

    @pl.when(jnp.logical_not(first))
    def _():
        ref[...] += val


_GELU_C = math.sqrt(2.0 / math.pi)


def _gelu_and_grad(x):
    inner = _GELU_C * (x + 0.044715 * x * x * x)
    t = jnp.tanh(inner)
    gelu = 0.5 * x * (1.0 + t)
    dgelu = 0.5 * (1.0 + t) + 0.5 * x * (1.0 - t * t) * _GELU_C * (1.0 + 3.0 * 0.044715 * x * x)
    return gelu, dgelu


def _log1p_small(t):
    return jnp.where(t < 1e-3, t * (1.0 - t * (0.5 - t * (1.0 / 3.0))), jnp.log(1.0 + t))


def _softplus(x):
    return jnp.maximum(x, 0.0) + _log1p_small(jnp.exp(-jnp.abs(x)))


def _sig_tanh(x):
    return 0.5 + 0.5 * jnp.tanh(0.5 * x)


def _ff_chunks(f):
    return 2 if (f // 2) % LANES == 0 else 1


def _swiglu_half(x, g_ref, wg_ref, wu_ref, wd_ref, a_ref, b_ref, fc):
    f = wg_ref.shape[0]
    u = (x * _rms_r(x) * g_ref[...]).astype(BF16)
    acc = jnp.zeros(x.shape, F32)
    for c in range(f // fc):
        cols = slice(c * fc, (c + 1) * fc)
        a = _nt(u, wg_ref[cols, :])
        b = _nt(u, wu_ref[cols, :])
        s = (a * _sig(a) * b).astype(BF16)
        acc = acc + _nn(s, wd_ref[cols, :])
        a_ref[:, cols] = a.astype(BF16)
        b_ref[:, cols] = b.astype(BF16)
    return x + 0.5 * acc, u


def ffn_up(h, g, wg, wu, name, comm=None):
    n, d = h.shape
    f = wg.shape[0]
    tm = _tile(n, 528)
    fc = 2 * LANES if f % (2 * LANES) == 0 else f

    def body(h_ref, g_ref, wg_ref, wu_ref, u_ref, a_ref, b_ref, s_ref):
        x = h_ref[...]
        u = (x * _rms_r(x) * g_ref[...]).astype(BF16)
        u_ref[...] = u
        for c in range(f // fc):
            cols = slice(c * fc, (c + 1) * fc)
            a = _nt(u, wg_ref[cols, :])
            b = _nt(u, wu_ref[cols, :])
            a_ref[:, cols] = a.astype(BF16)
            b_ref[:, cols] = b.astype(BF16)
            s_ref[:, cols] = (0.5 * (a * _sig(a) * b)).astype(BF16)

    wide = jax.ShapeDtypeStruct((n, f), BF16)
    return _call(
        body, name, (n // tm,),
        [_row(tm, d), _fixed((1, d)), VMEM_WHOLE, VMEM_WHOLE],
        [_row(tm, d), _row(tm, f), _row(tm, f), _row(tm, f)],
        [jax.ShapeDtypeStruct((n, d), BF16), wide, wide, wide],
        ("parallel",), (h, g, wg, wu), comm=comm)


def ffn_down_inproj(h, s, wd, g, wm, wl, name, comm=None):
    n, d = h.shape
    f = wd.shape[0]
    tm = _tile(n, 528)

    def body(h_ref, s_ref, wd_ref, g_ref, wm_ref, wl_ref, ho_ref, u_ref, zm_ref, zl_ref):
        x = h_ref[...] + _nn(s_ref[...], wd_ref[...])
        ho_ref[...] = x
        u = (x * _rms_r(x) * g_ref[...]).astype(BF16)
        u_ref[...] = u
        zm_ref[...] = _nt(u, wm_ref[...])
        zl_ref[...] = _nt(u, wl_ref[...])

    return _call(
        body, name, (n // tm,),
        [_row(tm, d), _row(tm, f), VMEM_WHOLE, _fixed((1, d)), VMEM_WHOLE, VMEM_WHOLE],
        [_row(tm, d), _row(tm, d), _row(tm, MLA_IN), _row(tm, 2 * LRU_WIDTH)],
        [jax.ShapeDtypeStruct((n, d), F32), jax.ShapeDtypeStruct((n, d), BF16),
         jax.ShapeDtypeStruct((n, MLA_IN), F32), jax.ShapeDtypeStruct((n, 2 * LRU_WIDTH), F32)],
        ("parallel",), (h, s, wd, g, wm, wl), comm=comm)


def ffn_fwd_loss(h, g, wg, wu, wd, g_final, tgt, lp, name):
    n, d = h.shape
    f = wg.shape[0]
    tm = _tile(lp, 528)
    per_seq = lp // tm
    fc = 2 * LANES if f % (2 * LANES) == 0 else f

    def body(h_ref, g_ref, wg_ref, wu_ref, wd_ref, gf_ref, t_ref, dh_ref, u_ref, a_ref, b_ref, loss_ref, dgf_ref):
        i = pl.program_id(0)
        y, u_ref[...] = _swiglu_half(h_ref[...], g_ref, wg_ref, wu_ref, wd_ref, a_ref, b_ref, fc)
        dh_ref[...], part, dg = _loss_and_grad(y, gf_ref[...], t_ref[...], (i % per_seq) * tm)
        _accumulate(loss_ref, jnp.broadcast_to(part, (1, LANES)), i == 0)
        _accumulate(dgf_ref, dg, i == 0)

    outs, _ = _call(
        body, name, (n // tm,),
        [_row(tm, d), _fixed((1, d)), VMEM_WHOLE, VMEM_WHOLE, VMEM_WHOLE, _fixed((1, d)), _row(tm, d)],
        [_row(tm, d), _row(tm, d), _row(tm, f), _row(tm, f), _fixed((1, LANES)), _fixed((1, d))],
        [jax.ShapeDtypeStruct((n, d), F32), jax.ShapeDtypeStruct((n, d), BF16),
         jax.ShapeDtypeStruct((n, f), BF16), jax.ShapeDtypeStruct((n, f), BF16),
         jax.ShapeDtypeStruct((1, LANES), F32), jax.ShapeDtypeStruct((1, d), F32)],
        ("arbitrary",), (h, g, wg, wu, wd, g_final, tgt))
    return outs


def ffn_bwd_act(dh, h, g, a, b, wg, wu, wd, name, comm=None, emit_sh=True):
    n, d = h.shape
    f = wg.shape[0]
    tm = _tile(n, 352 if emit_sh else 384)
    nc = _ff_chunks(f)
    fc = f // nc

    def body(dh_ref, h_ref, g_ref, a_ref, b_ref, wg_ref, wu_ref, wd_ref, dhi_ref, da_ref, db_ref, *rest):
        dg_ref = rest[-1]
        x = h_ref[...]
        dy = dh_ref[...]
        r = _rms_r(x)
        dhh = (0.5 * dy).astype(BF16)
        du = jnp.zeros((tm, d), F32)
        for c in range(nc):
            cols = slice(c * fc, (c + 1) * fc)
            ds = _nt(dhh, wd_ref[cols, :])
            av = a_ref[:, cols].astype(F32)
            bv = b_ref[:, cols].astype(F32)
            sg = _sig(av)
            sil = av * sg
            da = (ds * bv * (sg * (1.0 + av * (1.0 - sg)))).astype(BF16)
            db = (ds * sil).astype(BF16)
            da_ref[:, cols] = da
            db_ref[:, cols] = db
            if emit_sh:
                rest[0][:, cols] = (0.5 * sil * bv).astype(BF16)
            du = du + _nn(da, wg_ref[cols, :]) + _nn(db, wu_ref[cols, :])
        dx, dg = _rms_bwd(x, r, g_ref[...], du)
        dhi_ref[...] = dy + dx
        _accumulate(dg_ref, dg, pl.program_id(0) == 0)

    wide = [jax.ShapeDtypeStruct((n, f), BF16)] * (3 if emit_sh else 2)
    return _call(
        body, name, (n // tm,),
        [_row(tm, d), _row(tm, d), _fixed((1, d)), _row(tm, f), _row(tm, f), VMEM_WHOLE, VMEM_WHOLE, VMEM_WHOLE],
        [_row(tm, d)] + [_row(tm, f)] * len(wide) + [_fixed((1, d))],
        [jax.ShapeDtypeStruct((n, d), F32)] + wide + [jax.ShapeDtypeStruct((1, d), F32)],
        ("arbitrary",), (dh, h, g, a, b, wg, wu, wd), comm=comm)


def tn_matmul(x, y, name, out="f32", comm=None):
    n, k = x.shape
    m = y.shape[1]
    tm = _tile(n, TN_ROWS)
    kc, mc = k, (512 if m % 512 == 0 else m)
    while tm * kc * x.dtype.itemsize > TN_X_BYTES and kc % (2 * LANES) == 0:
        kc //= 2
    while tm * mc * y.dtype.itemsize > TN_Y_BYTES and mc % (2 * LANES) == 0:
        mc //= 2
    steps = n // tm

    def body(x_ref, y_ref, o_ref, *acc):
        i = pl.program_id(2)
        part = _tn(x_ref[...].astype(BF16), y_ref[...].astype(BF16))
        if steps == 1:
            o_ref[...] = part.astype(o_ref.dtype)
        elif out == "f32":
            _accumulate(o_ref, part, i == 0)
        else:
            _accumulate(acc[0], part, i == 0)

            @pl.when(i == steps - 1)
            def _():
                o_ref[...] = acc[0][...].astype(BF16)

    out_shape = jax.ShapeDtypeStruct((k, m), F32 if out == "f32" else BF16)
    (res,), landed = _call(
        body, name, (k // kc, m // mc, steps),
        [pl.BlockSpec((tm, kc), lambda a, b, i: (i, a)), pl.BlockSpec((tm, mc), lambda a, b, i: (i, b))],
        [pl.BlockSpec((kc, mc), lambda a, b, i: (a, b))], [out_shape], ("parallel", "parallel", "arbitrary"), (x, y),
        scratch=[pltpu.VMEM((kc, mc), F32)] if (out == "bf16" and steps > 1) else [], comm=comm)
    return (res, landed) if comm is not None else res


def inproj_bwd(dzm, du, dgate, dh2, h, g, wm, wl, comm=None):
    n, d = h.shape
    tm = _tile(n, 352)

    def body(dzm_ref, du_ref, dgt_ref, dh2_ref, h_ref, g_ref, wm_ref, wl_ref, dh_ref, dg_ref):
        x = h_ref[...]
        dun = (_nn(dzm_ref[...].astype(BF16), wm_ref[...])
               + _nn(du_ref[...].astype(BF16), wl_ref[:LRU_WIDTH, :])
               + _nn(dgt_ref[...].astype(BF16), wl_ref[LRU_WIDTH:, :]))
        dx, dg = _rms_bwd(x, _rms_r(x), g_ref[...], dun)
        dh_ref[...] = dh2_ref[...] + dx
        _accumulate(dg_ref, dg, pl.program_id(0) == 0)

    return _call(
        body, "inproj_bwd", (n // tm,),
        [_row(tm, MLA_IN), _row(tm, LRU_WIDTH), _row(tm, LRU_WIDTH), _row(tm, d), _row(tm, d),
         _fixed((1, d)), VMEM_WHOLE, VMEM_WHOLE],
        [_row(tm, d), _fixed((1, d))],
        [jax.ShapeDtypeStruct((n, d), F32), jax.ShapeDtypeStruct((1, d), F32)],
        ("arbitrary",), (dzm, du, dgate, dh2, h, g, wm, wl), comm=comm)


def _rope_tables(lp):
    pos = jnp.arange(lp, dtype=F32) - float(PAD)
    half = D_ROPE // 2
    inv_freq = ROPE_THETA ** (-jnp.arange(0, half, dtype=F32) / half)
    ang = pos[:, None] * inv_freq[None, :]
    cos, sin = jnp.cos(ang), jnp.sin(ang)
    one = jnp.ones((lp, D_NOPE), F32)
    z_nope = jnp.zeros((lp, D_NOPE), F32)
    z_half = jnp.zeros((lp, half), F32)
    z_tail = jnp.zeros((lp, HEAD_SLAB - D_QK), F32)
    cosr = jnp.concatenate([one, cos, cos, z_tail], axis=1)
    sin_up = jnp.concatenate([z_nope, z_half, sin, z_tail], axis=1)
    sin_dn = jnp.concatenate([z_nope, -sin, z_half, z_tail], axis=1)
    return cosr, sin_up, sin_dn


def _rope(x, cosr, sin_up, sin_dn):
    half = D_ROPE // 2
    return x * cosr + pltpu.roll(x, half, axis=1) * sin_up + pltpu.roll(x, HEAD_SLAB - half, axis=1) * sin_dn


def _rope_bwd(dy, cosr, sin_up, sin_dn):
    half = D_ROPE // 2
    return (dy * cosr + pltpu.roll(dy * sin_up, HEAD_SLAB - half, axis=1)
            + pltpu.roll(dy * sin_dn, half, axis=1))


def _k_rope_slab(zm_tile):
    tm = zm_tile.shape[0]
    krp = zm_tile[:, Q_RANK + KV_RANK:MLA_IN]
    return jnp.concatenate([jnp.zeros((tm, D_NOPE), F32), krp], axis=1)


def mla_prep_fwd(zm, gql, gkvl, wuq, wuk, wuv, gqh, gkh, tables, lp):
    n = zm.shape[0]
    tm = _tile(lp, 352)
    per_seq = lp // tm
    width = MLA_HEADS * HEAD_SLAB
    scale = 1.0 / math.sqrt(D_QK)

    def body(zm_ref, gql_ref, gkvl_ref, wuq_ref, wuk_ref, wuv_ref, gqh_ref, gkh_ref,
             cos_ref, up_ref, dn_ref, q_ref, k_ref, v_ref, qn_ref, cn_ref):
        z = zm_ref[...]
        cq = z[:, :Q_RANK]
        ckv = z[:, Q_RANK:Q_RANK + KV_RANK]
        qn = (cq * _rms_r(cq) * gql_ref[...]).astype(BF16)
        cn = (ckv * _rms_r(ckv) * gkvl_ref[...]).astype(BF16)
        qn_ref[...] = qn
        cn_ref[...] = cn
        q_raw = _nt(qn, wuq_ref[...])
        k_raw = _nt(cn, wuk_ref[...])
        v_ref[...] = _nt(cn, wuv_ref[...]).astype(BF16)
        kr_slab = _k_rope_slab(z)
        cosr, sin_up, sin_dn = cos_ref[...], up_ref[...], dn_ref[...]
        for hd in range(MLA_HEADS):
            cols = slice(hd * HEAD_SLAB, (hd + 1) * HEAD_SLAB)
            xq = q_raw[:, cols]
            yq = _rope(xq * _rms_r(xq, D_QK) * gqh_ref[...], cosr, sin_up, sin_dn)
            q_ref[:, cols] = (yq * scale).astype(BF16)
            xk = k_raw[:, cols] + kr_slab
            yk = _rope(xk * _rms_r(xk, D_QK) * gkh_ref[...], cosr, sin_up, sin_dn)
            k_ref[:, cols] = yk.astype(BF16)

    tab = pl.BlockSpec((tm, HEAD_SLAB), lambda i: (i % per_seq, 0))
    return pl.pallas_call(
        body, name="mla_prep_fwd", grid=(n // tm,),
        in_specs=[_row(tm, MLA_IN), _fixed((1, Q_RANK)), _fixed((1, KV_RANK)), VMEM_WHOLE, VMEM_WHOLE, VMEM_WHOLE,
                  _fixed((1, HEAD_SLAB)), _fixed((1, HEAD_SLAB)), tab, tab, tab],
        out_specs=[_row(tm, width), _row(tm, width), _row(tm, MLA_HEADS * D_V), _row(tm, Q_RANK), _row(tm, KV_RANK)],
        out_shape=[jax.ShapeDtypeStruct((n, width), BF16), jax.ShapeDtypeStruct((n, width), BF16),
                   jax.ShapeDtypeStruct((n, MLA_HEADS * D_V), BF16), jax.ShapeDtypeStruct((n, Q_RANK), BF16),
                   jax.ShapeDtypeStruct((n, KV_RANK), BF16)],
        compiler_params=_params(("parallel",)),
    )(zm, gql, gkvl, wuq, wuk, wuv, gqh, gkh, *tables)


def mla_prep_bwd(dq, dk, dv, zm, qn, cn, gql, gkvl, wuq, wuk, wuv, gqh, gkh, tables, lp, comm=None):
    n = zm.shape[0]
    tm = _tile(lp, 704)
    per_seq = lp // tm
    width = MLA_HEADS * HEAD_SLAB
    scale = 1.0 / math.sqrt(D_QK)

    def body(dq_ref, dk_ref, dv_ref, zm_ref, qn_ref, cn_ref, gql_ref, gkvl_ref, wuq_ref, wuk_ref, wuv_ref,
             gqh_ref, gkh_ref, cos_ref, up_ref, dn_ref,
             dzm_ref, dqr_ref, dkr_ref, dgql_ref, dgkvl_ref, dgqh_ref, dgkh_ref):
        z = zm_ref[...]
        cq = z[:, :Q_RANK]
        ckv = z[:, Q_RANK:Q_RANK + KV_RANK]
        q_raw = _nt(qn_ref[...], wuq_ref[...])
        k_raw = _nt(cn_ref[...], wuk_ref[...])
        kr_slab = _k_rope_slab(z)
        cosr, sin_up, sin_dn = cos_ref[...], up_ref[...], dn_ref[...]
        dgq = jnp.zeros((1, HEAD_SLAB), F32)
        dgk = jnp.zeros((1, HEAD_SLAB), F32)
        dkrp = jnp.zeros((tm, HEAD_SLAB - D_NOPE), F32)
        for hd in range(MLA_HEADS):
            cols = slice(hd * HEAD_SLAB, (hd + 1) * HEAD_SLAB)
            xq = q_raw[:, cols]
            dxn = _rope_bwd(dq_ref[:, cols] * scale, cosr, sin_up, sin_dn)
            dxq, dg = _rms_bwd(xq, _rms_r(xq, D_QK), gqh_ref[...], dxn, D_QK)
            dgq = dgq + dg
            dqr_ref[:, cols] = dxq.astype(BF16)
            xk = k_raw[:, cols] + kr_slab
            dxn = _rope_bwd(dk_ref[:, cols], cosr, sin_up, sin_dn)
            dxk, dg = _rms_bwd(xk, _rms_r(xk, D_QK), gkh_ref[...], dxn, D_QK)
            dgk = dgk + dg
            dkr_ref[:, cols] = dxk.astype(BF16)
            dkrp = dkrp + dxk[:, D_NOPE:]
        dqn = _nn(dqr_ref[...], wuq_ref[...])
        dcn = _nn(dkr_ref[...], wuk_ref[...]) + _nn(dv_ref[...].astype(BF16), wuv_ref[...])
        dcq, dg1 = _rms_bwd(cq, _rms_r(cq), gql_ref[...], dqn)
        dckv, dg2 = _rms_bwd(ckv, _rms_r(ckv), gkvl_ref[...], dcn)
        dzm_ref[:, :Q_RANK] = dcq.astype(BF16)
        dzm_ref[:, Q_RANK:Q_RANK + KV_RANK] = dckv.astype(BF16)
        dzm_ref[:, Q_RANK + KV_RANK:] = dkrp.astype(BF16)
        first = pl.program_id(0) == 0
        _accumulate(dgql_ref, dg1, first)
        _accumulate(dgkvl_ref, dg2, first)
        _accumulate(dgqh_ref, dgq, first)
        _accumulate(dgkh_ref, dgk, first)

    tab = pl.BlockSpec((tm, HEAD_SLAB), lambda i: (i % per_seq, 0))
    return _call(
        body, "mla_prep_bwd", (n // tm,),
        [_row(tm, width), _row(tm, width), _row(tm, MLA_HEADS * D_V), _row(tm, MLA_IN),
         _row(tm, Q_RANK), _row(tm, KV_RANK), _fixed((1, Q_RANK)), _fixed((1, KV_RANK)),
         VMEM_WHOLE, VMEM_WHOLE, VMEM_WHOLE, _fixed((1, HEAD_SLAB)), _fixed((1, HEAD_SLAB)), tab, tab, tab],
        [_row(tm, MLA_IN), _row(tm, width), _row(tm, width), _fixed((1, Q_RANK)), _fixed((1, KV_RANK)),
         _fixed((1, HEAD_SLAB)), _fixed((1, HEAD_SLAB))],
        [jax.ShapeDtypeStruct((n, MLA_IN), BF16), jax.ShapeDtypeStruct((n, width), BF16),
         jax.ShapeDtypeStruct((n, width), BF16), jax.ShapeDtypeStruct((1, Q_RANK), F32),
         jax.ShapeDtypeStruct((1, KV_RANK), F32), jax.ShapeDtypeStruct((1, HEAD_SLAB), F32),
         jax.ShapeDtypeStruct((1, HEAD_SLAB), F32)],
        ("arbitrary",), (dq, dk, dv, zm, qn, cn, gql, gkvl, wuq, wuk, wuv, gqh, gkh, *tables), comm=comm)


def _attn_tile(lp):
    return _tile(lp, 704, CHUNK)


def _chunk_mask(i, j, t):
    qpos = i * t + lax.broadcasted_iota(jnp.int32, (t, t), 0)
    kpos = j * t + lax.broadcasted_iota(jnp.int32, (t, t), 1)
    same_or_earlier = jnp.right_shift(kpos, CHUNK_SHIFT) <= jnp.right_shift(qpos, CHUNK_SHIFT)
    return jnp.logical_and(same_or_earlier, kpos >= PAD)


def _masked_scores(s, i, j, t, diagonal):
    if diagonal:
        return jnp.where(_chunk_mask(i, j, t), s, NEG_INF)
    kpos = j * t + lax.broadcasted_iota(jnp.int32, (1, t), 1)
    return s + jnp.where(kpos < PAD, NEG_INF, 0.0)


def attn_fwd(q, k, v, nb, lp, comm=None):
    n = q.shape[0]
    t = _attn_tile(lp)
    nq = lp // t

    hp = ATTN_HEADS_PER_STEP

    def body(q_ref, k_ref, v_ref, o_ref, lse_ref):
        i = pl.program_id(2)
        qs = [q_ref[:, hh * HEAD_SLAB:(hh + 1) * HEAD_SLAB] for hh in range(hp)]

        def kv_step(j, carry, diagonal=False):
            off = pl.multiple_of(j * t, t)
            out = []
            for hh in range(hp):
                m, l, acc = carry[hh]
                kv = k_ref[pl.ds(off, t), hh * HEAD_SLAB:(hh + 1) * HEAD_SLAB]
                s = _masked_scores(_nt(qs[hh], kv), i, j, t, diagonal)
                m_new = jnp.maximum(m, jnp.max(s, axis=-1, keepdims=True))
                p = jnp.exp(s - m_new)
                alpha = jnp.exp(m - m_new)
                l = alpha * l + jnp.sum(p, axis=-1, keepdims=True)
                acc = alpha * acc + _nn(p.astype(BF16), v_ref[pl.ds(off, t), hh * D_V:(hh + 1) * D_V])
                out.append((m_new, l, acc))
            return tuple(out)

        init = tuple((jnp.full((t, 1), NEG_INF, F32), jnp.zeros((t, 1), F32), jnp.zeros((t, D_V), F32))
                     for _ in range(hp))
        done = kv_step(i, lax.fori_loop(0, i, kv_step, init), diagonal=True)
        for hh, (m, l, acc) in enumerate(done):
            o_ref[:, hh * D_V:(hh + 1) * D_V] = acc * (1.0 / l)
            lse_ref[hh] = jnp.broadcast_to(m + jnp.log(l), (t, LANES))

    return _call(
        body, "attn_fwd", (nb, MLA_HEADS // hp, nq),
        [pl.BlockSpec((t, hp * HEAD_SLAB), lambda b, h, i: (b * nq + i, h)),
         pl.BlockSpec((lp, hp * HEAD_SLAB), lambda b, h, i: (b, h)),
         pl.BlockSpec((lp, hp * D_V), lambda b, h, i: (b, h))],
        [pl.BlockSpec((t, hp * D_V), lambda b, h, i: (b * nq + i, h)),
         pl.BlockSpec((hp, t, LANES), lambda b, h, i: (h, b * nq + i, 0))],
        [jax.ShapeDtypeStruct((n, MLA_HEADS * D_V), F32), jax.ShapeDtypeStruct((MLA_HEADS, n, LANES), F32)],
        ("parallel", "parallel", "parallel"), (q, k, v), comm=comm)


def attn_bwd(q, k, v, o, do, lse, nb, lp, comm=None):
    n = q.shape[0]
    t = _attn_tile(lp)
    nq = lp // t

    def body(q_ref, k_ref, v_ref, o_ref, do_ref, lse_ref, dq_ref, dk_ref, dv_ref):
        dk_ref[...] = jnp.zeros_like(dk_ref)
        dv_ref[...] = jnp.zeros_like(dv_ref)

        def q_step(i, _):
            qoff = pl.multiple_of(i * t, t)
            qv = q_ref[pl.ds(qoff, t), :]
            dov = do_ref[pl.ds(qoff, t), :]
            delta = jnp.sum(o_ref[pl.ds(qoff, t), :] * dov, axis=-1, keepdims=True)
            lse_q = jnp.max(lse_ref[0, pl.ds(qoff, t), :], axis=-1, keepdims=True)
            do16 = dov.astype(BF16)

            def kv_step(j, dq_acc, diagonal=False):
                koff = pl.multiple_of(j * t, t)
                kv = k_ref[pl.ds(koff, t), :]
                s = _masked_scores(_nt(qv, kv), i, j, t, diagonal)
                p = jnp.exp(s - lse_q)
                dp = _nt(do16, v_ref[pl.ds(koff, t), :])
                ds16 = (p * (dp - delta)).astype(BF16)
                dv_ref[pl.ds(koff, t), :] += _tn(p.astype(BF16), do16)
                dk_ref[pl.ds(koff, t), :] += _tn(ds16, qv)
                return dq_acc + _nn(ds16, kv)

            earlier = lax.fori_loop(0, i, kv_step, jnp.zeros((t, HEAD_SLAB), F32))
            dq_ref[pl.ds(qoff, t), :] = kv_step(i, earlier, diagonal=True)
            return 0

        lax.fori_loop(0, nq, q_step, 0)

    wide = pl.BlockSpec((lp, HEAD_SLAB), lambda b, h: (b, h))
    thin = pl.BlockSpec((lp, D_V), lambda b, h: (b, h))
    width = MLA_HEADS * HEAD_SLAB
    return _call(
        body, "attn_bwd", (nb, MLA_HEADS),
        [wide, wide, thin, thin, thin, pl.BlockSpec((1, lp, LANES), lambda b, h: (h, b, 0))],
        [wide, wide, thin],
        [jax.ShapeDtypeStruct((n, width), F32), jax.ShapeDtypeStruct((n, width), F32),
         jax.ShapeDtypeStruct((n, MLA_HEADS * D_V), F32)],
        ("parallel", "parallel"), (q, k, v, o, do, lse), comm=comm)


def _seq_rows(nb, lp, width):
    rows = lax.broadcasted_iota(jnp.int32, (lp, width), 0)
    return jnp.concatenate([rows] * nb, axis=0) if nb > 1 else rows


def _lru_gates(u, w_ref, cb, wa, wx, ba, bx, lam):
    xc = (cb + w_ref[pl.ds(3, 1), :] * u + w_ref[pl.ds(2, 1), :] * pltpu.roll(u, 1, axis=0)
          + w_ref[pl.ds(1, 1), :] * pltpu.roll(u, 2, axis=0) + w_ref[pl.ds(0, 1), :] * pltpu.roll(u, 3, axis=0))
    xc16 = xc.astype(BF16)
    ra = _sig_tanh(_nn(xc16, wa) + ba)
    ia = _sig_tanh(_nn(xc16, wx) + bx)
    sp = _softplus(-lam)
    log_a = -C_RGLRU * ra * sp
    a = jnp.exp(log_a)
    x2 = 2.0 * log_a
    mult = jnp.sqrt(jnp.where(x2 > -1e-2, -x2 * (1.0 + x2 * (0.5 + x2 * (1.0 / 6.0))), 1.0 - a * a))
    return xc, xc16, ra, ia, sp, a, mult


def _scan_block_rows(width):
    return lax.broadcasted_iota(jnp.int32, (8, width), 0)


def lru_fwd(zl, conv_w, conv_b, wa, wx, ba, bx, lam, nb, lp, comm=None):
    n = zl.shape[0]
    w = LRU_TILE
    nt = LRU_WIDTH // w
    nblk = lp // 8

    def body(u_ref, gt_ref, cw_ref, cb_ref, wa_ref, wx_ref, ba_ref, bx_ref, lam_ref, y_ref, h_ref, a_s, b_s):
        u = u_ref[...]
        xc, _, _, ia, _, a, mult = _lru_gates(u, cw_ref, cb_ref[...], wa_ref[...], wx_ref[...],
                                              ba_ref[...], bx_ref[...], lam_ref[...])
        row = _seq_rows(nb, lp, w)
        mult = jnp.where(row == PAD, 1.0, mult)
        a_s[...] = a
        b_s[...] = jnp.where(row < PAD, 0.0, mult * (ia * xc))
        r8 = _scan_block_rows(w)

        def blk(i, carry):
            out = []
            for s_id in range(nb):
                off = pl.multiple_of(s_id * lp + i * 8, 8)
                av = a_s[pl.ds(off, 8), :]
                bv = b_s[pl.ds(off, 8), :]
                for sh in (1, 2, 4):
                    keep = r8 >= sh
                    bv = jnp.where(keep, av * pltpu.roll(bv, sh, axis=0) + bv, bv)
                    av = jnp.where(keep, av * pltpu.roll(av, sh, axis=0), av)
                hv = bv + av * carry[s_id]
                h_ref[pl.ds(off, 8), :] = hv
                out.append(jnp.sum(jnp.where(r8 == 7, hv, 0.0), axis=0, keepdims=True))
            return tuple(out)

        lax.fori_loop(0, nblk, blk, tuple(jnp.zeros((1, w), F32) for _ in range(nb)))
        gelu, _ = _gelu_and_grad(gt_ref[...])
        y_ref[...] = h_ref[...] * gelu

    col = lambda c: (0, c)
    return _call(
        body, "lru_fwd", (nt,),
        [pl.BlockSpec((n, w), col), pl.BlockSpec((n, w), lambda c: (0, nt + c)),
         pl.BlockSpec((CONV_W, w), col), pl.BlockSpec((1, w), col),
         pl.BlockSpec((w, w), lambda c: (c, c)), pl.BlockSpec((w, w), lambda c: (c, c)),
         pl.BlockSpec((1, w), col), pl.BlockSpec((1, w), col), pl.BlockSpec((1, w), col)],
        [pl.BlockSpec((n, w), col), pl.BlockSpec((n, w), col)],
        [jax.ShapeDtypeStruct((n, LRU_WIDTH), F32), jax.ShapeDtypeStruct((n, LRU_WIDTH), F32)],
        ("parallel",), (zl, zl, conv_w, conv_b, wa, wx, ba, bx, lam),
        scratch=[pltpu.VMEM((n, w), F32), pltpu.VMEM((n, w), F32)], comm=comm)


def lru_bwd(zl, hs, dy, conv_w, conv_b, wa, wx, ba, bx, lam, nb, lp, comm=None):
    n = zl.shape[0]
    w = LRU_TILE
    nt = LRU_WIDTH // w
    nblk = lp // 8

    def body(u_ref, gt_ref, h_ref, dy_ref, cw_ref, cb_ref, wa_ref, wx_ref, ba_ref, bx_ref, lam_ref,
             du_ref, dgt_ref, dcw_ref, dcb_ref, dba_ref, dbx_ref, dlam_ref, dwa_ref, dwx_ref,
             c_s, d_s, g_s, dwa_s, dwx_s):
        u = u_ref[...]
        lam = lam_ref[...]
        xc, xc16, ra, ia, sp, a, mult = _lru_gates(u, cw_ref, cb_ref[...], wa_ref[...], wx_ref[...],
                                                   ba_ref[...], bx_ref[...], lam)
        row = lax.broadcasted_iota(jnp.int32, (lp, w), 0)
        hv = h_ref[...]
        dyv = dy_ref[...]
        gelu, dgelu = _gelu_and_grad(gt_ref[...])
        dgt_ref[...] = jnp.where(row >= PAD, dyv * hv * dgelu, 0.0).astype(BF16)
        c_s[...] = pltpu.roll(a, lp - 1, axis=0)
        d_s[...] = dyv * gelu
        r8 = _scan_block_rows(w)

        def blk(ii, carry):
            off = pl.multiple_of((nblk - 1 - ii) * 8, 8)
            cv = c_s[pl.ds(off, 8), :]
            dv = d_s[pl.ds(off, 8), :]
            for sh in (1, 2, 4):
                keep = r8 < 8 - sh
                dv = jnp.where(keep, cv * pltpu.roll(dv, 8 - sh, axis=0) + dv, dv)
                cv = jnp.where(keep, cv * pltpu.roll(cv, 8 - sh, axis=0), cv)
            gv = dv + cv * carry
            g_s[pl.ds(off, 8), :] = gv
            return jnp.sum(jnp.where(r8 == 0, gv, 0.0), axis=0, keepdims=True)

        lax.fori_loop(0, nblk, blk, jnp.zeros((1, w), F32))
        gv = g_s[...]
        first_row = row == PAD
        db = jnp.where(row >= PAD, gv, 0.0)
        da = jnp.where(row > PAD, gv * pltpu.roll(hv, 1, axis=0), 0.0)
        mult_eff = jnp.where(first_row, 1.0, mult)
        dmult = jnp.where(first_row, 0.0, db * (ia * xc))
        dia = db * mult_eff * xc
        dxc = db * mult_eff * ia
        dla = da * a - dmult * (a * a) / mult
        dra = dla * (-C_RGLRU * sp)
        dsp = jnp.sum(dla * (-C_RGLRU * ra), axis=0, keepdims=True)
        dpa = dra * ra * (1.0 - ra)
        dpx = dia * ia * (1.0 - ia)
        dpa16 = dpa.astype(BF16)
        dpx16 = dpx.astype(BF16)
        dxc = dxc + _nt(dpa16, wa_ref[...]) + _nt(dpx16, wx_ref[...])
        du = cw_ref[pl.ds(CONV_W - 1, 1), :] * dxc
        dcw = [jnp.sum(dxc * u, axis=0, keepdims=True)]
        for tap in range(1, CONV_W):
            dcw.insert(0, jnp.sum(dxc * pltpu.roll(u, tap, axis=0), axis=0, keepdims=True))
            du = du + cw_ref[pl.ds(CONV_W - 1 - tap, 1), :] * pltpu.roll(dxc, lp - tap, axis=0)
        du_ref[...] = jnp.where(row >= PAD, du, 0.0).astype(BF16)
        first = pl.program_id(1) == 0
        _accumulate(dlam_ref, -_sig(-lam) * dsp, first)
        _accumulate(dba_ref, jnp.sum(dpa, axis=0, keepdims=True), first)
        _accumulate(dbx_ref, jnp.sum(dpx, axis=0, keepdims=True), first)
        _accumulate(dcb_ref, jnp.sum(dxc, axis=0, keepdims=True), first)
        _accumulate(dcw_ref, jnp.concatenate(dcw, axis=0), first)
        _accumulate(dwa_s, _tn(xc16, dpa16), first)
        _accumulate(dwx_s, _tn(xc16, dpx16), first)

        @pl.when(pl.program_id(1) == nb - 1)
        def _():
            for j in range(w // LRU_BLOCK):
                blk_rows = slice(j * LRU_BLOCK, (j + 1) * LRU_BLOCK)
                dwa_ref[0, blk_rows, :] = dwa_s[blk_rows, blk_rows]
                dwx_ref[0, blk_rows, :] = dwx_s[blk_rows, blk_rows]

    col = lambda c, b: (0, c)
    vec = pl.BlockSpec((1, w), col)
    mat = pl.BlockSpec((w, w), lambda c, b: (c, c))
    big = pl.BlockSpec((lp, w), lambda c, b: (b, c))
    dmat = pl.BlockSpec((1, w, LRU_BLOCK), lambda c, b: (c, 0, 0))
    return _call(
        body, "lru_bwd", (nt, nb),
        [big, pl.BlockSpec((lp, w), lambda c, b: (b, nt + c)), big, big,
         pl.BlockSpec((CONV_W, w), col), vec, mat, mat, vec, vec, vec],
        [big, big, pl.BlockSpec((CONV_W, w), col), vec, vec, vec, vec, dmat, dmat],
        [jax.ShapeDtypeStruct((n, LRU_WIDTH), BF16), jax.ShapeDtypeStruct((n, LRU_WIDTH), BF16),
         jax.ShapeDtypeStruct((CONV_W, LRU_WIDTH), F32), jax.ShapeDtypeStruct((1, LRU_WIDTH), F32),
         jax.ShapeDtypeStruct((1, LRU_WIDTH), F32), jax.ShapeDtypeStruct((1, LRU_WIDTH), F32),
         jax.ShapeDtypeStruct((1, LRU_WIDTH), F32), jax.ShapeDtypeStruct((nt, w, LRU_BLOCK), F32),
         jax.ShapeDtypeStruct((nt, w, LRU_BLOCK), F32)],
        ("parallel", "arbitrary"), (zl, zl, hs, dy, conv_w, conv_b, wa, wx, ba, bx, lam),
        scratch=[pltpu.VMEM((lp, w), F32), pltpu.VMEM((lp, w), F32), pltpu.VMEM((lp, w), F32),
                 pltpu.VMEM((w, w), F32), pltpu.VMEM((w, w), F32)], comm=comm)


def outproj_fwd(h, ya, yl, gao, glo, wout):
    n, d = h.shape
    half = ya.shape[1]
    tm = _tile(n, 704)

    def body(h_ref, ya_ref, yl_ref, gao_ref, glo_ref, w_ref, ho_ref, yn_ref):
        xa = ya_ref[...]
        xl = yl_ref[...]
        na = (xa * _rms_r(xa) * gao_ref[...]).astype(BF16)
        nl = (xl * _rms_r(xl) * glo_ref[...]).astype(BF16)
        yn_ref[:, :half] = na
        yn_ref[:, half:] = nl
        ho_ref[...] = h_ref[...] + _nn(na, w_ref[:half, :]) + _nn(nl, w_ref[half:, :])

    return pl.pallas_call(
        body, name="outproj_fwd", grid=(n // tm,),
        in_specs=[_row(tm, d), _row(tm, half), _row(tm, half), _fixed((1, half)), _fixed((1, half)), VMEM_WHOLE],
        out_specs=[_row(tm, d), _row(tm, 2 * half)],
        out_shape=[jax.ShapeDtypeStruct((n, d), F32), jax.ShapeDtypeStruct((n, 2 * half), BF16)],
        compiler_params=_params(("parallel",)),
    )(h, ya, yl, gao, glo, wout)


def outproj_bwd(dh, ya, yl, gao, glo, wout):
    n, d = dh.shape
    half = ya.shape[1]
    tm = _tile(n, 704)

    def body(dh_ref, ya_ref, yl_ref, gao_ref, glo_ref, w_ref, dya_ref, dyl_ref, dgao_ref, dglo_ref):
        d16 = dh_ref[...].astype(BF16)
        xa = ya_ref[...]
        xl = yl_ref[...]
        dxa, dga = _rms_bwd(xa, _rms_r(xa), gao_ref[...], _nt(d16, w_ref[:half, :]))
        dxl, dgl = _rms_bwd(xl, _rms_r(xl), glo_ref[...], _nt(d16, w_ref[half:, :]))
        dya_ref[...] = dxa
        dyl_ref[...] = dxl
        first = pl.program_id(0) == 0
        _accumulate(dgao_ref, dga, first)
        _accumulate(dglo_ref, dgl, first)

    return pl.pallas_call(
        body, name="outproj_bwd", grid=(n // tm,),
        in_specs=[_row(tm, d), _row(tm, half), _row(tm, half), _fixed((1, half)), _fixed((1, half)), VMEM_WHOLE],
        out_specs=[_row(tm, half), _row(tm, half), _fixed((1, half)), _fixed((1, half))],
        out_shape=[jax.ShapeDtypeStruct((n, half), F32), jax.ShapeDtypeStruct((n, half), F32),
                   jax.ShapeDtypeStruct((1, half), F32), jax.ShapeDtypeStruct((1, half), F32)],
        compiler_params=_params(("arbitrary",)),
    )(dh, ya, yl, gao, glo, wout)


def _loss_and_grad(x, gv, tgt, first_row):
    tm, d = x.shape
    r = _rms_r(x)
    row = first_row + lax.broadcasted_iota(jnp.int32, (tm, d), 0)
    diff = jnp.where(row >= FIRST_FRAME, x * r * gv - tgt, 0.0)
    part = 0.5 * jnp.sum(jnp.sum(diff * diff, axis=-1, keepdims=True) * (1.0 / d), axis=0, keepdims=True)
    dx, dg = _rms_bwd(x, r, gv, diff * (1.0 / d))
    return dx, part, dg


def assemble_cols(g, name):
    _, k, ns = g.shape

    def body(g_ref, o_ref):
        for j in range(N_DEV):
            o_ref[:, j * ns:(j + 1) * ns] = g_ref[j]

    return pl.pallas_call(body, name=name, out_shape=jax.ShapeDtypeStruct((k, N_DEV * ns), g.dtype),
                          compiler_params=_params(None))(g)


def split_cols(x, name):
    k, cols = x.shape
    ns = cols // N_DEV

    def body(x_ref, o_ref):
        for j in range(N_DEV):
            o_ref[j] = x_ref[:, j * ns:(j + 1) * ns]

    return pl.pallas_call(body, name=name, out_shape=jax.ShapeDtypeStruct((N_DEV, k, ns), x.dtype),
                          compiler_params=_params(None))(x)


def _slab_rows(w, per_head):
    k = w.shape[1]
    w = w.reshape(MLA_HEADS, per_head, k)
    return jnp.pad(w, ((0, 0), (0, HEAD_SLAB - per_head), (0, 0))).reshape(MLA_HEADS * HEAD_SLAB, k)


def _unslab_rows(w, per_head):
    k = w.shape[1]
    return w.reshape(MLA_HEADS, HEAD_SLAB, k)[:, :per_head].reshape(MLA_HEADS * per_head, k)


def meta_grad(dh0, nb, lp):
    d = dh0.shape[1]
    ns = d // N_DEV
    per_seq = lp // N_META

    def body(x_ref, o_ref):
        x = x_ref[...]
        for j in range(N_DEV):
            _accumulate(o_ref.at[j], x[:, j * ns:(j + 1) * ns], pl.program_id(0) == 0)

    return pl.pallas_call(
        body, name="meta_grad", grid=(nb,),
        in_specs=[pl.BlockSpec((N_META, d), lambda b: (b * per_seq + PAD // N_META, 0))],
        out_specs=pl.BlockSpec((N_DEV, N_META, ns), lambda b: (0, 0, 0)),
        out_shape=jax.ShapeDtypeStruct((N_DEV, N_META, ns), F32),
        compiler_params=_params(("arbitrary",)))(dh0)


VECTORS = [("ffn1_norm", 1024), ("mix_norm", 1024), ("q_latent_norm", 384), ("kv_latent_norm", 256),
           ("q_head_norm", 192), ("k_head_norm", 192), ("conv_b", 512), ("gate_a_b", 512), ("gate_x_b", 512),
           ("lru_lambda", 512), ("attn_out_norm", 512), ("lru_out_norm", 512), ("ffn2_norm", 1024),
           ("final_norm", 1024)]
VEC_ROWS = 16
LOSS_ROW = len(VECTORS)
GATES = ["gate_a_w", "gate_x_w"]


def pack_vectors(grads, loss):
    def body(*refs):
        o_ref = refs[-1]
        o_ref[...] = jnp.zeros_like(o_ref)
        for t, (ref, (_, cnt)) in enumerate(zip(refs[:-2], VECTORS)):
            o_ref[t:t + 1, :cnt] = ref[:, :cnt]
        o_ref[LOSS_ROW:LOSS_ROW + 1, :LANES] = refs[-2][...]

    return pl.pallas_call(body, name="pack_vectors", out_shape=jax.ShapeDtypeStruct((VEC_ROWS, D_MODEL), F32),
                          compiler_params=_params(None))(*[grads[name] for name, _ in VECTORS], loss)


def _adamw_update(w, g, m, v):
    c1 = 1.0 / (1.0 - ADAM_B1 ** ADAM_STEP)
    c2 = 1.0 / (1.0 - ADAM_B2 ** ADAM_STEP)
    mn = ADAM_B1 * m + (1.0 - ADAM_B1) * g
    vn = ADAM_B2 * v + (1.0 - ADAM_B2) * (g * g)
    delta = -ADAM_LR * ((mn * c1) / (jnp.sqrt(vn * c2) + ADAM_EPS) + ADAM_WD * w)
    return delta, mn, vn


def _sum_slots(ref, index=()):
    acc = ref[(0,) + index].astype(F32)
    for s in range(1, N_DEV):
        acc = acc + ref[(s,) + index].astype(F32)
    return acc


def adamw_sharded(r, w, m, v, name):
    rows, cols = w.shape
    tr = _tile(rows, 256, 16) if rows % 16 == 0 else rows

    def body(r_ref, w_ref, m_ref, v_ref, g_ref, d_ref, mo_ref, vo_ref):
        g = _sum_slots(r_ref)
        g_ref[...] = g
        d_ref[...], mo_ref[...], vo_ref[...] = _adamw_update(w_ref[...], g, m_ref[...], v_ref[...])

    spec = pl.BlockSpec((tr, cols), lambda i: (i, 0))
    shape = jax.ShapeDtypeStruct((rows, cols), F32)
    return pl.pallas_call(
        body, name=name, grid=(rows // tr,),
        in_specs=[pl.BlockSpec((N_DEV, tr, cols), lambda i: (0, i, 0))] + [spec] * 3,
        out_specs=[spec] * 4, out_shape=[shape] * 4,
        compiler_params=_params(("parallel",)),
    )(r, w, m, v)


def adamw_small(r_vec, r_gates, w, m, v):
    nt = len(VECTORS) + len(GATES)

    def body(*refs):
        rv_ref = refs[0]
        rg_refs = refs[1:1 + len(GATES)]
        base = 1 + len(GATES)
        w_refs, m_refs, v_refs = (refs[base + i * nt:base + (i + 1) * nt] for i in range(3))
        outs = refs[base + 3 * nt:]
        g_o, d_o, m_o, v_o = (outs[i * nt:(i + 1) * nt] for i in range(4))
        outs[4 * nt][...] = _sum_slots(rv_ref, (slice(LOSS_ROW, LOSS_ROW + 1), slice(0, LANES)))
        for t in range(nt):
            if t < len(VECTORS):
                cnt = VECTORS[t][1]
                g = _sum_slots(rv_ref, (slice(t, t + 1), slice(0, cnt)))
            else:
                g = _sum_slots(rg_refs[t - len(VECTORS)])
            g_o[t][...] = g
            d_o[t][...], m_o[t][...], v_o[t][...] = _adamw_update(w_refs[t][...], g, m_refs[t][...], v_refs[t][...])

    shapes = [jax.ShapeDtypeStruct(a.shape, F32) for a in w]
    res = pl.pallas_call(body, name="adamw_small", out_shape=shapes * 4 + [jax.ShapeDtypeStruct((1, LANES), F32)],
                         compiler_params=_params(None))(r_vec, *r_gates, *w, *m, *v)
    return [res[i * nt:(i + 1) * nt] for i in range(4)], res[4 * nt]


def _block_diag(w):
    nb, n, _ = w.shape
    eye = jnp.eye(nb, dtype=w.dtype)
    return (eye[:, None, :, None] * w[:, :, None, :]).reshape(nb * n, nb * n)


def _two_d(a):
    if a.ndim == 3:
        return a.reshape(a.shape[1], a.shape[2])
    if a.ndim == 4:
        return a.reshape(a.shape[1] * a.shape[2], a.shape[3])
    return a


_WEIGHT_NAMES = ['meta_tokens', 'ffn1_norm', 'ffn1_w_gate', 'ffn1_w_up', 'ffn1_w_down', 'mix_norm', 'w_in',
                 'q_latent_norm', 'w_uq', 'kv_latent_norm', 'w_uk', 'w_uv', 'q_head_norm', 'k_head_norm', 'conv_w',
                 'conv_b', 'gate_a_w', 'gate_a_b', 'gate_x_w', 'gate_x_b', 'lru_lambda', 'attn_out_norm',
                 'lru_out_norm', 'w_out', 'ffn2_norm', 'ffn2_w_gate', 'ffn2_w_up', 'ffn2_w_down', 'final_norm']


COLUMN_SHARDED = ("ffn1_w_gate", "ffn1_w_up", "ffn2_w_gate", "ffn2_w_up", "w_in", "w_uq", "w_uk", "w_uv")


def train_step(x, tgt, w, m, v):
    nb, seq, d = x.shape
    lp = PAD + N_META + seq
    n = nb * lp

    def local(a, name):
        a = _two_d(a)
        return a.T if name in COLUMN_SHARDED else a

    sh = {name: local(w[name], name) for name in _WEIGHT_NAMES}
    m2 = {name: local(m[name], name) for name in _WEIGHT_NAMES}
    v2 = {name: local(v[name], name) for name in _WEIGHT_NAMES}

    def b16(name):
        return sh[name].astype(BF16)

    out = {}

    def update(name, landed):
        out[name] = adamw_sharded(landed, sh[name], m2[name], v2[name], "adamw_" + name)

    g_meta, g_conv, g_wg1, g_wu1 = exchange(
        [sh["meta_tokens"], sh["conv_w"], b16("ffn1_w_gate"), b16("ffn1_w_up")], ["gather"] * 4, "gather_ffn1")
    wg1, wu1 = g_wg1.reshape(D_FF, d), g_wu1.reshape(D_FF, d)
    meta = assemble_cols(g_meta, "assemble_meta")
    conv_w = assemble_cols(g_conv, "assemble_conv")

    front = jnp.concatenate([jnp.zeros((PAD, d), F32), meta], axis=0)
    h0 = jnp.concatenate([jnp.broadcast_to(front[None], (nb, FIRST_FRAME, d)), x], axis=1).reshape(n, d)
    tgt_p = jnp.concatenate([jnp.zeros((nb, FIRST_FRAME, d), F32), tgt], axis=1).reshape(n, d)
    tables = _rope_tables(lp)
    zero_tail = jnp.zeros((1, HEAD_SLAB - D_QK), F32)
    gqh = jnp.concatenate([sh["q_head_norm"], zero_tail], axis=1)
    gkh = jnp.concatenate([sh["k_head_norm"], zero_tail], axis=1)
    wa = _block_diag(w["gate_a_w"][0]).astype(BF16)
    wx = _block_diag(w["gate_x_w"][0]).astype(BF16)

    (u1, a1, b1, s1), (g_wd1, g_in) = ffn_up(h0, sh["ffn1_norm"], wg1, wu1, "ffn1_up",
                                             comm=([b16("ffn1_w_down"), b16("w_in")], ["gather"] * 2))
    wd1 = g_wd1.reshape(D_FF, d)
    mla_rows = MLA_IN - D_ROPE
    w_in = g_in.reshape(mla_rows + 2 * LRU_WIDTH, d)
    wm = jnp.concatenate([w_in[:mla_rows], jnp.zeros((D_ROPE, d), BF16)], axis=0)
    wl = w_in[mla_rows:]
    (h1, u2, zm, zl), (g_uq, g_uk, g_uv, g_out) = ffn_down_inproj(
        h0, s1, wd1, sh["mix_norm"], wm, wl, "ffn1_down_inproj",
        comm=([b16("w_uq"), b16("w_uk"), b16("w_uv"), b16("w_out")], ["gather"] * 4))
    wuq = _slab_rows(g_uq.reshape(MLA_HEADS * D_QK, Q_RANK), D_QK)
    wuk = _slab_rows(g_uk.reshape(MLA_HEADS * D_NOPE, KV_RANK), D_NOPE)
    wuv = g_uv.reshape(MLA_HEADS * D_V, KV_RANK)
    w_out = g_out.reshape(d, d)

    q, k, vv, qn, cn = mla_prep_fwd(zm, sh["q_latent_norm"], sh["kv_latent_norm"], wuq, wuk, wuv, gqh, gkh, tables, lp)
    (y_mla, lse), (g_wu2, g_wd2) = attn_fwd(
        q, k, vv, nb, lp, comm=([b16("ffn2_w_up"), b16("ffn2_w_down")], ["gather"] * 2))
    (y_lru, hs), (g_wg2,) = lru_fwd(zl, conv_w, sh["conv_b"], wa, wx, sh["gate_a_b"], sh["gate_x_b"], sh["lru_lambda"],
                                    nb, lp, comm=([b16("ffn2_w_gate")], ["gather"]))
    wg2, wu2, wd2 = (g.reshape(D_FF, d) for g in (g_wg2, g_wu2, g_wd2))
    h2, yn = outproj_fwd(h1, y_mla, y_lru, sh["attn_out_norm"], sh["lru_out_norm"], w_out)
    dh3, u3, a3, b3, loss, g_final = ffn_fwd_loss(h2, sh["ffn2_norm"], wg2, wu2, wd2, sh["final_norm"], tgt_p, lp,
                                                  "ffn2_fwd_loss")

    vec = {"final_norm": g_final}
    (dh2, da3, db3, sh3, vec["ffn2_norm"]), _ = ffn_bwd_act(dh3, h2, sh["ffn2_norm"], a3, b3, wg2, wu2, wd2, "ffn2_bwd")
    ff_shards = (N_DEV, D_FF // N_DEV, d)
    dwg2 = tn_matmul(da3, u3, "ffn2_dwg", "bf16").reshape(ff_shards)
    dwu2 = tn_matmul(db3, u3, "ffn2_dwu", "bf16").reshape(ff_shards)
    dwd2 = tn_matmul(sh3, dh3, "ffn2_dwd", "bf16").reshape(ff_shards)

    dy_mla, dy_lru, vec["attn_out_norm"], vec["lru_out_norm"] = outproj_bwd(
        dh2, y_mla, y_lru, sh["attn_out_norm"], sh["lru_out_norm"], w_out)
    dw_out = tn_matmul(yn, dh2, "dw_out", "bf16").reshape(N_DEV, d // N_DEV, d)
    (du, dgate, dconv, vec["conv_b"], vec["gate_a_b"], vec["gate_x_b"], vec["lru_lambda"], dga, dgx), landed = lru_bwd(
        zl, hs, dy_lru, conv_w, sh["conv_b"], wa, wx, sh["gate_a_b"], sh["gate_x_b"], sh["lru_lambda"], nb, lp,
        comm=([dwg2, dw_out], ["scatter"] * 2))
    update("ffn2_w_gate", landed[0])
    update("w_out", landed[1])

    (dq, dk, dv), (r_wu2,) = attn_bwd(q, k, vv, y_mla, dy_mla, lse, nb, lp, comm=([dwu2], ["scatter"]))
    update("ffn2_w_up", r_wu2)

    (dzm, dqr, dkr, vec["q_latent_norm"], vec["kv_latent_norm"], vec["q_head_norm"], vec["k_head_norm"]), (r_wd2,) = (
        mla_prep_bwd(dq, dk, dv, zm, qn, cn, sh["q_latent_norm"], sh["kv_latent_norm"], wuq, wuk, wuv, gqh, gkh,
                     tables, lp, comm=([dwd2], ["scatter"])))
    update("ffn2_w_down", r_wd2)
    dwuq = _unslab_rows(tn_matmul(dqr, qn, "dw_uq", "bf16"), D_QK).reshape(N_DEV, -1, Q_RANK)
    dwuk = _unslab_rows(tn_matmul(dkr, cn, "dw_uk", "bf16"), D_NOPE).reshape(N_DEV, -1, KV_RANK)
    dwuv = tn_matmul(dv, cn, "dw_uv", "bf16").reshape(N_DEV, -1, KV_RANK)
    (dh1, vec["mix_norm"]), landed = inproj_bwd(dzm, du, dgate, dh2, h1, sh["mix_norm"], wm, wl,
                                                comm=([dwuq, dwuk, dwuv], ["scatter"] * 3))
    for name, r in zip(("w_uq", "w_uk", "w_uv"), landed):
        update(name, r)
    dw_in = jnp.concatenate([tn_matmul(dzm, u2, "dw_in_mla", "bf16")[:mla_rows], tn_matmul(du, u2, "dw_in_u", "bf16"),
                             tn_matmul(dgate, u2, "dw_in_gate", "bf16")], axis=0).reshape(N_DEV, -1, d)

    dwd1 = tn_matmul(s1, dh1, "ffn1_dwd", "bf16").reshape(ff_shards)
    (dh0, da1, db1, vec["ffn1_norm"]), landed = ffn_bwd_act(
        dh1, h0, sh["ffn1_norm"], a1, b1, wg1, wu1, wd1, "ffn1_bwd", emit_sh=False,
        comm=([dw_in, split_cols(dconv, "split_conv"), dwd1], ["scatter"] * 3))
    for name, r in zip(("w_in", "conv_w", "ffn1_w_down"), landed):
        update(name, r)

    dwg1 = tn_matmul(da1, u1, "ffn1_dwg", "bf16").reshape(ff_shards)
    dwu1, (r_wg1,) = tn_matmul(db1, u1, "ffn1_dwu", "bf16", comm=([dwg1], ["scatter"]))
    dmeta = meta_grad(dh0, nb, lp)
    gates = [dga.reshape(LRU_WIDTH, LRU_BLOCK), dgx.reshape(LRU_WIDTH, LRU_BLOCK)]
    r_vec, r_ga, r_gx, r_meta, r_wu1 = exchange(
        [pack_vectors(vec, loss)] + gates + [dmeta, dwu1.reshape(ff_shards)], ["gather"] * 3 + ["scatter"] * 2,
        "exchange_last")
    update("ffn1_w_gate", r_wg1)
    update("ffn1_w_up", r_wu1)
    update("meta_tokens", r_meta)

    small = [name for name, _ in VECTORS] + GATES
    res, total_loss = adamw_small(r_vec, [r_ga, r_gx], [sh[nm] for nm in small], [m2[nm] for nm in small],
                                  [v2[nm] for nm in small])
    for i, name in enumerate(small):
        out[name] = [res[j][i] for j in range(4)]

    grad_x = dh0.reshape(nb, lp, d)[:, FIRST_FRAME:]
    loss = total_loss[0, 0]

    def as_given(a, name):
        return (a.T if name in COLUMN_SHARDED else a).reshape(w[name].shape)

    cols = [[as_given(out[name][j], name) for name in _WEIGHT_NAMES] for j in range(4)]
    return (loss, grad_x, *cols[0], *cols[1], *cols[2], *cols[3])


def kernel(x, meta_tokens, ffn1_norm, ffn1_w_gate, ffn1_w_up, ffn1_w_down, mix_norm, w_in, q_latent_norm, w_uq, kv_latent_norm, w_uk, w_uv, q_head_norm, k_head_norm, conv_w, conv_b, gate_a_w, gate_a_b, gate_x_w, gate_x_b, lru_lambda, attn_out_norm, lru_out_norm, w_out, ffn2_norm, ffn2_w_gate, ffn2_w_up, ffn2_w_down, final_norm, loss_target, m_meta_tokens, m_ffn1_norm, m_ffn1_w_gate, m_ffn1_w_up, m_ffn1_w_down, m_mix_norm, m_w_in, m_q_latent_norm, m_w_uq, m_kv_latent_norm, m_w_uk, m_w_uv, m_q_head_norm, m_k_head_norm, m_conv_w, m_conv_b, m_gate_a_w, m_gate_a_b, m_gate_x_w, m_gate_x_b, m_lru_lambda, m_attn_out_norm, m_lru_out_norm, m_w_out, m_ffn2_norm, m_ffn2_w_gate, m_ffn2_w_up, m_ffn2_w_down, m_final_norm, v_meta_tokens, v_ffn1_norm, v_ffn1_w_gate, v_ffn1_w_up, v_ffn1_w_down, v_mix_norm, v_w_in, v_q_latent_norm, v_w_uq, v_kv_latent_norm, v_w_uk, v_w_uv, v_q_head_norm, v_k_head_norm, v_conv_w, v_conv_b, v_gate_a_w, v_gate_a_b, v_gate_x_w, v_gate_x_b, v_lru_lambda, v_attn_out_norm, v_lru_out_norm, v_w_out, v_ffn2_norm, v_ffn2_w_gate, v_ffn2_w_up, v_ffn2_w_down, v_final_norm):
    args = locals()
    w = {name: args[name] for name in _WEIGHT_NAMES}
    m = {name: args["m_" + name] for name in _WEIGHT_NAMES}
    v = {name: args["v_" + name] for name in _WEIGHT_NAMES}
    return train_step(x, loss_target, w, m, v)
```
